```python
import jax, jax.numpy as jnp
from jax import lax
import numpy as np

D_MODEL = 1024
BATCH = 8
SEQ = 4096
DEPTH = 1

G_GROUPS = 8
G_WIDTH = 512
G_HEAD = G_WIDTH // G_GROUPS
CHUNK = 128
R_WIDTH = 512
R_HEAD = 64
R_HEADS = R_WIDTH // R_HEAD
DECAY_LORA = 32
AAA_LORA = 32
GATE_LORA = 96
D_FF = 4 * D_MODEL
ALPHA = (2.0 * DEPTH) ** 0.25
BETA = (8.0 * DEPTH) ** -0.25
LN_EPS = 1e-5
GN_EPS = 64e-5
R_SHIFT_COLS = 3 * R_WIDTH + DECAY_LORA + AAA_LORA + GATE_LORA
IN_COLS = 2 * G_WIDTH + R_SHIFT_COLS + 2 * D_MODEL

kernel_name = 'hybrid_gmlp_rwkv7_deepnorm_adaln'


def _layer_norm(x, g, b, eps):
    xf = x.astype(jnp.float32)
    mu = xf.mean(-1, keepdims=True)
    var = jnp.square(xf - mu).mean(-1, keepdims=True)
    return (xf - mu) * lax.rsqrt(var + eps) * g + b


def _gmlp_branch(z, g_ln_v, b_ln_v, w_spatial, b_spatial):
    B, S, _ = z.shape
    z = jax.nn.gelu(z)
    u, v = jnp.split(z, 2, axis=-1)
    v = _layer_norm(v, g_ln_v, b_ln_v, LN_EPS)
    v = v.reshape(B, S // CHUNK, CHUNK, G_GROUPS, G_HEAD)
    mask = jnp.tril(jnp.ones((CHUNK, CHUNK), dtype=bool))
    ws = jnp.where(mask[None], w_spatial, 0.0)
    s = jnp.einsum('gts,bcsgd->bctgd', ws, v) + b_spatial.T[:, :, None]
    return u * s.reshape(B, S, G_WIDTH)


def _rwkv7_branch(z, mu_shift, w0, w_decay_up, a0, w_aaa_up, w_gate_up,
                  k_k, k_a, r_k, gn_gain, gn_bias):
    B, S, _ = z.shape
    prev = jnp.pad(z, ((0, 0), (1, 0), (0, 0)))[:, :-1]
    z = z + (prev - z) * mu_shift
    i1 = R_WIDTH
    i2 = 2 * R_WIDTH
    i3 = 3 * R_WIDTH
    i4 = i3 + DECAY_LORA
    i5 = i4 + AAA_LORA
    r, k, v, xw, xa, xg = jnp.split(z, [i1, i2, i3, i4, i5], axis=-1)
    w_log = -jax.nn.softplus(-(w0 + jnp.tanh(xw) @ w_decay_up)) - 0.5
    decay = jnp.exp(-jnp.exp(w_log.astype(jnp.float32)))
    a = jax.nn.sigmoid(a0 + xa @ w_aaa_up)
    g = jax.nn.sigmoid(xg) @ w_gate_up

    def heads(t):
        return t.astype(jnp.float32).reshape(B, S, R_HEADS, R_HEAD)

    kk = heads(k * k_k)
    kk = kk / jnp.maximum(jnp.linalg.norm(kk, axis=-1, keepdims=True), 1e-12)
    k = k * (1.0 + (a - 1.0) * k_a)
    r_h, k_h, v_h, a_h = heads(r), heads(k), heads(v), heads(a)

    def tm(t):
        return jnp.transpose(t, (1, 0, 2, 3))

    def step(state, inp):
        rt, wt, kt, vt, at, bt = inp
        sa = jnp.einsum('bhij,bhj->bhi', state, at)
        state = (state * wt[:, :, None, :] + sa[..., None] * bt[:, :, None, :]
                 + vt[..., None] * kt[:, :, None, :])
        yt = jnp.einsum('bhij,bhj->bhi', state, rt)
        return state, yt

    s0 = jnp.zeros((B, R_HEADS, R_HEAD, R_HEAD), jnp.float32)
    _, y = lax.scan(step, s0, (tm(r_h), tm(heads(decay)), tm(k_h), tm(v_h),
                               tm(-kk), tm(kk * a_h)))
    y = jnp.transpose(y, (1, 0, 2, 3))
    mu = y.mean(-1, keepdims=True)
    var = jnp.square(y - mu).mean(-1, keepdims=True)
    y = ((y - mu) * lax.rsqrt(var + GN_EPS)).reshape(B, S, R_WIDTH) * gn_gain + gn_bias
    bonus = (r_h * k_h * r_k).sum(-1, keepdims=True) * v_h
    return (y + bonus.reshape(B, S, R_WIDTH)) * g


def _fwd_setup_inputs(seed: int = 0) -> dict:
    key = jax.random.key(seed)
    ks = jax.random.split(key, 36)
    L = DEPTH

    def nrm(k, shape, scale):
        return jax.random.normal(k, shape, jnp.float32) * scale

    w_in = nrm(ks[4], (L, D_MODEL, IN_COLS), D_MODEL ** -0.5)
    v_lo = 2 * G_WIDTH + 2 * R_WIDTH
    w_in = w_in.at[:, :, v_lo:v_lo + R_WIDTH].multiply(BETA)
    return {
        'x': nrm(ks[0], (BATCH, SEQ, D_MODEL), 1.0),
        'c': nrm(ks[1], (BATCH, D_MODEL), 1.0),
        'w_ada': nrm(ks[2], (L, D_MODEL, 6 * D_MODEL), 0.5 * D_MODEL ** -0.5),
        'b_ada': nrm(ks[3], (L, 6 * D_MODEL), 0.02),
        'w_in': w_in,
        'b_in': nrm(ks[5], (L, IN_COLS), 0.02),
        'g_ln_v': 1.0 + nrm(ks[6], (L, G_WIDTH), 0.05),
        'b_ln_v': nrm(ks[7], (L, G_WIDTH), 0.02),
        'w_spatial': nrm(ks[8], (L, G_GROUPS, CHUNK, CHUNK), CHUNK ** -0.5),
        'b_spatial': 1.0 + nrm(ks[9], (L, G_GROUPS, CHUNK), 0.1),
        'mu_shift': jax.random.uniform(ks[10], (L, R_SHIFT_COLS), jnp.float32),
        'w0': jax.random.uniform(ks[11], (L, R_WIDTH), jnp.float32, minval=-5.0, maxval=0.0),
        'w_decay_up': nrm(ks[12], (L, DECAY_LORA, R_WIDTH), DECAY_LORA ** -0.5),
        'a0': nrm(ks[13], (L, R_WIDTH), 0.1),
        'w_aaa_up': nrm(ks[14], (L, AAA_LORA, R_WIDTH), AAA_LORA ** -0.5),
        'w_gate_up': nrm(ks[15], (L, GATE_LORA, R_WIDTH), GATE_LORA ** -0.5),
        'k_k': 0.85 + nrm(ks[16], (L, R_WIDTH), 0.05),
        'k_a': 1.0 + nrm(ks[17], (L, R_WIDTH), 0.05),
        'r_k': nrm(ks[18], (L, R_HEADS, R_HEAD), 0.1),
        'gn_gain': 1.0 + nrm(ks[19], (L, R_WIDTH), 0.05),
        'gn_bias': nrm(ks[20], (L, R_WIDTH), 0.02),
        'w_branch_a': nrm(ks[21], (L, G_WIDTH, D_MODEL), BETA * G_WIDTH ** -0.5),
        'w_branch_b': nrm(ks[22], (L, R_WIDTH, D_MODEL), BETA * R_WIDTH ** -0.5),
        'w_out': nrm(ks[23], (L, D_MODEL, D_MODEL), BETA * D_MODEL ** -0.5),
        'b_out': nrm(ks[24], (L, D_MODEL), 0.01),
        'ln1_g': 1.0 + nrm(ks[25], (L, D_MODEL), 0.05),
        'ln1_b': nrm(ks[26], (L, D_MODEL), 0.02),
        'w_ff1': nrm(ks[27], (L, D_MODEL, D_FF), BETA * D_MODEL ** -0.5),
        'b_ff1': nrm(ks[28], (L, D_FF), 0.01),
        'w_ff2': nrm(ks[29], (L, D_FF, D_MODEL), BETA * D_FF ** -0.5),
        'b_ff2': nrm(ks[30], (L, D_MODEL), 0.01),
        'ln2_g': 1.0 + nrm(ks[31], (L, D_MODEL), 0.05),
        'ln2_b': nrm(ks[32], (L, D_MODEL), 0.02),
    }


def _fwd_reference(x, c, w_ada, b_ada, w_in, b_in, g_ln_v, b_ln_v, w_spatial, b_spatial,
              mu_shift, w0, w_decay_up, a0, w_aaa_up, w_gate_up, k_k, k_a, r_k,
              gn_gain, gn_bias, w_branch_a, w_branch_b, w_out, b_out, ln1_g, ln1_b,
              w_ff1, b_ff1, w_ff2, b_ff2, ln2_g, ln2_b):
    out_dtype = x.dtype
    h_res = x.astype(jnp.float32)
    c_act = jax.nn.silu(c.astype(jnp.float32))
    g_end = 2 * G_WIDTH
    r_end = g_end + R_SHIFT_COLS
    for l in range(DEPTH):
        mod = c_act @ w_ada[l] + b_ada[l]
        sh1, sc1, gt1, sh2, sc2, gt2 = [m[:, None, :] for m in jnp.split(mod, 6, axis=-1)]
        h = h_res * (1.0 + sc1) + sh1
        proj = h @ w_in[l] + b_in[l]
        y_a = _gmlp_branch(proj[..., :g_end], g_ln_v[l], b_ln_v[l], w_spatial[l], b_spatial[l])
        y_b = _rwkv7_branch(proj[..., g_end:r_end], mu_shift[l], w0[l], w_decay_up[l], a0[l],
                            w_aaa_up[l], w_gate_up[l], k_k[l], k_a[l], r_k[l],
                            gn_gain[l], gn_bias[l])
        gate_a, gate_b = jnp.split(proj[..., r_end:], 2, axis=-1)
        merged = (jax.nn.sigmoid(gate_a) * (y_a @ w_branch_a[l])
                  + jax.nn.sigmoid(gate_b) * (y_b @ w_branch_b[l]))
        mix = merged @ w_out[l] + b_out[l]
        h_res = _layer_norm(ALPHA * h_res + gt1 * mix, ln1_g[l], ln1_b[l], LN_EPS)
        h = h_res * (1.0 + sc2) + sh2
        ff = jnp.square(jax.nn.relu(h @ w_ff1[l] + b_ff1[l])) @ w_ff2[l] + b_ff2[l]
        h_res = _layer_norm(ALPHA * h_res + gt2 * ff, ln2_g[l], ln2_b[l], LN_EPS)
    return h_res.astype(out_dtype)


import jax as _jax
import jax.numpy as _jnp

TWIN_FORMAT = 'train_step'
FWD_PARAMS = ['x', 'c', 'w_ada', 'b_ada', 'w_in', 'b_in', 'g_ln_v', 'b_ln_v', 'w_spatial', 'b_spatial', 'mu_shift', 'w0', 'w_decay_up', 'a0', 'w_aaa_up', 'w_gate_up', 'k_k', 'k_a', 'r_k', 'gn_gain', 'gn_bias', 'w_branch_a', 'w_branch_b', 'w_out', 'b_out', 'ln1_g', 'ln1_b', 'w_ff1', 'b_ff1', 'w_ff2', 'b_ff2', 'ln2_g', 'ln2_b']
TWIN_WEIGHTS = ['w_ada', 'b_ada', 'w_in', 'b_in', 'g_ln_v', 'b_ln_v', 'w_spatial', 'b_spatial', 'mu_shift', 'w0', 'w_decay_up', 'a0', 'w_aaa_up', 'w_gate_up', 'k_k', 'k_a', 'r_k', 'gn_gain', 'gn_bias', 'w_branch_a', 'w_branch_b', 'w_out', 'b_out', 'ln1_g', 'ln1_b', 'w_ff1', 'b_ff1', 'w_ff2', 'b_ff2', 'ln2_g', 'ln2_b']
TWIN_DIFF_INPUT = 'x'
TWIN_INPUTS = ['x', 'c', 'w_ada', 'b_ada', 'w_in', 'b_in', 'g_ln_v', 'b_ln_v', 'w_spatial', 'b_spatial', 'mu_shift', 'w0', 'w_decay_up', 'a0', 'w_aaa_up', 'w_gate_up', 'k_k', 'k_a', 'r_k', 'gn_gain', 'gn_bias', 'w_branch_a', 'w_branch_b', 'w_out', 'b_out', 'ln1_g', 'ln1_b', 'w_ff1', 'b_ff1', 'w_ff2', 'b_ff2', 'ln2_g', 'ln2_b', 'loss_target', 'm_w_ada', 'm_b_ada', 'm_w_in', 'm_b_in', 'm_g_ln_v', 'm_b_ln_v', 'm_w_spatial', 'm_b_spatial', 'm_mu_shift', 'm_w0', 'm_w_decay_up', 'm_a0', 'm_w_aaa_up', 'm_w_gate_up', 'm_k_k', 'm_k_a', 'm_r_k', 'm_gn_gain', 'm_gn_bias', 'm_w_branch_a', 'm_w_branch_b', 'm_w_out', 'm_b_out', 'm_ln1_g', 'm_ln1_b', 'm_w_ff1', 'm_b_ff1', 'm_w_ff2', 'm_b_ff2', 'm_ln2_g', 'm_ln2_b', 'v_w_ada', 'v_b_ada', 'v_w_in', 'v_b_in', 'v_g_ln_v', 'v_b_ln_v', 'v_w_spatial', 'v_b_spatial', 'v_mu_shift', 'v_w0', 'v_w_decay_up', 'v_a0', 'v_w_aaa_up', 'v_w_gate_up', 'v_k_k', 'v_k_a', 'v_r_k', 'v_gn_gain', 'v_gn_bias', 'v_w_branch_a', 'v_w_branch_b', 'v_w_out', 'v_b_out', 'v_ln1_g', 'v_ln1_b', 'v_w_ff1', 'v_b_ff1', 'v_w_ff2', 'v_b_ff2', 'v_ln2_g', 'v_ln2_b']
TWIN_OUTPUTS = ['loss', 'grad_x', 'grad_w_ada', 'grad_b_ada', 'grad_w_in', 'grad_b_in', 'grad_g_ln_v', 'grad_b_ln_v', 'grad_w_spatial', 'grad_b_spatial', 'grad_mu_shift', 'grad_w0', 'grad_w_decay_up', 'grad_a0', 'grad_w_aaa_up', 'grad_w_gate_up', 'grad_k_k', 'grad_k_a', 'grad_r_k', 'grad_gn_gain', 'grad_gn_bias', 'grad_w_branch_a', 'grad_w_branch_b', 'grad_w_out', 'grad_b_out', 'grad_ln1_g', 'grad_ln1_b', 'grad_w_ff1', 'grad_b_ff1', 'grad_w_ff2', 'grad_b_ff2', 'grad_ln2_g', 'grad_ln2_b', 'delta_w_ada', 'delta_b_ada', 'delta_w_in', 'delta_b_in', 'delta_g_ln_v', 'delta_b_ln_v', 'delta_w_spatial', 'delta_b_spatial', 'delta_mu_shift', 'delta_w0', 'delta_w_decay_up', 'delta_a0', 'delta_w_aaa_up', 'delta_w_gate_up', 'delta_k_k', 'delta_k_a', 'delta_r_k', 'delta_gn_gain', 'delta_gn_bias', 'delta_w_branch_a', 'delta_w_branch_b', 'delta_w_out', 'delta_b_out', 'delta_ln1_g', 'delta_ln1_b', 'delta_w_ff1', 'delta_b_ff1', 'delta_w_ff2', 'delta_b_ff2', 'delta_ln2_g', 'delta_ln2_b', 'new_m_w_ada', 'new_m_b_ada', 'new_m_w_in', 'new_m_b_in', 'new_m_g_ln_v', 'new_m_b_ln_v', 'new_m_w_spatial', 'new_m_b_spatial', 'new_m_mu_shift', 'new_m_w0', 'new_m_w_decay_up', 'new_m_a0', 'new_m_w_aaa_up', 'new_m_w_gate_up', 'new_m_k_k', 'new_m_k_a', 'new_m_r_k', 'new_m_gn_gain', 'new_m_gn_bias', 'new_m_w_branch_a', 'new_m_w_branch_b', 'new_m_w_out', 'new_m_b_out', 'new_m_ln1_g', 'new_m_ln1_b', 'new_m_w_ff1', 'new_m_b_ff1', 'new_m_w_ff2', 'new_m_b_ff2', 'new_m_ln2_g', 'new_m_ln2_b', 'new_v_w_ada', 'new_v_b_ada', 'new_v_w_in', 'new_v_b_in', 'new_v_g_ln_v', 'new_v_b_ln_v', 'new_v_w_spatial', 'new_v_b_spatial', 'new_v_mu_shift', 'new_v_w0', 'new_v_w_decay_up', 'new_v_a0', 'new_v_w_aaa_up', 'new_v_w_gate_up', 'new_v_k_k', 'new_v_k_a', 'new_v_r_k', 'new_v_gn_gain', 'new_v_gn_bias', 'new_v_w_branch_a', 'new_v_w_branch_b', 'new_v_w_out', 'new_v_b_out', 'new_v_ln1_g', 'new_v_ln1_b', 'new_v_w_ff1', 'new_v_b_ff1', 'new_v_w_ff2', 'new_v_b_ff2', 'new_v_ln2_g', 'new_v_ln2_b']
TWIN_LEAF_KINDS = {'loss': 'loss', 'grad_x': 'grad_x', 'grad_w_ada': 'grad_w', 'grad_b_ada': 'grad_w', 'grad_w_in': 'grad_w', 'grad_b_in': 'grad_w', 'grad_g_ln_v': 'grad_w', 'grad_b_ln_v': 'grad_w', 'grad_w_spatial': 'grad_w', 'grad_b_spatial': 'grad_w', 'grad_mu_shift': 'grad_w', 'grad_w0': 'grad_w', 'grad_w_decay_up': 'grad_w', 'grad_a0': 'grad_w', 'grad_w_aaa_up': 'grad_w', 'grad_w_gate_up': 'grad_w', 'grad_k_k': 'grad_w', 'grad_k_a': 'grad_w', 'grad_r_k': 'grad_w', 'grad_gn_gain': 'grad_w', 'grad_gn_bias': 'grad_w', 'grad_w_branch_a': 'grad_w', 'grad_w_branch_b': 'grad_w', 'grad_w_out': 'grad_w', 'grad_b_out': 'grad_w', 'grad_ln1_g': 'grad_w', 'grad_ln1_b': 'grad_w', 'grad_w_ff1': 'grad_w', 'grad_b_ff1': 'grad_w', 'grad_w_ff2': 'grad_w', 'grad_b_ff2': 'grad_w', 'grad_ln2_g': 'grad_w', 'grad_ln2_b': 'grad_w', 'delta_w_ada': 'delta_w', 'delta_b_ada': 'delta_w', 'delta_w_in': 'delta_w', 'delta_b_in': 'delta_w', 'delta_g_ln_v': 'delta_w', 'delta_b_ln_v': 'delta_w', 'delta_w_spatial': 'delta_w', 'delta_b_spatial': 'delta_w', 'delta_mu_shift': 'delta_w', 'delta_w0': 'delta_w', 'delta_w_decay_up': 'delta_w', 'delta_a0': 'delta_w', 'delta_w_aaa_up': 'delta_w', 'delta_w_gate_up': 'delta_w', 'delta_k_k': 'delta_w', 'delta_k_a': 'delta_w', 'delta_r_k': 'delta_w', 'delta_gn_gain': 'delta_w', 'delta_gn_bias': 'delta_w', 'delta_w_branch_a': 'delta_w', 'delta_w_branch_b': 'delta_w', 'delta_w_out': 'delta_w', 'delta_b_out': 'delta_w', 'delta_ln1_g': 'delta_w', 'delta_ln1_b': 'delta_w', 'delta_w_ff1': 'delta_w', 'delta_b_ff1': 'delta_w', 'delta_w_ff2': 'delta_w', 'delta_b_ff2': 'delta_w', 'delta_ln2_g': 'delta_w', 'delta_ln2_b': 'delta_w', 'new_m_w_ada': 'new_m', 'new_m_b_ada': 'new_m', 'new_m_w_in': 'new_m', 'new_m_b_in': 'new_m', 'new_m_g_ln_v': 'new_m', 'new_m_b_ln_v': 'new_m', 'new_m_w_spatial': 'new_m', 'new_m_b_spatial': 'new_m', 'new_m_mu_shift': 'new_m', 'new_m_w0': 'new_m', 'new_m_w_decay_up': 'new_m', 'new_m_a0': 'new_m', 'new_m_w_aaa_up': 'new_m', 'new_m_w_gate_up': 'new_m', 'new_m_k_k': 'new_m', 'new_m_k_a': 'new_m', 'new_m_r_k': 'new_m', 'new_m_gn_gain': 'new_m', 'new_m_gn_bias': 'new_m', 'new_m_w_branch_a': 'new_m', 'new_m_w_branch_b': 'new_m', 'new_m_w_out': 'new_m', 'new_m_b_out': 'new_m', 'new_m_ln1_g': 'new_m', 'new_m_ln1_b': 'new_m', 'new_m_w_ff1': 'new_m', 'new_m_b_ff1': 'new_m', 'new_m_w_ff2': 'new_m', 'new_m_b_ff2': 'new_m', 'new_m_ln2_g': 'new_m', 'new_m_ln2_b': 'new_m', 'new_v_w_ada': 'new_v', 'new_v_b_ada': 'new_v', 'new_v_w_in': 'new_v', 'new_v_b_in': 'new_v', 'new_v_g_ln_v': 'new_v', 'new_v_b_ln_v': 'new_v', 'new_v_w_spatial': 'new_v', 'new_v_b_spatial': 'new_v', 'new_v_mu_shift': 'new_v', 'new_v_w0': 'new_v', 'new_v_w_decay_up': 'new_v', 'new_v_a0': 'new_v', 'new_v_w_aaa_up': 'new_v', 'new_v_w_gate_up': 'new_v', 'new_v_k_k': 'new_v', 'new_v_k_a': 'new_v', 'new_v_r_k': 'new_v', 'new_v_gn_gain': 'new_v', 'new_v_gn_bias': 'new_v', 'new_v_w_branch_a': 'new_v', 'new_v_w_branch_b': 'new_v', 'new_v_w_out': 'new_v', 'new_v_b_out': 'new_v', 'new_v_ln1_g': 'new_v', 'new_v_ln1_b': 'new_v', 'new_v_w_ff1': 'new_v', 'new_v_b_ff1': 'new_v', 'new_v_w_ff2': 'new_v', 'new_v_b_ff2': 'new_v', 'new_v_ln2_g': 'new_v', 'new_v_ln2_b': 'new_v'}


def _forward(args):
    return _fwd_reference(*[args[k] for k in FWD_PARAMS])


def _output_shape():
    out = _jax.eval_shape(lambda: _forward(_fwd_setup_inputs(0)))
    return out.shape, out.dtype

N_MICROBATCH = 1
ADAM_LR = 0.001
ADAM_B1 = 0.9
ADAM_B2 = 0.999
ADAM_EPS = 1e-08
ADAM_WD = 0.01
ADAM_STEP = 10
PER_EXAMPLE_BATCH_AXIS = {'x': 0, 'c': 0, 'loss_target': 0}
SHARED_INPUTS = []
_WEIGHT_DTYPES = {'w_ada': _jnp.float32, 'b_ada': _jnp.float32, 'w_in': _jnp.float32, 'b_in': _jnp.float32, 'g_ln_v': _jnp.float32, 'b_ln_v': _jnp.float32, 'w_spatial': _jnp.float32, 'b_spatial': _jnp.float32, 'mu_shift': _jnp.float32, 'w0': _jnp.float32, 'w_decay_up': _jnp.float32, 'a0': _jnp.float32, 'w_aaa_up': _jnp.float32, 'w_gate_up': _jnp.float32, 'k_k': _jnp.float32, 'k_a': _jnp.float32, 'r_k': _jnp.float32, 'gn_gain': _jnp.float32, 'gn_bias': _jnp.float32, 'w_branch_a': _jnp.float32, 'w_branch_b': _jnp.float32, 'w_out': _jnp.float32, 'b_out': _jnp.float32, 'ln1_g': _jnp.float32, 'ln1_b': _jnp.float32, 'w_ff1': _jnp.float32, 'b_ff1': _jnp.float32, 'w_ff2': _jnp.float32, 'b_ff2': _jnp.float32, 'ln2_g': _jnp.float32, 'ln2_b': _jnp.float32}
MOMENT_SCALE = {'w_ada': 1.747928e-02, 'b_ada': 3.197843e-02, 'w_in': 7.363021e-03, 'b_in': 9.720048e-03, 'g_ln_v': 6.888623e-03, 'b_ln_v': 6.659875e-03, 'w_spatial': 4.626042e-03, 'b_spatial': 6.506560e-03, 'mu_shift': 1.096086e-02, 'w0': 3.552003e-03, 'w_decay_up': 8.816080e-04, 'a0': 2.729813e-03, 'w_aaa_up': 2.558881e-03, 'w_gate_up': 6.942768e-03, 'k_k': 1.186454e-02, 'k_a': 1.038008e-02, 'r_k': 1.321703e-02, 'gn_gain': 7.164067e-03, 'gn_bias': 1.402517e-02, 'w_branch_a': 1.457407e-02, 'w_branch_b': 8.491587e-03, 'w_out': 1.671018e-02, 'b_out': 7.379964e-02, 'ln1_g': 2.958338e+00, 'ln1_b': 4.597670e-01, 'w_ff1': 1.381427e-02, 'b_ff1': 1.415667e-02, 'w_ff2': 2.640656e-02, 'b_ff2': 7.259239e-02, 'ln2_g': 3.220567e+01, 'ln2_b': 8.365663e-01}


def _to_microbatches(a, axis):
    t = _jnp.moveaxis(a, axis, 0)
    t = t.reshape((N_MICROBATCH, t.shape[0] // N_MICROBATCH) + t.shape[1:])
    return _jnp.moveaxis(t, 1, axis + 1)


def setup_inputs(seed: int = 0) -> dict:
    inp = _fwd_setup_inputs(seed)
    key = _jax.random.fold_in(_jax.random.key(seed), 7919)
    shape, _ = _output_shape()
    out = dict(inp)
    out["loss_target"] = _jax.random.normal(_jax.random.fold_in(key, 0), shape, _jnp.float32)
    for i, name in enumerate(TWIN_WEIGHTS):
        w = inp[name].astype(_jnp.float32)
        if MOMENT_SCALE is None:
            s = _jnp.sqrt(_jnp.mean(_jnp.square(w)) + 1e-30)
        else:
            s = MOMENT_SCALE[name]
        km, kv = _jax.random.split(_jax.random.fold_in(key, i + 1))
        out[name] = w
        out["m_" + name] = s * _jax.random.normal(km, w.shape, _jnp.float32)
        out["v_" + name] = (s * s) * _jax.random.uniform(kv, w.shape, _jnp.float32, 0.5, 1.5)
    if N_MICROBATCH > 1:
        for name, axis in PER_EXAMPLE_BATCH_AXIS.items():
            out[name] = _to_microbatches(out[name], axis)
    return {'x': out['x'], 'c': out['c'], 'w_ada': out['w_ada'], 'b_ada': out['b_ada'], 'w_in': out['w_in'], 'b_in': out['b_in'], 'g_ln_v': out['g_ln_v'], 'b_ln_v': out['b_ln_v'], 'w_spatial': out['w_spatial'], 'b_spatial': out['b_spatial'], 'mu_shift': out['mu_shift'], 'w0': out['w0'], 'w_decay_up': out['w_decay_up'], 'a0': out['a0'], 'w_aaa_up': out['w_aaa_up'], 'w_gate_up': out['w_gate_up'], 'k_k': out['k_k'], 'k_a': out['k_a'], 'r_k': out['r_k'], 'gn_gain': out['gn_gain'], 'gn_bias': out['gn_bias'], 'w_branch_a': out['w_branch_a'], 'w_branch_b': out['w_branch_b'], 'w_out': out['w_out'], 'b_out': out['b_out'], 'ln1_g': out['ln1_g'], 'ln1_b': out['ln1_b'], 'w_ff1': out['w_ff1'], 'b_ff1': out['b_ff1'], 'w_ff2': out['w_ff2'], 'b_ff2': out['b_ff2'], 'ln2_g': out['ln2_g'], 'ln2_b': out['ln2_b'], 'loss_target': out['loss_target'], 'm_w_ada': out['m_w_ada'], 'm_b_ada': out['m_b_ada'], 'm_w_in': out['m_w_in'], 'm_b_in': out['m_b_in'], 'm_g_ln_v': out['m_g_ln_v'], 'm_b_ln_v': out['m_b_ln_v'], 'm_w_spatial': out['m_w_spatial'], 'm_b_spatial': out['m_b_spatial'], 'm_mu_shift': out['m_mu_shift'], 'm_w0': out['m_w0'], 'm_w_decay_up': out['m_w_decay_up'], 'm_a0': out['m_a0'], 'm_w_aaa_up': out['m_w_aaa_up'], 'm_w_gate_up': out['m_w_gate_up'], 'm_k_k': out['m_k_k'], 'm_k_a': out['m_k_a'], 'm_r_k': out['m_r_k'], 'm_gn_gain': out['m_gn_gain'], 'm_gn_bias': out['m_gn_bias'], 'm_w_branch_a': out['m_w_branch_a'], 'm_w_branch_b': out['m_w_branch_b'], 'm_w_out': out['m_w_out'], 'm_b_out': out['m_b_out'], 'm_ln1_g': out['m_ln1_g'], 'm_ln1_b': out['m_ln1_b'], 'm_w_ff1': out['m_w_ff1'], 'm_b_ff1': out['m_b_ff1'], 'm_w_ff2': out['m_w_ff2'], 'm_b_ff2': out['m_b_ff2'], 'm_ln2_g': out['m_ln2_g'], 'm_ln2_b': out['m_ln2_b'], 'v_w_ada': out['v_w_ada'], 'v_b_ada': out['v_b_ada'], 'v_w_in': out['v_w_in'], 'v_b_in': out['v_b_in'], 'v_g_ln_v': out['v_g_ln_v'], 'v_b_ln_v': out['v_b_ln_v'], 'v_w_spatial': out['v_w_spatial'], 'v_b_spatial': out['v_b_spatial'], 'v_mu_shift': out['v_mu_shift'], 'v_w0': out['v_w0'], 'v_w_decay_up': out['v_w_decay_up'], 'v_a0': out['v_a0'], 'v_w_aaa_up': out['v_w_aaa_up'], 'v_w_gate_up': out['v_w_gate_up'], 'v_k_k': out['v_k_k'], 'v_k_a': out['v_k_a'], 'v_r_k': out['v_r_k'], 'v_gn_gain': out['v_gn_gain'], 'v_gn_bias': out['v_gn_bias'], 'v_w_branch_a': out['v_w_branch_a'], 'v_w_branch_b': out['v_w_branch_b'], 'v_w_out': out['v_w_out'], 'v_b_out': out['v_b_out'], 'v_ln1_g': out['v_ln1_g'], 'v_ln1_b': out['v_ln1_b'], 'v_w_ff1': out['v_w_ff1'], 'v_b_ff1': out['v_b_ff1'], 'v_w_ff2': out['v_w_ff2'], 'v_b_ff2': out['v_b_ff2'], 'v_ln2_g': out['v_ln2_g'], 'v_ln2_b': out['v_ln2_b']}


def _loss(weights, diff, rest, loss_target):
    with _jax.named_scope("forward"):
        args = {**rest, TWIN_DIFF_INPUT: diff, **{k: w.astype(_WEIGHT_DTYPES[k]) for k, w in weights.items()}}
        y = _forward(args)
    with _jax.named_scope("loss_head"):
        err = _jnp.square(y.astype(_jnp.float32) - loss_target)
        return 0.5 * _jnp.sum(_jnp.mean(err, axis=-1)) if err.ndim else 0.5 * err


def _adamw(w, g, m, v):
    m = ADAM_B1 * m + (1.0 - ADAM_B1) * g
    v = ADAM_B2 * v + (1.0 - ADAM_B2) * _jnp.square(g)
    m_hat = m / (1.0 - ADAM_B1 ** ADAM_STEP)
    v_hat = v / (1.0 - ADAM_B2 ** ADAM_STEP)
    delta = -ADAM_LR * (m_hat / (_jnp.sqrt(v_hat) + ADAM_EPS) + ADAM_WD * w)
    return delta, m, v


def reference(x, c, w_ada, b_ada, w_in, b_in, g_ln_v, b_ln_v, w_spatial, b_spatial, mu_shift, w0, w_decay_up, a0, w_aaa_up, w_gate_up, k_k, k_a, r_k, gn_gain, gn_bias, w_branch_a, w_branch_b, w_out, b_out, ln1_g, ln1_b, w_ff1, b_ff1, w_ff2, b_ff2, ln2_g, ln2_b, loss_target, m_w_ada, m_b_ada, m_w_in, m_b_in, m_g_ln_v, m_b_ln_v, m_w_spatial, m_b_spatial, m_mu_shift, m_w0, m_w_decay_up, m_a0, m_w_aaa_up, m_w_gate_up, m_k_k, m_k_a, m_r_k, m_gn_gain, m_gn_bias, m_w_branch_a, m_w_branch_b, m_w_out, m_b_out, m_ln1_g, m_ln1_b, m_w_ff1, m_b_ff1, m_w_ff2, m_b_ff2, m_ln2_g, m_ln2_b, v_w_ada, v_b_ada, v_w_in, v_b_in, v_g_ln_v, v_b_ln_v, v_w_spatial, v_b_spatial, v_mu_shift, v_w0, v_w_decay_up, v_a0, v_w_aaa_up, v_w_gate_up, v_k_k, v_k_a, v_r_k, v_gn_gain, v_gn_bias, v_w_branch_a, v_w_branch_b, v_w_out, v_b_out, v_ln1_g, v_ln1_b, v_w_ff1, v_b_ff1, v_w_ff2, v_b_ff2, v_ln2_g, v_ln2_b):
    given = dict(x=x, c=c, w_ada=w_ada, b_ada=b_ada, w_in=w_in, b_in=b_in, g_ln_v=g_ln_v, b_ln_v=b_ln_v, w_spatial=w_spatial, b_spatial=b_spatial, mu_shift=mu_shift, w0=w0, w_decay_up=w_decay_up, a0=a0, w_aaa_up=w_aaa_up, w_gate_up=w_gate_up, k_k=k_k, k_a=k_a, r_k=r_k, gn_gain=gn_gain, gn_bias=gn_bias, w_branch_a=w_branch_a, w_branch_b=w_branch_b, w_out=w_out, b_out=b_out, ln1_g=ln1_g, ln1_b=ln1_b, w_ff1=w_ff1, b_ff1=b_ff1, w_ff2=w_ff2, b_ff2=b_ff2, ln2_g=ln2_g, ln2_b=ln2_b, loss_target=loss_target, m_w_ada=m_w_ada, m_b_ada=m_b_ada, m_w_in=m_w_in, m_b_in=m_b_in, m_g_ln_v=m_g_ln_v, m_b_ln_v=m_b_ln_v, m_w_spatial=m_w_spatial, m_b_spatial=m_b_spatial, m_mu_shift=m_mu_shift, m_w0=m_w0, m_w_decay_up=m_w_decay_up, m_a0=m_a0, m_w_aaa_up=m_w_aaa_up, m_w_gate_up=m_w_gate_up, m_k_k=m_k_k, m_k_a=m_k_a, m_r_k=m_r_k, m_gn_gain=m_gn_gain, m_gn_bias=m_gn_bias, m_w_branch_a=m_w_branch_a, m_w_branch_b=m_w_branch_b, m_w_out=m_w_out, m_b_out=m_b_out, m_ln1_g=m_ln1_g, m_ln1_b=m_ln1_b, m_w_ff1=m_w_ff1, m_b_ff1=m_b_ff1, m_w_ff2=m_w_ff2, m_b_ff2=m_b_ff2, m_ln2_g=m_ln2_g, m_ln2_b=m_ln2_b, v_w_ada=v_w_ada, v_b_ada=v_b_ada, v_w_in=v_w_in, v_b_in=v_b_in, v_g_ln_v=v_g_ln_v, v_b_ln_v=v_b_ln_v, v_w_spatial=v_w_spatial, v_b_spatial=v_b_spatial, v_mu_shift=v_mu_shift, v_w0=v_w0, v_w_decay_up=v_w_decay_up, v_a0=v_a0, v_w_aaa_up=v_w_aaa_up, v_w_gate_up=v_w_gate_up, v_k_k=v_k_k, v_k_a=v_k_a, v_r_k=v_r_k, v_gn_gain=v_gn_gain, v_gn_bias=v_gn_bias, v_w_branch_a=v_w_branch_a, v_w_branch_b=v_w_branch_b, v_w_out=v_w_out, v_b_out=v_b_out, v_ln1_g=v_ln1_g, v_ln1_b=v_ln1_b, v_w_ff1=v_w_ff1, v_b_ff1=v_b_ff1, v_w_ff2=v_w_ff2, v_b_ff2=v_b_ff2, v_ln2_g=v_ln2_g, v_ln2_b=v_ln2_b)
    weights = {n: given[n] for n in TWIN_WEIGHTS}
    shared = {n: given[n] for n in SHARED_INPUTS}
    per_example = {n: given[n] for n in ['x', 'c']}
    grad_fn = _jax.value_and_grad(_loss, argnums=(0, 1))

    def one_microbatch(ex, loss_target):
        ex = dict(ex)
        diff = ex.pop(TWIN_DIFF_INPUT)
        return grad_fn(weights, diff, {**shared, **ex}, loss_target)

    if N_MICROBATCH == 1:
        loss, (grad_w, grad_x) = one_microbatch(per_example, given["loss_target"])
    else:
        def body(carry, xs):
            loss_sum, grad_sum = carry
            l_k, (gw_k, gx_k) = one_microbatch(xs[0], xs[1])
            with _jax.named_scope("update"):
                return (loss_sum + l_k, _jax.tree.map(_jnp.add, grad_sum, gw_k)), gx_k

        init = (_jnp.zeros((), _jnp.float32), _jax.tree.map(_jnp.zeros_like, weights))
        (loss, grad_w), grad_x = _jax.lax.scan(body, init, (per_example, given["loss_target"]))
    with _jax.named_scope("update"):
        delta_w, new_m, new_v = {}, {}, {}
        for n in TWIN_WEIGHTS:
            delta_w[n], new_m[n], new_v[n] = _adamw(weights[n], grad_w[n], given["m_" + n], given["v_" + n])
    return (loss, grad_x, *[grad_w[n] for n in TWIN_WEIGHTS], *[delta_w[n] for n in TWIN_WEIGHTS],
            *[new_m[n] for n in TWIN_WEIGHTS], *[new_v[n] for n in TWIN_WEIGHTS])
```

```python
import functools

import jax
import jax.numpy as jnp
from jax import lax
from jax.experimental import pallas as pl
from jax.experimental.pallas import tpu as pltpu

F32 = jnp.float32
_MXU_DTYPE = jnp.bfloat16
_HI = lax.Precision.HIGHEST
_VMEM_LIMIT = 48 * 1024 * 1024
_MESH_ID = pl.DeviceIdType.MESH
_N_DEV = 8

D_MODEL = 1024
G_WIDTH = 512
G_CHUNK = 128
R_WIDTH = 512
R_HEADS = 8
R_HEAD = 64
LORA_W, LORA_A, LORA_G = 32, 32, 96
D_FF = 4096
ALPHA = 2.0 ** 0.25
LN_EPS = 1e-5
GN_EPS = 64e-5
SCAN_CHUNK = 64
ADAM_LR, ADAM_B1, ADAM_B2, ADAM_EPS, ADAM_WD, ADAM_STEP = 0.001, 0.9, 0.999, 1e-08, 0.01, 10

P_COLS = 5120
RW_COLS = 2048
RW_USED = 3 * R_WIDTH + LORA_W + LORA_A + LORA_G
LORA_PAD = 512
IN_COLS = 2 * G_WIDTH + RW_USED + 2 * D_MODEL


def _cparams(sem=None, **kw):
    if sem is not None:
        kw["dimension_semantics"] = sem
    return pltpu.CompilerParams(vmem_limit_bytes=_VMEM_LIMIT, **kw)


def _dot(a, b, dims=(((1,), (0,)), ((), ())), hi=False):
    if hi:
        return lax.dot_general(a.astype(F32), b.astype(F32), dims, precision=_HI, preferred_element_type=F32)
    return lax.dot_general(a.astype(_MXU_DTYPE), b.astype(_MXU_DTYPE), dims, preferred_element_type=F32)


_NN = (((1,), (0,)), ((), ()))
_NT = (((1,), (1,)), ((), ()))
_TN = (((0,), (0,)), ((), ()))


def _pick(n, pref):
    for t in pref:
        if n % t == 0:
            return t
    return n


def _mm(name, a, b, mode, bias=None, out_dtype=F32, epi=None, epi_dtype=None, tm=None, tn=None, tk=None):
    if mode == "nn":
        (M, K), (_, N) = a.shape, b.shape
    elif mode == "nt":
        (M, K), (N, _) = a.shape, b.shape
    else:
        (K, M), (_, N) = a.shape, b.shape
    tm = tm or _pick(M, (512, 256, 128, 64, 32, 16, 8))
    tn = tn or _pick(N, (1024, 512, 640, 384, 256, 128))
    tk = tk or _pick(K, (1024, 512, 256, 128))
    nk = K // tk
    dims = {"nn": _NN, "nt": _NT, "tn": _TN}[mode]
    a_spec = pl.BlockSpec((tk, tm), lambda i, j, k: (k, i)) if mode == "tn" else pl.BlockSpec((tm, tk), lambda i, j, k: (i, k))
    b_spec = pl.BlockSpec((tn, tk), lambda i, j, k: (j, k)) if mode == "nt" else pl.BlockSpec((tk, tn), lambda i, j, k: (k, j))
    o_spec = pl.BlockSpec((tm, tn), lambda i, j, k: (i, j))
    has_bias = bias is not None

    def body(*refs):
        a_ref, b_ref = refs[0], refs[1]
        bias_ref = refs[2] if has_bias else None
        outs = refs[3:] if has_bias else refs[2:]
        o_ref, acc_ref = outs[0], outs[-1]
        k = pl.program_id(2)

        @pl.when(k == 0)
        def _():
            acc_ref[...] = jnp.zeros_like(acc_ref)

        acc_ref[...] += _dot(a_ref[...], b_ref[...], dims)

        @pl.when(k == nk - 1)
        def _():
            res = acc_ref[...]
            if has_bias:
                res = res + bias_ref[...]
            o_ref[...] = res.astype(o_ref.dtype)
            if epi is not None:
                outs[1][...] = epi(res).astype(outs[1].dtype)

    in_specs = [a_spec, b_spec]
    args = [a, b]
    if has_bias:
        in_specs.append(pl.BlockSpec((1, tn), lambda i, j, k: (0, j)))
        args.append(bias)
    out_shape = [jax.ShapeDtypeStruct((M, N), out_dtype)]
    out_specs = [o_spec]
    if epi is not None:
        out_shape.append(jax.ShapeDtypeStruct((M, N), epi_dtype))
        out_specs.append(o_spec)
    res = pl.pallas_call(
        body, name=name, grid=(M // tm, N // tn, nk), in_specs=in_specs, out_specs=out_specs, out_shape=out_shape,
        scratch_shapes=[pltpu.VMEM((tm, tn), F32)],
        compiler_params=_cparams(("parallel", "parallel", "arbitrary")),
    )(*args)
    return res if epi is not None else res[0]


def _rowwise(name, fn, tiled, bcast, tiled_out, red_out, tile):
    tiled = [t if isinstance(t, tuple) else (t, t.shape[1], 0) for t in tiled]
    T = tiled[0][0].shape[0]
    n = T // tile
    nt, nb, nto = len(tiled), len(bcast), len(tiled_out)

    def body(*refs):
        t_refs, b_refs = refs[:nt], refs[nt:nt + nb]
        to_refs, ro_refs = refs[nt + nb:nt + nb + nto], refs[nt + nb + nto:]
        touts, routs = fn([r[...] for r in t_refs], [r[...] for r in b_refs])
        for r, v in zip(to_refs, touts, strict=True):
            r[...] = v.astype(r.dtype)
        if ro_refs:
            i = pl.program_id(0)

            @pl.when(i == 0)
            def _():
                for r, v in zip(ro_refs, routs, strict=True):
                    r[...] = v.astype(F32)

            @pl.when(i > 0)
            def _():
                for r, v in zip(ro_refs, routs, strict=True):
                    r[...] += v.astype(F32)

    def whole(shape):
        nd = len(shape)
        return pl.BlockSpec(tuple(shape), lambda i: (0,) * nd)

    in_specs = [pl.BlockSpec((tile, w), functools.partial(lambda i, cb: (i, cb), cb=cb)) for (_, w, cb) in tiled]
    in_specs += [whole(b.shape) for b in bcast]
    out_specs = [pl.BlockSpec((tile, c), lambda i: (i, 0)) for (c, _) in tiled_out] + [whole(s) for s in red_out]
    out_shape = [jax.ShapeDtypeStruct((T, c), dt) for (c, dt) in tiled_out] + [jax.ShapeDtypeStruct(tuple(s), F32) for s in red_out]
    res = pl.pallas_call(
        body, name=name, grid=(n,), in_specs=in_specs, out_specs=out_specs, out_shape=out_shape,
        compiler_params=_cparams(("arbitrary",)),
    )(*[t[0] for t in tiled], *bcast)
    return list(res[:nto]), list(res[nto:])


def _rowwise_vjp(name, f, tiled, bcast, cts, tile, wrt_t, wrt_b, t_dtypes=None, colsum=()):
    tiled = [t if isinstance(t, tuple) else (t, t.shape[1], 0) for t in tiled]
    npr = len(tiled)
    t_dtypes = t_dtypes or [F32] * len(wrt_t)
    groups = [c if isinstance(c, list) else [c] for c in cts]
    cts = [a_ for grp in groups for a_ in grp]

    def fn(tv, bv):
        prim, flat_ct = tv[:npr], list(tv[npr:])
        ct = []
        for grp in groups:
            parts = [flat_ct.pop(0).astype(F32) for _ in grp]
            ct.append(functools.reduce(lambda p_, q_: p_ + q_, parts))

        def g(dt_vals, db_vals):
            full_t, full_b = list(prim), list(bv)
            for i, v in zip(wrt_t, dt_vals, strict=True):
                full_t[i] = v
            for j, v in zip(wrt_b, db_vals, strict=True):
                full_b[j] = v
            return f(full_t, full_b)

        outs, pull = jax.vjp(g, [prim[i].astype(F32) for i in wrt_t], [bv[j] for j in wrt_b])
        dts, dbs = pull([c.astype(o.dtype) for c, o in zip(ct, outs, strict=True)])
        sums = [jnp.sum(dts[i].astype(F32), axis=0, keepdims=True) for i in colsum]
        return dts, list(dbs) + sums

    tiled_out = [(tiled[i][1], dt) for i, dt in zip(wrt_t, t_dtypes, strict=True)]
    red_out = [bcast[j].shape for j in wrt_b] + [(1, tiled[wrt_t[i]][1]) for i in colsum]
    dts, reds = _rowwise(name, fn, tiled + list(cts), bcast, tiled_out, red_out, tile)
    nb = len(wrt_b)
    return dts, reds[:nb], reds[nb:]


def _layer_norm(x, g, b, eps):
    mu = jnp.mean(x, axis=-1, keepdims=True)
    xc = x - mu
    var = jnp.mean(xc * xc, axis=-1, keepdims=True)
    return xc * lax.rsqrt(var + eps) * g + b


def _gelu_tanh(x):
    return 0.5 * x * (1.0 + jnp.tanh(0.7978845608028654 * (x + 0.044715 * (x * x * x))))


def _sigmoid(x):
    return 1.0 / (1.0 + jnp.exp(-x))


def _seg_modulate(tv, bv):
    (x,), (sc, sh) = tv, bv
    return [x * (1.0 + sc) + sh]


def _seg_gmlp(tv, bv):
    (z,), (g_ln, b_ln, ws, b_tg, expand) = tv, bv
    bias_full = _dot(b_tg, expand, hi=True)
    zz = _gelu_tanh(z)
    u, v = zz[:, :G_WIDTH], zz[:, G_WIDTH:]
    v = _layer_norm(v, g_ln, b_ln, LN_EPS)
    row = lax.broadcasted_iota(jnp.int32, (G_CHUNK, G_CHUNK), 0)
    col = lax.broadcasted_iota(jnp.int32, (G_CHUNK, G_CHUNK), 1)
    causal = col <= row
    first_group = lax.broadcasted_iota(jnp.int32, (G_CHUNK, 128), 1) < 64
    parts = []
    for p in range(4):
        vp = v[:, 128 * p:128 * (p + 1)]
        s_even = _dot(jnp.where(causal, ws[2 * p], 0.0), vp)
        s_odd = _dot(jnp.where(causal, ws[2 * p + 1], 0.0), vp)
        parts.append(jnp.where(first_group, s_even, s_odd))
    s = jnp.concatenate(parts, axis=1) + bias_full
    return [u * s]


def _seg_rwkv_pre(tv, bv):
    (z, prev), (mu, w0, wd, a0, wa, wg, k_k, k_a, gsum) = tv, bv
    zs = z + (prev - z) * mu
    r, k, v = zs[:, 0:512], zs[:, 512:1024], zs[:, 1024:1536]
    zl = zs[:, 1536:2048]
    x = w0 + _dot(jnp.tanh(zl), wd)
    softplus = jnp.maximum(-x, 0.0) + jnp.log(1.0 + jnp.exp(-jnp.abs(x)))
    lw = -jnp.exp(-softplus - 0.5)
    a = _sigmoid(a0 + _dot(zl, wa))
    g = _dot(_sigmoid(zl), wg)
    kk = k * k_k
    nrm = jnp.sqrt(_dot(kk * kk, gsum, hi=True))
    kk = kk / jnp.maximum(nrm, 1e-12)
    k2 = k * (1.0 + (a - 1.0) * k_a)
    return [r, lw, k2, v, -kk, kk * a, g]


def _seg_rwkv_post(tv, bv):
    (y, r, k2, v, g), (r_k, gain, bias, gsum) = tv, bv
    mu = _dot(y, gsum, hi=True) * (1.0 / R_HEAD)
    yc = y - mu
    var = _dot(yc * yc, gsum, hi=True) * (1.0 / R_HEAD)
    yn = yc * lax.rsqrt(var + GN_EPS) * gain + bias
    bonus = _dot(r * k2 * r_k, gsum, hi=True) * v
    return [(yn + bonus) * g]


def _seg_merge(tv, bv):
    (ga, gb, pa, pb), () = tv, bv
    return [_sigmoid(ga) * pa + _sigmoid(gb) * pb]


def _seg_mid(tv, bv):
    (x, mix), (gt1, g1, b1, sc2, sh2) = tv, bv
    h1 = _layer_norm(ALPHA * x + gt1 * mix, g1, b1, LN_EPS)
    return [h1, h1 * (1.0 + sc2) + sh2]


def _seg_relu2(tv, bv):
    (f1,), () = tv, bv
    return [jnp.square(jnp.maximum(f1, 0.0))]


def _seg_loss(tv, bv):
    (h1, ff, target), (gt2, g2, b2) = tv, bv
    out = _layer_norm(ALPHA * h1 + gt2 * ff, g2, b2, LN_EPS)
    err = jnp.square(out - target)
    return 0.5 * jnp.sum(jnp.mean(err, axis=-1))


def _scan_chunk(r, lw, k, v, a, b, s0):
    L = r.shape[0]
    row = lax.broadcasted_iota(jnp.int32, (L, L), 0)
    col = lax.broadcasted_iota(jnp.int32, (L, L), 1)
    incl, strict = col <= row, col < row
    cs = _dot(jnp.where(incl, 1.0, 0.0), lw, hi=True)
    cs_end = cs[L - 1:L, :]
    p, p_inv = jnp.exp(cs), jnp.exp(-cs)
    at, bt, kt, rt = a * jnp.exp(cs - lw), b * p_inv, k * p_inv, r * p
    a_ab = jnp.where(strict, _dot(at, bt, _NT), 0.0)
    a_ak = jnp.where(strict, _dot(at, kt, _NT), 0.0)
    a_rb = jnp.where(incl, _dot(rt, bt, _NT), 0.0)
    a_rk = jnp.where(incl, _dot(rt, kt, _NT), 0.0)
    inv = jnp.where(row == col, 1.0, 0.0) + a_ab
    pw = a_ab
    n = 2
    while n < L:
        pw = _dot(pw, pw)
        inv = inv + _dot(inv, pw)
        n *= 2
    u = _dot(inv, _dot(at, s0, _NT) + _dot(a_ak, v))
    y = _dot(rt, s0, _NT) + _dot(a_rb, u) + _dot(a_rk, v)
    to_end = jnp.exp(cs_end - cs)
    s1 = s0 * jnp.exp(cs_end) + _dot(u, b * to_end, _TN) + _dot(v, k * to_end, _TN)
    return y, s1


def _scan_fwd(r, lw, k, v, a, b):
    H, T, N = r.shape
    L = SCAN_CHUNK
    nc = T // L

    def body(r_ref, lw_ref, k_ref, v_ref, a_ref, b_ref, y_ref, st_ref, s_ref):
        @pl.when(pl.program_id(0) == 0)
        def _():
            s_ref[...] = jnp.zeros_like(s_ref)

        for h in range(H):
            s0 = s_ref[h]
            st_ref[0, h] = s0
            y, s1 = _scan_chunk(r_ref[h], lw_ref[h], k_ref[h], v_ref[h], a_ref[h], b_ref[h], s0)
            y_ref[h] = y
            s_ref[h] = s1

    blk = pl.BlockSpec((H, L, N), lambda c: (0, c, 0))
    return pl.pallas_call(
        body, name="scan_fwd", grid=(nc,), in_specs=[blk] * 6,
        out_specs=[blk, pl.BlockSpec((1, H, N, N), lambda c: (c, 0, 0, 0))],
        out_shape=[jax.ShapeDtypeStruct((H, T, N), F32), jax.ShapeDtypeStruct((nc, H, N, N), F32)],
        scratch_shapes=[pltpu.VMEM((H, N, N), F32)],
        compiler_params=_cparams(("arbitrary",)),
    )(r, lw, k, v, a, b)


def _scan_bwd(r, lw, k, v, a, b, states, dy):
    H, T, N = r.shape
    L = SCAN_CHUNK
    nc = T // L

    def body(r_ref, lw_ref, k_ref, v_ref, a_ref, b_ref, st_ref, dy_ref, dr_ref, dlw_ref, dk_ref, dv_ref, da_ref, db_ref, ds_ref):
        @pl.when(pl.program_id(0) == 0)
        def _():
            ds_ref[...] = jnp.zeros_like(ds_ref)

        for h in range(H):
            args = (r_ref[h], lw_ref[h], k_ref[h], v_ref[h], a_ref[h], b_ref[h], st_ref[0, h])
            _, pull = jax.vjp(_scan_chunk, *args)
            dr, dlw, dk, dv, da, db, ds0 = pull((dy_ref[h], ds_ref[h]))
            dr_ref[h], dlw_ref[h], dk_ref[h], dv_ref[h], da_ref[h], db_ref[h] = dr, dlw, dk, dv, da, db
            ds_ref[h] = ds0

    blk = pl.BlockSpec((H, L, N), lambda c: (0, nc - 1 - c, 0))
    st_blk = pl.BlockSpec((1, H, N, N), lambda c: (nc - 1 - c, 0, 0, 0))
    return pl.pallas_call(
        body, name="scan_bwd", grid=(nc,), in_specs=[blk] * 6 + [st_blk, blk], out_specs=[blk] * 6,
        out_shape=[jax.ShapeDtypeStruct((H, T, N), F32)] * 6,
        scratch_shapes=[pltpu.VMEM((H, N, N), F32)],
        compiler_params=_cparams(("arbitrary",)),
    )(r, lw, k, v, a, b, states, dy)


def _to_heads(x):
    T = x.shape[0]
    return x.reshape(T, R_HEADS, R_HEAD).transpose(1, 0, 2)


def _from_heads(x):
    T = x.shape[1]
    return x.transpose(1, 0, 2).reshape(T, R_WIDTH)


def _place():
    x, y, c = lax.axis_index("x"), lax.axis_index("y"), lax.axis_index("c")
    return x, y, c


def _all_gather(name, block):
    R, C = block.shape

    def body(x_ref, out_ref, send_sems, recv_sems, local_sem):
        x, y, c = _place()
        me, sibling = (x, y, c), (x, y, 1 - c)
        chips = [(1 - x, y), (x, 1 - y), (1 - x, 1 - y)]

        def slot(px, py, pc):
            return out_ref.at[4 * px + 2 * py + pc]

        def copy(k, blk, to, src=None):
            return pltpu.make_async_remote_copy(
                src_ref=slot(*blk) if src is None else src, dst_ref=slot(*blk),
                send_sem=send_sems.at[k], recv_sem=recv_sems.at[k], device_id=to, device_id_type=_MESH_ID)

        mine = pltpu.make_async_copy(x_ref, slot(*me), local_sem)
        mine.start()
        first = [copy(0, me, sibling, src=x_ref)]
        first += [copy(1 + j, me, (*chip, c), src=x_ref) for j, chip in enumerate(chips)]
        for cp in first:
            cp.start()
        passed = [copy(4 + j, (*chip, c), sibling) for j, chip in enumerate(chips)]
        for j, chip in enumerate(chips):
            copy(1 + j, (*chip, c), me).wait_recv()
            passed[j].start()
        copy(0, sibling, me).wait_recv()
        for j, chip in enumerate(chips):
            copy(4 + j, (*chip, 1 - c), me).wait_recv()
        for cp in first + passed:
            cp.wait_send()
        mine.wait()

    hbm = pl.BlockSpec(memory_space=pltpu.HBM)
    return pl.pallas_call(
        body, name=name, in_specs=[hbm], out_specs=hbm,
        out_shape=jax.ShapeDtypeStruct((_N_DEV, R, C), block.dtype),
        scratch_shapes=[pltpu.SemaphoreType.DMA((7,)), pltpu.SemaphoreType.DMA((7,)), pltpu.SemaphoreType.DMA],
    )(block)


def _all_to_all(name, blocks):
    _, R, C = blocks.shape

    def body(x_ref, out_ref, send_sems, recv_sems, local_sem):
        x, y, c = _place()
        me = 4 * x + 2 * y + c
        mine = pltpu.make_async_copy(x_ref.at[me], out_ref.at[me], local_sem)
        mine.start()
        copies = []
        for rel in range(1, _N_DEV):
            dx, dy, dc = (rel >> 2) & 1, (rel >> 1) & 1, rel & 1
            px, py, pc = (1 - x if dx else x), (1 - y if dy else y), (1 - c if dc else c)
            copies.append(pltpu.make_async_remote_copy(
                src_ref=x_ref.at[4 * px + 2 * py + pc], dst_ref=out_ref.at[me],
                send_sem=send_sems.at[rel - 1], recv_sem=recv_sems.at[rel - 1],
                device_id=(px, py, pc), device_id_type=_MESH_ID))
        for cp in copies:
            cp.start()
        for cp in copies:
            cp.wait()
        mine.wait()

    hbm = pl.BlockSpec(memory_space=pltpu.HBM)
    return pl.pallas_call(
        body, name=name, in_specs=[hbm], out_specs=hbm,
        out_shape=jax.ShapeDtypeStruct(blocks.shape, blocks.dtype),
        scratch_shapes=[pltpu.SemaphoreType.DMA((7,)), pltpu.SemaphoreType.DMA((7,)), pltpu.SemaphoreType.DMA],
    )(blocks)


def _sum_leading(name, x, tile=None):
    n, R, C = x.shape
    tile = tile or _pick(R, (512, 256, 128, 64, 32, 16, 8))

    def body(x_ref, o_ref):
        acc = x_ref[0]
        for k in range(1, n):
            acc = acc + x_ref[k]
        o_ref[...] = acc

    return pl.pallas_call(
        body, name=name, grid=(R // tile,), in_specs=[pl.BlockSpec((n, tile, C), lambda i: (0, i, 0))],
        out_specs=pl.BlockSpec((tile, C), lambda i: (i, 0)), out_shape=jax.ShapeDtypeStruct((R, C), F32),
        compiler_params=_cparams(("parallel",)),
    )(x)


def _adamw(name, w, g, m, v):
    R, C = w.shape
    tile = _pick(R, (256, 128, 64, 32, 16, 8))

    def fn(tv, bv):
        w_, g_, m_, v_ = tv
        m2 = ADAM_B1 * m_ + (1.0 - ADAM_B1) * g_
        v2 = ADAM_B2 * v_ + (1.0 - ADAM_B2) * jnp.square(g_)
        m_hat = m2 / (1.0 - ADAM_B1 ** ADAM_STEP)
        v_hat = v2 / (1.0 - ADAM_B2 ** ADAM_STEP)
        delta = -ADAM_LR * (m_hat / (jnp.sqrt(v_hat) + ADAM_EPS) + ADAM_WD * w_)
        return [delta, m2, v2], []

    outs, _ = _rowwise(name, fn, [w, g, m, v], [], [(C, F32)] * 3, [], tile)
    return outs


def _pack_rows(arrs, lanes=128, row_mult=8):
    flat, places, off = [], [], 0
    for a_ in arrs:
        n = a_.size
        flat.append(a_.reshape(-1).astype(F32))
        places.append((off, n, a_.shape))
        off += n
    total = -(-off // (lanes * row_mult)) * (lanes * row_mult)
    if total > off:
        flat.append(jnp.zeros((total - off,), F32))
    return jnp.concatenate(flat).reshape(total // lanes, lanes), places


def _unpack_rows(packed, places):
    flat = packed.reshape(-1)
    return [flat[o:o + n].reshape(s) for (o, n, s) in places]


_WEIGHTS = ['w_ada', 'b_ada', 'w_in', 'b_in', 'g_ln_v', 'b_ln_v', 'w_spatial', 'b_spatial', 'mu_shift', 'w0', 'w_decay_up', 'a0',
            'w_aaa_up', 'w_gate_up', 'k_k', 'k_a', 'r_k', 'gn_gain', 'gn_bias', 'w_branch_a', 'w_branch_b', 'w_out', 'b_out',
            'ln1_g', 'ln1_b', 'w_ff1', 'b_ff1', 'w_ff2', 'b_ff2', 'ln2_g', 'ln2_b']
_BIG = {'w_in': (1, (D_MODEL, IN_COLS)), 'w_branch_a': (1, (G_WIDTH, D_MODEL)), 'w_branch_b': (1, (R_WIDTH, D_MODEL)),
        'w_out': (0, (D_MODEL, D_MODEL)), 'w_ff1': (1, (D_MODEL, D_FF)), 'w_ff2': (0, (D_FF, D_MODEL))}
_LORA = {'w_decay_up': (0, LORA_W), 'w_aaa_up': (LORA_W, LORA_A), 'w_gate_up': (LORA_W + LORA_A, LORA_G)}


def _join_shards(name, shards):
    axis, _ = _BIG[name]
    if axis == 0:
        return shards.reshape(-1, shards.shape[2])
    return shards.transpose(1, 0, 2).reshape(shards.shape[1], -1)


def _cut_shards(name, full):
    axis, _ = _BIG[name]
    R, C = full.shape
    if axis == 0:
        return full.reshape(_N_DEV, R // _N_DEV, C)
    return full.reshape(R, _N_DEV, C // _N_DEV).transpose(1, 0, 2)


def _to_padded_cols(a):
    g_end = 2 * G_WIDTH
    r_end = g_end + RW_USED
    return jnp.concatenate([a[:, r_end:], a[:, g_end:r_end], jnp.zeros((a.shape[0], RW_COLS - RW_USED), a.dtype), a[:, :g_end]], axis=1)


def _from_padded_cols(a):
    return jnp.concatenate([a[:, 2 * D_MODEL + RW_COLS:], a[:, 2 * D_MODEL:2 * D_MODEL + RW_USED], a[:, :2 * D_MODEL]], axis=1)


def _pad_rows(a, rows):
    return jnp.concatenate([a, jnp.zeros((rows - a.shape[0],) + a.shape[1:], a.dtype)], axis=0)


def _step(p, m, v, x, c, target):
    T = x.shape[0]
    xi, yi, ci = _place()
    me = 4 * xi + 2 * yi + ci
    tile = 256

    lane = jnp.arange(R_WIDTH)
    gsum = (lane[:, None] // R_HEAD == lane[None, :] // R_HEAD).astype(F32)
    expand = (jnp.arange(128)[:, None] == (lane[None, :] // (G_WIDTH // 8))).astype(F32)

    (c_act,), _ = _rowwise("silu_c", lambda tv, bv: ([tv[0] * _sigmoid(tv[0])], []), [c], [], [(D_MODEL, F32)], [], 1)
    small, places = _pack_rows([c_act, p['w_decay_up'], p['w_aaa_up'], p['w_gate_up']])
    small_all = _all_gather("gather_small", small)
    per_dev = [_unpack_rows(small_all[d], places) for d in range(_N_DEV)]
    c_act_all = _pad_rows(jnp.concatenate([pd[0] for pd in per_dev], axis=0), 16)
    lora_full = {n: jnp.concatenate([pd[i + 1] for pd in per_dev], axis=1) for i, n in enumerate(_LORA)}
    lora_pad = {n: jnp.zeros((LORA_PAD, R_WIDTH), F32).at[r0:r0 + nr].set(lora_full[n]) for n, (r0, nr) in _LORA.items()}

    big_names = list(_BIG)
    packed, big_places = _pack_rows([p[n] for n in big_names], row_mult=512)
    packed_all = _all_gather("gather_weights", packed.astype(_MXU_DTYPE))
    per_dev = [_unpack_rows(packed_all[d], big_places) for d in range(_N_DEV)]
    wfull = {n: _join_shards(n, jnp.stack([pd[i] for pd in per_dev])) for i, n in enumerate(big_names)}
    w_in_p = _to_padded_cols(wfull['w_in'])
    b_in_p = _to_padded_cols(p['b_in'])
    mu_p = jnp.concatenate([p['mu_shift'], jnp.zeros((1, RW_COLS - RW_USED), F32)], axis=1)

    b_ada_mine = lax.dynamic_slice(p['b_ada'], (0, me * 768), (1, 768))
    mod_cols = _mm("ada_mod", c_act_all, p['w_ada'], "nn", bias=b_ada_mine)
    mod_all = _all_gather("gather_mod", mod_cols)
    mod = lax.dynamic_index_in_dim(mod_all, me, axis=1, keepdims=False).reshape(1, 6 * D_MODEL)
    sh1, sc1, gt1, sh2, sc2, gt2 = [mod[:, i * D_MODEL:(i + 1) * D_MODEL] for i in range(6)]

    (h,), _ = _rowwise("modulate1", lambda tv, bv: (_seg_modulate(tv, bv), []), [x], [sc1, sh1], [(D_MODEL, _MXU_DTYPE)], [], tile)
    proj = _mm("in_proj", h, w_in_p, "nn", bias=b_in_p)
    ws = p['w_spatial']
    b_tg = jnp.zeros((G_CHUNK, 128), F32).at[:, :8].set(p['b_spatial'].T)
    gmlp_b = [p['g_ln_v'], p['b_ln_v'], ws, b_tg, expand]
    z_gmlp = (proj, 2 * G_WIDTH, 4)
    (ya,), _ = _rowwise("gmlp", lambda tv, bv: (_seg_gmlp(tv, bv), []), [z_gmlp], gmlp_b, [(G_WIDTH, F32)], [], G_CHUNK)
    z_rw = (proj, RW_COLS, 1)
    prev = jnp.concatenate([jnp.zeros((1, RW_COLS), F32), proj[:-1, RW_COLS:2 * RW_COLS]], axis=0)
    pre_b = [mu_p, p['w0'], lora_pad['w_decay_up'], p['a0'], lora_pad['w_aaa_up'], lora_pad['w_gate_up'], p['k_k'], p['k_a'], gsum]
    pre_out, _ = _rowwise("rwkv_pre", lambda tv, bv: (_seg_rwkv_pre(tv, bv), []), [z_rw, prev], pre_b, [(R_WIDTH, F32)] * 7, [], tile)
    r_, lw_, k2_, v_, a_, b_, g_ = pre_out
    heads = [_to_heads(t) for t in (r_, lw_, k2_, v_, a_, b_)]
    y_h, states = _scan_fwd(*heads)
    y_ = _from_heads(y_h)
    post_b = [p['r_k'].reshape(1, R_WIDTH), p['gn_gain'], p['gn_bias'], gsum]
    (yb,), _ = _rowwise("rwkv_post", lambda tv, bv: (_seg_rwkv_post(tv, bv), []), [y_, r_, k2_, v_, g_], post_b, [(R_WIDTH, F32)], [], tile)
    pa = _mm("branch_a", ya, wfull['w_branch_a'], "nn")
    pb = _mm("branch_b", yb, wfull['w_branch_b'], "nn")
    gates = [(proj, D_MODEL, 0), (proj, D_MODEL, 1)]
    (merged,), _ = _rowwise("merge", lambda tv, bv: (_seg_merge(tv, bv), []), gates + [pa, pb], [], [(D_MODEL, _MXU_DTYPE)], [], tile)
    mix = _mm("out_proj", merged, wfull['w_out'], "nn", bias=p['b_out'])
    mid_b = [gt1, p['ln1_g'], p['ln1_b'], sc2, sh2]
    (h1, h2in), _ = _rowwise("mid", lambda tv, bv: (_seg_mid(tv, bv), []), [x, mix], mid_b, [(D_MODEL, F32), (D_MODEL, _MXU_DTYPE)], [], tile)
    f1, act = _mm("ff1", h2in, wfull['w_ff1'], "nn", bias=p['b_ff1'], epi=lambda t: _seg_relu2([t], [])[0], epi_dtype=_MXU_DTYPE)
    ff = _mm("ff2", act, wfull['w_ff2'], "nn", bias=p['b_ff2'])

    def loss_fn(tv, bv):
        h1_t, ff_t, tgt = tv
        val, grads = jax.value_and_grad(lambda a0_, a1_, b0_, b1_, b2_: _seg_loss([a0_, a1_, tgt], [b0_, b1_, b2_]), argnums=(0, 1, 2, 3, 4))(h1_t, ff_t, *bv)
        return [grads[0], grads[1]], [grads[2], grads[3], grads[4], jnp.sum(grads[1], axis=0, keepdims=True), jnp.full((1, 128), val, F32)]

    (dh1_a, dff), (d_gt2, d_ln2_g, d_ln2_b, d_b_ff2, loss_row) = _rowwise(
        "loss", loss_fn, [h1, ff, target], [gt2, p['ln2_g'], p['ln2_b']], [(D_MODEL, F32), (D_MODEL, _MXU_DTYPE)], [(1, D_MODEL)] * 4 + [(1, 128)], tile)
    loss = lax.psum(loss_row[0, 0], ("x", "y", "c"))

    g = {}
    g['ln2_g'], g['ln2_b'], g['b_ff2'] = d_ln2_g, d_ln2_b, d_b_ff2
    gw = {}
    gw['w_ff2'] = _mm("g_w_ff2", act, dff, "tn")
    dact = _mm("d_act", dff, wfull['w_ff2'], "nt")
    (df1,), _, (g['b_ff1'],) = _rowwise_vjp("relu2_bwd", _seg_relu2, [f1], [], [dact], tile, [0], [], t_dtypes=[_MXU_DTYPE], colsum=[0])
    gw['w_ff1'] = _mm("g_w_ff1", h2in, df1, "tn")
    dh2in = _mm("d_h2in", df1, wfull['w_ff1'], "nt")
    (dx_a, dmix), (d_gt1, g['ln1_g'], g['ln1_b'], d_sc2, d_sh2), (g['b_out'],) = _rowwise_vjp(
        "mid_bwd", _seg_mid, [x, mix], mid_b, [dh1_a, dh2in], tile, [0, 1], [0, 1, 2, 3, 4], t_dtypes=[F32, _MXU_DTYPE], colsum=[1])
    gw['w_out'] = _mm("g_w_out", merged, dmix, "tn")
    dmerged = _mm("d_merged", dmix, wfull['w_out'], "nt")
    (dga, dgb, dpa, dpb), _, (cs_ga, cs_gb) = _rowwise_vjp(
        "merge_bwd", _seg_merge, gates + [pa, pb], [], [dmerged], tile, [0, 1, 2, 3], [], t_dtypes=[_MXU_DTYPE] * 4, colsum=[0, 1])
    gw['w_branch_a'] = _mm("g_w_branch_a", ya, dpa, "tn")
    gw['w_branch_b'] = _mm("g_w_branch_b", yb, dpb, "tn")
    dya = _mm("d_ya", dpa, wfull['w_branch_a'], "nt")
    dyb = _mm("d_yb", dpb, wfull['w_branch_b'], "nt")
    (dy, dr1, dk1, dv1, dg_), (d_r_k, g['gn_gain'], g['gn_bias']), _ = _rowwise_vjp(
        "rwkv_post_bwd", _seg_rwkv_post, [y_, r_, k2_, v_, g_], post_b, [dyb], tile, [0, 1, 2, 3, 4], [0, 1, 2])
    g['r_k'] = d_r_k
    dscan = _scan_bwd(*heads, states, _to_heads(dy))
    dr2, dlw, dk2, dv2, da, db = [_from_heads(t) for t in dscan]
    (dz_rw, dprev), (d_mu, g['w0'], d_wd, g['a0'], d_wa, d_wg, g['k_k'], g['k_a']), _ = _rowwise_vjp(
        "rwkv_pre_bwd", _seg_rwkv_pre, [z_rw, prev], pre_b, [[dr1, dr2], dlw, [dk1, dk2], [dv1, dv2], da, db, dg_], 128, [0, 1], [0, 1, 2, 3, 4, 5, 6, 7])
    g['mu_shift'] = d_mu[:, :RW_USED]
    for n, d_ in (('w_decay_up', d_wd), ('w_aaa_up', d_wa), ('w_gate_up', d_wg)):
        r0, nr = _LORA[n]
        g[n] = d_[r0:r0 + nr]
    dprev_next = jnp.concatenate([dprev[1:], jnp.zeros((1, RW_COLS), F32)], axis=0)
    (dz_rw_all,), (cs_rw,) = _rowwise(
        "shift_bwd", lambda tv, bv: ([tv[0] + tv[1]], [jnp.sum(tv[0] + tv[1], axis=0, keepdims=True)]), [dz_rw, dprev_next], [], [(RW_COLS, _MXU_DTYPE)], [(1, RW_COLS)], tile)
    (dz_g,), (g['g_ln_v'], g['b_ln_v'], g['w_spatial'], d_b_tg), (cs_g,) = _rowwise_vjp(
        "gmlp_bwd", _seg_gmlp, [z_gmlp], gmlp_b, [dya], G_CHUNK, [0], [0, 1, 2, 3], t_dtypes=[_MXU_DTYPE], colsum=[0])
    g['b_spatial'] = d_b_tg[:, :8].T
    dproj = jnp.concatenate([dga, dgb, dz_rw_all, dz_g], axis=1)
    g['b_in'] = _from_padded_cols(jnp.concatenate([cs_ga, cs_gb, cs_rw, cs_g], axis=1))
    gw['w_in'] = _from_padded_cols(_mm("g_w_in", h, dproj, "tn"))
    dh = _mm("d_h", dproj, w_in_p, "nt")

    def mod1_bwd(tv, bv):
        x_t, dh_t, dxa_t = tv
        (sc,) = bv
        return [dxa_t + dh_t * (1.0 + sc)], [jnp.sum(dh_t * x_t, axis=0, keepdims=True), jnp.sum(dh_t, axis=0, keepdims=True)]

    (grad_x,), (d_sc1, d_sh1) = _rowwise("modulate1_bwd", mod1_bwd, [x, dh, dx_a], [sc1], [(D_MODEL, F32)], [(1, D_MODEL)] * 2, tile)

    dmod = jnp.concatenate([d_sh1, d_sc1, d_gt1, d_sh2, d_sc2, d_gt2], axis=1).reshape(6 * D_MODEL // 128, 128)
    dmod_all = _all_gather("gather_dmod", dmod)
    g['b_ada'] = _sum_leading("sum_dmod", dmod_all).reshape(1, 6 * D_MODEL)
    dmod_mine = lax.dynamic_slice(dmod_all.reshape(_N_DEV, 6 * D_MODEL), (0, me * 768), (_N_DEV, 768))
    g_w_ada = _mm("g_w_ada", c_act_all, _pad_rows(dmod_mine, 16), "tn")

    small_names = [n for n in _WEIGHTS if n not in _BIG and n not in ('w_ada', 'b_ada')]
    packed_g, g_places = _pack_rows([g[n] for n in small_names], row_mult=256)
    small_sum = _unpack_rows(_sum_leading("sum_small", _all_gather("gather_small_grads", packed_g)), g_places)
    for n, t in zip(small_names, small_sum, strict=True):
        g[n] = t
    for n in _LORA:
        g[n] = lax.dynamic_slice(g[n], (0, me * R_HEAD), (g[n].shape[0], R_HEAD))
    g['w_ada'] = g_w_ada

    parts = [_cut_shards(n, gw[n]).reshape(_N_DEV, -1) for n in big_names]
    used = sum(pt.shape[1] for pt in parts)
    rows = packed.shape[0]
    sendbuf = jnp.concatenate(parts + [jnp.zeros((_N_DEV, rows * 128 - used), F32)], axis=1).reshape(_N_DEV, rows, 128)
    summed = _sum_leading("sum_big", _all_to_all("scatter_big_grads", sendbuf))
    flat, off = summed.reshape(-1), 0
    for n in big_names:
        g[n] = flat[off:off + p[n].size].reshape(p[n].shape)
        off += p[n].size

    delta, new_m, new_v = {}, {}, {}
    own_call = ['w_ada'] + big_names
    for n in own_call:
        delta[n], new_m[n], new_v[n] = _adamw("adamw_" + n, p[n], g[n], m[n], v[n])
    rest = [n for n in _WEIGHTS if n not in own_call]
    packs = [_pack_rows([d[n] for n in rest], row_mult=256) for d in (p, g, m, v)]
    outs = _adamw("adamw_rest", *[pk[0] for pk in packs])
    for d, o in zip((delta, new_m, new_v), outs, strict=True):
        for n, t in zip(rest, _unpack_rows(o, packs[0][1]), strict=True):
            d[n] = t
    return loss, grad_x, g, delta, new_m, new_v


def kernel(x, c, w_ada, b_ada, w_in, b_in, g_ln_v, b_ln_v, w_spatial, b_spatial, mu_shift, w0, w_decay_up, a0, w_aaa_up, w_gate_up, k_k, k_a, r_k, gn_gain, gn_bias, w_branch_a, w_branch_b, w_out, b_out, ln1_g, ln1_b, w_ff1, b_ff1, w_ff2, b_ff2, ln2_g, ln2_b, loss_target, m_w_ada, m_b_ada, m_w_in, m_b_in, m_g_ln_v, m_b_ln_v, m_w_spatial, m_b_spatial, m_mu_shift, m_w0, m_w_decay_up, m_a0, m_w_aaa_up, m_w_gate_up, m_k_k, m_k_a, m_r_k, m_gn_gain, m_gn_bias, m_w_branch_a, m_w_branch_b, m_w_out, m_b_out, m_ln1_g, m_ln1_b, m_w_ff1, m_b_ff1, m_w_ff2, m_b_ff2, m_ln2_g, m_ln2_b, v_w_ada, v_b_ada, v_w_in, v_b_in, v_g_ln_v, v_b_ln_v, v_w_spatial, v_b_spatial, v_mu_shift, v_w0, v_w_decay_up, v_a0, v_w_aaa_up, v_w_gate_up, v_k_k, v_k_a, v_r_k, v_gn_gain, v_gn_bias, v_w_branch_a, v_w_branch_b, v_w_out, v_b_out, v_ln1_g, v_ln1_b, v_w_ff1, v_b_ff1, v_w_ff2, v_b_ff2, v_ln2_g, v_ln2_b):
    given = dict(locals())
    shapes = {n: given[n].shape for n in _WEIGHTS}
    def two_d(a_):
        a_ = a_[0]
        return a_.reshape(1, -1) if a_.ndim == 1 else a_
    p = {n: two_d(given[n]) for n in _WEIGHTS}
    m = {n: two_d(given["m_" + n]) for n in _WEIGHTS}
    v = {n: two_d(given["v_" + n]) for n in _WEIGHTS}
    loss, grad_x, g, delta, new_m, new_v = _step(p, m, v, x[0], c, loss_target[0])
    outs = [loss, grad_x[None]]
    for d in (g, delta, new_m, new_v):
        outs += [d[n].reshape(shapes[n]) for n in _WEIGHTS]
    return tuple(outs)
```

```python
import functools

import jax
import jax.numpy as jnp
from jax import lax
from jax.experimental import pallas as pl
from jax.experimental.pallas import tpu as pltpu

F32 = jnp.float32
_MXU_DTYPE = jnp.bfloat16
_HI = lax.Precision.HIGHEST
_VMEM_LIMIT = 48 * 1024 * 1024
_MESH_ID = pl.DeviceIdType.MESH
_N_DEV = 8

D_MODEL = 1024
G_WIDTH = 512
G_CHUNK = 128
R_WIDTH = 512
R_HEADS = 8
R_HEAD = 64
LORA_W, LORA_A, LORA_G = 32, 32, 96
D_FF = 4096
ALPHA = 2.0 ** 0.25
LN_EPS = 1e-5
GN_EPS = 64e-5
SCAN_CHUNK = 64
ADAM_LR, ADAM_B1, ADAM_B2, ADAM_EPS, ADAM_WD, ADAM_STEP = 0.001, 0.9, 0.999, 1e-08, 0.01, 10

P_COLS = 5120
RW_COLS = 2048
RW_USED = 3 * R_WIDTH + LORA_W + LORA_A + LORA_G
LORA_PAD = 512
IN_COLS = 2 * G_WIDTH + RW_USED + 2 * D_MODEL


def _cparams(sem=None, **kw):
    if sem is not None:
        kw["dimension_semantics"] = sem
    return pltpu.CompilerParams(vmem_limit_bytes=_VMEM_LIMIT, **kw)


def _dot(a, b, dims=(((1,), (0,)), ((), ())), hi=False):
    if hi:
        return lax.dot_general(a.astype(F32), b.astype(F32), dims, precision=_HI, preferred_element_type=F32)
    return lax.dot_general(a.astype(_MXU_DTYPE), b.astype(_MXU_DTYPE), dims, preferred_element_type=F32)


_NN = (((1,), (0,)), ((), ()))
_NT = (((1,), (1,)), ((), ()))
_TN = (((0,), (0,)), ((), ()))


def _pick(n, pref):
    for t in pref:
        if n % t == 0:
            return t
    return n


def _mm(name, a, b, mode, bias=None, out_dtype=F32, epi=None, epi_dtype=None, tm=None, tn=None, tk=None):
    if mode == "nn":
        (M, K), (_, N) = a.shape, b.shape
    elif mode == "nt":
        (M, K), (N, _) = a.shape, b.shape
    else:
        (K, M), (_, N) = a.shape, b.shape
    tm = tm or _pick(M, (512, 256, 128, 64, 32, 16, 8))
    tn = tn or _pick(N, (1024, 512, 640, 384, 256, 128))
    tk = tk or _pick(K, (1024, 512, 256, 128))
    nk = K // tk
    dims = {"nn": _NN, "nt": _NT, "tn": _TN}[mode]
    a_spec = pl.BlockSpec((tk, tm), lambda i, j, k: (k, i)) if mode == "tn" else pl.BlockSpec((tm, tk), lambda i, j, k: (i, k))
    b_spec = pl.BlockSpec((tn, tk), lambda i, j, k: (j, k)) if mode == "nt" else pl.BlockSpec((tk, tn), lambda i, j, k: (k, j))
    o_spec = pl.BlockSpec((tm, tn), lambda i, j, k: (i, j))
    has_bias = bias is not None

    def body(*refs):
        a_ref, b_ref = refs[0], refs[1]
        bias_ref = refs[2] if has_bias else None
        outs = refs[3:] if has_bias else refs[2:]
        o_ref, acc_ref = outs[0], outs[-1]
        k = pl.program_id(2)

        @pl.when(k == 0)
        def _():
            acc_ref[...] = jnp.zeros_like(acc_ref)

        acc_ref[...] += _dot(a_ref[...], b_ref[...], dims)

        @pl.when(k == nk - 1)
        def _():
            res = acc_ref[...]
            if has_bias:
                res = res + bias_ref[...]
            o_ref[...] = res.astype(o_ref.dtype)
            if epi is not None:
                outs[1][...] = epi(res).astype(outs[1].dtype)

    in_specs = [a_spec, b_spec]
    args = [a, b]
    if has_bias:
        in_specs.append(pl.BlockSpec((1, tn), lambda i, j, k: (0, j)))
        args.append(bias)
    out_shape = [jax.ShapeDtypeStruct((M, N), out_dtype)]
    out_specs = [o_spec]
    if epi is not None:
        out_shape.append(jax.ShapeDtypeStruct((M, N), epi_dtype))
        out_specs.append(o_spec)
    res = pl.pallas_call(
        body, name=name, grid=(M // tm, N // tn, nk), in_specs=in_specs, out_specs=out_specs, out_shape=out_shape,
        scratch_shapes=[pltpu.VMEM((tm, tn), F32)],
        compiler_params=_cparams(("parallel", "parallel", "arbitrary")),
    )(*args)
    return res if epi is not None else res[0]


def _rowwise(name, fn, tiled, bcast, tiled_out, red_out, tile):
    tiled = [t if isinstance(t, tuple) else (t, t.shape[1], 0) for t in tiled]
    T = tiled[0][0].shape[0]
    n = T // tile
    nt, nb, nto = len(tiled), len(bcast), len(tiled_out)

    def body(*refs):
        t_refs, b_refs = refs[:nt], refs[nt:nt + nb]
        to_refs, ro_refs = refs[nt + nb:nt + nb + nto], refs[nt + nb + nto:]
        touts, routs = fn([r[...] for r in t_refs], [r[...] for r in b_refs])
        for r, v in zip(to_refs, touts, strict=True):
            r[...] = v.astype(r.dtype)
        if ro_refs:
            i = pl.program_id(0)

            @pl.when(i == 0)
            def _():
                for r, v in zip(ro_refs, routs, strict=True):
                    r[...] = v.astype(F32)

            @pl.when(i > 0)
            def _():
                for r, v in zip(ro_refs, routs, strict=True):
                    r[...] += v.astype(F32)

    def whole(shape):
        nd = len(shape)
        return pl.BlockSpec(tuple(shape), lambda i: (0,) * nd)

    in_specs = [pl.BlockSpec((tile, w), functools.partial(lambda i, cb: (i, cb), cb=cb)) for (_, w, cb) in tiled]
    in_specs += [whole(b.shape) for b in bcast]
    out_specs = [pl.BlockSpec((tile, c), lambda i: (i, 0)) for (c, _) in tiled_out] + [whole(s) for s in red_out]
    out_shape = [jax.ShapeDtypeStruct((T, c), dt) for (c, dt) in tiled_out] + [jax.ShapeDtypeStruct(tuple(s), F32) for s in red_out]
    res = pl.pallas_call(
        body, name=name, grid=(n,), in_specs=in_specs, out_specs=out_specs, out_shape=out_shape,
        compiler_params=_cparams(("arbitrary",)),
    )(*[t[0] for t in tiled], *bcast)
    return list(res[:nto]), list(res[nto:])


def _rowwise_vjp(name, f, tiled, bcast, cts, tile, wrt_t, wrt_b, t_dtypes=None, colsum=()):
    tiled = [t if isinstance(t, tuple) else (t, t.shape[1], 0) for t in tiled]
    npr = len(tiled)
    t_dtypes = t_dtypes or [F32] * len(wrt_t)
    groups = [c if isinstance(c, list) else [c] for c in cts]
    cts = [a_ for grp in groups for a_ in grp]

    def fn(tv, bv):
        prim, flat_ct = tv[:npr], list(tv[npr:])
        ct = []
        for grp in groups:
            parts = [flat_ct.pop(0).astype(F32) for _ in grp]
            ct.append(functools.reduce(lambda p_, q_: p_ + q_, parts))

        def g(dt_vals, db_vals):
            full_t, full_b = list(prim), list(bv)
            for i, v in zip(wrt_t, dt_vals, strict=True):
                full_t[i] = v
            for j, v in zip(wrt_b, db_vals, strict=True):
                full_b[j] = v
            return f(full_t, full_b)

        outs, pull = jax.vjp(g, [prim[i].astype(F32) for i in wrt_t], [bv[j] for j in wrt_b])
        dts, dbs = pull([c.astype(o.dtype) for c, o in zip(ct, outs, strict=True)])
        sums = [jnp.sum(dts[i].astype(F32), axis=0, keepdims=True) for i in colsum]
        return dts, list(dbs) + sums

    tiled_out = [(tiled[i][1], dt) for i, dt in zip(wrt_t, t_dtypes, strict=True)]
    red_out = [bcast[j].shape for j in wrt_b] + [(1, tiled[wrt_t[i]][1]) for i in colsum]
    dts, reds = _rowwise(name, fn, tiled + list(cts), bcast, tiled_out, red_out, tile)
    nb = len(wrt_b)
    return dts, reds[:nb], reds[nb:]


def _layer_norm(x, g, b, eps):
    mu = jnp.mean(x, axis=-1, keepdims=True)
    xc = x - mu
    var = jnp.mean(xc * xc, axis=-1, keepdims=True)
    return xc * lax.rsqrt(var + eps) * g + b


def _gelu_tanh(x):
    return 0.5 * x * (1.0 + jnp.tanh(0.7978845608028654 * (x + 0.044715 * (x * x * x))))


def _sigmoid(x):
    return 1.0 / (1.0 + jnp.exp(-x))


def _seg_modulate(tv, bv):
    (x,), (sc, sh) = tv, bv
    return [x * (1.0 + sc) + sh]


def _seg_gmlp(tv, bv):
    (z,), (g_ln, b_ln, ws, b_tg, expand) = tv, bv
    bias_full = _dot(b_tg, expand, hi=True)
    zz = _gelu_tanh(z)
    u, v = zz[:, :G_WIDTH], zz[:, G_WIDTH:]
    v = _layer_norm(v, g_ln, b_ln, LN_EPS)
    row = lax.broadcasted_iota(jnp.int32, (G_CHUNK, G_CHUNK), 0)
    col = lax.broadcasted_iota(jnp.int32, (G_CHUNK, G_CHUNK), 1)
    causal = col <= row
    first_group = lax.broadcasted_iota(jnp.int32, (G_CHUNK, 128), 1) < 64
    parts = []
    for p in range(4):
        vp = v[:, 128 * p:128 * (p + 1)]
        s_even = _dot(jnp.where(causal, ws[2 * p], 0.0), vp)
        s_odd = _dot(jnp.where(causal, ws[2 * p + 1], 0.0), vp)
        parts.append(jnp.where(first_group, s_even, s_odd))
    s = jnp.concatenate(parts, axis=1) + bias_full
    return [u * s]


def _seg_rwkv_pre(tv, bv):
    (z, prev), (mu, w0, wd, a0, wa, wg, k_k, k_a, gsum) = tv, bv
    zs = z + (prev - z) * mu
    r, k, v = zs[:, 0:512], zs[:, 512:1024], zs[:, 1024:1536]
    zl = zs[:, 1536:2048]
    x = w0 + _dot(jnp.tanh(zl), wd)
    softplus = jnp.maximum(-x, 0.0) + jnp.log(1.0 + jnp.exp(-jnp.abs(x)))
    lw = -jnp.exp(-softplus - 0.5)
    a = _sigmoid(a0 + _dot(zl, wa))
    g = _dot(_sigmoid(zl), wg)
    kk = k * k_k
    nrm = jnp.sqrt(_dot(kk * kk, gsum, hi=True))
    kk = kk / jnp.maximum(nrm, 1e-12)
    k2 = k * (1.0 + (a - 1.0) * k_a)
    return [r, lw, k2, v, -kk, kk * a, g]


def _seg_rwkv_post(tv, bv):
    (y, r, k2, v, g), (r_k, gain, bias, gsum) = tv, bv
    mu = _dot(y, gsum, hi=True) * (1.0 / R_HEAD)
    yc = y - mu
    var = _dot(yc * yc, gsum, hi=True) * (1.0 / R_HEAD)
    yn = yc * lax.rsqrt(var + GN_EPS) * gain + bias
    bonus = _dot(r * k2 * r_k, gsum, hi=True) * v
    return [(yn + bonus) * g]


def _seg_merge(tv, bv):
    (ga, gb, pa, pb), () = tv, bv
    return [_sigmoid(ga) * pa + _sigmoid(gb) * pb]


def _seg_mid(tv, bv):
    (x, mix), (gt1, g1, b1, sc2, sh2) = tv, bv
    h1 = _layer_norm(ALPHA * x + gt1 * mix, g1, b1, LN_EPS)
    return [h1, h1 * (1.0 + sc2) + sh2]


def _seg_relu2(tv, bv):
    (f1,), () = tv, bv
    return [jnp.square(jnp.maximum(f1, 0.0))]


def _seg_loss(tv, bv):
    (h1, ff, target), (gt2, g2, b2) = tv, bv
    out = _layer_norm(ALPHA * h1 + gt2 * ff, g2, b2, LN_EPS)
    err = jnp.square(out - target)
    return 0.5 * jnp.sum(jnp.mean(err, axis=-1))


_BNN = (((2,), (1,)), ((0,), (0,)))
_BNT = (((2,), (2,)), ((0,), (0,)))
_BTN = (((1,), (1,)), ((0,), (0,)))


def _scan_chunk(r, lw, k, v, a, b, s0):
    H, L, _ = r.shape
    row = lax.broadcasted_iota(jnp.int32, (H, L, L), 1)
    col = lax.broadcasted_iota(jnp.int32, (H, L, L), 2)
    incl, strict = col <= row, col < row
    cs = _dot(jnp.where(incl, 1.0, 0.0), lw, _BNN, hi=True)
    cs_end = cs[:, L - 1:L, :]
    p, p_inv = jnp.exp(cs), jnp.exp(-cs)
    at, bt, kt, rt = a * jnp.exp(cs - lw), b * p_inv, k * p_inv, r * p
    a_ab = jnp.where(strict, _dot(at, bt, _BNT), 0.0)
    a_ak = jnp.where(strict, _dot(at, kt, _BNT), 0.0)
    a_rb = jnp.where(incl, _dot(rt, bt, _BNT), 0.0)
    a_rk = jnp.where(incl, _dot(rt, kt, _BNT), 0.0)
    inv = jnp.where(row == col, 1.0, 0.0) + a_ab
    pw = a_ab
    n = 2
    while n < L:
        pw = _dot(pw, pw, _BNN)
        inv = inv + _dot(inv, pw, _BNN)
        n *= 2
    u = _dot(inv, _dot(at, s0, _BNT) + _dot(a_ak, v, _BNN), _BNN)
    y = _dot(rt, s0, _BNT) + _dot(a_rb, u, _BNN) + _dot(a_rk, v, _BNN)
    to_end = jnp.exp(cs_end - cs)
    s1 = s0 * jnp.exp(cs_end) + _dot(u, b * to_end, _BTN) + _dot(v, k * to_end, _BTN)
    return y, s1


def _scan_fwd(r, lw, k, v, a, b):
    H, T, N = r.shape
    L = SCAN_CHUNK
    nc = T // L

    def body(r_ref, lw_ref, k_ref, v_ref, a_ref, b_ref, y_ref, st_ref, s_ref):
        @pl.when(pl.program_id(0) == 0)
        def _():
            s_ref[...] = jnp.zeros_like(s_ref)

        s0 = s_ref[...]
        st_ref[0] = s0
        y, s1 = _scan_chunk(r_ref[...], lw_ref[...], k_ref[...], v_ref[...], a_ref[...], b_ref[...], s0)
        y_ref[...] = y
        s_ref[...] = s1

    blk = pl.BlockSpec((H, L, N), lambda c: (0, c, 0))
    return pl.pallas_call(
        body, name="scan_fwd", grid=(nc,), in_specs=[blk] * 6,
        out_specs=[blk, pl.BlockSpec((1, H, N, N), lambda c: (c, 0, 0, 0))],
        out_shape=[jax.ShapeDtypeStruct((H, T, N), F32), jax.ShapeDtypeStruct((nc, H, N, N), F32)],
        scratch_shapes=[pltpu.VMEM((H, N, N), F32)],
        compiler_params=_cparams(("arbitrary",)),
    )(r, lw, k, v, a, b)


def _scan_bwd(r, lw, k, v, a, b, states, dy):
    H, T, N = r.shape
    L = SCAN_CHUNK
    nc = T // L

    def body(r_ref, lw_ref, k_ref, v_ref, a_ref, b_ref, st_ref, dy_ref, dr_ref, dlw_ref, dk_ref, dv_ref, da_ref, db_ref, ds_ref):
        @pl.when(pl.program_id(0) == 0)
        def _():
            ds_ref[...] = jnp.zeros_like(ds_ref)

        args = (r_ref[...], lw_ref[...], k_ref[...], v_ref[...], a_ref[...], b_ref[...], st_ref[0])
        _, pull = jax.vjp(_scan_chunk, *args)
        dr_ref[...], dlw_ref[...], dk_ref[...], dv_ref[...], da_ref[...], db_ref[...], ds_ref[...] = pull((dy_ref[...], ds_ref[...]))

    blk = pl.BlockSpec((H, L, N), lambda c: (0, nc - 1 - c, 0))
    st_blk = pl.BlockSpec((1, H, N, N), lambda c: (nc - 1 - c, 0, 0, 0))
    return pl.pallas_call(
        body, name="scan_bwd", grid=(nc,), in_specs=[blk] * 6 + [st_blk, blk], out_specs=[blk] * 6,
        out_shape=[jax.ShapeDtypeStruct((H, T, N), F32)] * 6,
        scratch_shapes=[pltpu.VMEM((H, N, N), F32)],
        compiler_params=_cparams(("arbitrary",)),
    )(r, lw, k, v, a, b, states, dy)


def _to_heads(x):
    T = x.shape[0]
    return x.reshape(T, R_HEADS, R_HEAD).transpose(1, 0, 2)


def _from_heads(x):
    T = x.shape[1]
    return x.transpose(1, 0, 2).reshape(T, R_WIDTH)


def _place():
    x, y, c = lax.axis_index("x"), lax.axis_index("y"), lax.axis_index("c")
    return x, y, c


def _all_gather(name, block):
    R, C = block.shape

    def body(x_ref, out_ref, send_sems, recv_sems, local_sem):
        x, y, c = _place()
        me, sibling = (x, y, c), (x, y, 1 - c)
        chips = [(1 - x, y), (x, 1 - y), (1 - x, 1 - y)]

        def slot(px, py, pc):
            return out_ref.at[4 * px + 2 * py + pc]

        def copy(k, blk, to, src=None):
            return pltpu.make_async_remote_copy(
                src_ref=slot(*blk) if src is None else src, dst_ref=slot(*blk),
                send_sem=send_sems.at[k], recv_sem=recv_sems.at[k], device_id=to, device_id_type=_MESH_ID)

        mine = pltpu.make_async_copy(x_ref, slot(*me), local_sem)
        mine.start()
        first = [copy(0, me, sibling, src=x_ref)]
        first += [copy(1 + j, me, (*chip, c), src=x_ref) for j, chip in enumerate(chips)]
        for cp in first:
            cp.start()
        passed = [copy(4 + j, (*chip, c), sibling) for j, chip in enumerate(chips)]
        for j, chip in enumerate(chips):
            copy(1 + j, (*chip, c), me).wait_recv()
            passed[j].start()
        copy(0, sibling, me).wait_recv()
        for j, chip in enumerate(chips):
            copy(4 + j, (*chip, 1 - c), me).wait_recv()
        for cp in first + passed:
            cp.wait_send()
        mine.wait()

    hbm = pl.BlockSpec(memory_space=pltpu.HBM)
    return pl.pallas_call(
        body, name=name, in_specs=[hbm], out_specs=hbm,
        out_shape=jax.ShapeDtypeStruct((_N_DEV, R, C), block.dtype),
        scratch_shapes=[pltpu.SemaphoreType.DMA((7,)), pltpu.SemaphoreType.DMA((7,)), pltpu.SemaphoreType.DMA],
    )(block)


def _sibling_exchange(name, blocks):
    _, R, C = blocks.shape

    def body(x_ref, out_ref, send_sems, recv_sems):
        x, y, c = _place()
        copies = [pltpu.make_async_remote_copy(
            src_ref=x_ref.at[2 * q + (1 - c)], dst_ref=out_ref.at[q], send_sem=send_sems.at[q], recv_sem=recv_sems.at[q],
            device_id=(x, y, 1 - c), device_id_type=_MESH_ID) for q in range(4)]
        for cp in copies:
            cp.start()
        for cp in copies:
            cp.wait()

    hbm = pl.BlockSpec(memory_space=pltpu.HBM)
    return pl.pallas_call(
        body, name=name, in_specs=[hbm], out_specs=hbm, out_shape=jax.ShapeDtypeStruct((4, R, C), blocks.dtype),
        scratch_shapes=[pltpu.SemaphoreType.DMA((4,)), pltpu.SemaphoreType.DMA((4,))],
    )(blocks)


def _chip_partials(name, blocks, from_sibling, tile, out_dtype):
    _, R, C = blocks.shape

    def body(x_ref, s_ref, o_ref):
        c = lax.axis_index("c")
        for q in range(4):
            o_ref[q] = (x_ref[2 * q + c] + s_ref[q]).astype(o_ref.dtype)

    return pl.pallas_call(
        body, name=name, grid=(R // tile,),
        in_specs=[pl.BlockSpec((_N_DEV, tile, C), lambda i: (0, i, 0)), pl.BlockSpec((4, tile, C), lambda i: (0, i, 0))],
        out_specs=pl.BlockSpec((4, tile, C), lambda i: (0, i, 0)), out_shape=jax.ShapeDtypeStruct((4, R, C), out_dtype),
        compiler_params=_cparams(("parallel",)),
    )(blocks, from_sibling)


def _chip_exchange(name, partials):
    _, R, C = partials.shape

    def body(x_ref, out_ref, send_sems, recv_sems, local_sem):
        x, y, c = _place()
        my_chip = 2 * x + y
        mine = pltpu.make_async_copy(x_ref.at[my_chip], out_ref.at[my_chip], local_sem)
        mine.start()
        copies = []
        for rel in range(1, 4):
            px, py = (1 - x if (rel >> 1) & 1 else x), (1 - y if rel & 1 else y)
            copies.append(pltpu.make_async_remote_copy(
                src_ref=x_ref.at[2 * px + py], dst_ref=out_ref.at[my_chip],
                send_sem=send_sems.at[rel - 1], recv_sem=recv_sems.at[rel - 1],
                device_id=(px, py, c), device_id_type=_MESH_ID))
        for cp in copies:
            cp.start()
        for cp in copies:
            cp.wait()
        mine.wait()

    hbm = pl.BlockSpec(memory_space=pltpu.HBM)
    return pl.pallas_call(
        body, name=name, in_specs=[hbm], out_specs=hbm, out_shape=jax.ShapeDtypeStruct(partials.shape, partials.dtype),
        scratch_shapes=[pltpu.SemaphoreType.DMA((3,)), pltpu.SemaphoreType.DMA((3,)), pltpu.SemaphoreType.DMA],
    )(partials)


def _sum_leading(name, x, tile=None):
    n, R, C = x.shape
    tile = tile or _pick(R, (512, 256, 128, 64, 32, 16, 8))

    def body(x_ref, o_ref):
        acc = x_ref[0].astype(F32)
        for k in range(1, n):
            acc = acc + x_ref[k].astype(F32)
        o_ref[...] = acc

    return pl.pallas_call(
        body, name=name, grid=(R // tile,), in_specs=[pl.BlockSpec((n, tile, C), lambda i: (0, i, 0))],
        out_specs=pl.BlockSpec((tile, C), lambda i: (i, 0)), out_shape=jax.ShapeDtypeStruct((R, C), F32),
        compiler_params=_cparams(("parallel",)),
    )(x)


def _adamw(name, w, g, m, v):
    R, C = w.shape
    tile = _pick(R, (256, 128, 64, 32, 16, 8))

    def fn(tv, bv):
        w_, g_, m_, v_ = tv
        m2 = ADAM_B1 * m_ + (1.0 - ADAM_B1) * g_
        v2 = ADAM_B2 * v_ + (1.0 - ADAM_B2) * jnp.square(g_)
        m_hat = m2 / (1.0 - ADAM_B1 ** ADAM_STEP)
        v_hat = v2 / (1.0 - ADAM_B2 ** ADAM_STEP)
        delta = -ADAM_LR * (m_hat / (jnp.sqrt(v_hat) + ADAM_EPS) + ADAM_WD * w_)
        return [delta, m2, v2], []

    outs, _ = _rowwise(name, fn, [w, g, m, v], [], [(C, F32)] * 3, [], tile)
    return outs


def _pack_rows(arrs, lanes=128, row_mult=8):
    flat, places, off = [], [], 0
    for a_ in arrs:
        n = a_.size
        flat.append(a_.reshape(-1).astype(F32))
        places.append((off, n, a_.shape))
        off += n
    total = -(-off // (lanes * row_mult)) * (lanes * row_mult)
    if total > off:
        flat.append(jnp.zeros((total - off,), F32))
    return jnp.concatenate(flat).reshape(total // lanes, lanes), places


def _unpack_rows(packed, places):
    flat = packed.reshape(-1)
    return [flat[o:o + n].reshape(s) for (o, n, s) in places]


_WEIGHTS = ['w_ada', 'b_ada', 'w_in', 'b_in', 'g_ln_v', 'b_ln_v', 'w_spatial', 'b_spatial', 'mu_shift', 'w0', 'w_decay_up', 'a0',
            'w_aaa_up', 'w_gate_up', 'k_k', 'k_a', 'r_k', 'gn_gain', 'gn_bias', 'w_branch_a', 'w_branch_b', 'w_out', 'b_out',
            'ln1_g', 'ln1_b', 'w_ff1', 'b_ff1', 'w_ff2', 'b_ff2', 'ln2_g', 'ln2_b']
_BIG = {'w_ff1': (0, 512), 'w_ff2': (512, 512), 'w_in': (1024, 640), 'w_out': (1664, 128), 'w_branch_a': (1792, 64), 'w_branch_b': (1856, 64)}
_BIG_ROWS = 1920
_CUT_BY_COLS = ('w_ff1', 'w_in', 'w_branch_a', 'w_branch_b')
IN_SHARD = IN_COLS // _N_DEV
_LORA = {'w_decay_up': (0, LORA_W), 'w_aaa_up': (LORA_W, LORA_A), 'w_gate_up': (LORA_W + LORA_A, LORA_G)}
_COMM_DTYPE = jnp.bfloat16


def _pad_rows(a, rows):
    return jnp.pad(a, ((0, rows - a.shape[0]),) + ((0, 0),) * (a.ndim - 1))


def _pack_big(shards):
    blocks = []
    for n, (_, rows) in _BIG.items():
        a = shards[n].T if n in _CUT_BY_COLS else shards[n]
        blocks.append(_pad_rows(a.reshape(-1, D_MODEL), rows))
    return jnp.concatenate(blocks, axis=0)


def _unpack_big(block, like):
    out = {}
    for n, (r0, _) in _BIG.items():
        rr, cc = like[n].shape
        if n in _CUT_BY_COLS:
            out[n] = block[r0:r0 + rr * cc // D_MODEL].reshape(cc, rr).T
        else:
            out[n] = block[r0:r0 + rr]
    return out


def _to_padded(a, axis):
    g_end = 2 * G_WIDTH
    r_end = g_end + RW_USED
    take = lambda lo, hi: lax.slice_in_dim(a, lo, hi, axis=axis)
    zshape = list(a.shape)
    zshape[axis] = RW_COLS - RW_USED
    return jnp.concatenate([take(r_end, IN_COLS), take(g_end, r_end), jnp.zeros(zshape, a.dtype), take(0, g_end)], axis=axis)


def _from_padded(a, axis):
    take = lambda lo, hi: lax.slice_in_dim(a, lo, hi, axis=axis)
    return jnp.concatenate([take(2 * D_MODEL + RW_COLS, P_COLS), take(2 * D_MODEL, 2 * D_MODEL + RW_USED), take(0, 2 * D_MODEL)], axis=axis)


def _step(p, m, v, x, c, target):
    T = x.shape[0]
    xi, yi, ci = _place()
    me = 4 * xi + 2 * yi + ci
    tile = 256

    lane = jnp.arange(R_WIDTH)
    gsum = (lane[:, None] // R_HEAD == lane[None, :] // R_HEAD).astype(F32)
    expand = (jnp.arange(128)[:, None] == (lane[None, :] // (G_WIDTH // 8))).astype(F32)

    (c_act,), _ = _rowwise("silu_c", lambda tv, bv: ([tv[0] * _sigmoid(tv[0])], []), [c], [], [(D_MODEL, F32)], [], 1)
    small, places = _pack_rows([c_act, p['w_decay_up'], p['w_aaa_up'], p['w_gate_up']])
    small_all = _all_gather("gather_small", small)
    per_dev = [_unpack_rows(small_all[d], places) for d in range(_N_DEV)]
    c_act_all = _pad_rows(jnp.concatenate([pd[0] for pd in per_dev], axis=0), 16)
    lora_full = {n: jnp.concatenate([pd[i + 1] for pd in per_dev], axis=1) for i, n in enumerate(_LORA)}
    lora_pad = {n: jnp.zeros((LORA_PAD, R_WIDTH), F32).at[r0:r0 + nr].set(lora_full[n]) for n, (r0, nr) in _LORA.items()}

    big_names = list(_BIG)
    packed_all = _all_gather("gather_weights", _pack_big(p).astype(_MXU_DTYPE))

    def whole(n, rows):
        r0 = _BIG[n][0]
        return packed_all[:, r0:r0 + rows].reshape(_N_DEV * rows, D_MODEL)

    w_ff1_t, w_ff2, w_out = whole('w_ff1', 512), whole('w_ff2', 512), whole('w_out', 128)
    w_in_t = _to_padded(whole('w_in', IN_SHARD), 0)
    w_ba_t = whole('w_branch_a', 64).reshape(D_MODEL, G_WIDTH)
    w_bb_t = whole('w_branch_b', 64).reshape(D_MODEL, R_WIDTH)
    b_in_p = _to_padded(p['b_in'], 1)
    mu_p = jnp.concatenate([p['mu_shift'], jnp.zeros((1, RW_COLS - RW_USED), F32)], axis=1)

    b_ada_mine = lax.dynamic_slice(p['b_ada'], (0, me * 768), (1, 768))
    mod_cols = _mm("ada_mod", c_act_all, p['w_ada'], "nn", bias=b_ada_mine)
    mod_all = _all_gather("gather_mod", mod_cols)
    mod = lax.dynamic_index_in_dim(mod_all, me, axis=1, keepdims=False).reshape(1, 6 * D_MODEL)
    sh1, sc1, gt1, sh2, sc2, gt2 = [mod[:, i * D_MODEL:(i + 1) * D_MODEL] for i in range(6)]

    (h,), _ = _rowwise("modulate1", lambda tv, bv: (_seg_modulate(tv, bv), []), [x], [sc1, sh1], [(D_MODEL, _MXU_DTYPE)], [], tile)
    proj = _mm("in_proj", h, w_in_t, "nt", bias=b_in_p)
    ws = p['w_spatial']
    b_tg = jnp.zeros((G_CHUNK, 128), F32).at[:, :8].set(p['b_spatial'].T)
    gmlp_b = [p['g_ln_v'], p['b_ln_v'], ws, b_tg, expand]
    z_gmlp = (proj, 2 * G_WIDTH, 4)
    (ya,), _ = _rowwise("gmlp", lambda tv, bv: (_seg_gmlp(tv, bv), []), [z_gmlp], gmlp_b, [(G_WIDTH, F32)], [], G_CHUNK)
    z_rw = (proj, RW_COLS, 1)
    prev = jnp.concatenate([jnp.zeros((1, RW_COLS), F32), proj[:-1, RW_COLS:2 * RW_COLS]], axis=0)
    pre_b = [mu_p, p['w0'], lora_pad['w_decay_up'], p['a0'], lora_pad['w_aaa_up'], lora_pad['w_gate_up'], p['k_k'], p['k_a'], gsum]
    pre_out, _ = _rowwise("rwkv_pre", lambda tv, bv: (_seg_rwkv_pre(tv, bv), []), [z_rw, prev], pre_b, [(R_WIDTH, F32)] * 7, [], tile)
    r_, lw_, k2_, v_, a_, b_, g_ = pre_out
    heads = [_to_heads(t) for t in (r_, lw_, k2_, v_, a_, b_)]
    y_h, states = _scan_fwd(*heads)
    y_ = _from_heads(y_h)
    post_b = [p['r_k'].reshape(1, R_WIDTH), p['gn_gain'], p['gn_bias'], gsum]
    (yb,), _ = _rowwise("rwkv_post", lambda tv, bv: (_seg_rwkv_post(tv, bv), []), [y_, r_, k2_, v_, g_], post_b, [(R_WIDTH, F32)], [], tile)
    pa = _mm("branch_a", ya, w_ba_t, "nt")
    pb = _mm("branch_b", yb, w_bb_t, "nt")
    gates = [(proj, D_MODEL, 0), (proj, D_MODEL, 1)]
    (merged,), _ = _rowwise("merge", lambda tv, bv: (_seg_merge(tv, bv), []), gates + [pa, pb], [], [(D_MODEL, _MXU_DTYPE)], [], tile)
    mix = _mm("out_proj", merged, w_out, "nn", bias=p['b_out'])
    mid_b = [gt1, p['ln1_g'], p['ln1_b'], sc2, sh2]
    (h1, h2in), _ = _rowwise("mid", lambda tv, bv: (_seg_mid(tv, bv), []), [x, mix], mid_b, [(D_MODEL, F32), (D_MODEL, _MXU_DTYPE)], [], tile)
    f1, act = _mm("ff1", h2in, w_ff1_t, "nt", bias=p['b_ff1'], epi=lambda t: _seg_relu2([t], [])[0], epi_dtype=_MXU_DTYPE)
    ff = _mm("ff2", act, w_ff2, "nn", bias=p['b_ff2'])

    def loss_fn(tv, bv):
        h1_t, ff_t, tgt = tv
        val, grads = jax.value_and_grad(lambda a0_, a1_, b0_, b1_, b2_: _seg_loss([a0_, a1_, tgt], [b0_, b1_, b2_]), argnums=(0, 1, 2, 3, 4))(h1_t, ff_t, *bv)
        return [grads[0], grads[1]], [grads[2], grads[3], grads[4], jnp.sum(grads[1], axis=0, keepdims=True), jnp.full((1, 128), val, F32)]

    (dh1_a, dff), (d_gt2, d_ln2_g, d_ln2_b, d_b_ff2, loss_row) = _rowwise(
        "loss", loss_fn, [h1, ff, target], [gt2, p['ln2_g'], p['ln2_b']], [(D_MODEL, F32), (D_MODEL, _MXU_DTYPE)], [(1, D_MODEL)] * 4 + [(1, 128)], tile)
    loss = lax.psum(loss_row[0, 0], ("x", "y", "c"))

    g = {}
    g['ln2_g'], g['ln2_b'], g['b_ff2'] = d_ln2_g, d_ln2_b, d_b_ff2
    gw = {}
    gw['w_ff2'] = _mm("g_w_ff2", act, dff, "tn")
    dact = _mm("d_act", dff, w_ff2, "nt")
    (df1,), _, (g['b_ff1'],) = _rowwise_vjp("relu2_bwd", _seg_relu2, [f1], [], [dact], tile, [0], [], t_dtypes=[_MXU_DTYPE], colsum=[0])
    gw['w_ff1'] = _mm("g_w_ff1", df1, h2in, "tn")
    dh2in = _mm("d_h2in", df1, w_ff1_t, "nn")
    (dx_a, dmix), (d_gt1, g['ln1_g'], g['ln1_b'], d_sc2, d_sh2), (g['b_out'],) = _rowwise_vjp(
        "mid_bwd", _seg_mid, [x, mix], mid_b, [dh1_a, dh2in], tile, [0, 1], [0, 1, 2, 3, 4], t_dtypes=[F32, _MXU_DTYPE], colsum=[1])
    gw['w_out'] = _mm("g_w_out", merged, dmix, "tn")
    dmerged = _mm("d_merged", dmix, w_out, "nt")
    (dga, dgb, dpa, dpb), _, (cs_ga, cs_gb) = _rowwise_vjp(
        "merge_bwd", _seg_merge, gates + [pa, pb], [], [dmerged], tile, [0, 1, 2, 3], [], t_dtypes=[_MXU_DTYPE] * 4, colsum=[0, 1])
    gw['w_branch_a'] = _mm("g_w_branch_a", dpa, ya, "tn")
    gw['w_branch_b'] = _mm("g_w_branch_b", dpb, yb, "tn")
    dya = _mm("d_ya", dpa, w_ba_t, "nn")
    dyb = _mm("d_yb", dpb, w_bb_t, "nn")
    (dy, dr1, dk1, dv1, dg_), (d_r_k, g['gn_gain'], g['gn_bias']), _ = _rowwise_vjp(
        "rwkv_post_bwd", _seg_rwkv_post, [y_, r_, k2_, v_, g_], post_b, [dyb], tile, [0, 1, 2, 3, 4], [0, 1, 2])
    g['r_k'] = d_r_k
    dscan = _scan_bwd(*heads, states, _to_heads(dy))
    dr2, dlw, dk2, dv2, da, db = [_from_heads(t) for t in dscan]
    (dz_rw, dprev), (d_mu, g['w0'], d_wd, g['a0'], d_wa, d_wg, g['k_k'], g['k_a']), _ = _rowwise_vjp(
        "rwkv_pre_bwd", _seg_rwkv_pre, [z_rw, prev], pre_b, [[dr1, dr2], dlw, [dk1, dk2], [dv1, dv2], da, db, dg_], 128, [0, 1], [0, 1, 2, 3, 4, 5, 6, 7])
    g['mu_shift'] = d_mu[:, :RW_USED]
    for n, d_ in (('w_decay_up', d_wd), ('w_aaa_up', d_wa), ('w_gate_up', d_wg)):
        r0, nr = _LORA[n]
        g[n] = d_[r0:r0 + nr]
    dprev_next = jnp.concatenate([dprev[1:], jnp.zeros((1, RW_COLS), F32)], axis=0)
    (dz_rw_all,), (cs_rw,) = _rowwise(
        "shift_bwd", lambda tv, bv: ([tv[0] + tv[1]], [jnp.sum(tv[0] + tv[1], axis=0, keepdims=True)]), [dz_rw, dprev_next], [], [(RW_COLS, _MXU_DTYPE)], [(1, RW_COLS)], tile)
    (dz_g,), (g['g_ln_v'], g['b_ln_v'], g['w_spatial'], d_b_tg), (cs_g,) = _rowwise_vjp(
        "gmlp_bwd", _seg_gmlp, [z_gmlp], gmlp_b, [dya], G_CHUNK, [0], [0, 1, 2, 3], t_dtypes=[_MXU_DTYPE], colsum=[0])
    g['b_spatial'] = d_b_tg[:, :8].T
    dproj = jnp.concatenate([dga, dgb, dz_rw_all, dz_g], axis=1)
    g['b_in'] = _from_padded(jnp.concatenate([cs_ga, cs_gb, cs_rw, cs_g], axis=1), 1)
    gw['w_in'] = _from_padded(_mm("g_w_in", dproj, h, "tn"), 0)
    dh = _mm("d_h", dproj, w_in_t, "nn")

    def mod1_bwd(tv, bv):
        x_t, dh_t, dxa_t = tv
        (sc,) = bv
        return [dxa_t + dh_t * (1.0 + sc)], [jnp.sum(dh_t * x_t, axis=0, keepdims=True), jnp.sum(dh_t, axis=0, keepdims=True)]

    (grad_x,), (d_sc1, d_sh1) = _rowwise("modulate1_bwd", mod1_bwd, [x, dh, dx_a], [sc1], [(D_MODEL, F32)], [(1, D_MODEL)] * 2, tile)

    dmod = jnp.concatenate([d_sh1, d_sc1, d_gt1, d_sh2, d_sc2, d_gt2], axis=1).reshape(6 * D_MODEL // 128, 128)
    dmod_all = _all_gather("gather_dmod", dmod)
    g['b_ada'] = _sum_leading("sum_dmod", dmod_all).reshape(1, 6 * D_MODEL)
    dmod_mine = lax.dynamic_slice(dmod_all.reshape(_N_DEV, 6 * D_MODEL), (0, me * 768), (_N_DEV, 768))
    g_w_ada = _mm("g_w_ada", c_act_all, _pad_rows(dmod_mine, 16), "tn")

    small_names = [n for n in _WEIGHTS if n not in _BIG and n not in ('w_ada', 'b_ada')]
    packed_g, g_places = _pack_rows([g[n] for n in small_names], row_mult=256)
    small_sum = _unpack_rows(_sum_leading("sum_small", _all_gather("gather_small_grads", packed_g)), g_places)
    for n, t in zip(small_names, small_sum, strict=True):
        g[n] = t
    for n in _LORA:
        g[n] = lax.dynamic_slice(g[n], (0, me * R_HEAD), (g[n].shape[0], R_HEAD))
    g['w_ada'] = g_w_ada

    parts = []
    for n, (_, rows) in _BIG.items():
        per_dev = gw[n].reshape(_N_DEV, -1, D_MODEL)
        parts.append(jnp.pad(per_dev, ((0, 0), (0, rows - per_dev.shape[1]), (0, 0))))
    sendbuf = jnp.concatenate(parts, axis=1)
    from_sibling = _sibling_exchange("pair_exchange_grads", sendbuf)
    partials = _chip_partials("chip_partials", sendbuf, from_sibling, 128, _COMM_DTYPE)
    summed = _sum_leading("sum_big", _chip_exchange("chip_exchange_grads", partials))
    g.update(_unpack_big(summed, p))

    delta, new_m, new_v = {}, {}, {}
    own_call = ['w_ada'] + big_names
    for n in own_call:
        delta[n], new_m[n], new_v[n] = _adamw("adamw_" + n, p[n], g[n], m[n], v[n])
    rest = [n for n in _WEIGHTS if n not in own_call]
    packs = [_pack_rows([d[n] for n in rest], row_mult=256) for d in (p, g, m, v)]
    outs = _adamw("adamw_rest", *[pk[0] for pk in packs])
    for d, o in zip((delta, new_m, new_v), outs, strict=True):
        for n, t in zip(rest, _unpack_rows(o, packs[0][1]), strict=True):
            d[n] = t
    return loss, grad_x, g, delta, new_m, new_v


def kernel(x, c, w_ada, b_ada, w_in, b_in, g_ln_v, b_ln_v, w_spatial, b_spatial, mu_shift, w0, w_decay_up, a0, w_aaa_up, w_gate_up, k_k, k_a, r_k, gn_gain, gn_bias, w_branch_a, w_branch_b, w_out, b_out, ln1_g, ln1_b, w_ff1, b_ff1, w_ff2, b_ff2, ln2_g, ln2_b, loss_target, m_w_ada, m_b_ada, m_w_in, m_b_in, m_g_ln_v, m_b_ln_v, m_w_spatial, m_b_spatial, m_mu_shift, m_w0, m_w_decay_up, m_a0, m_w_aaa_up, m_w_gate_up, m_k_k, m_k_a, m_r_k, m_gn_gain, m_gn_bias, m_w_branch_a, m_w_branch_b, m_w_out, m_b_out, m_ln1_g, m_ln1_b, m_w_ff1, m_b_ff1, m_w_ff2, m_b_ff2, m_ln2_g, m_ln2_b, v_w_ada, v_b_ada, v_w_in, v_b_in, v_g_ln_v, v_b_ln_v, v_w_spatial, v_b_spatial, v_mu_shift, v_w0, v_w_decay_up, v_a0, v_w_aaa_up, v_w_gate_up, v_k_k, v_k_a, v_r_k, v_gn_gain, v_gn_bias, v_w_branch_a, v_w_branch_b, v_w_out, v_b_out, v_ln1_g, v_ln1_b, v_w_ff1, v_b_ff1, v_w_ff2, v_b_ff2, v_ln2_g, v_ln2_b):
    given = dict(locals())
    shapes = {n: given[n].shape for n in _WEIGHTS}
    def two_d(a_):
        a_ = a_[0]
        return a_.reshape(1, -1) if a_.ndim == 1 else a_
    p = {n: two_d(given[n]) for n in _WEIGHTS}
    m = {n: two_d(given["m_" + n]) for n in _WEIGHTS}
    v = {n: two_d(given["v_" + n]) for n in _WEIGHTS}
    loss, grad_x, g, delta, new_m, new_v = _step(p, m, v, x[0], c, loss_target[0])
    outs = [loss, grad_x[None]]
    for d in (g, delta, new_m, new_v):
        outs += [d[n].reshape(shapes[n]) for n in _WEIGHTS]
    return tuple(outs)
```

```python
import functools

import jax
import jax.numpy as jnp
from jax import lax
from jax.experimental import pallas as pl
from jax.experimental.pallas import tpu as pltpu

F32 = jnp.float32
_MXU_DTYPE = jnp.bfloat16
_HI = lax.Precision.HIGHEST
_VMEM_LIMIT = 48 * 1024 * 1024
_MESH_ID = pl.DeviceIdType.MESH
_N_DEV = 8

D_MODEL = 1024
G_WIDTH = 512
G_CHUNK = 128
R_WIDTH = 512
R_HEADS = 8
R_HEAD = 64
LORA_W, LORA_A, LORA_G = 32, 32, 96
D_FF = 4096
ALPHA = 2.0 ** 0.25
LN_EPS = 1e-5
GN_EPS = 64e-5
SCAN_CHUNK = 64
ADAM_LR, ADAM_B1, ADAM_B2, ADAM_EPS, ADAM_WD, ADAM_STEP = 0.001, 0.9, 0.999, 1e-08, 0.01, 10

P_COLS = 5120
RW_COLS = 2048
RW_USED = 3 * R_WIDTH + LORA_W + LORA_A + LORA_G
LORA_PAD = 512
IN_COLS = 2 * G_WIDTH + RW_USED + 2 * D_MODEL


def _cparams(sem=None, **kw):
    if sem is not None:
        kw["dimension_semantics"] = sem
    return pltpu.CompilerParams(vmem_limit_bytes=_VMEM_LIMIT, **kw)


def _dot(a, b, dims=(((1,), (0,)), ((), ())), hi=False):
    if hi:
        return lax.dot_general(a.astype(F32), b.astype(F32), dims, precision=_HI, preferred_element_type=F32)
    return lax.dot_general(a.astype(_MXU_DTYPE), b.astype(_MXU_DTYPE), dims, preferred_element_type=F32)


_NN = (((1,), (0,)), ((), ()))
_NT = (((1,), (1,)), ((), ()))
_TN = (((0,), (0,)), ((), ()))


def _pick(n, pref):
    for t in pref:
        if n % t == 0:
            return t
    return n


def _mm(name, a, b, mode, bias=None, out_dtype=F32, epi=None, epi_dtype=None, tm=None, tn=None, tk=None):
    if mode == "nn":
        (M, K), (_, N) = a.shape, b.shape
    elif mode == "nt":
        (M, K), (N, _) = a.shape, b.shape
    else:
        (K, M), (_, N) = a.shape, b.shape
    tm = tm or _pick(M, (512, 256, 128, 64, 32, 16, 8))
    tn = tn or _pick(N, (1024, 512, 640, 384, 256, 128))
    tk = tk or _pick(K, (1024, 512, 256, 128))
    nk = K // tk
    dims = {"nn": _NN, "nt": _NT, "tn": _TN}[mode]
    a_spec = pl.BlockSpec((tk, tm), lambda i, j, k: (k, i)) if mode == "tn" else pl.BlockSpec((tm, tk), lambda i, j, k: (i, k))
    b_spec = pl.BlockSpec((tn, tk), lambda i, j, k: (j, k)) if mode == "nt" else pl.BlockSpec((tk, tn), lambda i, j, k: (k, j))
    o_spec = pl.BlockSpec((tm, tn), lambda i, j, k: (i, j))
    has_bias = bias is not None

    def body(*refs):
        a_ref, b_ref = refs[0], refs[1]
        bias_ref = refs[2] if has_bias else None
        outs = refs[3:] if has_bias else refs[2:]
        o_ref, acc_ref = outs[0], outs[-1]
        k = pl.program_id(2)

        @pl.when(k == 0)
        def _():
            acc_ref[...] = jnp.zeros_like(acc_ref)

        acc_ref[...] += _dot(a_ref[...], b_ref[...], dims)

        @pl.when(k == nk - 1)
        def _():
            res = acc_ref[...]
            if has_bias:
                res = res + bias_ref[...]
            o_ref[...] = res.astype(o_ref.dtype)
            if epi is not None:
                outs[1][...] = epi(res).astype(outs[1].dtype)

    in_specs = [a_spec, b_spec]
    args = [a, b]
    if has_bias:
        in_specs.append(pl.BlockSpec((1, tn), lambda i, j, k: (0, j)))
        args.append(bias)
    out_shape = [jax.ShapeDtypeStruct((M, N), out_dtype)]
    out_specs = [o_spec]
    if epi is not None:
        out_shape.append(jax.ShapeDtypeStruct((M, N), epi_dtype))
        out_specs.append(o_spec)
    res = pl.pallas_call(
        body, name=name, grid=(M // tm, N // tn, nk), in_specs=in_specs, out_specs=out_specs, out_shape=out_shape,
        scratch_shapes=[pltpu.VMEM((tm, tn), F32)],
        compiler_params=_cparams(("parallel", "parallel", "arbitrary")),
    )(*args)
    return res if epi is not None else res[0]


def _rowwise(name, fn, tiled, bcast, tiled_out, red_out, tile):
    tiled = [t if isinstance(t, tuple) else (t, t.shape[1], 0) for t in tiled]
    T = tiled[0][0].shape[0]
    n = T // tile
    nt, nb, nto = len(tiled), len(bcast), len(tiled_out)

    def body(*refs):
        t_refs, b_refs = refs[:nt], refs[nt:nt + nb]
        to_refs, ro_refs = refs[nt + nb:nt + nb + nto], refs[nt + nb + nto:]
        touts, routs = fn([r[...] for r in t_refs], [r[...] for r in b_refs])
        for r, v in zip(to_refs, touts, strict=True):
            r[...] = v.astype(r.dtype)
        if ro_refs:
            i = pl.program_id(0)

            @pl.when(i == 0)
            def _():
                for r, v in zip(ro_refs, routs, strict=True):
                    r[...] = v.astype(F32)

            @pl.when(i > 0)
            def _():
                for r, v in zip(ro_refs, routs, strict=True):
                    r[...] += v.astype(F32)

    def whole(shape):
        nd = len(shape)
        return pl.BlockSpec(tuple(shape), lambda i: (0,) * nd)

    in_specs = [pl.BlockSpec((tile, w), functools.partial(lambda i, cb: (i, cb), cb=cb)) for (_, w, cb) in tiled]
    in_specs += [whole(b.shape) for b in bcast]
    out_specs = [pl.BlockSpec((tile, c), lambda i: (i, 0)) for (c, _) in tiled_out] + [whole(s) for s in red_out]
    out_shape = [jax.ShapeDtypeStruct((T, c), dt) for (c, dt) in tiled_out] + [jax.ShapeDtypeStruct(tuple(s), F32) for s in red_out]
    res = pl.pallas_call(
        body, name=name, grid=(n,), in_specs=in_specs, out_specs=out_specs, out_shape=out_shape,
        compiler_params=_cparams(("arbitrary",)),
    )(*[t[0] for t in tiled], *bcast)
    return list(res[:nto]), list(res[nto:])


def _rowwise_vjp(name, f, tiled, bcast, cts, tile, wrt_t, wrt_b, t_dtypes=None, colsum=()):
    tiled = [t if isinstance(t, tuple) else (t, t.shape[1], 0) for t in tiled]
    npr = len(tiled)
    t_dtypes = t_dtypes or [F32] * len(wrt_t)
    groups = [c if isinstance(c, list) else [c] for c in cts]
    cts = [a_ for grp in groups for a_ in grp]

    def fn(tv, bv):
        prim, flat_ct = tv[:npr], list(tv[npr:])
        ct = []
        for grp in groups:
            parts = [flat_ct.pop(0).astype(F32) for _ in grp]
            ct.append(functools.reduce(lambda p_, q_: p_ + q_, parts))

        def g(dt_vals, db_vals):
            full_t, full_b = list(prim), list(bv)
            for i, v in zip(wrt_t, dt_vals, strict=True):
                full_t[i] = v
            for j, v in zip(wrt_b, db_vals, strict=True):
                full_b[j] = v
            return f(full_t, full_b)

        outs, pull = jax.vjp(g, [prim[i].astype(F32) for i in wrt_t], [bv[j] for j in wrt_b])
        dts, dbs = pull([c.astype(o.dtype) for c, o in zip(ct, outs, strict=True)])
        sums = [jnp.sum(dts[i].astype(F32), axis=0, keepdims=True) for i in colsum]
        return dts, list(dbs) + sums

    tiled_out = [(tiled[i][1], dt) for i, dt in zip(wrt_t, t_dtypes, strict=True)]
    red_out = [bcast[j].shape for j in wrt_b] + [(1, tiled[wrt_t[i]][1]) for i in colsum]
    dts, reds = _rowwise(name, fn, tiled + list(cts), bcast, tiled_out, red_out, tile)
    nb = len(wrt_b)
    return dts, reds[:nb], reds[nb:]


def _layer_norm(x, g, b, eps):
    mu = jnp.mean(x, axis=-1, keepdims=True)
    xc = x - mu
    var = jnp.mean(xc * xc, axis=-1, keepdims=True)
    return xc * lax.rsqrt(var + eps) * g + b


def _gelu_tanh(x):
    return 0.5 * x * (1.0 + jnp.tanh(0.7978845608028654 * (x + 0.044715 * (x * x * x))))


def _sigmoid(x):
    return 1.0 / (1.0 + jnp.exp(-x))


def _seg_modulate(tv, bv):
    (x,), (sc, sh) = tv, bv
    return [x * (1.0 + sc) + sh]


def _seg_gmlp(tv, bv):
    (z,), (g_ln, b_ln, ws, b_tg, expand) = tv, bv
    bias_full = _dot(b_tg, expand, hi=True)
    zz = _gelu_tanh(z)
    u, v = zz[:, :G_WIDTH], zz[:, G_WIDTH:]
    v = _layer_norm(v, g_ln, b_ln, LN_EPS)
    row = lax.broadcasted_iota(jnp.int32, (G_CHUNK, G_CHUNK), 0)
    col = lax.broadcasted_iota(jnp.int32, (G_CHUNK, G_CHUNK), 1)
    causal = col <= row
    first_group = lax.broadcasted_iota(jnp.int32, (G_CHUNK, 128), 1) < 64
    parts = []
    for p in range(4):
        vp = v[:, 128 * p:128 * (p + 1)]
        s_even = _dot(jnp.where(causal, ws[2 * p], 0.0), vp)
        s_odd = _dot(jnp.where(causal, ws[2 * p + 1], 0.0), vp)
        parts.append(jnp.where(first_group, s_even, s_odd))
    s = jnp.concatenate(parts, axis=1) + bias_full
    return [u * s]


def _split2(x):
    hi = x.astype(_MXU_DTYPE)
    return hi, (x - hi.astype(F32)).astype(_MXU_DTYPE)


@jax.custom_vjp
def _group_sum(x, ones_blocks):
    hi, lo = _split2(x)
    return _dot(hi, ones_blocks) + _dot(lo, ones_blocks)


def _group_sum_fwd(x, ones_blocks):
    return _group_sum(x, ones_blocks), ones_blocks


def _group_sum_bwd(ones_blocks, ct):
    return _group_sum(ct, ones_blocks), jnp.zeros_like(ones_blocks)


_group_sum.defvjp(_group_sum_fwd, _group_sum_bwd)


def _seg_rwkv_pre(tv, bv):
    (z, prev), (mu, w0, wd, a0, wa, wg, k_k, k_a, gsum) = tv, bv
    zs = z + (prev - z) * mu
    r, k, v = zs[:, 0:512], zs[:, 512:1024], zs[:, 1024:1536]
    zl = zs[:, 1536:2048]
    x = w0 + _dot(jnp.tanh(zl), wd)
    softplus = jnp.maximum(-x, 0.0) + jnp.log(1.0 + jnp.exp(-jnp.abs(x)))
    lw = -jnp.exp(-softplus - 0.5)
    a = _sigmoid(a0 + _dot(zl, wa))
    g = _dot(_sigmoid(zl), wg)
    kk = k * k_k
    nrm = jnp.sqrt(_group_sum(kk * kk, gsum))
    kk = kk / jnp.maximum(nrm, 1e-12)
    k2 = k * (1.0 + (a - 1.0) * k_a)
    return [r, lw, k2, v, -kk, kk * a, g]


def _seg_rwkv_post(tv, bv):
    (y, r, k2, v, g), (r_k, gain, bias, gsum) = tv, bv
    mu = _group_sum(y, gsum) * (1.0 / R_HEAD)
    yc = y - mu
    var = _group_sum(yc * yc, gsum) * (1.0 / R_HEAD)
    yn = yc * lax.rsqrt(var + GN_EPS) * gain + bias
    bonus = _group_sum(r * k2 * r_k, gsum) * v
    return [(yn + bonus) * g]


def _seg_merge(tv, bv):
    (ga, gb, pa, pb), () = tv, bv
    return [_sigmoid(ga) * pa + _sigmoid(gb) * pb]


def _seg_mid(tv, bv):
    (x, mix), (gt1, g1, b1, sc2, sh2) = tv, bv
    h1 = _layer_norm(ALPHA * x + gt1 * mix, g1, b1, LN_EPS)
    return [h1, h1 * (1.0 + sc2) + sh2]


def _seg_relu2(tv, bv):
    (f1,), () = tv, bv
    return [jnp.square(jnp.maximum(f1, 0.0))]


def _seg_loss(tv, bv):
    (h1, ff, target), (gt2, g2, b2) = tv, bv
    out = _layer_norm(ALPHA * h1 + gt2 * ff, g2, b2, LN_EPS)
    err = jnp.square(out - target)
    return 0.5 * jnp.sum(jnp.mean(err, axis=-1))


_BNN = (((2,), (1,)), ((0,), (0,)))
_BNT = (((2,), (2,)), ((0,), (0,)))
_BTN = (((1,), (1,)), ((0,), (0,)))


def _tri_dot(x, dims):
    H, L, _ = x.shape
    tri = (lax.broadcasted_iota(jnp.int32, (H, L, L), 2) <= lax.broadcasted_iota(jnp.int32, (H, L, L), 1)).astype(F32)
    hi, lo = _split2(x)
    return _dot(tri, hi, dims) + _dot(tri, lo, dims)


@jax.custom_vjp
def _running_sum(x):
    return _tri_dot(x, _BNN)


_running_sum.defvjp(lambda x: (_tri_dot(x, _BNN), None), lambda _, ct: (_tri_dot(ct, _BTN),))


def _scan_chunk(r, lw, k, v, a, b, s0):
    H, L, _ = r.shape
    row = lax.broadcasted_iota(jnp.int32, (H, L, L), 1)
    col = lax.broadcasted_iota(jnp.int32, (H, L, L), 2)
    incl, strict = col <= row, col < row
    cs = _running_sum(lw)
    cs_end = cs[:, L - 1:L, :]
    p, p_inv = jnp.exp(cs), jnp.exp(-cs)
    at, bt, kt, rt = a * jnp.exp(cs - lw), b * p_inv, k * p_inv, r * p
    a_ab = jnp.where(strict, _dot(at, bt, _BNT), 0.0)
    a_ak = jnp.where(strict, _dot(at, kt, _BNT), 0.0)
    a_rb = jnp.where(incl, _dot(rt, bt, _BNT), 0.0)
    a_rk = jnp.where(incl, _dot(rt, kt, _BNT), 0.0)
    inv = jnp.where(row == col, 1.0, 0.0) + a_ab
    pw = a_ab
    n = 2
    while n < L:
        pw = _dot(pw, pw, _BNN)
        inv = inv + _dot(inv, pw, _BNN)
        n *= 2
    u = _dot(inv, _dot(at, s0, _BNT) + _dot(a_ak, v, _BNN), _BNN)
    y = _dot(rt, s0, _BNT) + _dot(a_rb, u, _BNN) + _dot(a_rk, v, _BNN)
    to_end = jnp.exp(cs_end - cs)
    s1 = s0 * jnp.exp(cs_end) + _dot(u, b * to_end, _BTN) + _dot(v, k * to_end, _BTN)
    return y, s1


def _split_heads(x):
    return jnp.stack([x[:, R_HEAD * h:R_HEAD * (h + 1)] for h in range(R_HEADS)])


def _merge_heads(x):
    return jnp.concatenate([x[h] for h in range(R_HEADS)], axis=1)


def _scan_fwd(r, lw, k, v, a, b):
    T = r.shape[0]
    H, N, L = R_HEADS, R_HEAD, SCAN_CHUNK
    nc = T // L

    def body(r_ref, lw_ref, k_ref, v_ref, a_ref, b_ref, y_ref, st_ref, s_ref):
        @pl.when(pl.program_id(0) == 0)
        def _():
            s_ref[...] = jnp.zeros_like(s_ref)

        s0 = s_ref[...]
        st_ref[0] = s0
        y, s1 = _scan_chunk(*[_split_heads(t[...]) for t in (r_ref, lw_ref, k_ref, v_ref, a_ref, b_ref)], s0)
        y_ref[...] = _merge_heads(y)
        s_ref[...] = s1

    blk = pl.BlockSpec((L, R_WIDTH), lambda c: (c, 0))
    return pl.pallas_call(
        body, name="scan_fwd", grid=(nc,), in_specs=[blk] * 6,
        out_specs=[blk, pl.BlockSpec((1, H, N, N), lambda c: (c, 0, 0, 0))],
        out_shape=[jax.ShapeDtypeStruct((T, R_WIDTH), F32), jax.ShapeDtypeStruct((nc, H, N, N), F32)],
        scratch_shapes=[pltpu.VMEM((H, N, N), F32)],
        compiler_params=_cparams(("arbitrary",)),
    )(r, lw, k, v, a, b)


def _scan_bwd(r, lw, k, v, a, b, states, dy):
    T = r.shape[0]
    H, N, L = R_HEADS, R_HEAD, SCAN_CHUNK
    nc = T // L

    def body(r_ref, lw_ref, k_ref, v_ref, a_ref, b_ref, st_ref, dy_ref, dr_ref, dlw_ref, dk_ref, dv_ref, da_ref, db_ref, ds_ref):
        @pl.when(pl.program_id(0) == 0)
        def _():
            ds_ref[...] = jnp.zeros_like(ds_ref)

        args = [_split_heads(t[...]) for t in (r_ref, lw_ref, k_ref, v_ref, a_ref, b_ref)] + [st_ref[0]]
        _, pull = jax.vjp(_scan_chunk, *args)
        grads = pull((_split_heads(dy_ref[...]), ds_ref[...]))
        for o_ref, g_ in zip((dr_ref, dlw_ref, dk_ref, dv_ref, da_ref, db_ref), grads[:6], strict=True):
            o_ref[...] = _merge_heads(g_)
        ds_ref[...] = grads[6]

    blk = pl.BlockSpec((L, R_WIDTH), lambda c: (nc - 1 - c, 0))
    st_blk = pl.BlockSpec((1, H, N, N), lambda c: (nc - 1 - c, 0, 0, 0))
    return pl.pallas_call(
        body, name="scan_bwd", grid=(nc,), in_specs=[blk] * 6 + [st_blk, blk], out_specs=[blk] * 6,
        out_shape=[jax.ShapeDtypeStruct((T, R_WIDTH), F32)] * 6,
        scratch_shapes=[pltpu.VMEM((H, N, N), F32)],
        compiler_params=_cparams(("arbitrary",)),
    )(r, lw, k, v, a, b, states, dy)


def _place():
    x, y, c = lax.axis_index("x"), lax.axis_index("y"), lax.axis_index("c")
    return x, y, c


def _all_gather(name, block):
    R, C = block.shape

    def body(x_ref, out_ref, send_sems, recv_sems, local_sem):
        x, y, c = _place()
        me, sibling = (x, y, c), (x, y, 1 - c)
        chips = [(1 - x, y), (x, 1 - y), (1 - x, 1 - y)]

        def slot(px, py, pc):
            return out_ref.at[4 * px + 2 * py + pc]

        def copy(k, blk, to, src=None):
            return pltpu.make_async_remote_copy(
                src_ref=slot(*blk) if src is None else src, dst_ref=slot(*blk),
                send_sem=send_sems.at[k], recv_sem=recv_sems.at[k], device_id=to, device_id_type=_MESH_ID)

        mine = pltpu.make_async_copy(x_ref, slot(*me), local_sem)
        mine.start()
        first = [copy(0, me, sibling, src=x_ref)]
        first += [copy(1 + j, me, (*chip, c), src=x_ref) for j, chip in enumerate(chips)]
        for cp in first:
            cp.start()
        passed = [copy(4 + j, (*chip, c), sibling) for j, chip in enumerate(chips)]
        for j, chip in enumerate(chips):
            copy(1 + j, (*chip, c), me).wait_recv()
            passed[j].start()
        copy(0, sibling, me).wait_recv()
        for j, chip in enumerate(chips):
            copy(4 + j, (*chip, 1 - c), me).wait_recv()
        for cp in first + passed:
            cp.wait_send()
        mine.wait()

    hbm = pl.BlockSpec(memory_space=pltpu.HBM)
    return pl.pallas_call(
        body, name=name, in_specs=[hbm], out_specs=hbm,
        out_shape=jax.ShapeDtypeStruct((_N_DEV, R, C), block.dtype),
        scratch_shapes=[pltpu.SemaphoreType.DMA((7,)), pltpu.SemaphoreType.DMA((7,)), pltpu.SemaphoreType.DMA],
    )(block)


def _sibling_exchange(name, blocks):
    _, R, C = blocks.shape

    def body(x_ref, out_ref, send_sems, recv_sems):
        x, y, c = _place()
        copies = [pltpu.make_async_remote_copy(
            src_ref=x_ref.at[2 * q + (1 - c)], dst_ref=out_ref.at[q], send_sem=send_sems.at[q], recv_sem=recv_sems.at[q],
            device_id=(x, y, 1 - c), device_id_type=_MESH_ID) for q in range(4)]
        for cp in copies:
            cp.start()
        for cp in copies:
            cp.wait()

    hbm = pl.BlockSpec(memory_space=pltpu.HBM)
    return pl.pallas_call(
        body, name=name, in_specs=[hbm], out_specs=hbm, out_shape=jax.ShapeDtypeStruct((4, R, C), blocks.dtype),
        scratch_shapes=[pltpu.SemaphoreType.DMA((4,)), pltpu.SemaphoreType.DMA((4,))],
    )(blocks)


def _chip_partials(name, blocks, from_sibling, tile, out_dtype):
    _, R, C = blocks.shape

    def body(x_ref, s_ref, o_ref):
        c = lax.axis_index("c")
        for q in range(4):
            o_ref[q] = (x_ref[2 * q + c] + s_ref[q]).astype(o_ref.dtype)

    return pl.pallas_call(
        body, name=name, grid=(R // tile,),
        in_specs=[pl.BlockSpec((_N_DEV, tile, C), lambda i: (0, i, 0)), pl.BlockSpec((4, tile, C), lambda i: (0, i, 0))],
        out_specs=pl.BlockSpec((4, tile, C), lambda i: (0, i, 0)), out_shape=jax.ShapeDtypeStruct((4, R, C), out_dtype),
        compiler_params=_cparams(("parallel",)),
    )(blocks, from_sibling)


def _chip_exchange(name, partials):
    _, R, C = partials.shape

    def body(x_ref, out_ref, send_sems, recv_sems, local_sem):
        x, y, c = _place()
        my_chip = 2 * x + y
        mine = pltpu.make_async_copy(x_ref.at[my_chip], out_ref.at[my_chip], local_sem)
        mine.start()
        copies = []
        for rel in range(1, 4):
            px, py = (1 - x if (rel >> 1) & 1 else x), (1 - y if rel & 1 else y)
            copies.append(pltpu.make_async_remote_copy(
                src_ref=x_ref.at[2 * px + py], dst_ref=out_ref.at[my_chip],
                send_sem=send_sems.at[rel - 1], recv_sem=recv_sems.at[rel - 1],
                device_id=(px, py, c), device_id_type=_MESH_ID))
        for cp in copies:
            cp.start()
        for cp in copies:
            cp.wait()
        mine.wait()

    hbm = pl.BlockSpec(memory_space=pltpu.HBM)
    return pl.pallas_call(
        body, name=name, in_specs=[hbm], out_specs=hbm, out_shape=jax.ShapeDtypeStruct(partials.shape, partials.dtype),
        scratch_shapes=[pltpu.SemaphoreType.DMA((3,)), pltpu.SemaphoreType.DMA((3,)), pltpu.SemaphoreType.DMA],
    )(partials)


def _sum_leading(name, x, tile=None):
    n, R, C = x.shape
    tile = tile or _pick(R, (512, 256, 128, 64, 32, 16, 8))

    def body(x_ref, o_ref):
        acc = x_ref[0].astype(F32)
        for k in range(1, n):
            acc = acc + x_ref[k].astype(F32)
        o_ref[...] = acc

    return pl.pallas_call(
        body, name=name, grid=(R // tile,), in_specs=[pl.BlockSpec((n, tile, C), lambda i: (0, i, 0))],
        out_specs=pl.BlockSpec((tile, C), lambda i: (i, 0)), out_shape=jax.ShapeDtypeStruct((R, C), F32),
        compiler_params=_cparams(("parallel",)),
    )(x)


def _adamw_update(w_, g_, m_, v_):
    m2 = ADAM_B1 * m_ + (1.0 - ADAM_B1) * g_
    v2 = ADAM_B2 * v_ + (1.0 - ADAM_B2) * jnp.square(g_)
    m_hat = m2 / (1.0 - ADAM_B1 ** ADAM_STEP)
    v_hat = v2 / (1.0 - ADAM_B2 ** ADAM_STEP)
    delta = -ADAM_LR * (m_hat / (jnp.sqrt(v_hat) + ADAM_EPS) + ADAM_WD * w_)
    return delta, m2, v2


def _adamw(name, w, g, m, v):
    R, C = w.shape
    tile = _pick(R, (256, 128, 64, 32, 16, 8))
    outs, _ = _rowwise(name, lambda tv, bv: (list(_adamw_update(*tv)), []), [w, g, m, v], [], [(C, F32)] * 3, [], tile)
    return outs


def _adamw_many(name, ws, gs, ms, vs):
    n = len(ws)

    def body(*refs):
        ins, outs = refs[:4 * n], refs[4 * n:]
        for i in range(n):
            res = _adamw_update(ins[i][...], ins[n + i][...], ins[2 * n + i][...], ins[3 * n + i][...])
            for j in range(3):
                outs[j * n + i][...] = res[j]

    out_shape = [jax.ShapeDtypeStruct(w.shape, F32) for w in ws] * 3
    res = pl.pallas_call(body, name=name, out_shape=out_shape, compiler_params=_cparams())(*ws, *gs, *ms, *vs)
    return res[:n], res[n:2 * n], res[2 * n:]


def _pack_rows(arrs, lanes=128, row_mult=8):
    flat, places, off = [], [], 0
    for a_ in arrs:
        n = a_.size
        flat.append(a_.reshape(-1).astype(F32))
        places.append((off, n, a_.shape))
        off += n
    total = -(-off // (lanes * row_mult)) * (lanes * row_mult)
    if total > off:
        flat.append(jnp.zeros((total - off,), F32))
    return jnp.concatenate(flat).reshape(total // lanes, lanes), places


def _unpack_rows(packed, places):
    flat = packed.reshape(-1)
    return [flat[o:o + n].reshape(s) for (o, n, s) in places]


_WEIGHTS = ['w_ada', 'b_ada', 'w_in', 'b_in', 'g_ln_v', 'b_ln_v', 'w_spatial', 'b_spatial', 'mu_shift', 'w0', 'w_decay_up', 'a0',
            'w_aaa_up', 'w_gate_up', 'k_k', 'k_a', 'r_k', 'gn_gain', 'gn_bias', 'w_branch_a', 'w_branch_b', 'w_out', 'b_out',
            'ln1_g', 'ln1_b', 'w_ff1', 'b_ff1', 'w_ff2', 'b_ff2', 'ln2_g', 'ln2_b']
_BIG = {'w_ff1': (0, 512), 'w_ff2': (512, 512), 'w_in': (1024, 640), 'w_out': (1664, 128), 'w_branch_a': (1792, 64), 'w_branch_b': (1856, 64)}
_BIG_ROWS = 1920
_CUT_BY_COLS = ('w_ff1', 'w_in', 'w_branch_a', 'w_branch_b')
IN_SHARD = IN_COLS // _N_DEV
_LORA = {'w_decay_up': (0, LORA_W), 'w_aaa_up': (LORA_W, LORA_A), 'w_gate_up': (LORA_W + LORA_A, LORA_G)}
_COMM_DTYPE = jnp.bfloat16


def _pad_rows(a, rows):
    return jnp.pad(a, ((0, rows - a.shape[0]),) + ((0, 0),) * (a.ndim - 1))


def _pack_big(shards):
    blocks = []
    for n, (_, rows) in _BIG.items():
        a = shards[n].T if n in _CUT_BY_COLS else shards[n]
        blocks.append(_pad_rows(a.reshape(-1, D_MODEL), rows))
    return jnp.concatenate(blocks, axis=0)


def _unpack_big(block, like):
    out = {}
    for n, (r0, _) in _BIG.items():
        rr, cc = like[n].shape
        if n in _CUT_BY_COLS:
            out[n] = block[r0:r0 + rr * cc // D_MODEL].reshape(cc, rr).T
        else:
            out[n] = block[r0:r0 + rr]
    return out


def _to_padded(a, axis):
    g_end = 2 * G_WIDTH
    r_end = g_end + RW_USED
    take = lambda lo, hi: lax.slice_in_dim(a, lo, hi, axis=axis)
    zshape = list(a.shape)
    zshape[axis] = RW_COLS - RW_USED
    return jnp.concatenate([take(r_end, IN_COLS), take(g_end, r_end), jnp.zeros(zshape, a.dtype), take(0, g_end)], axis=axis)


def _from_padded(a, axis):
    take = lambda lo, hi: lax.slice_in_dim(a, lo, hi, axis=axis)
    return jnp.concatenate([take(2 * D_MODEL + RW_COLS, P_COLS), take(2 * D_MODEL, 2 * D_MODEL + RW_USED), take(0, 2 * D_MODEL)], axis=axis)


def _step(p, m, v, x, c, target):
    T = x.shape[0]
    xi, yi, ci = _place()
    me = 4 * xi + 2 * yi + ci
    tile = 256

    lane = jnp.arange(R_WIDTH)
    gsum = (lane[:, None] // R_HEAD == lane[None, :] // R_HEAD).astype(F32)
    expand = (jnp.arange(128)[:, None] == (lane[None, :] // (G_WIDTH // 8))).astype(F32)

    (c_act,), _ = _rowwise("silu_c", lambda tv, bv: ([tv[0] * _sigmoid(tv[0])], []), [c], [], [(D_MODEL, F32)], [], 1)
    small, places = _pack_rows([c_act, p['w_decay_up'], p['w_aaa_up'], p['w_gate_up']])
    small_all = _all_gather("gather_small", small)
    per_dev = [_unpack_rows(small_all[d], places) for d in range(_N_DEV)]
    c_act_all = _pad_rows(jnp.concatenate([pd[0] for pd in per_dev], axis=0), 16)
    lora_full = {n: jnp.concatenate([pd[i + 1] for pd in per_dev], axis=1) for i, n in enumerate(_LORA)}
    lora_pad = {n: jnp.zeros((LORA_PAD, R_WIDTH), F32).at[r0:r0 + nr].set(lora_full[n]) for n, (r0, nr) in _LORA.items()}

    big_names = list(_BIG)
    packed_all = _all_gather("gather_weights", _pack_big(p).astype(_MXU_DTYPE))

    def whole(n, rows):
        r0 = _BIG[n][0]
        return packed_all[:, r0:r0 + rows].reshape(_N_DEV * rows, D_MODEL)

    w_ff1_t, w_ff2, w_out = whole('w_ff1', 512), whole('w_ff2', 512), whole('w_out', 128)
    w_in_t = _to_padded(whole('w_in', IN_SHARD), 0)
    w_ba_t = whole('w_branch_a', 64).reshape(D_MODEL, G_WIDTH)
    w_bb_t = whole('w_branch_b', 64).reshape(D_MODEL, R_WIDTH)
    b_in_p = _to_padded(p['b_in'], 1)
    mu_p = jnp.concatenate([p['mu_shift'], jnp.zeros((1, RW_COLS - RW_USED), F32)], axis=1)

    b_ada_mine = lax.dynamic_slice(p['b_ada'], (0, me * 768), (1, 768))
    mod_cols = _mm("ada_mod", c_act_all, p['w_ada'], "nn", bias=b_ada_mine)
    mod_all = _all_gather("gather_mod", mod_cols)
    mod = lax.dynamic_index_in_dim(mod_all, me, axis=1, keepdims=False).reshape(1, 6 * D_MODEL)
    sh1, sc1, gt1, sh2, sc2, gt2 = [mod[:, i * D_MODEL:(i + 1) * D_MODEL] for i in range(6)]

    (h,), _ = _rowwise("modulate1", lambda tv, bv: (_seg_modulate(tv, bv), []), [x], [sc1, sh1], [(D_MODEL, _MXU_DTYPE)], [], tile)
    proj = _mm("in_proj", h, w_in_t, "nt", bias=b_in_p)
    ws = p['w_spatial']
    b_tg = jnp.zeros((G_CHUNK, 128), F32).at[:, :8].set(p['b_spatial'].T)
    gmlp_b = [p['g_ln_v'], p['b_ln_v'], ws, b_tg, expand]
    z_gmlp = (proj, 2 * G_WIDTH, 4)
    (ya,), _ = _rowwise("gmlp", lambda tv, bv: (_seg_gmlp(tv, bv), []), [z_gmlp], gmlp_b, [(G_WIDTH, F32)], [], G_CHUNK)
    z_rw = (proj, RW_COLS, 1)
    prev = jnp.concatenate([jnp.zeros((1, RW_COLS), F32), proj[:-1, RW_COLS:2 * RW_COLS]], axis=0)
    pre_b = [mu_p, p['w0'], lora_pad['w_decay_up'], p['a0'], lora_pad['w_aaa_up'], lora_pad['w_gate_up'], p['k_k'], p['k_a'], gsum]
    pre_out, _ = _rowwise("rwkv_pre", lambda tv, bv: (_seg_rwkv_pre(tv, bv), []), [z_rw, prev], pre_b, [(R_WIDTH, F32)] * 7, [], tile)
    r_, lw_, k2_, v_, a_, b_, g_ = pre_out
    scan_in = (r_, lw_, k2_, v_, a_, b_)
    y_, states = _scan_fwd(*scan_in)
    post_b = [p['r_k'].reshape(1, R_WIDTH), p['gn_gain'], p['gn_bias'], gsum]
    (yb,), _ = _rowwise("rwkv_post", lambda tv, bv: (_seg_rwkv_post(tv, bv), []), [y_, r_, k2_, v_, g_], post_b, [(R_WIDTH, F32)], [], tile)
    pa = _mm("branch_a", ya, w_ba_t, "nt")
    pb = _mm("branch_b", yb, w_bb_t, "nt")
    gates = [(proj, D_MODEL, 0), (proj, D_MODEL, 1)]
    (merged,), _ = _rowwise("merge", lambda tv, bv: (_seg_merge(tv, bv), []), gates + [pa, pb], [], [(D_MODEL, _MXU_DTYPE)], [], tile)
    mix = _mm("out_proj", merged, w_out, "nn", bias=p['b_out'])
    mid_b = [gt1, p['ln1_g'], p['ln1_b'], sc2, sh2]
    (h1, h2in), _ = _rowwise("mid", lambda tv, bv: (_seg_mid(tv, bv), []), [x, mix], mid_b, [(D_MODEL, F32), (D_MODEL, _MXU_DTYPE)], [], tile)
    f1, act = _mm("ff1", h2in, w_ff1_t, "nt", bias=p['b_ff1'], epi=lambda t: _seg_relu2([t], [])[0], epi_dtype=_MXU_DTYPE)
    ff = _mm("ff2", act, w_ff2, "nn", bias=p['b_ff2'])

    def loss_fn(tv, bv):
        h1_t, ff_t, tgt = tv
        val, grads = jax.value_and_grad(lambda a0_, a1_, b0_, b1_, b2_: _seg_loss([a0_, a1_, tgt], [b0_, b1_, b2_]), argnums=(0, 1, 2, 3, 4))(h1_t, ff_t, *bv)
        return [grads[0], grads[1]], [grads[2], grads[3], grads[4], jnp.sum(grads[1], axis=0, keepdims=True), jnp.full((1, 128), val, F32)]

    (dh1_a, dff), (d_gt2, d_ln2_g, d_ln2_b, d_b_ff2, loss_row) = _rowwise(
        "loss", loss_fn, [h1, ff, target], [gt2, p['ln2_g'], p['ln2_b']], [(D_MODEL, F32), (D_MODEL, _MXU_DTYPE)], [(1, D_MODEL)] * 4 + [(1, 128)], tile)
    loss = lax.psum(loss_row[0, 0], ("x", "y", "c"))

    g = {}
    g['ln2_g'], g['ln2_b'], g['b_ff2'] = d_ln2_g, d_ln2_b, d_b_ff2
    gw = {}
    gw['w_ff2'] = _mm("g_w_ff2", act, dff, "tn")
    dact = _mm("d_act", dff, w_ff2, "nt")
    (df1,), _, (g['b_ff1'],) = _rowwise_vjp("relu2_bwd", _seg_relu2, [f1], [], [dact], tile, [0], [], t_dtypes=[_MXU_DTYPE], colsum=[0])
    gw['w_ff1'] = _mm("g_w_ff1", df1, h2in, "tn")
    dh2in = _mm("d_h2in", df1, w_ff1_t, "nn")
    (dx_a, dmix), (d_gt1, g['ln1_g'], g['ln1_b'], d_sc2, d_sh2), (g['b_out'],) = _rowwise_vjp(
        "mid_bwd", _seg_mid, [x, mix], mid_b, [dh1_a, dh2in], tile, [0, 1], [0, 1, 2, 3, 4], t_dtypes=[F32, _MXU_DTYPE], colsum=[1])
    gw['w_out'] = _mm("g_w_out", merged, dmix, "tn")
    dmerged = _mm("d_merged", dmix, w_out, "nt")
    (dga, dgb, dpa, dpb), _, (cs_ga, cs_gb) = _rowwise_vjp(
        "merge_bwd", _seg_merge, gates + [pa, pb], [], [dmerged], tile, [0, 1, 2, 3], [], t_dtypes=[_MXU_DTYPE] * 4, colsum=[0, 1])
    gw['w_branch_a'] = _mm("g_w_branch_a", dpa, ya, "tn")
    gw['w_branch_b'] = _mm("g_w_branch_b", dpb, yb, "tn")
    dya = _mm("d_ya", dpa, w_ba_t, "nn")
    dyb = _mm("d_yb", dpb, w_bb_t, "nn")
    (dy, dr1, dk1, dv1, dg_), (d_r_k, g['gn_gain'], g['gn_bias']), _ = _rowwise_vjp(
        "rwkv_post_bwd", _seg_rwkv_post, [y_, r_, k2_, v_, g_], post_b, [dyb], tile, [0, 1, 2, 3, 4], [0, 1, 2])
    g['r_k'] = d_r_k
    dr2, dlw, dk2, dv2, da, db = _scan_bwd(*scan_in, states, dy)
    (dz_rw, dprev), (d_mu, g['w0'], d_wd, g['a0'], d_wa, d_wg, g['k_k'], g['k_a']), _ = _rowwise_vjp(
        "rwkv_pre_bwd", _seg_rwkv_pre, [z_rw, prev], pre_b, [[dr1, dr2], dlw, [dk1, dk2], [dv1, dv2], da, db, dg_], 128, [0, 1], [0, 1, 2, 3, 4, 5, 6, 7])
    g['mu_shift'] = d_mu[:, :RW_USED]
    for n, d_ in (('w_decay_up', d_wd), ('w_aaa_up', d_wa), ('w_gate_up', d_wg)):
        r0, nr = _LORA[n]
        g[n] = d_[r0:r0 + nr]
    dprev_next = jnp.concatenate([dprev[1:], jnp.zeros((1, RW_COLS), F32)], axis=0)
    (dz_rw_all,), (cs_rw,) = _rowwise(
        "shift_bwd", lambda tv, bv: ([tv[0] + tv[1]], [jnp.sum(tv[0] + tv[1], axis=0, keepdims=True)]), [dz_rw, dprev_next], [], [(RW_COLS, _MXU_DTYPE)], [(1, RW_COLS)], tile)
    (dz_g,), (g['g_ln_v'], g['b_ln_v'], g['w_spatial'], d_b_tg), (cs_g,) = _rowwise_vjp(
        "gmlp_bwd", _seg_gmlp, [z_gmlp], gmlp_b, [dya], G_CHUNK, [0], [0, 1, 2, 3], t_dtypes=[_MXU_DTYPE], colsum=[0])
    g['b_spatial'] = d_b_tg[:, :8].T
    dproj = jnp.concatenate([dga, dgb, dz_rw_all, dz_g], axis=1)
    g['b_in'] = _from_padded(jnp.concatenate([cs_ga, cs_gb, cs_rw, cs_g], axis=1), 1)
    gw['w_in'] = _from_padded(_mm("g_w_in", dproj, h, "tn"), 0)
    dh = _mm("d_h", dproj, w_in_t, "nn")

    def mod1_bwd(tv, bv):
        x_t, dh_t, dxa_t = tv
        (sc,) = bv
        return [dxa_t + dh_t * (1.0 + sc)], [jnp.sum(dh_t * x_t, axis=0, keepdims=True), jnp.sum(dh_t, axis=0, keepdims=True)]

    (grad_x,), (d_sc1, d_sh1) = _rowwise("modulate1_bwd", mod1_bwd, [x, dh, dx_a], [sc1], [(D_MODEL, F32)], [(1, D_MODEL)] * 2, tile)

    dmod = jnp.concatenate([d_sh1, d_sc1, d_gt1, d_sh2, d_sc2, d_gt2], axis=1).reshape(6 * D_MODEL // 128, 128)
    dmod_all = _all_gather("gather_dmod", dmod)
    g['b_ada'] = _sum_leading("sum_dmod", dmod_all).reshape(1, 6 * D_MODEL)
    dmod_mine = lax.dynamic_slice(dmod_all.reshape(_N_DEV, 6 * D_MODEL), (0, me * 768), (_N_DEV, 768))
    g_w_ada = _mm("g_w_ada", c_act_all, _pad_rows(dmod_mine, 16), "tn")

    small_names = [n for n in _WEIGHTS if n not in _BIG and n not in ('w_ada', 'b_ada')]
    packed_g, g_places = _pack_rows([g[n] for n in small_names], row_mult=256)
    small_sum = _unpack_rows(_sum_leading("sum_small", _all_gather("gather_small_grads", packed_g)), g_places)
    for n, t in zip(small_names, small_sum, strict=True):
        g[n] = t
    for n in _LORA:
        g[n] = lax.dynamic_slice(g[n], (0, me * R_HEAD), (g[n].shape[0], R_HEAD))
    g['w_ada'] = g_w_ada

    parts = []
    for n, (_, rows) in _BIG.items():
        per_dev = gw[n].reshape(_N_DEV, -1, D_MODEL)
        parts.append(jnp.pad(per_dev, ((0, 0), (0, rows - per_dev.shape[1]), (0, 0))))
    sendbuf = jnp.concatenate(parts, axis=1)
    from_sibling = _sibling_exchange("pair_exchange_grads", sendbuf)
    partials = _chip_partials("chip_partials", sendbuf, from_sibling, 128, _COMM_DTYPE)
    summed = _sum_leading("sum_big", _chip_exchange("chip_exchange_grads", partials))
    g.update(_unpack_big(summed, p))

    delta, new_m, new_v = {}, {}, {}
    own_call = ['w_ada'] + big_names
    for n in own_call:
        delta[n], new_m[n], new_v[n] = _adamw("adamw_" + n, p[n], g[n], m[n], v[n])
    rest = [n for n in _WEIGHTS if n not in own_call]
    outs = _adamw_many("adamw_rest", *[[d[n].reshape(p[n].shape) for n in rest] for d in (p, g, m, v)])
    for d, o in zip((delta, new_m, new_v), outs, strict=True):
        d.update(zip(rest, o, strict=True))
    return loss, grad_x, g, delta, new_m, new_v


def kernel(x, c, w_ada, b_ada, w_in, b_in, g_ln_v, b_ln_v, w_spatial, b_spatial, mu_shift, w0, w_decay_up, a0, w_aaa_up, w_gate_up, k_k, k_a, r_k, gn_gain, gn_bias, w_branch_a, w_branch_b, w_out, b_out, ln1_g, ln1_b, w_ff1, b_ff1, w_ff2, b_ff2, ln2_g, ln2_b, loss_target, m_w_ada, m_b_ada, m_w_in, m_b_in, m_g_ln_v, m_b_ln_v, m_w_spatial, m_b_spatial, m_mu_shift, m_w0, m_w_decay_up, m_a0, m_w_aaa_up, m_w_gate_up, m_k_k, m_k_a, m_r_k, m_gn_gain, m_gn_bias, m_w_branch_a, m_w_branch_b, m_w_out, m_b_out, m_ln1_g, m_ln1_b, m_w_ff1, m_b_ff1, m_w_ff2, m_b_ff2, m_ln2_g, m_ln2_b, v_w_ada, v_b_ada, v_w_in, v_b_in, v_g_ln_v, v_b_ln_v, v_w_spatial, v_b_spatial, v_mu_shift, v_w0, v_w_decay_up, v_a0, v_w_aaa_up, v_w_gate_up, v_k_k, v_k_a, v_r_k, v_gn_gain, v_gn_bias, v_w_branch_a, v_w_branch_b, v_w_out, v_b_out, v_ln1_g, v_ln1_b, v_w_ff1, v_b_ff1, v_w_ff2, v_b_ff2, v_ln2_g, v_ln2_b):
    given = dict(locals())
    shapes = {n: given[n].shape for n in _WEIGHTS}
    def two_d(a_):
        a_ = a_[0]
        return a_.reshape(1, -1) if a_.ndim == 1 else a_
    p = {n: two_d(given[n]) for n in _WEIGHTS}
    m = {n: two_d(given["m_" + n]) for n in _WEIGHTS}
    v = {n: two_d(given["v_" + n]) for n in _WEIGHTS}
    loss, grad_x, g, delta, new_m, new_v = _step(p, m, v, x[0], c, loss_target[0])
    outs = [loss, grad_x[None]]
    for d in (g, delta, new_m, new_v):
        outs += [d[n].reshape(shapes[n]) for n in _WEIGHTS]
    return tuple(outs)
```

```python
import functools

import jax
import jax.numpy as jnp
from jax import lax
from jax.experimental import pallas as pl
from jax.experimental.pallas import tpu as pltpu

F32 = jnp.float32
_MXU_DTYPE = jnp.bfloat16
_HI = lax.Precision.HIGHEST
_VMEM_LIMIT = 48 * 1024 * 1024
_MESH_ID = pl.DeviceIdType.MESH
_N_DEV = 8

D_MODEL = 1024
G_WIDTH = 512
G_CHUNK = 128
R_WIDTH = 512
R_HEADS = 8
R_HEAD = 64
LORA_W, LORA_A, LORA_G = 32, 32, 96
D_FF = 4096
ALPHA = 2.0 ** 0.25
LN_EPS = 1e-5
GN_EPS = 64e-5
SCAN_CHUNK = 64
ADAM_LR, ADAM_B1, ADAM_B2, ADAM_EPS, ADAM_WD, ADAM_STEP = 0.001, 0.9, 0.999, 1e-08, 0.01, 10

P_COLS = 5120
RW_COLS = 2048
RW_USED = 3 * R_WIDTH + LORA_W + LORA_A + LORA_G
LORA_PAD = 512
IN_COLS = 2 * G_WIDTH + RW_USED + 2 * D_MODEL


def _cparams(sem=None, **kw):
    if sem is not None:
        kw["dimension_semantics"] = sem
    return pltpu.CompilerParams(vmem_limit_bytes=_VMEM_LIMIT, **kw)


def _dot(a, b, dims=(((1,), (0,)), ((), ())), hi=False):
    if hi:
        return lax.dot_general(a.astype(F32), b.astype(F32), dims, precision=_HI, preferred_element_type=F32)
    return lax.dot_general(a.astype(_MXU_DTYPE), b.astype(_MXU_DTYPE), dims, preferred_element_type=F32)


_NN = (((1,), (0,)), ((), ()))
_NT = (((1,), (1,)), ((), ()))
_TN = (((0,), (0,)), ((), ()))


def _pick(n, pref):
    for t in pref:
        if n % t == 0:
            return t
    return n


def _mm(name, a, b, mode, bias=None, out_dtype=F32, epi=None, epi_dtype=None, beside=None, tm=None, tn=None, tk=None):
    if mode == "nn":
        (M, K), (_, N) = a.shape, b.shape
    elif mode == "nt":
        (M, K), (N, _) = a.shape, b.shape
    else:
        (K, M), (_, N) = a.shape, b.shape
    tm = tm or _pick(M, (512, 256, 128, 64, 32, 16, 8))
    tn = tn or _pick(N, (1024, 512, 640, 384, 256, 128))
    tk = tk or _pick(K, (1024, 512, 256, 128))
    nk = K // tk
    dims = {"nn": _NN, "nt": _NT, "tn": _TN}[mode]
    a_spec = pl.BlockSpec((tk, tm), lambda i, j, k: (k, i)) if mode == "tn" else pl.BlockSpec((tm, tk), lambda i, j, k: (i, k))
    b_spec = pl.BlockSpec((tn, tk), lambda i, j, k: (j, k)) if mode == "nt" else pl.BlockSpec((tk, tn), lambda i, j, k: (k, j))
    o_spec = pl.BlockSpec((tm, tn), lambda i, j, k: (i, j))
    has_bias, has_beside = bias is not None, beside is not None

    def body(*refs):
        a_ref, b_ref = refs[0], refs[1]
        n_in = 2 + has_bias + has_beside
        bias_ref = refs[2] if has_bias else None
        beside_ref = refs[n_in - 1] if has_beside else None
        outs = refs[n_in:]
        o_ref, acc_ref = outs[0], outs[-1]
        k = pl.program_id(2)

        @pl.when(k == 0)
        def _():
            acc_ref[...] = jnp.zeros_like(acc_ref)

        acc_ref[...] += _dot(a_ref[...], b_ref[...], dims)

        @pl.when(k == nk - 1)
        def _():
            res = acc_ref[...]
            if has_bias:
                res = res + bias_ref[...]
            if has_beside:
                o_ref[...] = epi(res, beside_ref[...]).astype(o_ref.dtype)
            else:
                o_ref[...] = res.astype(o_ref.dtype)
                if epi is not None:
                    outs[1][...] = epi(res).astype(outs[1].dtype)

    in_specs = [a_spec, b_spec]
    args = [a, b]
    if has_bias:
        in_specs.append(pl.BlockSpec((1, tn), lambda i, j, k: (0, j)))
        args.append(bias)
    if has_beside:
        in_specs.append(o_spec)
        args.append(beside)
    two = epi is not None and not has_beside
    out_shape = [jax.ShapeDtypeStruct((M, N), epi_dtype if has_beside else out_dtype)]
    out_specs = [o_spec]
    if two:
        out_shape.append(jax.ShapeDtypeStruct((M, N), epi_dtype))
        out_specs.append(o_spec)
    res = pl.pallas_call(
        body, name=name, grid=(M // tm, N // tn, nk), in_specs=in_specs, out_specs=out_specs, out_shape=out_shape,
        scratch_shapes=[pltpu.VMEM((tm, tn), F32)],
        compiler_params=_cparams(("parallel", "parallel", "arbitrary")),
    )(*args)
    return res if two else res[0]


_HALO = 8


def _rowwise(name, fn, tiled, bcast, tiled_out, red_out, tile, reverse=False, scratch=()):
    tiled = [t if isinstance(t, tuple) else (t, t.shape[1], 0) for t in tiled]
    T = next(t[0] for t in tiled if not isinstance(t[0], str)).shape[0]
    n = T // tile
    nt, nb, nto, nsc = len(tiled), len(bcast), len(tiled_out), len(scratch)

    def row_block(i):
        return n - 1 - i if reverse else i

    def body(*refs):
        t_refs, b_refs = refs[:nt], refs[nt:nt + nb]
        to_refs, ro_refs = refs[nt + nb:nt + nb + nto], refs[nt + nb + nto:len(refs) - nsc]
        extra = (list(refs[len(refs) - nsc:]),) if nsc else ()
        touts, routs = fn([r[...] for r in t_refs], [r[...] for r in b_refs], *extra)
        for r, v in zip(to_refs, touts, strict=True):
            r[...] = v.astype(r.dtype)
        if ro_refs:
            i = pl.program_id(0)

            @pl.when(i == 0)
            def _():
                for r, v in zip(ro_refs, routs, strict=True):
                    r[...] = v.astype(F32)

            @pl.when(i > 0)
            def _():
                for r, v in zip(ro_refs, routs, strict=True):
                    r[...] += v.astype(F32)

    def whole(shape):
        nd = len(shape)
        return pl.BlockSpec(tuple(shape), lambda i: (0,) * nd)

    per_tile = tile // _HALO
    in_specs, arrays = [], []
    for t in tiled:
        if isinstance(t[0], str):
            _, arr, w, cb = t
            in_specs.append(pl.BlockSpec((_HALO, w), functools.partial(lambda i, cb: (jnp.maximum(row_block(i) * per_tile - 1, 0), cb), cb=cb)))
        else:
            arr, w, cb = t
            in_specs.append(pl.BlockSpec((tile, w), functools.partial(lambda i, cb: (row_block(i), cb), cb=cb)))
        arrays.append(arr)
    in_specs += [whole(b.shape) for b in bcast]
    out_specs = [pl.BlockSpec((tile, c), lambda i: (row_block(i), 0)) for (c, _) in tiled_out] + [whole(s) for s in red_out]
    out_shape = [jax.ShapeDtypeStruct((T, c), dt) for (c, dt) in tiled_out] + [jax.ShapeDtypeStruct(tuple(s), F32) for s in red_out]
    res = pl.pallas_call(
        body, name=name, grid=(n,), in_specs=in_specs, out_specs=out_specs, out_shape=out_shape,
        scratch_shapes=[pltpu.VMEM(tuple(s), F32) for s in scratch],
        compiler_params=_cparams(("arbitrary",)),
    )(*arrays, *bcast)
    return list(res[:nto]), list(res[nto:])


def _rowwise_vjp(name, f, tiled, bcast, cts, tile, wrt_t, wrt_b, t_dtypes=None, colsum=(), prep=None, finish=None,
                 out_widths=None, reverse=False, scratch=()):
    tiled = [t if isinstance(t, tuple) else (t, t.shape[1], 0) for t in tiled]
    npr = len(tiled)
    t_dtypes = t_dtypes or [F32] * len(wrt_t)
    groups = [c if isinstance(c, list) else [c] for c in cts]
    cts = [a_ for grp in groups for a_ in grp]

    def fn(tv, bv, sc=None):
        prim, flat_ct = tv[:npr], list(tv[npr:])
        if prep is not None:
            prim = prep(prim)
        ct = []
        for grp in groups:
            parts = [flat_ct.pop(0).astype(F32) for _ in grp]
            ct.append(functools.reduce(lambda p_, q_: p_ + q_, parts))

        def g(dt_vals, db_vals):
            full_t, full_b = list(prim), list(bv)
            for i, v in zip(wrt_t, dt_vals, strict=True):
                full_t[i] = v
            for j, v in zip(wrt_b, db_vals, strict=True):
                full_b[j] = v
            return f(full_t, full_b)

        outs, pull = jax.vjp(g, [prim[i].astype(F32) for i in wrt_t], [bv[j] for j in wrt_b])
        dts, dbs = pull([c.astype(o.dtype) for c, o in zip(ct, outs, strict=True)])
        if finish is not None:
            dts = finish(dts, sc)
        sums = [jnp.sum(dts[i].astype(F32), axis=0, keepdims=True) for i in colsum]
        return dts, list(dbs) + sums

    widths = out_widths or [tiled[i][1] for i in wrt_t]
    tiled_out = list(zip(widths, t_dtypes, strict=True))
    red_out = [bcast[j].shape for j in wrt_b] + [(1, widths[i]) for i in colsum]
    dts, reds = _rowwise(name, fn, tiled + list(cts), bcast, tiled_out, red_out, tile, reverse=reverse, scratch=scratch)
    nb = len(wrt_b)
    return dts, reds[:nb], reds[nb:]


def _layer_norm(x, g, b, eps):
    mu = jnp.mean(x, axis=-1, keepdims=True)
    xc = x - mu
    var = jnp.mean(xc * xc, axis=-1, keepdims=True)
    return xc * lax.rsqrt(var + eps) * g + b


def _gelu_tanh(x):
    return 0.5 * x * (1.0 + jnp.tanh(0.7978845608028654 * (x + 0.044715 * (x * x * x))))


def _sigmoid(x):
    return 1.0 / (1.0 + jnp.exp(-x))


def _seg_modulate(tv, bv):
    (x,), (sc, sh) = tv, bv
    return [x * (1.0 + sc) + sh]


def _seg_gmlp(tv, bv):
    (z,), (g_ln, b_ln, ws, b_tg, expand) = tv, bv
    bias_full = _dot(b_tg, expand, hi=True)
    zz = _gelu_tanh(z)
    u, v = zz[:, :G_WIDTH], zz[:, G_WIDTH:]
    v = _layer_norm(v, g_ln, b_ln, LN_EPS)
    row = lax.broadcasted_iota(jnp.int32, (G_CHUNK, G_CHUNK), 0)
    col = lax.broadcasted_iota(jnp.int32, (G_CHUNK, G_CHUNK), 1)
    causal = col <= row
    first_group = lax.broadcasted_iota(jnp.int32, (G_CHUNK, 128), 1) < 64
    parts = []
    for p in range(4):
        vp = v[:, 128 * p:128 * (p + 1)]
        s_even = _dot(jnp.where(causal, ws[2 * p], 0.0), vp)
        s_odd = _dot(jnp.where(causal, ws[2 * p + 1], 0.0), vp)
        parts.append(jnp.where(first_group, s_even, s_odd))
    s = jnp.concatenate(parts, axis=1) + bias_full
    return [u * s]


def _split2(x):
    hi = x.astype(_MXU_DTYPE)
    return hi, (x - hi.astype(F32)).astype(_MXU_DTYPE)


@jax.custom_vjp
def _group_sum(x, ones_blocks):
    hi, lo = _split2(x)
    return _dot(hi, ones_blocks) + _dot(lo, ones_blocks)


def _group_sum_fwd(x, ones_blocks):
    return _group_sum(x, ones_blocks), ones_blocks


def _group_sum_bwd(ones_blocks, ct):
    return _group_sum(ct, ones_blocks), jnp.zeros_like(ones_blocks)


_group_sum.defvjp(_group_sum_fwd, _group_sum_bwd)


def _shift_down(z, halo, is_first):
    _, W = z.shape
    rolled = pltpu.roll(z, 1, 0)
    before = jnp.where(is_first, 0.0, pltpu.roll(halo, 1, 0))
    top_row = lax.broadcasted_iota(jnp.int32, (_HALO, W), 0) == 0
    return jnp.concatenate([jnp.where(top_row, before, rolled[:_HALO]), rolled[_HALO:]], axis=0)


def _shift_up(d, after):
    tile, W = d.shape
    rolled = pltpu.roll(d, tile - 1, 0)
    last_row = lax.broadcasted_iota(jnp.int32, (_HALO, W), 0) == _HALO - 1
    bottom = jnp.where(last_row, pltpu.roll(after, _HALO - 1, 0), rolled[tile - _HALO:])
    return jnp.concatenate([rolled[:tile - _HALO], bottom], axis=0)


def _seg_rwkv_pre(tv, bv):
    (z, prev), (mu, w0, wd, a0, wa, wg, k_k, k_a, gsum) = tv, bv
    zs = z + (prev - z) * mu
    r, k, v = zs[:, 0:512], zs[:, 512:1024], zs[:, 1024:1536]
    zl = zs[:, 1536:2048]
    x = w0 + _dot(jnp.tanh(zl), wd)
    softplus = jnp.maximum(-x, 0.0) + jnp.log(1.0 + jnp.exp(-jnp.abs(x)))
    lw = -jnp.exp(-softplus - 0.5)
    a = _sigmoid(a0 + _dot(zl, wa))
    g = _dot(_sigmoid(zl), wg)
    kk = k * k_k
    nrm = jnp.sqrt(_group_sum(kk * kk, gsum))
    kk = kk / jnp.maximum(nrm, 1e-12)
    k2 = k * (1.0 + (a - 1.0) * k_a)
    return [r, lw, k2, v, -kk, kk * a, g]


def _seg_rwkv_post(tv, bv):
    (y, r, k2, v, g), (r_k, gain, bias, gsum) = tv, bv
    mu = _group_sum(y, gsum) * (1.0 / R_HEAD)
    yc = y - mu
    var = _group_sum(yc * yc, gsum) * (1.0 / R_HEAD)
    yn = yc * lax.rsqrt(var + GN_EPS) * gain + bias
    bonus = _group_sum(r * k2 * r_k, gsum) * v
    return [(yn + bonus) * g]


def _seg_merge(tv, bv):
    (ga, gb, pa, pb), () = tv, bv
    return [_sigmoid(ga) * pa + _sigmoid(gb) * pb]


def _seg_mid(tv, bv):
    (x, mix), (gt1, g1, b1, sc2, sh2) = tv, bv
    h1 = _layer_norm(ALPHA * x + gt1 * mix, g1, b1, LN_EPS)
    return [h1, h1 * (1.0 + sc2) + sh2]


def _seg_relu2(tv, bv):
    (f1,), () = tv, bv
    return [jnp.square(jnp.maximum(f1, 0.0))]


def _seg_loss(tv, bv):
    (h1, ff, target), (gt2, g2, b2) = tv, bv
    out = _layer_norm(ALPHA * h1 + gt2 * ff, g2, b2, LN_EPS)
    err = jnp.square(out - target)
    return 0.5 * jnp.sum(jnp.mean(err, axis=-1))


_BNN = (((2,), (1,)), ((0,), (0,)))
_BNT = (((2,), (2,)), ((0,), (0,)))
_BTN = (((1,), (1,)), ((0,), (0,)))


def _tri_dot(x, dims):
    H, L, _ = x.shape
    tri = (lax.broadcasted_iota(jnp.int32, (H, L, L), 2) <= lax.broadcasted_iota(jnp.int32, (H, L, L), 1)).astype(F32)
    hi, lo = _split2(x)
    return _dot(tri, hi, dims) + _dot(tri, lo, dims)


@jax.custom_vjp
def _running_sum(x):
    return _tri_dot(x, _BNN)


_running_sum.defvjp(lambda x: (_tri_dot(x, _BNN), None), lambda _, ct: (_tri_dot(ct, _BTN),))


def _inverse_pullback(inv, ct):
    return _dot(_dot(inv, ct, _BTN), inv, _BNT)


@jax.custom_vjp
def _unit_lower_inverse(n_mat):
    H, L, _ = n_mat.shape
    eye = lax.broadcasted_iota(jnp.int32, (H, L, L), 1) == lax.broadcasted_iota(jnp.int32, (H, L, L), 2)
    inv = jnp.where(eye, 1.0, 0.0) + n_mat
    pw = n_mat
    n = 2
    while n < L:
        pw = _dot(pw, pw, _BNN)
        inv = inv + _dot(inv, pw, _BNN)
        n *= 2
    return inv


def _unit_lower_inverse_fwd(n_mat):
    inv = _unit_lower_inverse(n_mat)
    return inv, inv


_unit_lower_inverse.defvjp(_unit_lower_inverse_fwd, lambda inv, ct: (_inverse_pullback(inv, ct),))


@jax.custom_vjp
def _known_inverse(n_mat, inv):
    return inv


_known_inverse.defvjp(lambda n_mat, inv: (inv, inv), lambda inv, ct: (_inverse_pullback(inv, ct), jnp.zeros_like(inv)))


def _scan_chunk(r, lw, k, v, a, b, s0, inv=None, with_inverse=False):
    H, L, _ = r.shape
    row = lax.broadcasted_iota(jnp.int32, (H, L, L), 1)
    col = lax.broadcasted_iota(jnp.int32, (H, L, L), 2)
    incl, strict = col <= row, col < row
    cs = _running_sum(lw)
    cs_end = cs[:, L - 1:L, :]
    p, p_inv = jnp.exp(cs), jnp.exp(-cs)
    at, bt, kt, rt = a * jnp.exp(cs - lw), b * p_inv, k * p_inv, r * p
    a_ab = jnp.where(strict, _dot(at, bt, _BNT), 0.0)
    a_ak = jnp.where(strict, _dot(at, kt, _BNT), 0.0)
    a_rb = jnp.where(incl, _dot(rt, bt, _BNT), 0.0)
    a_rk = jnp.where(incl, _dot(rt, kt, _BNT), 0.0)
    inv = _unit_lower_inverse(a_ab) if inv is None else _known_inverse(a_ab, inv)
    u = _dot(inv, _dot(at, s0, _BNT) + _dot(a_ak, v, _BNN), _BNN)
    y = _dot(rt, s0, _BNT) + _dot(a_rb, u, _BNN) + _dot(a_rk, v, _BNN)
    to_end = jnp.exp(cs_end - cs)
    s1 = s0 * jnp.exp(cs_end) + _dot(u, b * to_end, _BTN) + _dot(v, k * to_end, _BTN)
    return (y, s1, inv) if with_inverse else (y, s1)


def _split_heads(x):
    return jnp.stack([x[:, R_HEAD * h:R_HEAD * (h + 1)] for h in range(R_HEADS)])


def _merge_heads(x):
    return jnp.concatenate([x[h] for h in range(R_HEADS)], axis=1)


def _scan_fwd(r, lw, k, v, a, b):
    T = r.shape[0]
    H, N, L = R_HEADS, R_HEAD, SCAN_CHUNK
    nc = T // L

    def body(r_ref, lw_ref, k_ref, v_ref, a_ref, b_ref, y_ref, st_ref, inv_ref, s_ref):
        @pl.when(pl.program_id(0) == 0)
        def _():
            s_ref[...] = jnp.zeros_like(s_ref)

        s0 = s_ref[...]
        st_ref[0] = s0
        y, s1, inv = _scan_chunk(*[_split_heads(t[...]) for t in (r_ref, lw_ref, k_ref, v_ref, a_ref, b_ref)], s0, with_inverse=True)
        y_ref[...] = _merge_heads(y)
        inv_ref[0] = inv
        s_ref[...] = s1

    blk = pl.BlockSpec((L, R_WIDTH), lambda c: (c, 0))
    per_chunk = pl.BlockSpec((1, H, N, N), lambda c: (c, 0, 0, 0))
    return pl.pallas_call(
        body, name="scan_fwd", grid=(nc,), in_specs=[blk] * 6, out_specs=[blk, per_chunk, per_chunk],
        out_shape=[jax.ShapeDtypeStruct((T, R_WIDTH), F32)] + [jax.ShapeDtypeStruct((nc, H, N, N), F32)] * 2,
        scratch_shapes=[pltpu.VMEM((H, N, N), F32)],
        compiler_params=_cparams(("arbitrary",)),
    )(r, lw, k, v, a, b)


def _scan_bwd(r, lw, k, v, a, b, states, inverses, dy):
    T = r.shape[0]
    H, N, L = R_HEADS, R_HEAD, SCAN_CHUNK
    nc = T // L

    def body(r_ref, lw_ref, k_ref, v_ref, a_ref, b_ref, st_ref, inv_ref, dy_ref, dr_ref, dlw_ref, dk_ref, dv_ref, da_ref, db_ref, ds_ref):
        @pl.when(pl.program_id(0) == 0)
        def _():
            ds_ref[...] = jnp.zeros_like(ds_ref)

        args = [_split_heads(t[...]) for t in (r_ref, lw_ref, k_ref, v_ref, a_ref, b_ref)] + [st_ref[0]]
        inv = inv_ref[0]
        _, pull = jax.vjp(lambda *xs: _scan_chunk(*xs, inv=inv), *args)
        grads = pull((_split_heads(dy_ref[...]), ds_ref[...]))
        for o_ref, g_ in zip((dr_ref, dlw_ref, dk_ref, dv_ref, da_ref, db_ref), grads[:6], strict=True):
            o_ref[...] = _merge_heads(g_)
        ds_ref[...] = grads[6]

    blk = pl.BlockSpec((L, R_WIDTH), lambda c: (nc - 1 - c, 0))
    per_chunk = pl.BlockSpec((1, H, N, N), lambda c: (nc - 1 - c, 0, 0, 0))
    return pl.pallas_call(
        body, name="scan_bwd", grid=(nc,), in_specs=[blk] * 6 + [per_chunk, per_chunk, blk], out_specs=[blk] * 6,
        out_shape=[jax.ShapeDtypeStruct((T, R_WIDTH), F32)] * 6,
        scratch_shapes=[pltpu.VMEM((H, N, N), F32)],
        compiler_params=_cparams(("arbitrary",)),
    )(r, lw, k, v, a, b, states, inverses, dy)


def _place():
    x, y, c = lax.axis_index("x"), lax.axis_index("y"), lax.axis_index("c")
    return x, y, c


def _all_gather(name, block):
    R, C = block.shape

    def body(x_ref, out_ref, send_sems, recv_sems, local_sem):
        x, y, c = _place()
        me, sibling = (x, y, c), (x, y, 1 - c)
        chips = [(1 - x, y), (x, 1 - y), (1 - x, 1 - y)]

        def slot(px, py, pc):
            return out_ref.at[4 * px + 2 * py + pc]

        def copy(k, blk, to, src=None):
            return pltpu.make_async_remote_copy(
                src_ref=slot(*blk) if src is None else src, dst_ref=slot(*blk),
                send_sem=send_sems.at[k], recv_sem=recv_sems.at[k], device_id=to, device_id_type=_MESH_ID)

        mine = pltpu.make_async_copy(x_ref, slot(*me), local_sem)
        mine.start()
        first = [copy(0, me, sibling, src=x_ref)]
        first += [copy(1 + j, me, (*chip, c), src=x_ref) for j, chip in enumerate(chips)]
        for cp in first:
            cp.start()
        passed = [copy(4 + j, (*chip, c), sibling) for j, chip in enumerate(chips)]
        for j, chip in enumerate(chips):
            copy(1 + j, (*chip, c), me).wait_recv()
            passed[j].start()
        copy(0, sibling, me).wait_recv()
        for j, chip in enumerate(chips):
            copy(4 + j, (*chip, 1 - c), me).wait_recv()
        for cp in first + passed:
            cp.wait_send()
        mine.wait()

    hbm = pl.BlockSpec(memory_space=pltpu.HBM)
    return pl.pallas_call(
        body, name=name, in_specs=[hbm], out_specs=hbm,
        out_shape=jax.ShapeDtypeStruct((_N_DEV, R, C), block.dtype),
        scratch_shapes=[pltpu.SemaphoreType.DMA((7,)), pltpu.SemaphoreType.DMA((7,)), pltpu.SemaphoreType.DMA],
    )(block)


def _sibling_exchange(name, blocks):
    _, R, C = blocks.shape

    def body(x_ref, out_ref, send_sems, recv_sems):
        x, y, c = _place()
        copies = [pltpu.make_async_remote_copy(
            src_ref=x_ref.at[2 * q + (1 - c)], dst_ref=out_ref.at[q], send_sem=send_sems.at[q], recv_sem=recv_sems.at[q],
            device_id=(x, y, 1 - c), device_id_type=_MESH_ID) for q in range(4)]
        for cp in copies:
            cp.start()
        for cp in copies:
            cp.wait()

    hbm = pl.BlockSpec(memory_space=pltpu.HBM)
    return pl.pallas_call(
        body, name=name, in_specs=[hbm], out_specs=hbm, out_shape=jax.ShapeDtypeStruct((4, R, C), blocks.dtype),
        scratch_shapes=[pltpu.SemaphoreType.DMA((4,)), pltpu.SemaphoreType.DMA((4,))],
    )(blocks)


def _chip_partials(name, blocks, from_sibling, tile, out_dtype):
    _, R, C = blocks.shape

    def body(x_ref, s_ref, o_ref):
        c = lax.axis_index("c")
        for q in range(4):
            o_ref[q] = (x_ref[2 * q + c] + s_ref[q]).astype(o_ref.dtype)

    return pl.pallas_call(
        body, name=name, grid=(R // tile,),
        in_specs=[pl.BlockSpec((_N_DEV, tile, C), lambda i: (0, i, 0)), pl.BlockSpec((4, tile, C), lambda i: (0, i, 0))],
        out_specs=pl.BlockSpec((4, tile, C), lambda i: (0, i, 0)), out_shape=jax.ShapeDtypeStruct((4, R, C), out_dtype),
        compiler_params=_cparams(("parallel",)),
    )(blocks, from_sibling)


def _chip_exchange(name, partials):
    _, R, C = partials.shape

    def body(x_ref, out_ref, send_sems, recv_sems, local_sem):
        x, y, c = _place()
        my_chip = 2 * x + y
        mine = pltpu.make_async_copy(x_ref.at[my_chip], out_ref.at[my_chip], local_sem)
        mine.start()
        copies = []
        for rel in range(1, 4):
            px, py = (1 - x if (rel >> 1) & 1 else x), (1 - y if rel & 1 else y)
            copies.append(pltpu.make_async_remote_copy(
                src_ref=x_ref.at[2 * px + py], dst_ref=out_ref.at[my_chip],
                send_sem=send_sems.at[rel - 1], recv_sem=recv_sems.at[rel - 1],
                device_id=(px, py, c), device_id_type=_MESH_ID))
        for cp in copies:
            cp.start()
        for cp in copies:
            cp.wait()
        mine.wait()

    hbm = pl.BlockSpec(memory_space=pltpu.HBM)
    return pl.pallas_call(
        body, name=name, in_specs=[hbm], out_specs=hbm, out_shape=jax.ShapeDtypeStruct(partials.shape, partials.dtype),
        scratch_shapes=[pltpu.SemaphoreType.DMA((3,)), pltpu.SemaphoreType.DMA((3,)), pltpu.SemaphoreType.DMA],
    )(partials)


def _sum_leading(name, x, tile=None):
    n, R, C = x.shape
    tile = tile or _pick(R, (512, 256, 128, 64, 32, 16, 8))

    def body(x_ref, o_ref):
        acc = x_ref[0].astype(F32)
        for k in range(1, n):
            acc = acc + x_ref[k].astype(F32)
        o_ref[...] = acc

    return pl.pallas_call(
        body, name=name, grid=(R // tile,), in_specs=[pl.BlockSpec((n, tile, C), lambda i: (0, i, 0))],
        out_specs=pl.BlockSpec((tile, C), lambda i: (i, 0)), out_shape=jax.ShapeDtypeStruct((R, C), F32),
        compiler_params=_cparams(("parallel",)),
    )(x)


def _adamw_update(w_, g_, m_, v_):
    m2 = ADAM_B1 * m_ + (1.0 - ADAM_B1) * g_
    v2 = ADAM_B2 * v_ + (1.0 - ADAM_B2) * jnp.square(g_)
    m_hat = m2 / (1.0 - ADAM_B1 ** ADAM_STEP)
    v_hat = v2 / (1.0 - ADAM_B2 ** ADAM_STEP)
    delta = -ADAM_LR * (m_hat / (jnp.sqrt(v_hat) + ADAM_EPS) + ADAM_WD * w_)
    return delta, m2, v2


def _adamw(name, w, g, m, v):
    R, C = w.shape
    tile = _pick(R, (256, 128, 64, 32, 16, 8))
    outs, _ = _rowwise(name, lambda tv, bv: (list(_adamw_update(*tv)), []), [w, g, m, v], [], [(C, F32)] * 3, [], tile)
    return outs


def _adamw_many(name, ws, gs, ms, vs):
    n = len(ws)

    def body(*refs):
        ins, outs = refs[:4 * n], refs[4 * n:]
        for i in range(n):
            res = _adamw_update(ins[i][...], ins[n + i][...], ins[2 * n + i][...], ins[3 * n + i][...])
            for j in range(3):
                outs[j * n + i][...] = res[j]

    out_shape = [jax.ShapeDtypeStruct(w.shape, F32) for w in ws] * 3
    res = pl.pallas_call(body, name=name, out_shape=out_shape, compiler_params=_cparams())(*ws, *gs, *ms, *vs)
    return res[:n], res[n:2 * n], res[2 * n:]


def _pack_rows(arrs, lanes=128, row_mult=8):
    flat, places, off = [], [], 0
    for a_ in arrs:
        n = a_.size
        flat.append(a_.reshape(-1).astype(F32))
        places.append((off, n, a_.shape))
        off += n
    total = -(-off // (lanes * row_mult)) * (lanes * row_mult)
    if total > off:
        flat.append(jnp.zeros((total - off,), F32))
    return jnp.concatenate(flat).reshape(total // lanes, lanes), places


def _unpack_rows(packed, places):
    flat = packed.reshape(-1)
    return [flat[o:o + n].reshape(s) for (o, n, s) in places]


_WEIGHTS = ['w_ada', 'b_ada', 'w_in', 'b_in', 'g_ln_v', 'b_ln_v', 'w_spatial', 'b_spatial', 'mu_shift', 'w0', 'w_decay_up', 'a0',
            'w_aaa_up', 'w_gate_up', 'k_k', 'k_a', 'r_k', 'gn_gain', 'gn_bias', 'w_branch_a', 'w_branch_b', 'w_out', 'b_out',
            'ln1_g', 'ln1_b', 'w_ff1', 'b_ff1', 'w_ff2', 'b_ff2', 'ln2_g', 'ln2_b']
_BIG = {'w_ff1': (0, 512), 'w_ff2': (512, 512), 'w_in': (1024, 640), 'w_out': (1664, 128), 'w_branch_a': (1792, 64), 'w_branch_b': (1856, 64)}
_BIG_ROWS = 1920
_CUT_BY_COLS = ('w_ff1', 'w_in', 'w_branch_a', 'w_branch_b')
IN_SHARD = IN_COLS // _N_DEV
_LORA = {'w_decay_up': (0, LORA_W), 'w_aaa_up': (LORA_W, LORA_A), 'w_gate_up': (LORA_W + LORA_A, LORA_G)}
_COMM_DTYPE = jnp.bfloat16


def _pad_rows(a, rows):
    return jnp.pad(a, ((0, rows - a.shape[0]),) + ((0, 0),) * (a.ndim - 1))


def _pack_big(shards):
    blocks = []
    for n, (_, rows) in _BIG.items():
        a = shards[n].T if n in _CUT_BY_COLS else shards[n]
        blocks.append(_pad_rows(a.reshape(-1, D_MODEL), rows))
    return jnp.concatenate(blocks, axis=0)


def _unpack_big(block, like):
    out = {}
    for n, (r0, _) in _BIG.items():
        rr, cc = like[n].shape
        if n in _CUT_BY_COLS:
            out[n] = block[r0:r0 + rr * cc // D_MODEL].reshape(cc, rr).T
        else:
            out[n] = block[r0:r0 + rr]
    return out


def _to_padded(a, axis):
    g_end = 2 * G_WIDTH
    r_end = g_end + RW_USED
    take = lambda lo, hi: lax.slice_in_dim(a, lo, hi, axis=axis)
    zshape = list(a.shape)
    zshape[axis] = RW_COLS - RW_USED
    return jnp.concatenate([take(r_end, IN_COLS), take(g_end, r_end), jnp.zeros(zshape, a.dtype), take(0, g_end)], axis=axis)


def _from_padded(a, axis):
    take = lambda lo, hi: lax.slice_in_dim(a, lo, hi, axis=axis)
    return jnp.concatenate([take(2 * D_MODEL + RW_COLS, P_COLS), take(2 * D_MODEL, 2 * D_MODEL + RW_USED), take(0, 2 * D_MODEL)], axis=axis)


def _step(p, m, v, x, c, target):
    T = x.shape[0]
    xi, yi, ci = _place()
    me = 4 * xi + 2 * yi + ci
    tile = 256

    lane = jnp.arange(R_WIDTH)
    gsum = (lane[:, None] // R_HEAD == lane[None, :] // R_HEAD).astype(F32)
    expand = (jnp.arange(128)[:, None] == (lane[None, :] // (G_WIDTH // 8))).astype(F32)

    (c_act,), _ = _rowwise("silu_c", lambda tv, bv: ([tv[0] * _sigmoid(tv[0])], []), [c], [], [(D_MODEL, F32)], [], 1)
    small, places = _pack_rows([c_act, p['w_decay_up'], p['w_aaa_up'], p['w_gate_up']])
    small_all = _all_gather("gather_small", small)
    per_dev = [_unpack_rows(small_all[d], places) for d in range(_N_DEV)]
    c_act_all = _pad_rows(jnp.concatenate([pd[0] for pd in per_dev], axis=0), 16)
    lora_full = {n: jnp.concatenate([pd[i + 1] for pd in per_dev], axis=1) for i, n in enumerate(_LORA)}
    lora_pad = {n: jnp.zeros((LORA_PAD, R_WIDTH), F32).at[r0:r0 + nr].set(lora_full[n]) for n, (r0, nr) in _LORA.items()}

    big_names = list(_BIG)
    packed_all = _all_gather("gather_weights", _pack_big(p).astype(_MXU_DTYPE))

    def whole(n, rows):
        r0 = _BIG[n][0]
        return packed_all[:, r0:r0 + rows].reshape(_N_DEV * rows, D_MODEL)

    w_ff1_t, w_ff2, w_out = whole('w_ff1', 512), whole('w_ff2', 512), whole('w_out', 128)
    w_in_t = _to_padded(whole('w_in', IN_SHARD), 0)
    w_ba_t = whole('w_branch_a', 64).reshape(D_MODEL, G_WIDTH)
    w_bb_t = whole('w_branch_b', 64).reshape(D_MODEL, R_WIDTH)
    b_in_p = _to_padded(p['b_in'], 1)
    mu_p = jnp.concatenate([p['mu_shift'], jnp.zeros((1, RW_COLS - RW_USED), F32)], axis=1)

    b_ada_mine = lax.dynamic_slice(p['b_ada'], (0, me * 768), (1, 768))
    mod_cols = _mm("ada_mod", c_act_all, p['w_ada'], "nn", bias=b_ada_mine)
    mod_all = _all_gather("gather_mod", mod_cols)
    mod = lax.dynamic_index_in_dim(mod_all, me, axis=1, keepdims=False).reshape(1, 6 * D_MODEL)
    sh1, sc1, gt1, sh2, sc2, gt2 = [mod[:, i * D_MODEL:(i + 1) * D_MODEL] for i in range(6)]

    (h,), _ = _rowwise("modulate1", lambda tv, bv: (_seg_modulate(tv, bv), []), [x], [sc1, sh1], [(D_MODEL, _MXU_DTYPE)], [], tile)
    proj = _mm("in_proj", h, w_in_t, "nt", bias=b_in_p)
    ws = p['w_spatial']
    b_tg = jnp.zeros((G_CHUNK, 128), F32).at[:, :8].set(p['b_spatial'].T)
    gmlp_b = [p['g_ln_v'], p['b_ln_v'], ws, b_tg, expand]
    z_gmlp = (proj, 2 * G_WIDTH, 4)
    (ya,), _ = _rowwise("gmlp", lambda tv, bv: (_seg_gmlp(tv, bv), []), [z_gmlp], gmlp_b, [(G_WIDTH, F32)], [], G_CHUNK)
    z_rw = (proj, RW_COLS, 1)
    z_rw_halo = ("halo", proj, RW_COLS, 1)
    pre_b = [mu_p, p['w0'], lora_pad['w_decay_up'], p['a0'], lora_pad['w_aaa_up'], lora_pad['w_gate_up'], p['k_k'], p['k_a'], gsum]

    def pre_fwd(tv, bv):
        z_t, halo_t = tv
        return _seg_rwkv_pre([z_t, _shift_down(z_t, halo_t, pl.program_id(0) == 0)], bv), []

    pre_out, _ = _rowwise("rwkv_pre", pre_fwd, [z_rw, z_rw_halo], pre_b, [(R_WIDTH, F32)] * 7, [], tile)
    r_, lw_, k2_, v_, a_, b_, g_ = pre_out
    scan_in = (r_, lw_, k2_, v_, a_, b_)
    y_, states, inverses = _scan_fwd(*scan_in)
    post_b = [p['r_k'].reshape(1, R_WIDTH), p['gn_gain'], p['gn_bias'], gsum]
    (yb,), _ = _rowwise("rwkv_post", lambda tv, bv: (_seg_rwkv_post(tv, bv), []), [y_, r_, k2_, v_, g_], post_b, [(R_WIDTH, F32)], [], tile)
    pa = _mm("branch_a", ya, w_ba_t, "nt")
    pb = _mm("branch_b", yb, w_bb_t, "nt")
    gates = [(proj, D_MODEL, 0), (proj, D_MODEL, 1)]
    (merged,), _ = _rowwise("merge", lambda tv, bv: (_seg_merge(tv, bv), []), gates + [pa, pb], [], [(D_MODEL, _MXU_DTYPE)], [], tile)
    mix = _mm("out_proj", merged, w_out, "nn", bias=p['b_out'])
    mid_b = [gt1, p['ln1_g'], p['ln1_b'], sc2, sh2]
    (h1, h2in), _ = _rowwise("mid", lambda tv, bv: (_seg_mid(tv, bv), []), [x, mix], mid_b, [(D_MODEL, F32), (D_MODEL, _MXU_DTYPE)], [], tile)
    f1, act = _mm("ff1", h2in, w_ff1_t, "nt", bias=p['b_ff1'], epi=lambda t: _seg_relu2([t], [])[0], epi_dtype=_MXU_DTYPE)
    ff = _mm("ff2", act, w_ff2, "nn", bias=p['b_ff2'])

    def loss_fn(tv, bv):
        h1_t, ff_t, tgt = tv
        val, grads = jax.value_and_grad(lambda a0_, a1_, b0_, b1_, b2_: _seg_loss([a0_, a1_, tgt], [b0_, b1_, b2_]), argnums=(0, 1, 2, 3, 4))(h1_t, ff_t, *bv)
        return [grads[0], grads[1]], [grads[2], grads[3], grads[4], jnp.sum(grads[1], axis=0, keepdims=True), jnp.full((1, 128), val, F32)]

    (dh1_a, dff), (d_gt2, d_ln2_g, d_ln2_b, d_b_ff2, loss_row) = _rowwise(
        "loss", loss_fn, [h1, ff, target], [gt2, p['ln2_g'], p['ln2_b']], [(D_MODEL, F32), (D_MODEL, _MXU_DTYPE)], [(1, D_MODEL)] * 4 + [(1, 128)], tile)
    loss = lax.psum(loss_row[0, 0], ("x", "y", "c"))

    g = {}
    g['ln2_g'], g['ln2_b'], g['b_ff2'] = d_ln2_g, d_ln2_b, d_b_ff2
    gw = {}
    gw['w_ff2'] = _mm("g_w_ff2", act, dff, "tn")
    df1 = _mm("d_act", dff, w_ff2, "nt", beside=f1, epi=lambda d_, f_: d_ * (2.0 * jnp.maximum(f_, 0.0)), epi_dtype=_MXU_DTYPE)
    g['b_ff1'] = _mm("g_b_ff1", jnp.ones((16, T), _MXU_DTYPE), df1, "nn")[0:1]
    gw['w_ff1'] = _mm("g_w_ff1", df1, h2in, "tn")
    dh2in = _mm("d_h2in", df1, w_ff1_t, "nn")
    (dx_a, dmix), (d_gt1, g['ln1_g'], g['ln1_b'], d_sc2, d_sh2), (g['b_out'],) = _rowwise_vjp(
        "mid_bwd", _seg_mid, [x, mix], mid_b, [dh1_a, dh2in], tile, [0, 1], [0, 1, 2, 3, 4], t_dtypes=[F32, _MXU_DTYPE], colsum=[1])
    gw['w_out'] = _mm("g_w_out", merged, dmix, "tn")
    dmerged = _mm("d_merged", dmix, w_out, "nt")
    (dga, dgb, dpa, dpb), _, (cs_ga, cs_gb) = _rowwise_vjp(
        "merge_bwd", _seg_merge, gates + [pa, pb], [], [dmerged], tile, [0, 1, 2, 3], [], t_dtypes=[_MXU_DTYPE] * 4, colsum=[0, 1])
    gw['w_branch_a'] = _mm("g_w_branch_a", dpa, ya, "tn")
    gw['w_branch_b'] = _mm("g_w_branch_b", dpb, yb, "tn")
    dya = _mm("d_ya", dpa, w_ba_t, "nn")
    dyb = _mm("d_yb", dpb, w_bb_t, "nn")
    (dy, dr1, dk1, dv1, dg_), (d_r_k, g['gn_gain'], g['gn_bias']), _ = _rowwise_vjp(
        "rwkv_post_bwd", _seg_rwkv_post, [y_, r_, k2_, v_, g_], post_b, [dyb], tile, [0, 1, 2, 3, 4], [0, 1, 2])
    g['r_k'] = d_r_k
    dr2, dlw, dk2, dv2, da, db = _scan_bwd(*scan_in, states, inverses, dy)
    pre_tile = 128
    last_step = T // pre_tile - 1

    def pre_prep(prim):
        z_t, halo_t = prim
        return [z_t, _shift_down(z_t, halo_t, pl.program_id(0) == last_step)]

    def pre_finish(dts, sc):
        dz_direct, dprev = dts
        (row_after,) = sc

        @pl.when(pl.program_id(0) == 0)
        def _():
            row_after[...] = jnp.zeros_like(row_after)

        dz = dz_direct + _shift_up(dprev, row_after[...])
        row_after[...] = dprev[:_HALO]
        return [dz]

    (dz_rw_all,), (d_mu, g['w0'], d_wd, g['a0'], d_wa, d_wg, g['k_k'], g['k_a']), (cs_rw,) = _rowwise_vjp(
        "rwkv_pre_bwd", _seg_rwkv_pre, [z_rw, z_rw_halo], pre_b, [[dr1, dr2], dlw, [dk1, dk2], [dv1, dv2], da, db, dg_], pre_tile,
        [0, 1], [0, 1, 2, 3, 4, 5, 6, 7], t_dtypes=[_MXU_DTYPE], colsum=[0], prep=pre_prep, finish=pre_finish,
        out_widths=[RW_COLS], reverse=True, scratch=[(_HALO, RW_COLS)])
    g['mu_shift'] = d_mu[:, :RW_USED]
    for n, d_ in (('w_decay_up', d_wd), ('w_aaa_up', d_wa), ('w_gate_up', d_wg)):
        r0, nr = _LORA[n]
        g[n] = d_[r0:r0 + nr]
    (dz_g,), (g['g_ln_v'], g['b_ln_v'], g['w_spatial'], d_b_tg), (cs_g,) = _rowwise_vjp(
        "gmlp_bwd", _seg_gmlp, [z_gmlp], gmlp_b, [dya], G_CHUNK, [0], [0, 1, 2, 3], t_dtypes=[_MXU_DTYPE], colsum=[0])
    g['b_spatial'] = d_b_tg[:, :8].T
    dproj = jnp.concatenate([dga, dgb, dz_rw_all, dz_g], axis=1)
    g['b_in'] = _from_padded(jnp.concatenate([cs_ga, cs_gb, cs_rw, cs_g], axis=1), 1)
    gw['w_in'] = _from_padded(_mm("g_w_in", dproj, h, "tn"), 0)
    dh = _mm("d_h", dproj, w_in_t, "nn")

    def mod1_bwd(tv, bv):
        x_t, dh_t, dxa_t = tv
        (sc,) = bv
        return [dxa_t + dh_t * (1.0 + sc)], [jnp.sum(dh_t * x_t, axis=0, keepdims=True), jnp.sum(dh_t, axis=0, keepdims=True)]

    (grad_x,), (d_sc1, d_sh1) = _rowwise("modulate1_bwd", mod1_bwd, [x, dh, dx_a], [sc1], [(D_MODEL, F32)], [(1, D_MODEL)] * 2, tile)

    dmod = jnp.concatenate([d_sh1, d_sc1, d_gt1, d_sh2, d_sc2, d_gt2], axis=1).reshape(6 * D_MODEL // 128, 128)
    dmod_all = _all_gather("gather_dmod", dmod)
    g['b_ada'] = _sum_leading("sum_dmod", dmod_all).reshape(1, 6 * D_MODEL)
    dmod_mine = lax.dynamic_slice(dmod_all.reshape(_N_DEV, 6 * D_MODEL), (0, me * 768), (_N_DEV, 768))
    g_w_ada = _mm("g_w_ada", c_act_all, _pad_rows(dmod_mine, 16), "tn")

    small_names = [n for n in _WEIGHTS if n not in _BIG and n not in ('w_ada', 'b_ada')]
    packed_g, g_places = _pack_rows([g[n] for n in small_names], row_mult=256)
    small_sum = _unpack_rows(_sum_leading("sum_small", _all_gather("gather_small_grads", packed_g)), g_places)
    for n, t in zip(small_names, small_sum, strict=True):
        g[n] = t
    for n in _LORA:
        g[n] = lax.dynamic_slice(g[n], (0, me * R_HEAD), (g[n].shape[0], R_HEAD))
    g['w_ada'] = g_w_ada

    parts = []
    for n, (_, rows) in _BIG.items():
        per_dev = gw[n].reshape(_N_DEV, -1, D_MODEL)
        parts.append(jnp.pad(per_dev, ((0, 0), (0, rows - per_dev.shape[1]), (0, 0))))
    sendbuf = jnp.concatenate(parts, axis=1)
    from_sibling = _sibling_exchange("pair_exchange_grads", sendbuf)
    partials = _chip_partials("chip_partials", sendbuf, from_sibling, 128, _COMM_DTYPE)
    summed = _sum_leading("sum_big", _chip_exchange("chip_exchange_grads", partials))
    g.update(_unpack_big(summed, p))

    delta, new_m, new_v = {}, {}, {}
    own_call = ['w_ada'] + big_names
    for n in own_call:
        delta[n], new_m[n], new_v[n] = _adamw("adamw_" + n, p[n], g[n], m[n], v[n])
    rest = [n for n in _WEIGHTS if n not in own_call]
    outs = _adamw_many("adamw_rest", *[[d[n].reshape(p[n].shape) for n in rest] for d in (p, g, m, v)])
    for d, o in zip((delta, new_m, new_v), outs, strict=True):
        d.update(zip(rest, o, strict=True))
    return loss, grad_x, g, delta, new_m, new_v


def kernel(x, c, w_ada, b_ada, w_in, b_in, g_ln_v, b_ln_v, w_spatial, b_spatial, mu_shift, w0, w_decay_up, a0, w_aaa_up, w_gate_up, k_k, k_a, r_k, gn_gain, gn_bias, w_branch_a, w_branch_b, w_out, b_out, ln1_g, ln1_b, w_ff1, b_ff1, w_ff2, b_ff2, ln2_g, ln2_b, loss_target, m_w_ada, m_b_ada, m_w_in, m_b_in, m_g_ln_v, m_b_ln_v, m_w_spatial, m_b_spatial, m_mu_shift, m_w0, m_w_decay_up, m_a0, m_w_aaa_up, m_w_gate_up, m_k_k, m_k_a, m_r_k, m_gn_gain, m_gn_bias, m_w_branch_a, m_w_branch_b, m_w_out, m_b_out, m_ln1_g, m_ln1_b, m_w_ff1, m_b_ff1, m_w_ff2, m_b_ff2, m_ln2_g, m_ln2_b, v_w_ada, v_b_ada, v_w_in, v_b_in, v_g_ln_v, v_b_ln_v, v_w_spatial, v_b_spatial, v_mu_shift, v_w0, v_w_decay_up, v_a0, v_w_aaa_up, v_w_gate_up, v_k_k, v_k_a, v_r_k, v_gn_gain, v_gn_bias, v_w_branch_a, v_w_branch_b, v_w_out, v_b_out, v_ln1_g, v_ln1_b, v_w_ff1, v_b_ff1, v_w_ff2, v_b_ff2, v_ln2_g, v_ln2_b):
    given = dict(locals())
    shapes = {n: given[n].shape for n in _WEIGHTS}
    def two_d(a_):
        a_ = a_[0]
        return a_.reshape(1, -1) if a_.ndim == 1 else a_
    p = {n: two_d(given[n]) for n in _WEIGHTS}
    m = {n: two_d(given["m_" + n]) for n in _WEIGHTS}
    v = {n: two_d(given["v_" + n]) for n in _WEIGHTS}
    loss, grad_x, g, delta, new_m, new_v = _step(p, m, v, x[0], c, loss_target[0])
    outs = [loss, grad_x[None]]
    for d in (g, delta, new_m, new_v):
        outs += [d[n].reshape(shapes[n]) for n in _WEIGHTS]
    return tuple(outs)
```

```python
import functools

import jax
import jax.numpy as jnp
from jax import lax
from jax.experimental import pallas as pl
from jax.experimental.pallas import tpu as pltpu

F32 = jnp.float32
_MXU_DTYPE = jnp.bfloat16
_HI = lax.Precision.HIGHEST
_VMEM_LIMIT = 48 * 1024 * 1024
_MESH_ID = pl.DeviceIdType.MESH
_N_DEV = 8

D_MODEL = 1024
G_WIDTH = 512
G_CHUNK = 128
R_WIDTH = 512
R_HEADS = 8
R_HEAD = 64
LORA_W, LORA_A, LORA_G = 32, 32, 96
D_FF = 4096
ALPHA = 2.0 ** 0.25
LN_EPS = 1e-5
GN_EPS = 64e-5
SCAN_CHUNK = 64
ADAM_LR, ADAM_B1, ADAM_B2, ADAM_EPS, ADAM_WD, ADAM_STEP = 0.001, 0.9, 0.999, 1e-08, 0.01, 10

P_COLS = 5120
RW_COLS = 2048
RW_USED = 3 * R_WIDTH + LORA_W + LORA_A + LORA_G
LORA_PAD = 512
IN_COLS = 2 * G_WIDTH + RW_USED + 2 * D_MODEL


def _cparams(sem=None, **kw):
    if sem is not None:
        kw["dimension_semantics"] = sem
    return pltpu.CompilerParams(vmem_limit_bytes=_VMEM_LIMIT, **kw)


def _dot(a, b, dims=(((1,), (0,)), ((), ())), hi=False):
    if hi:
        return lax.dot_general(a.astype(F32), b.astype(F32), dims, precision=_HI, preferred_element_type=F32)
    return lax.dot_general(a.astype(_MXU_DTYPE), b.astype(_MXU_DTYPE), dims, preferred_element_type=F32)


_NN = (((1,), (0,)), ((), ()))
_NT = (((1,), (1,)), ((), ()))
_TN = (((0,), (0,)), ((), ()))


def _pick(n, pref):
    for t in pref:
        if n % t == 0:
            return t
    return n


def _mm(name, a, b, mode, bias=None, out_dtype=F32, epi=None, epi_dtype=None, beside=None, tm=None, tn=None, tk=None):
    if mode == "nn":
        (M, K), (_, N) = a.shape, b.shape
    elif mode == "nt":
        (M, K), (N, _) = a.shape, b.shape
    else:
        (K, M), (_, N) = a.shape, b.shape
    tm = tm or _pick(M, (512, 256, 128, 64, 32, 16, 8))
    tn = tn or _pick(N, (1024, 512, 640, 384, 256, 128))
    tk = tk or _pick(K, (1024, 512, 256, 128))
    nk = K // tk
    dims = {"nn": _NN, "nt": _NT, "tn": _TN}[mode]
    a_spec = pl.BlockSpec((tk, tm), lambda i, j, k: (k, i)) if mode == "tn" else pl.BlockSpec((tm, tk), lambda i, j, k: (i, k))
    b_spec = pl.BlockSpec((tn, tk), lambda i, j, k: (j, k)) if mode == "nt" else pl.BlockSpec((tk, tn), lambda i, j, k: (k, j))
    o_spec = pl.BlockSpec((tm, tn), lambda i, j, k: (i, j))
    has_bias, has_beside = bias is not None, beside is not None

    def body(*refs):
        a_ref, b_ref = refs[0], refs[1]
        n_in = 2 + has_bias + has_beside
        bias_ref = refs[2] if has_bias else None
        beside_ref = refs[n_in - 1] if has_beside else None
        outs = refs[n_in:]
        o_ref, acc_ref = outs[0], outs[-1]
        k = pl.program_id(2)

        @pl.when(k == 0)
        def _():
            acc_ref[...] = jnp.zeros_like(acc_ref)

        acc_ref[...] += _dot(a_ref[...], b_ref[...], dims)

        @pl.when(k == nk - 1)
        def _():
            res = acc_ref[...]
            if has_bias:
                res = res + bias_ref[...]
            if has_beside:
                o_ref[...] = epi(res, beside_ref[...]).astype(o_ref.dtype)
            else:
                o_ref[...] = res.astype(o_ref.dtype)
                if epi is not None:
                    outs[1][...] = epi(res).astype(outs[1].dtype)

    in_specs = [a_spec, b_spec]
    args = [a, b]
    if has_bias:
        in_specs.append(pl.BlockSpec((1, tn), lambda i, j, k: (0, j)))
        args.append(bias)
    if has_beside:
        in_specs.append(o_spec)
        args.append(beside)
    two = epi is not None and not has_beside
    out_shape = [jax.ShapeDtypeStruct((M, N), epi_dtype if has_beside else out_dtype)]
    out_specs = [o_spec]
    if two:
        out_shape.append(jax.ShapeDtypeStruct((M, N), epi_dtype))
        out_specs.append(o_spec)
    res = pl.pallas_call(
        body, name=name, grid=(M // tm, N // tn, nk), in_specs=in_specs, out_specs=out_specs, out_shape=out_shape,
        scratch_shapes=[pltpu.VMEM((tm, tn), F32)],
        compiler_params=_cparams(("parallel", "parallel", "arbitrary")),
    )(*args)
    return res if two else res[0]


_HALO = 8


def _rowwise(name, fn, tiled, bcast, tiled_out, red_out, tile, reverse=False, scratch=(), rider=None):
    tiled = [t if isinstance(t, tuple) else (t, t.shape[1], 0) for t in tiled]
    T = next(t[0] for t in tiled if not isinstance(t[0], str)).shape[0]
    n = T // tile
    nt, nb, nto, nsc, nro = len(tiled), len(bcast), len(tiled_out), len(scratch), len(red_out)
    ride_shape, ride_sems, _ = rider[1](rider[0]) if rider else (None, [], None)

    def row_block(i):
        return n - 1 - i if reverse else i

    def body(*refs):
        refs = list(refs)
        if rider:
            sem_refs = [refs.pop() for _ in ride_sems][::-1]
            ride_out = refs.pop(nt + nb + 1 + nto + nro)
            ride_in = refs.pop(nt + nb)
            _ride(rider, pl.program_id(0), n, ride_in, ride_out, sem_refs)
        t_refs, b_refs = refs[:nt], refs[nt:nt + nb]
        to_refs, ro_refs = refs[nt + nb:nt + nb + nto], refs[nt + nb + nto:len(refs) - nsc]
        extra = (list(refs[len(refs) - nsc:]),) if nsc else ()
        touts, routs = fn([r[...] for r in t_refs], [r[...] for r in b_refs], *extra)
        for r, v in zip(to_refs, touts, strict=True):
            r[...] = v.astype(r.dtype)
        if ro_refs:
            i = pl.program_id(0)

            @pl.when(i == 0)
            def _():
                for r, v in zip(ro_refs, routs, strict=True):
                    r[...] = v.astype(F32)

            @pl.when(i > 0)
            def _():
                for r, v in zip(ro_refs, routs, strict=True):
                    r[...] += v.astype(F32)

    def whole(shape):
        nd = len(shape)
        return pl.BlockSpec(tuple(shape), lambda i: (0,) * nd)

    per_tile = tile // _HALO
    in_specs, arrays = [], []
    for t in tiled:
        if isinstance(t[0], str):
            _, arr, w, cb = t
            in_specs.append(pl.BlockSpec((_HALO, w), functools.partial(lambda i, cb: (jnp.maximum(row_block(i) * per_tile - 1, 0), cb), cb=cb)))
        else:
            arr, w, cb = t
            in_specs.append(pl.BlockSpec((tile, w), functools.partial(lambda i, cb: (row_block(i), cb), cb=cb)))
        arrays.append(arr)
    in_specs += [whole(b.shape) for b in bcast]
    out_specs = [pl.BlockSpec((tile, c), lambda i: (row_block(i), 0)) for (c, _) in tiled_out] + [whole(s) for s in red_out]
    out_shape = [jax.ShapeDtypeStruct((T, c), dt) for (c, dt) in tiled_out] + [jax.ShapeDtypeStruct(tuple(s), F32) for s in red_out]
    ride_args = []
    if rider:
        in_specs.append(_HBM)
        out_specs.append(_HBM)
        out_shape.append(ride_shape)
        ride_args = [rider[0]]
    res = pl.pallas_call(
        body, name=name, grid=(n,), in_specs=in_specs, out_specs=out_specs, out_shape=out_shape,
        scratch_shapes=[pltpu.VMEM(tuple(s), F32) for s in scratch] + list(ride_sems),
        compiler_params=_cparams(("arbitrary",)),
    )(*arrays, *bcast, *ride_args)
    if rider:
        return list(res[:nto]), list(res[nto:nto + nro]), res[-1]
    return list(res[:nto]), list(res[nto:])


def _rowwise_vjp(name, f, tiled, bcast, cts, tile, wrt_t, wrt_b, t_dtypes=None, colsum=(), prep=None, finish=None,
                 out_widths=None, reverse=False, scratch=(), rider=None):
    tiled = [t if isinstance(t, tuple) else (t, t.shape[1], 0) for t in tiled]
    npr = len(tiled)
    t_dtypes = t_dtypes or [F32] * len(wrt_t)
    groups = [c if isinstance(c, list) else [c] for c in cts]
    cts = [a_ for grp in groups for a_ in grp]

    def fn(tv, bv, sc=None):
        prim, flat_ct = tv[:npr], list(tv[npr:])
        if prep is not None:
            prim = prep(prim)
        ct = []
        for grp in groups:
            parts = [flat_ct.pop(0).astype(F32) for _ in grp]
            ct.append(functools.reduce(lambda p_, q_: p_ + q_, parts))

        def g(dt_vals, db_vals):
            full_t, full_b = list(prim), list(bv)
            for i, v in zip(wrt_t, dt_vals, strict=True):
                full_t[i] = v
            for j, v in zip(wrt_b, db_vals, strict=True):
                full_b[j] = v
            return f(full_t, full_b)

        outs, pull = jax.vjp(g, [prim[i].astype(F32) for i in wrt_t], [bv[j] for j in wrt_b])
        dts, dbs = pull([c.astype(o.dtype) for c, o in zip(ct, outs, strict=True)])
        if finish is not None:
            dts = finish(dts, sc)
        sums = [jnp.sum(dts[i].astype(F32), axis=0, keepdims=True) for i in colsum]
        return dts, list(dbs) + sums

    widths = out_widths or [tiled[i][1] for i in wrt_t]
    tiled_out = list(zip(widths, t_dtypes, strict=True))
    red_out = [bcast[j].shape for j in wrt_b] + [(1, widths[i]) for i in colsum]
    res = _rowwise(name, fn, tiled + list(cts), bcast, tiled_out, red_out, tile, reverse=reverse, scratch=scratch, rider=rider)
    dts, reds = res[0], res[1]
    nb = len(wrt_b)
    return (dts, reds[:nb], reds[nb:]) + tuple(res[2:])


def _layer_norm(x, g, b, eps):
    mu = jnp.mean(x, axis=-1, keepdims=True)
    xc = x - mu
    var = jnp.mean(xc * xc, axis=-1, keepdims=True)
    return xc * lax.rsqrt(var + eps) * g + b


def _gelu_tanh(x):
    return 0.5 * x * (1.0 + jnp.tanh(0.7978845608028654 * (x + 0.044715 * (x * x * x))))


def _sigmoid(x):
    return 1.0 / (1.0 + jnp.exp(-x))


def _seg_modulate(tv, bv):
    (x,), (sc, sh) = tv, bv
    return [x * (1.0 + sc) + sh]


def _seg_gmlp(tv, bv):
    (z,), (g_ln, b_ln, ws, b_tg, expand) = tv, bv
    bias_full = _dot(b_tg, expand, hi=True)
    zz = _gelu_tanh(z)
    u, v = zz[:, :G_WIDTH], zz[:, G_WIDTH:]
    v = _layer_norm(v, g_ln, b_ln, LN_EPS)
    row = lax.broadcasted_iota(jnp.int32, (G_CHUNK, G_CHUNK), 0)
    col = lax.broadcasted_iota(jnp.int32, (G_CHUNK, G_CHUNK), 1)
    causal = col <= row
    first_group = lax.broadcasted_iota(jnp.int32, (G_CHUNK, 128), 1) < 64
    parts = []
    for p in range(4):
        vp = v[:, 128 * p:128 * (p + 1)]
        s_even = _dot(jnp.where(causal, ws[2 * p], 0.0), vp)
        s_odd = _dot(jnp.where(causal, ws[2 * p + 1], 0.0), vp)
        parts.append(jnp.where(first_group, s_even, s_odd))
    s = jnp.concatenate(parts, axis=1) + bias_full
    return [u * s]


def _split2(x):
    hi = x.astype(_MXU_DTYPE)
    return hi, (x - hi.astype(F32)).astype(_MXU_DTYPE)


@jax.custom_vjp
def _group_sum(x, ones_blocks):
    hi, lo = _split2(x)
    return _dot(hi, ones_blocks) + _dot(lo, ones_blocks)


def _group_sum_fwd(x, ones_blocks):
    return _group_sum(x, ones_blocks), ones_blocks


def _group_sum_bwd(ones_blocks, ct):
    return _group_sum(ct, ones_blocks), jnp.zeros_like(ones_blocks)


_group_sum.defvjp(_group_sum_fwd, _group_sum_bwd)


def _shift_down(z, halo, is_first):
    _, W = z.shape
    rolled = pltpu.roll(z, 1, 0)
    before = jnp.where(is_first, 0.0, pltpu.roll(halo, 1, 0))
    top_row = lax.broadcasted_iota(jnp.int32, (_HALO, W), 0) == 0
    return jnp.concatenate([jnp.where(top_row, before, rolled[:_HALO]), rolled[_HALO:]], axis=0)


def _shift_up(d, after):
    tile, W = d.shape
    rolled = pltpu.roll(d, tile - 1, 0)
    last_row = lax.broadcasted_iota(jnp.int32, (_HALO, W), 0) == _HALO - 1
    bottom = jnp.where(last_row, pltpu.roll(after, _HALO - 1, 0), rolled[tile - _HALO:])
    return jnp.concatenate([rolled[:tile - _HALO], bottom], axis=0)


def _seg_rwkv_pre(tv, bv):
    (z, prev), (mu, w0, wd, a0, wa, wg, k_k, k_a, gsum) = tv, bv
    zs = z + (prev - z) * mu
    r, k, v = zs[:, 0:512], zs[:, 512:1024], zs[:, 1024:1536]
    zl = zs[:, 1536:2048]
    x = w0 + _dot(jnp.tanh(zl), wd)
    softplus = jnp.maximum(-x, 0.0) + jnp.log(1.0 + jnp.exp(-jnp.abs(x)))
    lw = -jnp.exp(-softplus - 0.5)
    a = _sigmoid(a0 + _dot(zl, wa))
    g = _dot(_sigmoid(zl), wg)
    kk = k * k_k
    nrm = jnp.sqrt(_group_sum(kk * kk, gsum))
    kk = kk / jnp.maximum(nrm, 1e-12)
    k2 = k * (1.0 + (a - 1.0) * k_a)
    return [r, lw, k2, v, -kk, kk * a, g]


def _seg_rwkv_post(tv, bv):
    (y, r, k2, v, g), (r_k, gain, bias, gsum) = tv, bv
    mu = _group_sum(y, gsum) * (1.0 / R_HEAD)
    yc = y - mu
    var = _group_sum(yc * yc, gsum) * (1.0 / R_HEAD)
    yn = yc * lax.rsqrt(var + GN_EPS) * gain + bias
    bonus = _group_sum(r * k2 * r_k, gsum) * v
    return [(yn + bonus) * g]


def _seg_merge(tv, bv):
    (ga, gb, pa, pb), () = tv, bv
    return [_sigmoid(ga) * pa + _sigmoid(gb) * pb]


def _seg_mid(tv, bv):
    (x, mix), (gt1, g1, b1, sc2, sh2) = tv, bv
    h1 = _layer_norm(ALPHA * x + gt1 * mix, g1, b1, LN_EPS)
    return [h1, h1 * (1.0 + sc2) + sh2]


def _seg_relu2(tv, bv):
    (f1,), () = tv, bv
    return [jnp.square(jnp.maximum(f1, 0.0))]


def _seg_loss(tv, bv):
    (h1, ff, target), (gt2, g2, b2) = tv, bv
    out = _layer_norm(ALPHA * h1 + gt2 * ff, g2, b2, LN_EPS)
    err = jnp.square(out - target)
    return 0.5 * jnp.sum(jnp.mean(err, axis=-1))


_BNN = (((2,), (1,)), ((0,), (0,)))
_BNT = (((2,), (2,)), ((0,), (0,)))
_BTN = (((1,), (1,)), ((0,), (0,)))


def _tri_dot(x, dims):
    H, L, _ = x.shape
    tri = (lax.broadcasted_iota(jnp.int32, (H, L, L), 2) <= lax.broadcasted_iota(jnp.int32, (H, L, L), 1)).astype(F32)
    hi, lo = _split2(x)
    return _dot(tri, hi, dims) + _dot(tri, lo, dims)


@jax.custom_vjp
def _running_sum(x):
    return _tri_dot(x, _BNN)


_running_sum.defvjp(lambda x: (_tri_dot(x, _BNN), None), lambda _, ct: (_tri_dot(ct, _BTN),))


def _inverse_pullback(inv, ct):
    return _dot(_dot(inv, ct, _BTN), inv, _BNT)


@jax.custom_vjp
def _unit_lower_inverse(n_mat):
    H, L, _ = n_mat.shape
    eye = lax.broadcasted_iota(jnp.int32, (H, L, L), 1) == lax.broadcasted_iota(jnp.int32, (H, L, L), 2)
    inv = jnp.where(eye, 1.0, 0.0) + n_mat
    pw = n_mat
    n = 2
    while n < L:
        pw = _dot(pw, pw, _BNN)
        inv = inv + _dot(inv, pw, _BNN)
        n *= 2
    return inv


def _unit_lower_inverse_fwd(n_mat):
    inv = _unit_lower_inverse(n_mat)
    return inv, inv


_unit_lower_inverse.defvjp(_unit_lower_inverse_fwd, lambda inv, ct: (_inverse_pullback(inv, ct),))


@jax.custom_vjp
def _known_inverse(n_mat, inv):
    return inv


_known_inverse.defvjp(lambda n_mat, inv: (inv, inv), lambda inv, ct: (_inverse_pullback(inv, ct), jnp.zeros_like(inv)))


def _scan_chunk(r, lw, k, v, a, b, s0, inv=None, with_inverse=False):
    H, L, _ = r.shape
    row = lax.broadcasted_iota(jnp.int32, (H, L, L), 1)
    col = lax.broadcasted_iota(jnp.int32, (H, L, L), 2)
    incl, strict = col <= row, col < row
    cs = _running_sum(lw)
    cs_end = cs[:, L - 1:L, :]
    p, p_inv = jnp.exp(cs), jnp.exp(-cs)
    at, bt, kt, rt = a * jnp.exp(cs - lw), b * p_inv, k * p_inv, r * p
    a_ab = jnp.where(strict, _dot(at, bt, _BNT), 0.0)
    a_ak = jnp.where(strict, _dot(at, kt, _BNT), 0.0)
    a_rb = jnp.where(incl, _dot(rt, bt, _BNT), 0.0)
    a_rk = jnp.where(incl, _dot(rt, kt, _BNT), 0.0)
    inv = _unit_lower_inverse(a_ab) if inv is None else _known_inverse(a_ab, inv)
    u = _dot(inv, _dot(at, s0, _BNT) + _dot(a_ak, v, _BNN), _BNN)
    y = _dot(rt, s0, _BNT) + _dot(a_rb, u, _BNN) + _dot(a_rk, v, _BNN)
    to_end = jnp.exp(cs_end - cs)
    s1 = s0 * jnp.exp(cs_end) + _dot(u, b * to_end, _BTN) + _dot(v, k * to_end, _BTN)
    return (y, s1, inv) if with_inverse else (y, s1)


def _split_heads(x):
    return jnp.stack([x[:, R_HEAD * h:R_HEAD * (h + 1)] for h in range(R_HEADS)])


def _merge_heads(x):
    return jnp.concatenate([x[h] for h in range(R_HEADS)], axis=1)


def _scan_fwd(r, lw, k, v, a, b, rider):
    T = r.shape[0]
    H, N, L = R_HEADS, R_HEAD, SCAN_CHUNK
    nc = T // L
    ride_shape, ride_sems, _ = rider[1](rider[0])

    def body(r_ref, lw_ref, k_ref, v_ref, a_ref, b_ref, ride_in, y_ref, st_ref, inv_ref, ride_out, s_ref, *sem_refs):
        _ride(rider, pl.program_id(0), nc, ride_in, ride_out, sem_refs)

        @pl.when(pl.program_id(0) == 0)
        def _():
            s_ref[...] = jnp.zeros_like(s_ref)

        s0 = s_ref[...]
        st_ref[0] = s0
        y, s1, inv = _scan_chunk(*[_split_heads(t[...]) for t in (r_ref, lw_ref, k_ref, v_ref, a_ref, b_ref)], s0, with_inverse=True)
        y_ref[...] = _merge_heads(y)
        inv_ref[0] = inv
        s_ref[...] = s1

    blk = pl.BlockSpec((L, R_WIDTH), lambda c: (c, 0))
    per_chunk = pl.BlockSpec((1, H, N, N), lambda c: (c, 0, 0, 0))
    return pl.pallas_call(
        body, name="scan_fwd", grid=(nc,), in_specs=[blk] * 6 + [_HBM], out_specs=[blk, per_chunk, per_chunk, _HBM],
        out_shape=[jax.ShapeDtypeStruct((T, R_WIDTH), F32)] + [jax.ShapeDtypeStruct((nc, H, N, N), F32)] * 2 + [ride_shape],
        scratch_shapes=[pltpu.VMEM((H, N, N), F32)] + list(ride_sems),
        compiler_params=_cparams(("arbitrary",)),
    )(r, lw, k, v, a, b, rider[0])


def _scan_bwd(r, lw, k, v, a, b, states, inverses, dy, rider):
    T = r.shape[0]
    H, N, L = R_HEADS, R_HEAD, SCAN_CHUNK
    nc = T // L
    ride_shape, ride_sems, _ = rider[1](rider[0])

    def body(r_ref, lw_ref, k_ref, v_ref, a_ref, b_ref, st_ref, inv_ref, dy_ref, ride_in,
             dr_ref, dlw_ref, dk_ref, dv_ref, da_ref, db_ref, ride_out, ds_ref, *sem_refs):
        _ride(rider, pl.program_id(0), nc, ride_in, ride_out, sem_refs)

        @pl.when(pl.program_id(0) == 0)
        def _():
            ds_ref[...] = jnp.zeros_like(ds_ref)

        args = [_split_heads(t[...]) for t in (r_ref, lw_ref, k_ref, v_ref, a_ref, b_ref)] + [st_ref[0]]
        inv = inv_ref[0]
        _, pull = jax.vjp(lambda *xs: _scan_chunk(*xs, inv=inv), *args)
        grads = pull((_split_heads(dy_ref[...]), ds_ref[...]))
        for o_ref, g_ in zip((dr_ref, dlw_ref, dk_ref, dv_ref, da_ref, db_ref), grads[:6], strict=True):
            o_ref[...] = _merge_heads(g_)
        ds_ref[...] = grads[6]

    blk = pl.BlockSpec((L, R_WIDTH), lambda c: (nc - 1 - c, 0))
    per_chunk = pl.BlockSpec((1, H, N, N), lambda c: (nc - 1 - c, 0, 0, 0))
    return pl.pallas_call(
        body, name="scan_bwd", grid=(nc,), in_specs=[blk] * 6 + [per_chunk, per_chunk, blk, _HBM], out_specs=[blk] * 6 + [_HBM],
        out_shape=[jax.ShapeDtypeStruct((T, R_WIDTH), F32)] * 6 + [ride_shape],
        scratch_shapes=[pltpu.VMEM((H, N, N), F32)] + list(ride_sems),
        compiler_params=_cparams(("arbitrary",)),
    )(r, lw, k, v, a, b, states, inverses, dy, rider[0])


def _place():
    x, y, c = lax.axis_index("x"), lax.axis_index("y"), lax.axis_index("c")
    return x, y, c


def _gather_def(block):
    R, C = block.shape

    def phases(x_ref, out_ref, send_sems, recv_sems, local_sem):
        x, y, c = _place()
        me, sibling = (x, y, c), (x, y, 1 - c)
        chips = [(1 - x, y), (x, 1 - y), (1 - x, 1 - y)]

        def slot(px, py, pc):
            return out_ref.at[4 * px + 2 * py + pc]

        def copy(k, blk, to, src=None):
            return pltpu.make_async_remote_copy(
                src_ref=slot(*blk) if src is None else src, dst_ref=slot(*blk),
                send_sem=send_sems.at[k], recv_sem=recv_sems.at[k], device_id=to, device_id_type=_MESH_ID)

        mine = pltpu.make_async_copy(x_ref, slot(*me), local_sem)
        first = [copy(0, me, sibling, src=x_ref)]
        first += [copy(1 + j, me, (*chip, c), src=x_ref) for j, chip in enumerate(chips)]
        passed = [copy(4 + j, (*chip, c), sibling) for j, chip in enumerate(chips)]

        def begin():
            mine.start()
            for cp in first:
                cp.start()

        def forward():
            for j, chip in enumerate(chips):
                copy(1 + j, (*chip, c), me).wait_recv()
                passed[j].start()

        def finish():
            copy(0, sibling, me).wait_recv()
            for j, chip in enumerate(chips):
                copy(4 + j, (*chip, 1 - c), me).wait_recv()
            for cp in first + passed:
                cp.wait_send()
            mine.wait()

        return [begin, forward, finish]

    sems = [pltpu.SemaphoreType.DMA((7,)), pltpu.SemaphoreType.DMA((7,)), pltpu.SemaphoreType.DMA]
    return jax.ShapeDtypeStruct((_N_DEV, R, C), block.dtype), sems, phases


def _sibling_def(blocks):
    _, R, C = blocks.shape

    def phases(x_ref, out_ref, send_sems, recv_sems):
        x, y, c = _place()
        copies = [pltpu.make_async_remote_copy(
            src_ref=x_ref.at[2 * q + (1 - c)], dst_ref=out_ref.at[q], send_sem=send_sems.at[q], recv_sem=recv_sems.at[q],
            device_id=(x, y, 1 - c), device_id_type=_MESH_ID) for q in range(4)]

        def begin():
            for cp in copies:
                cp.start()

        def finish():
            for cp in copies:
                cp.wait()

        return [begin, finish]

    return jax.ShapeDtypeStruct((4, R, C), blocks.dtype), [pltpu.SemaphoreType.DMA((4,)), pltpu.SemaphoreType.DMA((4,))], phases


def _chips_def(partials):
    def phases(x_ref, out_ref, send_sems, recv_sems, local_sem):
        x, y, c = _place()
        my_chip = 2 * x + y
        mine = pltpu.make_async_copy(x_ref.at[my_chip], out_ref.at[my_chip], local_sem)
        copies = []
        for rel in range(1, 4):
            px, py = (1 - x if (rel >> 1) & 1 else x), (1 - y if rel & 1 else y)
            copies.append(pltpu.make_async_remote_copy(
                src_ref=x_ref.at[2 * px + py], dst_ref=out_ref.at[my_chip],
                send_sem=send_sems.at[rel - 1], recv_sem=recv_sems.at[rel - 1],
                device_id=(px, py, c), device_id_type=_MESH_ID))

        def begin():
            mine.start()
            for cp in copies:
                cp.start()

        def finish():
            for cp in copies:
                cp.wait()
            mine.wait()

        return [begin, finish]

    sems = [pltpu.SemaphoreType.DMA((3,)), pltpu.SemaphoreType.DMA((3,)), pltpu.SemaphoreType.DMA]
    return jax.ShapeDtypeStruct(partials.shape, partials.dtype), sems, phases


_HBM = pl.BlockSpec(memory_space=pltpu.HBM)


def _exchange(name, array, definition):
    out_shape, sems, phases = definition(array)

    def body(x_ref, out_ref, *sem_refs):
        for phase in phases(x_ref, out_ref, *sem_refs):
            phase()

    return pl.pallas_call(body, name=name, in_specs=[_HBM], out_specs=_HBM, out_shape=out_shape, scratch_shapes=sems)(array)


def _ride(rider, step, nsteps, x_ref, out_ref, sem_refs):
    array, definition, fractions = rider
    for phase, frac in zip(definition(array)[2](x_ref, out_ref, *sem_refs), fractions, strict=True):
        pl.when(step == min(int(frac * nsteps), nsteps - 1))(phase)


def _all_gather(name, block):
    return _exchange(name, block, _gather_def)


def _chip_partials(name, blocks, from_sibling, tile, out_dtype):
    _, R, C = blocks.shape

    def body(x_ref, s_ref, o_ref):
        c = lax.axis_index("c")
        for q in range(4):
            o_ref[q] = (x_ref[2 * q + c] + s_ref[q]).astype(o_ref.dtype)

    return pl.pallas_call(
        body, name=name, grid=(R // tile,),
        in_specs=[pl.BlockSpec((_N_DEV, tile, C), lambda i: (0, i, 0)), pl.BlockSpec((4, tile, C), lambda i: (0, i, 0))],
        out_specs=pl.BlockSpec((4, tile, C), lambda i: (0, i, 0)), out_shape=jax.ShapeDtypeStruct((4, R, C), out_dtype),
        compiler_params=_cparams(("parallel",)),
    )(blocks, from_sibling)


def _sum_leading(name, x, tile=None):
    n, R, C = x.shape
    tile = tile or _pick(R, (512, 256, 128, 64, 32, 16, 8))

    def body(x_ref, o_ref):
        acc = x_ref[0].astype(F32)
        for k in range(1, n):
            acc = acc + x_ref[k].astype(F32)
        o_ref[...] = acc

    return pl.pallas_call(
        body, name=name, grid=(R // tile,), in_specs=[pl.BlockSpec((n, tile, C), lambda i: (0, i, 0))],
        out_specs=pl.BlockSpec((tile, C), lambda i: (i, 0)), out_shape=jax.ShapeDtypeStruct((R, C), F32),
        compiler_params=_cparams(("parallel",)),
    )(x)


def _adamw_update(w_, g_, m_, v_):
    m2 = ADAM_B1 * m_ + (1.0 - ADAM_B1) * g_
    v2 = ADAM_B2 * v_ + (1.0 - ADAM_B2) * jnp.square(g_)
    m_hat = m2 / (1.0 - ADAM_B1 ** ADAM_STEP)
    v_hat = v2 / (1.0 - ADAM_B2 ** ADAM_STEP)
    delta = -ADAM_LR * (m_hat / (jnp.sqrt(v_hat) + ADAM_EPS) + ADAM_WD * w_)
    return delta, m2, v2


def _adamw(name, w, g, m, v):
    R, C = w.shape
    tile = _pick(R, (256, 128, 64, 32, 16, 8))
    outs, _ = _rowwise(name, lambda tv, bv: (list(_adamw_update(*tv)), []), [w, g, m, v], [], [(C, F32)] * 3, [], tile)
    return outs


def _adamw_many(name, ws, gs, ms, vs):
    n = len(ws)

    def body(*refs):
        ins, outs = refs[:4 * n], refs[4 * n:]
        for i in range(n):
            res = _adamw_update(ins[i][...], ins[n + i][...], ins[2 * n + i][...], ins[3 * n + i][...])
            for j in range(3):
                outs[j * n + i][...] = res[j]

    out_shape = [jax.ShapeDtypeStruct(w.shape, F32) for w in ws] * 3
    res = pl.pallas_call(body, name=name, out_shape=out_shape, compiler_params=_cparams())(*ws, *gs, *ms, *vs)
    return res[:n], res[n:2 * n], res[2 * n:]


def _pack_rows(arrs, lanes=128, row_mult=8):
    flat, places, off = [], [], 0
    for a_ in arrs:
        n = a_.size
        flat.append(a_.reshape(-1).astype(F32))
        places.append((off, n, a_.shape))
        off += n
    total = -(-off // (lanes * row_mult)) * (lanes * row_mult)
    if total > off:
        flat.append(jnp.zeros((total - off,), F32))
    return jnp.concatenate(flat).reshape(total // lanes, lanes), places


def _unpack_rows(packed, places):
    flat = packed.reshape(-1)
    return [flat[o:o + n].reshape(s) for (o, n, s) in places]


_WEIGHTS = ['w_ada', 'b_ada', 'w_in', 'b_in', 'g_ln_v', 'b_ln_v', 'w_spatial', 'b_spatial', 'mu_shift', 'w0', 'w_decay_up', 'a0',
            'w_aaa_up', 'w_gate_up', 'k_k', 'k_a', 'r_k', 'gn_gain', 'gn_bias', 'w_branch_a', 'w_branch_b', 'w_out', 'b_out',
            'ln1_g', 'ln1_b', 'w_ff1', 'b_ff1', 'w_ff2', 'b_ff2', 'ln2_g', 'ln2_b']
_BIG = {'w_ff1': (0, 512), 'w_ff2': (512, 512), 'w_out': (1024, 128), 'w_branch_a': (1152, 64), 'w_branch_b': (1216, 64), 'w_in': (1280, 640)}
_LATER_ROWS = 1280
_CUT_BY_COLS = ('w_ff1', 'w_in', 'w_branch_a', 'w_branch_b')
IN_SHARD = IN_COLS // _N_DEV
_LORA = {'w_decay_up': (0, LORA_W), 'w_aaa_up': (LORA_W, LORA_A), 'w_gate_up': (LORA_W + LORA_A, LORA_G)}
_COMM_DTYPE = jnp.bfloat16


def _pad_rows(a, rows):
    return jnp.pad(a, ((0, rows - a.shape[0]),) + ((0, 0),) * (a.ndim - 1))


def _pack_big(shards):
    blocks = []
    for n, (_, rows) in _BIG.items():
        a = shards[n].T if n in _CUT_BY_COLS else shards[n]
        blocks.append(_pad_rows(a.reshape(-1, D_MODEL), rows))
    return jnp.concatenate(blocks, axis=0)


def _unpack_big(block, like):
    out = {}
    for n, (r0, _) in _BIG.items():
        rr, cc = like[n].shape
        if n in _CUT_BY_COLS:
            out[n] = block[r0:r0 + rr * cc // D_MODEL].reshape(cc, rr).T
        else:
            out[n] = block[r0:r0 + rr]
    return out


def _to_padded(a, axis):
    g_end = 2 * G_WIDTH
    r_end = g_end + RW_USED
    take = lambda lo, hi: lax.slice_in_dim(a, lo, hi, axis=axis)
    zshape = list(a.shape)
    zshape[axis] = RW_COLS - RW_USED
    return jnp.concatenate([take(r_end, IN_COLS), take(g_end, r_end), jnp.zeros(zshape, a.dtype), take(0, g_end)], axis=axis)


def _from_padded(a, axis):
    take = lambda lo, hi: lax.slice_in_dim(a, lo, hi, axis=axis)
    return jnp.concatenate([take(2 * D_MODEL + RW_COLS, P_COLS), take(2 * D_MODEL, 2 * D_MODEL + RW_USED), take(0, 2 * D_MODEL)], axis=axis)


def _step(p, m, v, x, c, target):
    T = x.shape[0]
    xi, yi, ci = _place()
    me = 4 * xi + 2 * yi + ci
    tile = 256

    lane = jnp.arange(R_WIDTH)
    gsum = (lane[:, None] // R_HEAD == lane[None, :] // R_HEAD).astype(F32)
    expand = (jnp.arange(128)[:, None] == (lane[None, :] // (G_WIDTH // 8))).astype(F32)

    (c_act,), _ = _rowwise("silu_c", lambda tv, bv: ([tv[0] * _sigmoid(tv[0])], []), [c], [], [(D_MODEL, F32)], [], 1)
    small, places = _pack_rows([c_act, p['w_decay_up'], p['w_aaa_up'], p['w_gate_up']])
    small_all = _all_gather("gather_small", small)
    per_dev = [_unpack_rows(small_all[d], places) for d in range(_N_DEV)]
    c_act_all = _pad_rows(jnp.concatenate([pd[0] for pd in per_dev], axis=0), 16)
    lora_full = {n: jnp.concatenate([pd[i + 1] for pd in per_dev], axis=1) for i, n in enumerate(_LORA)}
    lora_pad = {n: jnp.zeros((LORA_PAD, R_WIDTH), F32).at[r0:r0 + nr].set(lora_full[n]) for n, (r0, nr) in _LORA.items()}

    big_names = list(_BIG)
    my_rows = _pack_big(p).astype(_MXU_DTYPE)
    w_in_all = _all_gather("gather_w_in", my_rows[_LATER_ROWS:])
    w_in_t = _to_padded(w_in_all[:, :IN_SHARD].reshape(IN_COLS, D_MODEL), 0)
    b_in_p = _to_padded(p['b_in'], 1)
    mu_p = jnp.concatenate([p['mu_shift'], jnp.zeros((1, RW_COLS - RW_USED), F32)], axis=1)

    b_ada_mine = lax.dynamic_slice(p['b_ada'], (0, me * 768), (1, 768))
    mod_cols = _mm("ada_mod", c_act_all, p['w_ada'], "nn", bias=b_ada_mine)
    mod_all = _all_gather("gather_mod", mod_cols)
    mod = lax.dynamic_index_in_dim(mod_all, me, axis=1, keepdims=False).reshape(1, 6 * D_MODEL)
    sh1, sc1, gt1, sh2, sc2, gt2 = [mod[:, i * D_MODEL:(i + 1) * D_MODEL] for i in range(6)]

    (h,), _ = _rowwise("modulate1", lambda tv, bv: (_seg_modulate(tv, bv), []), [x], [sc1, sh1], [(D_MODEL, _MXU_DTYPE)], [], tile)
    proj = _mm("in_proj", h, w_in_t, "nt", bias=b_in_p)
    ws = p['w_spatial']
    b_tg = jnp.zeros((G_CHUNK, 128), F32).at[:, :8].set(p['b_spatial'].T)
    gmlp_b = [p['g_ln_v'], p['b_ln_v'], ws, b_tg, expand]
    z_gmlp = (proj, 2 * G_WIDTH, 4)
    (ya,), _ = _rowwise("gmlp", lambda tv, bv: (_seg_gmlp(tv, bv), []), [z_gmlp], gmlp_b, [(G_WIDTH, F32)], [], G_CHUNK)
    z_rw = (proj, RW_COLS, 1)
    z_rw_halo = ("halo", proj, RW_COLS, 1)
    pre_b = [mu_p, p['w0'], lora_pad['w_decay_up'], p['a0'], lora_pad['w_aaa_up'], lora_pad['w_gate_up'], p['k_k'], p['k_a'], gsum]

    def pre_fwd(tv, bv):
        z_t, halo_t = tv
        return _seg_rwkv_pre([z_t, _shift_down(z_t, halo_t, pl.program_id(0) == 0)], bv), []

    pre_out, _ = _rowwise("rwkv_pre", pre_fwd, [z_rw, z_rw_halo], pre_b, [(R_WIDTH, F32)] * 7, [], tile)
    r_, lw_, k2_, v_, a_, b_, g_ = pre_out
    scan_in = (r_, lw_, k2_, v_, a_, b_)
    y_, states, inverses, later_all = _scan_fwd(*scan_in, rider=(my_rows[:_LATER_ROWS], _gather_def, (0.0, 0.875, 1.0)))

    def whole(n, rows):
        r0 = _BIG[n][0]
        return later_all[:, r0:r0 + rows].reshape(_N_DEV * rows, D_MODEL)

    w_ff1_t, w_ff2, w_out = whole('w_ff1', 512), whole('w_ff2', 512), whole('w_out', 128)
    w_ba_t = whole('w_branch_a', 64).reshape(D_MODEL, G_WIDTH)
    w_bb_t = whole('w_branch_b', 64).reshape(D_MODEL, R_WIDTH)
    post_b = [p['r_k'].reshape(1, R_WIDTH), p['gn_gain'], p['gn_bias'], gsum]
    (yb,), _ = _rowwise("rwkv_post", lambda tv, bv: (_seg_rwkv_post(tv, bv), []), [y_, r_, k2_, v_, g_], post_b, [(R_WIDTH, F32)], [], tile)
    pa = _mm("branch_a", ya, w_ba_t, "nt")
    pb = _mm("branch_b", yb, w_bb_t, "nt")
    gates = [(proj, D_MODEL, 0), (proj, D_MODEL, 1)]
    (merged,), _ = _rowwise("merge", lambda tv, bv: (_seg_merge(tv, bv), []), gates + [pa, pb], [], [(D_MODEL, _MXU_DTYPE)], [], tile)
    mix = _mm("out_proj", merged, w_out, "nn", bias=p['b_out'])
    mid_b = [gt1, p['ln1_g'], p['ln1_b'], sc2, sh2]
    (h1, h2in), _ = _rowwise("mid", lambda tv, bv: (_seg_mid(tv, bv), []), [x, mix], mid_b, [(D_MODEL, F32), (D_MODEL, _MXU_DTYPE)], [], tile)
    f1, act = _mm("ff1", h2in, w_ff1_t, "nt", bias=p['b_ff1'], epi=lambda t: _seg_relu2([t], [])[0], epi_dtype=_MXU_DTYPE)
    ff = _mm("ff2", act, w_ff2, "nn", bias=p['b_ff2'])

    def loss_fn(tv, bv):
        h1_t, ff_t, tgt = tv
        val, grads = jax.value_and_grad(lambda a0_, a1_, b0_, b1_, b2_: _seg_loss([a0_, a1_, tgt], [b0_, b1_, b2_]), argnums=(0, 1, 2, 3, 4))(h1_t, ff_t, *bv)
        return [grads[0], grads[1]], [grads[2], grads[3], grads[4], jnp.sum(grads[1], axis=0, keepdims=True), jnp.full((1, 128), val, F32)]

    (dh1_a, dff), (d_gt2, d_ln2_g, d_ln2_b, d_b_ff2, loss_row) = _rowwise(
        "loss", loss_fn, [h1, ff, target], [gt2, p['ln2_g'], p['ln2_b']], [(D_MODEL, F32), (D_MODEL, _MXU_DTYPE)], [(1, D_MODEL)] * 4 + [(1, 128)], tile)
    loss = lax.psum(loss_row[0, 0], ("x", "y", "c"))

    g = {}
    g['ln2_g'], g['ln2_b'], g['b_ff2'] = d_ln2_g, d_ln2_b, d_b_ff2
    gw = {}
    gw['w_ff2'] = _mm("g_w_ff2", act, dff, "tn")
    df1 = _mm("d_act", dff, w_ff2, "nt", beside=f1, epi=lambda d_, f_: d_ * (2.0 * jnp.maximum(f_, 0.0)), epi_dtype=_MXU_DTYPE)
    g['b_ff1'] = _mm("g_b_ff1", jnp.ones((16, T), _MXU_DTYPE), df1, "nn")[0:1]
    gw['w_ff1'] = _mm("g_w_ff1", df1, h2in, "tn")
    dh2in = _mm("d_h2in", df1, w_ff1_t, "nn")
    (dx_a, dmix), (d_gt1, g['ln1_g'], g['ln1_b'], d_sc2, d_sh2), (g['b_out'],) = _rowwise_vjp(
        "mid_bwd", _seg_mid, [x, mix], mid_b, [dh1_a, dh2in], tile, [0, 1], [0, 1, 2, 3, 4], t_dtypes=[F32, _MXU_DTYPE], colsum=[1])
    gw['w_out'] = _mm("g_w_out", merged, dmix, "tn")
    dmerged = _mm("d_merged", dmix, w_out, "nt")
    (dga, dgb, dpa, dpb), _, (cs_ga, cs_gb) = _rowwise_vjp(
        "merge_bwd", _seg_merge, gates + [pa, pb], [], [dmerged], tile, [0, 1, 2, 3], [], t_dtypes=[_MXU_DTYPE] * 4, colsum=[0, 1])
    gw['w_branch_a'] = _mm("g_w_branch_a", dpa, ya, "tn")
    gw['w_branch_b'] = _mm("g_w_branch_b", dpb, yb, "tn")
    dya = _mm("d_ya", dpa, w_ba_t, "nn")
    dyb = _mm("d_yb", dpb, w_bb_t, "nn")
    def send_rows(names):
        parts = []
        for n in names:
            per_dev = gw[n].reshape(_N_DEV, -1, D_MODEL)
            parts.append(jnp.pad(per_dev, ((0, 0), (0, _BIG[n][1] - per_dev.shape[1]), (0, 0))))
        return jnp.concatenate(parts, axis=1) if len(parts) > 1 else parts[0]

    send_early = send_rows(big_names[:-1])
    (dy, dr1, dk1, dv1, dg_), (d_r_k, g['gn_gain'], g['gn_bias']), _, sibling_early = _rowwise_vjp(
        "rwkv_post_bwd", _seg_rwkv_post, [y_, r_, k2_, v_, g_], post_b, [dyb], tile, [0, 1, 2, 3, 4], [0, 1, 2],
        rider=(send_early, _sibling_def, (0.0, 1.0)))
    g['r_k'] = d_r_k
    partials_early = _chip_partials("chip_partials_early", send_early, sibling_early, 128, _COMM_DTYPE)
    dr2, dlw, dk2, dv2, da, db, landed_early = _scan_bwd(*scan_in, states, inverses, dy, rider=(partials_early, _chips_def, (0.0, 1.0)))
    pre_tile = 128
    last_step = T // pre_tile - 1

    def pre_prep(prim):
        z_t, halo_t = prim
        return [z_t, _shift_down(z_t, halo_t, pl.program_id(0) == last_step)]

    def pre_finish(dts, sc):
        dz_direct, dprev = dts
        (row_after,) = sc

        @pl.when(pl.program_id(0) == 0)
        def _():
            row_after[...] = jnp.zeros_like(row_after)

        dz = dz_direct + _shift_up(dprev, row_after[...])
        row_after[...] = dprev[:_HALO]
        return [dz]

    (dz_rw_all,), (d_mu, g['w0'], d_wd, g['a0'], d_wa, d_wg, g['k_k'], g['k_a']), (cs_rw,) = _rowwise_vjp(
        "rwkv_pre_bwd", _seg_rwkv_pre, [z_rw, z_rw_halo], pre_b, [[dr1, dr2], dlw, [dk1, dk2], [dv1, dv2], da, db, dg_], pre_tile,
        [0, 1], [0, 1, 2, 3, 4, 5, 6, 7], t_dtypes=[_MXU_DTYPE], colsum=[0], prep=pre_prep, finish=pre_finish,
        out_widths=[RW_COLS], reverse=True, scratch=[(_HALO, RW_COLS)])
    g['mu_shift'] = d_mu[:, :RW_USED]
    for n, d_ in (('w_decay_up', d_wd), ('w_aaa_up', d_wa), ('w_gate_up', d_wg)):
        r0, nr = _LORA[n]
        g[n] = d_[r0:r0 + nr]
    (dz_g,), (g['g_ln_v'], g['b_ln_v'], g['w_spatial'], d_b_tg), (cs_g,) = _rowwise_vjp(
        "gmlp_bwd", _seg_gmlp, [z_gmlp], gmlp_b, [dya], G_CHUNK, [0], [0, 1, 2, 3], t_dtypes=[_MXU_DTYPE], colsum=[0])
    g['b_spatial'] = d_b_tg[:, :8].T
    dproj = jnp.concatenate([dga, dgb, dz_rw_all, dz_g], axis=1)
    g['b_in'] = _from_padded(jnp.concatenate([cs_ga, cs_gb, cs_rw, cs_g], axis=1), 1)
    gw['w_in'] = _from_padded(_mm("g_w_in", dproj, h, "tn"), 0)
    dh = _mm("d_h", dproj, w_in_t, "nn")

    def mod1_bwd(tv, bv):
        x_t, dh_t, dxa_t = tv
        (sc,) = bv
        return [dxa_t + dh_t * (1.0 + sc)], [jnp.sum(dh_t * x_t, axis=0, keepdims=True), jnp.sum(dh_t, axis=0, keepdims=True)]

    (grad_x,), (d_sc1, d_sh1) = _rowwise("modulate1_bwd", mod1_bwd, [x, dh, dx_a], [sc1], [(D_MODEL, F32)], [(1, D_MODEL)] * 2, tile)

    dmod = jnp.concatenate([d_sh1, d_sc1, d_gt1, d_sh2, d_sc2, d_gt2], axis=1).reshape(6 * D_MODEL // 128, 128)
    dmod_all = _all_gather("gather_dmod", dmod)
    g['b_ada'] = _sum_leading("sum_dmod", dmod_all).reshape(1, 6 * D_MODEL)
    dmod_mine = lax.dynamic_slice(dmod_all.reshape(_N_DEV, 6 * D_MODEL), (0, me * 768), (_N_DEV, 768))
    g_w_ada = _mm("g_w_ada", c_act_all, _pad_rows(dmod_mine, 16), "tn")

    small_names = [n for n in _WEIGHTS if n not in _BIG and n not in ('w_ada', 'b_ada')]
    packed_g, g_places = _pack_rows([g[n] for n in small_names], row_mult=256)
    small_sum = _unpack_rows(_sum_leading("sum_small", _all_gather("gather_small_grads", packed_g)), g_places)
    for n, t in zip(small_names, small_sum, strict=True):
        g[n] = t
    for n in _LORA:
        g[n] = lax.dynamic_slice(g[n], (0, me * R_HEAD), (g[n].shape[0], R_HEAD))
    g['w_ada'] = g_w_ada

    send_late = send_rows(big_names[-1:])
    sibling_late = _exchange("pair_exchange_late", send_late, _sibling_def)
    partials_late = _chip_partials("chip_partials_late", send_late, sibling_late, 128, _COMM_DTYPE)
    landed_late = _exchange("chip_exchange_late", partials_late, _chips_def)
    summed = jnp.concatenate([_sum_leading("sum_early", landed_early), _sum_leading("sum_late", landed_late)], axis=0)
    g.update(_unpack_big(summed, p))

    delta, new_m, new_v = {}, {}, {}
    own_call = ['w_ada'] + big_names
    for n in own_call:
        delta[n], new_m[n], new_v[n] = _adamw("adamw_" + n, p[n], g[n], m[n], v[n])
    rest = [n for n in _WEIGHTS if n not in own_call]
    outs = _adamw_many("adamw_rest", *[[d[n].reshape(p[n].shape) for n in rest] for d in (p, g, m, v)])
    for d, o in zip((delta, new_m, new_v), outs, strict=True):
        d.update(zip(rest, o, strict=True))
    return loss, grad_x, g, delta, new_m, new_v


def kernel(x, c, w_ada, b_ada, w_in, b_in, g_ln_v, b_ln_v, w_spatial, b_spatial, mu_shift, w0, w_decay_up, a0, w_aaa_up, w_gate_up, k_k, k_a, r_k, gn_gain, gn_bias, w_branch_a, w_branch_b, w_out, b_out, ln1_g, ln1_b, w_ff1, b_ff1, w_ff2, b_ff2, ln2_g, ln2_b, loss_target, m_w_ada, m_b_ada, m_w_in, m_b_in, m_g_ln_v, m_b_ln_v, m_w_spatial, m_b_spatial, m_mu_shift, m_w0, m_w_decay_up, m_a0, m_w_aaa_up, m_w_gate_up, m_k_k, m_k_a, m_r_k, m_gn_gain, m_gn_bias, m_w_branch_a, m_w_branch_b, m_w_out, m_b_out, m_ln1_g, m_ln1_b, m_w_ff1, m_b_ff1, m_w_ff2, m_b_ff2, m_ln2_g, m_ln2_b, v_w_ada, v_b_ada, v_w_in, v_b_in, v_g_ln_v, v_b_ln_v, v_w_spatial, v_b_spatial, v_mu_shift, v_w0, v_w_decay_up, v_a0, v_w_aaa_up, v_w_gate_up, v_k_k, v_k_a, v_r_k, v_gn_gain, v_gn_bias, v_w_branch_a, v_w_branch_b, v_w_out, v_b_out, v_ln1_g, v_ln1_b, v_w_ff1, v_b_ff1, v_w_ff2, v_b_ff2, v_ln2_g, v_ln2_b):
    given = dict(locals())
    shapes = {n: given[n].shape for n in _WEIGHTS}
    def two_d(a_):
        a_ = a_[0]
        return a_.reshape(1, -1) if a_.ndim == 1 else a_
    p = {n: two_d(given[n]) for n in _WEIGHTS}
    m = {n: two_d(given["m_" + n]) for n in _WEIGHTS}
    v = {n: two_d(given["v_" + n]) for n in _WEIGHTS}
    loss, grad_x, g, delta, new_m, new_v = _step(p, m, v, x[0], c, loss_target[0])
    outs = [loss, grad_x[None]]
    for d in (g, delta, new_m, new_v):
        outs += [d[n].reshape(shapes[n]) for n in _WEIGHTS]
    return tuple(outs)
```

```python
import functools

import jax
import jax.numpy as jnp
from jax import lax
from jax.experimental import pallas as pl
from jax.experimental.pallas import tpu as pltpu

F32 = jnp.float32
_MXU_DTYPE = jnp.bfloat16
_HI = lax.Precision.HIGHEST
_VMEM_LIMIT = 48 * 1024 * 1024
_MESH_ID = pl.DeviceIdType.MESH
_N_DEV = 8

D_MODEL = 1024
G_WIDTH = 512
G_CHUNK = 128
R_WIDTH = 512
R_HEADS = 8
R_HEAD = 64
LORA_W, LORA_A, LORA_G = 32, 32, 96
D_FF = 4096
ALPHA = 2.0 ** 0.25
LN_EPS = 1e-5
GN_EPS = 64e-5
SCAN_CHUNK = 64
ADAM_LR, ADAM_B1, ADAM_B2, ADAM_EPS, ADAM_WD, ADAM_STEP = 0.001, 0.9, 0.999, 1e-08, 0.01, 10

P_COLS = 5120
RW_COLS = 2048
RW_USED = 3 * R_WIDTH + LORA_W + LORA_A + LORA_G
LORA_PAD = 256
IN_COLS = 2 * G_WIDTH + RW_USED + 2 * D_MODEL


def _cparams(sem=None, **kw):
    if sem is not None:
        kw["dimension_semantics"] = sem
    return pltpu.CompilerParams(vmem_limit_bytes=_VMEM_LIMIT, **kw)


def _dot(a, b, dims=(((1,), (0,)), ((), ())), hi=False):
    if hi:
        return lax.dot_general(a.astype(F32), b.astype(F32), dims, precision=_HI, preferred_element_type=F32)
    return lax.dot_general(a.astype(_MXU_DTYPE), b.astype(_MXU_DTYPE), dims, preferred_element_type=F32)


_NN = (((1,), (0,)), ((), ()))
_NT = (((1,), (1,)), ((), ()))
_TN = (((0,), (0,)), ((), ()))


def _pick(n, pref):
    for t in pref:
        if n % t == 0:
            return t
    return n


def _mm(name, a, b, mode, bias=None, out_dtype=F32, epi=None, epi_dtype=None, beside=None, rider=None, tm=None, tn=None, tk=None):
    if mode == "nn":
        (M, K), (_, N) = a.shape, b.shape
    elif mode == "nt":
        (M, K), (N, _) = a.shape, b.shape
    else:
        (K, M), (_, N) = a.shape, b.shape
    tm = tm or _pick(M, (512, 256, 128, 64, 32, 16, 8))
    tn = tn or _pick(N, (1024, 512, 640, 384, 256, 128))
    tk = tk or _pick(K, (1024, 512, 256, 128))
    nk = K // tk
    dims = {"nn": _NN, "nt": _NT, "tn": _TN}[mode]
    a_spec = pl.BlockSpec((tk, tm), lambda i, j, k: (k, i)) if mode == "tn" else pl.BlockSpec((tm, tk), lambda i, j, k: (i, k))
    b_spec = pl.BlockSpec((tn, tk), lambda i, j, k: (j, k)) if mode == "nt" else pl.BlockSpec((tk, tn), lambda i, j, k: (k, j))
    o_spec = pl.BlockSpec((tm, tn), lambda i, j, k: (i, j))
    has_bias, has_beside = bias is not None, beside is not None
    two = epi is not None and not has_beside
    grid = (M // tm, N // tn, nk)
    ride_shape, ride_sems, _ = rider[1](rider[0]) if rider else (None, [], None)

    def body(*refs):
        refs = list(refs)
        n_in = 2 + has_bias + has_beside
        if rider:
            sem_refs = [refs.pop() for _ in ride_sems][::-1]
            ride_out = refs.pop(n_in + 1 + 1 + two)
            ride_in = refs.pop(n_in)
            step = (pl.program_id(0) * grid[1] + pl.program_id(1)) * grid[2] + pl.program_id(2)
            _ride(rider, step, grid[0] * grid[1] * grid[2], ride_in, ride_out, sem_refs)
        a_ref, b_ref = refs[0], refs[1]
        bias_ref = refs[2] if has_bias else None
        beside_ref = refs[n_in - 1] if has_beside else None
        outs = refs[n_in:]
        o_ref, acc_ref = outs[0], outs[-1]
        k = pl.program_id(2)

        @pl.when(k == 0)
        def _():
            acc_ref[...] = jnp.zeros_like(acc_ref)

        acc_ref[...] += _dot(a_ref[...], b_ref[...], dims)

        @pl.when(k == nk - 1)
        def _():
            res = acc_ref[...]
            if has_bias:
                res = res + bias_ref[...]
            if has_beside:
                o_ref[...] = epi(res, beside_ref[...]).astype(o_ref.dtype)
            else:
                o_ref[...] = res.astype(o_ref.dtype)
                if epi is not None:
                    outs[1][...] = epi(res).astype(outs[1].dtype)

    in_specs = [a_spec, b_spec]
    args = [a, b]
    if has_bias:
        in_specs.append(pl.BlockSpec((1, tn), lambda i, j, k: (0, j)))
        args.append(bias)
    if has_beside:
        in_specs.append(o_spec)
        args.append(beside)
    out_shape = [jax.ShapeDtypeStruct((M, N), epi_dtype if has_beside else out_dtype)]
    out_specs = [o_spec]
    if two:
        out_shape.append(jax.ShapeDtypeStruct((M, N), epi_dtype))
        out_specs.append(o_spec)
    if rider:
        in_specs.append(_HBM)
        args.append(rider[0])
        out_shape.append(ride_shape)
        out_specs.append(_HBM)
    res = pl.pallas_call(
        body, name=name, grid=grid, in_specs=in_specs, out_specs=out_specs, out_shape=out_shape,
        scratch_shapes=[pltpu.VMEM((tm, tn), F32)] + list(ride_sems),
        compiler_params=_cparams(("arbitrary",) * 3 if rider else ("parallel", "parallel", "arbitrary")),
    )(*args)
    return res if (two or rider) else res[0]


_HALO = 8


def _rowwise(name, fn, tiled, bcast, tiled_out, red_out, tile, reverse=False, scratch=(), rider=None):
    tiled = [t if isinstance(t, tuple) else (t, t.shape[1], 0) for t in tiled]
    T = next(t[0] for t in tiled if not isinstance(t[0], str)).shape[0]
    n = T // tile
    nt, nb, nto, nsc, nro = len(tiled), len(bcast), len(tiled_out), len(scratch), len(red_out)
    ride_shape, ride_sems, _ = rider[1](rider[0]) if rider else (None, [], None)

    def row_block(i):
        return n - 1 - i if reverse else i

    def body(*refs):
        refs = list(refs)
        if rider:
            sem_refs = [refs.pop() for _ in ride_sems][::-1]
            ride_out = refs.pop(nt + nb + 1 + nto + nro)
            ride_in = refs.pop(nt + nb)
            _ride(rider, pl.program_id(0), n, ride_in, ride_out, sem_refs)
        t_refs, b_refs = refs[:nt], refs[nt:nt + nb]
        to_refs, ro_refs = refs[nt + nb:nt + nb + nto], refs[nt + nb + nto:len(refs) - nsc]
        extra = (list(refs[len(refs) - nsc:]),) if nsc else ()
        touts, routs = fn([r[...] for r in t_refs], [r[...] for r in b_refs], *extra)
        for r, v in zip(to_refs, touts, strict=True):
            r[...] = v.astype(r.dtype)
        if ro_refs:
            i = pl.program_id(0)

            @pl.when(i == 0)
            def _():
                for r, v in zip(ro_refs, routs, strict=True):
                    r[...] = v.astype(F32)

            @pl.when(i > 0)
            def _():
                for r, v in zip(ro_refs, routs, strict=True):
                    r[...] += v.astype(F32)

    def whole(shape):
        nd = len(shape)
        return pl.BlockSpec(tuple(shape), lambda i: (0,) * nd)

    per_tile = tile // _HALO
    in_specs, arrays = [], []
    for t in tiled:
        if isinstance(t[0], str):
            _, arr, w, cb = t
            in_specs.append(pl.BlockSpec((_HALO, w), functools.partial(lambda i, cb: (jnp.maximum(row_block(i) * per_tile - 1, 0), cb), cb=cb)))
        else:
            arr, w, cb = t
            in_specs.append(pl.BlockSpec((tile, w), functools.partial(lambda i, cb: (row_block(i), cb), cb=cb)))
        arrays.append(arr)
    in_specs += [whole(b.shape) for b in bcast]
    out_specs = [pl.BlockSpec((tile, c), lambda i: (row_block(i), 0)) for (c, _) in tiled_out] + [whole(s) for s in red_out]
    out_shape = [jax.ShapeDtypeStruct((T, c), dt) for (c, dt) in tiled_out] + [jax.ShapeDtypeStruct(tuple(s), F32) for s in red_out]
    ride_args = []
    if rider:
        in_specs.append(_HBM)
        out_specs.append(_HBM)
        out_shape.append(ride_shape)
        ride_args = [rider[0]]
    res = pl.pallas_call(
        body, name=name, grid=(n,), in_specs=in_specs, out_specs=out_specs, out_shape=out_shape,
        scratch_shapes=[pltpu.VMEM(tuple(s), F32) for s in scratch] + list(ride_sems),
        compiler_params=_cparams(("arbitrary",)),
    )(*arrays, *bcast, *ride_args)
    if rider:
        return list(res[:nto]), list(res[nto:nto + nro]), res[-1]
    return list(res[:nto]), list(res[nto:])


def _rowwise_vjp(name, f, tiled, bcast, cts, tile, wrt_t, wrt_b, t_dtypes=None, colsum=(), prep=None, finish=None,
                 out_widths=None, reverse=False, scratch=(), rider=None):
    tiled = [t if isinstance(t, tuple) else (t, t.shape[1], 0) for t in tiled]
    npr = len(tiled)
    t_dtypes = t_dtypes or [F32] * len(wrt_t)
    groups = [c if isinstance(c, list) else [c] for c in cts]
    cts = [a_ for grp in groups for a_ in grp]

    def fn(tv, bv, sc=None):
        prim, flat_ct = tv[:npr], list(tv[npr:])
        if prep is not None:
            prim = prep(prim)
        ct = []
        for grp in groups:
            parts = [flat_ct.pop(0).astype(F32) for _ in grp]
            ct.append(functools.reduce(lambda p_, q_: p_ + q_, parts))

        def g(dt_vals, db_vals):
            full_t, full_b = list(prim), list(bv)
            for i, v in zip(wrt_t, dt_vals, strict=True):
                full_t[i] = v
            for j, v in zip(wrt_b, db_vals, strict=True):
                full_b[j] = v
            return f(full_t, full_b)

        outs, pull = jax.vjp(g, [prim[i].astype(F32) for i in wrt_t], [bv[j] for j in wrt_b])
        dts, dbs = pull([c.astype(o.dtype) for c, o in zip(ct, outs, strict=True)])
        if finish is not None:
            dts = finish(dts, sc)
        sums = [jnp.sum(dts[i].astype(F32), axis=0, keepdims=True) for i in colsum]
        return dts, list(dbs) + sums

    widths = out_widths or [tiled[i][1] for i in wrt_t]
    tiled_out = list(zip(widths, t_dtypes, strict=True))
    red_out = [bcast[j].shape for j in wrt_b] + [(1, widths[i]) for i in colsum]
    res = _rowwise(name, fn, tiled + list(cts), bcast, tiled_out, red_out, tile, reverse=reverse, scratch=scratch, rider=rider)
    dts, reds = res[0], res[1]
    nb = len(wrt_b)
    return (dts, reds[:nb], reds[nb:]) + tuple(res[2:])


def _layer_norm(x, g, b, eps):
    mu = jnp.mean(x, axis=-1, keepdims=True)
    xc = x - mu
    var = jnp.mean(xc * xc, axis=-1, keepdims=True)
    return xc * lax.rsqrt(var + eps) * g + b


def _gelu_tanh(x):
    return 0.5 * x * (1.0 + jnp.tanh(0.7978845608028654 * (x + 0.044715 * (x * x * x))))


def _sigmoid(x):
    return 1.0 / (1.0 + jnp.exp(-x))


def _seg_modulate(tv, bv):
    (x,), (sc, sh) = tv, bv
    return [x * (1.0 + sc) + sh]


def _seg_gmlp(tv, bv):
    (z,), (g_ln, b_ln, ws, b_tg, expand) = tv, bv
    bias_full = _dot(b_tg, expand, hi=True)
    zz = _gelu_tanh(z)
    u, v = zz[:, :G_WIDTH], zz[:, G_WIDTH:]
    v = _layer_norm(v, g_ln, b_ln, LN_EPS)
    row = lax.broadcasted_iota(jnp.int32, (G_CHUNK, G_CHUNK), 0)
    col = lax.broadcasted_iota(jnp.int32, (G_CHUNK, G_CHUNK), 1)
    causal = col <= row
    first_group = lax.broadcasted_iota(jnp.int32, (G_CHUNK, 128), 1) < 64
    parts = []
    for p in range(4):
        vp = v[:, 128 * p:128 * (p + 1)]
        s_even = _dot(jnp.where(causal, ws[2 * p], 0.0), vp)
        s_odd = _dot(jnp.where(causal, ws[2 * p + 1], 0.0), vp)
        parts.append(jnp.where(first_group, s_even, s_odd))
    s = jnp.concatenate(parts, axis=1) + bias_full
    return [u * s]


def _split2(x):
    hi = x.astype(_MXU_DTYPE)
    return hi, (x - hi.astype(F32)).astype(_MXU_DTYPE)


@jax.custom_vjp
def _group_sum(x, ones_blocks):
    hi, lo = _split2(x)
    return _dot(hi, ones_blocks) + _dot(lo, ones_blocks)


def _group_sum_fwd(x, ones_blocks):
    return _group_sum(x, ones_blocks), ones_blocks


def _group_sum_bwd(ones_blocks, ct):
    return _group_sum(ct, ones_blocks), jnp.zeros_like(ones_blocks)


_group_sum.defvjp(_group_sum_fwd, _group_sum_bwd)


def _shift_down(z, halo, is_first):
    _, W = z.shape
    rolled = pltpu.roll(z, 1, 0)
    before = jnp.where(is_first, 0.0, pltpu.roll(halo, 1, 0))
    top_row = lax.broadcasted_iota(jnp.int32, (_HALO, W), 0) == 0
    return jnp.concatenate([jnp.where(top_row, before, rolled[:_HALO]), rolled[_HALO:]], axis=0)


def _shift_up(d, after):
    tile, W = d.shape
    rolled = pltpu.roll(d, tile - 1, 0)
    last_row = lax.broadcasted_iota(jnp.int32, (_HALO, W), 0) == _HALO - 1
    bottom = jnp.where(last_row, pltpu.roll(after, _HALO - 1, 0), rolled[tile - _HALO:])
    return jnp.concatenate([rolled[:tile - _HALO], bottom], axis=0)


def _seg_rwkv_pre(tv, bv):
    (z, prev), (mu, w0, wd, a0, wa, wg, k_k, k_a, gsum) = tv, bv
    zs = z + (prev - z) * mu
    r, k, v = zs[:, 0:512], zs[:, 512:1024], zs[:, 1024:1536]
    zl = zs[:, 3 * R_WIDTH:3 * R_WIDTH + LORA_PAD]
    x = w0 + _dot(jnp.tanh(zl), wd)
    softplus = jnp.maximum(-x, 0.0) + jnp.log(1.0 + jnp.exp(-jnp.abs(x)))
    lw = -jnp.exp(-softplus - 0.5)
    a = _sigmoid(a0 + _dot(zl, wa))
    g = _dot(_sigmoid(zl), wg)
    kk = k * k_k
    nrm = jnp.sqrt(_group_sum(kk * kk, gsum))
    kk = kk / jnp.maximum(nrm, 1e-12)
    k2 = k * (1.0 + (a - 1.0) * k_a)
    return [r, lw, k2, v, -kk, kk * a, g]


def _seg_rwkv_post(tv, bv):
    (y, r, k2, v, g), (r_k, gain, bias, gsum) = tv, bv
    mu = _group_sum(y, gsum) * (1.0 / R_HEAD)
    yc = y - mu
    var = _group_sum(yc * yc, gsum) * (1.0 / R_HEAD)
    yn = yc * lax.rsqrt(var + GN_EPS) * gain + bias
    bonus = _group_sum(r * k2 * r_k, gsum) * v
    return [(yn + bonus) * g]


def _seg_merge(tv, bv):
    (ga, gb, pa, pb), () = tv, bv
    return [_sigmoid(ga) * pa + _sigmoid(gb) * pb]


def _seg_mid(tv, bv):
    (x, mix), (gt1, g1, b1, sc2, sh2) = tv, bv
    h1 = _layer_norm(ALPHA * x + gt1 * mix, g1, b1, LN_EPS)
    return [h1, h1 * (1.0 + sc2) + sh2]


def _seg_relu2(tv, bv):
    (f1,), () = tv, bv
    return [jnp.square(jnp.maximum(f1, 0.0))]


def _seg_loss(tv, bv):
    (h1, ff, target), (gt2, g2, b2) = tv, bv
    out = _layer_norm(ALPHA * h1 + gt2 * ff, g2, b2, LN_EPS)
    err = jnp.square(out - target)
    return 0.5 * jnp.sum(jnp.mean(err, axis=-1))


_BNN = (((2,), (1,)), ((0,), (0,)))
_BNT = (((2,), (2,)), ((0,), (0,)))
_BTN = (((1,), (1,)), ((0,), (0,)))


def _tri_dot(x, dims):
    H, L, _ = x.shape
    tri = (lax.broadcasted_iota(jnp.int32, (H, L, L), 2) <= lax.broadcasted_iota(jnp.int32, (H, L, L), 1)).astype(F32)
    hi, lo = _split2(x)
    return _dot(tri, hi, dims) + _dot(tri, lo, dims)


@jax.custom_vjp
def _running_sum(x):
    return _tri_dot(x, _BNN)


_running_sum.defvjp(lambda x: (_tri_dot(x, _BNN), None), lambda _, ct: (_tri_dot(ct, _BTN),))


def _inverse_pullback(inv, ct):
    return _dot(_dot(inv, ct, _BTN), inv, _BNT)


@jax.custom_vjp
def _unit_lower_inverse(n_mat):
    H, L, _ = n_mat.shape
    eye = lax.broadcasted_iota(jnp.int32, (H, L, L), 1) == lax.broadcasted_iota(jnp.int32, (H, L, L), 2)
    inv = jnp.where(eye, 1.0, 0.0) + n_mat
    pw = n_mat
    n = 2
    while n < L:
        pw = _dot(pw, pw, _BNN)
        inv = inv + _dot(inv, pw, _BNN)
        n *= 2
    return inv


def _unit_lower_inverse_fwd(n_mat):
    inv = _unit_lower_inverse(n_mat)
    return inv, inv


_unit_lower_inverse.defvjp(_unit_lower_inverse_fwd, lambda inv, ct: (_inverse_pullback(inv, ct),))


@jax.custom_vjp
def _known_inverse(n_mat, inv):
    return inv


_known_inverse.defvjp(lambda n_mat, inv: (inv, inv), lambda inv, ct: (_inverse_pullback(inv, ct), jnp.zeros_like(inv)))


def _scan_chunk(r, lw, k, v, a, b, s0, inv=None, with_inverse=False):
    H, L, _ = r.shape
    row = lax.broadcasted_iota(jnp.int32, (H, L, L), 1)
    col = lax.broadcasted_iota(jnp.int32, (H, L, L), 2)
    incl, strict = col <= row, col < row
    cs = _running_sum(lw)
    cs_end = cs[:, L - 1:L, :]
    p, p_inv = jnp.exp(cs), jnp.exp(-cs)
    at, bt, kt, rt = a * jnp.exp(cs - lw), b * p_inv, k * p_inv, r * p
    a_ab = jnp.where(strict, _dot(at, bt, _BNT), 0.0)
    a_ak = jnp.where(strict, _dot(at, kt, _BNT), 0.0)
    a_rb = jnp.where(incl, _dot(rt, bt, _BNT), 0.0)
    a_rk = jnp.where(incl, _dot(rt, kt, _BNT), 0.0)
    inv = _unit_lower_inverse(a_ab) if inv is None else _known_inverse(a_ab, inv)
    u = _dot(inv, _dot(at, s0, _BNT) + _dot(a_ak, v, _BNN), _BNN)
    y = _dot(rt, s0, _BNT) + _dot(a_rb, u, _BNN) + _dot(a_rk, v, _BNN)
    to_end = jnp.exp(cs_end - cs)
    s1 = s0 * jnp.exp(cs_end) + _dot(u, b * to_end, _BTN) + _dot(v, k * to_end, _BTN)
    return (y, s1, inv) if with_inverse else (y, s1)


def _split_heads(x):
    return jnp.stack([x[:, R_HEAD * h:R_HEAD * (h + 1)] for h in range(R_HEADS)])


def _merge_heads(x):
    return jnp.concatenate([x[h] for h in range(R_HEADS)], axis=1)


def _scan_fwd(r, lw, k, v, a, b, rider):
    T = r.shape[0]
    H, N, L = R_HEADS, R_HEAD, SCAN_CHUNK
    nc = T // L
    ride_shape, ride_sems, _ = rider[1](rider[0])

    def body(r_ref, lw_ref, k_ref, v_ref, a_ref, b_ref, ride_in, y_ref, st_ref, inv_ref, ride_out, s_ref, *sem_refs):
        _ride(rider, pl.program_id(0), nc, ride_in, ride_out, sem_refs)

        @pl.when(pl.program_id(0) == 0)
        def _():
            s_ref[...] = jnp.zeros_like(s_ref)

        s0 = s_ref[...]
        st_ref[0] = s0
        y, s1, inv = _scan_chunk(*[_split_heads(t[...]) for t in (r_ref, lw_ref, k_ref, v_ref, a_ref, b_ref)], s0, with_inverse=True)
        y_ref[...] = _merge_heads(y)
        inv_ref[0] = inv
        s_ref[...] = s1

    blk = pl.BlockSpec((L, R_WIDTH), lambda c: (c, 0))
    per_chunk = pl.BlockSpec((1, H, N, N), lambda c: (c, 0, 0, 0))
    return pl.pallas_call(
        body, name="scan_fwd", grid=(nc,), in_specs=[blk] * 6 + [_HBM], out_specs=[blk, per_chunk, per_chunk, _HBM],
        out_shape=[jax.ShapeDtypeStruct((T, R_WIDTH), F32)] + [jax.ShapeDtypeStruct((nc, H, N, N), F32)] * 2 + [ride_shape],
        scratch_shapes=[pltpu.VMEM((H, N, N), F32)] + list(ride_sems),
        compiler_params=_cparams(("arbitrary",)),
    )(r, lw, k, v, a, b, rider[0])


def _scan_bwd(r, lw, k, v, a, b, states, inverses, dy, rider):
    T = r.shape[0]
    H, N, L = R_HEADS, R_HEAD, SCAN_CHUNK
    nc = T // L
    ride_shape, ride_sems, _ = rider[1](rider[0])

    def body(r_ref, lw_ref, k_ref, v_ref, a_ref, b_ref, st_ref, inv_ref, dy_ref, ride_in,
             dr_ref, dlw_ref, dk_ref, dv_ref, da_ref, db_ref, ride_out, ds_ref, *sem_refs):
        _ride(rider, pl.program_id(0), nc, ride_in, ride_out, sem_refs)

        @pl.when(pl.program_id(0) == 0)
        def _():
            ds_ref[...] = jnp.zeros_like(ds_ref)

        args = [_split_heads(t[...]) for t in (r_ref, lw_ref, k_ref, v_ref, a_ref, b_ref)] + [st_ref[0]]
        inv = inv_ref[0]
        _, pull = jax.vjp(lambda *xs: _scan_chunk(*xs, inv=inv), *args)
        grads = pull((_split_heads(dy_ref[...]), ds_ref[...]))
        for o_ref, g_ in zip((dr_ref, dlw_ref, dk_ref, dv_ref, da_ref, db_ref), grads[:6], strict=True):
            o_ref[...] = _merge_heads(g_)
        ds_ref[...] = grads[6]

    blk = pl.BlockSpec((L, R_WIDTH), lambda c: (nc - 1 - c, 0))
    per_chunk = pl.BlockSpec((1, H, N, N), lambda c: (nc - 1 - c, 0, 0, 0))
    return pl.pallas_call(
        body, name="scan_bwd", grid=(nc,), in_specs=[blk] * 6 + [per_chunk, per_chunk, blk, _HBM], out_specs=[blk] * 6 + [_HBM],
        out_shape=[jax.ShapeDtypeStruct((T, R_WIDTH), F32)] * 6 + [ride_shape],
        scratch_shapes=[pltpu.VMEM((H, N, N), F32)] + list(ride_sems),
        compiler_params=_cparams(("arbitrary",)),
    )(r, lw, k, v, a, b, states, inverses, dy, rider[0])


def _place():
    x, y, c = lax.axis_index("x"), lax.axis_index("y"), lax.axis_index("c")
    return x, y, c


def _gather_def(block):
    R, C = block.shape

    def phases(x_ref, out_ref, send_sems, recv_sems, local_sem):
        x, y, c = _place()
        me, sibling = (x, y, c), (x, y, 1 - c)
        chips = [(1 - x, y), (x, 1 - y), (1 - x, 1 - y)]

        def slot(px, py, pc):
            return out_ref.at[4 * px + 2 * py + pc]

        def copy(k, blk, to, src=None):
            return pltpu.make_async_remote_copy(
                src_ref=slot(*blk) if src is None else src, dst_ref=slot(*blk),
                send_sem=send_sems.at[k], recv_sem=recv_sems.at[k], device_id=to, device_id_type=_MESH_ID)

        mine = pltpu.make_async_copy(x_ref, slot(*me), local_sem)
        first = [copy(0, me, sibling, src=x_ref)]
        first += [copy(1 + j, me, (*chip, c), src=x_ref) for j, chip in enumerate(chips)]
        passed = [copy(4 + j, (*chip, c), sibling) for j, chip in enumerate(chips)]

        def begin():
            mine.start()
            for cp in first:
                cp.start()

        def forward():
            for j, chip in enumerate(chips):
                copy(1 + j, (*chip, c), me).wait_recv()
                passed[j].start()

        def finish():
            copy(0, sibling, me).wait_recv()
            for j, chip in enumerate(chips):
                copy(4 + j, (*chip, 1 - c), me).wait_recv()
            for cp in first + passed:
                cp.wait_send()
            mine.wait()

        return [begin, forward, finish]

    sems = [pltpu.SemaphoreType.DMA((7,)), pltpu.SemaphoreType.DMA((7,)), pltpu.SemaphoreType.DMA]
    return jax.ShapeDtypeStruct((_N_DEV, R, C), block.dtype), sems, phases


def _sibling_def(blocks):
    _, R, C = blocks.shape

    def phases(x_ref, out_ref, send_sems, recv_sems):
        x, y, c = _place()
        copies = [pltpu.make_async_remote_copy(
            src_ref=x_ref.at[2 * q + (1 - c)], dst_ref=out_ref.at[q], send_sem=send_sems.at[q], recv_sem=recv_sems.at[q],
            device_id=(x, y, 1 - c), device_id_type=_MESH_ID) for q in range(4)]

        def begin():
            for cp in copies:
                cp.start()

        def finish():
            for cp in copies:
                cp.wait()

        return [begin, finish]

    return jax.ShapeDtypeStruct((4, R, C), blocks.dtype), [pltpu.SemaphoreType.DMA((4,)), pltpu.SemaphoreType.DMA((4,))], phases


def _chips_def(partials):
    def phases(x_ref, out_ref, send_sems, recv_sems, local_sem):
        x, y, c = _place()
        my_chip = 2 * x + y
        mine = pltpu.make_async_copy(x_ref.at[my_chip], out_ref.at[my_chip], local_sem)
        copies = []
        for rel in range(1, 4):
            px, py = (1 - x if (rel >> 1) & 1 else x), (1 - y if rel & 1 else y)
            copies.append(pltpu.make_async_remote_copy(
                src_ref=x_ref.at[2 * px + py], dst_ref=out_ref.at[my_chip],
                send_sem=send_sems.at[rel - 1], recv_sem=recv_sems.at[rel - 1],
                device_id=(px, py, c), device_id_type=_MESH_ID))

        def begin():
            mine.start()
            for cp in copies:
                cp.start()

        def finish():
            for cp in copies:
                cp.wait()
            mine.wait()

        return [begin, finish]

    sems = [pltpu.SemaphoreType.DMA((3,)), pltpu.SemaphoreType.DMA((3,)), pltpu.SemaphoreType.DMA]
    return jax.ShapeDtypeStruct(partials.shape, partials.dtype), sems, phases


_HBM = pl.BlockSpec(memory_space=pltpu.HBM)


def _exchange(name, array, definition):
    out_shape, sems, phases = definition(array)

    def body(x_ref, out_ref, *sem_refs):
        for phase in phases(x_ref, out_ref, *sem_refs):
            phase()

    return pl.pallas_call(body, name=name, in_specs=[_HBM], out_specs=_HBM, out_shape=out_shape, scratch_shapes=sems)(array)


def _ride(rider, step, nsteps, x_ref, out_ref, sem_refs):
    array, definition, fractions = rider
    for phase, frac in zip(definition(array)[2](x_ref, out_ref, *sem_refs), fractions, strict=True):
        pl.when(step == min(int(frac * nsteps), nsteps - 1))(phase)


def _all_gather(name, block):
    return _exchange(name, block, _gather_def)


def _chip_partials(name, blocks, from_sibling, tile, out_dtype):
    _, R, C = blocks.shape

    def body(x_ref, s_ref, o_ref):
        c = lax.axis_index("c")
        for q in range(4):
            o_ref[q] = (x_ref[2 * q + c] + s_ref[q]).astype(o_ref.dtype)

    return pl.pallas_call(
        body, name=name, grid=(R // tile,),
        in_specs=[pl.BlockSpec((_N_DEV, tile, C), lambda i: (0, i, 0)), pl.BlockSpec((4, tile, C), lambda i: (0, i, 0))],
        out_specs=pl.BlockSpec((4, tile, C), lambda i: (0, i, 0)), out_shape=jax.ShapeDtypeStruct((4, R, C), out_dtype),
        compiler_params=_cparams(("parallel",)),
    )(blocks, from_sibling)


def _sum_leading(name, x, tile=None):
    n, R, C = x.shape
    tile = tile or _pick(R, (512, 256, 128, 64, 32, 16, 8))

    def body(x_ref, o_ref):
        acc = x_ref[0].astype(F32)
        for k in range(1, n):
            acc = acc + x_ref[k].astype(F32)
        o_ref[...] = acc

    return pl.pallas_call(
        body, name=name, grid=(R // tile,), in_specs=[pl.BlockSpec((n, tile, C), lambda i: (0, i, 0))],
        out_specs=pl.BlockSpec((tile, C), lambda i: (i, 0)), out_shape=jax.ShapeDtypeStruct((R, C), F32),
        compiler_params=_cparams(("parallel",)),
    )(x)


def _adamw_update(w_, g_, m_, v_):
    m2 = ADAM_B1 * m_ + (1.0 - ADAM_B1) * g_
    v2 = ADAM_B2 * v_ + (1.0 - ADAM_B2) * jnp.square(g_)
    m_hat = m2 / (1.0 - ADAM_B1 ** ADAM_STEP)
    v_hat = v2 / (1.0 - ADAM_B2 ** ADAM_STEP)
    delta = -ADAM_LR * (m_hat / (jnp.sqrt(v_hat) + ADAM_EPS) + ADAM_WD * w_)
    return delta, m2, v2


def _adamw(name, w, g, m, v):
    R, C = w.shape
    tile = _pick(R, (256, 128, 64, 32, 16, 8))
    outs, _ = _rowwise(name, lambda tv, bv: (list(_adamw_update(*tv)), []), [w, g, m, v], [], [(C, F32)] * 3, [], tile)
    return outs


def _adamw_many(name, ws, gs, ms, vs):
    n = len(ws)

    def body(*refs):
        ins, outs = refs[:4 * n], refs[4 * n:]
        for i in range(n):
            res = _adamw_update(ins[i][...], ins[n + i][...], ins[2 * n + i][...], ins[3 * n + i][...])
            for j in range(3):
                outs[j * n + i][...] = res[j]

    out_shape = [jax.ShapeDtypeStruct(w.shape, F32) for w in ws] * 3
    res = pl.pallas_call(body, name=name, out_shape=out_shape, compiler_params=_cparams())(*ws, *gs, *ms, *vs)
    return res[:n], res[n:2 * n], res[2 * n:]


def _pack_rows(arrs, lanes=128, row_mult=8):
    flat, places, off = [], [], 0
    for a_ in arrs:
        n = a_.size
        flat.append(a_.reshape(-1).astype(F32))
        places.append((off, n, a_.shape))
        off += n
    total = -(-off // (lanes * row_mult)) * (lanes * row_mult)
    if total > off:
        flat.append(jnp.zeros((total - off,), F32))
    return jnp.concatenate(flat).reshape(total // lanes, lanes), places


def _unpack_rows(packed, places):
    flat = packed.reshape(-1)
    return [flat[o:o + n].reshape(s) for (o, n, s) in places]


_WEIGHTS = ['w_ada', 'b_ada', 'w_in', 'b_in', 'g_ln_v', 'b_ln_v', 'w_spatial', 'b_spatial', 'mu_shift', 'w0', 'w_decay_up', 'a0',
            'w_aaa_up', 'w_gate_up', 'k_k', 'k_a', 'r_k', 'gn_gain', 'gn_bias', 'w_branch_a', 'w_branch_b', 'w_out', 'b_out',
            'ln1_g', 'ln1_b', 'w_ff1', 'b_ff1', 'w_ff2', 'b_ff2', 'ln2_g', 'ln2_b']
_BIG = {'w_ff1': (0, 512), 'w_ff2': (512, 512), 'w_out': (1024, 128), 'w_branch_a': (1152, 64), 'w_branch_b': (1216, 64), 'w_in': (1280, 640)}
_LATER_ROWS = 1280
_CUT_BY_COLS = ('w_ff1', 'w_in', 'w_branch_a', 'w_branch_b')
IN_SHARD = IN_COLS // _N_DEV
_LORA = {'w_decay_up': (0, LORA_W), 'w_aaa_up': (LORA_W, LORA_A), 'w_gate_up': (LORA_W + LORA_A, LORA_G)}
_COMM_DTYPE = jnp.bfloat16


def _pad_rows(a, rows):
    return jnp.pad(a, ((0, rows - a.shape[0]),) + ((0, 0),) * (a.ndim - 1))


def _pack_big(shards):
    blocks = []
    for n, (_, rows) in _BIG.items():
        a = shards[n].T if n in _CUT_BY_COLS else shards[n]
        blocks.append(_pad_rows(a.reshape(-1, D_MODEL), rows))
    return jnp.concatenate(blocks, axis=0)


def _unpack_big(block, like):
    out = {}
    for n, (r0, _) in _BIG.items():
        rr, cc = like[n].shape
        if n in _CUT_BY_COLS:
            out[n] = block[r0:r0 + rr * cc // D_MODEL].reshape(cc, rr).T
        else:
            out[n] = block[r0:r0 + rr]
    return out


def _to_padded(a, axis):
    g_end = 2 * G_WIDTH
    r_end = g_end + RW_USED
    take = lambda lo, hi: lax.slice_in_dim(a, lo, hi, axis=axis)
    zshape = list(a.shape)
    zshape[axis] = RW_COLS - RW_USED
    return jnp.concatenate([take(r_end, IN_COLS), take(g_end, r_end), jnp.zeros(zshape, a.dtype), take(0, g_end)], axis=axis)


def _from_padded(a, axis):
    take = lambda lo, hi: lax.slice_in_dim(a, lo, hi, axis=axis)
    return jnp.concatenate([take(2 * D_MODEL + RW_COLS, P_COLS), take(2 * D_MODEL, 2 * D_MODEL + RW_USED), take(0, 2 * D_MODEL)], axis=axis)


def _step(p, m, v, x, c, target):
    T = x.shape[0]
    xi, yi, ci = _place()
    me = 4 * xi + 2 * yi + ci
    tile = 256

    lane = jnp.arange(R_WIDTH)
    gsum = (lane[:, None] // R_HEAD == lane[None, :] // R_HEAD).astype(F32)
    expand = (jnp.arange(128)[:, None] == (lane[None, :] // (G_WIDTH // 8))).astype(F32)

    (c_act,), _ = _rowwise("silu_c", lambda tv, bv: ([tv[0] * _sigmoid(tv[0])], []), [c], [], [(D_MODEL, F32)], [], 1)
    small, places = _pack_rows([c_act, p['w_decay_up'], p['w_aaa_up'], p['w_gate_up']])
    small_all = _all_gather("gather_small", small)
    per_dev = [_unpack_rows(small_all[d], places) for d in range(_N_DEV)]
    c_act_all = _pad_rows(jnp.concatenate([pd[0] for pd in per_dev], axis=0), 16)
    lora_full = {n: jnp.concatenate([pd[i + 1] for pd in per_dev], axis=1) for i, n in enumerate(_LORA)}
    lora_pad = {n: jnp.zeros((LORA_PAD, R_WIDTH), F32).at[r0:r0 + nr].set(lora_full[n]) for n, (r0, nr) in _LORA.items()}

    big_names = list(_BIG)
    my_rows = _pack_big(p).astype(_MXU_DTYPE)
    w_in_all = _all_gather("gather_w_in", my_rows[_LATER_ROWS:])
    w_in_t = _to_padded(w_in_all[:, :IN_SHARD].reshape(IN_COLS, D_MODEL), 0)
    b_in_p = _to_padded(p['b_in'], 1)
    mu_p = jnp.concatenate([p['mu_shift'], jnp.zeros((1, RW_COLS - RW_USED), F32)], axis=1)

    b_ada_mine = lax.dynamic_slice(p['b_ada'], (0, me * 768), (1, 768))
    mod_cols = _mm("ada_mod", c_act_all, p['w_ada'], "nn", bias=b_ada_mine)
    mod_all = _all_gather("gather_mod", mod_cols)
    mod = lax.dynamic_index_in_dim(mod_all, me, axis=1, keepdims=False).reshape(1, 6 * D_MODEL)
    sh1, sc1, gt1, sh2, sc2, gt2 = [mod[:, i * D_MODEL:(i + 1) * D_MODEL] for i in range(6)]

    (h,), _ = _rowwise("modulate1", lambda tv, bv: (_seg_modulate(tv, bv), []), [x], [sc1, sh1], [(D_MODEL, _MXU_DTYPE)], [], tile)
    proj = _mm("in_proj", h, w_in_t, "nt", bias=b_in_p)
    ws = p['w_spatial']
    b_tg = jnp.zeros((G_CHUNK, 128), F32).at[:, :8].set(p['b_spatial'].T)
    gmlp_b = [p['g_ln_v'], p['b_ln_v'], ws, b_tg, expand]
    z_gmlp = (proj, 2 * G_WIDTH, 4)
    (ya,), _ = _rowwise("gmlp", lambda tv, bv: (_seg_gmlp(tv, bv), []), [z_gmlp], gmlp_b, [(G_WIDTH, F32)], [], G_CHUNK)
    z_rw = (proj, RW_COLS, 1)
    z_rw_halo = ("halo", proj, RW_COLS, 1)
    pre_b = [mu_p, p['w0'], lora_pad['w_decay_up'], p['a0'], lora_pad['w_aaa_up'], lora_pad['w_gate_up'], p['k_k'], p['k_a'], gsum]

    def pre_fwd(tv, bv):
        z_t, halo_t = tv
        return _seg_rwkv_pre([z_t, _shift_down(z_t, halo_t, pl.program_id(0) == 0)], bv), []

    pre_out, _ = _rowwise("rwkv_pre", pre_fwd, [z_rw, z_rw_halo], pre_b, [(R_WIDTH, F32)] * 7, [], tile)
    r_, lw_, k2_, v_, a_, b_, g_ = pre_out
    scan_in = (r_, lw_, k2_, v_, a_, b_)
    y_, states, inverses, later_all = _scan_fwd(*scan_in, rider=(my_rows[:_LATER_ROWS], _gather_def, (0.0, 0.875, 1.0)))

    def whole(n, rows):
        r0 = _BIG[n][0]
        return later_all[:, r0:r0 + rows].reshape(_N_DEV * rows, D_MODEL)

    w_ff1_t, w_ff2, w_out = whole('w_ff1', 512), whole('w_ff2', 512), whole('w_out', 128)
    w_ba_t = whole('w_branch_a', 64).reshape(D_MODEL, G_WIDTH)
    w_bb_t = whole('w_branch_b', 64).reshape(D_MODEL, R_WIDTH)
    post_b = [p['r_k'].reshape(1, R_WIDTH), p['gn_gain'], p['gn_bias'], gsum]
    (yb,), _ = _rowwise("rwkv_post", lambda tv, bv: (_seg_rwkv_post(tv, bv), []), [y_, r_, k2_, v_, g_], post_b, [(R_WIDTH, F32)], [], tile)
    pa = _mm("branch_a", ya, w_ba_t, "nt")
    pb = _mm("branch_b", yb, w_bb_t, "nt")
    gates = [(proj, D_MODEL, 0), (proj, D_MODEL, 1)]
    (merged,), _ = _rowwise("merge", lambda tv, bv: (_seg_merge(tv, bv), []), gates + [pa, pb], [], [(D_MODEL, _MXU_DTYPE)], [], tile)
    mix = _mm("out_proj", merged, w_out, "nn", bias=p['b_out'])
    mid_b = [gt1, p['ln1_g'], p['ln1_b'], sc2, sh2]
    (h1, h2in), _ = _rowwise("mid", lambda tv, bv: (_seg_mid(tv, bv), []), [x, mix], mid_b, [(D_MODEL, F32), (D_MODEL, _MXU_DTYPE)], [], tile)
    f1, act = _mm("ff1", h2in, w_ff1_t, "nt", bias=p['b_ff1'], epi=lambda t: _seg_relu2([t], [])[0], epi_dtype=_MXU_DTYPE)
    ff = _mm("ff2", act, w_ff2, "nn", bias=p['b_ff2'])

    def loss_fn(tv, bv):
        h1_t, ff_t, tgt = tv
        val, grads = jax.value_and_grad(lambda a0_, a1_, b0_, b1_, b2_: _seg_loss([a0_, a1_, tgt], [b0_, b1_, b2_]), argnums=(0, 1, 2, 3, 4))(h1_t, ff_t, *bv)
        return [grads[0], grads[1]], [grads[2], grads[3], grads[4], jnp.sum(grads[1], axis=0, keepdims=True), jnp.full((1, 128), val, F32)]

    (dh1_a, dff), (d_gt2, d_ln2_g, d_ln2_b, d_b_ff2, loss_row) = _rowwise(
        "loss", loss_fn, [h1, ff, target], [gt2, p['ln2_g'], p['ln2_b']], [(D_MODEL, F32), (D_MODEL, _MXU_DTYPE)], [(1, D_MODEL)] * 4 + [(1, 128)], tile)
    loss = lax.psum(loss_row[0, 0], ("x", "y", "c"))

    g = {}
    g['ln2_g'], g['ln2_b'], g['b_ff2'] = d_ln2_g, d_ln2_b, d_b_ff2
    gw = {}
    gw['w_ff2'] = _mm("g_w_ff2", act, dff, "tn")
    df1 = _mm("d_act", dff, w_ff2, "nt", beside=f1, epi=lambda d_, f_: d_ * (2.0 * jnp.maximum(f_, 0.0)), epi_dtype=_MXU_DTYPE)
    g['b_ff1'] = _mm("g_b_ff1", jnp.ones((16, T), _MXU_DTYPE), df1, "nn")[0:1]
    gw['w_ff1'] = _mm("g_w_ff1", df1, h2in, "tn")
    dh2in = _mm("d_h2in", df1, w_ff1_t, "nn")
    (dx_a, dmix), (d_gt1, g['ln1_g'], g['ln1_b'], d_sc2, d_sh2), (g['b_out'],) = _rowwise_vjp(
        "mid_bwd", _seg_mid, [x, mix], mid_b, [dh1_a, dh2in], tile, [0, 1], [0, 1, 2, 3, 4], t_dtypes=[F32, _MXU_DTYPE], colsum=[1])
    gw['w_out'] = _mm("g_w_out", merged, dmix, "tn")
    dmerged = _mm("d_merged", dmix, w_out, "nt")
    (dga, dgb, dpa, dpb), _, (cs_ga, cs_gb) = _rowwise_vjp(
        "merge_bwd", _seg_merge, gates + [pa, pb], [], [dmerged], tile, [0, 1, 2, 3], [], t_dtypes=[_MXU_DTYPE] * 4, colsum=[0, 1])
    gw['w_branch_a'] = _mm("g_w_branch_a", dpa, ya, "tn")
    gw['w_branch_b'] = _mm("g_w_branch_b", dpb, yb, "tn")
    dya = _mm("d_ya", dpa, w_ba_t, "nn")
    dyb = _mm("d_yb", dpb, w_bb_t, "nn")
    def send_rows(names):
        parts = []
        for n in names:
            per_dev = gw[n].reshape(_N_DEV, -1, D_MODEL)
            parts.append(jnp.pad(per_dev, ((0, 0), (0, _BIG[n][1] - per_dev.shape[1]), (0, 0))))
        return jnp.concatenate(parts, axis=1) if len(parts) > 1 else parts[0]

    send_early = send_rows(big_names[:-1])
    (dy, dr1, dk1, dv1, dg_), (d_r_k, g['gn_gain'], g['gn_bias']), _, sibling_early = _rowwise_vjp(
        "rwkv_post_bwd", _seg_rwkv_post, [y_, r_, k2_, v_, g_], post_b, [dyb], tile, [0, 1, 2, 3, 4], [0, 1, 2],
        rider=(send_early, _sibling_def, (0.0, 1.0)))
    g['r_k'] = d_r_k
    partials_early = _chip_partials("chip_partials_early", send_early, sibling_early, 128, _COMM_DTYPE)
    dr2, dlw, dk2, dv2, da, db, landed_early = _scan_bwd(*scan_in, states, inverses, dy, rider=(partials_early, _chips_def, (0.0, 1.0)))
    pre_tile = 128
    last_step = T // pre_tile - 1

    def pre_prep(prim):
        z_t, halo_t = prim
        return [z_t, _shift_down(z_t, halo_t, pl.program_id(0) == last_step)]

    def pre_finish(dts, sc):
        dz_direct, dprev = dts
        (row_after,) = sc

        @pl.when(pl.program_id(0) == 0)
        def _():
            row_after[...] = jnp.zeros_like(row_after)

        dz = dz_direct + _shift_up(dprev, row_after[...])
        row_after[...] = dprev[:_HALO]
        return [dz]

    (dz_rw_all,), (d_mu, g['w0'], d_wd, g['a0'], d_wa, d_wg, g['k_k'], g['k_a']), (cs_rw,) = _rowwise_vjp(
        "rwkv_pre_bwd", _seg_rwkv_pre, [z_rw, z_rw_halo], pre_b, [[dr1, dr2], dlw, [dk1, dk2], [dv1, dv2], da, db, dg_], pre_tile,
        [0, 1], [0, 1, 2, 3, 4, 5, 6, 7], t_dtypes=[_MXU_DTYPE], colsum=[0], prep=pre_prep, finish=pre_finish,
        out_widths=[RW_COLS], reverse=True, scratch=[(_HALO, RW_COLS)])
    g['mu_shift'] = d_mu[:, :RW_USED]
    for n, d_ in (('w_decay_up', d_wd), ('w_aaa_up', d_wa), ('w_gate_up', d_wg)):
        r0, nr = _LORA[n]
        g[n] = d_[r0:r0 + nr]
    (dz_g,), (g['g_ln_v'], g['b_ln_v'], g['w_spatial'], d_b_tg), (cs_g,) = _rowwise_vjp(
        "gmlp_bwd", _seg_gmlp, [z_gmlp], gmlp_b, [dya], G_CHUNK, [0], [0, 1, 2, 3], t_dtypes=[_MXU_DTYPE], colsum=[0])
    g['b_spatial'] = d_b_tg[:, :8].T
    dproj = jnp.concatenate([dga, dgb, dz_rw_all, dz_g], axis=1)
    g['b_in'] = _from_padded(jnp.concatenate([cs_ga, cs_gb, cs_rw, cs_g], axis=1), 1)
    small_names = [n for n in _WEIGHTS if n not in _BIG and n not in ('w_ada', 'b_ada')]
    packed_g, g_places = _pack_rows([g[n] for n in small_names], row_mult=256)
    gw_in_t, small_all = _mm("g_w_in", dproj, h, "tn", rider=(packed_g, _gather_def, (0.0, 0.6, 1.0)))
    gw['w_in'] = _from_padded(gw_in_t, 0)
    send_late = send_rows(big_names[-1:])
    sibling_late = _exchange("pair_exchange_late", send_late, _sibling_def)
    partials_late = _chip_partials("chip_partials_late", send_late, sibling_late, 128, _COMM_DTYPE)
    dh, landed_late = _mm("d_h", dproj, w_in_t, "nn", rider=(partials_late, _chips_def, (0.0, 1.0)))

    def mod1_bwd(tv, bv):
        x_t, dh_t, dxa_t = tv
        (sc,) = bv
        return [dxa_t + dh_t * (1.0 + sc)], [jnp.sum(dh_t * x_t, axis=0, keepdims=True), jnp.sum(dh_t, axis=0, keepdims=True)]

    (grad_x,), (d_sc1, d_sh1) = _rowwise("modulate1_bwd", mod1_bwd, [x, dh, dx_a], [sc1], [(D_MODEL, F32)], [(1, D_MODEL)] * 2, tile)

    dmod = jnp.concatenate([d_sh1, d_sc1, d_gt1, d_sh2, d_sc2, d_gt2], axis=1).reshape(6 * D_MODEL // 128, 128)
    dmod_all = _all_gather("gather_dmod", dmod)
    g['b_ada'] = _sum_leading("sum_dmod", dmod_all).reshape(1, 6 * D_MODEL)
    dmod_mine = lax.dynamic_slice(dmod_all.reshape(_N_DEV, 6 * D_MODEL), (0, me * 768), (_N_DEV, 768))
    g_w_ada = _mm("g_w_ada", c_act_all, _pad_rows(dmod_mine, 16), "tn")

    small_sum = _unpack_rows(_sum_leading("sum_small", small_all), g_places)
    for n, t in zip(small_names, small_sum, strict=True):
        g[n] = t
    for n in _LORA:
        g[n] = lax.dynamic_slice(g[n], (0, me * R_HEAD), (g[n].shape[0], R_HEAD))
    g['w_ada'] = g_w_ada

    summed = jnp.concatenate([_sum_leading("sum_early", landed_early), _sum_leading("sum_late", landed_late)], axis=0)
    g.update(_unpack_big(summed, p))

    delta, new_m, new_v = {}, {}, {}
    own_call = ['w_ada'] + big_names
    for n in own_call:
        delta[n], new_m[n], new_v[n] = _adamw("adamw_" + n, p[n], g[n], m[n], v[n])
    rest = [n for n in _WEIGHTS if n not in own_call]
    outs = _adamw_many("adamw_rest", *[[d[n].reshape(p[n].shape) for n in rest] for d in (p, g, m, v)])
    for d, o in zip((delta, new_m, new_v), outs, strict=True):
        d.update(zip(rest, o, strict=True))
    return loss, grad_x, g, delta, new_m, new_v


def kernel(x, c, w_ada, b_ada, w_in, b_in, g_ln_v, b_ln_v, w_spatial, b_spatial, mu_shift, w0, w_decay_up, a0, w_aaa_up, w_gate_up, k_k, k_a, r_k, gn_gain, gn_bias, w_branch_a, w_branch_b, w_out, b_out, ln1_g, ln1_b, w_ff1, b_ff1, w_ff2, b_ff2, ln2_g, ln2_b, loss_target, m_w_ada, m_b_ada, m_w_in, m_b_in, m_g_ln_v, m_b_ln_v, m_w_spatial, m_b_spatial, m_mu_shift, m_w0, m_w_decay_up, m_a0, m_w_aaa_up, m_w_gate_up, m_k_k, m_k_a, m_r_k, m_gn_gain, m_gn_bias, m_w_branch_a, m_w_branch_b, m_w_out, m_b_out, m_ln1_g, m_ln1_b, m_w_ff1, m_b_ff1, m_w_ff2, m_b_ff2, m_ln2_g, m_ln2_b, v_w_ada, v_b_ada, v_w_in, v_b_in, v_g_ln_v, v_b_ln_v, v_w_spatial, v_b_spatial, v_mu_shift, v_w0, v_w_decay_up, v_a0, v_w_aaa_up, v_w_gate_up, v_k_k, v_k_a, v_r_k, v_gn_gain, v_gn_bias, v_w_branch_a, v_w_branch_b, v_w_out, v_b_out, v_ln1_g, v_ln1_b, v_w_ff1, v_b_ff1, v_w_ff2, v_b_ff2, v_ln2_g, v_ln2_b):
    given = dict(locals())
    shapes = {n: given[n].shape for n in _WEIGHTS}
    def two_d(a_):
        a_ = a_[0]
        return a_.reshape(1, -1) if a_.ndim == 1 else a_
    p = {n: two_d(given[n]) for n in _WEIGHTS}
    m = {n: two_d(given["m_" + n]) for n in _WEIGHTS}
    v = {n: two_d(given["v_" + n]) for n in _WEIGHTS}
    loss, grad_x, g, delta, new_m, new_v = _step(p, m, v, x[0], c, loss_target[0])
    outs = [loss, grad_x[None]]
    for d in (g, delta, new_m, new_v):
        outs += [d[n].reshape(shapes[n]) for n in _WEIGHTS]
    return tuple(outs)
```

```python
import functools

import jax
import jax.numpy as jnp
from jax import lax
from jax.experimental import pallas as pl
from jax.experimental.pallas import tpu as pltpu

F32 = jnp.float32
_MXU_DTYPE = jnp.bfloat16
_HI = lax.Precision.HIGHEST
_VMEM_LIMIT = 48 * 1024 * 1024
_MESH_ID = pl.DeviceIdType.MESH
_N_DEV = 8

D_MODEL = 1024
G_WIDTH = 512
G_CHUNK = 128
R_WIDTH = 512
R_HEADS = 8
R_HEAD = 64
LORA_W, LORA_A, LORA_G = 32, 32, 96
D_FF = 4096
ALPHA = 2.0 ** 0.25
LN_EPS = 1e-5
GN_EPS = 64e-5
SCAN_CHUNK = 64
ADAM_LR, ADAM_B1, ADAM_B2, ADAM_EPS, ADAM_WD, ADAM_STEP = 0.001, 0.9, 0.999, 1e-08, 0.01, 10

P_COLS = 5120
RW_COLS = 2048
RW_USED = 3 * R_WIDTH + LORA_W + LORA_A + LORA_G
LORA_PAD = 256
IN_COLS = 2 * G_WIDTH + RW_USED + 2 * D_MODEL


def _cparams(sem=None, **kw):
    if sem is not None:
        kw["dimension_semantics"] = sem
    return pltpu.CompilerParams(vmem_limit_bytes=_VMEM_LIMIT, **kw)


def _dot(a, b, dims=(((1,), (0,)), ((), ())), hi=False):
    if hi:
        return lax.dot_general(a.astype(F32), b.astype(F32), dims, precision=_HI, preferred_element_type=F32)
    return lax.dot_general(a.astype(_MXU_DTYPE), b.astype(_MXU_DTYPE), dims, preferred_element_type=F32)


_NN = (((1,), (0,)), ((), ()))
_NT = (((1,), (1,)), ((), ()))
_TN = (((0,), (0,)), ((), ()))


def _pick(n, pref):
    for t in pref:
        if n % t == 0:
            return t
    return n


def _mm(name, a, b, mode, bias=None, out_dtype=F32, epi=None, epi_dtype=None, raw=True, beside=None, rider=None, tm=None, tn=None, tk=None):
    if mode == "nn":
        (M, K), (_, N) = a.shape, b.shape
    elif mode == "nt":
        (M, K), (N, _) = a.shape, b.shape
    else:
        (K, M), (_, N) = a.shape, b.shape
    tm = tm or _pick(M, (2048, 1280, 1024, 512, 256, 128, 64, 32, 16, 8))
    tn = tn or _pick(N, (1024, 512, 640, 384, 256, 128))
    tk = tk or _pick(K, (1024, 512, 256, 128))
    nk = K // tk
    dims = {"nn": _NN, "nt": _NT, "tn": _TN}[mode]
    a_spec = pl.BlockSpec((tk, tm), lambda i, j, k: (k, i)) if mode == "tn" else pl.BlockSpec((tm, tk), lambda i, j, k: (i, k))
    b_spec = pl.BlockSpec((tn, tk), lambda i, j, k: (j, k)) if mode == "nt" else pl.BlockSpec((tk, tn), lambda i, j, k: (k, j))
    o_spec = pl.BlockSpec((tm, tn), lambda i, j, k: (i, j))
    has_bias, has_beside = bias is not None, beside is not None
    two = epi is not None and not has_beside and raw
    only_epi = epi is not None and not has_beside and not raw
    grid = (M // tm, N // tn, nk)
    ride_shape, ride_sems, _ = rider[1](rider[0]) if rider else (None, [], None)

    def body(*refs):
        refs = list(refs)
        n_in = 2 + has_bias + has_beside
        if rider:
            sem_refs = [refs.pop() for _ in ride_sems][::-1]
            ride_out = refs.pop(n_in + 1 + 1 + two)
            ride_in = refs.pop(n_in)
            step = (pl.program_id(0) * grid[1] + pl.program_id(1)) * grid[2] + pl.program_id(2)
            _ride(rider, step, grid[0] * grid[1] * grid[2], ride_in, ride_out, sem_refs)
        a_ref, b_ref = refs[0], refs[1]
        bias_ref = refs[2] if has_bias else None
        beside_ref = refs[n_in - 1] if has_beside else None
        outs = refs[n_in:]
        o_ref, acc_ref = outs[0], outs[-1]
        k = pl.program_id(2)

        @pl.when(k == 0)
        def _():
            acc_ref[...] = jnp.zeros_like(acc_ref)

        acc_ref[...] += _dot(a_ref[...], b_ref[...], dims)

        @pl.when(k == nk - 1)
        def _():
            res = acc_ref[...]
            if has_bias:
                res = res + bias_ref[...]
            if has_beside:
                o_ref[...] = epi(res, beside_ref[...]).astype(o_ref.dtype)
            elif only_epi:
                o_ref[...] = epi(res).astype(o_ref.dtype)
            else:
                o_ref[...] = res.astype(o_ref.dtype)
                if two:
                    outs[1][...] = epi(res).astype(outs[1].dtype)

    in_specs = [a_spec, b_spec]
    args = [a, b]
    if has_bias:
        in_specs.append(pl.BlockSpec((1, tn), lambda i, j, k: (0, j)))
        args.append(bias)
    if has_beside:
        in_specs.append(o_spec)
        args.append(beside)
    out_shape = [jax.ShapeDtypeStruct((M, N), epi_dtype if (has_beside or only_epi) else out_dtype)]
    out_specs = [o_spec]
    if two:
        out_shape.append(jax.ShapeDtypeStruct((M, N), epi_dtype))
        out_specs.append(o_spec)
    if rider:
        in_specs.append(_HBM)
        args.append(rider[0])
        out_shape.append(ride_shape)
        out_specs.append(_HBM)
    res = pl.pallas_call(
        body, name=name, grid=grid, in_specs=in_specs, out_specs=out_specs, out_shape=out_shape,
        scratch_shapes=[pltpu.VMEM((tm, tn), F32)] + list(ride_sems),
        compiler_params=_cparams(("arbitrary",) * 3 if rider else ("parallel", "parallel", "arbitrary")),
    )(*args)
    return res if (two or rider) else res[0]


_HALO = 8


def _rowwise(name, fn, tiled, bcast, tiled_out, red_out, tile, reverse=False, scratch=(), rider=None):
    tiled = [t if isinstance(t, tuple) else (t, t.shape[1], 0) for t in tiled]
    T = next(t[0] for t in tiled if not isinstance(t[0], str)).shape[0]
    n = T // tile
    nt, nb, nto, nsc, nro = len(tiled), len(bcast), len(tiled_out), len(scratch), len(red_out)
    ride_shape, ride_sems, _ = rider[1](rider[0]) if rider else (None, [], None)

    def row_block(i):
        return n - 1 - i if reverse else i

    def body(*refs):
        refs = list(refs)
        if rider:
            sem_refs = [refs.pop() for _ in ride_sems][::-1]
            ride_out = refs.pop(nt + nb + 1 + nto + nro)
            ride_in = refs.pop(nt + nb)
            _ride(rider, pl.program_id(0), n, ride_in, ride_out, sem_refs)
        t_refs, b_refs = refs[:nt], refs[nt:nt + nb]
        to_refs, ro_refs = refs[nt + nb:nt + nb + nto], refs[nt + nb + nto:len(refs) - nsc]
        extra = (list(refs[len(refs) - nsc:]),) if nsc else ()
        touts, routs = fn([r[...] for r in t_refs], [r[...] for r in b_refs], *extra)
        for r, v in zip(to_refs, touts, strict=True):
            r[...] = v.astype(r.dtype)
        if ro_refs:
            i = pl.program_id(0)

            @pl.when(i == 0)
            def _():
                for r, v in zip(ro_refs, routs, strict=True):
                    r[...] = v.astype(F32)

            @pl.when(i > 0)
            def _():
                for r, v in zip(ro_refs, routs, strict=True):
                    r[...] += v.astype(F32)

    def whole(shape):
        nd = len(shape)
        return pl.BlockSpec(tuple(shape), lambda i: (0,) * nd)

    per_tile = tile // _HALO
    in_specs, arrays = [], []
    for t in tiled:
        if isinstance(t[0], str):
            _, arr, w, cb = t
            in_specs.append(pl.BlockSpec((_HALO, w), functools.partial(lambda i, cb: (jnp.maximum(row_block(i) * per_tile - 1, 0), cb), cb=cb)))
        else:
            arr, w, cb = t
            in_specs.append(pl.BlockSpec((tile, w), functools.partial(lambda i, cb: (row_block(i), cb), cb=cb)))
        arrays.append(arr)
    in_specs += [whole(b.shape) for b in bcast]
    out_specs = [pl.BlockSpec((tile, c), lambda i: (row_block(i), 0)) for (c, _) in tiled_out] + [whole(s) for s in red_out]
    out_shape = [jax.ShapeDtypeStruct((T, c), dt) for (c, dt) in tiled_out] + [jax.ShapeDtypeStruct(tuple(s), F32) for s in red_out]
    ride_args = []
    if rider:
        in_specs.append(_HBM)
        out_specs.append(_HBM)
        out_shape.append(ride_shape)
        ride_args = [rider[0]]
    res = pl.pallas_call(
        body, name=name, grid=(n,), in_specs=in_specs, out_specs=out_specs, out_shape=out_shape,
        scratch_shapes=[pltpu.VMEM(tuple(s), F32) for s in scratch] + list(ride_sems),
        compiler_params=_cparams(("arbitrary",)),
    )(*arrays, *bcast, *ride_args)
    if rider:
        return list(res[:nto]), list(res[nto:nto + nro]), res[-1]
    return list(res[:nto]), list(res[nto:])


def _rowwise_vjp(name, f, tiled, bcast, cts, tile, wrt_t, wrt_b, t_dtypes=None, colsum=(), prep=None, finish=None,
                 out_widths=None, reverse=False, scratch=(), rider=None):
    tiled = [t if isinstance(t, tuple) else (t, t.shape[1], 0) for t in tiled]
    npr = len(tiled)
    t_dtypes = t_dtypes or [F32] * len(wrt_t)
    groups = [c if isinstance(c, list) else [c] for c in cts]
    cts = [a_ for grp in groups for a_ in grp]

    def fn(tv, bv, sc=None):
        prim, flat_ct = tv[:npr], list(tv[npr:])
        if prep is not None:
            prim = prep(prim)
        ct = []
        for grp in groups:
            parts = [flat_ct.pop(0).astype(F32) for _ in grp]
            ct.append(functools.reduce(lambda p_, q_: p_ + q_, parts))

        def g(dt_vals, db_vals):
            full_t, full_b = list(prim), list(bv)
            for i, v in zip(wrt_t, dt_vals, strict=True):
                full_t[i] = v
            for j, v in zip(wrt_b, db_vals, strict=True):
                full_b[j] = v
            return f(full_t, full_b)

        outs, pull = jax.vjp(g, [prim[i].astype(F32) for i in wrt_t], [bv[j] for j in wrt_b])
        dts, dbs = pull([c.astype(o.dtype) for c, o in zip(ct, outs, strict=True)])
        if finish is not None:
            dts = finish(dts, sc)
        sums = [jnp.sum(dts[i].astype(F32), axis=0, keepdims=True) for i in colsum]
        return dts, list(dbs) + sums

    widths = out_widths or [tiled[i][1] for i in wrt_t]
    tiled_out = list(zip(widths, t_dtypes, strict=True))
    red_out = [bcast[j].shape for j in wrt_b] + [(1, widths[i]) for i in colsum]
    res = _rowwise(name, fn, tiled + list(cts), bcast, tiled_out, red_out, tile, reverse=reverse, scratch=scratch, rider=rider)
    dts, reds = res[0], res[1]
    nb = len(wrt_b)
    return (dts, reds[:nb], reds[nb:]) + tuple(res[2:])


def _layer_norm(x, g, b, eps):
    mu = jnp.mean(x, axis=-1, keepdims=True)
    xc = x - mu
    var = jnp.mean(xc * xc, axis=-1, keepdims=True)
    return xc * lax.rsqrt(var + eps) * g + b


def _gelu_tanh(x):
    return 0.5 * x * (1.0 + jnp.tanh(0.7978845608028654 * (x + 0.044715 * (x * x * x))))


def _sigmoid(x):
    return 1.0 / (1.0 + jnp.exp(-x))


def _seg_modulate(tv, bv):
    (x,), (sc, sh) = tv, bv
    return [x * (1.0 + sc) + sh]


def _seg_gmlp(tv, bv):
    (z,), (g_ln, b_ln, ws, b_tg, expand) = tv, bv
    bias_full = _dot(b_tg, expand, hi=True)
    zz = _gelu_tanh(z)
    u, v = zz[:, :G_WIDTH], zz[:, G_WIDTH:]
    v = _layer_norm(v, g_ln, b_ln, LN_EPS)
    row = lax.broadcasted_iota(jnp.int32, (G_CHUNK, G_CHUNK), 0)
    col = lax.broadcasted_iota(jnp.int32, (G_CHUNK, G_CHUNK), 1)
    causal = col <= row
    first_group = lax.broadcasted_iota(jnp.int32, (G_CHUNK, 128), 1) < 64
    parts = []
    for p in range(4):
        vp = v[:, 128 * p:128 * (p + 1)]
        s_even = _dot(jnp.where(causal, ws[2 * p], 0.0), vp)
        s_odd = _dot(jnp.where(causal, ws[2 * p + 1], 0.0), vp)
        parts.append(jnp.where(first_group, s_even, s_odd))
    s = jnp.concatenate(parts, axis=1) + bias_full
    return [u * s]


def _split2(x):
    hi = x.astype(_MXU_DTYPE)
    return hi, (x - hi.astype(F32)).astype(_MXU_DTYPE)


@jax.custom_vjp
def _group_sum(x, ones_blocks):
    hi, lo = _split2(x)
    return _dot(hi, ones_blocks) + _dot(lo, ones_blocks)


def _group_sum_fwd(x, ones_blocks):
    return _group_sum(x, ones_blocks), ones_blocks


def _group_sum_bwd(ones_blocks, ct):
    return _group_sum(ct, ones_blocks), jnp.zeros_like(ones_blocks)


_group_sum.defvjp(_group_sum_fwd, _group_sum_bwd)


def _shift_down(z, halo, is_first):
    _, W = z.shape
    rolled = pltpu.roll(z, 1, 0)
    before = jnp.where(is_first, 0.0, pltpu.roll(halo, 1, 0))
    top_row = lax.broadcasted_iota(jnp.int32, (_HALO, W), 0) == 0
    return jnp.concatenate([jnp.where(top_row, before, rolled[:_HALO]), rolled[_HALO:]], axis=0)


def _shift_up(d, after):
    tile, W = d.shape
    rolled = pltpu.roll(d, tile - 1, 0)
    last_row = lax.broadcasted_iota(jnp.int32, (_HALO, W), 0) == _HALO - 1
    bottom = jnp.where(last_row, pltpu.roll(after, _HALO - 1, 0), rolled[tile - _HALO:])
    return jnp.concatenate([rolled[:tile - _HALO], bottom], axis=0)


def _seg_rwkv_pre(tv, bv):
    (z, prev), (mu, w0, wd, a0, wa, wg, k_k, k_a, gsum) = tv, bv
    zs = z + (prev - z) * mu
    r, k, v = zs[:, 0:512], zs[:, 512:1024], zs[:, 1024:1536]
    zl = zs[:, 3 * R_WIDTH:3 * R_WIDTH + LORA_PAD]
    x = w0 + _dot(jnp.tanh(zl), wd)
    softplus = jnp.maximum(-x, 0.0) + jnp.log(1.0 + jnp.exp(-jnp.abs(x)))
    lw = -jnp.exp(-softplus - 0.5)
    a = _sigmoid(a0 + _dot(zl, wa))
    g = _dot(_sigmoid(zl), wg)
    kk = k * k_k
    nrm = jnp.sqrt(_group_sum(kk * kk, gsum))
    kk = kk / jnp.maximum(nrm, 1e-12)
    k2 = k * (1.0 + (a - 1.0) * k_a)
    return [r, lw, k2, v, -kk, kk * a, g]


def _seg_rwkv_post(tv, bv):
    (y, r, k2, v, g), (r_k, gain, bias, gsum) = tv, bv
    mu = _group_sum(y, gsum) * (1.0 / R_HEAD)
    yc = y - mu
    var = _group_sum(yc * yc, gsum) * (1.0 / R_HEAD)
    yn = yc * lax.rsqrt(var + GN_EPS) * gain + bias
    bonus = _group_sum(r * k2 * r_k, gsum) * v
    return [(yn + bonus) * g]


def _seg_merge(tv, bv):
    (ga, gb, pa, pb), () = tv, bv
    return [_sigmoid(ga) * pa + _sigmoid(gb) * pb]


def _seg_mid(tv, bv):
    (x, mix), (gt1, g1, b1, sc2, sh2) = tv, bv
    h1 = _layer_norm(ALPHA * x + gt1 * mix, g1, b1, LN_EPS)
    return [h1, h1 * (1.0 + sc2) + sh2]


def _seg_relu2(tv, bv):
    (f1,), () = tv, bv
    return [jnp.square(jnp.maximum(f1, 0.0))]


def _seg_loss(tv, bv):
    (h1, ff, target), (gt2, g2, b2) = tv, bv
    out = _layer_norm(ALPHA * h1 + gt2 * ff, g2, b2, LN_EPS)
    err = jnp.square(out - target)
    return 0.5 * jnp.sum(jnp.mean(err, axis=-1))


_BNN = (((2,), (1,)), ((0,), (0,)))
_BNT = (((2,), (2,)), ((0,), (0,)))
_BTN = (((1,), (1,)), ((0,), (0,)))


def _tri_dot(x, dims):
    H, L, _ = x.shape
    tri = (lax.broadcasted_iota(jnp.int32, (H, L, L), 2) <= lax.broadcasted_iota(jnp.int32, (H, L, L), 1)).astype(F32)
    hi, lo = _split2(x)
    return _dot(tri, hi, dims) + _dot(tri, lo, dims)


@jax.custom_vjp
def _running_sum(x):
    return _tri_dot(x, _BNN)


_running_sum.defvjp(lambda x: (_tri_dot(x, _BNN), None), lambda _, ct: (_tri_dot(ct, _BTN),))


def _inverse_pullback(inv, ct):
    return _dot(_dot(inv, ct, _BTN), inv, _BNT)


@jax.custom_vjp
def _unit_lower_inverse(n_mat):
    H, L, _ = n_mat.shape
    eye = lax.broadcasted_iota(jnp.int32, (H, L, L), 1) == lax.broadcasted_iota(jnp.int32, (H, L, L), 2)
    inv = jnp.where(eye, 1.0, 0.0) + n_mat
    pw = n_mat
    n = 2
    while n < L:
        pw = _dot(pw, pw, _BNN)
        inv = inv + _dot(inv, pw, _BNN)
        n *= 2
    return inv


def _unit_lower_inverse_fwd(n_mat):
    inv = _unit_lower_inverse(n_mat)
    return inv, inv


_unit_lower_inverse.defvjp(_unit_lower_inverse_fwd, lambda inv, ct: (_inverse_pullback(inv, ct),))


@jax.custom_vjp
def _known_inverse(n_mat, inv):
    return inv


_known_inverse.defvjp(lambda n_mat, inv: (inv, inv), lambda inv, ct: (_inverse_pullback(inv, ct), jnp.zeros_like(inv)))


def _scan_chunk(r, lw, k, v, a, b, s0, inv=None, with_inverse=False):
    H, L, _ = r.shape
    row = lax.broadcasted_iota(jnp.int32, (H, L, L), 1)
    col = lax.broadcasted_iota(jnp.int32, (H, L, L), 2)
    incl, strict = col <= row, col < row
    cs = _running_sum(lw)
    cs_end = cs[:, L - 1:L, :]
    p, p_inv = jnp.exp(cs), jnp.exp(-cs)
    at, bt, kt, rt = a * jnp.exp(cs - lw), b * p_inv, k * p_inv, r * p
    a_ab = jnp.where(strict, _dot(at, bt, _BNT), 0.0)
    a_ak = jnp.where(strict, _dot(at, kt, _BNT), 0.0)
    a_rb = jnp.where(incl, _dot(rt, bt, _BNT), 0.0)
    a_rk = jnp.where(incl, _dot(rt, kt, _BNT), 0.0)
    inv = _unit_lower_inverse(a_ab) if inv is None else _known_inverse(a_ab, inv)
    u = _dot(inv, _dot(at, s0, _BNT) + _dot(a_ak, v, _BNN), _BNN)
    y = _dot(rt, s0, _BNT) + _dot(a_rb, u, _BNN) + _dot(a_rk, v, _BNN)
    to_end = jnp.exp(cs_end - cs)
    s1 = s0 * jnp.exp(cs_end) + _dot(u, b * to_end, _BTN) + _dot(v, k * to_end, _BTN)
    return (y, s1, inv) if with_inverse else (y, s1)


def _split_heads(x):
    return jnp.stack([x[:, R_HEAD * h:R_HEAD * (h + 1)] for h in range(R_HEADS)])


def _merge_heads(x):
    return jnp.concatenate([x[h] for h in range(R_HEADS)], axis=1)


def _scan_fwd(r, lw, k, v, a, b, rider):
    T = r.shape[0]
    H, N, L = R_HEADS, R_HEAD, SCAN_CHUNK
    nc = T // L
    ride_shape, ride_sems, _ = rider[1](rider[0])

    def body(r_ref, lw_ref, k_ref, v_ref, a_ref, b_ref, ride_in, y_ref, st_ref, inv_ref, ride_out, s_ref, *sem_refs):
        _ride(rider, pl.program_id(0), nc, ride_in, ride_out, sem_refs)

        @pl.when(pl.program_id(0) == 0)
        def _():
            s_ref[...] = jnp.zeros_like(s_ref)

        s0 = s_ref[...]
        st_ref[0] = s0
        y, s1, inv = _scan_chunk(*[_split_heads(t[...]) for t in (r_ref, lw_ref, k_ref, v_ref, a_ref, b_ref)], s0, with_inverse=True)
        y_ref[...] = _merge_heads(y)
        inv_ref[0] = inv
        s_ref[...] = s1

    blk = pl.BlockSpec((L, R_WIDTH), lambda c: (c, 0))
    per_chunk = pl.BlockSpec((1, H, N, N), lambda c: (c, 0, 0, 0))
    return pl.pallas_call(
        body, name="scan_fwd", grid=(nc,), in_specs=[blk] * 6 + [_HBM], out_specs=[blk, per_chunk, per_chunk, _HBM],
        out_shape=[jax.ShapeDtypeStruct((T, R_WIDTH), F32)] + [jax.ShapeDtypeStruct((nc, H, N, N), F32)] * 2 + [ride_shape],
        scratch_shapes=[pltpu.VMEM((H, N, N), F32)] + list(ride_sems),
        compiler_params=_cparams(("arbitrary",)),
    )(r, lw, k, v, a, b, rider[0])


def _scan_bwd(r, lw, k, v, a, b, states, inverses, dy, rider):
    T = r.shape[0]
    H, N, L = R_HEADS, R_HEAD, SCAN_CHUNK
    nc = T // L
    ride_shape, ride_sems, _ = rider[1](rider[0])

    def body(r_ref, lw_ref, k_ref, v_ref, a_ref, b_ref, st_ref, inv_ref, dy_ref, ride_in,
             dr_ref, dlw_ref, dk_ref, dv_ref, da_ref, db_ref, ride_out, ds_ref, *sem_refs):
        _ride(rider, pl.program_id(0), nc, ride_in, ride_out, sem_refs)

        @pl.when(pl.program_id(0) == 0)
        def _():
            ds_ref[...] = jnp.zeros_like(ds_ref)

        args = [_split_heads(t[...]) for t in (r_ref, lw_ref, k_ref, v_ref, a_ref, b_ref)] + [st_ref[0]]
        inv = inv_ref[0]
        _, pull = jax.vjp(lambda *xs: _scan_chunk(*xs, inv=inv), *args)
        grads = pull((_split_heads(dy_ref[...]), ds_ref[...]))
        for o_ref, g_ in zip((dr_ref, dlw_ref, dk_ref, dv_ref, da_ref, db_ref), grads[:6], strict=True):
            o_ref[...] = _merge_heads(g_)
        ds_ref[...] = grads[6]

    blk = pl.BlockSpec((L, R_WIDTH), lambda c: (nc - 1 - c, 0))
    per_chunk = pl.BlockSpec((1, H, N, N), lambda c: (nc - 1 - c, 0, 0, 0))
    return pl.pallas_call(
        body, name="scan_bwd", grid=(nc,), in_specs=[blk] * 6 + [per_chunk, per_chunk, blk, _HBM], out_specs=[blk] * 6 + [_HBM],
        out_shape=[jax.ShapeDtypeStruct((T, R_WIDTH), F32)] * 6 + [ride_shape],
        scratch_shapes=[pltpu.VMEM((H, N, N), F32)] + list(ride_sems),
        compiler_params=_cparams(("arbitrary",)),
    )(r, lw, k, v, a, b, states, inverses, dy, rider[0])


def _place():
    x, y, c = lax.axis_index("x"), lax.axis_index("y"), lax.axis_index("c")
    return x, y, c


def _gather_def(block):
    R, C = block.shape

    def phases(x_ref, out_ref, send_sems, recv_sems, local_sem):
        x, y, c = _place()
        me, sibling = (x, y, c), (x, y, 1 - c)
        chips = [(1 - x, y), (x, 1 - y), (1 - x, 1 - y)]

        def slot(px, py, pc):
            return out_ref.at[4 * px + 2 * py + pc]

        def copy(k, blk, to, src=None):
            return pltpu.make_async_remote_copy(
                src_ref=slot(*blk) if src is None else src, dst_ref=slot(*blk),
                send_sem=send_sems.at[k], recv_sem=recv_sems.at[k], device_id=to, device_id_type=_MESH_ID)

        mine = pltpu.make_async_copy(x_ref, slot(*me), local_sem)
        first = [copy(0, me, sibling, src=x_ref)]
        first += [copy(1 + j, me, (*chip, c), src=x_ref) for j, chip in enumerate(chips)]
        passed = [copy(4 + j, (*chip, c), sibling) for j, chip in enumerate(chips)]

        def begin():
            mine.start()
            for cp in first:
                cp.start()

        def forward():
            for j, chip in enumerate(chips):
                copy(1 + j, (*chip, c), me).wait_recv()
                passed[j].start()

        def finish():
            copy(0, sibling, me).wait_recv()
            for j, chip in enumerate(chips):
                copy(4 + j, (*chip, 1 - c), me).wait_recv()
            for cp in first + passed:
                cp.wait_send()
            mine.wait()

        return [begin, forward, finish]

    sems = [pltpu.SemaphoreType.DMA((7,)), pltpu.SemaphoreType.DMA((7,)), pltpu.SemaphoreType.DMA]
    return jax.ShapeDtypeStruct((_N_DEV, R, C), block.dtype), sems, phases


def _sibling_def(blocks):
    _, R, C = blocks.shape

    def phases(x_ref, out_ref, send_sems, recv_sems):
        x, y, c = _place()
        copies = [pltpu.make_async_remote_copy(
            src_ref=x_ref.at[2 * q + (1 - c)], dst_ref=out_ref.at[q], send_sem=send_sems.at[q], recv_sem=recv_sems.at[q],
            device_id=(x, y, 1 - c), device_id_type=_MESH_ID) for q in range(4)]

        def begin():
            for cp in copies:
                cp.start()

        def finish():
            for cp in copies:
                cp.wait()

        return [begin, finish]

    return jax.ShapeDtypeStruct((4, R, C), blocks.dtype), [pltpu.SemaphoreType.DMA((4,)), pltpu.SemaphoreType.DMA((4,))], phases


def _chips_def(partials):
    def phases(x_ref, out_ref, send_sems, recv_sems, local_sem):
        x, y, c = _place()
        my_chip = 2 * x + y
        mine = pltpu.make_async_copy(x_ref.at[my_chip], out_ref.at[my_chip], local_sem)
        copies = []
        for rel in range(1, 4):
            px, py = (1 - x if (rel >> 1) & 1 else x), (1 - y if rel & 1 else y)
            copies.append(pltpu.make_async_remote_copy(
                src_ref=x_ref.at[2 * px + py], dst_ref=out_ref.at[my_chip],
                send_sem=send_sems.at[rel - 1], recv_sem=recv_sems.at[rel - 1],
                device_id=(px, py, c), device_id_type=_MESH_ID))

        def begin():
            mine.start()
            for cp in copies:
                cp.start()

        def finish():
            for cp in copies:
                cp.wait()
            mine.wait()

        return [begin, finish]

    sems = [pltpu.SemaphoreType.DMA((3,)), pltpu.SemaphoreType.DMA((3,)), pltpu.SemaphoreType.DMA]
    return jax.ShapeDtypeStruct(partials.shape, partials.dtype), sems, phases


_HBM = pl.BlockSpec(memory_space=pltpu.HBM)


def _exchange(name, array, definition):
    out_shape, sems, phases = definition(array)

    def body(x_ref, out_ref, *sem_refs):
        for phase in phases(x_ref, out_ref, *sem_refs):
            phase()

    return pl.pallas_call(body, name=name, in_specs=[_HBM], out_specs=_HBM, out_shape=out_shape, scratch_shapes=sems)(array)


def _ride(rider, step, nsteps, x_ref, out_ref, sem_refs):
    array, definition, fractions = rider
    for phase, frac in zip(definition(array)[2](x_ref, out_ref, *sem_refs), fractions, strict=True):
        pl.when(step == min(int(frac * nsteps), nsteps - 1))(phase)


def _all_gather(name, block):
    return _exchange(name, block, _gather_def)


def _chip_partials(name, blocks, from_sibling, tile, out_dtype):
    _, R, C = blocks.shape

    def body(x_ref, s_ref, o_ref):
        c = lax.axis_index("c")
        for q in range(4):
            o_ref[q] = (x_ref[2 * q + c] + s_ref[q]).astype(o_ref.dtype)

    return pl.pallas_call(
        body, name=name, grid=(R // tile,),
        in_specs=[pl.BlockSpec((_N_DEV, tile, C), lambda i: (0, i, 0)), pl.BlockSpec((4, tile, C), lambda i: (0, i, 0))],
        out_specs=pl.BlockSpec((4, tile, C), lambda i: (0, i, 0)), out_shape=jax.ShapeDtypeStruct((4, R, C), out_dtype),
        compiler_params=_cparams(("parallel",)),
    )(blocks, from_sibling)


def _sum_leading(name, x, tile=None):
    n, R, C = x.shape
    tile = tile or _pick(R, (512, 256, 128, 64, 32, 16, 8))

    def body(x_ref, o_ref):
        acc = x_ref[0].astype(F32)
        for k in range(1, n):
            acc = acc + x_ref[k].astype(F32)
        o_ref[...] = acc

    return pl.pallas_call(
        body, name=name, grid=(R // tile,), in_specs=[pl.BlockSpec((n, tile, C), lambda i: (0, i, 0))],
        out_specs=pl.BlockSpec((tile, C), lambda i: (i, 0)), out_shape=jax.ShapeDtypeStruct((R, C), F32),
        compiler_params=_cparams(("parallel",)),
    )(x)


def _adamw_update(w_, g_, m_, v_):
    m2 = ADAM_B1 * m_ + (1.0 - ADAM_B1) * g_
    v2 = ADAM_B2 * v_ + (1.0 - ADAM_B2) * jnp.square(g_)
    m_hat = m2 / (1.0 - ADAM_B1 ** ADAM_STEP)
    v_hat = v2 / (1.0 - ADAM_B2 ** ADAM_STEP)
    delta = -ADAM_LR * (m_hat / (jnp.sqrt(v_hat) + ADAM_EPS) + ADAM_WD * w_)
    return delta, m2, v2


def _adamw(name, w, g, m, v):
    R, C = w.shape
    tile = _pick(R, (256, 128, 64, 32, 16, 8))
    outs, _ = _rowwise(name, lambda tv, bv: (list(_adamw_update(*tv)), []), [w, g, m, v], [], [(C, F32)] * 3, [], tile)
    return outs


def _adamw_many(name, ws, gs, ms, vs):
    n = len(ws)

    def body(*refs):
        ins, outs = refs[:4 * n], refs[4 * n:]
        for i in range(n):
            res = _adamw_update(ins[i][...], ins[n + i][...], ins[2 * n + i][...], ins[3 * n + i][...])
            for j in range(3):
                outs[j * n + i][...] = res[j]

    out_shape = [jax.ShapeDtypeStruct(w.shape, F32) for w in ws] * 3
    res = pl.pallas_call(body, name=name, out_shape=out_shape, compiler_params=_cparams())(*ws, *gs, *ms, *vs)
    return res[:n], res[n:2 * n], res[2 * n:]


def _pack_rows(arrs, lanes=128, row_mult=8):
    flat, places, off = [], [], 0
    for a_ in arrs:
        n = a_.size
        flat.append(a_.reshape(-1).astype(F32))
        places.append((off, n, a_.shape))
        off += n
    total = -(-off // (lanes * row_mult)) * (lanes * row_mult)
    if total > off:
        flat.append(jnp.zeros((total - off,), F32))
    return jnp.concatenate(flat).reshape(total // lanes, lanes), places


def _unpack_rows(packed, places):
    flat = packed.reshape(-1)
    return [flat[o:o + n].reshape(s) for (o, n, s) in places]


_WEIGHTS = ['w_ada', 'b_ada', 'w_in', 'b_in', 'g_ln_v', 'b_ln_v', 'w_spatial', 'b_spatial', 'mu_shift', 'w0', 'w_decay_up', 'a0',
            'w_aaa_up', 'w_gate_up', 'k_k', 'k_a', 'r_k', 'gn_gain', 'gn_bias', 'w_branch_a', 'w_branch_b', 'w_out', 'b_out',
            'ln1_g', 'ln1_b', 'w_ff1', 'b_ff1', 'w_ff2', 'b_ff2', 'ln2_g', 'ln2_b']
_BIG = {'w_ff1': (0, 512), 'w_ff2': (512, 512), 'w_out': (1024, 128), 'w_branch_a': (1152, 64), 'w_branch_b': (1216, 64), 'w_in': (1280, 640)}
_LATER_ROWS = 1280
_CUT_BY_COLS = ('w_ff1', 'w_in', 'w_branch_a', 'w_branch_b')
IN_SHARD = IN_COLS // _N_DEV
_LORA = {'w_decay_up': (0, LORA_W), 'w_aaa_up': (LORA_W, LORA_A), 'w_gate_up': (LORA_W + LORA_A, LORA_G)}
_COMM_DTYPE = jnp.bfloat16


def _pad_rows(a, rows):
    return jnp.pad(a, ((0, rows - a.shape[0]),) + ((0, 0),) * (a.ndim - 1))


def _pack_big(shards):
    blocks = []
    for n, (_, rows) in _BIG.items():
        a = shards[n].T if n in _CUT_BY_COLS else shards[n]
        blocks.append(_pad_rows(a.reshape(-1, D_MODEL), rows))
    return jnp.concatenate(blocks, axis=0)


def _unpack_big(block, like):
    out = {}
    for n, (r0, _) in _BIG.items():
        rr, cc = like[n].shape
        if n in _CUT_BY_COLS:
            out[n] = block[r0:r0 + rr * cc // D_MODEL].reshape(cc, rr).T
        else:
            out[n] = block[r0:r0 + rr]
    return out


def _to_padded(a, axis):
    g_end = 2 * G_WIDTH
    r_end = g_end + RW_USED
    take = lambda lo, hi: lax.slice_in_dim(a, lo, hi, axis=axis)
    zshape = list(a.shape)
    zshape[axis] = RW_COLS - RW_USED
    return jnp.concatenate([take(r_end, IN_COLS), take(g_end, r_end), jnp.zeros(zshape, a.dtype), take(0, g_end)], axis=axis)


def _from_padded(a, axis):
    take = lambda lo, hi: lax.slice_in_dim(a, lo, hi, axis=axis)
    return jnp.concatenate([take(2 * D_MODEL + RW_COLS, P_COLS), take(2 * D_MODEL, 2 * D_MODEL + RW_USED), take(0, 2 * D_MODEL)], axis=axis)


def _step(p, m, v, x, c, target):
    T = x.shape[0]
    xi, yi, ci = _place()
    me = 4 * xi + 2 * yi + ci
    tile = 256

    lane = jnp.arange(R_WIDTH)
    gsum = (lane[:, None] // R_HEAD == lane[None, :] // R_HEAD).astype(F32)
    expand = (jnp.arange(128)[:, None] == (lane[None, :] // (G_WIDTH // 8))).astype(F32)

    (c_act,), _ = _rowwise("silu_c", lambda tv, bv: ([tv[0] * _sigmoid(tv[0])], []), [c], [], [(D_MODEL, F32)], [], 1)
    small, places = _pack_rows([c_act, p['w_decay_up'], p['w_aaa_up'], p['w_gate_up']])
    small_all = _all_gather("gather_small", small)
    per_dev = [_unpack_rows(small_all[d], places) for d in range(_N_DEV)]
    c_act_all = _pad_rows(jnp.concatenate([pd[0] for pd in per_dev], axis=0), 16)
    lora_full = {n: jnp.concatenate([pd[i + 1] for pd in per_dev], axis=1) for i, n in enumerate(_LORA)}
    lora_pad = {n: jnp.zeros((LORA_PAD, R_WIDTH), F32).at[r0:r0 + nr].set(lora_full[n]) for n, (r0, nr) in _LORA.items()}

    big_names = list(_BIG)
    my_rows = _pack_big(p).astype(_MXU_DTYPE)
    w_in_all = _all_gather("gather_w_in", my_rows[_LATER_ROWS:])
    w_in_t = _to_padded(w_in_all[:, :IN_SHARD].reshape(IN_COLS, D_MODEL), 0)
    b_in_p = _to_padded(p['b_in'], 1)
    mu_p = jnp.concatenate([p['mu_shift'], jnp.zeros((1, RW_COLS - RW_USED), F32)], axis=1)

    b_ada_mine = lax.dynamic_slice(p['b_ada'], (0, me * 768), (1, 768))
    mod_cols = _mm("ada_mod", c_act_all, p['w_ada'], "nn", bias=b_ada_mine)
    mod_all = _all_gather("gather_mod", mod_cols)
    mod = lax.dynamic_index_in_dim(mod_all, me, axis=1, keepdims=False).reshape(1, 6 * D_MODEL)
    sh1, sc1, gt1, sh2, sc2, gt2 = [mod[:, i * D_MODEL:(i + 1) * D_MODEL] for i in range(6)]

    (h,), _ = _rowwise("modulate1", lambda tv, bv: (_seg_modulate(tv, bv), []), [x], [sc1, sh1], [(D_MODEL, _MXU_DTYPE)], [], tile)
    proj = _mm("in_proj", h, w_in_t, "nt", bias=b_in_p)
    ws = p['w_spatial']
    b_tg = jnp.zeros((G_CHUNK, 128), F32).at[:, :8].set(p['b_spatial'].T)
    gmlp_b = [p['g_ln_v'], p['b_ln_v'], ws, b_tg, expand]
    z_gmlp = (proj, 2 * G_WIDTH, 4)
    (ya,), _ = _rowwise("gmlp", lambda tv, bv: (_seg_gmlp(tv, bv), []), [z_gmlp], gmlp_b, [(G_WIDTH, F32)], [], G_CHUNK)
    z_rw = (proj, RW_COLS, 1)
    z_rw_halo = ("halo", proj, RW_COLS, 1)
    pre_b = [mu_p, p['w0'], lora_pad['w_decay_up'], p['a0'], lora_pad['w_aaa_up'], lora_pad['w_gate_up'], p['k_k'], p['k_a'], gsum]

    def pre_fwd(tv, bv):
        z_t, halo_t = tv
        return _seg_rwkv_pre([z_t, _shift_down(z_t, halo_t, pl.program_id(0) == 0)], bv), []

    pre_out, _ = _rowwise("rwkv_pre", pre_fwd, [z_rw, z_rw_halo], pre_b, [(R_WIDTH, F32)] * 7, [], tile)
    r_, lw_, k2_, v_, a_, b_, g_ = pre_out
    scan_in = (r_, lw_, k2_, v_, a_, b_)
    y_, states, inverses, later_all = _scan_fwd(*scan_in, rider=(my_rows[:_LATER_ROWS], _gather_def, (0.0, 0.875, 1.0)))

    def whole(n, rows):
        r0 = _BIG[n][0]
        return later_all[:, r0:r0 + rows].reshape(_N_DEV * rows, D_MODEL)

    w_ff1_t, w_ff2, w_out = whole('w_ff1', 512), whole('w_ff2', 512), whole('w_out', 128)
    w_ba_t = whole('w_branch_a', 64).reshape(D_MODEL, G_WIDTH)
    w_bb_t = whole('w_branch_b', 64).reshape(D_MODEL, R_WIDTH)
    post_b = [p['r_k'].reshape(1, R_WIDTH), p['gn_gain'], p['gn_bias'], gsum]
    (yb,), _ = _rowwise("rwkv_post", lambda tv, bv: (_seg_rwkv_post(tv, bv), []), [y_, r_, k2_, v_, g_], post_b, [(R_WIDTH, F32)], [], tile)
    pa = _mm("branch_a", ya, w_ba_t, "nt")
    pb = _mm("branch_b", yb, w_bb_t, "nt")
    gates = [(proj, D_MODEL, 0), (proj, D_MODEL, 1)]
    (merged,), _ = _rowwise("merge", lambda tv, bv: (_seg_merge(tv, bv), []), gates + [pa, pb], [], [(D_MODEL, _MXU_DTYPE)], [], tile)
    mix = _mm("out_proj", merged, w_out, "nn", bias=p['b_out'])
    mid_b = [gt1, p['ln1_g'], p['ln1_b'], sc2, sh2]
    (h1, h2in), _ = _rowwise("mid", lambda tv, bv: (_seg_mid(tv, bv), []), [x, mix], mid_b, [(D_MODEL, F32), (D_MODEL, _MXU_DTYPE)], [], tile)
    act = _mm("ff1", h2in, w_ff1_t, "nt", bias=p['b_ff1'], epi=lambda t: _seg_relu2([t], [])[0], epi_dtype=_MXU_DTYPE, raw=False)
    ff = _mm("ff2", act, w_ff2, "nn", bias=p['b_ff2'])

    def loss_fn(tv, bv):
        h1_t, ff_t, tgt = tv
        val, grads = jax.value_and_grad(lambda a0_, a1_, b0_, b1_, b2_: _seg_loss([a0_, a1_, tgt], [b0_, b1_, b2_]), argnums=(0, 1, 2, 3, 4))(h1_t, ff_t, *bv)
        return [grads[0], grads[1]], [grads[2], grads[3], grads[4], jnp.sum(grads[1], axis=0, keepdims=True), jnp.full((1, 128), val, F32)]

    (dh1_a, dff), (d_gt2, d_ln2_g, d_ln2_b, d_b_ff2, loss_row) = _rowwise(
        "loss", loss_fn, [h1, ff, target], [gt2, p['ln2_g'], p['ln2_b']], [(D_MODEL, F32), (D_MODEL, _MXU_DTYPE)], [(1, D_MODEL)] * 4 + [(1, 128)], tile)
    loss = lax.psum(loss_row[0, 0], ("x", "y", "c"))

    g = {}
    g['ln2_g'], g['ln2_b'], g['b_ff2'] = d_ln2_g, d_ln2_b, d_b_ff2
    gw = {}
    gw['w_ff2'] = _mm("g_w_ff2", act, dff, "tn")
    df1 = _mm("d_act", dff, w_ff2, "nt", beside=act, epi=lambda d_, a_: d_ * (2.0 * jnp.sqrt(a_.astype(F32))), epi_dtype=_MXU_DTYPE)
    g['b_ff1'] = _mm("g_b_ff1", jnp.ones((16, T), _MXU_DTYPE), df1, "nn")[0:1]
    gw['w_ff1'] = _mm("g_w_ff1", df1, h2in, "tn")
    dh2in = _mm("d_h2in", df1, w_ff1_t, "nn")
    (dx_a, dmix), (d_gt1, g['ln1_g'], g['ln1_b'], d_sc2, d_sh2), (g['b_out'],) = _rowwise_vjp(
        "mid_bwd", _seg_mid, [x, mix], mid_b, [dh1_a, dh2in], tile, [0, 1], [0, 1, 2, 3, 4], t_dtypes=[F32, _MXU_DTYPE], colsum=[1])
    gw['w_out'] = _mm("g_w_out", merged, dmix, "tn")
    dmerged = _mm("d_merged", dmix, w_out, "nt")
    (dga, dgb, dpa, dpb), _, (cs_ga, cs_gb) = _rowwise_vjp(
        "merge_bwd", _seg_merge, gates + [pa, pb], [], [dmerged], tile, [0, 1, 2, 3], [], t_dtypes=[_MXU_DTYPE] * 4, colsum=[0, 1])
    gw['w_branch_a'] = _mm("g_w_branch_a", dpa, ya, "tn")
    gw['w_branch_b'] = _mm("g_w_branch_b", dpb, yb, "tn")
    dya = _mm("d_ya", dpa, w_ba_t, "nn")
    dyb = _mm("d_yb", dpb, w_bb_t, "nn")
    def send_rows(names):
        parts = []
        for n in names:
            per_dev = gw[n].reshape(_N_DEV, -1, D_MODEL)
            parts.append(jnp.pad(per_dev, ((0, 0), (0, _BIG[n][1] - per_dev.shape[1]), (0, 0))))
        return jnp.concatenate(parts, axis=1) if len(parts) > 1 else parts[0]

    send_early = send_rows(big_names[:-1])
    (dy, dr1, dk1, dv1, dg_), (d_r_k, g['gn_gain'], g['gn_bias']), _, sibling_early = _rowwise_vjp(
        "rwkv_post_bwd", _seg_rwkv_post, [y_, r_, k2_, v_, g_], post_b, [dyb], tile, [0, 1, 2, 3, 4], [0, 1, 2],
        rider=(send_early, _sibling_def, (0.0, 1.0)))
    g['r_k'] = d_r_k
    partials_early = _chip_partials("chip_partials_early", send_early, sibling_early, 128, _COMM_DTYPE)
    dr2, dlw, dk2, dv2, da, db, landed_early = _scan_bwd(*scan_in, states, inverses, dy, rider=(partials_early, _chips_def, (0.0, 1.0)))
    pre_tile = 128
    last_step = T // pre_tile - 1

    def pre_prep(prim):
        z_t, halo_t = prim
        return [z_t, _shift_down(z_t, halo_t, pl.program_id(0) == last_step)]

    def pre_finish(dts, sc):
        dz_direct, dprev = dts
        (row_after,) = sc

        @pl.when(pl.program_id(0) == 0)
        def _():
            row_after[...] = jnp.zeros_like(row_after)

        dz = dz_direct + _shift_up(dprev, row_after[...])
        row_after[...] = dprev[:_HALO]
        return [dz]

    (dz_rw_all,), (d_mu, g['w0'], d_wd, g['a0'], d_wa, d_wg, g['k_k'], g['k_a']), (cs_rw,) = _rowwise_vjp(
        "rwkv_pre_bwd", _seg_rwkv_pre, [z_rw, z_rw_halo], pre_b, [[dr1, dr2], dlw, [dk1, dk2], [dv1, dv2], da, db, dg_], pre_tile,
        [0, 1], [0, 1, 2, 3, 4, 5, 6, 7], t_dtypes=[_MXU_DTYPE], colsum=[0], prep=pre_prep, finish=pre_finish,
        out_widths=[RW_COLS], reverse=True, scratch=[(_HALO, RW_COLS)])
    g['mu_shift'] = d_mu[:, :RW_USED]
    for n, d_ in (('w_decay_up', d_wd), ('w_aaa_up', d_wa), ('w_gate_up', d_wg)):
        r0, nr = _LORA[n]
        g[n] = d_[r0:r0 + nr]
    (dz_g,), (g['g_ln_v'], g['b_ln_v'], g['w_spatial'], d_b_tg), (cs_g,) = _rowwise_vjp(
        "gmlp_bwd", _seg_gmlp, [z_gmlp], gmlp_b, [dya], G_CHUNK, [0], [0, 1, 2, 3], t_dtypes=[_MXU_DTYPE], colsum=[0])
    g['b_spatial'] = d_b_tg[:, :8].T
    dproj = jnp.concatenate([dga, dgb, dz_rw_all, dz_g], axis=1)
    g['b_in'] = _from_padded(jnp.concatenate([cs_ga, cs_gb, cs_rw, cs_g], axis=1), 1)
    small_names = [n for n in _WEIGHTS if n not in _BIG and n not in ('w_ada', 'b_ada')]
    packed_g, g_places = _pack_rows([g[n] for n in small_names], row_mult=256)
    gw_in_t, small_all = _mm("g_w_in", dproj, h, "tn", rider=(packed_g, _gather_def, (0.0, 0.6, 1.0)))
    gw['w_in'] = _from_padded(gw_in_t, 0)
    send_late = send_rows(big_names[-1:])
    sibling_late = _exchange("pair_exchange_late", send_late, _sibling_def)
    partials_late = _chip_partials("chip_partials_late", send_late, sibling_late, 128, _COMM_DTYPE)
    dh, landed_late = _mm("d_h", dproj, w_in_t, "nn", rider=(partials_late, _chips_def, (0.0, 1.0)))

    def mod1_bwd(tv, bv):
        x_t, dh_t, dxa_t = tv
        (sc,) = bv
        return [dxa_t + dh_t * (1.0 + sc)], [jnp.sum(dh_t * x_t, axis=0, keepdims=True), jnp.sum(dh_t, axis=0, keepdims=True)]

    (grad_x,), (d_sc1, d_sh1) = _rowwise("modulate1_bwd", mod1_bwd, [x, dh, dx_a], [sc1], [(D_MODEL, F32)], [(1, D_MODEL)] * 2, tile)

    dmod = jnp.concatenate([d_sh1, d_sc1, d_gt1, d_sh2, d_sc2, d_gt2], axis=1).reshape(6 * D_MODEL // 128, 128)
    dmod_all = _all_gather("gather_dmod", dmod)
    g['b_ada'] = _sum_leading("sum_dmod", dmod_all).reshape(1, 6 * D_MODEL)
    dmod_mine = lax.dynamic_slice(dmod_all.reshape(_N_DEV, 6 * D_MODEL), (0, me * 768), (_N_DEV, 768))
    g_w_ada = _mm("g_w_ada", c_act_all, _pad_rows(dmod_mine, 16), "tn")

    small_sum = _unpack_rows(_sum_leading("sum_small", small_all), g_places)
    for n, t in zip(small_names, small_sum, strict=True):
        g[n] = t
    for n in _LORA:
        g[n] = lax.dynamic_slice(g[n], (0, me * R_HEAD), (g[n].shape[0], R_HEAD))
    g['w_ada'] = g_w_ada

    summed = jnp.concatenate([_sum_leading("sum_early", landed_early), _sum_leading("sum_late", landed_late)], axis=0)
    g.update(_unpack_big(summed, p))

    delta, new_m, new_v = {}, {}, {}
    own_call = ['w_ada'] + big_names
    for n in own_call:
        delta[n], new_m[n], new_v[n] = _adamw("adamw_" + n, p[n], g[n], m[n], v[n])
    rest = [n for n in _WEIGHTS if n not in own_call]
    outs = _adamw_many("adamw_rest", *[[d[n].reshape(p[n].shape) for n in rest] for d in (p, g, m, v)])
    for d, o in zip((delta, new_m, new_v), outs, strict=True):
        d.update(zip(rest, o, strict=True))
    return loss, grad_x, g, delta, new_m, new_v


def kernel(x, c, w_ada, b_ada, w_in, b_in, g_ln_v, b_ln_v, w_spatial, b_spatial, mu_shift, w0, w_decay_up, a0, w_aaa_up, w_gate_up, k_k, k_a, r_k, gn_gain, gn_bias, w_branch_a, w_branch_b, w_out, b_out, ln1_g, ln1_b, w_ff1, b_ff1, w_ff2, b_ff2, ln2_g, ln2_b, loss_target, m_w_ada, m_b_ada, m_w_in, m_b_in, m_g_ln_v, m_b_ln_v, m_w_spatial, m_b_spatial, m_mu_shift, m_w0, m_w_decay_up, m_a0, m_w_aaa_up, m_w_gate_up, m_k_k, m_k_a, m_r_k, m_gn_gain, m_gn_bias, m_w_branch_a, m_w_branch_b, m_w_out, m_b_out, m_ln1_g, m_ln1_b, m_w_ff1, m_b_ff1, m_w_ff2, m_b_ff2, m_ln2_g, m_ln2_b, v_w_ada, v_b_ada, v_w_in, v_b_in, v_g_ln_v, v_b_ln_v, v_w_spatial, v_b_spatial, v_mu_shift, v_w0, v_w_decay_up, v_a0, v_w_aaa_up, v_w_gate_up, v_k_k, v_k_a, v_r_k, v_gn_gain, v_gn_bias, v_w_branch_a, v_w_branch_b, v_w_out, v_b_out, v_ln1_g, v_ln1_b, v_w_ff1, v_b_ff1, v_w_ff2, v_b_ff2, v_ln2_g, v_ln2_b):
    given = dict(locals())
    shapes = {n: given[n].shape for n in _WEIGHTS}
    def two_d(a_):
        a_ = a_[0]
        return a_.reshape(1, -1) if a_.ndim == 1 else a_
    p = {n: two_d(given[n]) for n in _WEIGHTS}
    m = {n: two_d(given["m_" + n]) for n in _WEIGHTS}
    v = {n: two_d(given["v_" + n]) for n in _WEIGHTS}
    loss, grad_x, g, delta, new_m, new_v = _step(p, m, v, x[0], c, loss_target[0])
    outs = [loss, grad_x[None]]
    for d in (g, delta, new_m, new_v):
        outs += [d[n].reshape(shapes[n]) for n in _WEIGHTS]
    return tuple(outs)
```

```python
import functools

import jax
import jax.numpy as jnp
from jax import lax
from jax.experimental import pallas as pl
from jax.experimental.pallas import tpu as pltpu

F32 = jnp.float32
_MXU_DTYPE = jnp.bfloat16
_HI = lax.Precision.HIGHEST
_VMEM_LIMIT = 48 * 1024 * 1024
_MESH_ID = pl.DeviceIdType.MESH
_N_DEV = 8

D_MODEL = 1024
G_WIDTH = 512
G_CHUNK = 128
R_WIDTH = 512
R_HEADS = 8
R_HEAD = 64
LORA_W, LORA_A, LORA_G = 32, 32, 96
D_FF = 4096
ALPHA = 2.0 ** 0.25
LN_EPS = 1e-5
GN_EPS = 64e-5
SCAN_CHUNK = 64
ADAM_LR, ADAM_B1, ADAM_B2, ADAM_EPS, ADAM_WD, ADAM_STEP = 0.001, 0.9, 0.999, 1e-08, 0.01, 10

P_COLS = 5120
RW_COLS = 2048
RW_USED = 3 * R_WIDTH + LORA_W + LORA_A + LORA_G
LORA_PAD = 256
IN_COLS = 2 * G_WIDTH + RW_USED + 2 * D_MODEL


def _cparams(sem=None, **kw):
    if sem is not None:
        kw["dimension_semantics"] = sem
    return pltpu.CompilerParams(vmem_limit_bytes=_VMEM_LIMIT, **kw)


def _dot(a, b, dims=(((1,), (0,)), ((), ())), hi=False):
    if hi:
        return lax.dot_general(a.astype(F32), b.astype(F32), dims, precision=_HI, preferred_element_type=F32)
    return lax.dot_general(a.astype(_MXU_DTYPE), b.astype(_MXU_DTYPE), dims, preferred_element_type=F32)


_NN = (((1,), (0,)), ((), ()))
_NT = (((1,), (1,)), ((), ()))
_TN = (((0,), (0,)), ((), ()))


def _pick(n, pref):
    for t in pref:
        if n % t == 0:
            return t
    return n


def _mm(name, a, b, mode, bias=None, out_dtype=F32, epi=None, epi_dtype=None, raw=True, beside=None, rider=None, tm=None, tn=None, tk=None):
    if mode == "nn":
        (M, K), (_, N) = a.shape, b.shape
    elif mode == "nt":
        (M, K), (N, _) = a.shape, b.shape
    else:
        (K, M), (_, N) = a.shape, b.shape
    tm = tm or _pick(M, (2048, 1280, 1024, 512, 256, 128, 64, 32, 16, 8))
    tn = tn or _pick(N, (1024, 512, 640, 384, 256, 128))
    tk = tk or _pick(K, (1024, 512, 256, 128))
    nk = K // tk
    dims = {"nn": _NN, "nt": _NT, "tn": _TN}[mode]
    a_spec = pl.BlockSpec((tk, tm), lambda i, j, k: (k, i)) if mode == "tn" else pl.BlockSpec((tm, tk), lambda i, j, k: (i, k))
    b_spec = pl.BlockSpec((tn, tk), lambda i, j, k: (j, k)) if mode == "nt" else pl.BlockSpec((tk, tn), lambda i, j, k: (k, j))
    o_spec = pl.BlockSpec((tm, tn), lambda i, j, k: (i, j))
    has_bias, has_beside = bias is not None, beside is not None
    two = epi is not None and not has_beside and raw
    only_epi = epi is not None and not has_beside and not raw
    grid = (M // tm, N // tn, nk)
    ride_shape, ride_sems, _ = rider[1](rider[0]) if rider else (None, [], None)

    def body(*refs):
        refs = list(refs)
        n_in = 2 + has_bias + has_beside
        if rider:
            sem_refs = [refs.pop() for _ in ride_sems][::-1]
            ride_out = refs.pop(n_in + 1 + 1 + two)
            ride_in = refs.pop(n_in)
            step = (pl.program_id(0) * grid[1] + pl.program_id(1)) * grid[2] + pl.program_id(2)
            _ride(rider, step, grid[0] * grid[1] * grid[2], ride_in, ride_out, sem_refs)
        a_ref, b_ref = refs[0], refs[1]
        bias_ref = refs[2] if has_bias else None
        beside_ref = refs[n_in - 1] if has_beside else None
        outs = refs[n_in:]
        o_ref, acc_ref = outs[0], outs[-1]
        k = pl.program_id(2)

        @pl.when(k == 0)
        def _():
            acc_ref[...] = jnp.zeros_like(acc_ref)

        acc_ref[...] += _dot(a_ref[...], b_ref[...], dims)

        @pl.when(k == nk - 1)
        def _():
            res = acc_ref[...]
            if has_bias:
                res = res + bias_ref[...]
            if has_beside:
                o_ref[...] = epi(res, beside_ref[...]).astype(o_ref.dtype)
            elif only_epi:
                o_ref[...] = epi(res).astype(o_ref.dtype)
            else:
                o_ref[...] = res.astype(o_ref.dtype)
                if two:
                    outs[1][...] = epi(res).astype(outs[1].dtype)

    in_specs = [a_spec, b_spec]
    args = [a, b]
    if has_bias:
        in_specs.append(pl.BlockSpec((1, tn), lambda i, j, k: (0, j)))
        args.append(bias)
    if has_beside:
        in_specs.append(o_spec)
        args.append(beside)
    out_shape = [jax.ShapeDtypeStruct((M, N), epi_dtype if (has_beside or only_epi) else out_dtype)]
    out_specs = [o_spec]
    if two:
        out_shape.append(jax.ShapeDtypeStruct((M, N), epi_dtype))
        out_specs.append(o_spec)
    if rider:
        in_specs.append(_HBM)
        args.append(rider[0])
        out_shape.append(ride_shape)
        out_specs.append(_HBM)
    res = pl.pallas_call(
        body, name=name, grid=grid, in_specs=in_specs, out_specs=out_specs, out_shape=out_shape,
        scratch_shapes=[pltpu.VMEM((tm, tn), F32)] + list(ride_sems),
        compiler_params=_cparams(("arbitrary",) * 3 if rider else ("parallel", "parallel", "arbitrary")),
    )(*args)
    return res if (two or rider) else res[0]


_HALO = 8


def _rowwise(name, fn, tiled, bcast, tiled_out, red_out, tile, reverse=False, scratch=(), rider=None):
    tiled = [t if isinstance(t, tuple) else (t, t.shape[1], 0) for t in tiled]
    T = next(t[0] for t in tiled if not isinstance(t[0], str)).shape[0]
    n = T // tile
    nt, nb, nto, nsc, nro = len(tiled), len(bcast), len(tiled_out), len(scratch), len(red_out)
    ride_shape, ride_sems, _ = rider[1](rider[0]) if rider else (None, [], None)
    nr = 1 if rider else 0
    into = [(j, t[2], t[3]) for j, t in enumerate(tiled_out) if len(t) == 4]
    na = len(into)

    def row_block(i):
        return n - 1 - i if reverse else i

    def body(*refs):
        i2 = nt + nb
        o0 = i2 + nr + na
        o1, o2 = o0 + nto, o0 + nto + nro
        s0 = o2 + nr
        if rider:
            _ride(rider, pl.program_id(0), n, refs[i2], refs[o2], refs[s0 + nsc:])
        t_refs, b_refs = refs[:nt], refs[nt:i2]
        to_refs, ro_refs = refs[o0:o1], refs[o1:o2]
        extra = (list(refs[s0:s0 + nsc]),) if nsc else ()
        touts, routs = fn([r[...] for r in t_refs], [r[...] for r in b_refs], *extra)
        for r, v in zip(to_refs, touts, strict=True):
            r[...] = v.astype(r.dtype)
        if ro_refs:
            i = pl.program_id(0)

            @pl.when(i == 0)
            def _():
                for r, v in zip(ro_refs, routs, strict=True):
                    r[...] = v.astype(F32)

            @pl.when(i > 0)
            def _():
                for r, v in zip(ro_refs, routs, strict=True):
                    r[...] += v.astype(F32)

    def whole(shape):
        nd = len(shape)
        return pl.BlockSpec(tuple(shape), lambda i: (0,) * nd)

    per_tile = tile // _HALO
    in_specs, arrays = [], []
    for t in tiled:
        if isinstance(t[0], str):
            _, arr, w, cb = t
            in_specs.append(pl.BlockSpec((_HALO, w), functools.partial(lambda i, cb: (jnp.maximum(row_block(i) * per_tile - 1, 0), cb), cb=cb)))
        else:
            arr, w, cb = t
            in_specs.append(pl.BlockSpec((tile, w), functools.partial(lambda i, cb: (row_block(i), cb), cb=cb)))
        arrays.append(arr)
    in_specs += [whole(b.shape) for b in bcast]
    out_specs, out_shape = [], []
    for t in tiled_out:
        cb = t[3] if len(t) == 4 else 0
        out_specs.append(pl.BlockSpec((tile, t[0]), functools.partial(lambda i, cb: (row_block(i), cb), cb=cb)))
        out_shape.append(jax.ShapeDtypeStruct(t[2].shape if len(t) == 4 else (T, t[0]), t[1]))
    out_specs += [whole(s) for s in red_out]
    out_shape += [jax.ShapeDtypeStruct(tuple(s), F32) for s in red_out]
    ride_args = []
    if rider:
        in_specs.append(_HBM)
        out_specs.append(_HBM)
        out_shape.append(ride_shape)
        ride_args = [rider[0]]
    in_specs += [pl.BlockSpec(memory_space=pl.ANY)] * na
    aliases = {nt + nb + nr + k: j for k, (j, _, _) in enumerate(into)}
    res = pl.pallas_call(
        body, name=name, grid=(n,), in_specs=in_specs, out_specs=out_specs, out_shape=out_shape,
        scratch_shapes=[pltpu.VMEM(tuple(s), F32) for s in scratch] + list(ride_sems),
        input_output_aliases=aliases, compiler_params=_cparams(("arbitrary",)),
    )(*arrays, *bcast, *ride_args, *[buf for (_, buf, _) in into])
    if rider:
        return list(res[:nto]), list(res[nto:nto + nro]), res[-1]
    return list(res[:nto]), list(res[nto:])


def _rowwise_vjp(name, f, tiled, bcast, cts, tile, wrt_t, wrt_b, t_dtypes=None, colsum=(), prep=None, finish=None,
                 out_widths=None, reverse=False, scratch=(), rider=None, into=None):
    tiled = [t if isinstance(t, tuple) else (t, t.shape[1], 0) for t in tiled]
    npr = len(tiled)
    t_dtypes = t_dtypes or [F32] * len(wrt_t)
    groups = [c if isinstance(c, list) else [c] for c in cts]
    cts = [a_ for grp in groups for a_ in grp]

    def fn(tv, bv, sc=None):
        prim, flat_ct = tv[:npr], list(tv[npr:])
        if prep is not None:
            prim = prep(prim)
        ct = []
        for grp in groups:
            parts = [flat_ct.pop(0).astype(F32) for _ in grp]
            ct.append(functools.reduce(lambda p_, q_: p_ + q_, parts))

        def g(dt_vals, db_vals):
            full_t, full_b = list(prim), list(bv)
            for i, v in zip(wrt_t, dt_vals, strict=True):
                full_t[i] = v
            for j, v in zip(wrt_b, db_vals, strict=True):
                full_b[j] = v
            return f(full_t, full_b)

        outs, pull = jax.vjp(g, [prim[i].astype(F32) for i in wrt_t], [bv[j] for j in wrt_b])
        dts, dbs = pull([c.astype(o.dtype) for c, o in zip(ct, outs, strict=True)])
        if finish is not None:
            dts = finish(dts, sc)
        sums = [jnp.sum(dts[i].astype(F32), axis=0, keepdims=True) for i in colsum]
        return dts, list(dbs) + sums

    widths = out_widths or [tiled[i][1] for i in wrt_t]
    tiled_out = [(w, dt) + tuple((into or {}).get(j, ())) for j, (w, dt) in enumerate(zip(widths, t_dtypes, strict=True))]
    red_out = [bcast[j].shape for j in wrt_b] + [(1, widths[i]) for i in colsum]
    res = _rowwise(name, fn, tiled + list(cts), bcast, tiled_out, red_out, tile, reverse=reverse, scratch=scratch, rider=rider)
    dts, reds = res[0], res[1]
    nb = len(wrt_b)
    return (dts, reds[:nb], reds[nb:]) + tuple(res[2:])


def _layer_norm(x, g, b, eps):
    mu = jnp.mean(x, axis=-1, keepdims=True)
    xc = x - mu
    var = jnp.mean(xc * xc, axis=-1, keepdims=True)
    return xc * lax.rsqrt(var + eps) * g + b


def _gelu_tanh(x):
    return 0.5 * x * (1.0 + jnp.tanh(0.7978845608028654 * (x + 0.044715 * (x * x * x))))


def _sigmoid(x):
    return 1.0 / (1.0 + jnp.exp(-x))


def _seg_modulate(tv, bv):
    (x,), (sc, sh) = tv, bv
    return [x * (1.0 + sc) + sh]


def _seg_gmlp(tv, bv):
    (z,), (g_ln, b_ln, ws, b_tg, expand) = tv, bv
    bias_full = _dot(b_tg, expand, hi=True)
    zz = _gelu_tanh(z)
    u, v = zz[:, :G_WIDTH], zz[:, G_WIDTH:]
    v = _layer_norm(v, g_ln, b_ln, LN_EPS)
    row = lax.broadcasted_iota(jnp.int32, (G_CHUNK, G_CHUNK), 0)
    col = lax.broadcasted_iota(jnp.int32, (G_CHUNK, G_CHUNK), 1)
    causal = col <= row
    first_group = lax.broadcasted_iota(jnp.int32, (G_CHUNK, 128), 1) < 64
    parts = []
    for p in range(4):
        vp = v[:, 128 * p:128 * (p + 1)]
        s_even = _dot(jnp.where(causal, ws[2 * p], 0.0), vp)
        s_odd = _dot(jnp.where(causal, ws[2 * p + 1], 0.0), vp)
        parts.append(jnp.where(first_group, s_even, s_odd))
    s = jnp.concatenate(parts, axis=1) + bias_full
    return [u * s]


def _split2(x):
    hi = x.astype(_MXU_DTYPE)
    return hi, (x - hi.astype(F32)).astype(_MXU_DTYPE)


@jax.custom_vjp
def _group_sum(x, ones_blocks):
    hi, lo = _split2(x)
    return _dot(hi, ones_blocks) + _dot(lo, ones_blocks)


def _group_sum_fwd(x, ones_blocks):
    return _group_sum(x, ones_blocks), ones_blocks


def _group_sum_bwd(ones_blocks, ct):
    return _group_sum(ct, ones_blocks), jnp.zeros_like(ones_blocks)


_group_sum.defvjp(_group_sum_fwd, _group_sum_bwd)


def _shift_down(z, halo, is_first):
    _, W = z.shape
    rolled = pltpu.roll(z, 1, 0)
    before = jnp.where(is_first, 0.0, pltpu.roll(halo, 1, 0))
    top_row = lax.broadcasted_iota(jnp.int32, (_HALO, W), 0) == 0
    return jnp.concatenate([jnp.where(top_row, before, rolled[:_HALO]), rolled[_HALO:]], axis=0)


def _shift_up(d, after):
    tile, W = d.shape
    rolled = pltpu.roll(d, tile - 1, 0)
    last_row = lax.broadcasted_iota(jnp.int32, (_HALO, W), 0) == _HALO - 1
    bottom = jnp.where(last_row, pltpu.roll(after, _HALO - 1, 0), rolled[tile - _HALO:])
    return jnp.concatenate([rolled[:tile - _HALO], bottom], axis=0)


def _seg_rwkv_pre(tv, bv):
    (z, prev), (mu, w0, wd, a0, wa, wg, k_k, k_a, gsum) = tv, bv
    zs = z + (prev - z) * mu
    r, k, v = zs[:, 0:512], zs[:, 512:1024], zs[:, 1024:1536]
    zl = zs[:, 3 * R_WIDTH:3 * R_WIDTH + LORA_PAD]
    x = w0 + _dot(jnp.tanh(zl), wd)
    softplus = jnp.maximum(-x, 0.0) + jnp.log(1.0 + jnp.exp(-jnp.abs(x)))
    lw = -jnp.exp(-softplus - 0.5)
    a = _sigmoid(a0 + _dot(zl, wa))
    g = _dot(_sigmoid(zl), wg)
    kk = k * k_k
    nrm = jnp.sqrt(_group_sum(kk * kk, gsum))
    kk = kk / jnp.maximum(nrm, 1e-12)
    k2 = k * (1.0 + (a - 1.0) * k_a)
    return [r, lw, k2, v, -kk, kk * a, g]


def _seg_rwkv_post(tv, bv):
    (y, r, k2, v, g), (r_k, gain, bias, gsum) = tv, bv
    mu = _group_sum(y, gsum) * (1.0 / R_HEAD)
    yc = y - mu
    var = _group_sum(yc * yc, gsum) * (1.0 / R_HEAD)
    yn = yc * lax.rsqrt(var + GN_EPS) * gain + bias
    bonus = _group_sum(r * k2 * r_k, gsum) * v
    return [(yn + bonus) * g]


def _seg_merge(tv, bv):
    (ga, gb, pa, pb), () = tv, bv
    return [_sigmoid(ga) * pa + _sigmoid(gb) * pb]


def _seg_mid(tv, bv):
    (x, mix), (gt1, g1, b1, sc2, sh2) = tv, bv
    h1 = _layer_norm(ALPHA * x + gt1 * mix, g1, b1, LN_EPS)
    return [h1, h1 * (1.0 + sc2) + sh2]


def _seg_relu2(tv, bv):
    (f1,), () = tv, bv
    return [jnp.square(jnp.maximum(f1, 0.0))]


def _seg_loss(tv, bv):
    (h1, ff, target), (gt2, g2, b2) = tv, bv
    out = _layer_norm(ALPHA * h1 + gt2 * ff, g2, b2, LN_EPS)
    err = jnp.square(out - target)
    return 0.5 * jnp.sum(jnp.mean(err, axis=-1))


_BNN = (((2,), (1,)), ((0,), (0,)))
_BNT = (((2,), (2,)), ((0,), (0,)))
_BTN = (((1,), (1,)), ((0,), (0,)))


def _tri_dot(x, dims):
    L = x.shape[0]
    tri = (lax.broadcasted_iota(jnp.int32, (L, L), 1) <= lax.broadcasted_iota(jnp.int32, (L, L), 0)).astype(F32)
    hi, lo = _split2(x)
    return _dot(tri, hi, dims) + _dot(tri, lo, dims)


@jax.custom_vjp
def _running_sum(x):
    return _tri_dot(x, _NN)


_running_sum.defvjp(lambda x: (_tri_dot(x, _NN), None), lambda _, ct: (_tri_dot(ct, _TN),))


def _cut_heads(x):
    return jnp.stack([x[:, R_HEAD * h:R_HEAD * (h + 1)] for h in range(R_HEADS)])


def _join_heads(x):
    return jnp.concatenate([x[h] for h in range(R_HEADS)], axis=1)


@jax.custom_vjp
def _split_heads(x):
    return _cut_heads(x)


@jax.custom_vjp
def _merge_heads(x):
    return _join_heads(x)


_split_heads.defvjp(lambda x: (_cut_heads(x), None), lambda _, ct: (_join_heads(ct),))
_merge_heads.defvjp(lambda x: (_join_heads(x), None), lambda _, ct: (_cut_heads(ct),))


def _inverse_pullback(inv, ct):
    return _dot(_dot(inv, ct, _BTN), inv, _BNT)


@jax.custom_vjp
def _unit_lower_inverse(n_mat):
    H, L, _ = n_mat.shape
    eye = lax.broadcasted_iota(jnp.int32, (H, L, L), 1) == lax.broadcasted_iota(jnp.int32, (H, L, L), 2)
    inv = jnp.where(eye, 1.0, 0.0) + n_mat
    pw = n_mat
    n = 2
    while n < L:
        pw = _dot(pw, pw, _BNN)
        inv = inv + _dot(inv, pw, _BNN)
        n *= 2
    return inv


def _unit_lower_inverse_fwd(n_mat):
    inv = _unit_lower_inverse(n_mat)
    return inv, inv


_unit_lower_inverse.defvjp(_unit_lower_inverse_fwd, lambda inv, ct: (_inverse_pullback(inv, ct),))


@jax.custom_vjp
def _known_inverse(n_mat, inv):
    return inv


_known_inverse.defvjp(lambda n_mat, inv: (inv, inv), lambda inv, ct: (_inverse_pullback(inv, ct), jnp.zeros_like(inv)))


def _scan_chunk(r, lw, k, v, a, b, s0, inv=None, with_inverse=False):
    L, H = r.shape[0], R_HEADS
    row = lax.broadcasted_iota(jnp.int32, (H, L, L), 1)
    col = lax.broadcasted_iota(jnp.int32, (H, L, L), 2)
    incl, strict = col <= row, col < row
    cs = _running_sum(lw)
    cs_end = cs[L - 1:L, :]
    p, p_inv, to_end = jnp.exp(cs), jnp.exp(-cs), jnp.exp(cs_end - cs)
    at, bt, kt, rt = [_split_heads(t) for t in (a * jnp.exp(cs - lw), b * p_inv, k * p_inv, r * p)]
    b_end, k_end, v = [_split_heads(t) for t in (b * to_end, k * to_end, v)]
    a_ab = jnp.where(strict, _dot(at, bt, _BNT), 0.0)
    a_ak = jnp.where(strict, _dot(at, kt, _BNT), 0.0)
    a_rb = jnp.where(incl, _dot(rt, bt, _BNT), 0.0)
    a_rk = jnp.where(incl, _dot(rt, kt, _BNT), 0.0)
    inv = _unit_lower_inverse(a_ab) if inv is None else _known_inverse(a_ab, inv)
    u = _dot(inv, _dot(at, s0, _BNT) + _dot(a_ak, v, _BNN), _BNN)
    y = _merge_heads(_dot(rt, s0, _BNT) + _dot(a_rb, u, _BNN) + _dot(a_rk, v, _BNN))
    s1 = s0 * _split_heads(jnp.exp(cs_end)) + _dot(u, b_end, _BTN) + _dot(v, k_end, _BTN)
    return (y, s1, inv) if with_inverse else (y, s1)


def _scan_fwd(r, lw, k, v, a, b, rider):
    T = r.shape[0]
    H, N, L = R_HEADS, R_HEAD, SCAN_CHUNK
    nc = T // L
    ride_shape, ride_sems, _ = rider[1](rider[0])

    def body(r_ref, lw_ref, k_ref, v_ref, a_ref, b_ref, ride_in, y_ref, st_ref, inv_ref, ride_out, s_ref, *sem_refs):
        _ride(rider, pl.program_id(0), nc, ride_in, ride_out, sem_refs)

        @pl.when(pl.program_id(0) == 0)
        def _():
            s_ref[...] = jnp.zeros_like(s_ref)

        s0 = s_ref[...]
        st_ref[0] = s0
        y, s1, inv = _scan_chunk(*[t[...] for t in (r_ref, lw_ref, k_ref, v_ref, a_ref, b_ref)], s0, with_inverse=True)
        y_ref[...] = y
        inv_ref[0] = inv
        s_ref[...] = s1

    blk = pl.BlockSpec((L, R_WIDTH), lambda c: (c, 0))
    per_chunk = pl.BlockSpec((1, H, N, N), lambda c: (c, 0, 0, 0))
    return pl.pallas_call(
        body, name="scan_fwd", grid=(nc,), in_specs=[blk] * 6 + [_HBM], out_specs=[blk, per_chunk, per_chunk, _HBM],
        out_shape=[jax.ShapeDtypeStruct((T, R_WIDTH), F32)] + [jax.ShapeDtypeStruct((nc, H, N, N), F32)] * 2 + [ride_shape],
        scratch_shapes=[pltpu.VMEM((H, N, N), F32)] + list(ride_sems),
        compiler_params=_cparams(("arbitrary",)),
    )(r, lw, k, v, a, b, rider[0])


def _scan_bwd(r, lw, k, v, a, b, states, inverses, dy, rider):
    T = r.shape[0]
    H, N, L = R_HEADS, R_HEAD, SCAN_CHUNK
    nc = T // L
    ride_shape, ride_sems, _ = rider[1](rider[0])

    def body(r_ref, lw_ref, k_ref, v_ref, a_ref, b_ref, st_ref, inv_ref, dy_ref, ride_in,
             dr_ref, dlw_ref, dk_ref, dv_ref, da_ref, db_ref, ride_out, ds_ref, *sem_refs):
        _ride(rider, pl.program_id(0), nc, ride_in, ride_out, sem_refs)

        @pl.when(pl.program_id(0) == 0)
        def _():
            ds_ref[...] = jnp.zeros_like(ds_ref)

        args = [t[...] for t in (r_ref, lw_ref, k_ref, v_ref, a_ref, b_ref)] + [st_ref[0]]
        inv = inv_ref[0]
        _, pull = jax.vjp(lambda *xs: _scan_chunk(*xs, inv=inv), *args)
        grads = pull((dy_ref[...], ds_ref[...]))
        for o_ref, g_ in zip((dr_ref, dlw_ref, dk_ref, dv_ref, da_ref, db_ref, ds_ref), grads, strict=True):
            o_ref[...] = g_

    blk = pl.BlockSpec((L, R_WIDTH), lambda c: (nc - 1 - c, 0))
    per_chunk = pl.BlockSpec((1, H, N, N), lambda c: (nc - 1 - c, 0, 0, 0))
    return pl.pallas_call(
        body, name="scan_bwd", grid=(nc,), in_specs=[blk] * 6 + [per_chunk, per_chunk, blk, _HBM], out_specs=[blk] * 6 + [_HBM],
        out_shape=[jax.ShapeDtypeStruct((T, R_WIDTH), F32)] * 6 + [ride_shape],
        scratch_shapes=[pltpu.VMEM((H, N, N), F32)] + list(ride_sems),
        compiler_params=_cparams(("arbitrary",)),
    )(r, lw, k, v, a, b, states, inverses, dy, rider[0])


def _place():
    x, y, c = lax.axis_index("x"), lax.axis_index("y"), lax.axis_index("c")
    return x, y, c


def _gather_def(block):
    R, C = block.shape

    def phases(x_ref, out_ref, send_sems, recv_sems, local_sem):
        x, y, c = _place()
        me, sibling = (x, y, c), (x, y, 1 - c)
        chips = [(1 - x, y), (x, 1 - y), (1 - x, 1 - y)]

        def slot(px, py, pc):
            return out_ref.at[4 * px + 2 * py + pc]

        def copy(k, blk, to, src=None):
            return pltpu.make_async_remote_copy(
                src_ref=slot(*blk) if src is None else src, dst_ref=slot(*blk),
                send_sem=send_sems.at[k], recv_sem=recv_sems.at[k], device_id=to, device_id_type=_MESH_ID)

        mine = pltpu.make_async_copy(x_ref, slot(*me), local_sem)
        first = [copy(0, me, sibling, src=x_ref)]
        first += [copy(1 + j, me, (*chip, c), src=x_ref) for j, chip in enumerate(chips)]
        passed = [copy(4 + j, (*chip, c), sibling) for j, chip in enumerate(chips)]

        def begin():
            mine.start()
            for cp in first:
                cp.start()

        def forward():
            for j, chip in enumerate(chips):
                copy(1 + j, (*chip, c), me).wait_recv()
                passed[j].start()

        def finish():
            copy(0, sibling, me).wait_recv()
            for j, chip in enumerate(chips):
                copy(4 + j, (*chip, 1 - c), me).wait_recv()
            for cp in first + passed:
                cp.wait_send()
            mine.wait()

        return [begin, forward, finish]

    sems = [pltpu.SemaphoreType.DMA((7,)), pltpu.SemaphoreType.DMA((7,)), pltpu.SemaphoreType.DMA]
    return jax.ShapeDtypeStruct((_N_DEV, R, C), block.dtype), sems, phases


def _sibling_def(blocks):
    _, R, C = blocks.shape

    def phases(x_ref, out_ref, send_sems, recv_sems):
        x, y, c = _place()
        copies = [pltpu.make_async_remote_copy(
            src_ref=x_ref.at[2 * q + (1 - c)], dst_ref=out_ref.at[q], send_sem=send_sems.at[q], recv_sem=recv_sems.at[q],
            device_id=(x, y, 1 - c), device_id_type=_MESH_ID) for q in range(4)]

        def begin():
            for cp in copies:
                cp.start()

        def finish():
            for cp in copies:
                cp.wait()

        return [begin, finish]

    return jax.ShapeDtypeStruct((4, R, C), blocks.dtype), [pltpu.SemaphoreType.DMA((4,)), pltpu.SemaphoreType.DMA((4,))], phases


def _chips_def(partials):
    def phases(x_ref, out_ref, send_sems, recv_sems, local_sem):
        x, y, c = _place()
        my_chip = 2 * x + y
        mine = pltpu.make_async_copy(x_ref.at[my_chip], out_ref.at[my_chip], local_sem)
        copies = []
        for rel in range(1, 4):
            px, py = (1 - x if (rel >> 1) & 1 else x), (1 - y if rel & 1 else y)
            copies.append(pltpu.make_async_remote_copy(
                src_ref=x_ref.at[2 * px + py], dst_ref=out_ref.at[my_chip],
                send_sem=send_sems.at[rel - 1], recv_sem=recv_sems.at[rel - 1],
                device_id=(px, py, c), device_id_type=_MESH_ID))

        def begin():
            mine.start()
            for cp in copies:
                cp.start()

        def finish():
            for cp in copies:
                cp.wait()
            mine.wait()

        return [begin, finish]

    sems = [pltpu.SemaphoreType.DMA((3,)), pltpu.SemaphoreType.DMA((3,)), pltpu.SemaphoreType.DMA]
    return jax.ShapeDtypeStruct(partials.shape, partials.dtype), sems, phases


_HBM = pl.BlockSpec(memory_space=pltpu.HBM)


def _exchange(name, array, definition):
    out_shape, sems, phases = definition(array)

    def body(x_ref, out_ref, *sem_refs):
        for phase in phases(x_ref, out_ref, *sem_refs):
            phase()

    return pl.pallas_call(body, name=name, in_specs=[_HBM], out_specs=_HBM, out_shape=out_shape, scratch_shapes=sems)(array)


def _ride(rider, step, nsteps, x_ref, out_ref, sem_refs):
    array, definition, fractions = rider
    for phase, frac in zip(definition(array)[2](x_ref, out_ref, *sem_refs), fractions, strict=True):
        pl.when(step == min(int(frac * nsteps), nsteps - 1))(phase)


def _all_gather(name, block):
    return _exchange(name, block, _gather_def)


def _chip_partials(name, blocks, from_sibling, tile, out_dtype):
    _, R, C = blocks.shape

    def body(x_ref, s_ref, o_ref):
        c = lax.axis_index("c")
        for q in range(4):
            o_ref[q] = (x_ref[2 * q + c] + s_ref[q]).astype(o_ref.dtype)

    return pl.pallas_call(
        body, name=name, grid=(R // tile,),
        in_specs=[pl.BlockSpec((_N_DEV, tile, C), lambda i: (0, i, 0)), pl.BlockSpec((4, tile, C), lambda i: (0, i, 0))],
        out_specs=pl.BlockSpec((4, tile, C), lambda i: (0, i, 0)), out_shape=jax.ShapeDtypeStruct((4, R, C), out_dtype),
        compiler_params=_cparams(("parallel",)),
    )(blocks, from_sibling)


def _sum_leading(name, x, tile=None):
    n, R, C = x.shape
    tile = tile or _pick(R, (512, 256, 128, 64, 32, 16, 8))

    def body(x_ref, o_ref):
        acc = x_ref[0].astype(F32)
        for k in range(1, n):
            acc = acc + x_ref[k].astype(F32)
        o_ref[...] = acc

    return pl.pallas_call(
        body, name=name, grid=(R // tile,), in_specs=[pl.BlockSpec((n, tile, C), lambda i: (0, i, 0))],
        out_specs=pl.BlockSpec((tile, C), lambda i: (i, 0)), out_shape=jax.ShapeDtypeStruct((R, C), F32),
        compiler_params=_cparams(("parallel",)),
    )(x)


def _adamw_update(w_, g_, m_, v_):
    m2 = ADAM_B1 * m_ + (1.0 - ADAM_B1) * g_
    v2 = ADAM_B2 * v_ + (1.0 - ADAM_B2) * jnp.square(g_)
    m_hat = m2 / (1.0 - ADAM_B1 ** ADAM_STEP)
    v_hat = v2 / (1.0 - ADAM_B2 ** ADAM_STEP)
    delta = -ADAM_LR * (m_hat / (jnp.sqrt(v_hat) + ADAM_EPS) + ADAM_WD * w_)
    return delta, m2, v2


def _adamw(name, w, g, m, v):
    R, C = w.shape
    tile = _pick(R, (256, 128, 64, 32, 16, 8))
    outs, _ = _rowwise(name, lambda tv, bv: (list(_adamw_update(*tv)), []), [w, g, m, v], [], [(C, F32)] * 3, [], tile)
    return outs


def _adamw_many(name, ws, gs, ms, vs):
    n = len(ws)

    def body(*refs):
        ins, outs = refs[:4 * n], refs[4 * n:]
        for i in range(n):
            res = _adamw_update(ins[i][...], ins[n + i][...], ins[2 * n + i][...], ins[3 * n + i][...])
            for j in range(3):
                outs[j * n + i][...] = res[j]

    out_shape = [jax.ShapeDtypeStruct(w.shape, F32) for w in ws] * 3
    res = pl.pallas_call(body, name=name, out_shape=out_shape, compiler_params=_cparams())(*ws, *gs, *ms, *vs)
    return res[:n], res[n:2 * n], res[2 * n:]


def _pack_rows(arrs, lanes=128, row_mult=8):
    flat, places, off = [], [], 0
    for a_ in arrs:
        n = a_.size
        flat.append(a_.reshape(-1).astype(F32))
        places.append((off, n, a_.shape))
        off += n
    total = -(-off // (lanes * row_mult)) * (lanes * row_mult)
    if total > off:
        flat.append(jnp.zeros((total - off,), F32))
    return jnp.concatenate(flat).reshape(total // lanes, lanes), places


def _unpack_rows(packed, places):
    flat = packed.reshape(-1)
    return [flat[o:o + n].reshape(s) for (o, n, s) in places]


_WEIGHTS = ['w_ada', 'b_ada', 'w_in', 'b_in', 'g_ln_v', 'b_ln_v', 'w_spatial', 'b_spatial', 'mu_shift', 'w0', 'w_decay_up', 'a0',
            'w_aaa_up', 'w_gate_up', 'k_k', 'k_a', 'r_k', 'gn_gain', 'gn_bias', 'w_branch_a', 'w_branch_b', 'w_out', 'b_out',
            'ln1_g', 'ln1_b', 'w_ff1', 'b_ff1', 'w_ff2', 'b_ff2', 'ln2_g', 'ln2_b']
_BIG = {'w_ff1': (0, 512), 'w_ff2': (512, 512), 'w_out': (1024, 128), 'w_branch_a': (1152, 64), 'w_branch_b': (1216, 64), 'w_in': (1280, 640)}
_LATER_ROWS = 1280
_CUT_BY_COLS = ('w_ff1', 'w_in', 'w_branch_a', 'w_branch_b')
IN_SHARD = IN_COLS // _N_DEV
_LORA = {'w_decay_up': (0, LORA_W), 'w_aaa_up': (LORA_W, LORA_A), 'w_gate_up': (LORA_W + LORA_A, LORA_G)}
_COMM_DTYPE = jnp.bfloat16


def _pad_rows(a, rows):
    return jnp.pad(a, ((0, rows - a.shape[0]),) + ((0, 0),) * (a.ndim - 1))


def _pack_big(shards):
    blocks = []
    for n, (_, rows) in _BIG.items():
        a = shards[n].T if n in _CUT_BY_COLS else shards[n]
        blocks.append(_pad_rows(a.reshape(-1, D_MODEL), rows))
    return jnp.concatenate(blocks, axis=0)


def _unpack_big(block, like):
    out = {}
    for n, (r0, _) in _BIG.items():
        rr, cc = like[n].shape
        if n in _CUT_BY_COLS:
            out[n] = block[r0:r0 + rr * cc // D_MODEL].reshape(cc, rr).T
        else:
            out[n] = block[r0:r0 + rr]
    return out


def _to_padded(a, axis):
    g_end = 2 * G_WIDTH
    r_end = g_end + RW_USED
    take = lambda lo, hi: lax.slice_in_dim(a, lo, hi, axis=axis)
    zshape = list(a.shape)
    zshape[axis] = RW_COLS - RW_USED
    return jnp.concatenate([take(r_end, IN_COLS), take(g_end, r_end), jnp.zeros(zshape, a.dtype), take(0, g_end)], axis=axis)


def _from_padded(a, axis):
    take = lambda lo, hi: lax.slice_in_dim(a, lo, hi, axis=axis)
    return jnp.concatenate([take(2 * D_MODEL + RW_COLS, P_COLS), take(2 * D_MODEL, 2 * D_MODEL + RW_USED), take(0, 2 * D_MODEL)], axis=axis)


def _step(p, m, v, x, c, target):
    T = x.shape[0]
    xi, yi, ci = _place()
    me = 4 * xi + 2 * yi + ci
    tile = 256

    lane = jnp.arange(R_WIDTH)
    gsum = (lane[:, None] // R_HEAD == lane[None, :] // R_HEAD).astype(F32)
    expand = (jnp.arange(128)[:, None] == (lane[None, :] // (G_WIDTH // 8))).astype(F32)

    (c_act,), _ = _rowwise("silu_c", lambda tv, bv: ([tv[0] * _sigmoid(tv[0])], []), [c], [], [(D_MODEL, F32)], [], 1)
    small, places = _pack_rows([c_act, p['w_decay_up'], p['w_aaa_up'], p['w_gate_up']])
    small_all = _all_gather("gather_small", small)
    per_dev = [_unpack_rows(small_all[d], places) for d in range(_N_DEV)]
    c_act_all = _pad_rows(jnp.concatenate([pd[0] for pd in per_dev], axis=0), 16)
    lora_full = {n: jnp.concatenate([pd[i + 1] for pd in per_dev], axis=1) for i, n in enumerate(_LORA)}
    lora_pad = {n: jnp.zeros((LORA_PAD, R_WIDTH), F32).at[r0:r0 + nr].set(lora_full[n]) for n, (r0, nr) in _LORA.items()}

    big_names = list(_BIG)
    my_rows = _pack_big(p).astype(_MXU_DTYPE)
    w_in_all = _all_gather("gather_w_in", my_rows[_LATER_ROWS:])
    w_in_t = _to_padded(w_in_all[:, :IN_SHARD].reshape(IN_COLS, D_MODEL), 0)
    b_in_p = _to_padded(p['b_in'], 1)
    mu_p = jnp.concatenate([p['mu_shift'], jnp.zeros((1, RW_COLS - RW_USED), F32)], axis=1)

    b_ada_mine = lax.dynamic_slice(p['b_ada'], (0, me * 768), (1, 768))
    mod_cols = _mm("ada_mod", c_act_all, p['w_ada'], "nn", bias=b_ada_mine)
    mod_all = _all_gather("gather_mod", mod_cols)
    mod = lax.dynamic_index_in_dim(mod_all, me, axis=1, keepdims=False).reshape(1, 6 * D_MODEL)
    sh1, sc1, gt1, sh2, sc2, gt2 = [mod[:, i * D_MODEL:(i + 1) * D_MODEL] for i in range(6)]

    (h,), _ = _rowwise("modulate1", lambda tv, bv: (_seg_modulate(tv, bv), []), [x], [sc1, sh1], [(D_MODEL, _MXU_DTYPE)], [], tile)
    proj = _mm("in_proj", h, w_in_t, "nt", bias=b_in_p)
    ws = p['w_spatial']
    b_tg = jnp.zeros((G_CHUNK, 128), F32).at[:, :8].set(p['b_spatial'].T)
    gmlp_b = [p['g_ln_v'], p['b_ln_v'], ws, b_tg, expand]
    z_gmlp = (proj, 2 * G_WIDTH, 4)
    (ya,), _ = _rowwise("gmlp", lambda tv, bv: (_seg_gmlp(tv, bv), []), [z_gmlp], gmlp_b, [(G_WIDTH, F32)], [], G_CHUNK)
    z_rw = (proj, RW_COLS, 1)
    z_rw_halo = ("halo", proj, RW_COLS, 1)
    pre_b = [mu_p, p['w0'], lora_pad['w_decay_up'], p['a0'], lora_pad['w_aaa_up'], lora_pad['w_gate_up'], p['k_k'], p['k_a'], gsum]

    def pre_fwd(tv, bv):
        z_t, halo_t = tv
        return _seg_rwkv_pre([z_t, _shift_down(z_t, halo_t, pl.program_id(0) == 0)], bv), []

    pre_out, _ = _rowwise("rwkv_pre", pre_fwd, [z_rw, z_rw_halo], pre_b, [(R_WIDTH, F32)] * 7, [], tile)
    r_, lw_, k2_, v_, a_, b_, g_ = pre_out
    scan_in = (r_, lw_, k2_, v_, a_, b_)
    y_, states, inverses, later_all = _scan_fwd(*scan_in, rider=(my_rows[:_LATER_ROWS], _gather_def, (0.0, 0.875, 1.0)))

    def whole(n, rows):
        r0 = _BIG[n][0]
        return later_all[:, r0:r0 + rows].reshape(_N_DEV * rows, D_MODEL)

    w_ff1_t, w_ff2, w_out = whole('w_ff1', 512), whole('w_ff2', 512), whole('w_out', 128)
    w_ba_t = whole('w_branch_a', 64).reshape(D_MODEL, G_WIDTH)
    w_bb_t = whole('w_branch_b', 64).reshape(D_MODEL, R_WIDTH)
    post_b = [p['r_k'].reshape(1, R_WIDTH), p['gn_gain'], p['gn_bias'], gsum]
    (yb,), _ = _rowwise("rwkv_post", lambda tv, bv: (_seg_rwkv_post(tv, bv), []), [y_, r_, k2_, v_, g_], post_b, [(R_WIDTH, F32)], [], tile)
    pa = _mm("branch_a", ya, w_ba_t, "nt", out_dtype=_MXU_DTYPE)
    pb = _mm("branch_b", yb, w_bb_t, "nt", out_dtype=_MXU_DTYPE)
    gates = [(proj, D_MODEL, 0), (proj, D_MODEL, 1)]
    (merged,), _ = _rowwise("merge", lambda tv, bv: (_seg_merge(tv, bv), []), gates + [pa, pb], [], [(D_MODEL, _MXU_DTYPE)], [], tile)
    mix = _mm("out_proj", merged, w_out, "nn", bias=p['b_out'])
    mid_b = [gt1, p['ln1_g'], p['ln1_b'], sc2, sh2]
    (h1, h2in), _ = _rowwise("mid", lambda tv, bv: (_seg_mid(tv, bv), []), [x, mix], mid_b, [(D_MODEL, F32), (D_MODEL, _MXU_DTYPE)], [], tile)
    act = _mm("ff1", h2in, w_ff1_t, "nt", bias=p['b_ff1'], epi=lambda t: _seg_relu2([t], [])[0], epi_dtype=_MXU_DTYPE, raw=False)
    ff = _mm("ff2", act, w_ff2, "nn", bias=p['b_ff2'])

    def loss_fn(tv, bv):
        h1_t, ff_t, tgt = tv
        val, grads = jax.value_and_grad(lambda a0_, a1_, b0_, b1_, b2_: _seg_loss([a0_, a1_, tgt], [b0_, b1_, b2_]), argnums=(0, 1, 2, 3, 4))(h1_t, ff_t, *bv)
        return [grads[0], grads[1]], [grads[2], grads[3], grads[4], jnp.sum(grads[1], axis=0, keepdims=True), jnp.full((1, 128), val, F32)]

    (dh1_a, dff), (d_gt2, d_ln2_g, d_ln2_b, d_b_ff2, loss_row) = _rowwise(
        "loss", loss_fn, [h1, ff, target], [gt2, p['ln2_g'], p['ln2_b']], [(D_MODEL, F32), (D_MODEL, _MXU_DTYPE)], [(1, D_MODEL)] * 4 + [(1, 128)], tile)
    loss = lax.psum(loss_row[0, 0], ("x", "y", "c"))

    g = {}
    g['ln2_g'], g['ln2_b'], g['b_ff2'] = d_ln2_g, d_ln2_b, d_b_ff2
    gw = {}
    gw['w_ff2'] = _mm("g_w_ff2", act, dff, "tn")
    df1 = _mm("d_act", dff, w_ff2, "nt", beside=act, epi=lambda d_, a_: d_ * (2.0 * jnp.sqrt(a_.astype(F32))), epi_dtype=_MXU_DTYPE)
    g['b_ff1'] = _mm("g_b_ff1", jnp.ones((16, T), _MXU_DTYPE), df1, "nn")[0:1]
    gw['w_ff1'] = _mm("g_w_ff1", df1, h2in, "tn")
    dh2in = _mm("d_h2in", df1, w_ff1_t, "nn")
    (dx_a, dmix), (d_gt1, g['ln1_g'], g['ln1_b'], d_sc2, d_sh2), (g['b_out'],) = _rowwise_vjp(
        "mid_bwd", _seg_mid, [x, mix], mid_b, [dh1_a, dh2in], tile, [0, 1], [0, 1, 2, 3, 4], t_dtypes=[F32, _MXU_DTYPE], colsum=[1])
    gw['w_out'] = _mm("g_w_out", merged, dmix, "tn")
    dmerged = _mm("d_merged", dmix, w_out, "nt", out_dtype=_MXU_DTYPE)
    (dproj, dpa, dpb), _, (cs_gates,) = _rowwise_vjp(
        "merge_bwd", _seg_merge, gates + [pa, pb], [], [dmerged], tile, [0, 1, 2, 3], [], t_dtypes=[_MXU_DTYPE] * 3, colsum=[0],
        finish=lambda dts, sc: [jnp.concatenate(dts[:2], axis=1), dts[2], dts[3]], out_widths=[2 * D_MODEL, D_MODEL, D_MODEL],
        into={0: (lax.empty((T, P_COLS), _MXU_DTYPE), 0)})
    gw['w_branch_a'] = _mm("g_w_branch_a", dpa, ya, "tn")
    gw['w_branch_b'] = _mm("g_w_branch_b", dpb, yb, "tn")
    dya = _mm("d_ya", dpa, w_ba_t, "nn")
    dyb = _mm("d_yb", dpb, w_bb_t, "nn")
    def send_rows(names):
        parts = []
        for n in names:
            per_dev = gw[n].reshape(_N_DEV, -1, D_MODEL)
            parts.append(jnp.pad(per_dev, ((0, 0), (0, _BIG[n][1] - per_dev.shape[1]), (0, 0))))
        return jnp.concatenate(parts, axis=1) if len(parts) > 1 else parts[0]

    send_early = send_rows(big_names[:-1])
    (dy, dr1, dk1, dv1, dg_), (d_r_k, g['gn_gain'], g['gn_bias']), _, sibling_early = _rowwise_vjp(
        "rwkv_post_bwd", _seg_rwkv_post, [y_, r_, k2_, v_, g_], post_b, [dyb], tile, [0, 1, 2, 3, 4], [0, 1, 2],
        rider=(send_early, _sibling_def, (0.0, 1.0)))
    g['r_k'] = d_r_k
    partials_early = _chip_partials("chip_partials_early", send_early, sibling_early, 128, _COMM_DTYPE)
    dr2, dlw, dk2, dv2, da, db, landed_early = _scan_bwd(*scan_in, states, inverses, dy, rider=(partials_early, _chips_def, (0.0, 1.0)))
    pre_tile = 128
    last_step = T // pre_tile - 1

    def pre_prep(prim):
        z_t, halo_t = prim
        return [z_t, _shift_down(z_t, halo_t, pl.program_id(0) == last_step)]

    def pre_finish(dts, sc):
        dz_direct, dprev = dts
        (row_after,) = sc

        @pl.when(pl.program_id(0) == 0)
        def _():
            row_after[...] = jnp.zeros_like(row_after)

        dz = dz_direct + _shift_up(dprev, row_after[...])
        row_after[...] = dprev[:_HALO]
        return [dz]

    (dproj,), (d_mu, g['w0'], d_wd, g['a0'], d_wa, d_wg, g['k_k'], g['k_a']), (cs_rw,) = _rowwise_vjp(
        "rwkv_pre_bwd", _seg_rwkv_pre, [z_rw, z_rw_halo], pre_b, [[dr1, dr2], dlw, [dk1, dk2], [dv1, dv2], da, db, dg_], pre_tile,
        [0, 1], [0, 1, 2, 3, 4, 5, 6, 7], t_dtypes=[_MXU_DTYPE], colsum=[0], prep=pre_prep, finish=pre_finish,
        out_widths=[RW_COLS], reverse=True, scratch=[(_HALO, RW_COLS)], into={0: (dproj, 1)})
    g['mu_shift'] = d_mu[:, :RW_USED]
    for n, d_ in (('w_decay_up', d_wd), ('w_aaa_up', d_wa), ('w_gate_up', d_wg)):
        r0, nr = _LORA[n]
        g[n] = d_[r0:r0 + nr]
    (dproj,), (g['g_ln_v'], g['b_ln_v'], g['w_spatial'], d_b_tg), (cs_g,) = _rowwise_vjp(
        "gmlp_bwd", _seg_gmlp, [z_gmlp], gmlp_b, [dya], G_CHUNK, [0], [0, 1, 2, 3], t_dtypes=[_MXU_DTYPE], colsum=[0],
        into={0: (dproj, 4)})
    g['b_spatial'] = d_b_tg[:, :8].T
    g['b_in'] = _from_padded(jnp.concatenate([cs_gates, cs_rw, cs_g], axis=1), 1)
    small_names = [n for n in _WEIGHTS if n not in _BIG and n not in ('w_ada', 'b_ada')]
    packed_g, g_places = _pack_rows([g[n] for n in small_names], row_mult=256)
    gw_in_t, small_all = _mm("g_w_in", dproj, h, "tn", rider=(packed_g, _gather_def, (0.0, 0.6, 1.0)))
    gw['w_in'] = _from_padded(gw_in_t, 0)
    send_late = send_rows(big_names[-1:])
    sibling_late = _exchange("pair_exchange_late", send_late, _sibling_def)
    partials_late = _chip_partials("chip_partials_late", send_late, sibling_late, 128, _COMM_DTYPE)
    dh, landed_late = _mm("d_h", dproj, w_in_t, "nn", rider=(partials_late, _chips_def, (0.0, 1.0)))

    def mod1_bwd(tv, bv):
        x_t, dh_t, dxa_t = tv
        (sc,) = bv
        return [dxa_t + dh_t * (1.0 + sc)], [jnp.sum(dh_t * x_t, axis=0, keepdims=True), jnp.sum(dh_t, axis=0, keepdims=True)]

    (grad_x,), (d_sc1, d_sh1) = _rowwise("modulate1_bwd", mod1_bwd, [x, dh, dx_a], [sc1], [(D_MODEL, F32)], [(1, D_MODEL)] * 2, tile)

    dmod = jnp.concatenate([d_sh1, d_sc1, d_gt1, d_sh2, d_sc2, d_gt2], axis=1).reshape(6 * D_MODEL // 128, 128)
    dmod_all = _all_gather("gather_dmod", dmod)
    g['b_ada'] = _sum_leading("sum_dmod", dmod_all).reshape(1, 6 * D_MODEL)
    dmod_mine = lax.dynamic_slice(dmod_all.reshape(_N_DEV, 6 * D_MODEL), (0, me * 768), (_N_DEV, 768))
    g_w_ada = _mm("g_w_ada", c_act_all, _pad_rows(dmod_mine, 16), "tn")

    small_sum = _unpack_rows(_sum_leading("sum_small", small_all), g_places)
    for n, t in zip(small_names, small_sum, strict=True):
        g[n] = t
    for n in _LORA:
        g[n] = lax.dynamic_slice(g[n], (0, me * R_HEAD), (g[n].shape[0], R_HEAD))
    g['w_ada'] = g_w_ada

    summed = jnp.concatenate([_sum_leading("sum_early", landed_early), _sum_leading("sum_late", landed_late)], axis=0)
    g.update(_unpack_big(summed, p))

    delta, new_m, new_v = {}, {}, {}
    own_call = ['w_ada'] + big_names
    for n in own_call:
        delta[n], new_m[n], new_v[n] = _adamw("adamw_" + n, p[n], g[n], m[n], v[n])
    rest = [n for n in _WEIGHTS if n not in own_call]
    outs = _adamw_many("adamw_rest", *[[d[n].reshape(p[n].shape) for n in rest] for d in (p, g, m, v)])
    for d, o in zip((delta, new_m, new_v), outs, strict=True):
        d.update(zip(rest, o, strict=True))
    return loss, grad_x, g, delta, new_m, new_v


def kernel(x, c, w_ada, b_ada, w_in, b_in, g_ln_v, b_ln_v, w_spatial, b_spatial, mu_shift, w0, w_decay_up, a0, w_aaa_up, w_gate_up, k_k, k_a, r_k, gn_gain, gn_bias, w_branch_a, w_branch_b, w_out, b_out, ln1_g, ln1_b, w_ff1, b_ff1, w_ff2, b_ff2, ln2_g, ln2_b, loss_target, m_w_ada, m_b_ada, m_w_in, m_b_in, m_g_ln_v, m_b_ln_v, m_w_spatial, m_b_spatial, m_mu_shift, m_w0, m_w_decay_up, m_a0, m_w_aaa_up, m_w_gate_up, m_k_k, m_k_a, m_r_k, m_gn_gain, m_gn_bias, m_w_branch_a, m_w_branch_b, m_w_out, m_b_out, m_ln1_g, m_ln1_b, m_w_ff1, m_b_ff1, m_w_ff2, m_b_ff2, m_ln2_g, m_ln2_b, v_w_ada, v_b_ada, v_w_in, v_b_in, v_g_ln_v, v_b_ln_v, v_w_spatial, v_b_spatial, v_mu_shift, v_w0, v_w_decay_up, v_a0, v_w_aaa_up, v_w_gate_up, v_k_k, v_k_a, v_r_k, v_gn_gain, v_gn_bias, v_w_branch_a, v_w_branch_b, v_w_out, v_b_out, v_ln1_g, v_ln1_b, v_w_ff1, v_b_ff1, v_w_ff2, v_b_ff2, v_ln2_g, v_ln2_b):
    given = dict(locals())
    shapes = {n: given[n].shape for n in _WEIGHTS}
    def two_d(a_):
        a_ = a_[0]
        return a_.reshape(1, -1) if a_.ndim == 1 else a_
    p = {n: two_d(given[n]) for n in _WEIGHTS}
    m = {n: two_d(given["m_" + n]) for n in _WEIGHTS}
    v = {n: two_d(given["v_" + n]) for n in _WEIGHTS}
    loss, grad_x, g, delta, new_m, new_v = _step(p, m, v, x[0], c, loss_target[0])
    outs = [loss, grad_x[None]]
    for d in (g, delta, new_m, new_v):
        outs += [d[n].reshape(shapes[n]) for n in _WEIGHTS]
    return tuple(outs)
```

```python
import functools

import jax
import jax.numpy as jnp
from jax import lax
from jax.experimental import pallas as pl
from jax.experimental.pallas import tpu as pltpu

F32 = jnp.float32
_MXU_DTYPE = jnp.bfloat16
_HI = lax.Precision.HIGHEST
_VMEM_LIMIT = 48 * 1024 * 1024
_MESH_ID = pl.DeviceIdType.MESH
_N_DEV = 8

D_MODEL = 1024
G_WIDTH = 512
G_CHUNK = 128
R_WIDTH = 512
R_HEADS = 8
R_HEAD = 64
LORA_W, LORA_A, LORA_G = 32, 32, 96
D_FF = 4096
ALPHA = 2.0 ** 0.25
LN_EPS = 1e-5
GN_EPS = 64e-5
SCAN_CHUNK = 64
SCAN_PER_STEP = 4
ADAM_LR, ADAM_B1, ADAM_B2, ADAM_EPS, ADAM_WD, ADAM_STEP = 0.001, 0.9, 0.999, 1e-08, 0.01, 10

P_COLS = 5120
RW_COLS = 2048
RW_USED = 3 * R_WIDTH + LORA_W + LORA_A + LORA_G
LORA_PAD = 256
IN_COLS = 2 * G_WIDTH + RW_USED + 2 * D_MODEL


def _cparams(sem=None, **kw):
    if sem is not None:
        kw["dimension_semantics"] = sem
    return pltpu.CompilerParams(vmem_limit_bytes=_VMEM_LIMIT, **kw)


def _dot(a, b, dims=(((1,), (0,)), ((), ())), hi=False):
    if hi:
        return lax.dot_general(a.astype(F32), b.astype(F32), dims, precision=_HI, preferred_element_type=F32)
    return lax.dot_general(a.astype(_MXU_DTYPE), b.astype(_MXU_DTYPE), dims, preferred_element_type=F32)


_NN = (((1,), (0,)), ((), ()))
_NT = (((1,), (1,)), ((), ()))
_TN = (((0,), (0,)), ((), ()))


def _pick(n, pref):
    for t in pref:
        if n % t == 0:
            return t
    return n


def _mm(name, a, b, mode, bias=None, out_dtype=F32, epi=None, epi_dtype=None, raw=True, beside=None, rider=None, tm=None, tn=None, tk=None):
    if mode == "nn":
        (M, K), (_, N) = a.shape, b.shape
    elif mode == "nt":
        (M, K), (N, _) = a.shape, b.shape
    else:
        (K, M), (_, N) = a.shape, b.shape
    tm = tm or _pick(M, (2048, 1280, 1024, 512, 256, 128, 64, 32, 16, 8))
    tn = tn or _pick(N, (1024, 512, 640, 384, 256, 128))
    tk = tk or _pick(K, (1024, 512, 256, 128))
    nk = K // tk
    dims = {"nn": _NN, "nt": _NT, "tn": _TN}[mode]
    a_spec = pl.BlockSpec((tk, tm), lambda i, j, k: (k, i)) if mode == "tn" else pl.BlockSpec((tm, tk), lambda i, j, k: (i, k))
    b_spec = pl.BlockSpec((tn, tk), lambda i, j, k: (j, k)) if mode == "nt" else pl.BlockSpec((tk, tn), lambda i, j, k: (k, j))
    o_spec = pl.BlockSpec((tm, tn), lambda i, j, k: (i, j))
    has_bias, has_beside = bias is not None, beside is not None
    two = epi is not None and not has_beside and raw
    only_epi = epi is not None and not has_beside and not raw
    grid = (M // tm, N // tn, nk)
    ride_shape, ride_sems, _ = rider[1](rider[0]) if rider else (None, [], None)

    def body(*refs):
        refs = list(refs)
        n_in = 2 + has_bias + has_beside
        if rider:
            sem_refs = [refs.pop() for _ in ride_sems][::-1]
            ride_out = refs.pop(n_in + 1 + 1 + two)
            ride_in = refs.pop(n_in)
            step = (pl.program_id(0) * grid[1] + pl.program_id(1)) * grid[2] + pl.program_id(2)
            _ride(rider, step, grid[0] * grid[1] * grid[2], ride_in, ride_out, sem_refs)
        a_ref, b_ref = refs[0], refs[1]
        bias_ref = refs[2] if has_bias else None
        beside_ref = refs[n_in - 1] if has_beside else None
        outs = refs[n_in:]
        o_ref, acc_ref = outs[0], outs[-1]
        k = pl.program_id(2)

        @pl.when(k == 0)
        def _():
            acc_ref[...] = jnp.zeros_like(acc_ref)

        acc_ref[...] += _dot(a_ref[...], b_ref[...], dims)

        @pl.when(k == nk - 1)
        def _():
            res = acc_ref[...]
            if has_bias:
                res = res + bias_ref[...]
            if has_beside:
                o_ref[...] = epi(res, beside_ref[...]).astype(o_ref.dtype)
            elif only_epi:
                o_ref[...] = epi(res).astype(o_ref.dtype)
            else:
                o_ref[...] = res.astype(o_ref.dtype)
                if two:
                    outs[1][...] = epi(res).astype(outs[1].dtype)

    in_specs = [a_spec, b_spec]
    args = [a, b]
    if has_bias:
        in_specs.append(pl.BlockSpec((1, tn), lambda i, j, k: (0, j)))
        args.append(bias)
    if has_beside:
        in_specs.append(o_spec)
        args.append(beside)
    out_shape = [jax.ShapeDtypeStruct((M, N), epi_dtype if (has_beside or only_epi) else out_dtype)]
    out_specs = [o_spec]
    if two:
        out_shape.append(jax.ShapeDtypeStruct((M, N), epi_dtype))
        out_specs.append(o_spec)
    if rider:
        in_specs.append(_HBM)
        args.append(rider[0])
        out_shape.append(ride_shape)
        out_specs.append(_HBM)
    res = pl.pallas_call(
        body, name=name, grid=grid, in_specs=in_specs, out_specs=out_specs, out_shape=out_shape,
        scratch_shapes=[pltpu.VMEM((tm, tn), F32)] + list(ride_sems),
        compiler_params=_cparams(("arbitrary",) * 3 if rider else ("parallel", "parallel", "arbitrary")),
    )(*args)
    return res if (two or rider) else res[0]


_HALO = 8


def _rowwise(name, fn, tiled, bcast, tiled_out, red_out, tile, reverse=False, scratch=(), rider=None):
    tiled = [t if isinstance(t, tuple) else (t, t.shape[1], 0) for t in tiled]
    T = next(t[0] for t in tiled if not isinstance(t[0], str)).shape[0]
    n = T // tile
    nt, nb, nto, nsc, nro = len(tiled), len(bcast), len(tiled_out), len(scratch), len(red_out)
    ride_shape, ride_sems, _ = rider[1](rider[0]) if rider else (None, [], None)
    nr = 1 if rider else 0
    into = [(j, t[2], t[3]) for j, t in enumerate(tiled_out) if len(t) == 4]
    na = len(into)

    def row_block(i):
        return n - 1 - i if reverse else i

    def body(*refs):
        i2 = nt + nb
        o0 = i2 + nr + na
        o1, o2 = o0 + nto, o0 + nto + nro
        s0 = o2 + nr
        if rider:
            _ride(rider, pl.program_id(0), n, refs[i2], refs[o2], refs[s0 + nsc:])
        t_refs, b_refs = refs[:nt], refs[nt:i2]
        to_refs, ro_refs = refs[o0:o1], refs[o1:o2]
        extra = (list(refs[s0:s0 + nsc]),) if nsc else ()
        touts, routs = fn([r[...] for r in t_refs], [r[...] for r in b_refs], *extra)
        for r, v in zip(to_refs, touts, strict=True):
            r[...] = v.astype(r.dtype)
        if ro_refs:
            i = pl.program_id(0)

            @pl.when(i == 0)
            def _():
                for r, v in zip(ro_refs, routs, strict=True):
                    r[...] = v.astype(F32)

            @pl.when(i > 0)
            def _():
                for r, v in zip(ro_refs, routs, strict=True):
                    r[...] += v.astype(F32)

    def whole(shape):
        nd = len(shape)
        return pl.BlockSpec(tuple(shape), lambda i: (0,) * nd)

    per_tile = tile // _HALO
    in_specs, arrays = [], []
    for t in tiled:
        if isinstance(t[0], str):
            _, arr, w, cb = t
            in_specs.append(pl.BlockSpec((_HALO, w), functools.partial(lambda i, cb: (jnp.maximum(row_block(i) * per_tile - 1, 0), cb), cb=cb)))
        else:
            arr, w, cb = t
            in_specs.append(pl.BlockSpec((tile, w), functools.partial(lambda i, cb: (row_block(i), cb), cb=cb)))
        arrays.append(arr)
    in_specs += [whole(b.shape) for b in bcast]
    out_specs, out_shape = [], []
    for t in tiled_out:
        cb = t[3] if len(t) == 4 else 0
        out_specs.append(pl.BlockSpec((tile, t[0]), functools.partial(lambda i, cb: (row_block(i), cb), cb=cb)))
        out_shape.append(jax.ShapeDtypeStruct(t[2].shape if len(t) == 4 else (T, t[0]), t[1]))
    out_specs += [whole(s) for s in red_out]
    out_shape += [jax.ShapeDtypeStruct(tuple(s), F32) for s in red_out]
    ride_args = []
    if rider:
        in_specs.append(_HBM)
        out_specs.append(_HBM)
        out_shape.append(ride_shape)
        ride_args = [rider[0]]
    in_specs += [pl.BlockSpec(memory_space=pl.ANY)] * na
    aliases = {nt + nb + nr + k: j for k, (j, _, _) in enumerate(into)}
    res = pl.pallas_call(
        body, name=name, grid=(n,), in_specs=in_specs, out_specs=out_specs, out_shape=out_shape,
        scratch_shapes=[pltpu.VMEM(tuple(s), F32) for s in scratch] + list(ride_sems),
        input_output_aliases=aliases, compiler_params=_cparams(("arbitrary",)),
    )(*arrays, *bcast, *ride_args, *[buf for (_, buf, _) in into])
    if rider:
        return list(res[:nto]), list(res[nto:nto + nro]), res[-1]
    return list(res[:nto]), list(res[nto:])


def _rowwise_vjp(name, f, tiled, bcast, cts, tile, wrt_t, wrt_b, t_dtypes=None, colsum=(), prep=None, finish=None,
                 out_widths=None, reverse=False, scratch=(), rider=None, into=None):
    tiled = [t if isinstance(t, tuple) else (t, t.shape[1], 0) for t in tiled]
    npr = len(tiled)
    t_dtypes = t_dtypes or [F32] * len(wrt_t)
    groups = [c if isinstance(c, list) else [c] for c in cts]
    cts = [a_ for grp in groups for a_ in grp]

    def fn(tv, bv, sc=None):
        prim, flat_ct = tv[:npr], list(tv[npr:])
        if prep is not None:
            prim = prep(prim)
        ct = []
        for grp in groups:
            parts = [flat_ct.pop(0).astype(F32) for _ in grp]
            ct.append(functools.reduce(lambda p_, q_: p_ + q_, parts))

        def g(dt_vals, db_vals):
            full_t, full_b = list(prim), list(bv)
            for i, v in zip(wrt_t, dt_vals, strict=True):
                full_t[i] = v
            for j, v in zip(wrt_b, db_vals, strict=True):
                full_b[j] = v
            return f(full_t, full_b)

        outs, pull = jax.vjp(g, [prim[i].astype(F32) for i in wrt_t], [bv[j] for j in wrt_b])
        dts, dbs = pull([c.astype(o.dtype) for c, o in zip(ct, outs, strict=True)])
        if finish is not None:
            dts = finish(dts, sc)
        sums = [jnp.sum(dts[i].astype(F32), axis=0, keepdims=True) for i in colsum]
        return dts, list(dbs) + sums

    widths = out_widths or [tiled[i][1] for i in wrt_t]
    tiled_out = [(w, dt) + tuple((into or {}).get(j, ())) for j, (w, dt) in enumerate(zip(widths, t_dtypes, strict=True))]
    red_out = [bcast[j].shape for j in wrt_b] + [(1, widths[i]) for i in colsum]
    res = _rowwise(name, fn, tiled + list(cts), bcast, tiled_out, red_out, tile, reverse=reverse, scratch=scratch, rider=rider)
    dts, reds = res[0], res[1]
    nb = len(wrt_b)
    return (dts, reds[:nb], reds[nb:]) + tuple(res[2:])


def _layer_norm(x, g, b, eps):
    mu = jnp.mean(x, axis=-1, keepdims=True)
    xc = x - mu
    var = jnp.mean(xc * xc, axis=-1, keepdims=True)
    return xc * lax.rsqrt(var + eps) * g + b


def _gelu_tanh(x):
    return 0.5 * x * (1.0 + jnp.tanh(0.7978845608028654 * (x + 0.044715 * (x * x * x))))


def _sigmoid(x):
    return 1.0 / (1.0 + jnp.exp(-x))


def _seg_modulate(tv, bv):
    (x,), (sc, sh) = tv, bv
    return [x * (1.0 + sc) + sh]


def _seg_gmlp(tv, bv):
    (z,), (g_ln, b_ln, ws, b_tg, expand) = tv, bv
    bias_full = _dot(b_tg, expand, hi=True)
    zz = _gelu_tanh(z)
    u, v = zz[:, :G_WIDTH], zz[:, G_WIDTH:]
    v = _layer_norm(v, g_ln, b_ln, LN_EPS)
    row = lax.broadcasted_iota(jnp.int32, (G_CHUNK, G_CHUNK), 0)
    col = lax.broadcasted_iota(jnp.int32, (G_CHUNK, G_CHUNK), 1)
    causal = col <= row
    first_group = lax.broadcasted_iota(jnp.int32, (G_CHUNK, 128), 1) < 64
    parts = []
    for p in range(4):
        vp = v[:, 128 * p:128 * (p + 1)]
        s_even = _dot(jnp.where(causal, ws[2 * p], 0.0), vp)
        s_odd = _dot(jnp.where(causal, ws[2 * p + 1], 0.0), vp)
        parts.append(jnp.where(first_group, s_even, s_odd))
    s = jnp.concatenate(parts, axis=1) + bias_full
    return [u * s]


def _split2(x):
    hi = x.astype(_MXU_DTYPE)
    return hi, (x - hi.astype(F32)).astype(_MXU_DTYPE)


@jax.custom_vjp
def _group_sum(x, ones_blocks):
    hi, lo = _split2(x)
    return _dot(hi, ones_blocks) + _dot(lo, ones_blocks)


def _group_sum_fwd(x, ones_blocks):
    return _group_sum(x, ones_blocks), ones_blocks


def _group_sum_bwd(ones_blocks, ct):
    return _group_sum(ct, ones_blocks), jnp.zeros_like(ones_blocks)


_group_sum.defvjp(_group_sum_fwd, _group_sum_bwd)


def _shift_down(z, halo, is_first):
    _, W = z.shape
    rolled = pltpu.roll(z, 1, 0)
    before = jnp.where(is_first, 0.0, pltpu.roll(halo, 1, 0))
    top_row = lax.broadcasted_iota(jnp.int32, (_HALO, W), 0) == 0
    return jnp.concatenate([jnp.where(top_row, before, rolled[:_HALO]), rolled[_HALO:]], axis=0)


def _shift_up(d, after):
    tile, W = d.shape
    rolled = pltpu.roll(d, tile - 1, 0)
    last_row = lax.broadcasted_iota(jnp.int32, (_HALO, W), 0) == _HALO - 1
    bottom = jnp.where(last_row, pltpu.roll(after, _HALO - 1, 0), rolled[tile - _HALO:])
    return jnp.concatenate([rolled[:tile - _HALO], bottom], axis=0)


def _seg_rwkv_pre(tv, bv):
    (z, prev), (mu, w0, wd, a0, wa, wg, k_k, k_a, gsum) = tv, bv
    zs = z + (prev - z) * mu
    r, k, v = zs[:, 0:512], zs[:, 512:1024], zs[:, 1024:1536]
    zl = zs[:, 3 * R_WIDTH:3 * R_WIDTH + LORA_PAD]
    x = w0 + _dot(jnp.tanh(zl), wd)
    softplus = jnp.maximum(-x, 0.0) + jnp.log(1.0 + jnp.exp(-jnp.abs(x)))
    lw = -jnp.exp(-softplus - 0.5)
    a = _sigmoid(a0 + _dot(zl, wa))
    g = _dot(_sigmoid(zl), wg)
    kk = k * k_k
    nrm = jnp.sqrt(_group_sum(kk * kk, gsum))
    kk = kk / jnp.maximum(nrm, 1e-12)
    k2 = k * (1.0 + (a - 1.0) * k_a)
    return [r, lw, k2, v, -kk, kk * a, g]


def _seg_rwkv_post(tv, bv):
    (y, r, k2, v, g), (r_k, gain, bias, gsum) = tv, bv
    mu = _group_sum(y, gsum) * (1.0 / R_HEAD)
    yc = y - mu
    var = _group_sum(yc * yc, gsum) * (1.0 / R_HEAD)
    yn = yc * lax.rsqrt(var + GN_EPS) * gain + bias
    bonus = _group_sum(r * k2 * r_k, gsum) * v
    return [(yn + bonus) * g]


def _seg_merge(tv, bv):
    (ga, gb, pa, pb), () = tv, bv
    return [_sigmoid(ga) * pa + _sigmoid(gb) * pb]


def _seg_mid(tv, bv):
    (x, mix), (gt1, g1, b1, sc2, sh2) = tv, bv
    h1 = _layer_norm(ALPHA * x + gt1 * mix, g1, b1, LN_EPS)
    return [h1, h1 * (1.0 + sc2) + sh2]


def _seg_relu2(tv, bv):
    (f1,), () = tv, bv
    return [jnp.square(jnp.maximum(f1, 0.0))]


def _seg_loss(tv, bv):
    (h1, ff, target), (gt2, g2, b2) = tv, bv
    out = _layer_norm(ALPHA * h1 + gt2 * ff, g2, b2, LN_EPS)
    err = jnp.square(out - target)
    return 0.5 * jnp.sum(jnp.mean(err, axis=-1))


_BNN = (((2,), (1,)), ((0,), (0,)))
_BNT = (((2,), (2,)), ((0,), (0,)))
_BTN = (((1,), (1,)), ((0,), (0,)))


def _tri_dot(x, dims):
    L = x.shape[0]
    tri = (lax.broadcasted_iota(jnp.int32, (L, L), 1) <= lax.broadcasted_iota(jnp.int32, (L, L), 0)).astype(F32)
    hi, lo = _split2(x)
    return _dot(tri, hi, dims) + _dot(tri, lo, dims)


@jax.custom_vjp
def _running_sum(x):
    return _tri_dot(x, _NN)


_running_sum.defvjp(lambda x: (_tri_dot(x, _NN), None), lambda _, ct: (_tri_dot(ct, _TN),))


def _cut_heads(x):
    return jnp.stack([x[:, R_HEAD * h:R_HEAD * (h + 1)] for h in range(R_HEADS)])


def _join_heads(x):
    return jnp.concatenate([x[h] for h in range(R_HEADS)], axis=1)


@jax.custom_vjp
def _split_heads(x):
    return _cut_heads(x)


@jax.custom_vjp
def _merge_heads(x):
    return _join_heads(x)


_split_heads.defvjp(lambda x: (_cut_heads(x), None), lambda _, ct: (_join_heads(ct),))
_merge_heads.defvjp(lambda x: (_join_heads(x), None), lambda _, ct: (_cut_heads(ct),))


def _inverse_pullback(inv, ct):
    return _dot(_dot(inv, ct, _BTN), inv, _BNT)


@jax.custom_vjp
def _unit_lower_inverse(n_mat):
    H, L, _ = n_mat.shape
    eye = lax.broadcasted_iota(jnp.int32, (H, L, L), 1) == lax.broadcasted_iota(jnp.int32, (H, L, L), 2)
    inv = jnp.where(eye, 1.0, 0.0) + n_mat
    pw = n_mat
    n = 2
    while n < L:
        pw = _dot(pw, pw, _BNN)
        inv = inv + _dot(inv, pw, _BNN)
        n *= 2
    return inv


def _unit_lower_inverse_fwd(n_mat):
    inv = _unit_lower_inverse(n_mat)
    return inv, inv


_unit_lower_inverse.defvjp(_unit_lower_inverse_fwd, lambda inv, ct: (_inverse_pullback(inv, ct),))


@jax.custom_vjp
def _known_inverse(n_mat, inv):
    return inv


_known_inverse.defvjp(lambda n_mat, inv: (inv, inv), lambda inv, ct: (_inverse_pullback(inv, ct), jnp.zeros_like(inv)))


def _scan_chunk(r, lw, k, v, a, b, s0, inv=None, with_inverse=False):
    L, H = r.shape[0], R_HEADS
    row = lax.broadcasted_iota(jnp.int32, (H, L, L), 1)
    col = lax.broadcasted_iota(jnp.int32, (H, L, L), 2)
    incl, strict = col <= row, col < row
    cs = _running_sum(lw)
    cs_end = cs[L - 1:L, :]
    p, p_inv, to_end = jnp.exp(cs), jnp.exp(-cs), jnp.exp(cs_end - cs)
    at, bt, kt, rt = [_split_heads(t) for t in (a * jnp.exp(cs - lw), b * p_inv, k * p_inv, r * p)]
    b_end, k_end, v = [_split_heads(t) for t in (b * to_end, k * to_end, v)]
    a_ab = jnp.where(strict, _dot(at, bt, _BNT), 0.0)
    a_ak = jnp.where(strict, _dot(at, kt, _BNT), 0.0)
    a_rb = jnp.where(incl, _dot(rt, bt, _BNT), 0.0)
    a_rk = jnp.where(incl, _dot(rt, kt, _BNT), 0.0)
    inv = _unit_lower_inverse(a_ab) if inv is None else _known_inverse(a_ab, inv)
    u = _dot(inv, _dot(at, s0, _BNT) + _dot(a_ak, v, _BNN), _BNN)
    y = _merge_heads(_dot(rt, s0, _BNT) + _dot(a_rb, u, _BNN) + _dot(a_rk, v, _BNN))
    s1 = s0 * _split_heads(jnp.exp(cs_end)) + _dot(u, b_end, _BTN) + _dot(v, k_end, _BTN)
    return (y, s1, inv) if with_inverse else (y, s1)


def _scan_fwd(r, lw, k, v, a, b, rider):
    T = r.shape[0]
    H, N, L, P = R_HEADS, R_HEAD, SCAN_CHUNK, SCAN_PER_STEP
    nc = T // L
    steps = nc // P
    ride_shape, ride_sems, _ = rider[1](rider[0])

    def body(r_ref, lw_ref, k_ref, v_ref, a_ref, b_ref, ride_in, y_ref, st_ref, inv_ref, ride_out, s_ref, *sem_refs):
        _ride(rider, pl.program_id(0), steps, ride_in, ride_out, sem_refs)

        @pl.when(pl.program_id(0) == 0)
        def _():
            s_ref[...] = jnp.zeros_like(s_ref)

        s0 = s_ref[...]
        for j in range(P):
            rows = pl.ds(j * L, L)
            st_ref[j] = s0
            y, s0, inv = _scan_chunk(*[t[rows, :] for t in (r_ref, lw_ref, k_ref, v_ref, a_ref, b_ref)], s0, with_inverse=True)
            y_ref[rows, :] = y
            inv_ref[j] = inv
        s_ref[...] = s0

    blk = pl.BlockSpec((P * L, R_WIDTH), lambda c: (c, 0))
    per_chunk = pl.BlockSpec((P, H, N, N), lambda c: (c, 0, 0, 0))
    return pl.pallas_call(
        body, name="scan_fwd", grid=(steps,), in_specs=[blk] * 6 + [_HBM], out_specs=[blk, per_chunk, per_chunk, _HBM],
        out_shape=[jax.ShapeDtypeStruct((T, R_WIDTH), F32)] + [jax.ShapeDtypeStruct((nc, H, N, N), F32)] * 2 + [ride_shape],
        scratch_shapes=[pltpu.VMEM((H, N, N), F32)] + list(ride_sems),
        compiler_params=_cparams(("arbitrary",)),
    )(r, lw, k, v, a, b, rider[0])


def _scan_bwd(r, lw, k, v, a, b, states, inverses, dy, rider):
    T = r.shape[0]
    H, N, L, P = R_HEADS, R_HEAD, SCAN_CHUNK, SCAN_PER_STEP
    nc = T // L
    steps = nc // P
    ride_shape, ride_sems, _ = rider[1](rider[0])

    def body(r_ref, lw_ref, k_ref, v_ref, a_ref, b_ref, st_ref, inv_ref, dy_ref, ride_in,
             dr_ref, dlw_ref, dk_ref, dv_ref, da_ref, db_ref, ride_out, ds_ref, *sem_refs):
        _ride(rider, pl.program_id(0), steps, ride_in, ride_out, sem_refs)

        @pl.when(pl.program_id(0) == 0)
        def _():
            ds_ref[...] = jnp.zeros_like(ds_ref)

        ds = ds_ref[...]
        for j in reversed(range(P)):
            rows = pl.ds(j * L, L)
            args = [t[rows, :] for t in (r_ref, lw_ref, k_ref, v_ref, a_ref, b_ref)] + [st_ref[j]]
            inv = inv_ref[j]
            _, pull = jax.vjp(lambda *xs, inv=inv: _scan_chunk(*xs, inv=inv), *args)
            grads = pull((dy_ref[rows, :], ds))
            for o_ref, g_ in zip((dr_ref, dlw_ref, dk_ref, dv_ref, da_ref, db_ref), grads[:6], strict=True):
                o_ref[rows, :] = g_
            ds = grads[6]
        ds_ref[...] = ds

    blk = pl.BlockSpec((P * L, R_WIDTH), lambda c: (steps - 1 - c, 0))
    per_chunk = pl.BlockSpec((P, H, N, N), lambda c: (steps - 1 - c, 0, 0, 0))
    return pl.pallas_call(
        body, name="scan_bwd", grid=(steps,), in_specs=[blk] * 6 + [per_chunk, per_chunk, blk, _HBM], out_specs=[blk] * 6 + [_HBM],
        out_shape=[jax.ShapeDtypeStruct((T, R_WIDTH), F32)] * 6 + [ride_shape],
        scratch_shapes=[pltpu.VMEM((H, N, N), F32)] + list(ride_sems),
        compiler_params=_cparams(("arbitrary",)),
    )(r, lw, k, v, a, b, states, inverses, dy, rider[0])


def _place():
    x, y, c = lax.axis_index("x"), lax.axis_index("y"), lax.axis_index("c")
    return x, y, c


def _gather_def(block):
    R, C = block.shape

    def phases(x_ref, out_ref, send_sems, recv_sems, local_sem):
        x, y, c = _place()
        me, sibling = (x, y, c), (x, y, 1 - c)
        chips = [(1 - x, y), (x, 1 - y), (1 - x, 1 - y)]

        def slot(px, py, pc):
            return out_ref.at[4 * px + 2 * py + pc]

        def copy(k, blk, to, src=None):
            return pltpu.make_async_remote_copy(
                src_ref=slot(*blk) if src is None else src, dst_ref=slot(*blk),
                send_sem=send_sems.at[k], recv_sem=recv_sems.at[k], device_id=to, device_id_type=_MESH_ID)

        mine = pltpu.make_async_copy(x_ref, slot(*me), local_sem)
        first = [copy(0, me, sibling, src=x_ref)]
        first += [copy(1 + j, me, (*chip, c), src=x_ref) for j, chip in enumerate(chips)]
        passed = [copy(4 + j, (*chip, c), sibling) for j, chip in enumerate(chips)]

        def begin():
            mine.start()
            for cp in first:
                cp.start()

        def forward():
            for j, chip in enumerate(chips):
                copy(1 + j, (*chip, c), me).wait_recv()
                passed[j].start()

        def finish():
            copy(0, sibling, me).wait_recv()
            for j, chip in enumerate(chips):
                copy(4 + j, (*chip, 1 - c), me).wait_recv()
            for cp in first + passed:
                cp.wait_send()
            mine.wait()

        return [begin, forward, finish]

    sems = [pltpu.SemaphoreType.DMA((7,)), pltpu.SemaphoreType.DMA((7,)), pltpu.SemaphoreType.DMA]
    return jax.ShapeDtypeStruct((_N_DEV, R, C), block.dtype), sems, phases


def _sibling_def(blocks):
    _, R, C = blocks.shape

    def phases(x_ref, out_ref, send_sems, recv_sems):
        x, y, c = _place()
        copies = [pltpu.make_async_remote_copy(
            src_ref=x_ref.at[2 * q + (1 - c)], dst_ref=out_ref.at[q], send_sem=send_sems.at[q], recv_sem=recv_sems.at[q],
            device_id=(x, y, 1 - c), device_id_type=_MESH_ID) for q in range(4)]

        def begin():
            for cp in copies:
                cp.start()

        def finish():
            for cp in copies:
                cp.wait()

        return [begin, finish]

    return jax.ShapeDtypeStruct((4, R, C), blocks.dtype), [pltpu.SemaphoreType.DMA((4,)), pltpu.SemaphoreType.DMA((4,))], phases


def _chips_def(partials):
    def phases(x_ref, out_ref, send_sems, recv_sems, local_sem):
        x, y, c = _place()
        my_chip = 2 * x + y
        mine = pltpu.make_async_copy(x_ref.at[my_chip], out_ref.at[my_chip], local_sem)
        copies = []
        for rel in range(1, 4):
            px, py = (1 - x if (rel >> 1) & 1 else x), (1 - y if rel & 1 else y)
            copies.append(pltpu.make_async_remote_copy(
                src_ref=x_ref.at[2 * px + py], dst_ref=out_ref.at[my_chip],
                send_sem=send_sems.at[rel - 1], recv_sem=recv_sems.at[rel - 1],
                device_id=(px, py, c), device_id_type=_MESH_ID))

        def begin():
            mine.start()
            for cp in copies:
                cp.start()

        def finish():
            for cp in copies:
                cp.wait()
            mine.wait()

        return [begin, finish]

    sems = [pltpu.SemaphoreType.DMA((3,)), pltpu.SemaphoreType.DMA((3,)), pltpu.SemaphoreType.DMA]
    return jax.ShapeDtypeStruct(partials.shape, partials.dtype), sems, phases


_HBM = pl.BlockSpec(memory_space=pltpu.HBM)


def _exchange(name, array, definition):
    out_shape, sems, phases = definition(array)

    def body(x_ref, out_ref, *sem_refs):
        for phase in phases(x_ref, out_ref, *sem_refs):
            phase()

    return pl.pallas_call(body, name=name, in_specs=[_HBM], out_specs=_HBM, out_shape=out_shape, scratch_shapes=sems)(array)


def _ride(rider, step, nsteps, x_ref, out_ref, sem_refs):
    array, definition, fractions = rider
    for phase, frac in zip(definition(array)[2](x_ref, out_ref, *sem_refs), fractions, strict=True):
        pl.when(step == min(int(frac * nsteps), nsteps - 1))(phase)


def _all_gather(name, block):
    return _exchange(name, block, _gather_def)


def _chip_partials(name, blocks, from_sibling, tile, out_dtype):
    _, R, C = blocks.shape

    def body(x_ref, s_ref, o_ref):
        c = lax.axis_index("c")
        for q in range(4):
            o_ref[q] = (x_ref[2 * q + c] + s_ref[q]).astype(o_ref.dtype)

    return pl.pallas_call(
        body, name=name, grid=(R // tile,),
        in_specs=[pl.BlockSpec((_N_DEV, tile, C), lambda i: (0, i, 0)), pl.BlockSpec((4, tile, C), lambda i: (0, i, 0))],
        out_specs=pl.BlockSpec((4, tile, C), lambda i: (0, i, 0)), out_shape=jax.ShapeDtypeStruct((4, R, C), out_dtype),
        compiler_params=_cparams(("parallel",)),
    )(blocks, from_sibling)


def _sum_leading(name, x, tile=None):
    n, R, C = x.shape
    tile = tile or _pick(R, (512, 256, 128, 64, 32, 16, 8))

    def body(x_ref, o_ref):
        acc = x_ref[0].astype(F32)
        for k in range(1, n):
            acc = acc + x_ref[k].astype(F32)
        o_ref[...] = acc

    return pl.pallas_call(
        body, name=name, grid=(R // tile,), in_specs=[pl.BlockSpec((n, tile, C), lambda i: (0, i, 0))],
        out_specs=pl.BlockSpec((tile, C), lambda i: (i, 0)), out_shape=jax.ShapeDtypeStruct((R, C), F32),
        compiler_params=_cparams(("parallel",)),
    )(x)


def _adamw_update(w_, g_, m_, v_):
    m2 = ADAM_B1 * m_ + (1.0 - ADAM_B1) * g_
    v2 = ADAM_B2 * v_ + (1.0 - ADAM_B2) * jnp.square(g_)
    m_hat = m2 / (1.0 - ADAM_B1 ** ADAM_STEP)
    v_hat = v2 / (1.0 - ADAM_B2 ** ADAM_STEP)
    delta = -ADAM_LR * (m_hat / (jnp.sqrt(v_hat) + ADAM_EPS) + ADAM_WD * w_)
    return delta, m2, v2


def _adamw(name, w, g, m, v):
    R, C = w.shape
    tile = _pick(R, (256, 128, 64, 32, 16, 8))
    outs, _ = _rowwise(name, lambda tv, bv: (list(_adamw_update(*tv)), []), [w, g, m, v], [], [(C, F32)] * 3, [], tile)
    return outs


def _adamw_many(name, ws, gs, ms, vs):
    n = len(ws)

    def body(*refs):
        ins, outs = refs[:4 * n], refs[4 * n:]
        for i in range(n):
            res = _adamw_update(ins[i][...], ins[n + i][...], ins[2 * n + i][...], ins[3 * n + i][...])
            for j in range(3):
                outs[j * n + i][...] = res[j]

    out_shape = [jax.ShapeDtypeStruct(w.shape, F32) for w in ws] * 3
    res = pl.pallas_call(body, name=name, out_shape=out_shape, compiler_params=_cparams())(*ws, *gs, *ms, *vs)
    return res[:n], res[n:2 * n], res[2 * n:]


def _pack_rows(arrs, lanes=128, row_mult=8):
    flat, places, off = [], [], 0
    for a_ in arrs:
        n = a_.size
        flat.append(a_.reshape(-1).astype(F32))
        places.append((off, n, a_.shape))
        off += n
    total = -(-off // (lanes * row_mult)) * (lanes * row_mult)
    if total > off:
        flat.append(jnp.zeros((total - off,), F32))
    return jnp.concatenate(flat).reshape(total // lanes, lanes), places


def _unpack_rows(packed, places):
    flat = packed.reshape(-1)
    return [flat[o:o + n].reshape(s) for (o, n, s) in places]


_WEIGHTS = ['w_ada', 'b_ada', 'w_in', 'b_in', 'g_ln_v', 'b_ln_v', 'w_spatial', 'b_spatial', 'mu_shift', 'w0', 'w_decay_up', 'a0',
            'w_aaa_up', 'w_gate_up', 'k_k', 'k_a', 'r_k', 'gn_gain', 'gn_bias', 'w_branch_a', 'w_branch_b', 'w_out', 'b_out',
            'ln1_g', 'ln1_b', 'w_ff1', 'b_ff1', 'w_ff2', 'b_ff2', 'ln2_g', 'ln2_b']
_BIG = {'w_ff1': (0, 512), 'w_ff2': (512, 512), 'w_out': (1024, 128), 'w_branch_a': (1152, 64), 'w_branch_b': (1216, 64), 'w_in': (1280, 640)}
_LATER_ROWS = 1280
_CUT_BY_COLS = ('w_ff1', 'w_in', 'w_branch_a', 'w_branch_b')
IN_SHARD = IN_COLS // _N_DEV
_LORA = {'w_decay_up': (0, LORA_W), 'w_aaa_up': (LORA_W, LORA_A), 'w_gate_up': (LORA_W + LORA_A, LORA_G)}
_COMM_DTYPE = jnp.bfloat16


def _pad_rows(a, rows):
    return jnp.pad(a, ((0, rows - a.shape[0]),) + ((0, 0),) * (a.ndim - 1))


def _pack_big(shards):
    blocks = []
    for n, (_, rows) in _BIG.items():
        a = shards[n].T if n in _CUT_BY_COLS else shards[n]
        blocks.append(_pad_rows(a.reshape(-1, D_MODEL), rows))
    return jnp.concatenate(blocks, axis=0)


def _unpack_big(block, like):
    out = {}
    for n, (r0, _) in _BIG.items():
        rr, cc = like[n].shape
        if n in _CUT_BY_COLS:
            out[n] = block[r0:r0 + rr * cc // D_MODEL].reshape(cc, rr).T
        else:
            out[n] = block[r0:r0 + rr]
    return out


def _to_padded(a, axis):
    g_end = 2 * G_WIDTH
    r_end = g_end + RW_USED
    take = lambda lo, hi: lax.slice_in_dim(a, lo, hi, axis=axis)
    zshape = list(a.shape)
    zshape[axis] = RW_COLS - RW_USED
    return jnp.concatenate([take(r_end, IN_COLS), take(g_end, r_end), jnp.zeros(zshape, a.dtype), take(0, g_end)], axis=axis)


def _from_padded(a, axis):
    take = lambda lo, hi: lax.slice_in_dim(a, lo, hi, axis=axis)
    return jnp.concatenate([take(2 * D_MODEL + RW_COLS, P_COLS), take(2 * D_MODEL, 2 * D_MODEL + RW_USED), take(0, 2 * D_MODEL)], axis=axis)


def _step(p, m, v, x, c, target):
    T = x.shape[0]
    xi, yi, ci = _place()
    me = 4 * xi + 2 * yi + ci
    tile = 256

    lane = jnp.arange(R_WIDTH)
    gsum = (lane[:, None] // R_HEAD == lane[None, :] // R_HEAD).astype(F32)
    expand = (jnp.arange(128)[:, None] == (lane[None, :] // (G_WIDTH // 8))).astype(F32)

    (c_act,), _ = _rowwise("silu_c", lambda tv, bv: ([tv[0] * _sigmoid(tv[0])], []), [c], [], [(D_MODEL, F32)], [], 1)
    small, places = _pack_rows([c_act, p['w_decay_up'], p['w_aaa_up'], p['w_gate_up']])
    small_all = _all_gather("gather_small", small)
    per_dev = [_unpack_rows(small_all[d], places) for d in range(_N_DEV)]
    c_act_all = _pad_rows(jnp.concatenate([pd[0] for pd in per_dev], axis=0), 16)
    lora_full = {n: jnp.concatenate([pd[i + 1] for pd in per_dev], axis=1) for i, n in enumerate(_LORA)}
    lora_pad = {n: jnp.zeros((LORA_PAD, R_WIDTH), F32).at[r0:r0 + nr].set(lora_full[n]) for n, (r0, nr) in _LORA.items()}

    big_names = list(_BIG)
    my_rows = _pack_big(p).astype(_MXU_DTYPE)
    w_in_all = _all_gather("gather_w_in", my_rows[_LATER_ROWS:])
    w_in_t = _to_padded(w_in_all[:, :IN_SHARD].reshape(IN_COLS, D_MODEL), 0)
    b_in_p = _to_padded(p['b_in'], 1)
    mu_p = jnp.concatenate([p['mu_shift'], jnp.zeros((1, RW_COLS - RW_USED), F32)], axis=1)

    b_ada_mine = lax.dynamic_slice(p['b_ada'], (0, me * 768), (1, 768))
    mod_cols = _mm("ada_mod", c_act_all, p['w_ada'], "nn", bias=b_ada_mine)
    mod_all = _all_gather("gather_mod", mod_cols)
    mod = lax.dynamic_index_in_dim(mod_all, me, axis=1, keepdims=False).reshape(1, 6 * D_MODEL)
    sh1, sc1, gt1, sh2, sc2, gt2 = [mod[:, i * D_MODEL:(i + 1) * D_MODEL] for i in range(6)]

    (h,), _ = _rowwise("modulate1", lambda tv, bv: (_seg_modulate(tv, bv), []), [x], [sc1, sh1], [(D_MODEL, _MXU_DTYPE)], [], tile)
    proj = _mm("in_proj", h, w_in_t, "nt", bias=b_in_p)
    ws = p['w_spatial']
    b_tg = jnp.zeros((G_CHUNK, 128), F32).at[:, :8].set(p['b_spatial'].T)
    gmlp_b = [p['g_ln_v'], p['b_ln_v'], ws, b_tg, expand]
    z_gmlp = (proj, 2 * G_WIDTH, 4)
    (ya,), _ = _rowwise("gmlp", lambda tv, bv: (_seg_gmlp(tv, bv), []), [z_gmlp], gmlp_b, [(G_WIDTH, F32)], [], G_CHUNK)
    z_rw = (proj, RW_COLS, 1)
    z_rw_halo = ("halo", proj, RW_COLS, 1)
    pre_b = [mu_p, p['w0'], lora_pad['w_decay_up'], p['a0'], lora_pad['w_aaa_up'], lora_pad['w_gate_up'], p['k_k'], p['k_a'], gsum]

    def pre_fwd(tv, bv):
        z_t, halo_t = tv
        return _seg_rwkv_pre([z_t, _shift_down(z_t, halo_t, pl.program_id(0) == 0)], bv), []

    pre_out, _ = _rowwise("rwkv_pre", pre_fwd, [z_rw, z_rw_halo], pre_b, [(R_WIDTH, F32)] * 7, [], tile)
    r_, lw_, k2_, v_, a_, b_, g_ = pre_out
    scan_in = (r_, lw_, k2_, v_, a_, b_)
    y_, states, inverses, later_all = _scan_fwd(*scan_in, rider=(my_rows[:_LATER_ROWS], _gather_def, (0.0, 0.875, 1.0)))

    def whole(n, rows):
        r0 = _BIG[n][0]
        return later_all[:, r0:r0 + rows].reshape(_N_DEV * rows, D_MODEL)

    w_ff1_t, w_ff2, w_out = whole('w_ff1', 512), whole('w_ff2', 512), whole('w_out', 128)
    w_ba_t = whole('w_branch_a', 64).reshape(D_MODEL, G_WIDTH)
    w_bb_t = whole('w_branch_b', 64).reshape(D_MODEL, R_WIDTH)
    post_b = [p['r_k'].reshape(1, R_WIDTH), p['gn_gain'], p['gn_bias'], gsum]
    (yb,), _ = _rowwise("rwkv_post", lambda tv, bv: (_seg_rwkv_post(tv, bv), []), [y_, r_, k2_, v_, g_], post_b, [(R_WIDTH, F32)], [], tile)
    pa = _mm("branch_a", ya, w_ba_t, "nt", out_dtype=_MXU_DTYPE)
    pb = _mm("branch_b", yb, w_bb_t, "nt", out_dtype=_MXU_DTYPE)
    gates = [(proj, D_MODEL, 0), (proj, D_MODEL, 1)]
    (merged,), _ = _rowwise("merge", lambda tv, bv: (_seg_merge(tv, bv), []), gates + [pa, pb], [], [(D_MODEL, _MXU_DTYPE)], [], tile)
    mix = _mm("out_proj", merged, w_out, "nn", bias=p['b_out'])
    mid_b = [gt1, p['ln1_g'], p['ln1_b'], sc2, sh2]
    (h1, h2in), _ = _rowwise("mid", lambda tv, bv: (_seg_mid(tv, bv), []), [x, mix], mid_b, [(D_MODEL, F32), (D_MODEL, _MXU_DTYPE)], [], tile)
    act = _mm("ff1", h2in, w_ff1_t, "nt", bias=p['b_ff1'], epi=lambda t: _seg_relu2([t], [])[0], epi_dtype=_MXU_DTYPE, raw=False)
    ff = _mm("ff2", act, w_ff2, "nn", bias=p['b_ff2'])

    def loss_fn(tv, bv):
        h1_t, ff_t, tgt = tv
        val, grads = jax.value_and_grad(lambda a0_, a1_, b0_, b1_, b2_: _seg_loss([a0_, a1_, tgt], [b0_, b1_, b2_]), argnums=(0, 1, 2, 3, 4))(h1_t, ff_t, *bv)
        return [grads[0], grads[1]], [grads[2], grads[3], grads[4], jnp.sum(grads[1], axis=0, keepdims=True), jnp.full((1, 128), val, F32)]

    (dh1_a, dff), (d_gt2, d_ln2_g, d_ln2_b, d_b_ff2, loss_row) = _rowwise(
        "loss", loss_fn, [h1, ff, target], [gt2, p['ln2_g'], p['ln2_b']], [(D_MODEL, F32), (D_MODEL, _MXU_DTYPE)], [(1, D_MODEL)] * 4 + [(1, 128)], tile)
    loss = lax.psum(loss_row[0, 0], ("x", "y", "c"))

    g = {}
    g['ln2_g'], g['ln2_b'], g['b_ff2'] = d_ln2_g, d_ln2_b, d_b_ff2
    gw = {}
    gw['w_ff2'] = _mm("g_w_ff2", act, dff, "tn")
    df1 = _mm("d_act", dff, w_ff2, "nt", beside=act, epi=lambda d_, a_: d_ * (2.0 * jnp.sqrt(a_.astype(F32))), epi_dtype=_MXU_DTYPE)
    g['b_ff1'] = _mm("g_b_ff1", jnp.ones((16, T), _MXU_DTYPE), df1, "nn")[0:1]
    gw['w_ff1'] = _mm("g_w_ff1", df1, h2in, "tn")
    dh2in = _mm("d_h2in", df1, w_ff1_t, "nn")
    (dx_a, dmix), (d_gt1, g['ln1_g'], g['ln1_b'], d_sc2, d_sh2), (g['b_out'],) = _rowwise_vjp(
        "mid_bwd", _seg_mid, [x, mix], mid_b, [dh1_a, dh2in], tile, [0, 1], [0, 1, 2, 3, 4], t_dtypes=[F32, _MXU_DTYPE], colsum=[1])
    gw['w_out'] = _mm("g_w_out", merged, dmix, "tn")
    dmerged = _mm("d_merged", dmix, w_out, "nt", out_dtype=_MXU_DTYPE)
    (dproj, dpa, dpb), _, (cs_gates,) = _rowwise_vjp(
        "merge_bwd", _seg_merge, gates + [pa, pb], [], [dmerged], tile, [0, 1, 2, 3], [], t_dtypes=[_MXU_DTYPE] * 3, colsum=[0],
        finish=lambda dts, sc: [jnp.concatenate(dts[:2], axis=1), dts[2], dts[3]], out_widths=[2 * D_MODEL, D_MODEL, D_MODEL],
        into={0: (lax.empty((T, P_COLS), _MXU_DTYPE), 0)})
    gw['w_branch_a'] = _mm("g_w_branch_a", dpa, ya, "tn")
    gw['w_branch_b'] = _mm("g_w_branch_b", dpb, yb, "tn")
    dya = _mm("d_ya", dpa, w_ba_t, "nn")
    dyb = _mm("d_yb", dpb, w_bb_t, "nn")
    def send_rows(names):
        parts = []
        for n in names:
            per_dev = gw[n].reshape(_N_DEV, -1, D_MODEL)
            parts.append(jnp.pad(per_dev, ((0, 0), (0, _BIG[n][1] - per_dev.shape[1]), (0, 0))))
        return jnp.concatenate(parts, axis=1) if len(parts) > 1 else parts[0]

    send_early = send_rows(big_names[:-1])
    (dy, dr1, dk1, dv1, dg_), (d_r_k, g['gn_gain'], g['gn_bias']), _, sibling_early = _rowwise_vjp(
        "rwkv_post_bwd", _seg_rwkv_post, [y_, r_, k2_, v_, g_], post_b, [dyb], tile, [0, 1, 2, 3, 4], [0, 1, 2],
        rider=(send_early, _sibling_def, (0.0, 1.0)))
    g['r_k'] = d_r_k
    partials_early = _chip_partials("chip_partials_early", send_early, sibling_early, 128, _COMM_DTYPE)
    dr2, dlw, dk2, dv2, da, db, landed_early = _scan_bwd(*scan_in, states, inverses, dy, rider=(partials_early, _chips_def, (0.0, 1.0)))
    pre_tile = 128
    last_step = T // pre_tile - 1

    def pre_prep(prim):
        z_t, halo_t = prim
        return [z_t, _shift_down(z_t, halo_t, pl.program_id(0) == last_step)]

    def pre_finish(dts, sc):
        dz_direct, dprev = dts
        (row_after,) = sc

        @pl.when(pl.program_id(0) == 0)
        def _():
            row_after[...] = jnp.zeros_like(row_after)

        dz = dz_direct + _shift_up(dprev, row_after[...])
        row_after[...] = dprev[:_HALO]
        return [dz]

    (dproj,), (d_mu, g['w0'], d_wd, g['a0'], d_wa, d_wg, g['k_k'], g['k_a']), (cs_rw,) = _rowwise_vjp(
        "rwkv_pre_bwd", _seg_rwkv_pre, [z_rw, z_rw_halo], pre_b, [[dr1, dr2], dlw, [dk1, dk2], [dv1, dv2], da, db, dg_], pre_tile,
        [0, 1], [0, 1, 2, 3, 4, 5, 6, 7], t_dtypes=[_MXU_DTYPE], colsum=[0], prep=pre_prep, finish=pre_finish,
        out_widths=[RW_COLS], reverse=True, scratch=[(_HALO, RW_COLS)], into={0: (dproj, 1)})
    g['mu_shift'] = d_mu[:, :RW_USED]
    for n, d_ in (('w_decay_up', d_wd), ('w_aaa_up', d_wa), ('w_gate_up', d_wg)):
        r0, nr = _LORA[n]
        g[n] = d_[r0:r0 + nr]
    (dproj,), (g['g_ln_v'], g['b_ln_v'], g['w_spatial'], d_b_tg), (cs_g,) = _rowwise_vjp(
        "gmlp_bwd", _seg_gmlp, [z_gmlp], gmlp_b, [dya], G_CHUNK, [0], [0, 1, 2, 3], t_dtypes=[_MXU_DTYPE], colsum=[0],
        into={0: (dproj, 4)})
    g['b_spatial'] = d_b_tg[:, :8].T
    g['b_in'] = _from_padded(jnp.concatenate([cs_gates, cs_rw, cs_g], axis=1), 1)
    small_names = [n for n in _WEIGHTS if n not in _BIG and n not in ('w_ada', 'b_ada')]
    packed_g, g_places = _pack_rows([g[n] for n in small_names], row_mult=256)
    gw_in_t, small_all = _mm("g_w_in", dproj, h, "tn", rider=(packed_g, _gather_def, (0.0, 0.6, 1.0)))
    gw['w_in'] = _from_padded(gw_in_t, 0)
    send_late = send_rows(big_names[-1:])
    sibling_late = _exchange("pair_exchange_late", send_late, _sibling_def)
    partials_late = _chip_partials("chip_partials_late", send_late, sibling_late, 128, _COMM_DTYPE)
    dh, landed_late = _mm("d_h", dproj, w_in_t, "nn", rider=(partials_late, _chips_def, (0.0, 1.0)))

    def mod1_bwd(tv, bv):
        x_t, dh_t, dxa_t = tv
        (sc,) = bv
        return [dxa_t + dh_t * (1.0 + sc)], [jnp.sum(dh_t * x_t, axis=0, keepdims=True), jnp.sum(dh_t, axis=0, keepdims=True)]

    (grad_x,), (d_sc1, d_sh1) = _rowwise("modulate1_bwd", mod1_bwd, [x, dh, dx_a], [sc1], [(D_MODEL, F32)], [(1, D_MODEL)] * 2, tile)

    dmod = jnp.concatenate([d_sh1, d_sc1, d_gt1, d_sh2, d_sc2, d_gt2], axis=1).reshape(6 * D_MODEL // 128, 128)
    dmod_all = _all_gather("gather_dmod", dmod)
    g['b_ada'] = _sum_leading("sum_dmod", dmod_all).reshape(1, 6 * D_MODEL)
    dmod_mine = lax.dynamic_slice(dmod_all.reshape(_N_DEV, 6 * D_MODEL), (0, me * 768), (_N_DEV, 768))
    g_w_ada = _mm("g_w_ada", c_act_all, _pad_rows(dmod_mine, 16), "tn")

    small_sum = _unpack_rows(_sum_leading("sum_small", small_all), g_places)
    for n, t in zip(small_names, small_sum, strict=True):
        g[n] = t
    for n in _LORA:
        g[n] = lax.dynamic_slice(g[n], (0, me * R_HEAD), (g[n].shape[0], R_HEAD))
    g['w_ada'] = g_w_ada

    summed = jnp.concatenate([_sum_leading("sum_early", landed_early), _sum_leading("sum_late", landed_late)], axis=0)
    g.update(_unpack_big(summed, p))

    delta, new_m, new_v = {}, {}, {}
    own_call = ['w_ada'] + big_names
    for n in own_call:
        if n == 'w_in':
            r0 = _BIG[n][0]
            outs_t = _adamw("adamw_" + n, p[n].T, summed[r0:r0 + IN_SHARD], m[n].T, v[n].T)
            delta[n], new_m[n], new_v[n] = [o.T for o in outs_t]
        else:
            delta[n], new_m[n], new_v[n] = _adamw("adamw_" + n, p[n], g[n], m[n], v[n])
    rest = [n for n in _WEIGHTS if n not in own_call]
    outs = _adamw_many("adamw_rest", *[[d[n].reshape(p[n].shape) for n in rest] for d in (p, g, m, v)])
    for d, o in zip((delta, new_m, new_v), outs, strict=True):
        d.update(zip(rest, o, strict=True))
    return loss, grad_x, g, delta, new_m, new_v


def kernel(x, c, w_ada, b_ada, w_in, b_in, g_ln_v, b_ln_v, w_spatial, b_spatial, mu_shift, w0, w_decay_up, a0, w_aaa_up, w_gate_up, k_k, k_a, r_k, gn_gain, gn_bias, w_branch_a, w_branch_b, w_out, b_out, ln1_g, ln1_b, w_ff1, b_ff1, w_ff2, b_ff2, ln2_g, ln2_b, loss_target, m_w_ada, m_b_ada, m_w_in, m_b_in, m_g_ln_v, m_b_ln_v, m_w_spatial, m_b_spatial, m_mu_shift, m_w0, m_w_decay_up, m_a0, m_w_aaa_up, m_w_gate_up, m_k_k, m_k_a, m_r_k, m_gn_gain, m_gn_bias, m_w_branch_a, m_w_branch_b, m_w_out, m_b_out, m_ln1_g, m_ln1_b, m_w_ff1, m_b_ff1, m_w_ff2, m_b_ff2, m_ln2_g, m_ln2_b, v_w_ada, v_b_ada, v_w_in, v_b_in, v_g_ln_v, v_b_ln_v, v_w_spatial, v_b_spatial, v_mu_shift, v_w0, v_w_decay_up, v_a0, v_w_aaa_up, v_w_gate_up, v_k_k, v_k_a, v_r_k, v_gn_gain, v_gn_bias, v_w_branch_a, v_w_branch_b, v_w_out, v_b_out, v_ln1_g, v_ln1_b, v_w_ff1, v_b_ff1, v_w_ff2, v_b_ff2, v_ln2_g, v_ln2_b):
    given = dict(locals())
    shapes = {n: given[n].shape for n in _WEIGHTS}
    def two_d(a_):
        a_ = a_[0]
        return a_.reshape(1, -1) if a_.ndim == 1 else a_
    p = {n: two_d(given[n]) for n in _WEIGHTS}
    m = {n: two_d(given["m_" + n]) for n in _WEIGHTS}
    v = {n: two_d(given["v_" + n]) for n in _WEIGHTS}
    loss, grad_x, g, delta, new_m, new_v = _step(p, m, v, x[0], c, loss_target[0])
    outs = [loss, grad_x[None]]
    for d in (g, delta, new_m, new_v):
        outs += [d[n].reshape(shapes[n]) for n in _WEIGHTS]
    return tuple(outs)
```

```python
import functools

import jax
import jax.numpy as jnp
from jax import lax
from jax.experimental import pallas as pl
from jax.experimental.pallas import tpu as pltpu

F32 = jnp.float32
_MXU_DTYPE = jnp.bfloat16
_HI = lax.Precision.HIGHEST
_VMEM_LIMIT = 48 * 1024 * 1024
_MESH_ID = pl.DeviceIdType.MESH
_N_DEV = 8

D_MODEL = 1024
G_WIDTH = 512
G_CHUNK = 128
R_WIDTH = 512
R_HEADS = 8
R_HEAD = 64
LORA_W, LORA_A, LORA_G = 32, 32, 96
D_FF = 4096
ALPHA = 2.0 ** 0.25
LN_EPS = 1e-5
GN_EPS = 64e-5
SCAN_CHUNK = 64
SCAN_PER_STEP = 4
ADAM_LR, ADAM_B1, ADAM_B2, ADAM_EPS, ADAM_WD, ADAM_STEP = 0.001, 0.9, 0.999, 1e-08, 0.01, 10

P_COLS = 5120
RW_COLS = 2048
RW_USED = 3 * R_WIDTH + LORA_W + LORA_A + LORA_G
LORA_PAD = 256
IN_COLS = 2 * G_WIDTH + RW_USED + 2 * D_MODEL


def _cparams(sem=None, **kw):
    if sem is not None:
        kw["dimension_semantics"] = sem
    return pltpu.CompilerParams(vmem_limit_bytes=_VMEM_LIMIT, **kw)


def _dot(a, b, dims=(((1,), (0,)), ((), ())), hi=False):
    if hi:
        return lax.dot_general(a.astype(F32), b.astype(F32), dims, precision=_HI, preferred_element_type=F32)
    return lax.dot_general(a.astype(_MXU_DTYPE), b.astype(_MXU_DTYPE), dims, preferred_element_type=F32)


_NN = (((1,), (0,)), ((), ()))
_NT = (((1,), (1,)), ((), ()))
_TN = (((0,), (0,)), ((), ()))


def _pick(n, pref):
    for t in pref:
        if n % t == 0:
            return t
    return n


def _mm(name, a, b, mode, bias=None, out_dtype=F32, epi=None, epi_dtype=None, raw=True, beside=None, rider=None, tm=None, tn=None, tk=None):
    if mode == "nn":
        (M, K), (_, N) = a.shape, b.shape
    elif mode == "nt":
        (M, K), (N, _) = a.shape, b.shape
    else:
        (K, M), (_, N) = a.shape, b.shape
    tm = tm or _pick(M, (2048, 1280, 1024, 512, 256, 128, 64, 32, 16, 8))
    tn = tn or _pick(N, (1024, 512, 640, 384, 256, 128))
    tk = tk or _pick(K, (1024, 512, 256, 128))
    nk = K // tk
    dims = {"nn": _NN, "nt": _NT, "tn": _TN}[mode]
    a_spec = pl.BlockSpec((tk, tm), lambda i, j, k: (k, i)) if mode == "tn" else pl.BlockSpec((tm, tk), lambda i, j, k: (i, k))
    b_spec = pl.BlockSpec((tn, tk), lambda i, j, k: (j, k)) if mode == "nt" else pl.BlockSpec((tk, tn), lambda i, j, k: (k, j))
    o_spec = pl.BlockSpec((tm, tn), lambda i, j, k: (i, j))
    has_bias, has_beside = bias is not None, beside is not None
    two = epi is not None and not has_beside and raw
    only_epi = epi is not None and not has_beside and not raw
    grid = (M // tm, N // tn, nk)
    ride_shape, ride_sems, _ = rider[1](rider[0]) if rider else (None, [], None)

    def body(*refs):
        refs = list(refs)
        n_in = 2 + has_bias + has_beside
        if rider:
            sem_refs = [refs.pop() for _ in ride_sems][::-1]
            ride_out = refs.pop(n_in + 1 + 1 + two)
            ride_in = refs.pop(n_in)
            step = (pl.program_id(0) * grid[1] + pl.program_id(1)) * grid[2] + pl.program_id(2)
            _ride(rider, step, grid[0] * grid[1] * grid[2], ride_in, ride_out, sem_refs)
        a_ref, b_ref = refs[0], refs[1]
        bias_ref = refs[2] if has_bias else None
        beside_ref = refs[n_in - 1] if has_beside else None
        outs = refs[n_in:]
        o_ref, acc_ref = outs[0], outs[-1]
        k = pl.program_id(2)

        @pl.when(k == 0)
        def _():
            acc_ref[...] = jnp.zeros_like(acc_ref)

        acc_ref[...] += _dot(a_ref[...], b_ref[...], dims)

        @pl.when(k == nk - 1)
        def _():
            res = acc_ref[...]
            if has_bias:
                res = res + bias_ref[...]
            if has_beside:
                o_ref[...] = epi(res, beside_ref[...]).astype(o_ref.dtype)
            elif only_epi:
                o_ref[...] = epi(res).astype(o_ref.dtype)
            else:
                o_ref[...] = res.astype(o_ref.dtype)
                if two:
                    outs[1][...] = epi(res).astype(outs[1].dtype)

    in_specs = [a_spec, b_spec]
    args = [a, b]
    if has_bias:
        in_specs.append(pl.BlockSpec((1, tn), lambda i, j, k: (0, j)))
        args.append(bias)
    if has_beside:
        in_specs.append(o_spec)
        args.append(beside)
    out_shape = [jax.ShapeDtypeStruct((M, N), epi_dtype if (has_beside or only_epi) else out_dtype)]
    out_specs = [o_spec]
    if two:
        out_shape.append(jax.ShapeDtypeStruct((M, N), epi_dtype))
        out_specs.append(o_spec)
    if rider:
        in_specs.append(_HBM)
        args.append(rider[0])
        out_shape.append(ride_shape)
        out_specs.append(_HBM)
    res = pl.pallas_call(
        body, name=name, grid=grid, in_specs=in_specs, out_specs=out_specs, out_shape=out_shape,
        scratch_shapes=[pltpu.VMEM((tm, tn), F32)] + list(ride_sems),
        compiler_params=_cparams(("arbitrary",) * 3 if rider else ("parallel", "parallel", "arbitrary")),
    )(*args)
    return res if (two or rider) else res[0]


_HALO = 8


def _rowwise(name, fn, tiled, bcast, tiled_out, red_out, tile, reverse=False, scratch=(), rider=None):
    tiled = [t if isinstance(t, tuple) else (t, t.shape[1], 0) for t in tiled]
    T = next(t[0] for t in tiled if not isinstance(t[0], str)).shape[0]
    n = T // tile
    nt, nb, nto, nsc, nro = len(tiled), len(bcast), len(tiled_out), len(scratch), len(red_out)
    ride_shape, ride_sems, _ = rider[1](rider[0]) if rider else (None, [], None)
    nr = 1 if rider else 0
    into = [(j, t[2], t[3]) for j, t in enumerate(tiled_out) if len(t) == 4]
    na = len(into)

    def row_block(i):
        return n - 1 - i if reverse else i

    def body(*refs):
        i2 = nt + nb
        o0 = i2 + nr + na
        o1, o2 = o0 + nto, o0 + nto + nro
        s0 = o2 + nr
        if rider:
            _ride(rider, pl.program_id(0), n, refs[i2], refs[o2], refs[s0 + nsc:])
        t_refs, b_refs = refs[:nt], refs[nt:i2]
        to_refs, ro_refs = refs[o0:o1], refs[o1:o2]
        extra = (list(refs[s0:s0 + nsc]),) if nsc else ()
        touts, routs = fn([r[...] for r in t_refs], [r[...] for r in b_refs], *extra)
        for r, v in zip(to_refs, touts, strict=True):
            r[...] = v.astype(r.dtype)
        if ro_refs:
            i = pl.program_id(0)

            @pl.when(i == 0)
            def _():
                for r, v in zip(ro_refs, routs, strict=True):
                    r[...] = v.astype(F32)

            @pl.when(i > 0)
            def _():
                for r, v in zip(ro_refs, routs, strict=True):
                    r[...] += v.astype(F32)

    def whole(shape):
        nd = len(shape)
        return pl.BlockSpec(tuple(shape), lambda i: (0,) * nd)

    per_tile = tile // _HALO
    in_specs, arrays = [], []
    for t in tiled:
        if isinstance(t[0], str):
            _, arr, w, cb = t
            in_specs.append(pl.BlockSpec((_HALO, w), functools.partial(lambda i, cb: (jnp.maximum(row_block(i) * per_tile - 1, 0), cb), cb=cb)))
        else:
            arr, w, cb = t
            in_specs.append(pl.BlockSpec((tile, w), functools.partial(lambda i, cb: (row_block(i), cb), cb=cb)))
        arrays.append(arr)
    in_specs += [whole(b.shape) for b in bcast]
    out_specs, out_shape = [], []
    for t in tiled_out:
        cb = t[3] if len(t) == 4 else 0
        out_specs.append(pl.BlockSpec((tile, t[0]), functools.partial(lambda i, cb: (row_block(i), cb), cb=cb)))
        out_shape.append(jax.ShapeDtypeStruct(t[2].shape if len(t) == 4 else (T, t[0]), t[1]))
    out_specs += [whole(s) for s in red_out]
    out_shape += [jax.ShapeDtypeStruct(tuple(s), F32) for s in red_out]
    ride_args = []
    if rider:
        in_specs.append(_HBM)
        out_specs.append(_HBM)
        out_shape.append(ride_shape)
        ride_args = [rider[0]]
    in_specs += [pl.BlockSpec(memory_space=pl.ANY)] * na
    aliases = {nt + nb + nr + k: j for k, (j, _, _) in enumerate(into)}
    res = pl.pallas_call(
        body, name=name, grid=(n,), in_specs=in_specs, out_specs=out_specs, out_shape=out_shape,
        scratch_shapes=[pltpu.VMEM(tuple(s), F32) for s in scratch] + list(ride_sems),
        input_output_aliases=aliases, compiler_params=_cparams(("arbitrary",)),
    )(*arrays, *bcast, *ride_args, *[buf for (_, buf, _) in into])
    if rider:
        return list(res[:nto]), list(res[nto:nto + nro]), res[-1]
    return list(res[:nto]), list(res[nto:])


def _rowwise_vjp(name, f, tiled, bcast, cts, tile, wrt_t, wrt_b, t_dtypes=None, colsum=(), prep=None, finish=None,
                 out_widths=None, reverse=False, scratch=(), rider=None, into=None):
    tiled = [t if isinstance(t, tuple) else (t, t.shape[1], 0) for t in tiled]
    npr = len(tiled)
    t_dtypes = t_dtypes or [F32] * len(wrt_t)
    groups = [c if isinstance(c, list) else [c] for c in cts]
    cts = [a_ for grp in groups for a_ in grp]

    def fn(tv, bv, sc=None):
        prim, flat_ct = tv[:npr], list(tv[npr:])
        if prep is not None:
            prim = prep(prim)
        ct = []
        for grp in groups:
            parts = [flat_ct.pop(0).astype(F32) for _ in grp]
            ct.append(functools.reduce(lambda p_, q_: p_ + q_, parts))

        def g(dt_vals, db_vals):
            full_t, full_b = list(prim), list(bv)
            for i, v in zip(wrt_t, dt_vals, strict=True):
                full_t[i] = v
            for j, v in zip(wrt_b, db_vals, strict=True):
                full_b[j] = v
            return f(full_t, full_b)

        outs, pull = jax.vjp(g, [prim[i].astype(F32) for i in wrt_t], [bv[j] for j in wrt_b])
        dts, dbs = pull([c.astype(o.dtype) for c, o in zip(ct, outs, strict=True)])
        if finish is not None:
            dts = finish(dts, sc)
        sums = [jnp.sum(dts[i].astype(F32), axis=0, keepdims=True) for i in colsum]
        return dts, list(dbs) + sums

    widths = out_widths or [tiled[i][1] for i in wrt_t]
    tiled_out = [(w, dt) + tuple((into or {}).get(j, ())) for j, (w, dt) in enumerate(zip(widths, t_dtypes, strict=True))]
    red_out = [bcast[j].shape for j in wrt_b] + [(1, widths[i]) for i in colsum]
    res = _rowwise(name, fn, tiled + list(cts), bcast, tiled_out, red_out, tile, reverse=reverse, scratch=scratch, rider=rider)
    dts, reds = res[0], res[1]
    nb = len(wrt_b)
    return (dts, reds[:nb], reds[nb:]) + tuple(res[2:])


def _layer_norm(x, g, b, eps):
    mu = jnp.mean(x, axis=-1, keepdims=True)
    xc = x - mu
    var = jnp.mean(xc * xc, axis=-1, keepdims=True)
    return xc * lax.rsqrt(var + eps) * g + b


def _gelu_tanh(x):
    return 0.5 * x * (1.0 + jnp.tanh(0.7978845608028654 * (x + 0.044715 * (x * x * x))))


def _sigmoid(x):
    return 1.0 / (1.0 + jnp.exp(-x))


def _seg_modulate(tv, bv):
    (x,), (sc, sh) = tv, bv
    return [x * (1.0 + sc) + sh]


def _seg_gmlp(tv, bv):
    (z,), (g_ln, b_ln, ws, b_tg, expand) = tv, bv
    bias_full = _dot(b_tg, expand, hi=True)
    zz = _gelu_tanh(z)
    u, v = zz[:, :G_WIDTH], zz[:, G_WIDTH:]
    v = _layer_norm(v, g_ln, b_ln, LN_EPS)
    row = lax.broadcasted_iota(jnp.int32, (G_CHUNK, G_CHUNK), 0)
    col = lax.broadcasted_iota(jnp.int32, (G_CHUNK, G_CHUNK), 1)
    causal = col <= row
    first_group = lax.broadcasted_iota(jnp.int32, (G_CHUNK, 128), 1) < 64
    parts = []
    for p in range(4):
        vp = v[:, 128 * p:128 * (p + 1)]
        s_even = _dot(jnp.where(causal, ws[2 * p], 0.0), vp)
        s_odd = _dot(jnp.where(causal, ws[2 * p + 1], 0.0), vp)
        parts.append(jnp.where(first_group, s_even, s_odd))
    s = jnp.concatenate(parts, axis=1) + bias_full
    return [u * s]


def _split2(x):
    hi = x.astype(_MXU_DTYPE)
    return hi, (x - hi.astype(F32)).astype(_MXU_DTYPE)


@jax.custom_vjp
def _group_sum(x, ones_blocks):
    hi, lo = _split2(x)
    return _dot(hi, ones_blocks) + _dot(lo, ones_blocks)


def _group_sum_fwd(x, ones_blocks):
    return _group_sum(x, ones_blocks), ones_blocks


def _group_sum_bwd(ones_blocks, ct):
    return _group_sum(ct, ones_blocks), jnp.zeros_like(ones_blocks)


_group_sum.defvjp(_group_sum_fwd, _group_sum_bwd)


def _shift_down(z, halo, is_first):
    _, W = z.shape
    rolled = pltpu.roll(z, 1, 0)
    before = jnp.where(is_first, 0.0, pltpu.roll(halo, 1, 0))
    top_row = lax.broadcasted_iota(jnp.int32, (_HALO, W), 0) == 0
    return jnp.concatenate([jnp.where(top_row, before, rolled[:_HALO]), rolled[_HALO:]], axis=0)


def _shift_up(d, after):
    tile, W = d.shape
    rolled = pltpu.roll(d, tile - 1, 0)
    last_row = lax.broadcasted_iota(jnp.int32, (_HALO, W), 0) == _HALO - 1
    bottom = jnp.where(last_row, pltpu.roll(after, _HALO - 1, 0), rolled[tile - _HALO:])
    return jnp.concatenate([rolled[:tile - _HALO], bottom], axis=0)


def _seg_rwkv_pre(tv, bv):
    (z, prev), (mu, w0, wd, a0, wa, wg, k_k, k_a, gsum) = tv, bv
    zs = z + (prev - z) * mu
    r, k, v = zs[:, 0:512], zs[:, 512:1024], zs[:, 1024:1536]
    zl = zs[:, 3 * R_WIDTH:3 * R_WIDTH + LORA_PAD]
    x = w0 + _dot(jnp.tanh(zl), wd)
    softplus = jnp.maximum(-x, 0.0) + jnp.log(1.0 + jnp.exp(-jnp.abs(x)))
    lw = -jnp.exp(-softplus - 0.5)
    a = _sigmoid(a0 + _dot(zl, wa))
    g = _dot(_sigmoid(zl), wg)
    kk = k * k_k
    nrm = jnp.sqrt(_group_sum(kk * kk, gsum))
    kk = kk / jnp.maximum(nrm, 1e-12)
    k2 = k * (1.0 + (a - 1.0) * k_a)
    return [r, lw, k2, v, -kk, kk * a, g]


def _seg_rwkv_post(tv, bv):
    (y, r, k2, v, g), (r_k, gain, bias, gsum) = tv, bv
    mu = _group_sum(y, gsum) * (1.0 / R_HEAD)
    yc = y - mu
    var = _group_sum(yc * yc, gsum) * (1.0 / R_HEAD)
    yn = yc * lax.rsqrt(var + GN_EPS) * gain + bias
    bonus = _group_sum(r * k2 * r_k, gsum) * v
    return [(yn + bonus) * g]


def _seg_merge(tv, bv):
    (ga, gb, pa, pb), () = tv, bv
    return [_sigmoid(ga) * pa + _sigmoid(gb) * pb]


def _seg_mid(tv, bv):
    (x, mix), (gt1, g1, b1, sc2, sh2) = tv, bv
    h1 = _layer_norm(ALPHA * x + gt1 * mix, g1, b1, LN_EPS)
    return [h1, h1 * (1.0 + sc2) + sh2]


def _seg_relu2(tv, bv):
    (f1,), () = tv, bv
    return [jnp.square(jnp.maximum(f1, 0.0))]


def _seg_loss(tv, bv):
    (h1, ff, target), (gt2, g2, b2) = tv, bv
    out = _layer_norm(ALPHA * h1 + gt2 * ff, g2, b2, LN_EPS)
    err = jnp.square(out - target)
    return 0.5 * jnp.sum(jnp.mean(err, axis=-1))


_BNN = (((2,), (1,)), ((0,), (0,)))
_BNT = (((2,), (2,)), ((0,), (0,)))
_BTN = (((1,), (1,)), ((0,), (0,)))


def _tri_dot(x, dims):
    L = x.shape[0]
    tri = (lax.broadcasted_iota(jnp.int32, (L, L), 1) <= lax.broadcasted_iota(jnp.int32, (L, L), 0)).astype(F32)
    hi, lo = _split2(x)
    return _dot(tri, hi, dims) + _dot(tri, lo, dims)


@jax.custom_vjp
def _running_sum(x):
    return _tri_dot(x, _NN)


_running_sum.defvjp(lambda x: (_tri_dot(x, _NN), None), lambda _, ct: (_tri_dot(ct, _TN),))


def _cut_heads(x):
    return jnp.stack([x[:, R_HEAD * h:R_HEAD * (h + 1)] for h in range(R_HEADS)])


def _join_heads(x):
    return jnp.concatenate([x[h] for h in range(R_HEADS)], axis=1)


@jax.custom_vjp
def _split_heads(x):
    return _cut_heads(x)


@jax.custom_vjp
def _merge_heads(x):
    return _join_heads(x)


_split_heads.defvjp(lambda x: (_cut_heads(x), None), lambda _, ct: (_join_heads(ct),))
_merge_heads.defvjp(lambda x: (_join_heads(x), None), lambda _, ct: (_cut_heads(ct),))


def _inverse_pullback(inv, ct):
    return _dot(_dot(inv, ct, _BTN), inv, _BNT)


@jax.custom_vjp
def _unit_lower_inverse(n_mat):
    H, L, _ = n_mat.shape
    eye = lax.broadcasted_iota(jnp.int32, (H, L, L), 1) == lax.broadcasted_iota(jnp.int32, (H, L, L), 2)
    inv = jnp.where(eye, 1.0, 0.0) + n_mat
    pw = n_mat
    n = 2
    while n < L:
        pw = _dot(pw, pw, _BNN)
        inv = inv + _dot(inv, pw, _BNN)
        n *= 2
    return inv


def _unit_lower_inverse_fwd(n_mat):
    inv = _unit_lower_inverse(n_mat)
    return inv, inv


_unit_lower_inverse.defvjp(_unit_lower_inverse_fwd, lambda inv, ct: (_inverse_pullback(inv, ct),))


@jax.custom_vjp
def _known_inverse(n_mat, inv):
    return inv


_known_inverse.defvjp(lambda n_mat, inv: (inv, inv), lambda inv, ct: (_inverse_pullback(inv, ct), jnp.zeros_like(inv)))


def _scan_chunk(r, lw, k, v, a, b, s0, inv=None, with_inverse=False):
    L, H = r.shape[0], R_HEADS
    row = lax.broadcasted_iota(jnp.int32, (H, L, L), 1)
    col = lax.broadcasted_iota(jnp.int32, (H, L, L), 2)
    incl, strict = col <= row, col < row
    cs = _running_sum(lw)
    cs_end = cs[L - 1:L, :]
    p, p_inv, to_end = jnp.exp(cs), jnp.exp(-cs), jnp.exp(cs_end - cs)
    at, bt, kt, rt = [_split_heads(t) for t in (a * jnp.exp(cs - lw), b * p_inv, k * p_inv, r * p)]
    b_end, k_end, v = [_split_heads(t) for t in (b * to_end, k * to_end, v)]
    a_ab = jnp.where(strict, _dot(at, bt, _BNT), 0.0)
    a_ak = jnp.where(strict, _dot(at, kt, _BNT), 0.0)
    a_rb = jnp.where(incl, _dot(rt, bt, _BNT), 0.0)
    a_rk = jnp.where(incl, _dot(rt, kt, _BNT), 0.0)
    inv = _unit_lower_inverse(a_ab) if inv is None else _known_inverse(a_ab, inv)
    u = _dot(inv, _dot(at, s0, _BNT) + _dot(a_ak, v, _BNN), _BNN)
    y = _merge_heads(_dot(rt, s0, _BNT) + _dot(a_rb, u, _BNN) + _dot(a_rk, v, _BNN))
    s1 = s0 * _split_heads(jnp.exp(cs_end)) + _dot(u, b_end, _BTN) + _dot(v, k_end, _BTN)
    return (y, s1, inv) if with_inverse else (y, s1)


def _scan_fwd(r, lw, k, v, a, b, rider):
    T = r.shape[0]
    H, N, L, P = R_HEADS, R_HEAD, SCAN_CHUNK, SCAN_PER_STEP
    nc = T // L
    steps = nc // P
    ride_shape, ride_sems, _ = rider[1](rider[0])

    def body(r_ref, lw_ref, k_ref, v_ref, a_ref, b_ref, ride_in, y_ref, st_ref, inv_ref, ride_out, s_ref, *sem_refs):
        _ride(rider, pl.program_id(0), steps, ride_in, ride_out, sem_refs)

        @pl.when(pl.program_id(0) == 0)
        def _():
            s_ref[...] = jnp.zeros_like(s_ref)

        s0 = s_ref[...]
        for j in range(P):
            rows = pl.ds(j * L, L)
            st_ref[j] = s0
            y, s0, inv = _scan_chunk(*[t[rows, :] for t in (r_ref, lw_ref, k_ref, v_ref, a_ref, b_ref)], s0, with_inverse=True)
            y_ref[rows, :] = y
            inv_ref[j] = inv
        s_ref[...] = s0

    blk = pl.BlockSpec((P * L, R_WIDTH), lambda c: (c, 0))
    per_chunk = pl.BlockSpec((P, H, N, N), lambda c: (c, 0, 0, 0))
    return pl.pallas_call(
        body, name="scan_fwd", grid=(steps,), in_specs=[blk] * 6 + [_HBM], out_specs=[blk, per_chunk, per_chunk, _HBM],
        out_shape=[jax.ShapeDtypeStruct((T, R_WIDTH), F32)] + [jax.ShapeDtypeStruct((nc, H, N, N), F32)] * 2 + [ride_shape],
        scratch_shapes=[pltpu.VMEM((H, N, N), F32)] + list(ride_sems),
        compiler_params=_cparams(("arbitrary",)),
    )(r, lw, k, v, a, b, rider[0])


def _scan_bwd(r, lw, k, v, a, b, states, inverses, dy, rider):
    T = r.shape[0]
    H, N, L, P = R_HEADS, R_HEAD, SCAN_CHUNK, SCAN_PER_STEP
    nc = T // L
    steps = nc // P
    ride_shape, ride_sems, _ = rider[1](rider[0])

    def body(r_ref, lw_ref, k_ref, v_ref, a_ref, b_ref, st_ref, inv_ref, dy_ref, ride_in,
             dr_ref, dlw_ref, dk_ref, dv_ref, da_ref, db_ref, ride_out, ds_ref, *sem_refs):
        _ride(rider, pl.program_id(0), steps, ride_in, ride_out, sem_refs)

        @pl.when(pl.program_id(0) == 0)
        def _():
            ds_ref[...] = jnp.zeros_like(ds_ref)

        ds = ds_ref[...]
        for j in reversed(range(P)):
            rows = pl.ds(j * L, L)
            args = [t[rows, :] for t in (r_ref, lw_ref, k_ref, v_ref, a_ref, b_ref)] + [st_ref[j]]
            inv = inv_ref[j]
            _, pull = jax.vjp(lambda *xs, inv=inv: _scan_chunk(*xs, inv=inv), *args)
            grads = pull((dy_ref[rows, :], ds))
            for o_ref, g_ in zip((dr_ref, dlw_ref, dk_ref, dv_ref, da_ref, db_ref), grads[:6], strict=True):
                o_ref[rows, :] = g_
            ds = grads[6]
        ds_ref[...] = ds

    blk = pl.BlockSpec((P * L, R_WIDTH), lambda c: (steps - 1 - c, 0))
    per_chunk = pl.BlockSpec((P, H, N, N), lambda c: (steps - 1 - c, 0, 0, 0))
    return pl.pallas_call(
        body, name="scan_bwd", grid=(steps,), in_specs=[blk] * 6 + [per_chunk, per_chunk, blk, _HBM], out_specs=[blk] * 6 + [_HBM],
        out_shape=[jax.ShapeDtypeStruct((T, R_WIDTH), F32)] * 6 + [ride_shape],
        scratch_shapes=[pltpu.VMEM((H, N, N), F32)] + list(ride_sems),
        compiler_params=_cparams(("arbitrary",)),
    )(r, lw, k, v, a, b, states, inverses, dy, rider[0])


def _place():
    x, y, c = lax.axis_index("x"), lax.axis_index("y"), lax.axis_index("c")
    return x, y, c


def _gather_def(block):
    R, C = block.shape

    def phases(x_ref, out_ref, send_sems, recv_sems, local_sem):
        x, y, c = _place()
        me, sibling = (x, y, c), (x, y, 1 - c)
        chips = [(1 - x, y), (x, 1 - y), (1 - x, 1 - y)]

        def slot(px, py, pc):
            return out_ref.at[4 * px + 2 * py + pc]

        def copy(k, blk, to, src=None):
            return pltpu.make_async_remote_copy(
                src_ref=slot(*blk) if src is None else src, dst_ref=slot(*blk),
                send_sem=send_sems.at[k], recv_sem=recv_sems.at[k], device_id=to, device_id_type=_MESH_ID)

        mine = pltpu.make_async_copy(x_ref, slot(*me), local_sem)
        first = [copy(0, me, sibling, src=x_ref)]
        first += [copy(1 + j, me, (*chip, c), src=x_ref) for j, chip in enumerate(chips)]
        passed = [copy(4 + j, (*chip, c), sibling) for j, chip in enumerate(chips)]

        def begin():
            mine.start()
            for cp in first:
                cp.start()

        def forward():
            for j, chip in enumerate(chips):
                copy(1 + j, (*chip, c), me).wait_recv()
                passed[j].start()

        def finish():
            copy(0, sibling, me).wait_recv()
            for j, chip in enumerate(chips):
                copy(4 + j, (*chip, 1 - c), me).wait_recv()
            for cp in first + passed:
                cp.wait_send()
            mine.wait()

        return [begin, forward, finish]

    sems = [pltpu.SemaphoreType.DMA((7,)), pltpu.SemaphoreType.DMA((7,)), pltpu.SemaphoreType.DMA]
    return jax.ShapeDtypeStruct((_N_DEV, R, C), block.dtype), sems, phases


def _sibling_def(blocks):
    _, R, C = blocks.shape

    def phases(x_ref, out_ref, send_sems, recv_sems):
        x, y, c = _place()
        copies = [pltpu.make_async_remote_copy(
            src_ref=x_ref.at[2 * q + (1 - c)], dst_ref=out_ref.at[q], send_sem=send_sems.at[q], recv_sem=recv_sems.at[q],
            device_id=(x, y, 1 - c), device_id_type=_MESH_ID) for q in range(4)]

        def begin():
            for cp in copies:
                cp.start()

        def finish():
            for cp in copies:
                cp.wait()

        return [begin, finish]

    return jax.ShapeDtypeStruct((4, R, C), blocks.dtype), [pltpu.SemaphoreType.DMA((4,)), pltpu.SemaphoreType.DMA((4,))], phases


def _chips_def(partials):
    def phases(x_ref, out_ref, send_sems, recv_sems, local_sem):
        x, y, c = _place()
        my_chip = 2 * x + y
        mine = pltpu.make_async_copy(x_ref.at[my_chip], out_ref.at[my_chip], local_sem)
        copies = []
        for rel in range(1, 4):
            px, py = (1 - x if (rel >> 1) & 1 else x), (1 - y if rel & 1 else y)
            copies.append(pltpu.make_async_remote_copy(
                src_ref=x_ref.at[2 * px + py], dst_ref=out_ref.at[my_chip],
                send_sem=send_sems.at[rel - 1], recv_sem=recv_sems.at[rel - 1],
                device_id=(px, py, c), device_id_type=_MESH_ID))

        def begin():
            mine.start()
            for cp in copies:
                cp.start()

        def finish():
            for cp in copies:
                cp.wait()
            mine.wait()

        return [begin, finish]

    sems = [pltpu.SemaphoreType.DMA((3,)), pltpu.SemaphoreType.DMA((3,)), pltpu.SemaphoreType.DMA]
    return jax.ShapeDtypeStruct(partials.shape, partials.dtype), sems, phases


_HBM = pl.BlockSpec(memory_space=pltpu.HBM)


def _exchange(name, array, definition):
    out_shape, sems, phases = definition(array)

    def body(x_ref, out_ref, *sem_refs):
        for phase in phases(x_ref, out_ref, *sem_refs):
            phase()

    return pl.pallas_call(body, name=name, in_specs=[_HBM], out_specs=_HBM, out_shape=out_shape, scratch_shapes=sems)(array)


def _ride(rider, step, nsteps, x_ref, out_ref, sem_refs):
    array, definition, fractions = rider
    for phase, frac in zip(definition(array)[2](x_ref, out_ref, *sem_refs), fractions, strict=True):
        pl.when(step == min(int(frac * nsteps), nsteps - 1))(phase)


def _all_gather(name, block):
    return _exchange(name, block, _gather_def)


def _chip_partials(name, blocks, from_sibling, tile, out_dtype):
    _, R, C = blocks.shape

    def body(x_ref, s_ref, o_ref):
        c = lax.axis_index("c")
        for q in range(4):
            o_ref[q] = (x_ref[2 * q + c] + s_ref[q]).astype(o_ref.dtype)

    return pl.pallas_call(
        body, name=name, grid=(R // tile,),
        in_specs=[pl.BlockSpec((_N_DEV, tile, C), lambda i: (0, i, 0)), pl.BlockSpec((4, tile, C), lambda i: (0, i, 0))],
        out_specs=pl.BlockSpec((4, tile, C), lambda i: (0, i, 0)), out_shape=jax.ShapeDtypeStruct((4, R, C), out_dtype),
        compiler_params=_cparams(("parallel",)),
    )(blocks, from_sibling)


def _sum_leading(name, x, tile=None):
    n, R, C = x.shape
    tile = tile or _pick(R, (512, 256, 128, 64, 32, 16, 8))

    def body(x_ref, o_ref):
        acc = x_ref[0].astype(F32)
        for k in range(1, n):
            acc = acc + x_ref[k].astype(F32)
        o_ref[...] = acc

    return pl.pallas_call(
        body, name=name, grid=(R // tile,), in_specs=[pl.BlockSpec((n, tile, C), lambda i: (0, i, 0))],
        out_specs=pl.BlockSpec((tile, C), lambda i: (i, 0)), out_shape=jax.ShapeDtypeStruct((R, C), F32),
        compiler_params=_cparams(("parallel",)),
    )(x)


def _adamw_update(w_, g_, m_, v_):
    m2 = ADAM_B1 * m_ + (1.0 - ADAM_B1) * g_
    v2 = ADAM_B2 * v_ + (1.0 - ADAM_B2) * jnp.square(g_)
    m_hat = m2 / (1.0 - ADAM_B1 ** ADAM_STEP)
    v_hat = v2 / (1.0 - ADAM_B2 ** ADAM_STEP)
    delta = -ADAM_LR * (m_hat / (jnp.sqrt(v_hat) + ADAM_EPS) + ADAM_WD * w_)
    return delta, m2, v2


def _adamw(name, w, g, m, v):
    R, C = w.shape
    tile = _pick(R, (256, 128, 64, 32, 16, 8))
    outs, _ = _rowwise(name, lambda tv, bv: (list(_adamw_update(*tv)), []), [w, g, m, v], [], [(C, F32)] * 3, [], tile)
    return outs


def _adamw_many(name, ws, gs, ms, vs):
    n = len(ws)

    def body(*refs):
        ins, outs = refs[:4 * n], refs[4 * n:]
        for i in range(n):
            res = _adamw_update(ins[i][...], ins[n + i][...], ins[2 * n + i][...], ins[3 * n + i][...])
            for j in range(3):
                outs[j * n + i][...] = res[j]

    out_shape = [jax.ShapeDtypeStruct(w.shape, F32) for w in ws] * 3
    res = pl.pallas_call(body, name=name, out_shape=out_shape, compiler_params=_cparams())(*ws, *gs, *ms, *vs)
    return res[:n], res[n:2 * n], res[2 * n:]


def _pack_rows(arrs, lanes=128, row_mult=8):
    flat, places, off = [], [], 0
    for a_ in arrs:
        n = a_.size
        flat.append(a_.reshape(-1).astype(F32))
        places.append((off, n, a_.shape))
        off += n
    total = -(-off // (lanes * row_mult)) * (lanes * row_mult)
    if total > off:
        flat.append(jnp.zeros((total - off,), F32))
    return jnp.concatenate(flat).reshape(total // lanes, lanes), places


def _unpack_rows(packed, places):
    flat = packed.reshape(-1)
    return [flat[o:o + n].reshape(s) for (o, n, s) in places]


_WEIGHTS = ['w_ada', 'b_ada', 'w_in', 'b_in', 'g_ln_v', 'b_ln_v', 'w_spatial', 'b_spatial', 'mu_shift', 'w0', 'w_decay_up', 'a0',
            'w_aaa_up', 'w_gate_up', 'k_k', 'k_a', 'r_k', 'gn_gain', 'gn_bias', 'w_branch_a', 'w_branch_b', 'w_out', 'b_out',
            'ln1_g', 'ln1_b', 'w_ff1', 'b_ff1', 'w_ff2', 'b_ff2', 'ln2_g', 'ln2_b']
_BIG = {'w_ff1': (0, 512), 'w_ff2': (512, 512), 'w_out': (1024, 128), 'w_branch_a': (1152, 64), 'w_branch_b': (1216, 64), 'w_in': (1280, 640)}
_LATER_ROWS = 1280
_CUT_BY_COLS = ('w_ff1', 'w_in', 'w_branch_a', 'w_branch_b')
IN_SHARD = IN_COLS // _N_DEV
_LORA = {'w_decay_up': (0, LORA_W), 'w_aaa_up': (LORA_W, LORA_A), 'w_gate_up': (LORA_W + LORA_A, LORA_G)}
_COMM_DTYPE = jnp.bfloat16


def _pad_rows(a, rows):
    return jnp.pad(a, ((0, rows - a.shape[0]),) + ((0, 0),) * (a.ndim - 1))


def _pack_big(shards):
    blocks = []
    for n, (_, rows) in _BIG.items():
        a = shards[n].T if n in _CUT_BY_COLS else shards[n]
        blocks.append(_pad_rows(a.reshape(-1, D_MODEL), rows))
    return jnp.concatenate(blocks, axis=0)


def _unpack_big(block, like):
    out = {}
    for n, (r0, _) in _BIG.items():
        rr, cc = like[n].shape
        if n in _CUT_BY_COLS:
            out[n] = block[r0:r0 + rr * cc // D_MODEL].reshape(cc, rr).T
        else:
            out[n] = block[r0:r0 + rr]
    return out


def _to_padded(a, axis):
    g_end = 2 * G_WIDTH
    r_end = g_end + RW_USED
    take = lambda lo, hi: lax.slice_in_dim(a, lo, hi, axis=axis)
    zshape = list(a.shape)
    zshape[axis] = RW_COLS - RW_USED
    return jnp.concatenate([take(r_end, IN_COLS), take(g_end, r_end), jnp.zeros(zshape, a.dtype), take(0, g_end)], axis=axis)


def _from_padded(a, axis):
    take = lambda lo, hi: lax.slice_in_dim(a, lo, hi, axis=axis)
    return jnp.concatenate([take(2 * D_MODEL + RW_COLS, P_COLS), take(2 * D_MODEL, 2 * D_MODEL + RW_USED), take(0, 2 * D_MODEL)], axis=axis)


def _step(p, m, v, x, c, target):
    T = x.shape[0]
    xi, yi, ci = _place()
    me = 4 * xi + 2 * yi + ci
    tile = 256

    lane = jnp.arange(R_WIDTH)
    gsum = (lane[:, None] // R_HEAD == lane[None, :] // R_HEAD).astype(F32)
    expand = (jnp.arange(128)[:, None] == (lane[None, :] // (G_WIDTH // 8))).astype(F32)

    (c_act,), _ = _rowwise("silu_c", lambda tv, bv: ([tv[0] * _sigmoid(tv[0])], []), [c], [], [(D_MODEL, F32)], [], 1)
    small, places = _pack_rows([c_act, p['w_decay_up'], p['w_aaa_up'], p['w_gate_up']])
    small_all = _all_gather("gather_small", small)
    per_dev = [_unpack_rows(small_all[d], places) for d in range(_N_DEV)]
    c_act_all = _pad_rows(jnp.concatenate([pd[0] for pd in per_dev], axis=0), 16)
    lora_full = {n: jnp.concatenate([pd[i + 1] for pd in per_dev], axis=1) for i, n in enumerate(_LORA)}
    lora_pad = {n: jnp.zeros((LORA_PAD, R_WIDTH), F32).at[r0:r0 + nr].set(lora_full[n]) for n, (r0, nr) in _LORA.items()}

    big_names = list(_BIG)
    my_rows = _pack_big(p).astype(_MXU_DTYPE)
    b_in_p = _to_padded(p['b_in'], 1)
    mu_p = jnp.concatenate([p['mu_shift'], jnp.zeros((1, RW_COLS - RW_USED), F32)], axis=1)

    b_ada_mine = lax.dynamic_slice(p['b_ada'], (0, me * 768), (1, 768))
    mod_cols = _mm("ada_mod", c_act_all, p['w_ada'], "nn", bias=b_ada_mine)
    mod_all = _all_gather("gather_mod", mod_cols)
    mod = lax.dynamic_index_in_dim(mod_all, me, axis=1, keepdims=False).reshape(1, 6 * D_MODEL)
    sh1, sc1, gt1, sh2, sc2, gt2 = [mod[:, i * D_MODEL:(i + 1) * D_MODEL] for i in range(6)]

    (h,), _, w_in_all = _rowwise("modulate1", lambda tv, bv: (_seg_modulate(tv, bv), []), [x], [sc1, sh1], [(D_MODEL, _MXU_DTYPE)], [], tile,
                                 rider=(my_rows[_LATER_ROWS:], _gather_def, (0.0, 0.5, 1.0)))
    w_in_t = _to_padded(w_in_all[:, :IN_SHARD].reshape(IN_COLS, D_MODEL), 0)
    proj = _mm("in_proj", h, w_in_t, "nt", bias=b_in_p)
    ws = p['w_spatial']
    b_tg = jnp.zeros((G_CHUNK, 128), F32).at[:, :8].set(p['b_spatial'].T)
    gmlp_b = [p['g_ln_v'], p['b_ln_v'], ws, b_tg, expand]
    z_gmlp = (proj, 2 * G_WIDTH, 4)
    (ya,), _ = _rowwise("gmlp", lambda tv, bv: (_seg_gmlp(tv, bv), []), [z_gmlp], gmlp_b, [(G_WIDTH, F32)], [], G_CHUNK)
    z_rw = (proj, RW_COLS, 1)
    z_rw_halo = ("halo", proj, RW_COLS, 1)
    pre_b = [mu_p, p['w0'], lora_pad['w_decay_up'], p['a0'], lora_pad['w_aaa_up'], lora_pad['w_gate_up'], p['k_k'], p['k_a'], gsum]

    def pre_fwd(tv, bv):
        z_t, halo_t = tv
        return _seg_rwkv_pre([z_t, _shift_down(z_t, halo_t, pl.program_id(0) == 0)], bv), []

    pre_out, _ = _rowwise("rwkv_pre", pre_fwd, [z_rw, z_rw_halo], pre_b, [(R_WIDTH, F32)] * 7, [], tile)
    r_, lw_, k2_, v_, a_, b_, g_ = pre_out
    scan_in = (r_, lw_, k2_, v_, a_, b_)
    y_, states, inverses, later_all = _scan_fwd(*scan_in, rider=(my_rows[:_LATER_ROWS], _gather_def, (0.0, 0.875, 1.0)))

    def whole(n, rows):
        r0 = _BIG[n][0]
        return later_all[:, r0:r0 + rows].reshape(_N_DEV * rows, D_MODEL)

    w_ff1_t, w_ff2, w_out = whole('w_ff1', 512), whole('w_ff2', 512), whole('w_out', 128)
    w_ba_t = whole('w_branch_a', 64).reshape(D_MODEL, G_WIDTH)
    w_bb_t = whole('w_branch_b', 64).reshape(D_MODEL, R_WIDTH)
    post_b = [p['r_k'].reshape(1, R_WIDTH), p['gn_gain'], p['gn_bias'], gsum]
    (yb,), _ = _rowwise("rwkv_post", lambda tv, bv: (_seg_rwkv_post(tv, bv), []), [y_, r_, k2_, v_, g_], post_b, [(R_WIDTH, F32)], [], tile)
    pa = _mm("branch_a", ya, w_ba_t, "nt", out_dtype=_MXU_DTYPE)
    pb = _mm("branch_b", yb, w_bb_t, "nt", out_dtype=_MXU_DTYPE)
    gates = [(proj, D_MODEL, 0), (proj, D_MODEL, 1)]
    (merged,), _ = _rowwise("merge", lambda tv, bv: (_seg_merge(tv, bv), []), gates + [pa, pb], [], [(D_MODEL, _MXU_DTYPE)], [], tile)
    mix = _mm("out_proj", merged, w_out, "nn", bias=p['b_out'])
    mid_b = [gt1, p['ln1_g'], p['ln1_b'], sc2, sh2]
    (h1, h2in), _ = _rowwise("mid", lambda tv, bv: (_seg_mid(tv, bv), []), [x, mix], mid_b, [(D_MODEL, F32), (D_MODEL, _MXU_DTYPE)], [], tile)
    act = _mm("ff1", h2in, w_ff1_t, "nt", bias=p['b_ff1'], epi=lambda t: _seg_relu2([t], [])[0], epi_dtype=_MXU_DTYPE, raw=False)
    ff = _mm("ff2", act, w_ff2, "nn", bias=p['b_ff2'])

    def loss_fn(tv, bv):
        h1_t, ff_t, tgt = tv
        val, grads = jax.value_and_grad(lambda a0_, a1_, b0_, b1_, b2_: _seg_loss([a0_, a1_, tgt], [b0_, b1_, b2_]), argnums=(0, 1, 2, 3, 4))(h1_t, ff_t, *bv)
        return [grads[0], grads[1]], [grads[2], grads[3], grads[4], jnp.sum(grads[1], axis=0, keepdims=True), jnp.full((1, 128), val, F32)]

    (dh1_a, dff), (d_gt2, d_ln2_g, d_ln2_b, d_b_ff2, loss_row) = _rowwise(
        "loss", loss_fn, [h1, ff, target], [gt2, p['ln2_g'], p['ln2_b']], [(D_MODEL, F32), (D_MODEL, _MXU_DTYPE)], [(1, D_MODEL)] * 4 + [(1, 128)], tile)

    g = {}
    g['ln2_g'], g['ln2_b'], g['b_ff2'] = d_ln2_g, d_ln2_b, d_b_ff2
    gw = {}
    gw['w_ff2'] = _mm("g_w_ff2", act, dff, "tn")
    df1 = _mm("d_act", dff, w_ff2, "nt", beside=act, epi=lambda d_, a_: d_ * (2.0 * jnp.sqrt(a_.astype(F32))), epi_dtype=_MXU_DTYPE)
    g['b_ff1'] = _mm("g_b_ff1", jnp.ones((16, T), _MXU_DTYPE), df1, "nn")[0:1]
    gw['w_ff1'] = _mm("g_w_ff1", df1, h2in, "tn")
    dh2in = _mm("d_h2in", df1, w_ff1_t, "nn")
    (dx_a, dmix), (d_gt1, g['ln1_g'], g['ln1_b'], d_sc2, d_sh2), (g['b_out'],) = _rowwise_vjp(
        "mid_bwd", _seg_mid, [x, mix], mid_b, [dh1_a, dh2in], tile, [0, 1], [0, 1, 2, 3, 4], t_dtypes=[F32, _MXU_DTYPE], colsum=[1])
    gw['w_out'] = _mm("g_w_out", merged, dmix, "tn")
    dmerged = _mm("d_merged", dmix, w_out, "nt", out_dtype=_MXU_DTYPE)
    (dproj, dpa, dpb), _, (cs_gates,) = _rowwise_vjp(
        "merge_bwd", _seg_merge, gates + [pa, pb], [], [dmerged], tile, [0, 1, 2, 3], [], t_dtypes=[_MXU_DTYPE] * 3, colsum=[0],
        finish=lambda dts, sc: [jnp.concatenate(dts[:2], axis=1), dts[2], dts[3]], out_widths=[2 * D_MODEL, D_MODEL, D_MODEL],
        into={0: (lax.empty((T, P_COLS), _MXU_DTYPE), 0)})
    gw['w_branch_a'] = _mm("g_w_branch_a", dpa, ya, "tn")
    gw['w_branch_b'] = _mm("g_w_branch_b", dpb, yb, "tn")
    dya = _mm("d_ya", dpa, w_ba_t, "nn")
    dyb = _mm("d_yb", dpb, w_bb_t, "nn")
    def send_rows(names):
        parts = []
        for n in names:
            per_dev = gw[n].reshape(_N_DEV, -1, D_MODEL)
            parts.append(jnp.pad(per_dev, ((0, 0), (0, _BIG[n][1] - per_dev.shape[1]), (0, 0))))
        return jnp.concatenate(parts, axis=1) if len(parts) > 1 else parts[0]

    send_early = send_rows(big_names[:-1])
    (dy, dr1, dk1, dv1, dg_), (d_r_k, g['gn_gain'], g['gn_bias']), _, sibling_early = _rowwise_vjp(
        "rwkv_post_bwd", _seg_rwkv_post, [y_, r_, k2_, v_, g_], post_b, [dyb], tile, [0, 1, 2, 3, 4], [0, 1, 2],
        rider=(send_early, _sibling_def, (0.0, 1.0)))
    g['r_k'] = d_r_k
    partials_early = _chip_partials("chip_partials_early", send_early, sibling_early, 128, _COMM_DTYPE)
    dr2, dlw, dk2, dv2, da, db, landed_early = _scan_bwd(*scan_in, states, inverses, dy, rider=(partials_early, _chips_def, (0.0, 1.0)))
    pre_tile = 128
    last_step = T // pre_tile - 1

    def pre_prep(prim):
        z_t, halo_t = prim
        return [z_t, _shift_down(z_t, halo_t, pl.program_id(0) == last_step)]

    def pre_finish(dts, sc):
        dz_direct, dprev = dts
        (row_after,) = sc

        @pl.when(pl.program_id(0) == 0)
        def _():
            row_after[...] = jnp.zeros_like(row_after)

        dz = dz_direct + _shift_up(dprev, row_after[...])
        row_after[...] = dprev[:_HALO]
        return [dz]

    (dproj,), (d_mu, g['w0'], d_wd, g['a0'], d_wa, d_wg, g['k_k'], g['k_a']), (cs_rw,) = _rowwise_vjp(
        "rwkv_pre_bwd", _seg_rwkv_pre, [z_rw, z_rw_halo], pre_b, [[dr1, dr2], dlw, [dk1, dk2], [dv1, dv2], da, db, dg_], pre_tile,
        [0, 1], [0, 1, 2, 3, 4, 5, 6, 7], t_dtypes=[_MXU_DTYPE], colsum=[0], prep=pre_prep, finish=pre_finish,
        out_widths=[RW_COLS], reverse=True, scratch=[(_HALO, RW_COLS)], into={0: (dproj, 1)})
    g['mu_shift'] = d_mu[:, :RW_USED]
    for n, d_ in (('w_decay_up', d_wd), ('w_aaa_up', d_wa), ('w_gate_up', d_wg)):
        r0, nr = _LORA[n]
        g[n] = d_[r0:r0 + nr]
    (dproj,), (g['g_ln_v'], g['b_ln_v'], g['w_spatial'], d_b_tg), (cs_g,) = _rowwise_vjp(
        "gmlp_bwd", _seg_gmlp, [z_gmlp], gmlp_b, [dya], G_CHUNK, [0], [0, 1, 2, 3], t_dtypes=[_MXU_DTYPE], colsum=[0],
        into={0: (dproj, 4)})
    g['b_spatial'] = d_b_tg[:, :8].T
    g['b_in'] = _from_padded(jnp.concatenate([cs_gates, cs_rw, cs_g], axis=1), 1)
    small_names = [n for n in _WEIGHTS if n not in _BIG and n not in ('w_ada', 'b_ada')]
    packed_g, g_places = _pack_rows([g[n] for n in small_names] + [loss_row], row_mult=256)
    gw_in_t, small_all = _mm("g_w_in", dproj, h, "tn", rider=(packed_g, _gather_def, (0.0, 0.6, 1.0)))
    gw['w_in'] = _from_padded(gw_in_t, 0)
    send_late = send_rows(big_names[-1:])
    sibling_late = _exchange("pair_exchange_late", send_late, _sibling_def)
    partials_late = _chip_partials("chip_partials_late", send_late, sibling_late, 128, _COMM_DTYPE)
    dh, landed_late = _mm("d_h", dproj, w_in_t, "nn", rider=(partials_late, _chips_def, (0.0, 1.0)))

    def mod1_bwd(tv, bv):
        x_t, dh_t, dxa_t = tv
        (sc,) = bv
        return [dxa_t + dh_t * (1.0 + sc)], [jnp.sum(dh_t * x_t, axis=0, keepdims=True), jnp.sum(dh_t, axis=0, keepdims=True)]

    (grad_x,), (d_sc1, d_sh1) = _rowwise("modulate1_bwd", mod1_bwd, [x, dh, dx_a], [sc1], [(D_MODEL, F32)], [(1, D_MODEL)] * 2, tile)

    dmod = jnp.concatenate([d_sh1, d_sc1, d_gt1, d_sh2, d_sc2, d_gt2], axis=1).reshape(6 * D_MODEL // 128, 128)
    dmod_all = _all_gather("gather_dmod", dmod)
    g['b_ada'] = _sum_leading("sum_dmod", dmod_all).reshape(1, 6 * D_MODEL)
    dmod_mine = lax.dynamic_slice(dmod_all.reshape(_N_DEV, 6 * D_MODEL), (0, me * 768), (_N_DEV, 768))
    g_w_ada = _mm("g_w_ada", c_act_all, _pad_rows(dmod_mine, 16), "tn")

    small_sum = _unpack_rows(_sum_leading("sum_small", small_all), g_places)
    loss = small_sum.pop()[0, 0]
    for n, t in zip(small_names, small_sum, strict=True):
        g[n] = t
    for n in _LORA:
        g[n] = lax.dynamic_slice(g[n], (0, me * R_HEAD), (g[n].shape[0], R_HEAD))
    g['w_ada'] = g_w_ada

    summed = jnp.concatenate([_sum_leading("sum_early", landed_early), _sum_leading("sum_late", landed_late)], axis=0)
    g.update(_unpack_big(summed, p))

    delta, new_m, new_v = {}, {}, {}
    own_call = ['w_ada'] + big_names
    for n in own_call:
        if n == 'w_in':
            r0 = _BIG[n][0]
            outs_t = _adamw("adamw_" + n, p[n].T, summed[r0:r0 + IN_SHARD], m[n].T, v[n].T)
            delta[n], new_m[n], new_v[n] = [o.T for o in outs_t]
        else:
            delta[n], new_m[n], new_v[n] = _adamw("adamw_" + n, p[n], g[n], m[n], v[n])
    rest = [n for n in _WEIGHTS if n not in own_call]
    outs = _adamw_many("adamw_rest", *[[d[n].reshape(p[n].shape) for n in rest] for d in (p, g, m, v)])
    for d, o in zip((delta, new_m, new_v), outs, strict=True):
        d.update(zip(rest, o, strict=True))
    return loss, grad_x, g, delta, new_m, new_v


def kernel(x, c, w_ada, b_ada, w_in, b_in, g_ln_v, b_ln_v, w_spatial, b_spatial, mu_shift, w0, w_decay_up, a0, w_aaa_up, w_gate_up, k_k, k_a, r_k, gn_gain, gn_bias, w_branch_a, w_branch_b, w_out, b_out, ln1_g, ln1_b, w_ff1, b_ff1, w_ff2, b_ff2, ln2_g, ln2_b, loss_target, m_w_ada, m_b_ada, m_w_in, m_b_in, m_g_ln_v, m_b_ln_v, m_w_spatial, m_b_spatial, m_mu_shift, m_w0, m_w_decay_up, m_a0, m_w_aaa_up, m_w_gate_up, m_k_k, m_k_a, m_r_k, m_gn_gain, m_gn_bias, m_w_branch_a, m_w_branch_b, m_w_out, m_b_out, m_ln1_g, m_ln1_b, m_w_ff1, m_b_ff1, m_w_ff2, m_b_ff2, m_ln2_g, m_ln2_b, v_w_ada, v_b_ada, v_w_in, v_b_in, v_g_ln_v, v_b_ln_v, v_w_spatial, v_b_spatial, v_mu_shift, v_w0, v_w_decay_up, v_a0, v_w_aaa_up, v_w_gate_up, v_k_k, v_k_a, v_r_k, v_gn_gain, v_gn_bias, v_w_branch_a, v_w_branch_b, v_w_out, v_b_out, v_ln1_g, v_ln1_b, v_w_ff1, v_b_ff1, v_w_ff2, v_b_ff2, v_ln2_g, v_ln2_b):
    given = dict(locals())
    shapes = {n: given[n].shape for n in _WEIGHTS}
    def two_d(a_):
        a_ = a_[0]
        return a_.reshape(1, -1) if a_.ndim == 1 else a_
    p = {n: two_d(given[n]) for n in _WEIGHTS}
    m = {n: two_d(given["m_" + n]) for n in _WEIGHTS}
    v = {n: two_d(given["v_" + n]) for n in _WEIGHTS}
    loss, grad_x, g, delta, new_m, new_v = _step(p, m, v, x[0], c, loss_target[0])
    outs = [loss, grad_x[None]]
    for d in (g, delta, new_m, new_v):
        outs += [d[n].reshape(shapes[n]) for n in _WEIGHTS]
    return tuple(outs)
```

```python
import functools

import jax
import jax.numpy as jnp
from jax import lax
from jax.experimental import pallas as pl
from jax.experimental.pallas import tpu as pltpu

F32 = jnp.float32
_MXU_DTYPE = jnp.bfloat16
_HI = lax.Precision.HIGHEST
_VMEM_LIMIT = 48 * 1024 * 1024
_MESH_ID = pl.DeviceIdType.MESH
_N_DEV = 8

D_MODEL = 1024
G_WIDTH = 512
G_CHUNK = 128
R_WIDTH = 512
R_HEADS = 8
R_HEAD = 64
LORA_W, LORA_A, LORA_G = 32, 32, 96
D_FF = 4096
ALPHA = 2.0 ** 0.25
LN_EPS = 1e-5
GN_EPS = 64e-5
SCAN_CHUNK = 64
SCAN_PER_STEP = 4
ADAM_LR, ADAM_B1, ADAM_B2, ADAM_EPS, ADAM_WD, ADAM_STEP = 0.001, 0.9, 0.999, 1e-08, 0.01, 10

P_COLS = 5120
RW_COLS = 2048
RW_USED = 3 * R_WIDTH + LORA_W + LORA_A + LORA_G
LORA_PAD = 256
IN_COLS = 2 * G_WIDTH + RW_USED + 2 * D_MODEL


def _cparams(sem=None, **kw):
    if sem is not None:
        kw["dimension_semantics"] = sem
    return pltpu.CompilerParams(vmem_limit_bytes=_VMEM_LIMIT, **kw)


def _dot(a, b, dims=(((1,), (0,)), ((), ())), hi=False):
    if hi:
        return lax.dot_general(a.astype(F32), b.astype(F32), dims, precision=_HI, preferred_element_type=F32)
    return lax.dot_general(a.astype(_MXU_DTYPE), b.astype(_MXU_DTYPE), dims, preferred_element_type=F32)


_NN = (((1,), (0,)), ((), ()))
_NT = (((1,), (1,)), ((), ()))
_TN = (((0,), (0,)), ((), ()))


def _pick(n, pref):
    for t in pref:
        if n % t == 0:
            return t
    return n


def _mm(name, a, b, mode, bias=None, out_dtype=F32, epi=None, epi_dtype=None, raw=True, beside=None, colsum=False, rider=None,
        tm=None, tn=None, tk=None):
    if mode == "nn":
        (M, K), (_, N) = a.shape, b.shape
    elif mode == "nt":
        (M, K), (N, _) = a.shape, b.shape
    else:
        (K, M), (_, N) = a.shape, b.shape
    tm = tm or _pick(M, (2048, 1280, 1024, 512, 256, 128, 64, 32, 16, 8))
    tn = tn or _pick(N, (1024, 512, 640, 384, 256, 128))
    tk = tk or _pick(K, (1024, 512, 256, 128))
    nk = K // tk
    dims = {"nn": _NN, "nt": _NT, "tn": _TN}[mode]
    a_spec = pl.BlockSpec((tk, tm), lambda i, j, k: (k, i)) if mode == "tn" else pl.BlockSpec((tm, tk), lambda i, j, k: (i, k))
    b_spec = pl.BlockSpec((tn, tk), lambda i, j, k: (j, k)) if mode == "nt" else pl.BlockSpec((tk, tn), lambda i, j, k: (k, j))
    o_spec = pl.BlockSpec((tm, tn), lambda i, j, k: (i, j))
    has_bias, has_beside = bias is not None, beside is not None
    two = epi is not None and not has_beside and raw
    only_epi = epi is not None and not has_beside and not raw
    grid = (M // tm, N // tn, nk)
    ride_shape, ride_sems, _ = rider[1](rider[0]) if rider else (None, [], None)

    def body(*refs):
        refs = list(refs)
        n_in = 2 + has_bias + has_beside
        if rider:
            sem_refs = [refs.pop() for _ in ride_sems][::-1]
            ride_out = refs.pop(n_in + 1 + 1 + two)
            ride_in = refs.pop(n_in)
            step = (pl.program_id(0) * grid[1] + pl.program_id(1)) * grid[2] + pl.program_id(2)
            _ride(rider, step, grid[0] * grid[1] * grid[2], ride_in, ride_out, sem_refs)
        a_ref, b_ref = refs[0], refs[1]
        bias_ref = refs[2] if has_bias else None
        beside_ref = refs[n_in - 1] if has_beside else None
        outs = refs[n_in:]
        o_ref, acc_ref = outs[0], outs[-1]
        k = pl.program_id(2)

        @pl.when(k == 0)
        def _():
            acc_ref[...] = jnp.zeros_like(acc_ref)

        acc_ref[...] += _dot(a_ref[...], b_ref[...], dims)

        @pl.when(k == nk - 1)
        def _():
            res = acc_ref[...]
            if has_bias:
                res = res + bias_ref[...]
            if has_beside:
                val = epi(res, beside_ref[...])
                o_ref[...] = val.astype(o_ref.dtype)
                if colsum:
                    outs[1][...] = jnp.broadcast_to(jnp.sum(val, axis=0, keepdims=True), outs[1].shape)
            elif only_epi:
                o_ref[...] = epi(res).astype(o_ref.dtype)
            else:
                o_ref[...] = res.astype(o_ref.dtype)
                if two:
                    outs[1][...] = epi(res).astype(outs[1].dtype)

    in_specs = [a_spec, b_spec]
    args = [a, b]
    if has_bias:
        in_specs.append(pl.BlockSpec((1, tn), lambda i, j, k: (0, j)))
        args.append(bias)
    if has_beside:
        in_specs.append(o_spec)
        args.append(beside)
    out_shape = [jax.ShapeDtypeStruct((M, N), epi_dtype if (has_beside or only_epi) else out_dtype)]
    out_specs = [o_spec]
    if two:
        out_shape.append(jax.ShapeDtypeStruct((M, N), epi_dtype))
        out_specs.append(o_spec)
    if colsum:
        assert has_beside and not rider
        out_shape.append(jax.ShapeDtypeStruct((grid[0] * 8, N), F32))
        out_specs.append(pl.BlockSpec((8, tn), lambda i, j, k: (i, j)))
    if rider:
        in_specs.append(_HBM)
        args.append(rider[0])
        out_shape.append(ride_shape)
        out_specs.append(_HBM)
    res = pl.pallas_call(
        body, name=name, grid=grid, in_specs=in_specs, out_specs=out_specs, out_shape=out_shape,
        scratch_shapes=[pltpu.VMEM((tm, tn), F32)] + list(ride_sems),
        compiler_params=_cparams(("arbitrary",) * 3 if rider else ("parallel", "parallel", "arbitrary")),
    )(*args)
    if colsum:
        return res[0], _sum_leading(name + "_colsum", res[1].reshape(grid[0], 8, N))[0:1]
    return res if (two or rider) else res[0]


_HALO = 8


def _rowwise(name, fn, tiled, bcast, tiled_out, red_out, tile, reverse=False, scratch=(), rider=None):
    tiled = [t if isinstance(t, tuple) else (t, t.shape[1], 0) for t in tiled]
    T = next(t[0] for t in tiled if not isinstance(t[0], str)).shape[0]
    n = T // tile
    nt, nb, nto, nsc, nro = len(tiled), len(bcast), len(tiled_out), len(scratch), len(red_out)
    ride_shape, ride_sems, _ = rider[1](rider[0]) if rider else (None, [], None)
    nr = 1 if rider else 0
    into = [(j, t[2], t[3]) for j, t in enumerate(tiled_out) if len(t) == 4]
    na = len(into)

    def row_block(i):
        return n - 1 - i if reverse else i

    def body(*refs):
        i2 = nt + nb
        o0 = i2 + nr + na
        o1, o2 = o0 + nto, o0 + nto + nro
        s0 = o2 + nr
        if rider:
            _ride(rider, pl.program_id(0), n, refs[i2], refs[o2], refs[s0 + nsc:])
        t_refs, b_refs = refs[:nt], refs[nt:i2]
        to_refs, ro_refs = refs[o0:o1], refs[o1:o2]
        extra = (list(refs[s0:s0 + nsc]),) if nsc else ()
        touts, routs = fn([r[...] for r in t_refs], [r[...] for r in b_refs], *extra)
        for r, v in zip(to_refs, touts, strict=True):
            r[...] = v.astype(r.dtype)
        if ro_refs:
            i = pl.program_id(0)

            @pl.when(i == 0)
            def _():
                for r, v in zip(ro_refs, routs, strict=True):
                    r[...] = v.astype(F32)

            @pl.when(i > 0)
            def _():
                for r, v in zip(ro_refs, routs, strict=True):
                    r[...] += v.astype(F32)

    def whole(shape):
        nd = len(shape)
        return pl.BlockSpec(tuple(shape), lambda i: (0,) * nd)

    per_tile = tile // _HALO
    in_specs, arrays = [], []
    for t in tiled:
        if isinstance(t[0], str):
            _, arr, w, cb = t
            in_specs.append(pl.BlockSpec((_HALO, w), functools.partial(lambda i, cb: (jnp.maximum(row_block(i) * per_tile - 1, 0), cb), cb=cb)))
        else:
            arr, w, cb = t
            in_specs.append(pl.BlockSpec((tile, w), functools.partial(lambda i, cb: (row_block(i), cb), cb=cb)))
        arrays.append(arr)
    in_specs += [whole(b.shape) for b in bcast]
    out_specs, out_shape = [], []
    for t in tiled_out:
        cb = t[3] if len(t) == 4 else 0
        out_specs.append(pl.BlockSpec((tile, t[0]), functools.partial(lambda i, cb: (row_block(i), cb), cb=cb)))
        out_shape.append(jax.ShapeDtypeStruct(t[2].shape if len(t) == 4 else (T, t[0]), t[1]))
    out_specs += [whole(s) for s in red_out]
    out_shape += [jax.ShapeDtypeStruct(tuple(s), F32) for s in red_out]
    ride_args = []
    if rider:
        in_specs.append(_HBM)
        out_specs.append(_HBM)
        out_shape.append(ride_shape)
        ride_args = [rider[0]]
    in_specs += [pl.BlockSpec(memory_space=pl.ANY)] * na
    aliases = {nt + nb + nr + k: j for k, (j, _, _) in enumerate(into)}
    res = pl.pallas_call(
        body, name=name, grid=(n,), in_specs=in_specs, out_specs=out_specs, out_shape=out_shape,
        scratch_shapes=[pltpu.VMEM(tuple(s), F32) for s in scratch] + list(ride_sems),
        input_output_aliases=aliases, compiler_params=_cparams(("arbitrary",)),
    )(*arrays, *bcast, *ride_args, *[buf for (_, buf, _) in into])
    if rider:
        return list(res[:nto]), list(res[nto:nto + nro]), res[-1]
    return list(res[:nto]), list(res[nto:])


def _rowwise_vjp(name, f, tiled, bcast, cts, tile, wrt_t, wrt_b, t_dtypes=None, colsum=(), prep=None, finish=None,
                 out_widths=None, reverse=False, scratch=(), rider=None, into=None):
    tiled = [t if isinstance(t, tuple) else (t, t.shape[1], 0) for t in tiled]
    npr = len(tiled)
    t_dtypes = t_dtypes or [F32] * len(wrt_t)
    groups = [c if isinstance(c, list) else [c] for c in cts]
    cts = [a_ for grp in groups for a_ in grp]

    def fn(tv, bv, sc=None):
        prim, flat_ct = tv[:npr], list(tv[npr:])
        if prep is not None:
            prim = prep(prim)
        ct = []
        for grp in groups:
            parts = [flat_ct.pop(0).astype(F32) for _ in grp]
            ct.append(functools.reduce(lambda p_, q_: p_ + q_, parts))

        def g(dt_vals, db_vals):
            full_t, full_b = list(prim), list(bv)
            for i, v in zip(wrt_t, dt_vals, strict=True):
                full_t[i] = v
            for j, v in zip(wrt_b, db_vals, strict=True):
                full_b[j] = v
            return f(full_t, full_b)

        outs, pull = jax.vjp(g, [prim[i].astype(F32) for i in wrt_t], [bv[j] for j in wrt_b])
        dts, dbs = pull([c.astype(o.dtype) for c, o in zip(ct, outs, strict=True)])
        if finish is not None:
            dts = finish(dts, sc)
        sums = [jnp.sum(dts[i].astype(F32), axis=0, keepdims=True) for i in colsum]
        return dts, list(dbs) + sums

    widths = out_widths or [tiled[i][1] for i in wrt_t]
    tiled_out = [(w, dt) + tuple((into or {}).get(j, ())) for j, (w, dt) in enumerate(zip(widths, t_dtypes, strict=True))]
    red_out = [bcast[j].shape for j in wrt_b] + [(1, widths[i]) for i in colsum]
    res = _rowwise(name, fn, tiled + list(cts), bcast, tiled_out, red_out, tile, reverse=reverse, scratch=scratch, rider=rider)
    dts, reds = res[0], res[1]
    nb = len(wrt_b)
    return (dts, reds[:nb], reds[nb:]) + tuple(res[2:])


def _layer_norm(x, g, b, eps):
    mu = jnp.mean(x, axis=-1, keepdims=True)
    xc = x - mu
    var = jnp.mean(xc * xc, axis=-1, keepdims=True)
    return xc * lax.rsqrt(var + eps) * g + b


def _gelu_tanh(x):
    return 0.5 * x * (1.0 + jnp.tanh(0.7978845608028654 * (x + 0.044715 * (x * x * x))))


def _sigmoid(x):
    return 1.0 / (1.0 + jnp.exp(-x))


def _seg_modulate(tv, bv):
    (x,), (sc, sh) = tv, bv
    return [x * (1.0 + sc) + sh]


def _seg_gmlp(tv, bv):
    (z,), (g_ln, b_ln, ws, b_tg, expand) = tv, bv
    bias_full = _dot(b_tg, expand, hi=True)
    zz = _gelu_tanh(z)
    u, v = zz[:, :G_WIDTH], zz[:, G_WIDTH:]
    v = _layer_norm(v, g_ln, b_ln, LN_EPS)
    row = lax.broadcasted_iota(jnp.int32, (G_CHUNK, G_CHUNK), 0)
    col = lax.broadcasted_iota(jnp.int32, (G_CHUNK, G_CHUNK), 1)
    causal = col <= row
    first_group = lax.broadcasted_iota(jnp.int32, (G_CHUNK, 128), 1) < 64
    parts = []
    for p in range(4):
        vp = v[:, 128 * p:128 * (p + 1)]
        s_even = _dot(jnp.where(causal, ws[2 * p], 0.0), vp)
        s_odd = _dot(jnp.where(causal, ws[2 * p + 1], 0.0), vp)
        parts.append(jnp.where(first_group, s_even, s_odd))
    s = jnp.concatenate(parts, axis=1) + bias_full
    return [u * s]


def _split2(x):
    hi = x.astype(_MXU_DTYPE)
    return hi, (x - hi.astype(F32)).astype(_MXU_DTYPE)


@jax.custom_vjp
def _group_sum(x, ones_blocks):
    hi, lo = _split2(x)
    return _dot(hi, ones_blocks) + _dot(lo, ones_blocks)


def _group_sum_fwd(x, ones_blocks):
    return _group_sum(x, ones_blocks), ones_blocks


def _group_sum_bwd(ones_blocks, ct):
    return _group_sum(ct, ones_blocks), jnp.zeros_like(ones_blocks)


_group_sum.defvjp(_group_sum_fwd, _group_sum_bwd)


def _shift_down(z, halo, is_first):
    _, W = z.shape
    rolled = pltpu.roll(z, 1, 0)
    before = jnp.where(is_first, 0.0, pltpu.roll(halo, 1, 0))
    top_row = lax.broadcasted_iota(jnp.int32, (_HALO, W), 0) == 0
    return jnp.concatenate([jnp.where(top_row, before, rolled[:_HALO]), rolled[_HALO:]], axis=0)


def _shift_up(d, after):
    tile, W = d.shape
    rolled = pltpu.roll(d, tile - 1, 0)
    last_row = lax.broadcasted_iota(jnp.int32, (_HALO, W), 0) == _HALO - 1
    bottom = jnp.where(last_row, pltpu.roll(after, _HALO - 1, 0), rolled[tile - _HALO:])
    return jnp.concatenate([rolled[:tile - _HALO], bottom], axis=0)


def _seg_rwkv_pre(tv, bv):
    (z, prev), (mu, w0, wd, a0, wa, wg, k_k, k_a, gsum) = tv, bv
    zs = z + (prev - z) * mu
    r, k, v = zs[:, 0:512], zs[:, 512:1024], zs[:, 1024:1536]
    zl = zs[:, 3 * R_WIDTH:3 * R_WIDTH + LORA_PAD]
    x = w0 + _dot(jnp.tanh(zl), wd)
    softplus = jnp.maximum(-x, 0.0) + jnp.log(1.0 + jnp.exp(-jnp.abs(x)))
    lw = -jnp.exp(-softplus - 0.5)
    a = _sigmoid(a0 + _dot(zl, wa))
    g = _dot(_sigmoid(zl), wg)
    kk = k * k_k
    nrm = jnp.sqrt(_group_sum(kk * kk, gsum))
    kk = kk / jnp.maximum(nrm, 1e-12)
    k2 = k * (1.0 + (a - 1.0) * k_a)
    return [r, lw, k2, v, -kk, kk * a, g]


def _seg_rwkv_post(tv, bv):
    (y, r, k2, v, g), (r_k, gain, bias, gsum) = tv, bv
    mu = _group_sum(y, gsum) * (1.0 / R_HEAD)
    yc = y - mu
    var = _group_sum(yc * yc, gsum) * (1.0 / R_HEAD)
    yn = yc * lax.rsqrt(var + GN_EPS) * gain + bias
    bonus = _group_sum(r * k2 * r_k, gsum) * v
    return [(yn + bonus) * g]


def _seg_merge(tv, bv):
    (ga, gb, pa, pb), () = tv, bv
    return [_sigmoid(ga) * pa + _sigmoid(gb) * pb]


def _seg_mid(tv, bv):
    (x, mix), (gt1, g1, b1, sc2, sh2) = tv, bv
    h1 = _layer_norm(ALPHA * x + gt1 * mix, g1, b1, LN_EPS)
    return [h1, h1 * (1.0 + sc2) + sh2]


def _seg_relu2(tv, bv):
    (f1,), () = tv, bv
    return [jnp.square(jnp.maximum(f1, 0.0))]


def _seg_loss(tv, bv):
    (h1, ff, target), (gt2, g2, b2) = tv, bv
    out = _layer_norm(ALPHA * h1 + gt2 * ff, g2, b2, LN_EPS)
    err = jnp.square(out - target)
    return 0.5 * jnp.sum(jnp.mean(err, axis=-1))


_BNN = (((2,), (1,)), ((0,), (0,)))
_BNT = (((2,), (2,)), ((0,), (0,)))
_BTN = (((1,), (1,)), ((0,), (0,)))


def _tri_dot(x, dims):
    L = x.shape[0]
    tri = (lax.broadcasted_iota(jnp.int32, (L, L), 1) <= lax.broadcasted_iota(jnp.int32, (L, L), 0)).astype(F32)
    hi, lo = _split2(x)
    return _dot(tri, hi, dims) + _dot(tri, lo, dims)


@jax.custom_vjp
def _running_sum(x):
    return _tri_dot(x, _NN)


_running_sum.defvjp(lambda x: (_tri_dot(x, _NN), None), lambda _, ct: (_tri_dot(ct, _TN),))


def _cut_heads(x):
    return jnp.stack([x[:, R_HEAD * h:R_HEAD * (h + 1)] for h in range(R_HEADS)])


def _join_heads(x):
    return jnp.concatenate([x[h] for h in range(R_HEADS)], axis=1)


@jax.custom_vjp
def _split_heads(x):
    return _cut_heads(x)


@jax.custom_vjp
def _merge_heads(x):
    return _join_heads(x)


_split_heads.defvjp(lambda x: (_cut_heads(x), None), lambda _, ct: (_join_heads(ct),))
_merge_heads.defvjp(lambda x: (_join_heads(x), None), lambda _, ct: (_cut_heads(ct),))


def _inverse_pullback(inv, ct):
    return _dot(_dot(inv, ct, _BTN), inv, _BNT)


@jax.custom_vjp
def _unit_lower_inverse(n_mat):
    H, L, _ = n_mat.shape
    eye = lax.broadcasted_iota(jnp.int32, (H, L, L), 1) == lax.broadcasted_iota(jnp.int32, (H, L, L), 2)
    inv = jnp.where(eye, 1.0, 0.0) + n_mat
    pw = n_mat
    n = 2
    while n < L:
        pw = _dot(pw, pw, _BNN)
        inv = inv + _dot(inv, pw, _BNN)
        n *= 2
    return inv


def _unit_lower_inverse_fwd(n_mat):
    inv = _unit_lower_inverse(n_mat)
    return inv, inv


_unit_lower_inverse.defvjp(_unit_lower_inverse_fwd, lambda inv, ct: (_inverse_pullback(inv, ct),))


@jax.custom_vjp
def _known_inverse(n_mat, inv):
    return inv


_known_inverse.defvjp(lambda n_mat, inv: (inv, inv), lambda inv, ct: (_inverse_pullback(inv, ct), jnp.zeros_like(inv)))


def _scan_chunk(r, lw, k, v, a, b, s0, inv=None, with_inverse=False):
    L, H = r.shape[0], R_HEADS
    cs = _running_sum(lw)
    cs_end = cs[L - 1:L, :]
    p, p_inv, to_end = jnp.exp(cs), jnp.exp(-cs), jnp.exp(cs_end - cs)
    at, bt, kt, rt = [_split_heads(t) for t in (a * jnp.exp(cs - lw), b * p_inv, k * p_inv, r * p)]
    b_end, k_end, v = [_split_heads(t) for t in (b * to_end, k * to_end, v)]
    row = lax.broadcasted_iota(jnp.int32, (H, L, L), 1)
    col = lax.broadcasted_iota(jnp.int32, (H, L, L), 2)
    incl, strict = col <= row, col < row
    a_ab = jnp.where(strict, _dot(at, bt, _BNT), 0.0)
    a_ak = jnp.where(strict, _dot(at, kt, _BNT), 0.0)
    a_rb = jnp.where(incl, _dot(rt, bt, _BNT), 0.0)
    a_rk = jnp.where(incl, _dot(rt, kt, _BNT), 0.0)
    inv = _unit_lower_inverse(a_ab) if inv is None else _known_inverse(a_ab, inv)
    u = _dot(inv, _dot(at, s0, _BNT) + _dot(a_ak, v, _BNN), _BNN)
    y = _merge_heads(_dot(rt, s0, _BNT) + _dot(a_rb, u, _BNN) + _dot(a_rk, v, _BNN))
    s1 = s0 * _split_heads(jnp.exp(cs_end)) + _dot(u, b_end, _BTN) + _dot(v, k_end, _BTN)
    return (y, s1, inv) if with_inverse else (y, s1)


def _scan_fwd(r, lw, k, v, a, b, rider):
    T = r.shape[0]
    H, N, L, P = R_HEADS, R_HEAD, SCAN_CHUNK, SCAN_PER_STEP
    nc = T // L
    steps = nc // P
    ride_shape, ride_sems, _ = rider[1](rider[0])

    def body(r_ref, lw_ref, k_ref, v_ref, a_ref, b_ref, ride_in, y_ref, st_ref, inv_ref, ride_out, s_ref, *sem_refs):
        _ride(rider, pl.program_id(0), steps, ride_in, ride_out, sem_refs)

        @pl.when(pl.program_id(0) == 0)
        def _():
            s_ref[...] = jnp.zeros_like(s_ref)

        s0 = s_ref[...]
        for j in range(P):
            rows = pl.ds(j * L, L)
            st_ref[j] = s0
            y, s0, inv = _scan_chunk(*[t[rows, :] for t in (r_ref, lw_ref, k_ref, v_ref, a_ref, b_ref)], s0, with_inverse=True)
            y_ref[rows, :] = y
            inv_ref[j] = inv
        s_ref[...] = s0

    blk = pl.BlockSpec((P * L, R_WIDTH), lambda c: (c, 0))
    per_chunk = pl.BlockSpec((P, H, N, N), lambda c: (c, 0, 0, 0))
    return pl.pallas_call(
        body, name="scan_fwd", grid=(steps,), in_specs=[blk] * 6 + [_HBM], out_specs=[blk, per_chunk, per_chunk, _HBM],
        out_shape=[jax.ShapeDtypeStruct((T, R_WIDTH), F32)] + [jax.ShapeDtypeStruct((nc, H, N, N), F32)] * 2 + [ride_shape],
        scratch_shapes=[pltpu.VMEM((H, N, N), F32)] + list(ride_sems),
        compiler_params=_cparams(("arbitrary",)),
    )(r, lw, k, v, a, b, rider[0])


def _scan_bwd(r, lw, k, v, a, b, states, inverses, dy, rider):
    T = r.shape[0]
    H, N, L, P = R_HEADS, R_HEAD, SCAN_CHUNK, SCAN_PER_STEP
    nc = T // L
    steps = nc // P
    ride_shape, ride_sems, _ = rider[1](rider[0])

    def body(r_ref, lw_ref, k_ref, v_ref, a_ref, b_ref, st_ref, inv_ref, dy_ref, ride_in,
             dr_ref, dlw_ref, dk_ref, dv_ref, da_ref, db_ref, ride_out, ds_ref, *sem_refs):
        _ride(rider, pl.program_id(0), steps, ride_in, ride_out, sem_refs)

        @pl.when(pl.program_id(0) == 0)
        def _():
            ds_ref[...] = jnp.zeros_like(ds_ref)

        ds = ds_ref[...]
        for j in reversed(range(P)):
            rows = pl.ds(j * L, L)
            args = [t[rows, :] for t in (r_ref, lw_ref, k_ref, v_ref, a_ref, b_ref)] + [st_ref[j]]
            inv = inv_ref[j]
            _, pull = jax.vjp(lambda *xs, inv=inv: _scan_chunk(*xs, inv=inv), *args)
            grads = pull((dy_ref[rows, :], ds))
            for o_ref, g_ in zip((dr_ref, dlw_ref, dk_ref, dv_ref, da_ref, db_ref), grads[:6], strict=True):
                o_ref[rows, :] = g_
            ds = grads[6]
        ds_ref[...] = ds

    blk = pl.BlockSpec((P * L, R_WIDTH), lambda c: (steps - 1 - c, 0))
    per_chunk = pl.BlockSpec((P, H, N, N), lambda c: (steps - 1 - c, 0, 0, 0))
    return pl.pallas_call(
        body, name="scan_bwd", grid=(steps,), in_specs=[blk] * 6 + [per_chunk, per_chunk, blk, _HBM], out_specs=[blk] * 6 + [_HBM],
        out_shape=[jax.ShapeDtypeStruct((T, R_WIDTH), F32)] * 6 + [ride_shape],
        scratch_shapes=[pltpu.VMEM((H, N, N), F32)] + list(ride_sems),
        compiler_params=_cparams(("arbitrary",)),
    )(r, lw, k, v, a, b, states, inverses, dy, rider[0])


def _place():
    x, y, c = lax.axis_index("x"), lax.axis_index("y"), lax.axis_index("c")
    return x, y, c


def _gather_def(block):
    R, C = block.shape

    def phases(x_ref, out_ref, send_sems, recv_sems, local_sem):
        x, y, c = _place()
        me, sibling = (x, y, c), (x, y, 1 - c)
        chips = [(1 - x, y), (x, 1 - y), (1 - x, 1 - y)]

        def slot(px, py, pc):
            return out_ref.at[4 * px + 2 * py + pc]

        def copy(k, blk, to, src=None):
            return pltpu.make_async_remote_copy(
                src_ref=slot(*blk) if src is None else src, dst_ref=slot(*blk),
                send_sem=send_sems.at[k], recv_sem=recv_sems.at[k], device_id=to, device_id_type=_MESH_ID)

        mine = pltpu.make_async_copy(x_ref, slot(*me), local_sem)
        first = [copy(0, me, sibling, src=x_ref)]
        first += [copy(1 + j, me, (*chip, c), src=x_ref) for j, chip in enumerate(chips)]
        passed = [copy(4 + j, (*chip, c), sibling) for j, chip in enumerate(chips)]

        def begin():
            mine.start()
            for cp in first:
                cp.start()

        def forward():
            for j, chip in enumerate(chips):
                copy(1 + j, (*chip, c), me).wait_recv()
                passed[j].start()

        def finish():
            copy(0, sibling, me).wait_recv()
            for j, chip in enumerate(chips):
                copy(4 + j, (*chip, 1 - c), me).wait_recv()
            for cp in first + passed:
                cp.wait_send()
            mine.wait()

        return [begin, forward, finish]

    sems = [pltpu.SemaphoreType.DMA((7,)), pltpu.SemaphoreType.DMA((7,)), pltpu.SemaphoreType.DMA]
    return jax.ShapeDtypeStruct((_N_DEV, R, C), block.dtype), sems, phases


def _sibling_def(blocks):
    _, R, C = blocks.shape

    def phases(x_ref, out_ref, send_sems, recv_sems):
        x, y, c = _place()
        copies = [pltpu.make_async_remote_copy(
            src_ref=x_ref.at[2 * q + (1 - c)], dst_ref=out_ref.at[q], send_sem=send_sems.at[q], recv_sem=recv_sems.at[q],
            device_id=(x, y, 1 - c), device_id_type=_MESH_ID) for q in range(4)]

        def begin():
            for cp in copies:
                cp.start()

        def finish():
            for cp in copies:
                cp.wait()

        return [begin, finish]

    return jax.ShapeDtypeStruct((4, R, C), blocks.dtype), [pltpu.SemaphoreType.DMA((4,)), pltpu.SemaphoreType.DMA((4,))], phases


def _chips_def(partials):
    def phases(x_ref, out_ref, send_sems, recv_sems, local_sem):
        x, y, c = _place()
        my_chip = 2 * x + y
        mine = pltpu.make_async_copy(x_ref.at[my_chip], out_ref.at[my_chip], local_sem)
        copies = []
        for rel in range(1, 4):
            px, py = (1 - x if (rel >> 1) & 1 else x), (1 - y if rel & 1 else y)
            copies.append(pltpu.make_async_remote_copy(
                src_ref=x_ref.at[2 * px + py], dst_ref=out_ref.at[my_chip],
                send_sem=send_sems.at[rel - 1], recv_sem=recv_sems.at[rel - 1],
                device_id=(px, py, c), device_id_type=_MESH_ID))

        def begin():
            mine.start()
            for cp in copies:
                cp.start()

        def finish():
            for cp in copies:
                cp.wait()
            mine.wait()

        return [begin, finish]

    sems = [pltpu.SemaphoreType.DMA((3,)), pltpu.SemaphoreType.DMA((3,)), pltpu.SemaphoreType.DMA]
    return jax.ShapeDtypeStruct(partials.shape, partials.dtype), sems, phases


_HBM = pl.BlockSpec(memory_space=pltpu.HBM)


def _exchange(name, array, definition):
    out_shape, sems, phases = definition(array)

    def body(x_ref, out_ref, *sem_refs):
        for phase in phases(x_ref, out_ref, *sem_refs):
            phase()

    return pl.pallas_call(body, name=name, in_specs=[_HBM], out_specs=_HBM, out_shape=out_shape, scratch_shapes=sems)(array)


def _ride(rider, step, nsteps, x_ref, out_ref, sem_refs):
    array, definition, fractions = rider
    for phase, frac in zip(definition(array)[2](x_ref, out_ref, *sem_refs), fractions, strict=True):
        pl.when(step == min(int(frac * nsteps), nsteps - 1))(phase)


def _all_gather(name, block):
    return _exchange(name, block, _gather_def)


def _chip_partials(name, blocks, from_sibling, tile, out_dtype):
    _, R, C = blocks.shape

    def body(x_ref, s_ref, o_ref):
        c = lax.axis_index("c")
        for q in range(4):
            o_ref[q] = (x_ref[2 * q + c] + s_ref[q]).astype(o_ref.dtype)

    return pl.pallas_call(
        body, name=name, grid=(R // tile,),
        in_specs=[pl.BlockSpec((_N_DEV, tile, C), lambda i: (0, i, 0)), pl.BlockSpec((4, tile, C), lambda i: (0, i, 0))],
        out_specs=pl.BlockSpec((4, tile, C), lambda i: (0, i, 0)), out_shape=jax.ShapeDtypeStruct((4, R, C), out_dtype),
        compiler_params=_cparams(("parallel",)),
    )(blocks, from_sibling)


def _sum_leading(name, x, tile=None):
    n, R, C = x.shape
    tile = tile or _pick(R, (512, 256, 128, 64, 32, 16, 8))

    def body(x_ref, o_ref):
        acc = x_ref[0].astype(F32)
        for k in range(1, n):
            acc = acc + x_ref[k].astype(F32)
        o_ref[...] = acc

    return pl.pallas_call(
        body, name=name, grid=(R // tile,), in_specs=[pl.BlockSpec((n, tile, C), lambda i: (0, i, 0))],
        out_specs=pl.BlockSpec((tile, C), lambda i: (i, 0)), out_shape=jax.ShapeDtypeStruct((R, C), F32),
        compiler_params=_cparams(("parallel",)),
    )(x)


def _adamw_update(w_, g_, m_, v_):
    m2 = ADAM_B1 * m_ + (1.0 - ADAM_B1) * g_
    v2 = ADAM_B2 * v_ + (1.0 - ADAM_B2) * jnp.square(g_)
    m_hat = m2 / (1.0 - ADAM_B1 ** ADAM_STEP)
    v_hat = v2 / (1.0 - ADAM_B2 ** ADAM_STEP)
    delta = -ADAM_LR * (m_hat / (jnp.sqrt(v_hat) + ADAM_EPS) + ADAM_WD * w_)
    return delta, m2, v2


def _adamw(name, w, g, m, v):
    R, C = w.shape
    tile = _pick(R, (256, 128, 64, 32, 16, 8))
    outs, _ = _rowwise(name, lambda tv, bv: (list(_adamw_update(*tv)), []), [w, g, m, v], [], [(C, F32)] * 3, [], tile)
    return outs


def _adamw_many(name, ws, gs, ms, vs):
    n = len(ws)

    def body(*refs):
        ins, outs = refs[:4 * n], refs[4 * n:]
        for i in range(n):
            res = _adamw_update(ins[i][...], ins[n + i][...], ins[2 * n + i][...], ins[3 * n + i][...])
            for j in range(3):
                outs[j * n + i][...] = res[j]

    out_shape = [jax.ShapeDtypeStruct(w.shape, F32) for w in ws] * 3
    res = pl.pallas_call(body, name=name, out_shape=out_shape, compiler_params=_cparams())(*ws, *gs, *ms, *vs)
    return res[:n], res[n:2 * n], res[2 * n:]


def _pack_rows(arrs, lanes=128, row_mult=8):
    flat, places, off = [], [], 0
    for a_ in arrs:
        n = a_.size
        flat.append(a_.reshape(-1).astype(F32))
        places.append((off, n, a_.shape))
        off += n
    total = -(-off // (lanes * row_mult)) * (lanes * row_mult)
    if total > off:
        flat.append(jnp.zeros((total - off,), F32))
    return jnp.concatenate(flat).reshape(total // lanes, lanes), places


def _unpack_rows(packed, places):
    flat = packed.reshape(-1)
    return [flat[o:o + n].reshape(s) for (o, n, s) in places]


_WEIGHTS = ['w_ada', 'b_ada', 'w_in', 'b_in', 'g_ln_v', 'b_ln_v', 'w_spatial', 'b_spatial', 'mu_shift', 'w0', 'w_decay_up', 'a0',
            'w_aaa_up', 'w_gate_up', 'k_k', 'k_a', 'r_k', 'gn_gain', 'gn_bias', 'w_branch_a', 'w_branch_b', 'w_out', 'b_out',
            'ln1_g', 'ln1_b', 'w_ff1', 'b_ff1', 'w_ff2', 'b_ff2', 'ln2_g', 'ln2_b']
_BIG = {'w_ff1': (0, 512), 'w_ff2': (512, 512), 'w_out': (1024, 128), 'w_branch_a': (1152, 64), 'w_branch_b': (1216, 64), 'w_in': (1280, 640)}
_LATER_ROWS = 1280
_CUT_BY_COLS = ('w_ff1', 'w_in', 'w_branch_a', 'w_branch_b')
IN_SHARD = IN_COLS // _N_DEV
_LORA = {'w_decay_up': (0, LORA_W), 'w_aaa_up': (LORA_W, LORA_A), 'w_gate_up': (LORA_W + LORA_A, LORA_G)}
_COMM_DTYPE = jnp.bfloat16


def _pad_rows(a, rows):
    return jnp.pad(a, ((0, rows - a.shape[0]),) + ((0, 0),) * (a.ndim - 1))


def _pack_big(shards):
    blocks = []
    for n, (_, rows) in _BIG.items():
        a = shards[n].T if n in _CUT_BY_COLS else shards[n]
        blocks.append(_pad_rows(a.reshape(-1, D_MODEL), rows))
    return jnp.concatenate(blocks, axis=0)


def _unpack_big(block, like):
    out = {}
    for n, (r0, _) in _BIG.items():
        rr, cc = like[n].shape
        if n in _CUT_BY_COLS:
            out[n] = block[r0:r0 + rr * cc // D_MODEL].reshape(cc, rr).T
        else:
            out[n] = block[r0:r0 + rr]
    return out


def _to_padded(a, axis):
    g_end = 2 * G_WIDTH
    r_end = g_end + RW_USED
    take = lambda lo, hi: lax.slice_in_dim(a, lo, hi, axis=axis)
    zshape = list(a.shape)
    zshape[axis] = RW_COLS - RW_USED
    return jnp.concatenate([take(r_end, IN_COLS), take(g_end, r_end), jnp.zeros(zshape, a.dtype), take(0, g_end)], axis=axis)


def _from_padded(a, axis):
    take = lambda lo, hi: lax.slice_in_dim(a, lo, hi, axis=axis)
    return jnp.concatenate([take(2 * D_MODEL + RW_COLS, P_COLS), take(2 * D_MODEL, 2 * D_MODEL + RW_USED), take(0, 2 * D_MODEL)], axis=axis)


def _step(p, m, v, x, c, target):
    T = x.shape[0]
    xi, yi, ci = _place()
    me = 4 * xi + 2 * yi + ci
    tile = 256

    lane = jnp.arange(R_WIDTH)
    gsum = (lane[:, None] // R_HEAD == lane[None, :] // R_HEAD).astype(F32)
    expand = (jnp.arange(128)[:, None] == (lane[None, :] // (G_WIDTH // 8))).astype(F32)

    (c_act,), _ = _rowwise("silu_c", lambda tv, bv: ([tv[0] * _sigmoid(tv[0])], []), [c], [], [(D_MODEL, F32)], [], 1)
    small, places = _pack_rows([c_act, p['w_decay_up'], p['w_aaa_up'], p['w_gate_up']])
    small_all = _all_gather("gather_small", small)
    per_dev = [_unpack_rows(small_all[d], places) for d in range(_N_DEV)]
    c_act_all = _pad_rows(jnp.concatenate([pd[0] for pd in per_dev], axis=0), 16)
    lora_full = {n: jnp.concatenate([pd[i + 1] for pd in per_dev], axis=1) for i, n in enumerate(_LORA)}
    lora_pad = {n: jnp.zeros((LORA_PAD, R_WIDTH), F32).at[r0:r0 + nr].set(lora_full[n]) for n, (r0, nr) in _LORA.items()}

    big_names = list(_BIG)
    my_rows = _pack_big(p).astype(_MXU_DTYPE)
    b_in_p = _to_padded(p['b_in'], 1)
    mu_p = jnp.concatenate([p['mu_shift'], jnp.zeros((1, RW_COLS - RW_USED), F32)], axis=1)

    b_ada_mine = lax.dynamic_slice(p['b_ada'], (0, me * 768), (1, 768))
    mod_cols = _mm("ada_mod", c_act_all, p['w_ada'], "nn", bias=b_ada_mine)
    mod_all = _all_gather("gather_mod", mod_cols)
    mod = lax.dynamic_index_in_dim(mod_all, me, axis=1, keepdims=False).reshape(1, 6 * D_MODEL)
    sh1, sc1, gt1, sh2, sc2, gt2 = [mod[:, i * D_MODEL:(i + 1) * D_MODEL] for i in range(6)]

    (h,), _, w_in_all = _rowwise("modulate1", lambda tv, bv: (_seg_modulate(tv, bv), []), [x], [sc1, sh1], [(D_MODEL, _MXU_DTYPE)], [], tile,
                                 rider=(my_rows[_LATER_ROWS:], _gather_def, (0.0, 0.5, 1.0)))
    w_in_t = _to_padded(w_in_all[:, :IN_SHARD].reshape(IN_COLS, D_MODEL), 0)
    proj = _mm("in_proj", h, w_in_t, "nt", bias=b_in_p)
    ws = p['w_spatial']
    b_tg = jnp.zeros((G_CHUNK, 128), F32).at[:, :8].set(p['b_spatial'].T)
    gmlp_b = [p['g_ln_v'], p['b_ln_v'], ws, b_tg, expand]
    z_gmlp = (proj, 2 * G_WIDTH, 4)
    (ya,), _ = _rowwise("gmlp", lambda tv, bv: (_seg_gmlp(tv, bv), []), [z_gmlp], gmlp_b, [(G_WIDTH, F32)], [], G_CHUNK)
    z_rw = (proj, RW_COLS, 1)
    z_rw_halo = ("halo", proj, RW_COLS, 1)
    pre_b = [mu_p, p['w0'], lora_pad['w_decay_up'], p['a0'], lora_pad['w_aaa_up'], lora_pad['w_gate_up'], p['k_k'], p['k_a'], gsum]

    def pre_fwd(tv, bv):
        z_t, halo_t = tv
        return _seg_rwkv_pre([z_t, _shift_down(z_t, halo_t, pl.program_id(0) == 0)], bv), []

    pre_out, _ = _rowwise("rwkv_pre", pre_fwd, [z_rw, z_rw_halo], pre_b, [(R_WIDTH, F32)] * 7, [], tile)
    r_, lw_, k2_, v_, a_, b_, g_ = pre_out
    scan_in = (r_, lw_, k2_, v_, a_, b_)
    y_, states, inverses, later_all = _scan_fwd(*scan_in, rider=(my_rows[:_LATER_ROWS], _gather_def, (0.0, 0.875, 1.0)))

    def whole(n, rows):
        r0 = _BIG[n][0]
        return later_all[:, r0:r0 + rows].reshape(_N_DEV * rows, D_MODEL)

    w_ff1_t, w_ff2, w_out = whole('w_ff1', 512), whole('w_ff2', 512), whole('w_out', 128)
    w_ba_t = whole('w_branch_a', 64).reshape(D_MODEL, G_WIDTH)
    w_bb_t = whole('w_branch_b', 64).reshape(D_MODEL, R_WIDTH)
    post_b = [p['r_k'].reshape(1, R_WIDTH), p['gn_gain'], p['gn_bias'], gsum]
    (yb,), _ = _rowwise("rwkv_post", lambda tv, bv: (_seg_rwkv_post(tv, bv), []), [y_, r_, k2_, v_, g_], post_b, [(R_WIDTH, F32)], [], tile)
    pa = _mm("branch_a", ya, w_ba_t, "nt", out_dtype=_MXU_DTYPE)
    pb = _mm("branch_b", yb, w_bb_t, "nt", out_dtype=_MXU_DTYPE)
    gates = [(proj, D_MODEL, 0), (proj, D_MODEL, 1)]
    (merged,), _ = _rowwise("merge", lambda tv, bv: (_seg_merge(tv, bv), []), gates + [pa, pb], [], [(D_MODEL, _MXU_DTYPE)], [], tile)
    mix = _mm("out_proj", merged, w_out, "nn", bias=p['b_out'])
    mid_b = [gt1, p['ln1_g'], p['ln1_b'], sc2, sh2]
    (h1, h2in), _ = _rowwise("mid", lambda tv, bv: (_seg_mid(tv, bv), []), [x, mix], mid_b, [(D_MODEL, F32), (D_MODEL, _MXU_DTYPE)], [], tile)
    act = _mm("ff1", h2in, w_ff1_t, "nt", bias=p['b_ff1'], epi=lambda t: _seg_relu2([t], [])[0], epi_dtype=_MXU_DTYPE, raw=False)
    ff = _mm("ff2", act, w_ff2, "nn", bias=p['b_ff2'])

    def loss_fn(tv, bv):
        h1_t, ff_t, tgt = tv
        val, grads = jax.value_and_grad(lambda a0_, a1_, b0_, b1_, b2_: _seg_loss([a0_, a1_, tgt], [b0_, b1_, b2_]), argnums=(0, 1, 2, 3, 4))(h1_t, ff_t, *bv)
        return [grads[0], grads[1]], [grads[2], grads[3], grads[4], jnp.sum(grads[1], axis=0, keepdims=True), jnp.full((1, 128), val, F32)]

    (dh1_a, dff), (d_gt2, d_ln2_g, d_ln2_b, d_b_ff2, loss_row) = _rowwise(
        "loss", loss_fn, [h1, ff, target], [gt2, p['ln2_g'], p['ln2_b']], [(D_MODEL, F32), (D_MODEL, _MXU_DTYPE)], [(1, D_MODEL)] * 4 + [(1, 128)], tile)

    g = {}
    g['ln2_g'], g['ln2_b'], g['b_ff2'] = d_ln2_g, d_ln2_b, d_b_ff2
    gw = {}
    gw['w_ff2'] = _mm("g_w_ff2", act, dff, "tn")
    df1, g['b_ff1'] = _mm("d_act", dff, w_ff2, "nt", beside=act, epi=lambda d_, a_: d_ * (2.0 * jnp.sqrt(a_.astype(F32))),
                          epi_dtype=_MXU_DTYPE, colsum=True)
    gw['w_ff1'] = _mm("g_w_ff1", df1, h2in, "tn")
    dh2in = _mm("d_h2in", df1, w_ff1_t, "nn")
    (dx_a, dmix), (d_gt1, g['ln1_g'], g['ln1_b'], d_sc2, d_sh2), (g['b_out'],) = _rowwise_vjp(
        "mid_bwd", _seg_mid, [x, mix], mid_b, [dh1_a, dh2in], tile, [0, 1], [0, 1, 2, 3, 4], t_dtypes=[F32, _MXU_DTYPE], colsum=[1])
    gw['w_out'] = _mm("g_w_out", merged, dmix, "tn")
    dmerged = _mm("d_merged", dmix, w_out, "nt", out_dtype=_MXU_DTYPE)
    (dproj, dpa, dpb), _, (cs_gates,) = _rowwise_vjp(
        "merge_bwd", _seg_merge, gates + [pa, pb], [], [dmerged], tile, [0, 1, 2, 3], [], t_dtypes=[_MXU_DTYPE] * 3, colsum=[0],
        finish=lambda dts, sc: [jnp.concatenate(dts[:2], axis=1), dts[2], dts[3]], out_widths=[2 * D_MODEL, D_MODEL, D_MODEL],
        into={0: (lax.empty((T, P_COLS), _MXU_DTYPE), 0)})
    gw['w_branch_a'] = _mm("g_w_branch_a", dpa, ya, "tn")
    gw['w_branch_b'] = _mm("g_w_branch_b", dpb, yb, "tn")
    dya = _mm("d_ya", dpa, w_ba_t, "nn")
    dyb = _mm("d_yb", dpb, w_bb_t, "nn")
    def send_rows(names):
        parts = []
        for n in names:
            per_dev = gw[n].reshape(_N_DEV, -1, D_MODEL)
            parts.append(jnp.pad(per_dev, ((0, 0), (0, _BIG[n][1] - per_dev.shape[1]), (0, 0))))
        return jnp.concatenate(parts, axis=1) if len(parts) > 1 else parts[0]

    send_early = send_rows(big_names[:-1])
    (dy, dr1, dk1, dv1, dg_), (d_r_k, g['gn_gain'], g['gn_bias']), _, sibling_early = _rowwise_vjp(
        "rwkv_post_bwd", _seg_rwkv_post, [y_, r_, k2_, v_, g_], post_b, [dyb], tile, [0, 1, 2, 3, 4], [0, 1, 2],
        rider=(send_early, _sibling_def, (0.0, 1.0)))
    g['r_k'] = d_r_k
    partials_early = _chip_partials("chip_partials_early", send_early, sibling_early, 128, _COMM_DTYPE)
    dr2, dlw, dk2, dv2, da, db, landed_early = _scan_bwd(*scan_in, states, inverses, dy, rider=(partials_early, _chips_def, (0.0, 1.0)))
    pre_tile = 128
    last_step = T // pre_tile - 1

    def pre_prep(prim):
        z_t, halo_t = prim
        return [z_t, _shift_down(z_t, halo_t, pl.program_id(0) == last_step)]

    def pre_finish(dts, sc):
        dz_direct, dprev = dts
        (row_after,) = sc

        @pl.when(pl.program_id(0) == 0)
        def _():
            row_after[...] = jnp.zeros_like(row_after)

        dz = dz_direct + _shift_up(dprev, row_after[...])
        row_after[...] = dprev[:_HALO]
        return [dz]

    (dproj,), (d_mu, g['w0'], d_wd, g['a0'], d_wa, d_wg, g['k_k'], g['k_a']), (cs_rw,) = _rowwise_vjp(
        "rwkv_pre_bwd", _seg_rwkv_pre, [z_rw, z_rw_halo], pre_b, [[dr1, dr2], dlw, [dk1, dk2], [dv1, dv2], da, db, dg_], pre_tile,
        [0, 1], [0, 1, 2, 3, 4, 5, 6, 7], t_dtypes=[_MXU_DTYPE], colsum=[0], prep=pre_prep, finish=pre_finish,
        out_widths=[RW_COLS], reverse=True, scratch=[(_HALO, RW_COLS)], into={0: (dproj, 1)})
    g['mu_shift'] = d_mu[:, :RW_USED]
    for n, d_ in (('w_decay_up', d_wd), ('w_aaa_up', d_wa), ('w_gate_up', d_wg)):
        r0, nr = _LORA[n]
        g[n] = d_[r0:r0 + nr]
    (dproj,), (g['g_ln_v'], g['b_ln_v'], g['w_spatial'], d_b_tg), (cs_g,) = _rowwise_vjp(
        "gmlp_bwd", _seg_gmlp, [z_gmlp], gmlp_b, [dya], G_CHUNK, [0], [0, 1, 2, 3], t_dtypes=[_MXU_DTYPE], colsum=[0],
        into={0: (dproj, 4)})
    g['b_spatial'] = d_b_tg[:, :8].T
    g['b_in'] = _from_padded(jnp.concatenate([cs_gates, cs_rw, cs_g], axis=1), 1)
    small_names = [n for n in _WEIGHTS if n not in _BIG and n not in ('w_ada', 'b_ada')]
    packed_g, g_places = _pack_rows([g[n] for n in small_names] + [loss_row], row_mult=256)
    gw_in_t, small_all = _mm("g_w_in", dproj, h, "tn", rider=(packed_g, _gather_def, (0.0, 0.6, 1.0)))
    gw['w_in'] = _from_padded(gw_in_t, 0)
    send_late = send_rows(big_names[-1:])
    sibling_late = _exchange("pair_exchange_late", send_late, _sibling_def)
    partials_late = _chip_partials("chip_partials_late", send_late, sibling_late, 128, _COMM_DTYPE)
    dh, landed_late = _mm("d_h", dproj, w_in_t, "nn", rider=(partials_late, _chips_def, (0.0, 1.0)))

    def mod1_bwd(tv, bv):
        x_t, dh_t, dxa_t = tv
        (sc,) = bv
        return [dxa_t + dh_t * (1.0 + sc)], [jnp.sum(dh_t * x_t, axis=0, keepdims=True), jnp.sum(dh_t, axis=0, keepdims=True)]

    (grad_x,), (d_sc1, d_sh1) = _rowwise("modulate1_bwd", mod1_bwd, [x, dh, dx_a], [sc1], [(D_MODEL, F32)], [(1, D_MODEL)] * 2, tile)

    dmod = jnp.concatenate([d_sh1, d_sc1, d_gt1, d_sh2, d_sc2, d_gt2], axis=1).reshape(6 * D_MODEL // 128, 128)
    dmod_all = _all_gather("gather_dmod", dmod)
    g['b_ada'] = _sum_leading("sum_dmod", dmod_all).reshape(1, 6 * D_MODEL)
    dmod_mine = lax.dynamic_slice(dmod_all.reshape(_N_DEV, 6 * D_MODEL), (0, me * 768), (_N_DEV, 768))
    g_w_ada = _mm("g_w_ada", c_act_all, _pad_rows(dmod_mine, 16), "tn")

    small_sum = _unpack_rows(_sum_leading("sum_small", small_all), g_places)
    loss = small_sum.pop()[0, 0]
    for n, t in zip(small_names, small_sum, strict=True):
        g[n] = t
    for n in _LORA:
        g[n] = lax.dynamic_slice(g[n], (0, me * R_HEAD), (g[n].shape[0], R_HEAD))
    g['w_ada'] = g_w_ada

    summed = jnp.concatenate([_sum_leading("sum_early", landed_early), _sum_leading("sum_late", landed_late)], axis=0)
    g.update(_unpack_big(summed, p))

    delta, new_m, new_v = {}, {}, {}
    own_call = ['w_ada'] + big_names
    for n in own_call:
        if n == 'w_in':
            r0 = _BIG[n][0]
            outs_t = _adamw("adamw_" + n, p[n].T, summed[r0:r0 + IN_SHARD], m[n].T, v[n].T)
            delta[n], new_m[n], new_v[n] = [o.T for o in outs_t]
        else:
            delta[n], new_m[n], new_v[n] = _adamw("adamw_" + n, p[n], g[n], m[n], v[n])
    rest = [n for n in _WEIGHTS if n not in own_call]
    outs = _adamw_many("adamw_rest", *[[d[n].reshape(p[n].shape) for n in rest] for d in (p, g, m, v)])
    for d, o in zip((delta, new_m, new_v), outs, strict=True):
        d.update(zip(rest, o, strict=True))
    return loss, grad_x, g, delta, new_m, new_v


def kernel(x, c, w_ada, b_ada, w_in, b_in, g_ln_v, b_ln_v, w_spatial, b_spatial, mu_shift, w0, w_decay_up, a0, w_aaa_up, w_gate_up, k_k, k_a, r_k, gn_gain, gn_bias, w_branch_a, w_branch_b, w_out, b_out, ln1_g, ln1_b, w_ff1, b_ff1, w_ff2, b_ff2, ln2_g, ln2_b, loss_target, m_w_ada, m_b_ada, m_w_in, m_b_in, m_g_ln_v, m_b_ln_v, m_w_spatial, m_b_spatial, m_mu_shift, m_w0, m_w_decay_up, m_a0, m_w_aaa_up, m_w_gate_up, m_k_k, m_k_a, m_r_k, m_gn_gain, m_gn_bias, m_w_branch_a, m_w_branch_b, m_w_out, m_b_out, m_ln1_g, m_ln1_b, m_w_ff1, m_b_ff1, m_w_ff2, m_b_ff2, m_ln2_g, m_ln2_b, v_w_ada, v_b_ada, v_w_in, v_b_in, v_g_ln_v, v_b_ln_v, v_w_spatial, v_b_spatial, v_mu_shift, v_w0, v_w_decay_up, v_a0, v_w_aaa_up, v_w_gate_up, v_k_k, v_k_a, v_r_k, v_gn_gain, v_gn_bias, v_w_branch_a, v_w_branch_b, v_w_out, v_b_out, v_ln1_g, v_ln1_b, v_w_ff1, v_b_ff1, v_w_ff2, v_b_ff2, v_ln2_g, v_ln2_b):
    given = dict(locals())
    shapes = {n: given[n].shape for n in _WEIGHTS}
    def two_d(a_):
        a_ = a_[0]
        return a_.reshape(1, -1) if a_.ndim == 1 else a_
    p = {n: two_d(given[n]) for n in _WEIGHTS}
    m = {n: two_d(given["m_" + n]) for n in _WEIGHTS}
    v = {n: two_d(given["v_" + n]) for n in _WEIGHTS}
    loss, grad_x, g, delta, new_m, new_v = _step(p, m, v, x[0], c, loss_target[0])
    outs = [loss, grad_x[None]]
    for d in (g, delta, new_m, new_v):
        outs += [d[n].reshape(shapes[n]) for n in _WEIGHTS]
    return tuple(outs)
```

```python
import functools

import jax
import jax.numpy as jnp
from jax import lax
from jax.experimental import pallas as pl
from jax.experimental.pallas import tpu as pltpu

F32 = jnp.float32
_MXU_DTYPE = jnp.bfloat16
_HI = lax.Precision.HIGHEST
_VMEM_LIMIT = 48 * 1024 * 1024
_MESH_ID = pl.DeviceIdType.MESH
_N_DEV = 8

D_MODEL = 1024
G_WIDTH = 512
G_CHUNK = 128
R_WIDTH = 512
R_HEADS = 8
R_HEAD = 64
LORA_W, LORA_A, LORA_G = 32, 32, 96
D_FF = 4096
ALPHA = 2.0 ** 0.25
LN_EPS = 1e-5
GN_EPS = 64e-5
SCAN_CHUNK = 64
SCAN_PER_STEP = 4
ADAM_LR, ADAM_B1, ADAM_B2, ADAM_EPS, ADAM_WD, ADAM_STEP = 0.001, 0.9, 0.999, 1e-08, 0.01, 10

P_COLS = 5120
RW_COLS = 2048
RW_USED = 3 * R_WIDTH + LORA_W + LORA_A + LORA_G
LORA_PAD = 256
IN_COLS = 2 * G_WIDTH + RW_USED + 2 * D_MODEL


def _cparams(sem=None, **kw):
    if sem is not None:
        kw["dimension_semantics"] = sem
    return pltpu.CompilerParams(vmem_limit_bytes=_VMEM_LIMIT, **kw)


def _dot(a, b, dims=(((1,), (0,)), ((), ())), hi=False):
    if hi:
        return lax.dot_general(a.astype(F32), b.astype(F32), dims, precision=_HI, preferred_element_type=F32)
    return lax.dot_general(a.astype(_MXU_DTYPE), b.astype(_MXU_DTYPE), dims, preferred_element_type=F32)


_NN = (((1,), (0,)), ((), ()))
_NT = (((1,), (1,)), ((), ()))
_TN = (((0,), (0,)), ((), ()))


def _pick(n, pref):
    for t in pref:
        if n % t == 0:
            return t
    return n


def _mm(name, a, b, mode, bias=None, out_dtype=F32, epi=None, epi_dtype=None, raw=True, beside=None, colsum=False, rider=None,
        tm=None, tn=None, tk=None):
    if mode == "nn":
        (M, K), (_, N) = a.shape, b.shape
    elif mode == "nt":
        (M, K), (N, _) = a.shape, b.shape
    else:
        (K, M), (_, N) = a.shape, b.shape
    tm = tm or _pick(M, (2048, 1280, 1024, 512, 256, 128, 64, 32, 16, 8))
    tn = tn or _pick(N, (1024, 512, 640, 384, 256, 128))
    tk = tk or _pick(K, (1024, 512, 256, 128))
    nk = K // tk
    dims = {"nn": _NN, "nt": _NT, "tn": _TN}[mode]
    a_spec = pl.BlockSpec((tk, tm), lambda i, j, k: (k, i)) if mode == "tn" else pl.BlockSpec((tm, tk), lambda i, j, k: (i, k))
    b_spec = pl.BlockSpec((tn, tk), lambda i, j, k: (j, k)) if mode == "nt" else pl.BlockSpec((tk, tn), lambda i, j, k: (k, j))
    o_spec = pl.BlockSpec((tm, tn), lambda i, j, k: (i, j))
    has_bias, has_beside = bias is not None, beside is not None
    two = epi is not None and not has_beside and raw
    only_epi = epi is not None and not has_beside and not raw
    grid = (M // tm, N // tn, nk)
    ride_shape, ride_sems, _ = rider[1](rider[0]) if rider else (None, [], None)

    def body(*refs):
        refs = list(refs)
        n_in = 2 + has_bias + has_beside
        if rider:
            sem_refs = [refs.pop() for _ in ride_sems][::-1]
            ride_out = refs.pop(n_in + 1 + 1 + two)
            ride_in = refs.pop(n_in)
            step = (pl.program_id(0) * grid[1] + pl.program_id(1)) * grid[2] + pl.program_id(2)
            _ride(rider, step, grid[0] * grid[1] * grid[2], ride_in, ride_out, sem_refs)
        a_ref, b_ref = refs[0], refs[1]
        bias_ref = refs[2] if has_bias else None
        beside_ref = refs[n_in - 1] if has_beside else None
        outs = refs[n_in:]
        o_ref, acc_ref = outs[0], outs[-1]
        k = pl.program_id(2)

        @pl.when(k == 0)
        def _():
            acc_ref[...] = jnp.zeros_like(acc_ref)

        acc_ref[...] += _dot(a_ref[...], b_ref[...], dims)

        @pl.when(k == nk - 1)
        def _():
            res = acc_ref[...]
            if has_bias:
                res = res + bias_ref[...]
            if has_beside:
                val = epi(res, beside_ref[...])
                o_ref[...] = val.astype(o_ref.dtype)
                if colsum:
                    outs[1][...] = jnp.broadcast_to(jnp.sum(val, axis=0, keepdims=True), outs[1].shape)
            elif only_epi:
                o_ref[...] = epi(res).astype(o_ref.dtype)
            else:
                o_ref[...] = res.astype(o_ref.dtype)
                if two:
                    outs[1][...] = epi(res).astype(outs[1].dtype)

    in_specs = [a_spec, b_spec]
    args = [a, b]
    if has_bias:
        in_specs.append(pl.BlockSpec((1, tn), lambda i, j, k: (0, j)))
        args.append(bias)
    if has_beside:
        in_specs.append(o_spec)
        args.append(beside)
    out_shape = [jax.ShapeDtypeStruct((M, N), epi_dtype if (has_beside or only_epi) else out_dtype)]
    out_specs = [o_spec]
    if two:
        out_shape.append(jax.ShapeDtypeStruct((M, N), epi_dtype))
        out_specs.append(o_spec)
    if colsum:
        assert has_beside and not rider
        out_shape.append(jax.ShapeDtypeStruct((grid[0] * 8, N), F32))
        out_specs.append(pl.BlockSpec((8, tn), lambda i, j, k: (i, j)))
    if rider:
        in_specs.append(_HBM)
        args.append(rider[0])
        out_shape.append(ride_shape)
        out_specs.append(_HBM)
    res = pl.pallas_call(
        body, name=name, grid=grid, in_specs=in_specs, out_specs=out_specs, out_shape=out_shape,
        scratch_shapes=[pltpu.VMEM((tm, tn), F32)] + list(ride_sems),
        compiler_params=_cparams(("arbitrary",) * 3 if rider else ("parallel", "parallel", "arbitrary")),
    )(*args)
    if colsum:
        return res[0], _sum_leading(name + "_colsum", res[1].reshape(grid[0], 8, N))[0:1]
    return res if (two or rider) else res[0]


_HALO = 8


def _rowwise(name, fn, tiled, bcast, tiled_out, red_out, tile, reverse=False, scratch=(), rider=None):
    tiled = [t if isinstance(t, tuple) else (t, t.shape[1], 0) for t in tiled]
    T = next(t[0] for t in tiled if not isinstance(t[0], str)).shape[0]
    n = T // tile
    nt, nb, nto, nsc, nro = len(tiled), len(bcast), len(tiled_out), len(scratch), len(red_out)
    ride_shape, ride_sems, _ = rider[1](rider[0]) if rider else (None, [], None)
    nr = 1 if rider else 0
    into = [(j, t[2], t[3]) for j, t in enumerate(tiled_out) if len(t) == 4]
    na = len(into)

    def row_block(i):
        return n - 1 - i if reverse else i

    def body(*refs):
        i2 = nt + nb
        o0 = i2 + nr + na
        o1, o2 = o0 + nto, o0 + nto + nro
        s0 = o2 + nr
        if rider:
            _ride(rider, pl.program_id(0), n, refs[i2], refs[o2], refs[s0 + nsc:])
        t_refs, b_refs = refs[:nt], refs[nt:i2]
        to_refs, ro_refs = refs[o0:o1], refs[o1:o2]
        extra = (list(refs[s0:s0 + nsc]),) if nsc else ()
        touts, routs = fn([r[...] for r in t_refs], [r[...] for r in b_refs], *extra)
        for r, v in zip(to_refs, touts, strict=True):
            r[...] = v.astype(r.dtype)
        if ro_refs:
            i = pl.program_id(0)

            @pl.when(i == 0)
            def _():
                for r, v in zip(ro_refs, routs, strict=True):
                    r[...] = v.astype(F32)

            @pl.when(i > 0)
            def _():
                for r, v in zip(ro_refs, routs, strict=True):
                    r[...] += v.astype(F32)

    def whole(shape):
        nd = len(shape)
        return pl.BlockSpec(tuple(shape), lambda i: (0,) * nd)

    per_tile = tile // _HALO
    in_specs, arrays = [], []
    for t in tiled:
        if isinstance(t[0], str):
            _, arr, w, cb = t
            in_specs.append(pl.BlockSpec((_HALO, w), functools.partial(lambda i, cb: (jnp.maximum(row_block(i) * per_tile - 1, 0), cb), cb=cb)))
        else:
            arr, w, cb = t
            in_specs.append(pl.BlockSpec((tile, w), functools.partial(lambda i, cb: (row_block(i), cb), cb=cb)))
        arrays.append(arr)
    in_specs += [whole(b.shape) for b in bcast]
    out_specs, out_shape = [], []
    for t in tiled_out:
        cb = t[3] if len(t) == 4 else 0
        out_specs.append(pl.BlockSpec((tile, t[0]), functools.partial(lambda i, cb: (row_block(i), cb), cb=cb)))
        out_shape.append(jax.ShapeDtypeStruct(t[2].shape if len(t) == 4 else (T, t[0]), t[1]))
    out_specs += [whole(s) for s in red_out]
    out_shape += [jax.ShapeDtypeStruct(tuple(s), F32) for s in red_out]
    ride_args = []
    if rider:
        in_specs.append(_HBM)
        out_specs.append(_HBM)
        out_shape.append(ride_shape)
        ride_args = [rider[0]]
    in_specs += [pl.BlockSpec(memory_space=pl.ANY)] * na
    aliases = {nt + nb + nr + k: j for k, (j, _, _) in enumerate(into)}
    res = pl.pallas_call(
        body, name=name, grid=(n,), in_specs=in_specs, out_specs=out_specs, out_shape=out_shape,
        scratch_shapes=[pltpu.VMEM(tuple(s), F32) for s in scratch] + list(ride_sems),
        input_output_aliases=aliases, compiler_params=_cparams(("arbitrary",)),
    )(*arrays, *bcast, *ride_args, *[buf for (_, buf, _) in into])
    if rider:
        return list(res[:nto]), list(res[nto:nto + nro]), res[-1]
    return list(res[:nto]), list(res[nto:])


def _rowwise_vjp(name, f, tiled, bcast, cts, tile, wrt_t, wrt_b, t_dtypes=None, colsum=(), prep=None, finish=None,
                 out_widths=None, reverse=False, scratch=(), rider=None, into=None):
    tiled = [t if isinstance(t, tuple) else (t, t.shape[1], 0) for t in tiled]
    npr = len(tiled)
    t_dtypes = t_dtypes or [F32] * len(wrt_t)
    groups = [c if isinstance(c, list) else [c] for c in cts]
    cts = [a_ for grp in groups for a_ in grp]

    def fn(tv, bv, sc=None):
        prim, flat_ct = tv[:npr], list(tv[npr:])
        if prep is not None:
            prim = prep(prim)
        ct = []
        for grp in groups:
            parts = [flat_ct.pop(0).astype(F32) for _ in grp]
            ct.append(functools.reduce(lambda p_, q_: p_ + q_, parts))

        def g(dt_vals, db_vals):
            full_t, full_b = list(prim), list(bv)
            for i, v in zip(wrt_t, dt_vals, strict=True):
                full_t[i] = v
            for j, v in zip(wrt_b, db_vals, strict=True):
                full_b[j] = v
            return f(full_t, full_b)

        outs, pull = jax.vjp(g, [prim[i].astype(F32) for i in wrt_t], [bv[j] for j in wrt_b])
        dts, dbs = pull([c.astype(o.dtype) for c, o in zip(ct, outs, strict=True)])
        if finish is not None:
            dts = finish(dts, sc)
        sums = [jnp.sum(dts[i].astype(F32), axis=0, keepdims=True) for i in colsum]
        return dts, list(dbs) + sums

    widths = out_widths or [tiled[i][1] for i in wrt_t]
    tiled_out = [(w, dt) + tuple((into or {}).get(j, ())) for j, (w, dt) in enumerate(zip(widths, t_dtypes, strict=True))]
    red_out = [bcast[j].shape for j in wrt_b] + [(1, widths[i]) for i in colsum]
    res = _rowwise(name, fn, tiled + list(cts), bcast, tiled_out, red_out, tile, reverse=reverse, scratch=scratch, rider=rider)
    dts, reds = res[0], res[1]
    nb = len(wrt_b)
    return (dts, reds[:nb], reds[nb:]) + tuple(res[2:])


def _layer_norm(x, g, b, eps):
    mu = jnp.mean(x, axis=-1, keepdims=True)
    xc = x - mu
    var = jnp.mean(xc * xc, axis=-1, keepdims=True)
    return xc * lax.rsqrt(var + eps) * g + b


def _gelu_tanh(x):
    return 0.5 * x * (1.0 + jnp.tanh(0.7978845608028654 * (x + 0.044715 * (x * x * x))))


def _sigmoid(x):
    return 1.0 / (1.0 + jnp.exp(-x))


def _seg_modulate(tv, bv):
    (x,), (sc, sh) = tv, bv
    return [x * (1.0 + sc) + sh]


def _seg_gmlp(tv, bv):
    (z,), (g_ln, b_ln, ws, b_tg, expand) = tv, bv
    bias_full = _dot(b_tg, expand, hi=True)
    zz = _gelu_tanh(z)
    u, v = zz[:, :G_WIDTH], zz[:, G_WIDTH:]
    v = _layer_norm(v, g_ln, b_ln, LN_EPS)
    row = lax.broadcasted_iota(jnp.int32, (G_CHUNK, G_CHUNK), 0)
    col = lax.broadcasted_iota(jnp.int32, (G_CHUNK, G_CHUNK), 1)
    causal = col <= row
    first_group = lax.broadcasted_iota(jnp.int32, (G_CHUNK, 128), 1) < 64
    parts = []
    for p in range(4):
        vp = v[:, 128 * p:128 * (p + 1)]
        s_even = _dot(jnp.where(causal, ws[2 * p], 0.0), vp)
        s_odd = _dot(jnp.where(causal, ws[2 * p + 1], 0.0), vp)
        parts.append(jnp.where(first_group, s_even, s_odd))
    s = jnp.concatenate(parts, axis=1) + bias_full
    return [u * s]


def _split2(x):
    hi = x.astype(_MXU_DTYPE)
    return hi, (x - hi.astype(F32)).astype(_MXU_DTYPE)


@jax.custom_vjp
def _group_sum(x, ones_blocks):
    hi, lo = _split2(x)
    return _dot(hi, ones_blocks) + _dot(lo, ones_blocks)


def _group_sum_fwd(x, ones_blocks):
    return _group_sum(x, ones_blocks), ones_blocks


def _group_sum_bwd(ones_blocks, ct):
    return _group_sum(ct, ones_blocks), jnp.zeros_like(ones_blocks)


_group_sum.defvjp(_group_sum_fwd, _group_sum_bwd)


def _shift_down(z, halo, is_first):
    _, W = z.shape
    rolled = pltpu.roll(z, 1, 0)
    before = jnp.where(is_first, 0.0, pltpu.roll(halo, 1, 0))
    top_row = lax.broadcasted_iota(jnp.int32, (_HALO, W), 0) == 0
    return jnp.concatenate([jnp.where(top_row, before, rolled[:_HALO]), rolled[_HALO:]], axis=0)


def _shift_up(d, after):
    tile, W = d.shape
    rolled = pltpu.roll(d, tile - 1, 0)
    last_row = lax.broadcasted_iota(jnp.int32, (_HALO, W), 0) == _HALO - 1
    bottom = jnp.where(last_row, pltpu.roll(after, _HALO - 1, 0), rolled[tile - _HALO:])
    return jnp.concatenate([rolled[:tile - _HALO], bottom], axis=0)


def _seg_rwkv_pre(tv, bv):
    (z, prev), (mu, w0, wd, a0, wa, wg, k_k, k_a, gsum) = tv, bv
    zs = z + (prev - z) * mu
    r, k, v = zs[:, 0:512], zs[:, 512:1024], zs[:, 1024:1536]
    zl = zs[:, 3 * R_WIDTH:3 * R_WIDTH + LORA_PAD]
    x = w0 + _dot(jnp.tanh(zl), wd)
    softplus = jnp.maximum(-x, 0.0) + jnp.log(1.0 + jnp.exp(-jnp.abs(x)))
    lw = -jnp.exp(-softplus - 0.5)
    a = _sigmoid(a0 + _dot(zl, wa))
    g = _dot(_sigmoid(zl), wg)
    kk = k * k_k
    nrm = jnp.sqrt(_group_sum(kk * kk, gsum))
    kk = kk / jnp.maximum(nrm, 1e-12)
    k2 = k * (1.0 + (a - 1.0) * k_a)
    return [r, lw, k2, v, -kk, kk * a, g]


def _seg_rwkv_post(tv, bv):
    (y, r, k2, v, g), (r_k, gain, bias, gsum) = tv, bv
    mu = _group_sum(y, gsum) * (1.0 / R_HEAD)
    yc = y - mu
    var = _group_sum(yc * yc, gsum) * (1.0 / R_HEAD)
    yn = yc * lax.rsqrt(var + GN_EPS) * gain + bias
    bonus = _group_sum(r * k2 * r_k, gsum) * v
    return [(yn + bonus) * g]


def _seg_merge(tv, bv):
    (ga, gb, pa, pb), () = tv, bv
    return [_sigmoid(ga) * pa + _sigmoid(gb) * pb]


def _seg_mid(tv, bv):
    (x, mix), (gt1, g1, b1, sc2, sh2) = tv, bv
    h1 = _layer_norm(ALPHA * x + gt1 * mix, g1, b1, LN_EPS)
    return [h1, h1 * (1.0 + sc2) + sh2]


def _seg_relu2(tv, bv):
    (f1,), () = tv, bv
    return [jnp.square(jnp.maximum(f1, 0.0))]


def _seg_loss(tv, bv):
    (h1, ff, target), (gt2, g2, b2) = tv, bv
    out = _layer_norm(ALPHA * h1 + gt2 * ff, g2, b2, LN_EPS)
    err = jnp.square(out - target)
    return 0.5 * jnp.sum(jnp.mean(err, axis=-1))


_BNN = (((2,), (1,)), ((0,), (0,)))
_BNT = (((2,), (2,)), ((0,), (0,)))
_BTN = (((1,), (1,)), ((0,), (0,)))


def _tri_dot(x, dims):
    L = x.shape[0]
    tri = (lax.broadcasted_iota(jnp.int32, (L, L), 1) <= lax.broadcasted_iota(jnp.int32, (L, L), 0)).astype(F32)
    hi, lo = _split2(x)
    return _dot(tri, hi, dims) + _dot(tri, lo, dims)


@jax.custom_vjp
def _running_sum(x):
    return _tri_dot(x, _NN)


_running_sum.defvjp(lambda x: (_tri_dot(x, _NN), None), lambda _, ct: (_tri_dot(ct, _TN),))


def _cut_heads(x):
    return jnp.stack([x[:, R_HEAD * h:R_HEAD * (h + 1)] for h in range(R_HEADS)])


def _join_heads(x):
    return jnp.concatenate([x[h] for h in range(R_HEADS)], axis=1)


@jax.custom_vjp
def _split_heads(x):
    return _cut_heads(x)


@jax.custom_vjp
def _merge_heads(x):
    return _join_heads(x)


_split_heads.defvjp(lambda x: (_cut_heads(x), None), lambda _, ct: (_join_heads(ct),))
_merge_heads.defvjp(lambda x: (_join_heads(x), None), lambda _, ct: (_cut_heads(ct),))


def _inverse_pullback(inv, ct):
    return _dot(_dot(inv, ct, _BTN), inv, _BNT)


@jax.custom_vjp
def _unit_lower_inverse(n_mat):
    H, L, _ = n_mat.shape
    eye = lax.broadcasted_iota(jnp.int32, (H, L, L), 1) == lax.broadcasted_iota(jnp.int32, (H, L, L), 2)
    inv = jnp.where(eye, 1.0, 0.0) + n_mat
    pw = n_mat
    n = 2
    while n < L:
        pw = _dot(pw, pw, _BNN)
        inv = inv + _dot(inv, pw, _BNN)
        n *= 2
    return inv


def _unit_lower_inverse_fwd(n_mat):
    inv = _unit_lower_inverse(n_mat)
    return inv, inv


_unit_lower_inverse.defvjp(_unit_lower_inverse_fwd, lambda inv, ct: (_inverse_pullback(inv, ct),))


@jax.custom_vjp
def _known_inverse(n_mat, inv):
    return inv


_known_inverse.defvjp(lambda n_mat, inv: (inv, inv), lambda inv, ct: (_inverse_pullback(inv, ct), jnp.zeros_like(inv)))


def _scan_chunk(r, lw, k, v, a, b, s0, inv=None, with_inverse=False):
    L, H = r.shape[0], R_HEADS
    cs = _running_sum(lw)
    cs_end = cs[L - 1:L, :]
    p, p_inv, to_end = jnp.exp(cs), jnp.exp(-cs), jnp.exp(cs_end - cs)
    at, bt, kt, rt = [_split_heads(t) for t in (a * jnp.exp(cs - lw), b * p_inv, k * p_inv, r * p)]
    b_end, k_end, v = [_split_heads(t) for t in (b * to_end, k * to_end, v)]
    row = lax.broadcasted_iota(jnp.int32, (H, L, L), 1)
    col = lax.broadcasted_iota(jnp.int32, (H, L, L), 2)
    incl, strict = col <= row, col < row
    a_ab = jnp.where(strict, _dot(at, bt, _BNT), 0.0)
    a_ak = jnp.where(strict, _dot(at, kt, _BNT), 0.0)
    a_rb = jnp.where(incl, _dot(rt, bt, _BNT), 0.0)
    a_rk = jnp.where(incl, _dot(rt, kt, _BNT), 0.0)
    inv = _unit_lower_inverse(a_ab) if inv is None else _known_inverse(a_ab, inv)
    u = _dot(inv, _dot(at, s0, _BNT) + _dot(a_ak, v, _BNN), _BNN)
    y = _merge_heads(_dot(rt, s0, _BNT) + _dot(a_rb, u, _BNN) + _dot(a_rk, v, _BNN))
    s1 = s0 * _split_heads(jnp.exp(cs_end)) + _dot(u, b_end, _BTN) + _dot(v, k_end, _BTN)
    return (y, s1, inv) if with_inverse else (y, s1)


def _scan_fwd(r, lw, k, v, a, b, rider):
    T = r.shape[0]
    H, N, L, P = R_HEADS, R_HEAD, SCAN_CHUNK, SCAN_PER_STEP
    nc = T // L
    steps = nc // P
    ride_shape, ride_sems, _ = rider[1](rider[0])

    def body(r_ref, lw_ref, k_ref, v_ref, a_ref, b_ref, ride_in, y_ref, st_ref, inv_ref, ride_out, s_ref, *sem_refs):
        _ride(rider, pl.program_id(0), steps, ride_in, ride_out, sem_refs)

        @pl.when(pl.program_id(0) == 0)
        def _():
            s_ref[...] = jnp.zeros_like(s_ref)

        s0 = s_ref[...]
        for j in range(P):
            rows = pl.ds(j * L, L)
            st_ref[j] = s0
            y, s0, inv = _scan_chunk(*[t[rows, :] for t in (r_ref, lw_ref, k_ref, v_ref, a_ref, b_ref)], s0, with_inverse=True)
            y_ref[rows, :] = y
            inv_ref[j] = inv
        s_ref[...] = s0

    blk = pl.BlockSpec((P * L, R_WIDTH), lambda c: (c, 0))
    per_chunk = pl.BlockSpec((P, H, N, N), lambda c: (c, 0, 0, 0))
    return pl.pallas_call(
        body, name="scan_fwd", grid=(steps,), in_specs=[blk] * 6 + [_HBM], out_specs=[blk, per_chunk, per_chunk, _HBM],
        out_shape=[jax.ShapeDtypeStruct((T, R_WIDTH), F32)] + [jax.ShapeDtypeStruct((nc, H, N, N), F32)] * 2 + [ride_shape],
        scratch_shapes=[pltpu.VMEM((H, N, N), F32)] + list(ride_sems),
        compiler_params=_cparams(("arbitrary",)),
    )(r, lw, k, v, a, b, rider[0])


def _scan_bwd(r, lw, k, v, a, b, states, inverses, dy, rider):
    T = r.shape[0]
    H, N, L, P = R_HEADS, R_HEAD, SCAN_CHUNK, SCAN_PER_STEP
    nc = T // L
    steps = nc // P
    ride_shape, ride_sems, _ = rider[1](rider[0])

    def body(r_ref, lw_ref, k_ref, v_ref, a_ref, b_ref, st_ref, inv_ref, dy_ref, ride_in,
             dr_ref, dlw_ref, dk_ref, dv_ref, da_ref, db_ref, ride_out, ds_ref, *sem_refs):
        _ride(rider, pl.program_id(0), steps, ride_in, ride_out, sem_refs)

        @pl.when(pl.program_id(0) == 0)
        def _():
            ds_ref[...] = jnp.zeros_like(ds_ref)

        ds = ds_ref[...]
        for j in reversed(range(P)):
            rows = pl.ds(j * L, L)
            args = [t[rows, :] for t in (r_ref, lw_ref, k_ref, v_ref, a_ref, b_ref)] + [st_ref[j]]
            inv = inv_ref[j]
            _, pull = jax.vjp(lambda *xs, inv=inv: _scan_chunk(*xs, inv=inv), *args)
            grads = pull((dy_ref[rows, :], ds))
            for o_ref, g_ in zip((dr_ref, dlw_ref, dk_ref, dv_ref, da_ref, db_ref), grads[:6], strict=True):
                o_ref[rows, :] = g_
            ds = grads[6]
        ds_ref[...] = ds

    blk = pl.BlockSpec((P * L, R_WIDTH), lambda c: (steps - 1 - c, 0))
    per_chunk = pl.BlockSpec((P, H, N, N), lambda c: (steps - 1 - c, 0, 0, 0))
    return pl.pallas_call(
        body, name="scan_bwd", grid=(steps,), in_specs=[blk] * 6 + [per_chunk, per_chunk, blk, _HBM], out_specs=[blk] * 6 + [_HBM],
        out_shape=[jax.ShapeDtypeStruct((T, R_WIDTH), F32)] * 6 + [ride_shape],
        scratch_shapes=[pltpu.VMEM((H, N, N), F32)] + list(ride_sems),
        compiler_params=_cparams(("arbitrary",)),
    )(r, lw, k, v, a, b, states, inverses, dy, rider[0])


def _place():
    x, y, c = lax.axis_index("x"), lax.axis_index("y"), lax.axis_index("c")
    return x, y, c


def _gather_def(block):
    R, C = block.shape

    def phases(x_ref, out_ref, send_sems, recv_sems, local_sem):
        x, y, c = _place()
        me, sibling = (x, y, c), (x, y, 1 - c)
        chips = [(1 - x, y), (x, 1 - y), (1 - x, 1 - y)]

        def slot(px, py, pc):
            return out_ref.at[4 * px + 2 * py + pc]

        def copy(k, blk, to, src=None):
            return pltpu.make_async_remote_copy(
                src_ref=slot(*blk) if src is None else src, dst_ref=slot(*blk),
                send_sem=send_sems.at[k], recv_sem=recv_sems.at[k], device_id=to, device_id_type=_MESH_ID)

        mine = pltpu.make_async_copy(x_ref, slot(*me), local_sem)
        first = [copy(0, me, sibling, src=x_ref)]
        first += [copy(1 + j, me, (*chip, c), src=x_ref) for j, chip in enumerate(chips)]
        passed = [copy(4 + j, (*chip, c), sibling) for j, chip in enumerate(chips)]

        def begin():
            mine.start()
            for cp in first:
                cp.start()

        def forward():
            for j, chip in enumerate(chips):
                copy(1 + j, (*chip, c), me).wait_recv()
                passed[j].start()

        def finish():
            copy(0, sibling, me).wait_recv()
            for j, chip in enumerate(chips):
                copy(4 + j, (*chip, 1 - c), me).wait_recv()
            for cp in first + passed:
                cp.wait_send()
            mine.wait()

        return [begin, forward, finish]

    sems = [pltpu.SemaphoreType.DMA((7,)), pltpu.SemaphoreType.DMA((7,)), pltpu.SemaphoreType.DMA]
    return jax.ShapeDtypeStruct((_N_DEV, R, C), block.dtype), sems, phases


def _sibling_def(blocks):
    _, R, C = blocks.shape

    def phases(x_ref, out_ref, send_sems, recv_sems):
        x, y, c = _place()
        copies = [pltpu.make_async_remote_copy(
            src_ref=x_ref.at[2 * q + (1 - c)], dst_ref=out_ref.at[q], send_sem=send_sems.at[q], recv_sem=recv_sems.at[q],
            device_id=(x, y, 1 - c), device_id_type=_MESH_ID) for q in range(4)]

        def begin():
            for cp in copies:
                cp.start()

        def finish():
            for cp in copies:
                cp.wait()

        return [begin, finish]

    return jax.ShapeDtypeStruct((4, R, C), blocks.dtype), [pltpu.SemaphoreType.DMA((4,)), pltpu.SemaphoreType.DMA((4,))], phases


def _chips_def(partials):
    def phases(x_ref, out_ref, send_sems, recv_sems, local_sem):
        x, y, c = _place()
        my_chip = 2 * x + y
        mine = pltpu.make_async_copy(x_ref.at[my_chip], out_ref.at[my_chip], local_sem)
        copies = []
        for rel in range(1, 4):
            px, py = (1 - x if (rel >> 1) & 1 else x), (1 - y if rel & 1 else y)
            copies.append(pltpu.make_async_remote_copy(
                src_ref=x_ref.at[2 * px + py], dst_ref=out_ref.at[my_chip],
                send_sem=send_sems.at[rel - 1], recv_sem=recv_sems.at[rel - 1],
                device_id=(px, py, c), device_id_type=_MESH_ID))

        def begin():
            mine.start()
            for cp in copies:
                cp.start()

        def finish():
            for cp in copies:
                cp.wait()
            mine.wait()

        return [begin, finish]

    sems = [pltpu.SemaphoreType.DMA((3,)), pltpu.SemaphoreType.DMA((3,)), pltpu.SemaphoreType.DMA]
    return jax.ShapeDtypeStruct(partials.shape, partials.dtype), sems, phases


_HBM = pl.BlockSpec(memory_space=pltpu.HBM)


def _exchange(name, array, definition):
    out_shape, sems, phases = definition(array)

    def body(x_ref, out_ref, *sem_refs):
        for phase in phases(x_ref, out_ref, *sem_refs):
            phase()

    return pl.pallas_call(body, name=name, in_specs=[_HBM], out_specs=_HBM, out_shape=out_shape, scratch_shapes=sems)(array)


def _ride(rider, step, nsteps, x_ref, out_ref, sem_refs):
    array, definition, fractions = rider
    for phase, frac in zip(definition(array)[2](x_ref, out_ref, *sem_refs), fractions, strict=True):
        pl.when(step == min(int(frac * nsteps), nsteps - 1))(phase)


def _all_gather(name, block):
    return _exchange(name, block, _gather_def)


def _chip_partials(name, blocks, from_sibling, tile, out_dtype):
    _, R, C = blocks.shape
    by_chip = blocks.reshape(4, 2, R, C)

    def for_core(core):
        def body(x_ref, s_ref, o_ref):
            for q in range(4):
                o_ref[q] = (x_ref[q] + s_ref[q]).astype(o_ref.dtype)

        return lambda: pl.pallas_call(
            body, name=f"{name}_core{core}", grid=(R // tile,),
            in_specs=[pl.BlockSpec((4, None, tile, C), lambda i: (0, core, i, 0)), pl.BlockSpec((4, tile, C), lambda i: (0, i, 0))],
            out_specs=pl.BlockSpec((4, tile, C), lambda i: (0, i, 0)), out_shape=jax.ShapeDtypeStruct((4, R, C), out_dtype),
            compiler_params=_cparams(("parallel",)),
        )(by_chip, from_sibling)

    return lax.cond(lax.axis_index("c") == 0, for_core(0), for_core(1))


def _sum_leading(name, x, tile=None):
    n, R, C = x.shape
    tile = tile or _pick(R, (512, 256, 128, 64, 32, 16, 8))

    def body(x_ref, o_ref):
        acc = x_ref[0].astype(F32)
        for k in range(1, n):
            acc = acc + x_ref[k].astype(F32)
        o_ref[...] = acc

    return pl.pallas_call(
        body, name=name, grid=(R // tile,), in_specs=[pl.BlockSpec((n, tile, C), lambda i: (0, i, 0))],
        out_specs=pl.BlockSpec((tile, C), lambda i: (i, 0)), out_shape=jax.ShapeDtypeStruct((R, C), F32),
        compiler_params=_cparams(("parallel",)),
    )(x)


def _adamw_update(w_, g_, m_, v_):
    m2 = ADAM_B1 * m_ + (1.0 - ADAM_B1) * g_
    v2 = ADAM_B2 * v_ + (1.0 - ADAM_B2) * jnp.square(g_)
    m_hat = m2 / (1.0 - ADAM_B1 ** ADAM_STEP)
    v_hat = v2 / (1.0 - ADAM_B2 ** ADAM_STEP)
    delta = -ADAM_LR * (m_hat / (jnp.sqrt(v_hat) + ADAM_EPS) + ADAM_WD * w_)
    return delta, m2, v2


def _adamw(name, w, g, m, v):
    R, C = w.shape
    tile = _pick(R, (256, 128, 64, 32, 16, 8))
    outs, _ = _rowwise(name, lambda tv, bv: (list(_adamw_update(*tv)), []), [w, g, m, v], [], [(C, F32)] * 3, [], tile)
    return outs


def _adamw_many(name, ws, gs, ms, vs):
    n = len(ws)

    def body(*refs):
        ins, outs = refs[:4 * n], refs[4 * n:]
        for i in range(n):
            res = _adamw_update(ins[i][...], ins[n + i][...], ins[2 * n + i][...], ins[3 * n + i][...])
            for j in range(3):
                outs[j * n + i][...] = res[j]

    out_shape = [jax.ShapeDtypeStruct(w.shape, F32) for w in ws] * 3
    res = pl.pallas_call(body, name=name, out_shape=out_shape, compiler_params=_cparams())(*ws, *gs, *ms, *vs)
    return res[:n], res[n:2 * n], res[2 * n:]


def _pack_rows(arrs, lanes=128, row_mult=8):
    flat, places, off = [], [], 0
    for a_ in arrs:
        n = a_.size
        flat.append(a_.reshape(-1).astype(F32))
        places.append((off, n, a_.shape))
        off += n
    total = -(-off // (lanes * row_mult)) * (lanes * row_mult)
    if total > off:
        flat.append(jnp.zeros((total - off,), F32))
    return jnp.concatenate(flat).reshape(total // lanes, lanes), places


def _unpack_rows(packed, places):
    flat = packed.reshape(-1)
    return [flat[o:o + n].reshape(s) for (o, n, s) in places]


_WEIGHTS = ['w_ada', 'b_ada', 'w_in', 'b_in', 'g_ln_v', 'b_ln_v', 'w_spatial', 'b_spatial', 'mu_shift', 'w0', 'w_decay_up', 'a0',
            'w_aaa_up', 'w_gate_up', 'k_k', 'k_a', 'r_k', 'gn_gain', 'gn_bias', 'w_branch_a', 'w_branch_b', 'w_out', 'b_out',
            'ln1_g', 'ln1_b', 'w_ff1', 'b_ff1', 'w_ff2', 'b_ff2', 'ln2_g', 'ln2_b']
_BIG = {'w_ff1': (0, 512), 'w_ff2': (512, 512), 'w_out': (1024, 128), 'w_branch_a': (1152, 64), 'w_branch_b': (1216, 64), 'w_in': (1280, 640)}
_LATER_ROWS = 1280
_CUT_BY_COLS = ('w_ff1', 'w_in', 'w_branch_a', 'w_branch_b')
IN_SHARD = IN_COLS // _N_DEV
_LORA = {'w_decay_up': (0, LORA_W), 'w_aaa_up': (LORA_W, LORA_A), 'w_gate_up': (LORA_W + LORA_A, LORA_G)}
_COMM_DTYPE = jnp.bfloat16


def _pad_rows(a, rows):
    return jnp.pad(a, ((0, rows - a.shape[0]),) + ((0, 0),) * (a.ndim - 1))


def _pack_big(shards):
    blocks = []
    for n, (_, rows) in _BIG.items():
        a = shards[n].T if n in _CUT_BY_COLS else shards[n]
        blocks.append(_pad_rows(a.reshape(-1, D_MODEL), rows))
    return jnp.concatenate(blocks, axis=0)


def _unpack_big(block, like):
    out = {}
    for n, (r0, _) in _BIG.items():
        rr, cc = like[n].shape
        if n in _CUT_BY_COLS:
            out[n] = block[r0:r0 + rr * cc // D_MODEL].reshape(cc, rr).T
        else:
            out[n] = block[r0:r0 + rr]
    return out


def _to_padded(a, axis):
    g_end = 2 * G_WIDTH
    r_end = g_end + RW_USED
    take = lambda lo, hi: lax.slice_in_dim(a, lo, hi, axis=axis)
    zshape = list(a.shape)
    zshape[axis] = RW_COLS - RW_USED
    return jnp.concatenate([take(r_end, IN_COLS), take(g_end, r_end), jnp.zeros(zshape, a.dtype), take(0, g_end)], axis=axis)


def _from_padded(a, axis):
    take = lambda lo, hi: lax.slice_in_dim(a, lo, hi, axis=axis)
    return jnp.concatenate([take(2 * D_MODEL + RW_COLS, P_COLS), take(2 * D_MODEL, 2 * D_MODEL + RW_USED), take(0, 2 * D_MODEL)], axis=axis)


def _step(p, m, v, x, c, target):
    T = x.shape[0]
    xi, yi, ci = _place()
    me = 4 * xi + 2 * yi + ci
    tile = _pick(T, (512, 256))

    lane = jnp.arange(R_WIDTH)
    gsum = (lane[:, None] // R_HEAD == lane[None, :] // R_HEAD).astype(F32)
    expand = (jnp.arange(128)[:, None] == (lane[None, :] // (G_WIDTH // 8))).astype(F32)

    (c_act,), _ = _rowwise("silu_c", lambda tv, bv: ([tv[0] * _sigmoid(tv[0])], []), [c], [], [(D_MODEL, F32)], [], 1)
    small, places = _pack_rows([c_act, p['w_decay_up'], p['w_aaa_up'], p['w_gate_up']])
    small_all = _all_gather("gather_small", small)
    per_dev = [_unpack_rows(small_all[d], places) for d in range(_N_DEV)]
    c_act_all = _pad_rows(jnp.concatenate([pd[0] for pd in per_dev], axis=0), 16)
    lora_full = {n: jnp.concatenate([pd[i + 1] for pd in per_dev], axis=1) for i, n in enumerate(_LORA)}
    lora_pad = {n: jnp.zeros((LORA_PAD, R_WIDTH), F32).at[r0:r0 + nr].set(lora_full[n]) for n, (r0, nr) in _LORA.items()}

    big_names = list(_BIG)
    my_rows = _pack_big(p).astype(_MXU_DTYPE)
    b_in_p = _to_padded(p['b_in'], 1)
    mu_p = jnp.concatenate([p['mu_shift'], jnp.zeros((1, RW_COLS - RW_USED), F32)], axis=1)

    b_ada_mine = lax.dynamic_slice(p['b_ada'], (0, me * 768), (1, 768))
    mod_cols = _mm("ada_mod", c_act_all, p['w_ada'], "nn", bias=b_ada_mine)
    mod_all = _all_gather("gather_mod", mod_cols)
    mod = lax.dynamic_index_in_dim(mod_all, me, axis=1, keepdims=False).reshape(1, 6 * D_MODEL)
    sh1, sc1, gt1, sh2, sc2, gt2 = [mod[:, i * D_MODEL:(i + 1) * D_MODEL] for i in range(6)]

    (h,), _, w_in_all = _rowwise("modulate1", lambda tv, bv: (_seg_modulate(tv, bv), []), [x], [sc1, sh1], [(D_MODEL, _MXU_DTYPE)], [], tile,
                                 rider=(my_rows[_LATER_ROWS:], _gather_def, (0.0, 0.5, 1.0)))
    w_in_t = _to_padded(w_in_all[:, :IN_SHARD].reshape(IN_COLS, D_MODEL), 0)
    proj = _mm("in_proj", h, w_in_t, "nt", bias=b_in_p)
    ws = p['w_spatial']
    b_tg = jnp.zeros((G_CHUNK, 128), F32).at[:, :8].set(p['b_spatial'].T)
    gmlp_b = [p['g_ln_v'], p['b_ln_v'], ws, b_tg, expand]
    z_gmlp = (proj, 2 * G_WIDTH, 4)
    (ya,), _ = _rowwise("gmlp", lambda tv, bv: (_seg_gmlp(tv, bv), []), [z_gmlp], gmlp_b, [(G_WIDTH, F32)], [], G_CHUNK)
    z_rw = (proj, RW_COLS, 1)
    z_rw_halo = ("halo", proj, RW_COLS, 1)
    pre_b = [mu_p, p['w0'], lora_pad['w_decay_up'], p['a0'], lora_pad['w_aaa_up'], lora_pad['w_gate_up'], p['k_k'], p['k_a'], gsum]

    def pre_fwd(tv, bv):
        z_t, halo_t = tv
        return _seg_rwkv_pre([z_t, _shift_down(z_t, halo_t, pl.program_id(0) == 0)], bv), []

    pre_out, _ = _rowwise("rwkv_pre", pre_fwd, [z_rw, z_rw_halo], pre_b, [(R_WIDTH, F32)] * 7, [], tile)
    r_, lw_, k2_, v_, a_, b_, g_ = pre_out
    scan_in = (r_, lw_, k2_, v_, a_, b_)
    y_, states, inverses, later_all = _scan_fwd(*scan_in, rider=(my_rows[:_LATER_ROWS], _gather_def, (0.0, 0.875, 1.0)))

    def whole(n, rows):
        r0 = _BIG[n][0]
        return later_all[:, r0:r0 + rows].reshape(_N_DEV * rows, D_MODEL)

    w_ff1_t, w_ff2, w_out = whole('w_ff1', 512), whole('w_ff2', 512), whole('w_out', 128)
    w_ba_t = whole('w_branch_a', 64).reshape(D_MODEL, G_WIDTH)
    w_bb_t = whole('w_branch_b', 64).reshape(D_MODEL, R_WIDTH)
    post_b = [p['r_k'].reshape(1, R_WIDTH), p['gn_gain'], p['gn_bias'], gsum]
    (yb,), _ = _rowwise("rwkv_post", lambda tv, bv: (_seg_rwkv_post(tv, bv), []), [y_, r_, k2_, v_, g_], post_b, [(R_WIDTH, F32)], [], tile)
    pa = _mm("branch_a", ya, w_ba_t, "nt", out_dtype=_MXU_DTYPE)
    pb = _mm("branch_b", yb, w_bb_t, "nt", out_dtype=_MXU_DTYPE)
    gates = [(proj, D_MODEL, 0), (proj, D_MODEL, 1)]
    (merged,), _ = _rowwise("merge", lambda tv, bv: (_seg_merge(tv, bv), []), gates + [pa, pb], [], [(D_MODEL, _MXU_DTYPE)], [], tile)
    mix = _mm("out_proj", merged, w_out, "nn", bias=p['b_out'])
    mid_b = [gt1, p['ln1_g'], p['ln1_b'], sc2, sh2]
    (h1, h2in), _ = _rowwise("mid", lambda tv, bv: (_seg_mid(tv, bv), []), [x, mix], mid_b, [(D_MODEL, F32), (D_MODEL, _MXU_DTYPE)], [], tile)
    act = _mm("ff1", h2in, w_ff1_t, "nt", bias=p['b_ff1'], epi=lambda t: _seg_relu2([t], [])[0], epi_dtype=_MXU_DTYPE, raw=False)
    ff = _mm("ff2", act, w_ff2, "nn", bias=p['b_ff2'])

    def loss_fn(tv, bv):
        h1_t, ff_t, tgt = tv
        val, grads = jax.value_and_grad(lambda a0_, a1_, b0_, b1_, b2_: _seg_loss([a0_, a1_, tgt], [b0_, b1_, b2_]), argnums=(0, 1, 2, 3, 4))(h1_t, ff_t, *bv)
        return [grads[0], grads[1]], [grads[2], grads[3], grads[4], jnp.sum(grads[1], axis=0, keepdims=True), jnp.full((1, 128), val, F32)]

    (dh1_a, dff), (d_gt2, d_ln2_g, d_ln2_b, d_b_ff2, loss_row) = _rowwise(
        "loss", loss_fn, [h1, ff, target], [gt2, p['ln2_g'], p['ln2_b']], [(D_MODEL, F32), (D_MODEL, _MXU_DTYPE)], [(1, D_MODEL)] * 4 + [(1, 128)], tile)

    g = {}
    g['ln2_g'], g['ln2_b'], g['b_ff2'] = d_ln2_g, d_ln2_b, d_b_ff2
    gw = {}
    gw['w_ff2'] = _mm("g_w_ff2", act, dff, "tn")
    df1, g['b_ff1'] = _mm("d_act", dff, w_ff2, "nt", beside=act, epi=lambda d_, a_: d_ * (2.0 * jnp.sqrt(a_.astype(F32))),
                          epi_dtype=_MXU_DTYPE, colsum=True)
    gw['w_ff1'] = _mm("g_w_ff1", df1, h2in, "tn")
    dh2in = _mm("d_h2in", df1, w_ff1_t, "nn")
    (dx_a, dmix), (d_gt1, g['ln1_g'], g['ln1_b'], d_sc2, d_sh2), (g['b_out'],) = _rowwise_vjp(
        "mid_bwd", _seg_mid, [x, mix], mid_b, [dh1_a, dh2in], tile, [0, 1], [0, 1, 2, 3, 4], t_dtypes=[F32, _MXU_DTYPE], colsum=[1])
    gw['w_out'] = _mm("g_w_out", merged, dmix, "tn")
    dmerged = _mm("d_merged", dmix, w_out, "nt", out_dtype=_MXU_DTYPE)
    (dproj, dpa, dpb), _, (cs_gates,) = _rowwise_vjp(
        "merge_bwd", _seg_merge, gates + [pa, pb], [], [dmerged], tile, [0, 1, 2, 3], [], t_dtypes=[_MXU_DTYPE] * 3, colsum=[0],
        finish=lambda dts, sc: [jnp.concatenate(dts[:2], axis=1), dts[2], dts[3]], out_widths=[2 * D_MODEL, D_MODEL, D_MODEL],
        into={0: (lax.empty((T, P_COLS), _MXU_DTYPE), 0)})
    gw['w_branch_a'] = _mm("g_w_branch_a", dpa, ya, "tn")
    gw['w_branch_b'] = _mm("g_w_branch_b", dpb, yb, "tn")
    dya = _mm("d_ya", dpa, w_ba_t, "nn")
    dyb = _mm("d_yb", dpb, w_bb_t, "nn")
    def send_rows(names):
        parts = []
        for n in names:
            per_dev = gw[n].reshape(_N_DEV, -1, D_MODEL)
            parts.append(jnp.pad(per_dev, ((0, 0), (0, _BIG[n][1] - per_dev.shape[1]), (0, 0))))
        return jnp.concatenate(parts, axis=1) if len(parts) > 1 else parts[0]

    send_early = send_rows(big_names[:-1])
    (dy, dr1, dk1, dv1, dg_), (d_r_k, g['gn_gain'], g['gn_bias']), _, sibling_early = _rowwise_vjp(
        "rwkv_post_bwd", _seg_rwkv_post, [y_, r_, k2_, v_, g_], post_b, [dyb], tile, [0, 1, 2, 3, 4], [0, 1, 2],
        rider=(send_early, _sibling_def, (0.0, 1.0)))
    g['r_k'] = d_r_k
    partials_early = _chip_partials("chip_partials_early", send_early, sibling_early, 128, _COMM_DTYPE)
    dr2, dlw, dk2, dv2, da, db, landed_early = _scan_bwd(*scan_in, states, inverses, dy, rider=(partials_early, _chips_def, (0.0, 1.0)))
    pre_tile = 128
    last_step = T // pre_tile - 1

    def pre_prep(prim):
        z_t, halo_t = prim
        return [z_t, _shift_down(z_t, halo_t, pl.program_id(0) == last_step)]

    def pre_finish(dts, sc):
        dz_direct, dprev = dts
        (row_after,) = sc

        @pl.when(pl.program_id(0) == 0)
        def _():
            row_after[...] = jnp.zeros_like(row_after)

        dz = dz_direct + _shift_up(dprev, row_after[...])
        row_after[...] = dprev[:_HALO]
        return [dz]

    (dproj,), (d_mu, g['w0'], d_wd, g['a0'], d_wa, d_wg, g['k_k'], g['k_a']), (cs_rw,) = _rowwise_vjp(
        "rwkv_pre_bwd", _seg_rwkv_pre, [z_rw, z_rw_halo], pre_b, [[dr1, dr2], dlw, [dk1, dk2], [dv1, dv2], da, db, dg_], pre_tile,
        [0, 1], [0, 1, 2, 3, 4, 5, 6, 7], t_dtypes=[_MXU_DTYPE], colsum=[0], prep=pre_prep, finish=pre_finish,
        out_widths=[RW_COLS], reverse=True, scratch=[(_HALO, RW_COLS)], into={0: (dproj, 1)})
    g['mu_shift'] = d_mu[:, :RW_USED]
    for n, d_ in (('w_decay_up', d_wd), ('w_aaa_up', d_wa), ('w_gate_up', d_wg)):
        r0, nr = _LORA[n]
        g[n] = d_[r0:r0 + nr]
    (dproj,), (g['g_ln_v'], g['b_ln_v'], g['w_spatial'], d_b_tg), (cs_g,) = _rowwise_vjp(
        "gmlp_bwd", _seg_gmlp, [z_gmlp], gmlp_b, [dya], G_CHUNK, [0], [0, 1, 2, 3], t_dtypes=[_MXU_DTYPE], colsum=[0],
        into={0: (dproj, 4)})
    g['b_spatial'] = d_b_tg[:, :8].T
    g['b_in'] = _from_padded(jnp.concatenate([cs_gates, cs_rw, cs_g], axis=1), 1)
    small_names = [n for n in _WEIGHTS if n not in _BIG and n not in ('w_ada', 'b_ada')]
    packed_g, g_places = _pack_rows([g[n] for n in small_names] + [loss_row], row_mult=256)
    gw_in_t, small_all = _mm("g_w_in", dproj, h, "tn", rider=(packed_g, _gather_def, (0.0, 0.6, 1.0)))
    gw['w_in'] = _from_padded(gw_in_t, 0)
    send_late = send_rows(big_names[-1:])
    sibling_late = _exchange("pair_exchange_late", send_late, _sibling_def)
    partials_late = _chip_partials("chip_partials_late", send_late, sibling_late, 128, _COMM_DTYPE)
    dh, landed_late = _mm("d_h", dproj, w_in_t, "nn", rider=(partials_late, _chips_def, (0.0, 1.0)))

    def mod1_bwd(tv, bv):
        x_t, dh_t, dxa_t = tv
        (sc,) = bv
        return [dxa_t + dh_t * (1.0 + sc)], [jnp.sum(dh_t * x_t, axis=0, keepdims=True), jnp.sum(dh_t, axis=0, keepdims=True)]

    (grad_x,), (d_sc1, d_sh1) = _rowwise("modulate1_bwd", mod1_bwd, [x, dh, dx_a], [sc1], [(D_MODEL, F32)], [(1, D_MODEL)] * 2, tile)

    dmod = jnp.concatenate([d_sh1, d_sc1, d_gt1, d_sh2, d_sc2, d_gt2], axis=1).reshape(6 * D_MODEL // 128, 128)
    dmod_all = _all_gather("gather_dmod", dmod)
    g['b_ada'] = _sum_leading("sum_dmod", dmod_all).reshape(1, 6 * D_MODEL)
    dmod_mine = lax.dynamic_slice(dmod_all.reshape(_N_DEV, 6 * D_MODEL), (0, me * 768), (_N_DEV, 768))
    g_w_ada = _mm("g_w_ada", c_act_all, _pad_rows(dmod_mine, 16), "tn")

    small_sum = _unpack_rows(_sum_leading("sum_small", small_all), g_places)
    loss = small_sum.pop()[0, 0]
    for n, t in zip(small_names, small_sum, strict=True):
        g[n] = t
    for n in _LORA:
        g[n] = lax.dynamic_slice(g[n], (0, me * R_HEAD), (g[n].shape[0], R_HEAD))
    g['w_ada'] = g_w_ada

    summed = jnp.concatenate([_sum_leading("sum_early", landed_early), _sum_leading("sum_late", landed_late)], axis=0)
    g.update(_unpack_big(summed, p))

    delta, new_m, new_v = {}, {}, {}
    own_call = ['w_ada'] + big_names
    for n in own_call:
        if n == 'w_in':
            r0 = _BIG[n][0]
            outs_t = _adamw("adamw_" + n, p[n].T, summed[r0:r0 + IN_SHARD], m[n].T, v[n].T)
            delta[n], new_m[n], new_v[n] = [o.T for o in outs_t]
        else:
            delta[n], new_m[n], new_v[n] = _adamw("adamw_" + n, p[n], g[n], m[n], v[n])
    rest = [n for n in _WEIGHTS if n not in own_call]
    outs = _adamw_many("adamw_rest", *[[d[n].reshape(p[n].shape) for n in rest] for d in (p, g, m, v)])
    for d, o in zip((delta, new_m, new_v), outs, strict=True):
        d.update(zip(rest, o, strict=True))
    return loss, grad_x, g, delta, new_m, new_v


def kernel(x, c, w_ada, b_ada, w_in, b_in, g_ln_v, b_ln_v, w_spatial, b_spatial, mu_shift, w0, w_decay_up, a0, w_aaa_up, w_gate_up, k_k, k_a, r_k, gn_gain, gn_bias, w_branch_a, w_branch_b, w_out, b_out, ln1_g, ln1_b, w_ff1, b_ff1, w_ff2, b_ff2, ln2_g, ln2_b, loss_target, m_w_ada, m_b_ada, m_w_in, m_b_in, m_g_ln_v, m_b_ln_v, m_w_spatial, m_b_spatial, m_mu_shift, m_w0, m_w_decay_up, m_a0, m_w_aaa_up, m_w_gate_up, m_k_k, m_k_a, m_r_k, m_gn_gain, m_gn_bias, m_w_branch_a, m_w_branch_b, m_w_out, m_b_out, m_ln1_g, m_ln1_b, m_w_ff1, m_b_ff1, m_w_ff2, m_b_ff2, m_ln2_g, m_ln2_b, v_w_ada, v_b_ada, v_w_in, v_b_in, v_g_ln_v, v_b_ln_v, v_w_spatial, v_b_spatial, v_mu_shift, v_w0, v_w_decay_up, v_a0, v_w_aaa_up, v_w_gate_up, v_k_k, v_k_a, v_r_k, v_gn_gain, v_gn_bias, v_w_branch_a, v_w_branch_b, v_w_out, v_b_out, v_ln1_g, v_ln1_b, v_w_ff1, v_b_ff1, v_w_ff2, v_b_ff2, v_ln2_g, v_ln2_b):
    given = dict(locals())
    shapes = {n: given[n].shape for n in _WEIGHTS}
    def two_d(a_):
        a_ = a_[0]
        return a_.reshape(1, -1) if a_.ndim == 1 else a_
    p = {n: two_d(given[n]) for n in _WEIGHTS}
    m = {n: two_d(given["m_" + n]) for n in _WEIGHTS}
    v = {n: two_d(given["v_" + n]) for n in _WEIGHTS}
    loss, grad_x, g, delta, new_m, new_v = _step(p, m, v, x[0], c, loss_target[0])
    outs = [loss, grad_x[None]]
    for d in (g, delta, new_m, new_v):
        outs += [d[n].reshape(shapes[n]) for n in _WEIGHTS]
    return tuple(outs)
```

```python
import functools

import jax
import jax.numpy as jnp
from jax import lax
from jax.experimental import pallas as pl
from jax.experimental.pallas import tpu as pltpu

F32 = jnp.float32
_MXU_DTYPE = jnp.bfloat16
_HI = lax.Precision.HIGHEST
_VMEM_LIMIT = 48 * 1024 * 1024
_MESH_ID = pl.DeviceIdType.MESH
_N_DEV = 8

D_MODEL = 1024
G_WIDTH = 512
G_CHUNK = 128
R_WIDTH = 512
R_HEADS = 8
R_HEAD = 64
LORA_W, LORA_A, LORA_G = 32, 32, 96
D_FF = 4096
ALPHA = 2.0 ** 0.25
LN_EPS = 1e-5
GN_EPS = 64e-5
SCAN_CHUNK = 64
SCAN_PER_STEP = 4
ADAM_LR, ADAM_B1, ADAM_B2, ADAM_EPS, ADAM_WD, ADAM_STEP = 0.001, 0.9, 0.999, 1e-08, 0.01, 10

P_COLS = 5120
RW_COLS = 2048
RW_USED = 3 * R_WIDTH + LORA_W + LORA_A + LORA_G
LORA_PAD = 256
IN_COLS = 2 * G_WIDTH + RW_USED + 2 * D_MODEL


def _cparams(sem=None, **kw):
    if sem is not None:
        kw["dimension_semantics"] = sem
    return pltpu.CompilerParams(vmem_limit_bytes=_VMEM_LIMIT, **kw)


def _dot(a, b, dims=(((1,), (0,)), ((), ())), hi=False):
    if hi:
        return lax.dot_general(a.astype(F32), b.astype(F32), dims, precision=_HI, preferred_element_type=F32)
    return lax.dot_general(a.astype(_MXU_DTYPE), b.astype(_MXU_DTYPE), dims, preferred_element_type=F32)


_NN = (((1,), (0,)), ((), ()))
_NT = (((1,), (1,)), ((), ()))
_TN = (((0,), (0,)), ((), ()))


def _pick(n, pref):
    for t in pref:
        if n % t == 0:
            return t
    return n


def _mm(name, a, b, mode, bias=None, out_dtype=F32, epi=None, epi_dtype=None, raw=True, beside=None, colsum=False, rider=None,
        tm=None, tn=None, tk=None):
    if mode == "nn":
        (M, K), (_, N) = a.shape, b.shape
    elif mode == "nt":
        (M, K), (N, _) = a.shape, b.shape
    else:
        (K, M), (_, N) = a.shape, b.shape
    tm = tm or _pick(M, (2048, 1280, 1024, 512, 256, 128, 64, 32, 16, 8))
    tn = tn or _pick(N, (1024, 512, 640, 384, 256, 128))
    tk = tk or _pick(K, (1024, 512, 256, 128))
    nk = K // tk
    dims = {"nn": _NN, "nt": _NT, "tn": _TN}[mode]
    a_spec = pl.BlockSpec((tk, tm), lambda i, j, k: (k, i)) if mode == "tn" else pl.BlockSpec((tm, tk), lambda i, j, k: (i, k))
    b_spec = pl.BlockSpec((tn, tk), lambda i, j, k: (j, k)) if mode == "nt" else pl.BlockSpec((tk, tn), lambda i, j, k: (k, j))
    o_spec = pl.BlockSpec((tm, tn), lambda i, j, k: (i, j))
    has_bias, has_beside = bias is not None, beside is not None
    two = epi is not None and not has_beside and raw
    only_epi = epi is not None and not has_beside and not raw
    grid = (M // tm, N // tn, nk)
    ride_shape, ride_sems, _ = rider[1](rider[0]) if rider else (None, [], None)

    def body(*refs):
        refs = list(refs)
        n_in = 2 + has_bias + has_beside
        if rider:
            sem_refs = [refs.pop() for _ in ride_sems][::-1]
            ride_out = refs.pop(n_in + 1 + 1 + two)
            ride_in = refs.pop(n_in)
            step = (pl.program_id(0) * grid[1] + pl.program_id(1)) * grid[2] + pl.program_id(2)
            _ride(rider, step, grid[0] * grid[1] * grid[2], ride_in, ride_out, sem_refs)
        a_ref, b_ref = refs[0], refs[1]
        bias_ref = refs[2] if has_bias else None
        beside_ref = refs[n_in - 1] if has_beside else None
        outs = refs[n_in:]
        o_ref, acc_ref = outs[0], outs[-1]
        k = pl.program_id(2)

        @pl.when(k == 0)
        def _():
            acc_ref[...] = jnp.zeros_like(acc_ref)

        acc_ref[...] += _dot(a_ref[...], b_ref[...], dims)

        @pl.when(k == nk - 1)
        def _():
            res = acc_ref[...]
            if has_bias:
                res = res + bias_ref[...]
            if has_beside:
                val = epi(res, beside_ref[...])
                o_ref[...] = val.astype(o_ref.dtype)
                if colsum:
                    outs[1][...] = jnp.broadcast_to(jnp.sum(val, axis=0, keepdims=True), outs[1].shape)
            elif only_epi:
                o_ref[...] = epi(res).astype(o_ref.dtype)
            else:
                o_ref[...] = res.astype(o_ref.dtype)
                if two:
                    outs[1][...] = epi(res).astype(outs[1].dtype)

    in_specs = [a_spec, b_spec]
    args = [a, b]
    if has_bias:
        in_specs.append(pl.BlockSpec((1, tn), lambda i, j, k: (0, j)))
        args.append(bias)
    if has_beside:
        in_specs.append(o_spec)
        args.append(beside)
    out_shape = [jax.ShapeDtypeStruct((M, N), epi_dtype if (has_beside or only_epi) else out_dtype)]
    out_specs = [o_spec]
    if two:
        out_shape.append(jax.ShapeDtypeStruct((M, N), epi_dtype))
        out_specs.append(o_spec)
    if colsum:
        assert has_beside and not rider
        out_shape.append(jax.ShapeDtypeStruct((grid[0] * 8, N), F32))
        out_specs.append(pl.BlockSpec((8, tn), lambda i, j, k: (i, j)))
    if rider:
        in_specs.append(_HBM)
        args.append(rider[0])
        out_shape.append(ride_shape)
        out_specs.append(_HBM)
    res = pl.pallas_call(
        body, name=name, grid=grid, in_specs=in_specs, out_specs=out_specs, out_shape=out_shape,
        scratch_shapes=[pltpu.VMEM((tm, tn), F32)] + list(ride_sems),
        compiler_params=_cparams(("arbitrary",) * 3 if rider else ("parallel", "parallel", "arbitrary")),
    )(*args)
    if colsum:
        return res[0], _sum_leading(name + "_colsum", res[1].reshape(grid[0], 8, N))[0:1]
    return res if (two or rider) else res[0]


_HALO = 8


def _rowwise(name, fn, tiled, bcast, tiled_out, red_out, tile, reverse=False, scratch=(), rider=None):
    tiled = [t if isinstance(t, tuple) else (t, t.shape[1], 0) for t in tiled]
    T = next(t[0] for t in tiled if not isinstance(t[0], str)).shape[0]
    n = T // tile
    nt, nb, nto, nsc, nro = len(tiled), len(bcast), len(tiled_out), len(scratch), len(red_out)
    ride_shape, ride_sems, _ = rider[1](rider[0]) if rider else (None, [], None)
    nr = 1 if rider else 0
    into = [(j, t[2], t[3]) for j, t in enumerate(tiled_out) if len(t) == 4]
    na = len(into)

    def row_block(i):
        return n - 1 - i if reverse else i

    def body(*refs):
        i2 = nt + nb
        o0 = i2 + nr + na
        o1, o2 = o0 + nto, o0 + nto + nro
        s0 = o2 + nr
        if rider:
            _ride(rider, pl.program_id(0), n, refs[i2], refs[o2], refs[s0 + nsc:])
        t_refs, b_refs = refs[:nt], refs[nt:i2]
        to_refs, ro_refs = refs[o0:o1], refs[o1:o2]
        extra = (list(refs[s0:s0 + nsc]),) if nsc else ()
        touts, routs = fn([r[...] for r in t_refs], [r[...] for r in b_refs], *extra)
        for r, v in zip(to_refs, touts, strict=True):
            r[...] = v.astype(r.dtype)
        if ro_refs:
            i = pl.program_id(0)

            @pl.when(i == 0)
            def _():
                for r, v in zip(ro_refs, routs, strict=True):
                    r[...] = v.astype(F32)

            @pl.when(i > 0)
            def _():
                for r, v in zip(ro_refs, routs, strict=True):
                    r[...] += v.astype(F32)

    def whole(shape):
        nd = len(shape)
        return pl.BlockSpec(tuple(shape), lambda i: (0,) * nd)

    per_tile = tile // _HALO
    in_specs, arrays = [], []
    for t in tiled:
        if isinstance(t[0], str):
            _, arr, w, cb = t
            in_specs.append(pl.BlockSpec((_HALO, w), functools.partial(lambda i, cb: (jnp.maximum(row_block(i) * per_tile - 1, 0), cb), cb=cb)))
        else:
            arr, w, cb = t
            in_specs.append(pl.BlockSpec((tile, w), functools.partial(lambda i, cb: (row_block(i), cb), cb=cb)))
        arrays.append(arr)
    in_specs += [whole(b.shape) for b in bcast]
    out_specs, out_shape = [], []
    for t in tiled_out:
        cb = t[3] if len(t) == 4 else 0
        out_specs.append(pl.BlockSpec((tile, t[0]), functools.partial(lambda i, cb: (row_block(i), cb), cb=cb)))
        out_shape.append(jax.ShapeDtypeStruct(t[2].shape if len(t) == 4 else (T, t[0]), t[1]))
    out_specs += [whole(s) for s in red_out]
    out_shape += [jax.ShapeDtypeStruct(tuple(s), F32) for s in red_out]
    ride_args = []
    if rider:
        in_specs.append(_HBM)
        out_specs.append(_HBM)
        out_shape.append(ride_shape)
        ride_args = [rider[0]]
    in_specs += [pl.BlockSpec(memory_space=pl.ANY)] * na
    aliases = {nt + nb + nr + k: j for k, (j, _, _) in enumerate(into)}
    res = pl.pallas_call(
        body, name=name, grid=(n,), in_specs=in_specs, out_specs=out_specs, out_shape=out_shape,
        scratch_shapes=[pltpu.VMEM(tuple(s), F32) for s in scratch] + list(ride_sems),
        input_output_aliases=aliases, compiler_params=_cparams(("arbitrary",)),
    )(*arrays, *bcast, *ride_args, *[buf for (_, buf, _) in into])
    if rider:
        return list(res[:nto]), list(res[nto:nto + nro]), res[-1]
    return list(res[:nto]), list(res[nto:])


def _rowwise_vjp(name, f, tiled, bcast, cts, tile, wrt_t, wrt_b, t_dtypes=None, colsum=(), prep=None, finish=None,
                 out_widths=None, reverse=False, scratch=(), rider=None, into=None):
    tiled = [t if isinstance(t, tuple) else (t, t.shape[1], 0) for t in tiled]
    npr = len(tiled)
    t_dtypes = t_dtypes or [F32] * len(wrt_t)
    groups = [c if isinstance(c, list) else [c] for c in cts]
    cts = [a_ for grp in groups for a_ in grp]

    def fn(tv, bv, sc=None):
        prim, flat_ct = tv[:npr], list(tv[npr:])
        if prep is not None:
            prim = prep(prim)
        ct = []
        for grp in groups:
            parts = [flat_ct.pop(0).astype(F32) for _ in grp]
            ct.append(functools.reduce(lambda p_, q_: p_ + q_, parts))

        def g(dt_vals, db_vals):
            full_t, full_b = list(prim), list(bv)
            for i, v in zip(wrt_t, dt_vals, strict=True):
                full_t[i] = v
            for j, v in zip(wrt_b, db_vals, strict=True):
                full_b[j] = v
            return f(full_t, full_b)

        outs, pull = jax.vjp(g, [prim[i].astype(F32) for i in wrt_t], [bv[j] for j in wrt_b])
        dts, dbs = pull([c.astype(o.dtype) for c, o in zip(ct, outs, strict=True)])
        if finish is not None:
            dts = finish(dts, sc)
        sums = [jnp.sum(dts[i].astype(F32), axis=0, keepdims=True) for i in colsum]
        return dts, list(dbs) + sums

    widths = out_widths or [tiled[i][1] for i in wrt_t]
    tiled_out = [(w, dt) + tuple((into or {}).get(j, ())) for j, (w, dt) in enumerate(zip(widths, t_dtypes, strict=True))]
    red_out = [bcast[j].shape for j in wrt_b] + [(1, widths[i]) for i in colsum]
    res = _rowwise(name, fn, tiled + list(cts), bcast, tiled_out, red_out, tile, reverse=reverse, scratch=scratch, rider=rider)
    dts, reds = res[0], res[1]
    nb = len(wrt_b)
    return (dts, reds[:nb], reds[nb:]) + tuple(res[2:])


def _layer_norm(x, g, b, eps):
    mu = jnp.mean(x, axis=-1, keepdims=True)
    xc = x - mu
    var = jnp.mean(xc * xc, axis=-1, keepdims=True)
    return xc * lax.rsqrt(var + eps) * g + b


def _gelu_tanh(x):
    return 0.5 * x * (1.0 + jnp.tanh(0.7978845608028654 * (x + 0.044715 * (x * x * x))))


def _sigmoid(x):
    return 1.0 / (1.0 + jnp.exp(-x))


def _seg_modulate(tv, bv):
    (x,), (sc, sh) = tv, bv
    return [x * (1.0 + sc) + sh]


def _seg_gmlp(tv, bv):
    (z,), (g_ln, b_ln, ws, b_tg, expand) = tv, bv
    bias_full = _dot(b_tg, expand, hi=True)
    zz = _gelu_tanh(z)
    u, v = zz[:, :G_WIDTH], zz[:, G_WIDTH:]
    v = _layer_norm(v, g_ln, b_ln, LN_EPS)
    row = lax.broadcasted_iota(jnp.int32, (G_CHUNK, G_CHUNK), 0)
    col = lax.broadcasted_iota(jnp.int32, (G_CHUNK, G_CHUNK), 1)
    causal = col <= row
    first_group = lax.broadcasted_iota(jnp.int32, (G_CHUNK, 128), 1) < 64
    parts = []
    for p in range(4):
        vp = v[:, 128 * p:128 * (p + 1)]
        s_even = _dot(jnp.where(causal, ws[2 * p], 0.0), vp)
        s_odd = _dot(jnp.where(causal, ws[2 * p + 1], 0.0), vp)
        parts.append(jnp.where(first_group, s_even, s_odd))
    s = jnp.concatenate(parts, axis=1) + bias_full
    return [u * s]


def _split2(x):
    hi = x.astype(_MXU_DTYPE)
    return hi, (x - hi.astype(F32)).astype(_MXU_DTYPE)


@jax.custom_vjp
def _group_sum(x, ones_blocks):
    hi, lo = _split2(x)
    return _dot(hi, ones_blocks) + _dot(lo, ones_blocks)


def _group_sum_fwd(x, ones_blocks):
    return _group_sum(x, ones_blocks), ones_blocks


def _group_sum_bwd(ones_blocks, ct):
    return _group_sum(ct, ones_blocks), jnp.zeros_like(ones_blocks)


_group_sum.defvjp(_group_sum_fwd, _group_sum_bwd)


def _shift_down(z, halo, is_first):
    _, W = z.shape
    rolled = pltpu.roll(z, 1, 0)
    before = jnp.where(is_first, 0.0, pltpu.roll(halo, 1, 0))
    top_row = lax.broadcasted_iota(jnp.int32, (_HALO, W), 0) == 0
    return jnp.concatenate([jnp.where(top_row, before, rolled[:_HALO]), rolled[_HALO:]], axis=0)


def _shift_up(d, after):
    tile, W = d.shape
    rolled = pltpu.roll(d, tile - 1, 0)
    last_row = lax.broadcasted_iota(jnp.int32, (_HALO, W), 0) == _HALO - 1
    bottom = jnp.where(last_row, pltpu.roll(after, _HALO - 1, 0), rolled[tile - _HALO:])
    return jnp.concatenate([rolled[:tile - _HALO], bottom], axis=0)


def _seg_rwkv_pre(tv, bv):
    (z, prev), (mu, w0, wd, a0, wa, wg, k_k, k_a, gsum) = tv, bv
    zs = z + (prev - z) * mu
    r, k, v = zs[:, 0:512], zs[:, 512:1024], zs[:, 1024:1536]
    zl = zs[:, 3 * R_WIDTH:3 * R_WIDTH + LORA_PAD]
    x = w0 + _dot(jnp.tanh(zl), wd)
    softplus = jnp.maximum(-x, 0.0) + jnp.log(1.0 + jnp.exp(-jnp.abs(x)))
    lw = -jnp.exp(-softplus - 0.5)
    a = _sigmoid(a0 + _dot(zl, wa))
    g = _dot(_sigmoid(zl), wg)
    kk = k * k_k
    nrm = jnp.sqrt(_group_sum(kk * kk, gsum))
    kk = kk / jnp.maximum(nrm, 1e-12)
    k2 = k * (1.0 + (a - 1.0) * k_a)
    return [r, lw, k2, v, -kk, kk * a, g]


def _seg_rwkv_post(tv, bv):
    (y, r, k2, v, g), (r_k, gain, bias, gsum) = tv, bv
    mu = _group_sum(y, gsum) * (1.0 / R_HEAD)
    yc = y - mu
    var = _group_sum(yc * yc, gsum) * (1.0 / R_HEAD)
    yn = yc * lax.rsqrt(var + GN_EPS) * gain + bias
    bonus = _group_sum(r * k2 * r_k, gsum) * v
    return [(yn + bonus) * g]


def _seg_merge(tv, bv):
    (ga, gb, pa, pb), () = tv, bv
    return [_sigmoid(ga) * pa + _sigmoid(gb) * pb]


def _seg_mid(tv, bv):
    (x, mix), (gt1, g1, b1, sc2, sh2) = tv, bv
    h1 = _layer_norm(ALPHA * x + gt1 * mix, g1, b1, LN_EPS)
    return [h1, h1 * (1.0 + sc2) + sh2]


def _seg_relu2(tv, bv):
    (f1,), () = tv, bv
    return [jnp.square(jnp.maximum(f1, 0.0))]


def _seg_loss(tv, bv):
    (h1, ff, target), (gt2, g2, b2) = tv, bv
    out = _layer_norm(ALPHA * h1 + gt2 * ff, g2, b2, LN_EPS)
    err = jnp.square(out - target)
    return 0.5 * jnp.sum(jnp.mean(err, axis=-1))


_BNN = (((2,), (1,)), ((0,), (0,)))
_BNT = (((2,), (2,)), ((0,), (0,)))
_BTN = (((1,), (1,)), ((0,), (0,)))


def _tri_dot(x, dims):
    L = x.shape[0]
    tri = (lax.broadcasted_iota(jnp.int32, (L, L), 1) <= lax.broadcasted_iota(jnp.int32, (L, L), 0)).astype(F32)
    hi, lo = _split2(x)
    return _dot(tri, hi, dims) + _dot(tri, lo, dims)


@jax.custom_vjp
def _running_sum(x):
    return _tri_dot(x, _NN)


_running_sum.defvjp(lambda x: (_tri_dot(x, _NN), None), lambda _, ct: (_tri_dot(ct, _TN),))


def _cut_heads(x):
    return jnp.stack([x[:, R_HEAD * h:R_HEAD * (h + 1)] for h in range(R_HEADS)])


def _join_heads(x):
    return jnp.concatenate([x[h] for h in range(R_HEADS)], axis=1)


@jax.custom_vjp
def _split_heads(x):
    return _cut_heads(x)


@jax.custom_vjp
def _merge_heads(x):
    return _join_heads(x)


_split_heads.defvjp(lambda x: (_cut_heads(x), None), lambda _, ct: (_join_heads(ct),))
_merge_heads.defvjp(lambda x: (_join_heads(x), None), lambda _, ct: (_cut_heads(ct),))


def _inverse_pullback(inv, ct):
    return _dot(_dot(inv, ct, _BTN), inv, _BNT)


@jax.custom_vjp
def _unit_lower_inverse(n_mat):
    H, L, _ = n_mat.shape
    eye = lax.broadcasted_iota(jnp.int32, (H, L, L), 1) == lax.broadcasted_iota(jnp.int32, (H, L, L), 2)
    inv = jnp.where(eye, 1.0, 0.0) + n_mat
    pw = n_mat
    n = 2
    while n < L:
        pw = _dot(pw, pw, _BNN)
        inv = inv + _dot(inv, pw, _BNN)
        n *= 2
    return inv


def _unit_lower_inverse_fwd(n_mat):
    inv = _unit_lower_inverse(n_mat)
    return inv, inv


_unit_lower_inverse.defvjp(_unit_lower_inverse_fwd, lambda inv, ct: (_inverse_pullback(inv, ct),))


@jax.custom_vjp
def _known_inverse(n_mat, inv):
    return inv


_known_inverse.defvjp(lambda n_mat, inv: (inv, inv), lambda inv, ct: (_inverse_pullback(inv, ct), jnp.zeros_like(inv)))


def _scan_chunk(r, lw, k, v, a, b, s0, inv=None, with_inverse=False):
    L, H = r.shape[0], R_HEADS
    cs = _running_sum(lw)
    cs_end = cs[L - 1:L, :]
    p, p_inv, to_end = jnp.exp(cs), jnp.exp(-cs), jnp.exp(cs_end - cs)
    at, bt, kt, rt = [_split_heads(t) for t in (a * jnp.exp(cs - lw), b * p_inv, k * p_inv, r * p)]
    b_end, k_end, v = [_split_heads(t) for t in (b * to_end, k * to_end, v)]
    row = lax.broadcasted_iota(jnp.int32, (H, L, L), 1)
    col = lax.broadcasted_iota(jnp.int32, (H, L, L), 2)
    incl, strict = col <= row, col < row
    a_ab = jnp.where(strict, _dot(at, bt, _BNT), 0.0)
    a_ak = jnp.where(strict, _dot(at, kt, _BNT), 0.0)
    a_rb = jnp.where(incl, _dot(rt, bt, _BNT), 0.0)
    a_rk = jnp.where(incl, _dot(rt, kt, _BNT), 0.0)
    inv = _unit_lower_inverse(a_ab) if inv is None else _known_inverse(a_ab, inv)
    u = _dot(inv, _dot(at, s0, _BNT) + _dot(a_ak, v, _BNN), _BNN)
    y = _merge_heads(_dot(rt, s0, _BNT) + _dot(a_rb, u, _BNN) + _dot(a_rk, v, _BNN))
    s1 = s0 * _split_heads(jnp.exp(cs_end)) + _dot(u, b_end, _BTN) + _dot(v, k_end, _BTN)
    return (y, s1, inv) if with_inverse else (y, s1)


def _scan_fwd(r, lw, k, v, a, b, rider):
    T = r.shape[0]
    H, N, L, P = R_HEADS, R_HEAD, SCAN_CHUNK, SCAN_PER_STEP
    nc = T // L
    steps = nc // P
    ride_shape, ride_sems, _ = rider[1](rider[0])

    def body(r_ref, lw_ref, k_ref, v_ref, a_ref, b_ref, ride_in, y_ref, st_ref, inv_ref, ride_out, s_ref, *sem_refs):
        _ride(rider, pl.program_id(0), steps, ride_in, ride_out, sem_refs)

        @pl.when(pl.program_id(0) == 0)
        def _():
            s_ref[...] = jnp.zeros_like(s_ref)

        s0 = s_ref[...]
        for j in range(P):
            rows = pl.ds(j * L, L)
            st_ref[j] = s0
            y, s0, inv = _scan_chunk(*[t[rows, :] for t in (r_ref, lw_ref, k_ref, v_ref, a_ref, b_ref)], s0, with_inverse=True)
            y_ref[rows, :] = y
            inv_ref[j] = inv
        s_ref[...] = s0

    blk = pl.BlockSpec((P * L, R_WIDTH), lambda c: (c, 0))
    per_chunk = pl.BlockSpec((P, H, N, N), lambda c: (c, 0, 0, 0))
    return pl.pallas_call(
        body, name="scan_fwd", grid=(steps,), in_specs=[blk] * 6 + [_HBM], out_specs=[blk, per_chunk, per_chunk, _HBM],
        out_shape=[jax.ShapeDtypeStruct((T, R_WIDTH), F32)] + [jax.ShapeDtypeStruct((nc, H, N, N), F32)] * 2 + [ride_shape],
        scratch_shapes=[pltpu.VMEM((H, N, N), F32)] + list(ride_sems),
        compiler_params=_cparams(("arbitrary",)),
    )(r, lw, k, v, a, b, rider[0])


def _scan_bwd(r, lw, k, v, a, b, states, inverses, dy, rider):
    T = r.shape[0]
    H, N, L, P = R_HEADS, R_HEAD, SCAN_CHUNK, SCAN_PER_STEP
    nc = T // L
    steps = nc // P
    ride_shape, ride_sems, _ = rider[1](rider[0])

    def body(r_ref, lw_ref, k_ref, v_ref, a_ref, b_ref, st_ref, inv_ref, dy_ref, ride_in,
             dr_ref, dlw_ref, dk_ref, dv_ref, da_ref, db_ref, ride_out, ds_ref, *sem_refs):
        _ride(rider, pl.program_id(0), steps, ride_in, ride_out, sem_refs)

        @pl.when(pl.program_id(0) == 0)
        def _():
            ds_ref[...] = jnp.zeros_like(ds_ref)

        ds = ds_ref[...]
        for j in reversed(range(P)):
            rows = pl.ds(j * L, L)
            args = [t[rows, :] for t in (r_ref, lw_ref, k_ref, v_ref, a_ref, b_ref)] + [st_ref[j]]
            inv = inv_ref[j]
            _, pull = jax.vjp(lambda *xs, inv=inv: _scan_chunk(*xs, inv=inv), *args)
            grads = pull((dy_ref[rows, :], ds))
            for o_ref, g_ in zip((dr_ref, dlw_ref, dk_ref, dv_ref, da_ref, db_ref), grads[:6], strict=True):
                o_ref[rows, :] = g_
            ds = grads[6]
        ds_ref[...] = ds

    blk = pl.BlockSpec((P * L, R_WIDTH), lambda c: (steps - 1 - c, 0))
    per_chunk = pl.BlockSpec((P, H, N, N), lambda c: (steps - 1 - c, 0, 0, 0))
    return pl.pallas_call(
        body, name="scan_bwd", grid=(steps,), in_specs=[blk] * 6 + [per_chunk, per_chunk, blk, _HBM], out_specs=[blk] * 6 + [_HBM],
        out_shape=[jax.ShapeDtypeStruct((T, R_WIDTH), F32)] * 6 + [ride_shape],
        scratch_shapes=[pltpu.VMEM((H, N, N), F32)] + list(ride_sems),
        compiler_params=_cparams(("arbitrary",)),
    )(r, lw, k, v, a, b, states, inverses, dy, rider[0])


def _place():
    x, y, c = lax.axis_index("x"), lax.axis_index("y"), lax.axis_index("c")
    return x, y, c


def _gather_def(block):
    R, C = block.shape

    def phases(x_ref, out_ref, send_sems, recv_sems, local_sem):
        x, y, c = _place()
        me, sibling = (x, y, c), (x, y, 1 - c)
        chips = [(1 - x, y), (x, 1 - y), (1 - x, 1 - y)]

        def slot(px, py, pc):
            return out_ref.at[4 * px + 2 * py + pc]

        def copy(k, blk, to, src=None):
            return pltpu.make_async_remote_copy(
                src_ref=slot(*blk) if src is None else src, dst_ref=slot(*blk),
                send_sem=send_sems.at[k], recv_sem=recv_sems.at[k], device_id=to, device_id_type=_MESH_ID)

        mine = pltpu.make_async_copy(x_ref, slot(*me), local_sem)
        first = [copy(0, me, sibling, src=x_ref)]
        first += [copy(1 + j, me, (*chip, c), src=x_ref) for j, chip in enumerate(chips)]
        passed = [copy(4 + j, (*chip, c), sibling) for j, chip in enumerate(chips)]

        def begin():
            mine.start()
            for cp in first:
                cp.start()

        def forward():
            for j, chip in enumerate(chips):
                copy(1 + j, (*chip, c), me).wait_recv()
                passed[j].start()

        def finish():
            copy(0, sibling, me).wait_recv()
            for j, chip in enumerate(chips):
                copy(4 + j, (*chip, 1 - c), me).wait_recv()
            for cp in first + passed:
                cp.wait_send()
            mine.wait()

        return [begin, forward, finish]

    sems = [pltpu.SemaphoreType.DMA((7,)), pltpu.SemaphoreType.DMA((7,)), pltpu.SemaphoreType.DMA]
    return jax.ShapeDtypeStruct((_N_DEV, R, C), block.dtype), sems, phases


def _sibling_def(blocks):
    _, R, C = blocks.shape

    def phases(x_ref, out_ref, send_sems, recv_sems):
        x, y, c = _place()
        copies = [pltpu.make_async_remote_copy(
            src_ref=x_ref.at[2 * q + (1 - c)], dst_ref=out_ref.at[q], send_sem=send_sems.at[q], recv_sem=recv_sems.at[q],
            device_id=(x, y, 1 - c), device_id_type=_MESH_ID) for q in range(4)]

        def begin():
            for cp in copies:
                cp.start()

        def finish():
            for cp in copies:
                cp.wait()

        return [begin, finish]

    return jax.ShapeDtypeStruct((4, R, C), blocks.dtype), [pltpu.SemaphoreType.DMA((4,)), pltpu.SemaphoreType.DMA((4,))], phases


def _chips_def(partials):
    def phases(x_ref, out_ref, send_sems, recv_sems, local_sem):
        x, y, c = _place()
        my_chip = 2 * x + y
        mine = pltpu.make_async_copy(x_ref.at[my_chip], out_ref.at[my_chip], local_sem)
        copies = []
        for rel in range(1, 4):
            px, py = (1 - x if (rel >> 1) & 1 else x), (1 - y if rel & 1 else y)
            copies.append(pltpu.make_async_remote_copy(
                src_ref=x_ref.at[2 * px + py], dst_ref=out_ref.at[my_chip],
                send_sem=send_sems.at[rel - 1], recv_sem=recv_sems.at[rel - 1],
                device_id=(px, py, c), device_id_type=_MESH_ID))

        def begin():
            mine.start()
            for cp in copies:
                cp.start()

        def finish():
            for cp in copies:
                cp.wait()
            mine.wait()

        return [begin, finish]

    sems = [pltpu.SemaphoreType.DMA((3,)), pltpu.SemaphoreType.DMA((3,)), pltpu.SemaphoreType.DMA]
    return jax.ShapeDtypeStruct(partials.shape, partials.dtype), sems, phases


_HBM = pl.BlockSpec(memory_space=pltpu.HBM)


def _exchange(name, array, definition):
    out_shape, sems, phases = definition(array)

    def body(x_ref, out_ref, *sem_refs):
        for phase in phases(x_ref, out_ref, *sem_refs):
            phase()

    return pl.pallas_call(body, name=name, in_specs=[_HBM], out_specs=_HBM, out_shape=out_shape, scratch_shapes=sems)(array)


def _ride(rider, step, nsteps, x_ref, out_ref, sem_refs):
    array, definition, fractions = rider
    for phase, frac in zip(definition(array)[2](x_ref, out_ref, *sem_refs), fractions, strict=True):
        pl.when(step == min(int(frac * nsteps), nsteps - 1))(phase)


def _all_gather(name, block):
    return _exchange(name, block, _gather_def)


def _chip_partials(name, blocks, from_sibling, tile, out_dtype):
    _, R, C = blocks.shape

    def body(x_ref, s_ref, o_ref):
        c = lax.axis_index("c")
        for q in range(4):
            o_ref[q] = (x_ref[2 * q + c] + s_ref[q]).astype(o_ref.dtype)

    return pl.pallas_call(
        body, name=name, grid=(R // tile,),
        in_specs=[pl.BlockSpec((_N_DEV, tile, C), lambda i: (0, i, 0)), pl.BlockSpec((4, tile, C), lambda i: (0, i, 0))],
        out_specs=pl.BlockSpec((4, tile, C), lambda i: (0, i, 0)), out_shape=jax.ShapeDtypeStruct((4, R, C), out_dtype),
        compiler_params=_cparams(("parallel",)),
    )(blocks, from_sibling)


def _sum_leading(name, x, tile=None):
    n, R, C = x.shape
    tile = tile or _pick(R, (512, 256, 128, 64, 32, 16, 8))

    def body(x_ref, o_ref):
        acc = x_ref[0].astype(F32)
        for k in range(1, n):
            acc = acc + x_ref[k].astype(F32)
        o_ref[...] = acc

    return pl.pallas_call(
        body, name=name, grid=(R // tile,), in_specs=[pl.BlockSpec((n, tile, C), lambda i: (0, i, 0))],
        out_specs=pl.BlockSpec((tile, C), lambda i: (i, 0)), out_shape=jax.ShapeDtypeStruct((R, C), F32),
        compiler_params=_cparams(("parallel",)),
    )(x)


def _adamw_update(w_, g_, m_, v_):
    m2 = ADAM_B1 * m_ + (1.0 - ADAM_B1) * g_
    v2 = ADAM_B2 * v_ + (1.0 - ADAM_B2) * jnp.square(g_)
    m_hat = m2 / (1.0 - ADAM_B1 ** ADAM_STEP)
    v_hat = v2 / (1.0 - ADAM_B2 ** ADAM_STEP)
    delta = -ADAM_LR * (m_hat / (jnp.sqrt(v_hat) + ADAM_EPS) + ADAM_WD * w_)
    return delta, m2, v2


def _adamw(name, w, g, m, v):
    R, C = w.shape
    tile = _pick(R, (256, 128, 64, 32, 16, 8))
    outs, _ = _rowwise(name, lambda tv, bv: (list(_adamw_update(*tv)), []), [w, g, m, v], [], [(C, F32)] * 3, [], tile)
    return outs


def _adamw_many(name, ws, gs, ms, vs):
    n = len(ws)

    def body(*refs):
        ins, outs = refs[:4 * n], refs[4 * n:]
        for i in range(n):
            res = _adamw_update(ins[i][...], ins[n + i][...], ins[2 * n + i][...], ins[3 * n + i][...])
            for j in range(3):
                outs[j * n + i][...] = res[j]

    out_shape = [jax.ShapeDtypeStruct(w.shape, F32) for w in ws] * 3
    res = pl.pallas_call(body, name=name, out_shape=out_shape, compiler_params=_cparams())(*ws, *gs, *ms, *vs)
    return res[:n], res[n:2 * n], res[2 * n:]


def _pack_rows(arrs, lanes=128, row_mult=8):
    flat, places, off = [], [], 0
    for a_ in arrs:
        n = a_.size
        flat.append(a_.reshape(-1).astype(F32))
        places.append((off, n, a_.shape))
        off += n
    total = -(-off // (lanes * row_mult)) * (lanes * row_mult)
    if total > off:
        flat.append(jnp.zeros((total - off,), F32))
    return jnp.concatenate(flat).reshape(total // lanes, lanes), places


def _unpack_rows(packed, places):
    flat = packed.reshape(-1)
    return [flat[o:o + n].reshape(s) for (o, n, s) in places]


_WEIGHTS = ['w_ada', 'b_ada', 'w_in', 'b_in', 'g_ln_v', 'b_ln_v', 'w_spatial', 'b_spatial', 'mu_shift', 'w0', 'w_decay_up', 'a0',
            'w_aaa_up', 'w_gate_up', 'k_k', 'k_a', 'r_k', 'gn_gain', 'gn_bias', 'w_branch_a', 'w_branch_b', 'w_out', 'b_out',
            'ln1_g', 'ln1_b', 'w_ff1', 'b_ff1', 'w_ff2', 'b_ff2', 'ln2_g', 'ln2_b']
_BIG = {'w_ff1': (0, 512), 'w_ff2': (512, 512), 'w_out': (1024, 128), 'w_branch_a': (1152, 64), 'w_branch_b': (1216, 64), 'w_in': (1280, 640)}
_LATER_ROWS = 1280
_CUT_BY_COLS = ('w_ff1', 'w_in', 'w_branch_a', 'w_branch_b')
IN_SHARD = IN_COLS // _N_DEV
_LORA = {'w_decay_up': (0, LORA_W), 'w_aaa_up': (LORA_W, LORA_A), 'w_gate_up': (LORA_W + LORA_A, LORA_G)}
_COMM_DTYPE = jnp.bfloat16


def _pad_rows(a, rows):
    return jnp.pad(a, ((0, rows - a.shape[0]),) + ((0, 0),) * (a.ndim - 1))


def _pack_big(shards):
    blocks = []
    for n, (_, rows) in _BIG.items():
        a = shards[n].T if n in _CUT_BY_COLS else shards[n]
        blocks.append(_pad_rows(a.reshape(-1, D_MODEL), rows))
    return jnp.concatenate(blocks, axis=0)


def _unpack_big(block, like):
    out = {}
    for n, (r0, _) in _BIG.items():
        rr, cc = like[n].shape
        if n in _CUT_BY_COLS:
            out[n] = block[r0:r0 + rr * cc // D_MODEL].reshape(cc, rr).T
        else:
            out[n] = block[r0:r0 + rr]
    return out


def _to_padded(a, axis):
    g_end = 2 * G_WIDTH
    r_end = g_end + RW_USED
    take = lambda lo, hi: lax.slice_in_dim(a, lo, hi, axis=axis)
    zshape = list(a.shape)
    zshape[axis] = RW_COLS - RW_USED
    return jnp.concatenate([take(r_end, IN_COLS), take(g_end, r_end), jnp.zeros(zshape, a.dtype), take(0, g_end)], axis=axis)


def _from_padded(a, axis):
    take = lambda lo, hi: lax.slice_in_dim(a, lo, hi, axis=axis)
    return jnp.concatenate([take(2 * D_MODEL + RW_COLS, P_COLS), take(2 * D_MODEL, 2 * D_MODEL + RW_USED), take(0, 2 * D_MODEL)], axis=axis)


def _step(p, m, v, x, c, target):
    T = x.shape[0]
    xi, yi, ci = _place()
    me = 4 * xi + 2 * yi + ci
    tile = _pick(T, (512, 256))

    lane = jnp.arange(R_WIDTH)
    gsum = (lane[:, None] // R_HEAD == lane[None, :] // R_HEAD).astype(F32)
    expand = (jnp.arange(128)[:, None] == (lane[None, :] // (G_WIDTH // 8))).astype(F32)

    (c_act,), _ = _rowwise("silu_c", lambda tv, bv: ([tv[0] * _sigmoid(tv[0])], []), [c], [], [(D_MODEL, F32)], [], 1)
    small, places = _pack_rows([c_act, p['w_decay_up'], p['w_aaa_up'], p['w_gate_up']])
    small_all = _all_gather("gather_small", small)
    per_dev = [_unpack_rows(small_all[d], places) for d in range(_N_DEV)]
    c_act_all = _pad_rows(jnp.concatenate([pd[0] for pd in per_dev], axis=0), 16)
    lora_full = {n: jnp.concatenate([pd[i + 1] for pd in per_dev], axis=1) for i, n in enumerate(_LORA)}
    lora_pad = {n: jnp.zeros((LORA_PAD, R_WIDTH), F32).at[r0:r0 + nr].set(lora_full[n]) for n, (r0, nr) in _LORA.items()}

    big_names = list(_BIG)
    my_rows = _pack_big(p).astype(_MXU_DTYPE)
    b_in_p = _to_padded(p['b_in'], 1)
    mu_p = jnp.concatenate([p['mu_shift'], jnp.zeros((1, RW_COLS - RW_USED), F32)], axis=1)

    b_ada_mine = lax.dynamic_slice(p['b_ada'], (0, me * 768), (1, 768))
    mod_cols = _mm("ada_mod", c_act_all, p['w_ada'], "nn", bias=b_ada_mine)
    mod_all = _all_gather("gather_mod", mod_cols)
    mod = lax.dynamic_index_in_dim(mod_all, me, axis=1, keepdims=False).reshape(1, 6 * D_MODEL)
    sh1, sc1, gt1, sh2, sc2, gt2 = [mod[:, i * D_MODEL:(i + 1) * D_MODEL] for i in range(6)]

    (h,), _, w_in_all = _rowwise("modulate1", lambda tv, bv: (_seg_modulate(tv, bv), []), [x], [sc1, sh1], [(D_MODEL, _MXU_DTYPE)], [], tile,
                                 rider=(my_rows[_LATER_ROWS:], _gather_def, (0.0, 0.5, 1.0)))
    w_in_t = _to_padded(w_in_all[:, :IN_SHARD].reshape(IN_COLS, D_MODEL), 0)
    proj = _mm("in_proj", h, w_in_t, "nt", bias=b_in_p)
    ws = p['w_spatial']
    b_tg = jnp.zeros((G_CHUNK, 128), F32).at[:, :8].set(p['b_spatial'].T)
    gmlp_b = [p['g_ln_v'], p['b_ln_v'], ws, b_tg, expand]
    z_gmlp = (proj, 2 * G_WIDTH, 4)
    (ya,), _ = _rowwise("gmlp", lambda tv, bv: (_seg_gmlp(tv, bv), []), [z_gmlp], gmlp_b, [(G_WIDTH, F32)], [], G_CHUNK)
    z_rw = (proj, RW_COLS, 1)
    z_rw_halo = ("halo", proj, RW_COLS, 1)
    pre_b = [mu_p, p['w0'], lora_pad['w_decay_up'], p['a0'], lora_pad['w_aaa_up'], lora_pad['w_gate_up'], p['k_k'], p['k_a'], gsum]

    def pre_fwd(tv, bv):
        z_t, halo_t = tv
        return _seg_rwkv_pre([z_t, _shift_down(z_t, halo_t, pl.program_id(0) == 0)], bv), []

    pre_out, _ = _rowwise("rwkv_pre", pre_fwd, [z_rw, z_rw_halo], pre_b, [(R_WIDTH, F32)] * 7, [], tile)
    r_, lw_, k2_, v_, a_, b_, g_ = pre_out
    scan_in = (r_, lw_, k2_, v_, a_, b_)
    y_, states, inverses, later_all = _scan_fwd(*scan_in, rider=(my_rows[:_LATER_ROWS], _gather_def, (0.0, 0.875, 1.0)))

    def whole(n, rows):
        r0 = _BIG[n][0]
        return later_all[:, r0:r0 + rows].reshape(_N_DEV * rows, D_MODEL)

    w_ff1_t, w_ff2, w_out = whole('w_ff1', 512), whole('w_ff2', 512), whole('w_out', 128)
    w_ba_t = whole('w_branch_a', 64).reshape(D_MODEL, G_WIDTH)
    w_bb_t = whole('w_branch_b', 64).reshape(D_MODEL, R_WIDTH)
    post_b = [p['r_k'].reshape(1, R_WIDTH), p['gn_gain'], p['gn_bias'], gsum]
    (yb,), _ = _rowwise("rwkv_post", lambda tv, bv: (_seg_rwkv_post(tv, bv), []), [y_, r_, k2_, v_, g_], post_b, [(R_WIDTH, F32)], [], tile)
    pa = _mm("branch_a", ya, w_ba_t, "nt", out_dtype=_MXU_DTYPE)
    pb = _mm("branch_b", yb, w_bb_t, "nt", out_dtype=_MXU_DTYPE)
    gates = [(proj, D_MODEL, 0), (proj, D_MODEL, 1)]
    (merged,), _ = _rowwise("merge", lambda tv, bv: (_seg_merge(tv, bv), []), gates + [pa, pb], [], [(D_MODEL, _MXU_DTYPE)], [], tile)
    mix = _mm("out_proj", merged, w_out, "nn", bias=p['b_out'])
    mid_b = [gt1, p['ln1_g'], p['ln1_b'], sc2, sh2]
    (h1, h2in), _ = _rowwise("mid", lambda tv, bv: (_seg_mid(tv, bv), []), [x, mix], mid_b, [(D_MODEL, F32), (D_MODEL, _MXU_DTYPE)], [], tile)
    act = _mm("ff1", h2in, w_ff1_t, "nt", bias=p['b_ff1'], epi=lambda t: _seg_relu2([t], [])[0], epi_dtype=_MXU_DTYPE, raw=False)
    ff = _mm("ff2", act, w_ff2, "nn", bias=p['b_ff2'])

    def loss_fn(tv, bv):
        h1_t, ff_t, tgt = tv
        val, grads = jax.value_and_grad(lambda a0_, a1_, b0_, b1_, b2_: _seg_loss([a0_, a1_, tgt], [b0_, b1_, b2_]), argnums=(0, 1, 2, 3, 4))(h1_t, ff_t, *bv)
        return [grads[0], grads[1]], [grads[2], grads[3], grads[4], jnp.sum(grads[1], axis=0, keepdims=True), jnp.full((1, 128), val, F32)]

    (dh1_a, dff), (d_gt2, d_ln2_g, d_ln2_b, d_b_ff2, loss_row) = _rowwise(
        "loss", loss_fn, [h1, ff, target], [gt2, p['ln2_g'], p['ln2_b']], [(D_MODEL, F32), (D_MODEL, _MXU_DTYPE)], [(1, D_MODEL)] * 4 + [(1, 128)], tile)

    g = {}
    g['ln2_g'], g['ln2_b'], g['b_ff2'] = d_ln2_g, d_ln2_b, d_b_ff2
    gw = {}
    gw['w_ff2'] = _mm("g_w_ff2", act, dff, "tn")
    df1, g['b_ff1'] = _mm("d_act", dff, w_ff2, "nt", beside=act, epi=lambda d_, a_: d_ * (2.0 * jnp.sqrt(a_.astype(F32))),
                          epi_dtype=_MXU_DTYPE, colsum=True)
    gw['w_ff1'] = _mm("g_w_ff1", df1, h2in, "tn")
    dh2in = _mm("d_h2in", df1, w_ff1_t, "nn")
    (dx_a, dmix), (d_gt1, g['ln1_g'], g['ln1_b'], d_sc2, d_sh2), (g['b_out'],) = _rowwise_vjp(
        "mid_bwd", _seg_mid, [x, mix], mid_b, [dh1_a, dh2in], tile, [0, 1], [0, 1, 2, 3, 4], t_dtypes=[F32, _MXU_DTYPE], colsum=[1])
    gw['w_out'] = _mm("g_w_out", merged, dmix, "tn")
    dmerged = _mm("d_merged", dmix, w_out, "nt", out_dtype=_MXU_DTYPE)
    (dproj, dpa, dpb), _, (cs_gates,) = _rowwise_vjp(
        "merge_bwd", _seg_merge, gates + [pa, pb], [], [dmerged], tile, [0, 1, 2, 3], [], t_dtypes=[_MXU_DTYPE] * 3, colsum=[0],
        finish=lambda dts, sc: [jnp.concatenate(dts[:2], axis=1), dts[2], dts[3]], out_widths=[2 * D_MODEL, D_MODEL, D_MODEL],
        into={0: (lax.empty((T, P_COLS), _MXU_DTYPE), 0)})
    gw['w_branch_a'] = _mm("g_w_branch_a", dpa, ya, "tn")
    gw['w_branch_b'] = _mm("g_w_branch_b", dpb, yb, "tn")
    dya = _mm("d_ya", dpa, w_ba_t, "nn")
    dyb = _mm("d_yb", dpb, w_bb_t, "nn")
    def send_rows(names):
        parts = []
        for n in names:
            per_dev = gw[n].reshape(_N_DEV, -1, D_MODEL)
            parts.append(jnp.pad(per_dev, ((0, 0), (0, _BIG[n][1] - per_dev.shape[1]), (0, 0))))
        return jnp.concatenate(parts, axis=1) if len(parts) > 1 else parts[0]

    send_early = send_rows(big_names[:-1])
    (dy, dr1, dk1, dv1, dg_), (d_r_k, g['gn_gain'], g['gn_bias']), _, sibling_early = _rowwise_vjp(
        "rwkv_post_bwd", _seg_rwkv_post, [y_, r_, k2_, v_, g_], post_b, [dyb], tile, [0, 1, 2, 3, 4], [0, 1, 2],
        rider=(send_early, _sibling_def, (0.0, 1.0)))
    g['r_k'] = d_r_k
    partials_early = _chip_partials("chip_partials_early", send_early, sibling_early, 128, _COMM_DTYPE)
    dr2, dlw, dk2, dv2, da, db, landed_early = _scan_bwd(*scan_in, states, inverses, dy, rider=(partials_early, _chips_def, (0.0, 1.0)))
    pre_tile = 128
    last_step = T // pre_tile - 1

    def pre_prep(prim):
        z_t, halo_t = prim
        return [z_t, _shift_down(z_t, halo_t, pl.program_id(0) == last_step)]

    def pre_finish(dts, sc):
        dz_direct, dprev = dts
        (row_after,) = sc

        @pl.when(pl.program_id(0) == 0)
        def _():
            row_after[...] = jnp.zeros_like(row_after)

        dz = dz_direct + _shift_up(dprev, row_after[...])
        row_after[...] = dprev[:_HALO]
        return [dz]

    (dproj,), (d_mu, g['w0'], d_wd, g['a0'], d_wa, d_wg, g['k_k'], g['k_a']), (cs_rw,) = _rowwise_vjp(
        "rwkv_pre_bwd", _seg_rwkv_pre, [z_rw, z_rw_halo], pre_b, [[dr1, dr2], dlw, [dk1, dk2], [dv1, dv2], da, db, dg_], pre_tile,
        [0, 1], [0, 1, 2, 3, 4, 5, 6, 7], t_dtypes=[_MXU_DTYPE], colsum=[0], prep=pre_prep, finish=pre_finish,
        out_widths=[RW_COLS], reverse=True, scratch=[(_HALO, RW_COLS)], into={0: (dproj, 1)})
    g['mu_shift'] = d_mu[:, :RW_USED]
    for n, d_ in (('w_decay_up', d_wd), ('w_aaa_up', d_wa), ('w_gate_up', d_wg)):
        r0, nr = _LORA[n]
        g[n] = d_[r0:r0 + nr]
    (dproj,), (g['g_ln_v'], g['b_ln_v'], g['w_spatial'], d_b_tg), (cs_g,) = _rowwise_vjp(
        "gmlp_bwd", _seg_gmlp, [z_gmlp], gmlp_b, [dya], G_CHUNK, [0], [0, 1, 2, 3], t_dtypes=[_MXU_DTYPE], colsum=[0],
        into={0: (dproj, 4)})
    g['b_spatial'] = d_b_tg[:, :8].T
    g['b_in'] = _from_padded(jnp.concatenate([cs_gates, cs_rw, cs_g], axis=1), 1)
    small_names = [n for n in _WEIGHTS if n not in _BIG and n not in ('w_ada', 'b_ada')]
    packed_g, g_places = _pack_rows([g[n] for n in small_names] + [loss_row], row_mult=256)
    gw_in_t, small_all = _mm("g_w_in", dproj, h, "tn", rider=(packed_g, _gather_def, (0.0, 0.6, 1.0)))
    gw['w_in'] = _from_padded(gw_in_t, 0)
    send_late = send_rows(big_names[-1:])
    sibling_late = _exchange("pair_exchange_late", send_late, _sibling_def)
    partials_late = _chip_partials("chip_partials_late", send_late, sibling_late, 128, _COMM_DTYPE)
    dh, landed_late = _mm("d_h", dproj, w_in_t, "nn", rider=(partials_late, _chips_def, (0.0, 1.0)))

    def mod1_bwd(tv, bv):
        x_t, dh_t, dxa_t = tv
        (sc,) = bv
        return [dxa_t + dh_t * (1.0 + sc)], [jnp.sum(dh_t * x_t, axis=0, keepdims=True), jnp.sum(dh_t, axis=0, keepdims=True)]

    (grad_x,), (d_sc1, d_sh1) = _rowwise("modulate1_bwd", mod1_bwd, [x, dh, dx_a], [sc1], [(D_MODEL, F32)], [(1, D_MODEL)] * 2, tile)

    dmod = jnp.concatenate([d_sh1, d_sc1, d_gt1, d_sh2, d_sc2, d_gt2], axis=1).reshape(6 * D_MODEL // 128, 128)
    dmod_all = _all_gather("gather_dmod", dmod)
    g['b_ada'] = _sum_leading("sum_dmod", dmod_all).reshape(1, 6 * D_MODEL)
    dmod_mine = lax.dynamic_slice(dmod_all.reshape(_N_DEV, 6 * D_MODEL), (0, me * 768), (_N_DEV, 768))
    g_w_ada = _mm("g_w_ada", c_act_all, _pad_rows(dmod_mine, 16), "tn")

    small_sum = _unpack_rows(_sum_leading("sum_small", small_all), g_places)
    loss = small_sum.pop()[0, 0]
    for n, t in zip(small_names, small_sum, strict=True):
        g[n] = t
    for n in _LORA:
        g[n] = lax.dynamic_slice(g[n], (0, me * R_HEAD), (g[n].shape[0], R_HEAD))
    g['w_ada'] = g_w_ada

    summed = jnp.concatenate([_sum_leading("sum_early", landed_early), _sum_leading("sum_late", landed_late)], axis=0)
    g.update(_unpack_big(summed, p))

    delta, new_m, new_v = {}, {}, {}
    own_call = ['w_ada'] + big_names
    for n in own_call:
        if n == 'w_in':
            r0 = _BIG[n][0]
            outs_t = _adamw("adamw_" + n, p[n].T, summed[r0:r0 + IN_SHARD], m[n].T, v[n].T)
            delta[n], new_m[n], new_v[n] = [o.T for o in outs_t]
        else:
            delta[n], new_m[n], new_v[n] = _adamw("adamw_" + n, p[n], g[n], m[n], v[n])
    rest = [n for n in _WEIGHTS if n not in own_call]
    outs = _adamw_many("adamw_rest", *[[d[n].reshape(p[n].shape) for n in rest] for d in (p, g, m, v)])
    for d, o in zip((delta, new_m, new_v), outs, strict=True):
        d.update(zip(rest, o, strict=True))
    return loss, grad_x, g, delta, new_m, new_v


def kernel(x, c, w_ada, b_ada, w_in, b_in, g_ln_v, b_ln_v, w_spatial, b_spatial, mu_shift, w0, w_decay_up, a0, w_aaa_up, w_gate_up, k_k, k_a, r_k, gn_gain, gn_bias, w_branch_a, w_branch_b, w_out, b_out, ln1_g, ln1_b, w_ff1, b_ff1, w_ff2, b_ff2, ln2_g, ln2_b, loss_target, m_w_ada, m_b_ada, m_w_in, m_b_in, m_g_ln_v, m_b_ln_v, m_w_spatial, m_b_spatial, m_mu_shift, m_w0, m_w_decay_up, m_a0, m_w_aaa_up, m_w_gate_up, m_k_k, m_k_a, m_r_k, m_gn_gain, m_gn_bias, m_w_branch_a, m_w_branch_b, m_w_out, m_b_out, m_ln1_g, m_ln1_b, m_w_ff1, m_b_ff1, m_w_ff2, m_b_ff2, m_ln2_g, m_ln2_b, v_w_ada, v_b_ada, v_w_in, v_b_in, v_g_ln_v, v_b_ln_v, v_w_spatial, v_b_spatial, v_mu_shift, v_w0, v_w_decay_up, v_a0, v_w_aaa_up, v_w_gate_up, v_k_k, v_k_a, v_r_k, v_gn_gain, v_gn_bias, v_w_branch_a, v_w_branch_b, v_w_out, v_b_out, v_ln1_g, v_ln1_b, v_w_ff1, v_b_ff1, v_w_ff2, v_b_ff2, v_ln2_g, v_ln2_b):
    given = dict(locals())
    shapes = {n: given[n].shape for n in _WEIGHTS}
    def two_d(a_):
        a_ = a_[0]
        return a_.reshape(1, -1) if a_.ndim == 1 else a_
    p = {n: two_d(given[n]) for n in _WEIGHTS}
    m = {n: two_d(given["m_" + n]) for n in _WEIGHTS}
    v = {n: two_d(given["v_" + n]) for n in _WEIGHTS}
    loss, grad_x, g, delta, new_m, new_v = _step(p, m, v, x[0], c, loss_target[0])
    outs = [loss, grad_x[None]]
    for d in (g, delta, new_m, new_v):
        outs += [d[n].reshape(shapes[n]) for n in _WEIGHTS]
    return tuple(outs)
```

```python
import functools

import jax
import jax.numpy as jnp
from jax import lax
from jax.experimental import pallas as pl
from jax.experimental.pallas import tpu as pltpu

F32 = jnp.float32
_MXU_DTYPE = jnp.bfloat16
_HI = lax.Precision.HIGHEST
_VMEM_LIMIT = 48 * 1024 * 1024
_MESH_ID = pl.DeviceIdType.MESH
_N_DEV = 8

D_MODEL = 1024
G_WIDTH = 512
G_CHUNK = 128
R_WIDTH = 512
R_HEADS = 8
R_HEAD = 64
LORA_W, LORA_A, LORA_G = 32, 32, 96
D_FF = 4096
ALPHA = 2.0 ** 0.25
LN_EPS = 1e-5
GN_EPS = 64e-5
SCAN_CHUNK = 64
SCAN_PER_STEP = 4
ADAM_LR, ADAM_B1, ADAM_B2, ADAM_EPS, ADAM_WD, ADAM_STEP = 0.001, 0.9, 0.999, 1e-08, 0.01, 10

P_COLS = 5120
RW_COLS = 2048
RW_USED = 3 * R_WIDTH + LORA_W + LORA_A + LORA_G
LORA_PAD = 256
IN_COLS = 2 * G_WIDTH + RW_USED + 2 * D_MODEL


def _cparams(sem=None, **kw):
    if sem is not None:
        kw["dimension_semantics"] = sem
    return pltpu.CompilerParams(vmem_limit_bytes=_VMEM_LIMIT, **kw)


def _dot(a, b, dims=(((1,), (0,)), ((), ())), hi=False):
    if hi:
        return lax.dot_general(a.astype(F32), b.astype(F32), dims, precision=_HI, preferred_element_type=F32)
    return lax.dot_general(a.astype(_MXU_DTYPE), b.astype(_MXU_DTYPE), dims, preferred_element_type=F32)


_NN = (((1,), (0,)), ((), ()))
_NT = (((1,), (1,)), ((), ()))
_TN = (((0,), (0,)), ((), ()))


def _pick(n, pref):
    for t in pref:
        if n % t == 0:
            return t
    return n


def _mm(name, a, b, mode, bias=None, out_dtype=F32, epi=None, epi_dtype=None, raw=True, beside=None, colsum=False, rider=None,
        tm=None, tn=None, tk=None):
    if mode == "nn":
        (M, K), (_, N) = a.shape, b.shape
    elif mode == "nt":
        (M, K), (N, _) = a.shape, b.shape
    else:
        (K, M), (_, N) = a.shape, b.shape
    tm = tm or _pick(M, (2048, 1280, 1024, 512, 256, 128, 64, 32, 16, 8))
    tn = tn or _pick(N, (1024, 512, 640, 384, 256, 128))
    tk = tk or _pick(K, (1024, 512, 256, 128))
    nk = K // tk
    dims = {"nn": _NN, "nt": _NT, "tn": _TN}[mode]
    a_spec = pl.BlockSpec((tk, tm), lambda i, j, k: (k, i)) if mode == "tn" else pl.BlockSpec((tm, tk), lambda i, j, k: (i, k))
    b_spec = pl.BlockSpec((tn, tk), lambda i, j, k: (j, k)) if mode == "nt" else pl.BlockSpec((tk, tn), lambda i, j, k: (k, j))
    o_spec = pl.BlockSpec((tm, tn), lambda i, j, k: (i, j))
    has_bias, has_beside = bias is not None, beside is not None
    two = epi is not None and not has_beside and raw
    only_epi = epi is not None and not has_beside and not raw
    grid = (M // tm, N // tn, nk)
    ride_shape, ride_sems, _ = rider[1](rider[0]) if rider else (None, [], None)

    def body(*refs):
        refs = list(refs)
        n_in = 2 + has_bias + has_beside
        if rider:
            sem_refs = [refs.pop() for _ in ride_sems][::-1]
            ride_out = refs.pop(n_in + 1 + 1 + two)
            ride_in = refs.pop(n_in)
            step = (pl.program_id(0) * grid[1] + pl.program_id(1)) * grid[2] + pl.program_id(2)
            _ride(rider, step, grid[0] * grid[1] * grid[2], ride_in, ride_out, sem_refs)
        a_ref, b_ref = refs[0], refs[1]
        bias_ref = refs[2] if has_bias else None
        beside_ref = refs[n_in - 1] if has_beside else None
        outs = refs[n_in:]
        o_ref, acc_ref = outs[0], outs[-1]
        k = pl.program_id(2)

        @pl.when(k == 0)
        def _():
            acc_ref[...] = jnp.zeros_like(acc_ref)

        acc_ref[...] += _dot(a_ref[...], b_ref[...], dims)

        @pl.when(k == nk - 1)
        def _():
            res = acc_ref[...]
            if has_bias:
                res = res + bias_ref[...]
            if has_beside:
                val = epi(res, beside_ref[...])
                o_ref[...] = val.astype(o_ref.dtype)
                if colsum:
                    outs[1][...] = jnp.broadcast_to(jnp.sum(val, axis=0, keepdims=True), outs[1].shape)
            elif only_epi:
                o_ref[...] = epi(res).astype(o_ref.dtype)
            else:
                o_ref[...] = res.astype(o_ref.dtype)
                if two:
                    outs[1][...] = epi(res).astype(outs[1].dtype)

    in_specs = [a_spec, b_spec]
    args = [a, b]
    if has_bias:
        in_specs.append(pl.BlockSpec((1, tn), lambda i, j, k: (0, j)))
        args.append(bias)
    if has_beside:
        in_specs.append(o_spec)
        args.append(beside)
    out_shape = [jax.ShapeDtypeStruct((M, N), epi_dtype if (has_beside or only_epi) else out_dtype)]
    out_specs = [o_spec]
    if two:
        out_shape.append(jax.ShapeDtypeStruct((M, N), epi_dtype))
        out_specs.append(o_spec)
    if colsum:
        assert has_beside and not rider
        out_shape.append(jax.ShapeDtypeStruct((grid[0] * 8, N), F32))
        out_specs.append(pl.BlockSpec((8, tn), lambda i, j, k: (i, j)))
    if rider:
        in_specs.append(_HBM)
        args.append(rider[0])
        out_shape.append(ride_shape)
        out_specs.append(_HBM)
    res = pl.pallas_call(
        body, name=name, grid=grid, in_specs=in_specs, out_specs=out_specs, out_shape=out_shape,
        scratch_shapes=[pltpu.VMEM((tm, tn), F32)] + list(ride_sems),
        compiler_params=_cparams(("arbitrary",) * 3 if rider else ("parallel", "parallel", "arbitrary")),
    )(*args)
    if colsum:
        return res[0], _sum_leading(name + "_colsum", res[1].reshape(grid[0], 8, N))[0:1]
    return res if (two or rider) else res[0]


_HALO = 8


def _rowwise(name, fn, tiled, bcast, tiled_out, red_out, tile, reverse=False, scratch=(), rider=None):
    tiled = [t if isinstance(t, tuple) else (t, t.shape[1], 0) for t in tiled]
    T = next(t[0] for t in tiled if not isinstance(t[0], str)).shape[0]
    n = T // tile
    nt, nb, nto, nsc, nro = len(tiled), len(bcast), len(tiled_out), len(scratch), len(red_out)
    ride_shape, ride_sems, _ = rider[1](rider[0]) if rider else (None, [], None)
    nr = 1 if rider else 0
    into = [(j, t[2], t[3]) for j, t in enumerate(tiled_out) if len(t) == 4]
    na = len(into)

    def row_block(i):
        return n - 1 - i if reverse else i

    def body(*refs):
        i2 = nt + nb
        o0 = i2 + nr + na
        o1, o2 = o0 + nto, o0 + nto + nro
        s0 = o2 + nr
        if rider:
            _ride(rider, pl.program_id(0), n, refs[i2], refs[o2], refs[s0 + nsc:])
        t_refs, b_refs = refs[:nt], refs[nt:i2]
        to_refs, ro_refs = refs[o0:o1], refs[o1:o2]
        extra = (list(refs[s0:s0 + nsc]),) if nsc else ()
        touts, routs = fn([r[...] for r in t_refs], [r[...] for r in b_refs], *extra)
        for r, v in zip(to_refs, touts, strict=True):
            r[...] = v.astype(r.dtype)
        if ro_refs:
            i = pl.program_id(0)

            @pl.when(i == 0)
            def _():
                for r, v in zip(ro_refs, routs, strict=True):
                    r[...] = v.astype(F32)

            @pl.when(i > 0)
            def _():
                for r, v in zip(ro_refs, routs, strict=True):
                    r[...] += v.astype(F32)

    def whole(shape):
        nd = len(shape)
        return pl.BlockSpec(tuple(shape), lambda i: (0,) * nd)

    per_tile = tile // _HALO
    in_specs, arrays = [], []
    for t in tiled:
        if isinstance(t[0], str):
            _, arr, w, cb = t
            in_specs.append(pl.BlockSpec((_HALO, w), functools.partial(lambda i, cb: (jnp.maximum(row_block(i) * per_tile - 1, 0), cb), cb=cb)))
        else:
            arr, w, cb = t
            in_specs.append(pl.BlockSpec((tile, w), functools.partial(lambda i, cb: (row_block(i), cb), cb=cb)))
        arrays.append(arr)
    in_specs += [whole(b.shape) for b in bcast]
    out_specs, out_shape = [], []
    for t in tiled_out:
        cb = t[3] if len(t) == 4 else 0
        out_specs.append(pl.BlockSpec((tile, t[0]), functools.partial(lambda i, cb: (row_block(i), cb), cb=cb)))
        out_shape.append(jax.ShapeDtypeStruct(t[2].shape if len(t) == 4 else (T, t[0]), t[1]))
    out_specs += [whole(s) for s in red_out]
    out_shape += [jax.ShapeDtypeStruct(tuple(s), F32) for s in red_out]
    ride_args = []
    if rider:
        in_specs.append(_HBM)
        out_specs.append(_HBM)
        out_shape.append(ride_shape)
        ride_args = [rider[0]]
    in_specs += [pl.BlockSpec(memory_space=pl.ANY)] * na
    aliases = {nt + nb + nr + k: j for k, (j, _, _) in enumerate(into)}
    res = pl.pallas_call(
        body, name=name, grid=(n,), in_specs=in_specs, out_specs=out_specs, out_shape=out_shape,
        scratch_shapes=[pltpu.VMEM(tuple(s), F32) for s in scratch] + list(ride_sems),
        input_output_aliases=aliases, compiler_params=_cparams(("arbitrary",)),
    )(*arrays, *bcast, *ride_args, *[buf for (_, buf, _) in into])
    if rider:
        return list(res[:nto]), list(res[nto:nto + nro]), res[-1]
    return list(res[:nto]), list(res[nto:])


def _rowwise_vjp(name, f, tiled, bcast, cts, tile, wrt_t, wrt_b, t_dtypes=None, colsum=(), prep=None, finish=None,
                 out_widths=None, reverse=False, scratch=(), rider=None, into=None):
    tiled = [t if isinstance(t, tuple) else (t, t.shape[1], 0) for t in tiled]
    npr = len(tiled)
    t_dtypes = t_dtypes or [F32] * len(wrt_t)
    groups = [c if isinstance(c, list) else [c] for c in cts]
    cts = [a_ for grp in groups for a_ in grp]

    def fn(tv, bv, sc=None):
        prim, flat_ct = tv[:npr], list(tv[npr:])
        if prep is not None:
            prim = prep(prim)
        ct = []
        for grp in groups:
            parts = [flat_ct.pop(0).astype(F32) for _ in grp]
            ct.append(functools.reduce(lambda p_, q_: p_ + q_, parts))

        def g(dt_vals, db_vals):
            full_t, full_b = list(prim), list(bv)
            for i, v in zip(wrt_t, dt_vals, strict=True):
                full_t[i] = v
            for j, v in zip(wrt_b, db_vals, strict=True):
                full_b[j] = v
            return f(full_t, full_b)

        outs, pull = jax.vjp(g, [prim[i].astype(F32) for i in wrt_t], [bv[j] for j in wrt_b])
        dts, dbs = pull([c.astype(o.dtype) for c, o in zip(ct, outs, strict=True)])
        if finish is not None:
            dts = finish(dts, sc)
        sums = [jnp.sum(dts[i].astype(F32), axis=0, keepdims=True) for i in colsum]
        return dts, list(dbs) + sums

    widths = out_widths or [tiled[i][1] for i in wrt_t]
    tiled_out = [(w, dt) + tuple((into or {}).get(j, ())) for j, (w, dt) in enumerate(zip(widths, t_dtypes, strict=True))]
    red_out = [bcast[j].shape for j in wrt_b] + [(1, widths[i]) for i in colsum]
    res = _rowwise(name, fn, tiled + list(cts), bcast, tiled_out, red_out, tile, reverse=reverse, scratch=scratch, rider=rider)
    dts, reds = res[0], res[1]
    nb = len(wrt_b)
    return (dts, reds[:nb], reds[nb:]) + tuple(res[2:])


def _layer_norm(x, g, b, eps):
    mu = jnp.mean(x, axis=-1, keepdims=True)
    xc = x - mu
    var = jnp.mean(xc * xc, axis=-1, keepdims=True)
    return xc * lax.rsqrt(var + eps) * g + b


def _gelu_tanh(x):
    return 0.5 * x * (1.0 + jnp.tanh(0.7978845608028654 * (x + 0.044715 * (x * x * x))))


def _sigmoid(x):
    return 1.0 / (1.0 + jnp.exp(-x))


def _seg_modulate(tv, bv):
    (x,), (sc, sh) = tv, bv
    return [x * (1.0 + sc) + sh]


def _seg_gmlp(tv, bv):
    (z,), (g_ln, b_ln, ws, b_tg, expand) = tv, bv
    bias_full = _dot(b_tg, expand, hi=True)
    zz = _gelu_tanh(z)
    u, v = zz[:, :G_WIDTH], zz[:, G_WIDTH:]
    v = _layer_norm(v, g_ln, b_ln, LN_EPS)
    row = lax.broadcasted_iota(jnp.int32, (G_CHUNK, G_CHUNK), 0)
    col = lax.broadcasted_iota(jnp.int32, (G_CHUNK, G_CHUNK), 1)
    causal = col <= row
    first_group = lax.broadcasted_iota(jnp.int32, (G_CHUNK, 128), 1) < 64
    parts = []
    for p in range(4):
        vp = v[:, 128 * p:128 * (p + 1)]
        s_even = _dot(jnp.where(causal, ws[2 * p], 0.0), vp)
        s_odd = _dot(jnp.where(causal, ws[2 * p + 1], 0.0), vp)
        parts.append(jnp.where(first_group, s_even, s_odd))
    s = jnp.concatenate(parts, axis=1) + bias_full
    return [u * s]


def _split2(x):
    hi = x.astype(_MXU_DTYPE)
    return hi, (x - hi.astype(F32)).astype(_MXU_DTYPE)


@jax.custom_vjp
def _group_sum(x, ones_blocks):
    hi, lo = _split2(x)
    w = ones_blocks.shape[0]
    return jnp.concatenate([_dot(hi[:, j:j + w], ones_blocks) + _dot(lo[:, j:j + w], ones_blocks) for j in range(0, x.shape[1], w)], axis=1)


def _group_sum_fwd(x, ones_blocks):
    return _group_sum(x, ones_blocks), ones_blocks


def _group_sum_bwd(ones_blocks, ct):
    return _group_sum(ct, ones_blocks), jnp.zeros_like(ones_blocks)


_group_sum.defvjp(_group_sum_fwd, _group_sum_bwd)


def _shift_down(z, halo, is_first):
    _, W = z.shape
    rolled = pltpu.roll(z, 1, 0)
    before = jnp.where(is_first, 0.0, pltpu.roll(halo, 1, 0))
    top_row = lax.broadcasted_iota(jnp.int32, (_HALO, W), 0) == 0
    return jnp.concatenate([jnp.where(top_row, before, rolled[:_HALO]), rolled[_HALO:]], axis=0)


def _shift_up(d, after):
    tile, W = d.shape
    rolled = pltpu.roll(d, tile - 1, 0)
    last_row = lax.broadcasted_iota(jnp.int32, (_HALO, W), 0) == _HALO - 1
    bottom = jnp.where(last_row, pltpu.roll(after, _HALO - 1, 0), rolled[tile - _HALO:])
    return jnp.concatenate([rolled[:tile - _HALO], bottom], axis=0)


def _seg_rwkv_pre(tv, bv):
    (z, prev), (mu, w0, wd, a0, wa, wg, k_k, k_a, gsum) = tv, bv
    zs = z + (prev - z) * mu
    r, k, v = zs[:, 0:512], zs[:, 512:1024], zs[:, 1024:1536]
    zl = zs[:, 3 * R_WIDTH:3 * R_WIDTH + LORA_PAD]
    x = w0 + _dot(jnp.tanh(zl), wd)
    softplus = jnp.maximum(-x, 0.0) + jnp.log(1.0 + jnp.exp(-jnp.abs(x)))
    lw = -jnp.exp(-softplus - 0.5)
    a = _sigmoid(a0 + _dot(zl, wa))
    g = _dot(_sigmoid(zl), wg)
    kk = k * k_k
    nrm = jnp.sqrt(_group_sum(kk * kk, gsum))
    kk = kk / jnp.maximum(nrm, 1e-12)
    k2 = k * (1.0 + (a - 1.0) * k_a)
    return [r, lw, k2, v, -kk, kk * a, g]


def _seg_rwkv_post(tv, bv):
    (y, r, k2, v, g), (r_k, gain, bias, gsum) = tv, bv
    mu = _group_sum(y, gsum) * (1.0 / R_HEAD)
    yc = y - mu
    var = _group_sum(yc * yc, gsum) * (1.0 / R_HEAD)
    yn = yc * lax.rsqrt(var + GN_EPS) * gain + bias
    bonus = _group_sum(r * k2 * r_k, gsum) * v
    return [(yn + bonus) * g]


def _seg_merge(tv, bv):
    (ga, gb, pa, pb), () = tv, bv
    return [_sigmoid(ga) * pa + _sigmoid(gb) * pb]


def _seg_mid(tv, bv):
    (x, mix), (gt1, g1, b1, sc2, sh2) = tv, bv
    h1 = _layer_norm(ALPHA * x + gt1 * mix, g1, b1, LN_EPS)
    return [h1, h1 * (1.0 + sc2) + sh2]


def _seg_relu2(tv, bv):
    (f1,), () = tv, bv
    return [jnp.square(jnp.maximum(f1, 0.0))]


def _seg_loss(tv, bv):
    (h1, ff, target), (gt2, g2, b2) = tv, bv
    out = _layer_norm(ALPHA * h1 + gt2 * ff, g2, b2, LN_EPS)
    err = jnp.square(out - target)
    return 0.5 * jnp.sum(jnp.mean(err, axis=-1))


_BNN = (((2,), (1,)), ((0,), (0,)))
_BNT = (((2,), (2,)), ((0,), (0,)))
_BTN = (((1,), (1,)), ((0,), (0,)))


def _tri_dot(x, dims):
    L = x.shape[0]
    tri = (lax.broadcasted_iota(jnp.int32, (L, L), 1) <= lax.broadcasted_iota(jnp.int32, (L, L), 0)).astype(F32)
    hi, lo = _split2(x)
    return _dot(tri, hi, dims) + _dot(tri, lo, dims)


@jax.custom_vjp
def _running_sum(x):
    return _tri_dot(x, _NN)


_running_sum.defvjp(lambda x: (_tri_dot(x, _NN), None), lambda _, ct: (_tri_dot(ct, _TN),))


def _cut_heads(x):
    return jnp.stack([x[:, R_HEAD * h:R_HEAD * (h + 1)] for h in range(R_HEADS)])


def _join_heads(x):
    return jnp.concatenate([x[h] for h in range(R_HEADS)], axis=1)


@jax.custom_vjp
def _split_heads(x):
    return _cut_heads(x)


@jax.custom_vjp
def _merge_heads(x):
    return _join_heads(x)


_split_heads.defvjp(lambda x: (_cut_heads(x), None), lambda _, ct: (_join_heads(ct),))
_merge_heads.defvjp(lambda x: (_join_heads(x), None), lambda _, ct: (_cut_heads(ct),))


def _inverse_pullback(inv, ct):
    return _dot(_dot(inv, ct, _BTN), inv, _BNT)


@jax.custom_vjp
def _unit_lower_inverse(n_mat):
    H, L, _ = n_mat.shape
    eye = lax.broadcasted_iota(jnp.int32, (H, L, L), 1) == lax.broadcasted_iota(jnp.int32, (H, L, L), 2)
    inv = jnp.where(eye, 1.0, 0.0) + n_mat
    pw = n_mat
    n = 2
    while n < L:
        pw = _dot(pw, pw, _BNN)
        inv = inv + _dot(inv, pw, _BNN)
        n *= 2
    return inv


def _unit_lower_inverse_fwd(n_mat):
    inv = _unit_lower_inverse(n_mat)
    return inv, inv


_unit_lower_inverse.defvjp(_unit_lower_inverse_fwd, lambda inv, ct: (_inverse_pullback(inv, ct),))


@jax.custom_vjp
def _known_inverse(n_mat, inv):
    return inv


_known_inverse.defvjp(lambda n_mat, inv: (inv, inv), lambda inv, ct: (_inverse_pullback(inv, ct), jnp.zeros_like(inv)))


def _scan_chunk(r, lw, k, v, a, b, s0, inv=None, with_inverse=False):
    L, H = r.shape[0], R_HEADS
    cs = _running_sum(lw)
    cs_end = cs[L - 1:L, :]
    p, p_inv, to_end = jnp.exp(cs), jnp.exp(-cs), jnp.exp(cs_end - cs)
    at, bt, kt, rt = [_split_heads(t) for t in (a * jnp.exp(cs - lw), b * p_inv, k * p_inv, r * p)]
    b_end, k_end, v = [_split_heads(t) for t in (b * to_end, k * to_end, v)]
    row = lax.broadcasted_iota(jnp.int32, (H, L, L), 1)
    col = lax.broadcasted_iota(jnp.int32, (H, L, L), 2)
    incl, strict = col <= row, col < row
    a_ab = jnp.where(strict, _dot(at, bt, _BNT), 0.0)
    a_ak = jnp.where(strict, _dot(at, kt, _BNT), 0.0)
    a_rb = jnp.where(incl, _dot(rt, bt, _BNT), 0.0)
    a_rk = jnp.where(incl, _dot(rt, kt, _BNT), 0.0)
    inv = _unit_lower_inverse(a_ab) if inv is None else _known_inverse(a_ab, inv)
    u = _dot(inv, _dot(at, s0, _BNT) + _dot(a_ak, v, _BNN), _BNN)
    y = _merge_heads(_dot(rt, s0, _BNT) + _dot(a_rb, u, _BNN) + _dot(a_rk, v, _BNN))
    s1 = s0 * _split_heads(jnp.exp(cs_end)) + _dot(u, b_end, _BTN) + _dot(v, k_end, _BTN)
    return (y, s1, inv) if with_inverse else (y, s1)


def _scan_fwd(r, lw, k, v, a, b, rider):
    T = r.shape[0]
    H, N, L, P = R_HEADS, R_HEAD, SCAN_CHUNK, SCAN_PER_STEP
    nc = T // L
    steps = nc // P
    ride_shape, ride_sems, _ = rider[1](rider[0])

    def body(r_ref, lw_ref, k_ref, v_ref, a_ref, b_ref, ride_in, y_ref, st_ref, inv_ref, ride_out, s_ref, *sem_refs):
        _ride(rider, pl.program_id(0), steps, ride_in, ride_out, sem_refs)

        @pl.when(pl.program_id(0) == 0)
        def _():
            s_ref[...] = jnp.zeros_like(s_ref)

        s0 = s_ref[...]
        for j in range(P):
            rows = pl.ds(j * L, L)
            st_ref[j] = s0
            y, s0, inv = _scan_chunk(*[t[rows, :] for t in (r_ref, lw_ref, k_ref, v_ref, a_ref, b_ref)], s0, with_inverse=True)
            y_ref[rows, :] = y
            inv_ref[j] = inv
        s_ref[...] = s0

    blk = pl.BlockSpec((P * L, R_WIDTH), lambda c: (c, 0))
    per_chunk = pl.BlockSpec((P, H, N, N), lambda c: (c, 0, 0, 0))
    return pl.pallas_call(
        body, name="scan_fwd", grid=(steps,), in_specs=[blk] * 6 + [_HBM], out_specs=[blk, per_chunk, per_chunk, _HBM],
        out_shape=[jax.ShapeDtypeStruct((T, R_WIDTH), F32)] + [jax.ShapeDtypeStruct((nc, H, N, N), F32)] * 2 + [ride_shape],
        scratch_shapes=[pltpu.VMEM((H, N, N), F32)] + list(ride_sems),
        compiler_params=_cparams(("arbitrary",)),
    )(r, lw, k, v, a, b, rider[0])


def _scan_bwd(r, lw, k, v, a, b, states, inverses, dy, rider):
    T = r.shape[0]
    H, N, L, P = R_HEADS, R_HEAD, SCAN_CHUNK, SCAN_PER_STEP
    nc = T // L
    steps = nc // P
    ride_shape, ride_sems, _ = rider[1](rider[0])

    def body(r_ref, lw_ref, k_ref, v_ref, a_ref, b_ref, st_ref, inv_ref, dy_ref, ride_in,
             dr_ref, dlw_ref, dk_ref, dv_ref, da_ref, db_ref, ride_out, ds_ref, *sem_refs):
        _ride(rider, pl.program_id(0), steps, ride_in, ride_out, sem_refs)

        @pl.when(pl.program_id(0) == 0)
        def _():
            ds_ref[...] = jnp.zeros_like(ds_ref)

        ds = ds_ref[...]
        for j in reversed(range(P)):
            rows = pl.ds(j * L, L)
            args = [t[rows, :] for t in (r_ref, lw_ref, k_ref, v_ref, a_ref, b_ref)] + [st_ref[j]]
            inv = inv_ref[j]
            _, pull = jax.vjp(lambda *xs, inv=inv: _scan_chunk(*xs, inv=inv), *args)
            grads = pull((dy_ref[rows, :], ds))
            for o_ref, g_ in zip((dr_ref, dlw_ref, dk_ref, dv_ref, da_ref, db_ref), grads[:6], strict=True):
                o_ref[rows, :] = g_
            ds = grads[6]
        ds_ref[...] = ds

    blk = pl.BlockSpec((P * L, R_WIDTH), lambda c: (steps - 1 - c, 0))
    per_chunk = pl.BlockSpec((P, H, N, N), lambda c: (steps - 1 - c, 0, 0, 0))
    return pl.pallas_call(
        body, name="scan_bwd", grid=(steps,), in_specs=[blk] * 6 + [per_chunk, per_chunk, blk, _HBM], out_specs=[blk] * 6 + [_HBM],
        out_shape=[jax.ShapeDtypeStruct((T, R_WIDTH), F32)] * 6 + [ride_shape],
        scratch_shapes=[pltpu.VMEM((H, N, N), F32)] + list(ride_sems),
        compiler_params=_cparams(("arbitrary",)),
    )(r, lw, k, v, a, b, states, inverses, dy, rider[0])


def _place():
    x, y, c = lax.axis_index("x"), lax.axis_index("y"), lax.axis_index("c")
    return x, y, c


def _gather_def(block):
    R, C = block.shape

    def phases(x_ref, out_ref, send_sems, recv_sems, local_sem):
        x, y, c = _place()
        me, sibling = (x, y, c), (x, y, 1 - c)
        chips = [(1 - x, y), (x, 1 - y), (1 - x, 1 - y)]

        def slot(px, py, pc):
            return out_ref.at[4 * px + 2 * py + pc]

        def copy(k, blk, to, src=None):
            return pltpu.make_async_remote_copy(
                src_ref=slot(*blk) if src is None else src, dst_ref=slot(*blk),
                send_sem=send_sems.at[k], recv_sem=recv_sems.at[k], device_id=to, device_id_type=_MESH_ID)

        mine = pltpu.make_async_copy(x_ref, slot(*me), local_sem)
        first = [copy(0, me, sibling, src=x_ref)]
        first += [copy(1 + j, me, (*chip, c), src=x_ref) for j, chip in enumerate(chips)]
        passed = [copy(4 + j, (*chip, c), sibling) for j, chip in enumerate(chips)]

        def begin():
            mine.start()
            for cp in first:
                cp.start()

        def forward():
            for j, chip in enumerate(chips):
                copy(1 + j, (*chip, c), me).wait_recv()
                passed[j].start()

        def finish():
            copy(0, sibling, me).wait_recv()
            for j, chip in enumerate(chips):
                copy(4 + j, (*chip, 1 - c), me).wait_recv()
            for cp in first + passed:
                cp.wait_send()
            mine.wait()

        return [begin, forward, finish]

    sems = [pltpu.SemaphoreType.DMA((7,)), pltpu.SemaphoreType.DMA((7,)), pltpu.SemaphoreType.DMA]
    return jax.ShapeDtypeStruct((_N_DEV, R, C), block.dtype), sems, phases


def _sibling_def(blocks):
    _, R, C = blocks.shape

    def phases(x_ref, out_ref, send_sems, recv_sems):
        x, y, c = _place()
        copies = [pltpu.make_async_remote_copy(
            src_ref=x_ref.at[2 * q + (1 - c)], dst_ref=out_ref.at[q], send_sem=send_sems.at[q], recv_sem=recv_sems.at[q],
            device_id=(x, y, 1 - c), device_id_type=_MESH_ID) for q in range(4)]

        def begin():
            for cp in copies:
                cp.start()

        def finish():
            for cp in copies:
                cp.wait()

        return [begin, finish]

    return jax.ShapeDtypeStruct((4, R, C), blocks.dtype), [pltpu.SemaphoreType.DMA((4,)), pltpu.SemaphoreType.DMA((4,))], phases


def _chips_def(partials):
    def phases(x_ref, out_ref, send_sems, recv_sems, local_sem):
        x, y, c = _place()
        my_chip = 2 * x + y
        mine = pltpu.make_async_copy(x_ref.at[my_chip], out_ref.at[my_chip], local_sem)
        copies = []
        for rel in range(1, 4):
            px, py = (1 - x if (rel >> 1) & 1 else x), (1 - y if rel & 1 else y)
            copies.append(pltpu.make_async_remote_copy(
                src_ref=x_ref.at[2 * px + py], dst_ref=out_ref.at[my_chip],
                send_sem=send_sems.at[rel - 1], recv_sem=recv_sems.at[rel - 1],
                device_id=(px, py, c), device_id_type=_MESH_ID))

        def begin():
            mine.start()
            for cp in copies:
                cp.start()

        def finish():
            for cp in copies:
                cp.wait()
            mine.wait()

        return [begin, finish]

    sems = [pltpu.SemaphoreType.DMA((3,)), pltpu.SemaphoreType.DMA((3,)), pltpu.SemaphoreType.DMA]
    return jax.ShapeDtypeStruct(partials.shape, partials.dtype), sems, phases


_HBM = pl.BlockSpec(memory_space=pltpu.HBM)


def _exchange(name, array, definition):
    out_shape, sems, phases = definition(array)

    def body(x_ref, out_ref, *sem_refs):
        for phase in phases(x_ref, out_ref, *sem_refs):
            phase()

    return pl.pallas_call(body, name=name, in_specs=[_HBM], out_specs=_HBM, out_shape=out_shape, scratch_shapes=sems)(array)


def _ride(rider, step, nsteps, x_ref, out_ref, sem_refs):
    array, definition, fractions = rider
    for phase, frac in zip(definition(array)[2](x_ref, out_ref, *sem_refs), fractions, strict=True):
        pl.when(step == min(int(frac * nsteps), nsteps - 1))(phase)


def _all_gather(name, block):
    return _exchange(name, block, _gather_def)


def _chip_partials(name, blocks, from_sibling, tile, out_dtype):
    _, R, C = blocks.shape

    def body(x_ref, s_ref, o_ref):
        c = lax.axis_index("c")
        for q in range(4):
            o_ref[q] = (x_ref[2 * q + c] + s_ref[q]).astype(o_ref.dtype)

    return pl.pallas_call(
        body, name=name, grid=(R // tile,),
        in_specs=[pl.BlockSpec((_N_DEV, tile, C), lambda i: (0, i, 0)), pl.BlockSpec((4, tile, C), lambda i: (0, i, 0))],
        out_specs=pl.BlockSpec((4, tile, C), lambda i: (0, i, 0)), out_shape=jax.ShapeDtypeStruct((4, R, C), out_dtype),
        compiler_params=_cparams(("parallel",)),
    )(blocks, from_sibling)


def _sum_leading(name, x, tile=None):
    n, R, C = x.shape
    tile = tile or _pick(R, (512, 256, 128, 64, 32, 16, 8))

    def body(x_ref, o_ref):
        acc = x_ref[0].astype(F32)
        for k in range(1, n):
            acc = acc + x_ref[k].astype(F32)
        o_ref[...] = acc

    return pl.pallas_call(
        body, name=name, grid=(R // tile,), in_specs=[pl.BlockSpec((n, tile, C), lambda i: (0, i, 0))],
        out_specs=pl.BlockSpec((tile, C), lambda i: (i, 0)), out_shape=jax.ShapeDtypeStruct((R, C), F32),
        compiler_params=_cparams(("parallel",)),
    )(x)


def _adamw_update(w_, g_, m_, v_):
    m2 = ADAM_B1 * m_ + (1.0 - ADAM_B1) * g_
    v2 = ADAM_B2 * v_ + (1.0 - ADAM_B2) * jnp.square(g_)
    m_hat = m2 / (1.0 - ADAM_B1 ** ADAM_STEP)
    v_hat = v2 / (1.0 - ADAM_B2 ** ADAM_STEP)
    delta = -ADAM_LR * (m_hat / (jnp.sqrt(v_hat) + ADAM_EPS) + ADAM_WD * w_)
    return delta, m2, v2


def _adamw(name, w, g, m, v):
    R, C = w.shape
    tile = _pick(R, (256, 128, 64, 32, 16, 8))
    outs, _ = _rowwise(name, lambda tv, bv: (list(_adamw_update(*tv)), []), [w, g, m, v], [], [(C, F32)] * 3, [], tile)
    return outs


def _adamw_many(name, ws, gs, ms, vs):
    n = len(ws)

    def body(*refs):
        ins, outs = refs[:4 * n], refs[4 * n:]
        for i in range(n):
            res = _adamw_update(ins[i][...], ins[n + i][...], ins[2 * n + i][...], ins[3 * n + i][...])
            for j in range(3):
                outs[j * n + i][...] = res[j]

    out_shape = [jax.ShapeDtypeStruct(w.shape, F32) for w in ws] * 3
    res = pl.pallas_call(body, name=name, out_shape=out_shape, compiler_params=_cparams())(*ws, *gs, *ms, *vs)
    return res[:n], res[n:2 * n], res[2 * n:]


def _pack_rows(arrs, lanes=128, row_mult=8):
    flat, places, off = [], [], 0
    for a_ in arrs:
        n = a_.size
        flat.append(a_.reshape(-1).astype(F32))
        places.append((off, n, a_.shape))
        off += n
    total = -(-off // (lanes * row_mult)) * (lanes * row_mult)
    if total > off:
        flat.append(jnp.zeros((total - off,), F32))
    return jnp.concatenate(flat).reshape(total // lanes, lanes), places


def _unpack_rows(packed, places):
    flat = packed.reshape(-1)
    return [flat[o:o + n].reshape(s) for (o, n, s) in places]


_WEIGHTS = ['w_ada', 'b_ada', 'w_in', 'b_in', 'g_ln_v', 'b_ln_v', 'w_spatial', 'b_spatial', 'mu_shift', 'w0', 'w_decay_up', 'a0',
            'w_aaa_up', 'w_gate_up', 'k_k', 'k_a', 'r_k', 'gn_gain', 'gn_bias', 'w_branch_a', 'w_branch_b', 'w_out', 'b_out',
            'ln1_g', 'ln1_b', 'w_ff1', 'b_ff1', 'w_ff2', 'b_ff2', 'ln2_g', 'ln2_b']
_BIG = {'w_ff1': (0, 512), 'w_ff2': (512, 512), 'w_out': (1024, 128), 'w_branch_a': (1152, 64), 'w_branch_b': (1216, 64), 'w_in': (1280, 640)}
_LATER_ROWS = 1280
_CUT_BY_COLS = ('w_ff1', 'w_in', 'w_branch_a', 'w_branch_b')
IN_SHARD = IN_COLS // _N_DEV
_LORA = {'w_decay_up': (0, LORA_W), 'w_aaa_up': (LORA_W, LORA_A), 'w_gate_up': (LORA_W + LORA_A, LORA_G)}
_COMM_DTYPE = jnp.bfloat16


def _pad_rows(a, rows):
    return jnp.pad(a, ((0, rows - a.shape[0]),) + ((0, 0),) * (a.ndim - 1))


def _pack_big(shards):
    blocks = []
    for n, (_, rows) in _BIG.items():
        a = shards[n].T if n in _CUT_BY_COLS else shards[n]
        blocks.append(_pad_rows(a.reshape(-1, D_MODEL), rows))
    return jnp.concatenate(blocks, axis=0)


def _unpack_big(block, like):
    out = {}
    for n, (r0, _) in _BIG.items():
        rr, cc = like[n].shape
        if n in _CUT_BY_COLS:
            out[n] = block[r0:r0 + rr * cc // D_MODEL].reshape(cc, rr).T
        else:
            out[n] = block[r0:r0 + rr]
    return out


def _to_padded(a, axis):
    g_end = 2 * G_WIDTH
    r_end = g_end + RW_USED
    take = lambda lo, hi: lax.slice_in_dim(a, lo, hi, axis=axis)
    zshape = list(a.shape)
    zshape[axis] = RW_COLS - RW_USED
    return jnp.concatenate([take(r_end, IN_COLS), take(g_end, r_end), jnp.zeros(zshape, a.dtype), take(0, g_end)], axis=axis)


def _from_padded(a, axis):
    take = lambda lo, hi: lax.slice_in_dim(a, lo, hi, axis=axis)
    return jnp.concatenate([take(2 * D_MODEL + RW_COLS, P_COLS), take(2 * D_MODEL, 2 * D_MODEL + RW_USED), take(0, 2 * D_MODEL)], axis=axis)


def _step(p, m, v, x, c, target):
    T = x.shape[0]
    xi, yi, ci = _place()
    me = 4 * xi + 2 * yi + ci
    tile = _pick(T, (512, 256))

    lane = jnp.arange(R_WIDTH)
    gsum = (lane[:128, None] // R_HEAD == lane[None, :128] // R_HEAD).astype(F32)
    expand = (jnp.arange(128)[:, None] == (lane[None, :] // (G_WIDTH // 8))).astype(F32)

    (c_act,), _ = _rowwise("silu_c", lambda tv, bv: ([tv[0] * _sigmoid(tv[0])], []), [c], [], [(D_MODEL, F32)], [], 1)
    small, places = _pack_rows([c_act, p['w_decay_up'], p['w_aaa_up'], p['w_gate_up']])
    small_all = _all_gather("gather_small", small)
    per_dev = [_unpack_rows(small_all[d], places) for d in range(_N_DEV)]
    c_act_all = _pad_rows(jnp.concatenate([pd[0] for pd in per_dev], axis=0), 16)
    lora_full = {n: jnp.concatenate([pd[i + 1] for pd in per_dev], axis=1) for i, n in enumerate(_LORA)}
    lora_pad = {n: jnp.zeros((LORA_PAD, R_WIDTH), F32).at[r0:r0 + nr].set(lora_full[n]) for n, (r0, nr) in _LORA.items()}

    big_names = list(_BIG)
    my_rows = _pack_big(p).astype(_MXU_DTYPE)
    b_in_p = _to_padded(p['b_in'], 1)
    mu_p = jnp.concatenate([p['mu_shift'], jnp.zeros((1, RW_COLS - RW_USED), F32)], axis=1)

    b_ada_mine = lax.dynamic_slice(p['b_ada'], (0, me * 768), (1, 768))
    mod_cols = _mm("ada_mod", c_act_all, p['w_ada'], "nn", bias=b_ada_mine)
    mod_all = _all_gather("gather_mod", mod_cols)
    mod = lax.dynamic_index_in_dim(mod_all, me, axis=1, keepdims=False).reshape(1, 6 * D_MODEL)
    sh1, sc1, gt1, sh2, sc2, gt2 = [mod[:, i * D_MODEL:(i + 1) * D_MODEL] for i in range(6)]

    (h,), _, w_in_all = _rowwise("modulate1", lambda tv, bv: (_seg_modulate(tv, bv), []), [x], [sc1, sh1], [(D_MODEL, _MXU_DTYPE)], [], tile,
                                 rider=(my_rows[_LATER_ROWS:], _gather_def, (0.0, 0.5, 1.0)))
    w_in_t = _to_padded(w_in_all[:, :IN_SHARD].reshape(IN_COLS, D_MODEL), 0)
    proj = _mm("in_proj", h, w_in_t, "nt", bias=b_in_p)
    ws = p['w_spatial']
    b_tg = jnp.zeros((G_CHUNK, 128), F32).at[:, :8].set(p['b_spatial'].T)
    gmlp_b = [p['g_ln_v'], p['b_ln_v'], ws, b_tg, expand]
    z_gmlp = (proj, 2 * G_WIDTH, 4)
    (ya,), _ = _rowwise("gmlp", lambda tv, bv: (_seg_gmlp(tv, bv), []), [z_gmlp], gmlp_b, [(G_WIDTH, F32)], [], G_CHUNK)
    z_rw = (proj, RW_COLS, 1)
    z_rw_halo = ("halo", proj, RW_COLS, 1)
    pre_b = [mu_p, p['w0'], lora_pad['w_decay_up'], p['a0'], lora_pad['w_aaa_up'], lora_pad['w_gate_up'], p['k_k'], p['k_a'], gsum]

    def pre_fwd(tv, bv):
        z_t, halo_t = tv
        return _seg_rwkv_pre([z_t, _shift_down(z_t, halo_t, pl.program_id(0) == 0)], bv), []

    pre_out, _ = _rowwise("rwkv_pre", pre_fwd, [z_rw, z_rw_halo], pre_b, [(R_WIDTH, F32)] * 7, [], tile)
    r_, lw_, k2_, v_, a_, b_, g_ = pre_out
    scan_in = (r_, lw_, k2_, v_, a_, b_)
    y_, states, inverses, later_all = _scan_fwd(*scan_in, rider=(my_rows[:_LATER_ROWS], _gather_def, (0.0, 0.875, 1.0)))

    def whole(n, rows):
        r0 = _BIG[n][0]
        return later_all[:, r0:r0 + rows].reshape(_N_DEV * rows, D_MODEL)

    w_ff1_t, w_ff2, w_out = whole('w_ff1', 512), whole('w_ff2', 512), whole('w_out', 128)
    w_ba_t = whole('w_branch_a', 64).reshape(D_MODEL, G_WIDTH)
    w_bb_t = whole('w_branch_b', 64).reshape(D_MODEL, R_WIDTH)
    post_b = [p['r_k'].reshape(1, R_WIDTH), p['gn_gain'], p['gn_bias'], gsum]
    (yb,), _ = _rowwise("rwkv_post", lambda tv, bv: (_seg_rwkv_post(tv, bv), []), [y_, r_, k2_, v_, g_], post_b, [(R_WIDTH, F32)], [], tile)
    pa = _mm("branch_a", ya, w_ba_t, "nt", out_dtype=_MXU_DTYPE)
    pb = _mm("branch_b", yb, w_bb_t, "nt", out_dtype=_MXU_DTYPE)
    gates = [(proj, D_MODEL, 0), (proj, D_MODEL, 1)]
    (merged,), _ = _rowwise("merge", lambda tv, bv: (_seg_merge(tv, bv), []), gates + [pa, pb], [], [(D_MODEL, _MXU_DTYPE)], [], tile)
    mix = _mm("out_proj", merged, w_out, "nn", bias=p['b_out'])
    mid_b = [gt1, p['ln1_g'], p['ln1_b'], sc2, sh2]
    (h1, h2in), _ = _rowwise("mid", lambda tv, bv: (_seg_mid(tv, bv), []), [x, mix], mid_b, [(D_MODEL, F32), (D_MODEL, _MXU_DTYPE)], [], tile)
    act = _mm("ff1", h2in, w_ff1_t, "nt", bias=p['b_ff1'], epi=lambda t: _seg_relu2([t], [])[0], epi_dtype=_MXU_DTYPE, raw=False)
    ff = _mm("ff2", act, w_ff2, "nn", bias=p['b_ff2'])

    def loss_fn(tv, bv):
        h1_t, ff_t, tgt = tv
        val, grads = jax.value_and_grad(lambda a0_, a1_, b0_, b1_, b2_: _seg_loss([a0_, a1_, tgt], [b0_, b1_, b2_]), argnums=(0, 1, 2, 3, 4))(h1_t, ff_t, *bv)
        return [grads[0], grads[1]], [grads[2], grads[3], grads[4], jnp.sum(grads[1], axis=0, keepdims=True), jnp.full((1, 128), val, F32)]

    (dh1_a, dff), (d_gt2, d_ln2_g, d_ln2_b, d_b_ff2, loss_row) = _rowwise(
        "loss", loss_fn, [h1, ff, target], [gt2, p['ln2_g'], p['ln2_b']], [(D_MODEL, F32), (D_MODEL, _MXU_DTYPE)], [(1, D_MODEL)] * 4 + [(1, 128)], tile)

    g = {}
    g['ln2_g'], g['ln2_b'], g['b_ff2'] = d_ln2_g, d_ln2_b, d_b_ff2
    gw = {}
    gw['w_ff2'] = _mm("g_w_ff2", act, dff, "tn")
    df1, g['b_ff1'] = _mm("d_act", dff, w_ff2, "nt", beside=act, epi=lambda d_, a_: d_ * (2.0 * jnp.sqrt(a_.astype(F32))),
                          epi_dtype=_MXU_DTYPE, colsum=True)
    gw['w_ff1'] = _mm("g_w_ff1", df1, h2in, "tn")
    dh2in = _mm("d_h2in", df1, w_ff1_t, "nn")
    (dx_a, dmix), (d_gt1, g['ln1_g'], g['ln1_b'], d_sc2, d_sh2), (g['b_out'],) = _rowwise_vjp(
        "mid_bwd", _seg_mid, [x, mix], mid_b, [dh1_a, dh2in], tile, [0, 1], [0, 1, 2, 3, 4], t_dtypes=[F32, _MXU_DTYPE], colsum=[1])
    gw['w_out'] = _mm("g_w_out", merged, dmix, "tn")
    dmerged = _mm("d_merged", dmix, w_out, "nt", out_dtype=_MXU_DTYPE)
    (dproj, dpa, dpb), _, (cs_gates,) = _rowwise_vjp(
        "merge_bwd", _seg_merge, gates + [pa, pb], [], [dmerged], tile, [0, 1, 2, 3], [], t_dtypes=[_MXU_DTYPE] * 3, colsum=[0],
        finish=lambda dts, sc: [jnp.concatenate(dts[:2], axis=1), dts[2], dts[3]], out_widths=[2 * D_MODEL, D_MODEL, D_MODEL],
        into={0: (lax.empty((T, P_COLS), _MXU_DTYPE), 0)})
    gw['w_branch_a'] = _mm("g_w_branch_a", dpa, ya, "tn")
    gw['w_branch_b'] = _mm("g_w_branch_b", dpb, yb, "tn")
    dya = _mm("d_ya", dpa, w_ba_t, "nn")
    dyb = _mm("d_yb", dpb, w_bb_t, "nn")
    def send_rows(names):
        parts = []
        for n in names:
            per_dev = gw[n].reshape(_N_DEV, -1, D_MODEL)
            parts.append(jnp.pad(per_dev, ((0, 0), (0, _BIG[n][1] - per_dev.shape[1]), (0, 0))))
        return jnp.concatenate(parts, axis=1) if len(parts) > 1 else parts[0]

    send_early = send_rows(big_names[:-1])
    (dy, dr1, dk1, dv1, dg_), (d_r_k, g['gn_gain'], g['gn_bias']), _, sibling_early = _rowwise_vjp(
        "rwkv_post_bwd", _seg_rwkv_post, [y_, r_, k2_, v_, g_], post_b, [dyb], tile, [0, 1, 2, 3, 4], [0, 1, 2],
        rider=(send_early, _sibling_def, (0.0, 1.0)))
    g['r_k'] = d_r_k
    partials_early = _chip_partials("chip_partials_early", send_early, sibling_early, 128, _COMM_DTYPE)
    dr2, dlw, dk2, dv2, da, db, landed_early = _scan_bwd(*scan_in, states, inverses, dy, rider=(partials_early, _chips_def, (0.0, 1.0)))
    pre_tile = 256
    last_step = T // pre_tile - 1

    def pre_prep(prim):
        z_t, halo_t = prim
        return [z_t, _shift_down(z_t, halo_t, pl.program_id(0) == last_step)]

    def pre_finish(dts, sc):
        dz_direct, dprev = dts
        (row_after,) = sc

        @pl.when(pl.program_id(0) == 0)
        def _():
            row_after[...] = jnp.zeros_like(row_after)

        dz = dz_direct + _shift_up(dprev, row_after[...])
        row_after[...] = dprev[:_HALO]
        return [dz]

    (dproj,), (d_mu, g['w0'], d_wd, g['a0'], d_wa, d_wg, g['k_k'], g['k_a']), (cs_rw,) = _rowwise_vjp(
        "rwkv_pre_bwd", _seg_rwkv_pre, [z_rw, z_rw_halo], pre_b, [[dr1, dr2], dlw, [dk1, dk2], [dv1, dv2], da, db, dg_], pre_tile,
        [0, 1], [0, 1, 2, 3, 4, 5, 6, 7], t_dtypes=[_MXU_DTYPE], colsum=[0], prep=pre_prep, finish=pre_finish,
        out_widths=[RW_COLS], reverse=True, scratch=[(_HALO, RW_COLS)], into={0: (dproj, 1)})
    g['mu_shift'] = d_mu[:, :RW_USED]
    for n, d_ in (('w_decay_up', d_wd), ('w_aaa_up', d_wa), ('w_gate_up', d_wg)):
        r0, nr = _LORA[n]
        g[n] = d_[r0:r0 + nr]
    (dproj,), (g['g_ln_v'], g['b_ln_v'], g['w_spatial'], d_b_tg), (cs_g,) = _rowwise_vjp(
        "gmlp_bwd", _seg_gmlp, [z_gmlp], gmlp_b, [dya], G_CHUNK, [0], [0, 1, 2, 3], t_dtypes=[_MXU_DTYPE], colsum=[0],
        into={0: (dproj, 4)})
    g['b_spatial'] = d_b_tg[:, :8].T
    g['b_in'] = _from_padded(jnp.concatenate([cs_gates, cs_rw, cs_g], axis=1), 1)
    small_names = [n for n in _WEIGHTS if n not in _BIG and n not in ('w_ada', 'b_ada')]
    packed_g, g_places = _pack_rows([g[n] for n in small_names] + [loss_row], row_mult=256)
    gw_in_t, small_all = _mm("g_w_in", dproj, h, "tn", rider=(packed_g, _gather_def, (0.0, 0.6, 1.0)))
    gw['w_in'] = _from_padded(gw_in_t, 0)
    send_late = send_rows(big_names[-1:])
    sibling_late = _exchange("pair_exchange_late", send_late, _sibling_def)
    partials_late = _chip_partials("chip_partials_late", send_late, sibling_late, 128, _COMM_DTYPE)
    dh, landed_late = _mm("d_h", dproj, w_in_t, "nn", rider=(partials_late, _chips_def, (0.0, 1.0)))

    def mod1_bwd(tv, bv):
        x_t, dh_t, dxa_t = tv
        (sc,) = bv
        return [dxa_t + dh_t * (1.0 + sc)], [jnp.sum(dh_t * x_t, axis=0, keepdims=True), jnp.sum(dh_t, axis=0, keepdims=True)]

    (grad_x,), (d_sc1, d_sh1) = _rowwise("modulate1_bwd", mod1_bwd, [x, dh, dx_a], [sc1], [(D_MODEL, F32)], [(1, D_MODEL)] * 2, tile)

    dmod = jnp.concatenate([d_sh1, d_sc1, d_gt1, d_sh2, d_sc2, d_gt2], axis=1).reshape(6 * D_MODEL // 128, 128)
    dmod_all = _all_gather("gather_dmod", dmod)
    g['b_ada'] = _sum_leading("sum_dmod", dmod_all).reshape(1, 6 * D_MODEL)
    dmod_mine = lax.dynamic_slice(dmod_all.reshape(_N_DEV, 6 * D_MODEL), (0, me * 768), (_N_DEV, 768))
    g_w_ada = _mm("g_w_ada", c_act_all, _pad_rows(dmod_mine, 16), "tn")

    small_sum = _unpack_rows(_sum_leading("sum_small", small_all), g_places)
    loss = small_sum.pop()[0, 0]
    for n, t in zip(small_names, small_sum, strict=True):
        g[n] = t
    for n in _LORA:
        g[n] = lax.dynamic_slice(g[n], (0, me * R_HEAD), (g[n].shape[0], R_HEAD))
    g['w_ada'] = g_w_ada

    summed = jnp.concatenate([_sum_leading("sum_early", landed_early), _sum_leading("sum_late", landed_late)], axis=0)
    g.update(_unpack_big(summed, p))

    delta, new_m, new_v = {}, {}, {}
    own_call = ['w_ada'] + big_names
    for n in own_call:
        if n == 'w_in':
            r0 = _BIG[n][0]
            outs_t = _adamw("adamw_" + n, p[n].T, summed[r0:r0 + IN_SHARD], m[n].T, v[n].T)
            delta[n], new_m[n], new_v[n] = [o.T for o in outs_t]
        else:
            delta[n], new_m[n], new_v[n] = _adamw("adamw_" + n, p[n], g[n], m[n], v[n])
    rest = [n for n in _WEIGHTS if n not in own_call]
    outs = _adamw_many("adamw_rest", *[[d[n].reshape(p[n].shape) for n in rest] for d in (p, g, m, v)])
    for d, o in zip((delta, new_m, new_v), outs, strict=True):
        d.update(zip(rest, o, strict=True))
    return loss, grad_x, g, delta, new_m, new_v


def kernel(x, c, w_ada, b_ada, w_in, b_in, g_ln_v, b_ln_v, w_spatial, b_spatial, mu_shift, w0, w_decay_up, a0, w_aaa_up, w_gate_up, k_k, k_a, r_k, gn_gain, gn_bias, w_branch_a, w_branch_b, w_out, b_out, ln1_g, ln1_b, w_ff1, b_ff1, w_ff2, b_ff2, ln2_g, ln2_b, loss_target, m_w_ada, m_b_ada, m_w_in, m_b_in, m_g_ln_v, m_b_ln_v, m_w_spatial, m_b_spatial, m_mu_shift, m_w0, m_w_decay_up, m_a0, m_w_aaa_up, m_w_gate_up, m_k_k, m_k_a, m_r_k, m_gn_gain, m_gn_bias, m_w_branch_a, m_w_branch_b, m_w_out, m_b_out, m_ln1_g, m_ln1_b, m_w_ff1, m_b_ff1, m_w_ff2, m_b_ff2, m_ln2_g, m_ln2_b, v_w_ada, v_b_ada, v_w_in, v_b_in, v_g_ln_v, v_b_ln_v, v_w_spatial, v_b_spatial, v_mu_shift, v_w0, v_w_decay_up, v_a0, v_w_aaa_up, v_w_gate_up, v_k_k, v_k_a, v_r_k, v_gn_gain, v_gn_bias, v_w_branch_a, v_w_branch_b, v_w_out, v_b_out, v_ln1_g, v_ln1_b, v_w_ff1, v_b_ff1, v_w_ff2, v_b_ff2, v_ln2_g, v_ln2_b):
    given = dict(locals())
    shapes = {n: given[n].shape for n in _WEIGHTS}
    def two_d(a_):
        a_ = a_[0]
        return a_.reshape(1, -1) if a_.ndim == 1 else a_
    p = {n: two_d(given[n]) for n in _WEIGHTS}
    m = {n: two_d(given["m_" + n]) for n in _WEIGHTS}
    v = {n: two_d(given["v_" + n]) for n in _WEIGHTS}
    loss, grad_x, g, delta, new_m, new_v = _step(p, m, v, x[0], c, loss_target[0])
    outs = [loss, grad_x[None]]
    for d in (g, delta, new_m, new_v):
        outs += [d[n].reshape(shapes[n]) for n in _WEIGHTS]
    return tuple(outs)
```

```python
import functools

import jax
import jax.numpy as jnp
from jax import lax
from jax.experimental import pallas as pl
from jax.experimental.pallas import tpu as pltpu

F32 = jnp.float32
_MXU_DTYPE = jnp.bfloat16
_HI = lax.Precision.HIGHEST
_VMEM_LIMIT = 48 * 1024 * 1024
_MESH_ID = pl.DeviceIdType.MESH
_N_DEV = 8

D_MODEL = 1024
G_WIDTH = 512
G_CHUNK = 128
R_WIDTH = 512
R_HEADS = 8
R_HEAD = 64
LORA_W, LORA_A, LORA_G = 32, 32, 96
D_FF = 4096
ALPHA = 2.0 ** 0.25
LN_EPS = 1e-5
GN_EPS = 64e-5
SCAN_CHUNK = 64
SCAN_PER_STEP = 4
ADAM_LR, ADAM_B1, ADAM_B2, ADAM_EPS, ADAM_WD, ADAM_STEP = 0.001, 0.9, 0.999, 1e-08, 0.01, 10

P_COLS = 5120
RW_COLS = 2048
RW_USED = 3 * R_WIDTH + LORA_W + LORA_A + LORA_G
LORA_PAD = 256
IN_COLS = 2 * G_WIDTH + RW_USED + 2 * D_MODEL


def _cparams(sem=None, **kw):
    if sem is not None:
        kw["dimension_semantics"] = sem
    return pltpu.CompilerParams(vmem_limit_bytes=_VMEM_LIMIT, **kw)


def _dot(a, b, dims=(((1,), (0,)), ((), ())), hi=False):
    if hi:
        return lax.dot_general(a.astype(F32), b.astype(F32), dims, precision=_HI, preferred_element_type=F32)
    return lax.dot_general(a.astype(_MXU_DTYPE), b.astype(_MXU_DTYPE), dims, preferred_element_type=F32)


_NN = (((1,), (0,)), ((), ()))
_NT = (((1,), (1,)), ((), ()))
_TN = (((0,), (0,)), ((), ()))


def _pick(n, pref):
    for t in pref:
        if n % t == 0:
            return t
    return n


def _mm(name, a, b, mode, bias=None, out_dtype=F32, epi=None, epi_dtype=None, raw=True, beside=None, colsum=False, rider=None,
        tm=None, tn=None, tk=None):
    if mode == "nn":
        (M, K), (_, N) = a.shape, b.shape
    elif mode == "nt":
        (M, K), (N, _) = a.shape, b.shape
    else:
        (K, M), (_, N) = a.shape, b.shape
    tm = tm or _pick(M, (2048, 1280, 1024, 512, 256, 128, 64, 32, 16, 8))
    tn = tn or _pick(N, (1024, 512, 640, 384, 256, 128))
    tk = tk or _pick(K, (1024, 512, 256, 128))
    nk = K // tk
    dims = {"nn": _NN, "nt": _NT, "tn": _TN}[mode]
    a_spec = pl.BlockSpec((tk, tm), lambda i, j, k: (k, i)) if mode == "tn" else pl.BlockSpec((tm, tk), lambda i, j, k: (i, k))
    b_spec = pl.BlockSpec((tn, tk), lambda i, j, k: (j, k)) if mode == "nt" else pl.BlockSpec((tk, tn), lambda i, j, k: (k, j))
    o_spec = pl.BlockSpec((tm, tn), lambda i, j, k: (i, j))
    has_bias, has_beside = bias is not None, beside is not None
    two = epi is not None and not has_beside and raw
    only_epi = epi is not None and not has_beside and not raw
    grid = (M // tm, N // tn, nk)
    ride_shape, ride_sems, _ = rider[1](rider[0]) if rider else (None, [], None)

    def body(*refs):
        refs = list(refs)
        n_in = 2 + has_bias + has_beside
        if rider:
            sem_refs = [refs.pop() for _ in ride_sems][::-1]
            ride_out = refs.pop(n_in + 1 + 1 + two)
            ride_in = refs.pop(n_in)
            step = (pl.program_id(0) * grid[1] + pl.program_id(1)) * grid[2] + pl.program_id(2)
            _ride(rider, step, grid[0] * grid[1] * grid[2], ride_in, ride_out, sem_refs)
        a_ref, b_ref = refs[0], refs[1]
        bias_ref = refs[2] if has_bias else None
        beside_ref = refs[n_in - 1] if has_beside else None
        outs = refs[n_in:]
        o_ref, acc_ref = outs[0], outs[-1]
        k = pl.program_id(2)

        @pl.when(k == 0)
        def _():
            acc_ref[...] = jnp.zeros_like(acc_ref)

        acc_ref[...] += _dot(a_ref[...], b_ref[...], dims)

        @pl.when(k == nk - 1)
        def _():
            res = acc_ref[...]
            if has_bias:
                res = res + bias_ref[...]
            if has_beside:
                val = epi(res, beside_ref[...])
                o_ref[...] = val.astype(o_ref.dtype)
                if colsum:
                    outs[1][...] = jnp.broadcast_to(jnp.sum(val, axis=0, keepdims=True), outs[1].shape)
            elif only_epi:
                o_ref[...] = epi(res).astype(o_ref.dtype)
            else:
                o_ref[...] = res.astype(o_ref.dtype)
                if two:
                    outs[1][...] = epi(res).astype(outs[1].dtype)

    in_specs = [a_spec, b_spec]
    args = [a, b]
    if has_bias:
        in_specs.append(pl.BlockSpec((1, tn), lambda i, j, k: (0, j)))
        args.append(bias)
    if has_beside:
        in_specs.append(o_spec)
        args.append(beside)
    out_shape = [jax.ShapeDtypeStruct((M, N), epi_dtype if (has_beside or only_epi) else out_dtype)]
    out_specs = [o_spec]
    if two:
        out_shape.append(jax.ShapeDtypeStruct((M, N), epi_dtype))
        out_specs.append(o_spec)
    if colsum:
        assert has_beside and not rider
        out_shape.append(jax.ShapeDtypeStruct((grid[0] * 8, N), F32))
        out_specs.append(pl.BlockSpec((8, tn), lambda i, j, k: (i, j)))
    if rider:
        in_specs.append(_HBM)
        args.append(rider[0])
        out_shape.append(ride_shape)
        out_specs.append(_HBM)
    res = pl.pallas_call(
        body, name=name, grid=grid, in_specs=in_specs, out_specs=out_specs, out_shape=out_shape,
        scratch_shapes=[pltpu.VMEM((tm, tn), F32)] + list(ride_sems),
        compiler_params=_cparams(("arbitrary",) * 3 if rider else ("parallel", "parallel", "arbitrary")),
    )(*args)
    if colsum:
        return res[0], _sum_leading(name + "_colsum", res[1].reshape(grid[0], 8, N))[0:1]
    return res if (two or rider) else res[0]


_HALO = 8


def _rowwise(name, fn, tiled, bcast, tiled_out, red_out, tile, reverse=False, scratch=(), rider=None):
    tiled = [t if isinstance(t, tuple) else (t, t.shape[1], 0) for t in tiled]
    T = next(t[0] for t in tiled if not isinstance(t[0], str)).shape[0]
    n = T // tile
    nt, nb, nto, nsc, nro = len(tiled), len(bcast), len(tiled_out), len(scratch), len(red_out)
    ride_shape, ride_sems, _ = rider[1](rider[0]) if rider else (None, [], None)
    nr = 1 if rider else 0
    into = [(j, t[2], t[3]) for j, t in enumerate(tiled_out) if len(t) == 4]
    na = len(into)

    def row_block(i):
        return n - 1 - i if reverse else i

    def body(*refs):
        i2 = nt + nb
        o0 = i2 + nr + na
        o1, o2 = o0 + nto, o0 + nto + nro
        s0 = o2 + nr
        if rider:
            _ride(rider, pl.program_id(0), n, refs[i2], refs[o2], refs[s0 + nsc:])
        t_refs, b_refs = refs[:nt], refs[nt:i2]
        to_refs, ro_refs = refs[o0:o1], refs[o1:o2]
        extra = (list(refs[s0:s0 + nsc]),) if nsc else ()
        touts, routs = fn([r[...] for r in t_refs], [r[...] for r in b_refs], *extra)
        for r, v in zip(to_refs, touts, strict=True):
            r[...] = v.astype(r.dtype)
        if ro_refs:
            i = pl.program_id(0)

            @pl.when(i == 0)
            def _():
                for r, v in zip(ro_refs, routs, strict=True):
                    r[...] = v.astype(F32)

            @pl.when(i > 0)
            def _():
                for r, v in zip(ro_refs, routs, strict=True):
                    r[...] += v.astype(F32)

    def whole(shape):
        nd = len(shape)
        return pl.BlockSpec(tuple(shape), lambda i: (0,) * nd)

    per_tile = tile // _HALO
    in_specs, arrays = [], []
    for t in tiled:
        if isinstance(t[0], str):
            _, arr, w, cb = t
            in_specs.append(pl.BlockSpec((_HALO, w), functools.partial(lambda i, cb: (jnp.maximum(row_block(i) * per_tile - 1, 0), cb), cb=cb)))
        else:
            arr, w, cb = t
            in_specs.append(pl.BlockSpec((tile, w), functools.partial(lambda i, cb: (row_block(i), cb), cb=cb)))
        arrays.append(arr)
    in_specs += [whole(b.shape) for b in bcast]
    out_specs, out_shape = [], []
    for t in tiled_out:
        cb = t[3] if len(t) == 4 else 0
        out_specs.append(pl.BlockSpec((tile, t[0]), functools.partial(lambda i, cb: (row_block(i), cb), cb=cb)))
        out_shape.append(jax.ShapeDtypeStruct(t[2].shape if len(t) == 4 else (T, t[0]), t[1]))
    out_specs += [whole(s) for s in red_out]
    out_shape += [jax.ShapeDtypeStruct(tuple(s), F32) for s in red_out]
    ride_args = []
    if rider:
        in_specs.append(_HBM)
        out_specs.append(_HBM)
        out_shape.append(ride_shape)
        ride_args = [rider[0]]
    in_specs += [pl.BlockSpec(memory_space=pl.ANY)] * na
    aliases = {nt + nb + nr + k: j for k, (j, _, _) in enumerate(into)}
    res = pl.pallas_call(
        body, name=name, grid=(n,), in_specs=in_specs, out_specs=out_specs, out_shape=out_shape,
        scratch_shapes=[pltpu.VMEM(tuple(s), F32) for s in scratch] + list(ride_sems),
        input_output_aliases=aliases, compiler_params=_cparams(("arbitrary",)),
    )(*arrays, *bcast, *ride_args, *[buf for (_, buf, _) in into])
    if rider:
        return list(res[:nto]), list(res[nto:nto + nro]), res[-1]
    return list(res[:nto]), list(res[nto:])


def _rowwise_vjp(name, f, tiled, bcast, cts, tile, wrt_t, wrt_b, t_dtypes=None, colsum=(), prep=None, finish=None,
                 out_widths=None, reverse=False, scratch=(), rider=None, into=None):
    tiled = [t if isinstance(t, tuple) else (t, t.shape[1], 0) for t in tiled]
    npr = len(tiled)
    t_dtypes = t_dtypes or [F32] * len(wrt_t)
    groups = [c if isinstance(c, list) else [c] for c in cts]
    cts = [a_ for grp in groups for a_ in grp]

    def fn(tv, bv, sc=None):
        prim, flat_ct = tv[:npr], list(tv[npr:])
        if prep is not None:
            prim = prep(prim)
        ct = []
        for grp in groups:
            parts = [flat_ct.pop(0).astype(F32) for _ in grp]
            ct.append(functools.reduce(lambda p_, q_: p_ + q_, parts))

        def g(dt_vals, db_vals):
            full_t, full_b = list(prim), list(bv)
            for i, v in zip(wrt_t, dt_vals, strict=True):
                full_t[i] = v
            for j, v in zip(wrt_b, db_vals, strict=True):
                full_b[j] = v
            return f(full_t, full_b)

        outs, pull = jax.vjp(g, [prim[i].astype(F32) for i in wrt_t], [bv[j] for j in wrt_b])
        dts, dbs = pull([c.astype(o.dtype) for c, o in zip(ct, outs, strict=True)])
        if finish is not None:
            dts = finish(dts, sc)
        sums = [jnp.sum(dts[i].astype(F32), axis=0, keepdims=True) for i in colsum]
        return dts, list(dbs) + sums

    widths = out_widths or [tiled[i][1] for i in wrt_t]
    tiled_out = [(w, dt) + tuple((into or {}).get(j, ())) for j, (w, dt) in enumerate(zip(widths, t_dtypes, strict=True))]
    red_out = [bcast[j].shape for j in wrt_b] + [(1, widths[i]) for i in colsum]
    res = _rowwise(name, fn, tiled + list(cts), bcast, tiled_out, red_out, tile, reverse=reverse, scratch=scratch, rider=rider)
    dts, reds = res[0], res[1]
    nb = len(wrt_b)
    return (dts, reds[:nb], reds[nb:]) + tuple(res[2:])


def _layer_norm(x, g, b, eps):
    mu = jnp.mean(x, axis=-1, keepdims=True)
    xc = x - mu
    var = jnp.mean(xc * xc, axis=-1, keepdims=True)
    return xc * lax.rsqrt(var + eps) * g + b


def _gelu_tanh(x):
    return 0.5 * x * (1.0 + jnp.tanh(0.7978845608028654 * (x + 0.044715 * (x * x * x))))


def _sigmoid(x):
    return 1.0 / (1.0 + jnp.exp(-x))


def _seg_modulate(tv, bv):
    (x,), (sc, sh) = tv, bv
    return [x * (1.0 + sc) + sh]


def _seg_gmlp(tv, bv):
    (z,), (g_ln, b_ln, ws, b_tg, expand) = tv, bv
    bias_full = _dot(b_tg, expand, hi=True)
    zz = _gelu_tanh(z)
    u, v = zz[:, :G_WIDTH], zz[:, G_WIDTH:]
    v = _layer_norm(v, g_ln, b_ln, LN_EPS)
    row = lax.broadcasted_iota(jnp.int32, (G_CHUNK, G_CHUNK), 0)
    col = lax.broadcasted_iota(jnp.int32, (G_CHUNK, G_CHUNK), 1)
    causal = col <= row
    first_group = lax.broadcasted_iota(jnp.int32, (G_CHUNK, 128), 1) < 64
    parts = []
    for p in range(4):
        vp = v[:, 128 * p:128 * (p + 1)]
        s_even = _dot(jnp.where(causal, ws[2 * p], 0.0), vp)
        s_odd = _dot(jnp.where(causal, ws[2 * p + 1], 0.0), vp)
        parts.append(jnp.where(first_group, s_even, s_odd))
    s = jnp.concatenate(parts, axis=1) + bias_full
    return [u * s]


def _split2(x):
    hi = x.astype(_MXU_DTYPE)
    return hi, (x - hi.astype(F32)).astype(_MXU_DTYPE)


@jax.custom_vjp
def _group_sum(x, ones_blocks):
    hi, lo = _split2(x)
    w = ones_blocks.shape[0]
    return jnp.concatenate([_dot(hi[:, j:j + w], ones_blocks) + _dot(lo[:, j:j + w], ones_blocks) for j in range(0, x.shape[1], w)], axis=1)


def _group_sum_fwd(x, ones_blocks):
    return _group_sum(x, ones_blocks), ones_blocks


def _group_sum_bwd(ones_blocks, ct):
    return _group_sum(ct, ones_blocks), jnp.zeros_like(ones_blocks)


_group_sum.defvjp(_group_sum_fwd, _group_sum_bwd)


def _shift_down(z, halo, is_first):
    _, W = z.shape
    rolled = pltpu.roll(z, 1, 0)
    before = jnp.where(is_first, 0.0, pltpu.roll(halo, 1, 0))
    top_row = lax.broadcasted_iota(jnp.int32, (_HALO, W), 0) == 0
    return jnp.concatenate([jnp.where(top_row, before, rolled[:_HALO]), rolled[_HALO:]], axis=0)


def _shift_up(d, after):
    tile, W = d.shape
    rolled = pltpu.roll(d, tile - 1, 0)
    last_row = lax.broadcasted_iota(jnp.int32, (_HALO, W), 0) == _HALO - 1
    bottom = jnp.where(last_row, pltpu.roll(after, _HALO - 1, 0), rolled[tile - _HALO:])
    return jnp.concatenate([rolled[:tile - _HALO], bottom], axis=0)


def _seg_rwkv_pre(tv, bv):
    (z, prev), (mu, w0, wd, a0, wa, wg, k_k, k_a, gsum) = tv, bv
    zs = z + (prev - z) * mu
    r, k, v = zs[:, 0:512], zs[:, 512:1024], zs[:, 1024:1536]
    zl = zs[:, 3 * R_WIDTH:3 * R_WIDTH + LORA_PAD]
    x = w0 + _dot(jnp.tanh(zl), wd)
    softplus = jnp.maximum(-x, 0.0) + jnp.log(1.0 + jnp.exp(-jnp.abs(x)))
    lw = -jnp.exp(-softplus - 0.5)
    a = _sigmoid(a0 + _dot(zl, wa))
    g = _dot(_sigmoid(zl), wg)
    kk = k * k_k
    nrm = jnp.sqrt(_group_sum(kk * kk, gsum))
    kk = kk / jnp.maximum(nrm, 1e-12)
    k2 = k * (1.0 + (a - 1.0) * k_a)
    return [r, lw, k2, v, -kk, kk * a, g]


def _seg_rwkv_post(tv, bv):
    (y, r, k2, v, g), (r_k, gain, bias, gsum) = tv, bv
    mu = _group_sum(y, gsum) * (1.0 / R_HEAD)
    yc = y - mu
    var = _group_sum(yc * yc, gsum) * (1.0 / R_HEAD)
    yn = yc * lax.rsqrt(var + GN_EPS) * gain + bias
    bonus = _group_sum(r * k2 * r_k, gsum) * v
    return [(yn + bonus) * g]


def _seg_merge(tv, bv):
    (ga, gb, pa, pb), () = tv, bv
    return [_sigmoid(ga) * pa + _sigmoid(gb) * pb]


def _seg_mid(tv, bv):
    (x, mix), (gt1, g1, b1, sc2, sh2) = tv, bv
    h1 = _layer_norm(ALPHA * x + gt1 * mix, g1, b1, LN_EPS)
    return [h1, h1 * (1.0 + sc2) + sh2]


def _seg_relu2(tv, bv):
    (f1,), () = tv, bv
    return [jnp.square(jnp.maximum(f1, 0.0))]


def _seg_loss(tv, bv):
    (h1, ff, target), (gt2, g2, b2) = tv, bv
    out = _layer_norm(ALPHA * h1 + gt2 * ff, g2, b2, LN_EPS)
    err = jnp.square(out - target)
    return 0.5 * jnp.sum(jnp.mean(err, axis=-1))


_BNN = (((2,), (1,)), ((0,), (0,)))
_BNT = (((2,), (2,)), ((0,), (0,)))
_BTN = (((1,), (1,)), ((0,), (0,)))


def _tri_dot(x, dims):
    L = x.shape[0]
    tri = (lax.broadcasted_iota(jnp.int32, (L, L), 1) <= lax.broadcasted_iota(jnp.int32, (L, L), 0)).astype(F32)
    hi, lo = _split2(x)
    return _dot(tri, hi, dims) + _dot(tri, lo, dims)


@jax.custom_vjp
def _running_sum(x):
    return _tri_dot(x, _NN)


_running_sum.defvjp(lambda x: (_tri_dot(x, _NN), None), lambda _, ct: (_tri_dot(ct, _TN),))


def _cut_heads(x):
    return jnp.stack([x[:, R_HEAD * h:R_HEAD * (h + 1)] for h in range(R_HEADS)])


def _join_heads(x):
    return jnp.concatenate([x[h] for h in range(R_HEADS)], axis=1)


@jax.custom_vjp
def _split_heads(x):
    return _cut_heads(x)


@jax.custom_vjp
def _merge_heads(x):
    return _join_heads(x)


_split_heads.defvjp(lambda x: (_cut_heads(x), None), lambda _, ct: (_join_heads(ct),))
_merge_heads.defvjp(lambda x: (_join_heads(x), None), lambda _, ct: (_cut_heads(ct),))


def _inverse_pullback(inv, ct):
    return _dot(_dot(inv, ct, _BTN), inv, _BNT)


@jax.custom_vjp
def _unit_lower_inverse(n_mat):
    H, L, _ = n_mat.shape
    eye = lax.broadcasted_iota(jnp.int32, (H, L, L), 1) == lax.broadcasted_iota(jnp.int32, (H, L, L), 2)
    inv = jnp.where(eye, 1.0, 0.0) + n_mat
    pw = n_mat
    n = 2
    while n < L:
        pw = _dot(pw, pw, _BNN)
        inv = inv + _dot(inv, pw, _BNN)
        n *= 2
    return inv


def _unit_lower_inverse_fwd(n_mat):
    inv = _unit_lower_inverse(n_mat)
    return inv, inv


_unit_lower_inverse.defvjp(_unit_lower_inverse_fwd, lambda inv, ct: (_inverse_pullback(inv, ct),))


@jax.custom_vjp
def _known_inverse(n_mat, inv):
    return inv


_known_inverse.defvjp(lambda n_mat, inv: (inv, inv), lambda inv, ct: (_inverse_pullback(inv, ct), jnp.zeros_like(inv)))


def _scan_chunk(r, lw, k, v, a, b, s0, inv=None, with_inverse=False):
    L, H = r.shape[0], R_HEADS
    cs = _running_sum(lw)
    cs_end = cs[L - 1:L, :]
    p, p_inv, to_end = jnp.exp(cs), jnp.exp(-cs), jnp.exp(cs_end - cs)
    at, bt, kt, rt = [_split_heads(t) for t in (a * jnp.exp(cs - lw), b * p_inv, k * p_inv, r * p)]
    b_end, k_end, v = [_split_heads(t) for t in (b * to_end, k * to_end, v)]
    row = lax.broadcasted_iota(jnp.int32, (H, L, L), 1)
    col = lax.broadcasted_iota(jnp.int32, (H, L, L), 2)
    incl, strict = col <= row, col < row
    a_ab = jnp.where(strict, _dot(at, bt, _BNT), 0.0)
    a_ak = jnp.where(strict, _dot(at, kt, _BNT), 0.0)
    a_rb = jnp.where(incl, _dot(rt, bt, _BNT), 0.0)
    a_rk = jnp.where(incl, _dot(rt, kt, _BNT), 0.0)
    inv = _unit_lower_inverse(a_ab) if inv is None else _known_inverse(a_ab, inv)
    u = _dot(inv, _dot(at, s0, _BNT) + _dot(a_ak, v, _BNN), _BNN)
    y = _merge_heads(_dot(rt, s0, _BNT) + _dot(a_rb, u, _BNN) + _dot(a_rk, v, _BNN))
    s1 = s0 * _split_heads(jnp.exp(cs_end)) + _dot(u, b_end, _BTN) + _dot(v, k_end, _BTN)
    return (y, s1, inv) if with_inverse else (y, s1)


def _scan_fwd(r, lw, k, v, a, b, rider):
    T = r.shape[0]
    H, N, L, P = R_HEADS, R_HEAD, SCAN_CHUNK, SCAN_PER_STEP
    nc = T // L
    steps = nc // P
    ride_shape, ride_sems, _ = rider[1](rider[0])

    def body(r_ref, lw_ref, k_ref, v_ref, a_ref, b_ref, ride_in, y_ref, st_ref, inv_ref, ride_out, s_ref, *sem_refs):
        _ride(rider, pl.program_id(0), steps, ride_in, ride_out, sem_refs)

        @pl.when(pl.program_id(0) == 0)
        def _():
            s_ref[...] = jnp.zeros_like(s_ref)

        s0 = s_ref[...]
        for j in range(P):
            rows = pl.ds(j * L, L)
            st_ref[j] = s0
            y, s0, inv = _scan_chunk(*[t[rows, :] for t in (r_ref, lw_ref, k_ref, v_ref, a_ref, b_ref)], s0, with_inverse=True)
            y_ref[rows, :] = y
            inv_ref[j] = inv
        s_ref[...] = s0

    blk = pl.BlockSpec((P * L, R_WIDTH), lambda c: (c, 0))
    per_chunk = pl.BlockSpec((P, H, N, N), lambda c: (c, 0, 0, 0))
    return pl.pallas_call(
        body, name="scan_fwd", grid=(steps,), in_specs=[blk] * 6 + [_HBM], out_specs=[blk, per_chunk, per_chunk, _HBM],
        out_shape=[jax.ShapeDtypeStruct((T, R_WIDTH), F32)] + [jax.ShapeDtypeStruct((nc, H, N, N), F32)] * 2 + [ride_shape],
        scratch_shapes=[pltpu.VMEM((H, N, N), F32)] + list(ride_sems),
        compiler_params=_cparams(("arbitrary",)),
    )(r, lw, k, v, a, b, rider[0])


def _scan_bwd(r, lw, k, v, a, b, states, inverses, dy, rider):
    T = r.shape[0]
    H, N, L, P = R_HEADS, R_HEAD, SCAN_CHUNK, SCAN_PER_STEP
    nc = T // L
    steps = nc // P
    ride_shape, ride_sems, _ = rider[1](rider[0])

    def body(r_ref, lw_ref, k_ref, v_ref, a_ref, b_ref, st_ref, inv_ref, dy_ref, ride_in,
             dr_ref, dlw_ref, dk_ref, dv_ref, da_ref, db_ref, ride_out, ds_ref, *sem_refs):
        _ride(rider, pl.program_id(0), steps, ride_in, ride_out, sem_refs)

        @pl.when(pl.program_id(0) == 0)
        def _():
            ds_ref[...] = jnp.zeros_like(ds_ref)

        ds = ds_ref[...]
        for j in reversed(range(P)):
            rows = pl.ds(j * L, L)
            args = [t[rows, :] for t in (r_ref, lw_ref, k_ref, v_ref, a_ref, b_ref)] + [st_ref[j]]
            inv = inv_ref[j]
            _, pull = jax.vjp(lambda *xs, inv=inv: _scan_chunk(*xs, inv=inv), *args)
            grads = pull((dy_ref[rows, :], ds))
            for o_ref, g_ in zip((dr_ref, dlw_ref, dk_ref, dv_ref, da_ref, db_ref), grads[:6], strict=True):
                o_ref[rows, :] = g_
            ds = grads[6]
        ds_ref[...] = ds

    blk = pl.BlockSpec((P * L, R_WIDTH), lambda c: (steps - 1 - c, 0))
    per_chunk = pl.BlockSpec((P, H, N, N), lambda c: (steps - 1 - c, 0, 0, 0))
    return pl.pallas_call(
        body, name="scan_bwd", grid=(steps,), in_specs=[blk] * 6 + [per_chunk, per_chunk, blk, _HBM], out_specs=[blk] * 6 + [_HBM],
        out_shape=[jax.ShapeDtypeStruct((T, R_WIDTH), F32)] * 6 + [ride_shape],
        scratch_shapes=[pltpu.VMEM((H, N, N), F32)] + list(ride_sems),
        compiler_params=_cparams(("arbitrary",)),
    )(r, lw, k, v, a, b, states, inverses, dy, rider[0])


def _place():
    x, y, c = lax.axis_index("x"), lax.axis_index("y"), lax.axis_index("c")
    return x, y, c


def _gather_def(block):
    R, C = block.shape

    def phases(x_ref, out_ref, send_sems, recv_sems, local_sem):
        x, y, c = _place()
        me, sibling = (x, y, c), (x, y, 1 - c)
        chips = [(1 - x, y), (x, 1 - y), (1 - x, 1 - y)]

        def slot(px, py, pc):
            return out_ref.at[4 * px + 2 * py + pc]

        def copy(k, blk, to, src=None):
            return pltpu.make_async_remote_copy(
                src_ref=slot(*blk) if src is None else src, dst_ref=slot(*blk),
                send_sem=send_sems.at[k], recv_sem=recv_sems.at[k], device_id=to, device_id_type=_MESH_ID)

        mine = pltpu.make_async_copy(x_ref, slot(*me), local_sem)
        first = [copy(0, me, sibling, src=x_ref)]
        first += [copy(1 + j, me, (*chip, c), src=x_ref) for j, chip in enumerate(chips)]
        passed = [copy(4 + j, (*chip, c), sibling) for j, chip in enumerate(chips)]

        def begin():
            mine.start()
            for cp in first:
                cp.start()

        def forward():
            for j, chip in enumerate(chips):
                copy(1 + j, (*chip, c), me).wait_recv()
                passed[j].start()

        def finish():
            copy(0, sibling, me).wait_recv()
            for j, chip in enumerate(chips):
                copy(4 + j, (*chip, 1 - c), me).wait_recv()
            for cp in first + passed:
                cp.wait_send()
            mine.wait()

        return [begin, forward, finish]

    sems = [pltpu.SemaphoreType.DMA((7,)), pltpu.SemaphoreType.DMA((7,)), pltpu.SemaphoreType.DMA]
    return jax.ShapeDtypeStruct((_N_DEV, R, C), block.dtype), sems, phases


def _sibling_def(blocks):
    _, R, C = blocks.shape

    def phases(x_ref, out_ref, send_sems, recv_sems):
        x, y, c = _place()
        copies = [pltpu.make_async_remote_copy(
            src_ref=x_ref.at[2 * q + (1 - c)], dst_ref=out_ref.at[q], send_sem=send_sems.at[q], recv_sem=recv_sems.at[q],
            device_id=(x, y, 1 - c), device_id_type=_MESH_ID) for q in range(4)]

        def begin():
            for cp in copies:
                cp.start()

        def finish():
            for cp in copies:
                cp.wait()

        return [begin, finish]

    return jax.ShapeDtypeStruct((4, R, C), blocks.dtype), [pltpu.SemaphoreType.DMA((4,)), pltpu.SemaphoreType.DMA((4,))], phases


def _chips_def(partials):
    def phases(x_ref, out_ref, send_sems, recv_sems, local_sem):
        x, y, c = _place()
        my_chip = 2 * x + y
        mine = pltpu.make_async_copy(x_ref.at[my_chip], out_ref.at[my_chip], local_sem)
        copies = []
        for rel in range(1, 4):
            px, py = (1 - x if (rel >> 1) & 1 else x), (1 - y if rel & 1 else y)
            copies.append(pltpu.make_async_remote_copy(
                src_ref=x_ref.at[2 * px + py], dst_ref=out_ref.at[my_chip],
                send_sem=send_sems.at[rel - 1], recv_sem=recv_sems.at[rel - 1],
                device_id=(px, py, c), device_id_type=_MESH_ID))

        def begin():
            mine.start()
            for cp in copies:
                cp.start()

        def finish():
            for cp in copies:
                cp.wait()
            mine.wait()

        return [begin, finish]

    sems = [pltpu.SemaphoreType.DMA((3,)), pltpu.SemaphoreType.DMA((3,)), pltpu.SemaphoreType.DMA]
    return jax.ShapeDtypeStruct(partials.shape, partials.dtype), sems, phases


_HBM = pl.BlockSpec(memory_space=pltpu.HBM)


def _exchange(name, array, definition):
    out_shape, sems, phases = definition(array)

    def body(x_ref, out_ref, *sem_refs):
        for phase in phases(x_ref, out_ref, *sem_refs):
            phase()

    return pl.pallas_call(body, name=name, in_specs=[_HBM], out_specs=_HBM, out_shape=out_shape, scratch_shapes=sems)(array)


def _ride(rider, step, nsteps, x_ref, out_ref, sem_refs):
    array, definition, fractions = rider
    for phase, frac in zip(definition(array)[2](x_ref, out_ref, *sem_refs), fractions, strict=True):
        pl.when(step == min(int(frac * nsteps), nsteps - 1))(phase)


def _all_gather(name, block):
    return _exchange(name, block, _gather_def)


def _chip_partials(name, blocks, from_sibling, tile, out_dtype):
    _, R, C = blocks.shape

    def body(x_ref, s_ref, o_ref):
        c = lax.axis_index("c")
        for q in range(4):
            o_ref[q] = (x_ref[2 * q + c].astype(F32) + s_ref[q].astype(F32)).astype(o_ref.dtype)

    return pl.pallas_call(
        body, name=name, grid=(R // tile,),
        in_specs=[pl.BlockSpec((_N_DEV, tile, C), lambda i: (0, i, 0)), pl.BlockSpec((4, tile, C), lambda i: (0, i, 0))],
        out_specs=pl.BlockSpec((4, tile, C), lambda i: (0, i, 0)), out_shape=jax.ShapeDtypeStruct((4, R, C), out_dtype),
        compiler_params=_cparams(("parallel",)),
    )(blocks, from_sibling)


def _sum_leading(name, x, tile=None):
    n, R, C = x.shape
    tile = tile or _pick(R, (512, 256, 128, 64, 32, 16, 8))

    def body(x_ref, o_ref):
        acc = x_ref[0].astype(F32)
        for k in range(1, n):
            acc = acc + x_ref[k].astype(F32)
        o_ref[...] = acc

    return pl.pallas_call(
        body, name=name, grid=(R // tile,), in_specs=[pl.BlockSpec((n, tile, C), lambda i: (0, i, 0))],
        out_specs=pl.BlockSpec((tile, C), lambda i: (i, 0)), out_shape=jax.ShapeDtypeStruct((R, C), F32),
        compiler_params=_cparams(("parallel",)),
    )(x)


def _adamw_update(w_, g_, m_, v_):
    m2 = ADAM_B1 * m_ + (1.0 - ADAM_B1) * g_
    v2 = ADAM_B2 * v_ + (1.0 - ADAM_B2) * jnp.square(g_)
    m_hat = m2 / (1.0 - ADAM_B1 ** ADAM_STEP)
    v_hat = v2 / (1.0 - ADAM_B2 ** ADAM_STEP)
    delta = -ADAM_LR * (m_hat / (jnp.sqrt(v_hat) + ADAM_EPS) + ADAM_WD * w_)
    return delta, m2, v2


def _adamw(name, w, g, m, v):
    R, C = w.shape
    tile = _pick(R, (256, 128, 64, 32, 16, 8))
    outs, _ = _rowwise(name, lambda tv, bv: (list(_adamw_update(*tv)), []), [w, g, m, v], [], [(C, F32)] * 3, [], tile)
    return outs


def _adamw_many(name, ws, gs, ms, vs):
    n = len(ws)

    def body(*refs):
        ins, outs = refs[:4 * n], refs[4 * n:]
        for i in range(n):
            res = _adamw_update(ins[i][...], ins[n + i][...], ins[2 * n + i][...], ins[3 * n + i][...])
            for j in range(3):
                outs[j * n + i][...] = res[j]

    out_shape = [jax.ShapeDtypeStruct(w.shape, F32) for w in ws] * 3
    res = pl.pallas_call(body, name=name, out_shape=out_shape, compiler_params=_cparams())(*ws, *gs, *ms, *vs)
    return res[:n], res[n:2 * n], res[2 * n:]


def _pack_rows(arrs, lanes=128, row_mult=8):
    flat, places, off = [], [], 0
    for a_ in arrs:
        n = a_.size
        flat.append(a_.reshape(-1).astype(F32))
        places.append((off, n, a_.shape))
        off += n
    total = -(-off // (lanes * row_mult)) * (lanes * row_mult)
    if total > off:
        flat.append(jnp.zeros((total - off,), F32))
    return jnp.concatenate(flat).reshape(total // lanes, lanes), places


def _unpack_rows(packed, places):
    flat = packed.reshape(-1)
    return [flat[o:o + n].reshape(s) for (o, n, s) in places]


_WEIGHTS = ['w_ada', 'b_ada', 'w_in', 'b_in', 'g_ln_v', 'b_ln_v', 'w_spatial', 'b_spatial', 'mu_shift', 'w0', 'w_decay_up', 'a0',
            'w_aaa_up', 'w_gate_up', 'k_k', 'k_a', 'r_k', 'gn_gain', 'gn_bias', 'w_branch_a', 'w_branch_b', 'w_out', 'b_out',
            'ln1_g', 'ln1_b', 'w_ff1', 'b_ff1', 'w_ff2', 'b_ff2', 'ln2_g', 'ln2_b']
_BIG = {'w_ff1': (0, 512), 'w_ff2': (512, 512), 'w_out': (1024, 128), 'w_branch_a': (1152, 64), 'w_branch_b': (1216, 64), 'w_in': (1280, 640)}
_LATER_ROWS = 1280
_CUT_BY_COLS = ('w_ff1', 'w_in', 'w_branch_a', 'w_branch_b')
IN_SHARD = IN_COLS // _N_DEV
_LORA = {'w_decay_up': (0, LORA_W), 'w_aaa_up': (LORA_W, LORA_A), 'w_gate_up': (LORA_W + LORA_A, LORA_G)}
_COMM_DTYPE = jnp.bfloat16


def _pad_rows(a, rows):
    return jnp.pad(a, ((0, rows - a.shape[0]),) + ((0, 0),) * (a.ndim - 1))


def _pack_big(shards):
    blocks = []
    for n, (_, rows) in _BIG.items():
        a = shards[n].T if n in _CUT_BY_COLS else shards[n]
        blocks.append(_pad_rows(a.reshape(-1, D_MODEL), rows))
    return jnp.concatenate(blocks, axis=0)


def _unpack_big(block, like):
    out = {}
    for n, (r0, _) in _BIG.items():
        rr, cc = like[n].shape
        if n in _CUT_BY_COLS:
            out[n] = block[r0:r0 + rr * cc // D_MODEL].reshape(cc, rr).T
        else:
            out[n] = block[r0:r0 + rr]
    return out


def _to_padded(a, axis):
    g_end = 2 * G_WIDTH
    r_end = g_end + RW_USED
    take = lambda lo, hi: lax.slice_in_dim(a, lo, hi, axis=axis)
    zshape = list(a.shape)
    zshape[axis] = RW_COLS - RW_USED
    return jnp.concatenate([take(r_end, IN_COLS), take(g_end, r_end), jnp.zeros(zshape, a.dtype), take(0, g_end)], axis=axis)


def _from_padded(a, axis):
    take = lambda lo, hi: lax.slice_in_dim(a, lo, hi, axis=axis)
    return jnp.concatenate([take(2 * D_MODEL + RW_COLS, P_COLS), take(2 * D_MODEL, 2 * D_MODEL + RW_USED), take(0, 2 * D_MODEL)], axis=axis)


def _step(p, m, v, x, c, target):
    T = x.shape[0]
    xi, yi, ci = _place()
    me = 4 * xi + 2 * yi + ci
    tile = _pick(T, (512, 256))

    lane = jnp.arange(R_WIDTH)
    gsum = (lane[:128, None] // R_HEAD == lane[None, :128] // R_HEAD).astype(F32)
    expand = (jnp.arange(128)[:, None] == (lane[None, :] // (G_WIDTH // 8))).astype(F32)

    (c_act,), _ = _rowwise("silu_c", lambda tv, bv: ([tv[0] * _sigmoid(tv[0])], []), [c], [], [(D_MODEL, F32)], [], 1)
    small, places = _pack_rows([c_act, p['w_decay_up'], p['w_aaa_up'], p['w_gate_up']])
    small_all = _all_gather("gather_small", small)
    per_dev = [_unpack_rows(small_all[d], places) for d in range(_N_DEV)]
    c_act_all = _pad_rows(jnp.concatenate([pd[0] for pd in per_dev], axis=0), 16)
    lora_full = {n: jnp.concatenate([pd[i + 1] for pd in per_dev], axis=1) for i, n in enumerate(_LORA)}
    lora_pad = {n: jnp.zeros((LORA_PAD, R_WIDTH), F32).at[r0:r0 + nr].set(lora_full[n]) for n, (r0, nr) in _LORA.items()}

    big_names = list(_BIG)
    my_rows = _pack_big(p).astype(_MXU_DTYPE)
    b_in_p = _to_padded(p['b_in'], 1)
    mu_p = jnp.concatenate([p['mu_shift'], jnp.zeros((1, RW_COLS - RW_USED), F32)], axis=1)

    b_ada_mine = lax.dynamic_slice(p['b_ada'], (0, me * 768), (1, 768))
    mod_cols = _mm("ada_mod", c_act_all, p['w_ada'], "nn", bias=b_ada_mine)
    mod_all = _all_gather("gather_mod", mod_cols)
    mod = lax.dynamic_index_in_dim(mod_all, me, axis=1, keepdims=False).reshape(1, 6 * D_MODEL)
    sh1, sc1, gt1, sh2, sc2, gt2 = [mod[:, i * D_MODEL:(i + 1) * D_MODEL] for i in range(6)]

    (h,), _, w_in_all = _rowwise("modulate1", lambda tv, bv: (_seg_modulate(tv, bv), []), [x], [sc1, sh1], [(D_MODEL, _MXU_DTYPE)], [], tile,
                                 rider=(my_rows[_LATER_ROWS:], _gather_def, (0.0, 0.5, 1.0)))
    w_in_t = _to_padded(w_in_all[:, :IN_SHARD].reshape(IN_COLS, D_MODEL), 0)
    proj = _mm("in_proj", h, w_in_t, "nt", bias=b_in_p)
    ws = p['w_spatial']
    b_tg = jnp.zeros((G_CHUNK, 128), F32).at[:, :8].set(p['b_spatial'].T)
    gmlp_b = [p['g_ln_v'], p['b_ln_v'], ws, b_tg, expand]
    z_gmlp = (proj, 2 * G_WIDTH, 4)
    (ya,), _ = _rowwise("gmlp", lambda tv, bv: (_seg_gmlp(tv, bv), []), [z_gmlp], gmlp_b, [(G_WIDTH, F32)], [], G_CHUNK)
    z_rw = (proj, RW_COLS, 1)
    z_rw_halo = ("halo", proj, RW_COLS, 1)
    pre_b = [mu_p, p['w0'], lora_pad['w_decay_up'], p['a0'], lora_pad['w_aaa_up'], lora_pad['w_gate_up'], p['k_k'], p['k_a'], gsum]

    def pre_fwd(tv, bv):
        z_t, halo_t = tv
        return _seg_rwkv_pre([z_t, _shift_down(z_t, halo_t, pl.program_id(0) == 0)], bv), []

    pre_out, _ = _rowwise("rwkv_pre", pre_fwd, [z_rw, z_rw_halo], pre_b, [(R_WIDTH, F32)] * 7, [], tile)
    r_, lw_, k2_, v_, a_, b_, g_ = pre_out
    scan_in = (r_, lw_, k2_, v_, a_, b_)
    y_, states, inverses, later_all = _scan_fwd(*scan_in, rider=(my_rows[:_LATER_ROWS], _gather_def, (0.0, 0.875, 1.0)))

    def whole(n, rows):
        r0 = _BIG[n][0]
        return later_all[:, r0:r0 + rows].reshape(_N_DEV * rows, D_MODEL)

    w_ff1_t, w_ff2, w_out = whole('w_ff1', 512), whole('w_ff2', 512), whole('w_out', 128)
    w_ba_t = whole('w_branch_a', 64).reshape(D_MODEL, G_WIDTH)
    w_bb_t = whole('w_branch_b', 64).reshape(D_MODEL, R_WIDTH)
    post_b = [p['r_k'].reshape(1, R_WIDTH), p['gn_gain'], p['gn_bias'], gsum]
    (yb,), _ = _rowwise("rwkv_post", lambda tv, bv: (_seg_rwkv_post(tv, bv), []), [y_, r_, k2_, v_, g_], post_b, [(R_WIDTH, F32)], [], tile)
    pa = _mm("branch_a", ya, w_ba_t, "nt", out_dtype=_MXU_DTYPE)
    pb = _mm("branch_b", yb, w_bb_t, "nt", out_dtype=_MXU_DTYPE)
    gates = [(proj, D_MODEL, 0), (proj, D_MODEL, 1)]
    (merged,), _ = _rowwise("merge", lambda tv, bv: (_seg_merge(tv, bv), []), gates + [pa, pb], [], [(D_MODEL, _MXU_DTYPE)], [], tile)
    mix = _mm("out_proj", merged, w_out, "nn", bias=p['b_out'])
    mid_b = [gt1, p['ln1_g'], p['ln1_b'], sc2, sh2]
    (h1, h2in), _ = _rowwise("mid", lambda tv, bv: (_seg_mid(tv, bv), []), [x, mix], mid_b, [(D_MODEL, F32), (D_MODEL, _MXU_DTYPE)], [], tile)
    act = _mm("ff1", h2in, w_ff1_t, "nt", bias=p['b_ff1'], epi=lambda t: _seg_relu2([t], [])[0], epi_dtype=_MXU_DTYPE, raw=False)
    ff = _mm("ff2", act, w_ff2, "nn", bias=p['b_ff2'])

    def loss_fn(tv, bv):
        h1_t, ff_t, tgt = tv
        val, grads = jax.value_and_grad(lambda a0_, a1_, b0_, b1_, b2_: _seg_loss([a0_, a1_, tgt], [b0_, b1_, b2_]), argnums=(0, 1, 2, 3, 4))(h1_t, ff_t, *bv)
        return [grads[0], grads[1]], [grads[2], grads[3], grads[4], jnp.sum(grads[1], axis=0, keepdims=True), jnp.full((1, 128), val, F32)]

    (dh1_a, dff), (d_gt2, d_ln2_g, d_ln2_b, d_b_ff2, loss_row) = _rowwise(
        "loss", loss_fn, [h1, ff, target], [gt2, p['ln2_g'], p['ln2_b']], [(D_MODEL, F32), (D_MODEL, _MXU_DTYPE)], [(1, D_MODEL)] * 4 + [(1, 128)], tile)

    g = {}
    g['ln2_g'], g['ln2_b'], g['b_ff2'] = d_ln2_g, d_ln2_b, d_b_ff2
    gw = {}
    gw['w_ff2'] = _mm("g_w_ff2", act, dff, "tn", out_dtype=_COMM_DTYPE)
    df1, g['b_ff1'] = _mm("d_act", dff, w_ff2, "nt", beside=act, epi=lambda d_, a_: d_ * (2.0 * jnp.sqrt(a_.astype(F32))),
                          epi_dtype=_MXU_DTYPE, colsum=True)
    gw['w_ff1'] = _mm("g_w_ff1", df1, h2in, "tn", out_dtype=_COMM_DTYPE)
    dh2in = _mm("d_h2in", df1, w_ff1_t, "nn")
    (dx_a, dmix), (d_gt1, g['ln1_g'], g['ln1_b'], d_sc2, d_sh2), (g['b_out'],) = _rowwise_vjp(
        "mid_bwd", _seg_mid, [x, mix], mid_b, [dh1_a, dh2in], tile, [0, 1], [0, 1, 2, 3, 4], t_dtypes=[F32, _MXU_DTYPE], colsum=[1])
    gw['w_out'] = _mm("g_w_out", merged, dmix, "tn", out_dtype=_COMM_DTYPE)
    dmerged = _mm("d_merged", dmix, w_out, "nt", out_dtype=_MXU_DTYPE)
    (dproj, dpa, dpb), _, (cs_gates,) = _rowwise_vjp(
        "merge_bwd", _seg_merge, gates + [pa, pb], [], [dmerged], tile, [0, 1, 2, 3], [], t_dtypes=[_MXU_DTYPE] * 3, colsum=[0],
        finish=lambda dts, sc: [jnp.concatenate(dts[:2], axis=1), dts[2], dts[3]], out_widths=[2 * D_MODEL, D_MODEL, D_MODEL],
        into={0: (lax.empty((T, P_COLS), _MXU_DTYPE), 0)})
    gw['w_branch_a'] = _mm("g_w_branch_a", dpa, ya, "tn", out_dtype=_COMM_DTYPE)
    gw['w_branch_b'] = _mm("g_w_branch_b", dpb, yb, "tn", out_dtype=_COMM_DTYPE)
    dya = _mm("d_ya", dpa, w_ba_t, "nn")
    dyb = _mm("d_yb", dpb, w_bb_t, "nn")
    def send_rows(names):
        parts = []
        for n in names:
            per_dev = gw[n].reshape(_N_DEV, -1, D_MODEL)
            parts.append(jnp.pad(per_dev, ((0, 0), (0, _BIG[n][1] - per_dev.shape[1]), (0, 0))))
        return jnp.concatenate(parts, axis=1) if len(parts) > 1 else parts[0]

    send_early = send_rows(big_names[:-1])
    (dy, dr1, dk1, dv1, dg_), (d_r_k, g['gn_gain'], g['gn_bias']), _, sibling_early = _rowwise_vjp(
        "rwkv_post_bwd", _seg_rwkv_post, [y_, r_, k2_, v_, g_], post_b, [dyb], tile, [0, 1, 2, 3, 4], [0, 1, 2],
        rider=(send_early, _sibling_def, (0.0, 1.0)))
    g['r_k'] = d_r_k
    partials_early = _chip_partials("chip_partials_early", send_early, sibling_early, 128, _COMM_DTYPE)
    dr2, dlw, dk2, dv2, da, db, landed_early = _scan_bwd(*scan_in, states, inverses, dy, rider=(partials_early, _chips_def, (0.0, 1.0)))
    pre_tile = 256
    last_step = T // pre_tile - 1

    def pre_prep(prim):
        z_t, halo_t = prim
        return [z_t, _shift_down(z_t, halo_t, pl.program_id(0) == last_step)]

    def pre_finish(dts, sc):
        dz_direct, dprev = dts
        (row_after,) = sc

        @pl.when(pl.program_id(0) == 0)
        def _():
            row_after[...] = jnp.zeros_like(row_after)

        dz = dz_direct + _shift_up(dprev, row_after[...])
        row_after[...] = dprev[:_HALO]
        return [dz]

    (dproj,), (d_mu, g['w0'], d_wd, g['a0'], d_wa, d_wg, g['k_k'], g['k_a']), (cs_rw,) = _rowwise_vjp(
        "rwkv_pre_bwd", _seg_rwkv_pre, [z_rw, z_rw_halo], pre_b, [[dr1, dr2], dlw, [dk1, dk2], [dv1, dv2], da, db, dg_], pre_tile,
        [0, 1], [0, 1, 2, 3, 4, 5, 6, 7], t_dtypes=[_MXU_DTYPE], colsum=[0], prep=pre_prep, finish=pre_finish,
        out_widths=[RW_COLS], reverse=True, scratch=[(_HALO, RW_COLS)], into={0: (dproj, 1)})
    g['mu_shift'] = d_mu[:, :RW_USED]
    for n, d_ in (('w_decay_up', d_wd), ('w_aaa_up', d_wa), ('w_gate_up', d_wg)):
        r0, nr = _LORA[n]
        g[n] = d_[r0:r0 + nr]
    (dproj,), (g['g_ln_v'], g['b_ln_v'], g['w_spatial'], d_b_tg), (cs_g,) = _rowwise_vjp(
        "gmlp_bwd", _seg_gmlp, [z_gmlp], gmlp_b, [dya], G_CHUNK, [0], [0, 1, 2, 3], t_dtypes=[_MXU_DTYPE], colsum=[0],
        into={0: (dproj, 4)})
    g['b_spatial'] = d_b_tg[:, :8].T
    g['b_in'] = _from_padded(jnp.concatenate([cs_gates, cs_rw, cs_g], axis=1), 1)
    small_names = [n for n in _WEIGHTS if n not in _BIG and n not in ('w_ada', 'b_ada')]
    packed_g, g_places = _pack_rows([g[n] for n in small_names] + [loss_row], row_mult=256)
    gw_in_t, small_all = _mm("g_w_in", dproj, h, "tn", out_dtype=_COMM_DTYPE, rider=(packed_g, _gather_def, (0.0, 0.6, 1.0)))
    gw['w_in'] = _from_padded(gw_in_t, 0)
    send_late = send_rows(big_names[-1:])
    sibling_late = _exchange("pair_exchange_late", send_late, _sibling_def)
    partials_late = _chip_partials("chip_partials_late", send_late, sibling_late, 128, _COMM_DTYPE)
    dh, landed_late = _mm("d_h", dproj, w_in_t, "nn", rider=(partials_late, _chips_def, (0.0, 1.0)))

    def mod1_bwd(tv, bv):
        x_t, dh_t, dxa_t = tv
        (sc,) = bv
        return [dxa_t + dh_t * (1.0 + sc)], [jnp.sum(dh_t * x_t, axis=0, keepdims=True), jnp.sum(dh_t, axis=0, keepdims=True)]

    (grad_x,), (d_sc1, d_sh1) = _rowwise("modulate1_bwd", mod1_bwd, [x, dh, dx_a], [sc1], [(D_MODEL, F32)], [(1, D_MODEL)] * 2, tile)

    dmod = jnp.concatenate([d_sh1, d_sc1, d_gt1, d_sh2, d_sc2, d_gt2], axis=1).reshape(6 * D_MODEL // 128, 128)
    dmod_all = _all_gather("gather_dmod", dmod)
    g['b_ada'] = _sum_leading("sum_dmod", dmod_all).reshape(1, 6 * D_MODEL)
    dmod_mine = lax.dynamic_slice(dmod_all.reshape(_N_DEV, 6 * D_MODEL), (0, me * 768), (_N_DEV, 768))
    g_w_ada = _mm("g_w_ada", c_act_all, _pad_rows(dmod_mine, 16), "tn")

    small_sum = _unpack_rows(_sum_leading("sum_small", small_all), g_places)
    loss = small_sum.pop()[0, 0]
    for n, t in zip(small_names, small_sum, strict=True):
        g[n] = t
    for n in _LORA:
        g[n] = lax.dynamic_slice(g[n], (0, me * R_HEAD), (g[n].shape[0], R_HEAD))
    g['w_ada'] = g_w_ada

    summed = jnp.concatenate([_sum_leading("sum_early", landed_early), _sum_leading("sum_late", landed_late)], axis=0)
    g.update(_unpack_big(summed, p))

    delta, new_m, new_v = {}, {}, {}
    own_call = ['w_ada'] + big_names
    for n in own_call:
        if n == 'w_in':
            r0 = _BIG[n][0]
            outs_t = _adamw("adamw_" + n, p[n].T, summed[r0:r0 + IN_SHARD], m[n].T, v[n].T)
            delta[n], new_m[n], new_v[n] = [o.T for o in outs_t]
        else:
            delta[n], new_m[n], new_v[n] = _adamw("adamw_" + n, p[n], g[n], m[n], v[n])
    rest = [n for n in _WEIGHTS if n not in own_call]
    outs = _adamw_many("adamw_rest", *[[d[n].reshape(p[n].shape) for n in rest] for d in (p, g, m, v)])
    for d, o in zip((delta, new_m, new_v), outs, strict=True):
        d.update(zip(rest, o, strict=True))
    return loss, grad_x, g, delta, new_m, new_v


def kernel(x, c, w_ada, b_ada, w_in, b_in, g_ln_v, b_ln_v, w_spatial, b_spatial, mu_shift, w0, w_decay_up, a0, w_aaa_up, w_gate_up, k_k, k_a, r_k, gn_gain, gn_bias, w_branch_a, w_branch_b, w_out, b_out, ln1_g, ln1_b, w_ff1, b_ff1, w_ff2, b_ff2, ln2_g, ln2_b, loss_target, m_w_ada, m_b_ada, m_w_in, m_b_in, m_g_ln_v, m_b_ln_v, m_w_spatial, m_b_spatial, m_mu_shift, m_w0, m_w_decay_up, m_a0, m_w_aaa_up, m_w_gate_up, m_k_k, m_k_a, m_r_k, m_gn_gain, m_gn_bias, m_w_branch_a, m_w_branch_b, m_w_out, m_b_out, m_ln1_g, m_ln1_b, m_w_ff1, m_b_ff1, m_w_ff2, m_b_ff2, m_ln2_g, m_ln2_b, v_w_ada, v_b_ada, v_w_in, v_b_in, v_g_ln_v, v_b_ln_v, v_w_spatial, v_b_spatial, v_mu_shift, v_w0, v_w_decay_up, v_a0, v_w_aaa_up, v_w_gate_up, v_k_k, v_k_a, v_r_k, v_gn_gain, v_gn_bias, v_w_branch_a, v_w_branch_b, v_w_out, v_b_out, v_ln1_g, v_ln1_b, v_w_ff1, v_b_ff1, v_w_ff2, v_b_ff2, v_ln2_g, v_ln2_b):
    given = dict(locals())
    shapes = {n: given[n].shape for n in _WEIGHTS}
    def two_d(a_):
        a_ = a_[0]
        return a_.reshape(1, -1) if a_.ndim == 1 else a_
    p = {n: two_d(given[n]) for n in _WEIGHTS}
    m = {n: two_d(given["m_" + n]) for n in _WEIGHTS}
    v = {n: two_d(given["v_" + n]) for n in _WEIGHTS}
    loss, grad_x, g, delta, new_m, new_v = _step(p, m, v, x[0], c, loss_target[0])
    outs = [loss, grad_x[None]]
    for d in (g, delta, new_m, new_v):
        outs += [d[n].reshape(shapes[n]) for n in _WEIGHTS]
    return tuple(outs)
```

```python
import functools

import jax
import jax.numpy as jnp
from jax import lax
from jax.experimental import pallas as pl
from jax.experimental.pallas import tpu as pltpu

F32 = jnp.float32
_MXU_DTYPE = jnp.bfloat16
_HI = lax.Precision.HIGHEST
_VMEM_LIMIT = 48 * 1024 * 1024
_MESH_ID = pl.DeviceIdType.MESH
_N_DEV = 8

D_MODEL = 1024
G_WIDTH = 512
G_CHUNK = 128
R_WIDTH = 512
R_HEADS = 8
R_HEAD = 64
LORA_W, LORA_A, LORA_G = 32, 32, 96
D_FF = 4096
ALPHA = 2.0 ** 0.25
LN_EPS = 1e-5
GN_EPS = 64e-5
SCAN_CHUNK = 64
SCAN_PER_STEP = 4
ADAM_LR, ADAM_B1, ADAM_B2, ADAM_EPS, ADAM_WD, ADAM_STEP = 0.001, 0.9, 0.999, 1e-08, 0.01, 10

P_COLS = 5120
RW_COLS = 2048
RW_USED = 3 * R_WIDTH + LORA_W + LORA_A + LORA_G
LORA_PAD = 256
IN_COLS = 2 * G_WIDTH + RW_USED + 2 * D_MODEL


def _cparams(sem=None, **kw):
    if sem is not None:
        kw["dimension_semantics"] = sem
    return pltpu.CompilerParams(vmem_limit_bytes=_VMEM_LIMIT, **kw)


def _dot(a, b, dims=(((1,), (0,)), ((), ())), hi=False):
    if hi:
        return lax.dot_general(a.astype(F32), b.astype(F32), dims, precision=_HI, preferred_element_type=F32)
    return lax.dot_general(a.astype(_MXU_DTYPE), b.astype(_MXU_DTYPE), dims, preferred_element_type=F32)


_NN = (((1,), (0,)), ((), ()))
_NT = (((1,), (1,)), ((), ()))
_TN = (((0,), (0,)), ((), ()))


def _pick(n, pref):
    for t in pref:
        if n % t == 0:
            return t
    return n


def _mm(name, a, b, mode, bias=None, out_dtype=F32, epi=None, epi_dtype=None, raw=True, beside=None, colsum=False, rider=None,
        tm=None, tn=None, tk=None):
    if mode == "nn":
        (M, K), (_, N) = a.shape, b.shape
    elif mode == "nt":
        (M, K), (N, _) = a.shape, b.shape
    else:
        (K, M), (_, N) = a.shape, b.shape
    tm = tm or _pick(M, (2048, 1280, 1024, 512, 256, 128, 64, 32, 16, 8))
    tn = tn or _pick(N, (1024, 512, 640, 384, 256, 128))
    tk = tk or _pick(K, (1024, 512, 256, 128))
    nk = K // tk
    dims = {"nn": _NN, "nt": _NT, "tn": _TN}[mode]
    a_spec = pl.BlockSpec((tk, tm), lambda i, j, k: (k, i)) if mode == "tn" else pl.BlockSpec((tm, tk), lambda i, j, k: (i, k))
    b_spec = pl.BlockSpec((tn, tk), lambda i, j, k: (j, k)) if mode == "nt" else pl.BlockSpec((tk, tn), lambda i, j, k: (k, j))
    o_spec = pl.BlockSpec((tm, tn), lambda i, j, k: (i, j))
    has_bias, has_beside = bias is not None, beside is not None
    two = epi is not None and not has_beside and raw
    only_epi = epi is not None and not has_beside and not raw
    grid = (M // tm, N // tn, nk)
    ride_shape, ride_sems, _ = rider[1](rider[0]) if rider else (None, [], None)

    def body(*refs):
        refs = list(refs)
        n_in = 2 + has_bias + has_beside
        if rider:
            sem_refs = [refs.pop() for _ in ride_sems][::-1]
            ride_out = refs.pop(n_in + 1 + 1 + two)
            ride_in = refs.pop(n_in)
            step = (pl.program_id(0) * grid[1] + pl.program_id(1)) * grid[2] + pl.program_id(2)
            _ride(rider, step, grid[0] * grid[1] * grid[2], ride_in, ride_out, sem_refs)
        a_ref, b_ref = refs[0], refs[1]
        bias_ref = refs[2] if has_bias else None
        beside_ref = refs[n_in - 1] if has_beside else None
        outs = refs[n_in:]
        o_ref, acc_ref = outs[0], outs[-1]
        k = pl.program_id(2)

        @pl.when(k == 0)
        def _():
            acc_ref[...] = jnp.zeros_like(acc_ref)

        acc_ref[...] += _dot(a_ref[...], b_ref[...], dims)

        @pl.when(k == nk - 1)
        def _():
            res = acc_ref[...]
            if has_bias:
                res = res + bias_ref[...]
            if has_beside:
                val = epi(res, beside_ref[...])
                o_ref[...] = val.astype(o_ref.dtype)
                if colsum:
                    outs[1][...] = jnp.broadcast_to(jnp.sum(val, axis=0, keepdims=True), outs[1].shape)
            elif only_epi:
                o_ref[...] = epi(res).astype(o_ref.dtype)
            else:
                o_ref[...] = res.astype(o_ref.dtype)
                if two:
                    outs[1][...] = epi(res).astype(outs[1].dtype)

    in_specs = [a_spec, b_spec]
    args = [a, b]
    if has_bias:
        in_specs.append(pl.BlockSpec((1, tn), lambda i, j, k: (0, j)))
        args.append(bias)
    if has_beside:
        in_specs.append(o_spec)
        args.append(beside)
    out_shape = [jax.ShapeDtypeStruct((M, N), epi_dtype if (has_beside or only_epi) else out_dtype)]
    out_specs = [o_spec]
    if two:
        out_shape.append(jax.ShapeDtypeStruct((M, N), epi_dtype))
        out_specs.append(o_spec)
    if colsum:
        assert has_beside and not rider
        out_shape.append(jax.ShapeDtypeStruct((grid[0] * 8, N), F32))
        out_specs.append(pl.BlockSpec((8, tn), lambda i, j, k: (i, j)))
    if rider:
        in_specs.append(_HBM)
        args.append(rider[0])
        out_shape.append(ride_shape)
        out_specs.append(_HBM)
    res = pl.pallas_call(
        body, name=name, grid=grid, in_specs=in_specs, out_specs=out_specs, out_shape=out_shape,
        scratch_shapes=[pltpu.VMEM((tm, tn), F32)] + list(ride_sems),
        compiler_params=_cparams(("arbitrary",) * 3 if rider else ("parallel", "parallel", "arbitrary")),
    )(*args)
    if colsum:
        return res[0], _sum_leading(name + "_colsum", res[1].reshape(grid[0], 8, N))[0:1]
    return res if (two or rider) else res[0]


_HALO = 8


def _rowwise(name, fn, tiled, bcast, tiled_out, red_out, tile, reverse=False, scratch=(), rider=None):
    tiled = [t if isinstance(t, tuple) else (t, t.shape[1], 0) for t in tiled]
    T = next(t[0] for t in tiled if not isinstance(t[0], str)).shape[0]
    n = T // tile
    nt, nb, nto, nsc, nro = len(tiled), len(bcast), len(tiled_out), len(scratch), len(red_out)
    ride_shape, ride_sems, _ = rider[1](rider[0]) if rider else (None, [], None)
    nr = 1 if rider else 0
    into = [(j, t[2], t[3]) for j, t in enumerate(tiled_out) if len(t) == 4]
    na = len(into)

    def row_block(i):
        return n - 1 - i if reverse else i

    def body(*refs):
        i2 = nt + nb
        o0 = i2 + nr + na
        o1, o2 = o0 + nto, o0 + nto + nro
        s0 = o2 + nr
        if rider:
            _ride(rider, pl.program_id(0), n, refs[i2], refs[o2], refs[s0 + nsc:])
        t_refs, b_refs = refs[:nt], refs[nt:i2]
        to_refs, ro_refs = refs[o0:o1], refs[o1:o2]
        extra = (list(refs[s0:s0 + nsc]),) if nsc else ()
        touts, routs = fn([r[...] for r in t_refs], [r[...] for r in b_refs], *extra)
        for r, v in zip(to_refs, touts, strict=True):
            r[...] = v.astype(r.dtype)
        if ro_refs:
            i = pl.program_id(0)

            @pl.when(i == 0)
            def _():
                for r, v in zip(ro_refs, routs, strict=True):
                    r[...] = v.astype(F32)

            @pl.when(i > 0)
            def _():
                for r, v in zip(ro_refs, routs, strict=True):
                    r[...] += v.astype(F32)

    def whole(shape):
        nd = len(shape)
        return pl.BlockSpec(tuple(shape), lambda i: (0,) * nd)

    per_tile = tile // _HALO
    in_specs, arrays = [], []
    for t in tiled:
        if isinstance(t[0], str):
            _, arr, w, cb = t
            in_specs.append(pl.BlockSpec((_HALO, w), functools.partial(lambda i, cb: (jnp.maximum(row_block(i) * per_tile - 1, 0), cb), cb=cb)))
        else:
            arr, w, cb = t
            in_specs.append(pl.BlockSpec((tile, w), functools.partial(lambda i, cb: (row_block(i), cb), cb=cb)))
        arrays.append(arr)
    in_specs += [whole(b.shape) for b in bcast]
    out_specs, out_shape = [], []
    for t in tiled_out:
        cb = t[3] if len(t) == 4 else 0
        out_specs.append(pl.BlockSpec((tile, t[0]), functools.partial(lambda i, cb: (row_block(i), cb), cb=cb)))
        out_shape.append(jax.ShapeDtypeStruct(t[2].shape if len(t) == 4 else (T, t[0]), t[1]))
    out_specs += [whole(s) for s in red_out]
    out_shape += [jax.ShapeDtypeStruct(tuple(s), F32) for s in red_out]
    ride_args = []
    if rider:
        in_specs.append(_HBM)
        out_specs.append(_HBM)
        out_shape.append(ride_shape)
        ride_args = [rider[0]]
    in_specs += [pl.BlockSpec(memory_space=pl.ANY)] * na
    aliases = {nt + nb + nr + k: j for k, (j, _, _) in enumerate(into)}
    res = pl.pallas_call(
        body, name=name, grid=(n,), in_specs=in_specs, out_specs=out_specs, out_shape=out_shape,
        scratch_shapes=[pltpu.VMEM(tuple(s), F32) for s in scratch] + list(ride_sems),
        input_output_aliases=aliases, compiler_params=_cparams(("arbitrary",)),
    )(*arrays, *bcast, *ride_args, *[buf for (_, buf, _) in into])
    if rider:
        return list(res[:nto]), list(res[nto:nto + nro]), res[-1]
    return list(res[:nto]), list(res[nto:])


def _rowwise_vjp(name, f, tiled, bcast, cts, tile, wrt_t, wrt_b, t_dtypes=None, colsum=(), prep=None, finish=None,
                 out_widths=None, reverse=False, scratch=(), rider=None, into=None):
    tiled = [t if isinstance(t, tuple) else (t, t.shape[1], 0) for t in tiled]
    npr = len(tiled)
    t_dtypes = t_dtypes or [F32] * len(wrt_t)
    groups = [c if isinstance(c, list) else [c] for c in cts]
    cts = [a_ for grp in groups for a_ in grp]

    def fn(tv, bv, sc=None):
        prim, flat_ct = tv[:npr], list(tv[npr:])
        if prep is not None:
            prim = prep(prim)
        ct = []
        for grp in groups:
            parts = [flat_ct.pop(0).astype(F32) for _ in grp]
            ct.append(functools.reduce(lambda p_, q_: p_ + q_, parts))

        def g(dt_vals, db_vals):
            full_t, full_b = list(prim), list(bv)
            for i, v in zip(wrt_t, dt_vals, strict=True):
                full_t[i] = v
            for j, v in zip(wrt_b, db_vals, strict=True):
                full_b[j] = v
            return f(full_t, full_b)

        outs, pull = jax.vjp(g, [prim[i].astype(F32) for i in wrt_t], [bv[j] for j in wrt_b])
        dts, dbs = pull([c.astype(o.dtype) for c, o in zip(ct, outs, strict=True)])
        if finish is not None:
            dts = finish(dts, sc)
        sums = [jnp.sum(dts[i].astype(F32), axis=0, keepdims=True) for i in colsum]
        return dts, list(dbs) + sums

    widths = out_widths or [tiled[i][1] for i in wrt_t]
    tiled_out = [(w, dt) + tuple((into or {}).get(j, ())) for j, (w, dt) in enumerate(zip(widths, t_dtypes, strict=True))]
    red_out = [bcast[j].shape for j in wrt_b] + [(1, widths[i]) for i in colsum]
    res = _rowwise(name, fn, tiled + list(cts), bcast, tiled_out, red_out, tile, reverse=reverse, scratch=scratch, rider=rider)
    dts, reds = res[0], res[1]
    nb = len(wrt_b)
    return (dts, reds[:nb], reds[nb:]) + tuple(res[2:])


def _layer_norm(x, g, b, eps):
    mu = jnp.mean(x, axis=-1, keepdims=True)
    xc = x - mu
    var = jnp.mean(xc * xc, axis=-1, keepdims=True)
    return xc * lax.rsqrt(var + eps) * g + b


def _gelu_tanh(x):
    return 0.5 * x * (1.0 + jnp.tanh(0.7978845608028654 * (x + 0.044715 * (x * x * x))))


def _sigmoid(x):
    return 1.0 / (1.0 + jnp.exp(-x))


def _seg_modulate(tv, bv):
    (x,), (sc, sh) = tv, bv
    return [x * (1.0 + sc) + sh]


def _seg_gmlp(tv, bv):
    (z,), (g_ln, b_ln, ws, b_tg, expand) = tv, bv
    bias_full = _dot(b_tg, expand, hi=True)
    zz = _gelu_tanh(z)
    u, v = zz[:, :G_WIDTH], zz[:, G_WIDTH:]
    v = _layer_norm(v, g_ln, b_ln, LN_EPS)
    row = lax.broadcasted_iota(jnp.int32, (G_CHUNK, G_CHUNK), 0)
    col = lax.broadcasted_iota(jnp.int32, (G_CHUNK, G_CHUNK), 1)
    causal = col <= row
    first_group = lax.broadcasted_iota(jnp.int32, (G_CHUNK, 128), 1) < 64
    parts = []
    for p in range(4):
        vp = v[:, 128 * p:128 * (p + 1)]
        s_even = _dot(jnp.where(causal, ws[2 * p], 0.0), vp)
        s_odd = _dot(jnp.where(causal, ws[2 * p + 1], 0.0), vp)
        parts.append(jnp.where(first_group, s_even, s_odd))
    s = jnp.concatenate(parts, axis=1) + bias_full
    return [u * s]


def _split2(x):
    hi = x.astype(_MXU_DTYPE)
    return hi, (x - hi.astype(F32)).astype(_MXU_DTYPE)


@jax.custom_vjp
def _group_sum(x, ones_blocks):
    hi, lo = _split2(x)
    w = ones_blocks.shape[0]
    return jnp.concatenate([_dot(hi[:, j:j + w], ones_blocks) + _dot(lo[:, j:j + w], ones_blocks) for j in range(0, x.shape[1], w)], axis=1)


def _group_sum_fwd(x, ones_blocks):
    return _group_sum(x, ones_blocks), ones_blocks


def _group_sum_bwd(ones_blocks, ct):
    return _group_sum(ct, ones_blocks), jnp.zeros_like(ones_blocks)


_group_sum.defvjp(_group_sum_fwd, _group_sum_bwd)


def _shift_down(z, halo, is_first):
    _, W = z.shape
    rolled = pltpu.roll(z, 1, 0)
    before = jnp.where(is_first, 0.0, pltpu.roll(halo, 1, 0))
    top_row = lax.broadcasted_iota(jnp.int32, (_HALO, W), 0) == 0
    return jnp.concatenate([jnp.where(top_row, before, rolled[:_HALO]), rolled[_HALO:]], axis=0)


def _shift_up(d, after):
    tile, W = d.shape
    rolled = pltpu.roll(d, tile - 1, 0)
    last_row = lax.broadcasted_iota(jnp.int32, (_HALO, W), 0) == _HALO - 1
    bottom = jnp.where(last_row, pltpu.roll(after, _HALO - 1, 0), rolled[tile - _HALO:])
    return jnp.concatenate([rolled[:tile - _HALO], bottom], axis=0)


def _seg_rwkv_pre(tv, bv):
    (z, prev), (mu, w0, wd, a0, wa, wg, k_k, k_a, gsum) = tv, bv
    zs = z + (prev - z) * mu
    r, k, v = zs[:, 0:512], zs[:, 512:1024], zs[:, 1024:1536]
    zl = zs[:, 3 * R_WIDTH:3 * R_WIDTH + LORA_PAD]
    x = w0 + _dot(jnp.tanh(zl), wd)
    softplus = jnp.maximum(-x, 0.0) + jnp.log(1.0 + jnp.exp(-jnp.abs(x)))
    lw = -jnp.exp(-softplus - 0.5)
    a = _sigmoid(a0 + _dot(zl, wa))
    g = _dot(_sigmoid(zl), wg)
    kk = k * k_k
    nrm = jnp.sqrt(_group_sum(kk * kk, gsum))
    kk = kk / jnp.maximum(nrm, 1e-12)
    k2 = k * (1.0 + (a - 1.0) * k_a)
    return [r, lw, k2, v, -kk, kk * a, g]


def _seg_rwkv_post(tv, bv):
    (y, r, k2, v, g), (r_k, gain, bias, gsum) = tv, bv
    mu = _group_sum(y, gsum) * (1.0 / R_HEAD)
    yc = y - mu
    var = _group_sum(yc * yc, gsum) * (1.0 / R_HEAD)
    yn = yc * lax.rsqrt(var + GN_EPS) * gain + bias
    bonus = _group_sum(r * k2 * r_k, gsum) * v
    return [(yn + bonus) * g]


def _seg_merge(tv, bv):
    (ga, gb, pa, pb), () = tv, bv
    return [_sigmoid(ga) * pa + _sigmoid(gb) * pb]


def _seg_mid(tv, bv):
    (x, mix), (gt1, g1, b1, sc2, sh2) = tv, bv
    h1 = _layer_norm(ALPHA * x + gt1 * mix, g1, b1, LN_EPS)
    return [h1, h1 * (1.0 + sc2) + sh2]


def _seg_relu2(tv, bv):
    (f1,), () = tv, bv
    return [jnp.square(jnp.maximum(f1, 0.0))]


def _seg_loss(tv, bv):
    (h1, ff, target), (gt2, g2, b2) = tv, bv
    out = _layer_norm(ALPHA * h1 + gt2 * ff, g2, b2, LN_EPS)
    err = jnp.square(out - target)
    return 0.5 * jnp.sum(jnp.mean(err, axis=-1))


_BNN = (((2,), (1,)), ((0,), (0,)))
_BNT = (((2,), (2,)), ((0,), (0,)))
_BTN = (((1,), (1,)), ((0,), (0,)))


def _tri_dot(x, dims):
    L = x.shape[0]
    tri = (lax.broadcasted_iota(jnp.int32, (L, L), 1) <= lax.broadcasted_iota(jnp.int32, (L, L), 0)).astype(F32)
    hi, lo = _split2(x)
    return _dot(tri, hi, dims) + _dot(tri, lo, dims)


@jax.custom_vjp
def _running_sum(x):
    return _tri_dot(x, _NN)


_running_sum.defvjp(lambda x: (_tri_dot(x, _NN), None), lambda _, ct: (_tri_dot(ct, _TN),))


def _cut_heads(x):
    return jnp.stack([x[:, R_HEAD * h:R_HEAD * (h + 1)] for h in range(R_HEADS)])


def _join_heads(x):
    return jnp.concatenate([x[h] for h in range(R_HEADS)], axis=1)


@jax.custom_vjp
def _split_heads(x):
    return _cut_heads(x)


@jax.custom_vjp
def _merge_heads(x):
    return _join_heads(x)


_split_heads.defvjp(lambda x: (_cut_heads(x), None), lambda _, ct: (_join_heads(ct),))
_merge_heads.defvjp(lambda x: (_join_heads(x), None), lambda _, ct: (_cut_heads(ct),))


def _inverse_pullback(inv, ct):
    return _dot(_dot(inv, ct, _BTN), inv, _BNT)


@jax.custom_vjp
def _unit_lower_inverse(n_mat):
    H, L, _ = n_mat.shape
    eye = lax.broadcasted_iota(jnp.int32, (H, L, L), 1) == lax.broadcasted_iota(jnp.int32, (H, L, L), 2)
    inv = jnp.where(eye, 1.0, 0.0) + n_mat
    pw = n_mat
    n = 2
    while n < L:
        pw = _dot(pw, pw, _BNN)
        inv = inv + _dot(inv, pw, _BNN)
        n *= 2
    return inv


def _unit_lower_inverse_fwd(n_mat):
    inv = _unit_lower_inverse(n_mat)
    return inv, inv


_unit_lower_inverse.defvjp(_unit_lower_inverse_fwd, lambda inv, ct: (_inverse_pullback(inv, ct),))


@jax.custom_vjp
def _known_inverse(n_mat, inv):
    return inv


_known_inverse.defvjp(lambda n_mat, inv: (inv, inv), lambda inv, ct: (_inverse_pullback(inv, ct), jnp.zeros_like(inv)))


def _scan_chunk(r, lw, k, v, a, b, s0, inv=None, with_inverse=False):
    L, H = r.shape[0], R_HEADS
    cs = _running_sum(lw)
    cs_end = cs[L - 1:L, :]
    p, p_inv, to_end = jnp.exp(cs), jnp.exp(-cs), jnp.exp(cs_end - cs)
    at, bt, kt, rt = [_split_heads(t) for t in (a * jnp.exp(cs - lw), b * p_inv, k * p_inv, r * p)]
    b_end, k_end, v = [_split_heads(t) for t in (b * to_end, k * to_end, v)]
    row = lax.broadcasted_iota(jnp.int32, (H, L, L), 1)
    col = lax.broadcasted_iota(jnp.int32, (H, L, L), 2)
    incl, strict = col <= row, col < row
    a_ab = jnp.where(strict, _dot(at, bt, _BNT), 0.0)
    a_ak = jnp.where(strict, _dot(at, kt, _BNT), 0.0)
    a_rb = jnp.where(incl, _dot(rt, bt, _BNT), 0.0)
    a_rk = jnp.where(incl, _dot(rt, kt, _BNT), 0.0)
    inv = _unit_lower_inverse(a_ab) if inv is None else _known_inverse(a_ab, inv)
    u = _dot(inv, _dot(at, s0, _BNT) + _dot(a_ak, v, _BNN), _BNN)
    y = _merge_heads(_dot(rt, s0, _BNT) + _dot(a_rb, u, _BNN) + _dot(a_rk, v, _BNN))
    s1 = s0 * _split_heads(jnp.exp(cs_end)) + _dot(u, b_end, _BTN) + _dot(v, k_end, _BTN)
    return (y, s1, inv) if with_inverse else (y, s1)


def _scan_fwd(r, lw, k, v, a, b, rider):
    T = r.shape[0]
    H, N, L, P = R_HEADS, R_HEAD, SCAN_CHUNK, SCAN_PER_STEP
    nc = T // L
    steps = nc // P
    ride_shape, ride_sems, _ = rider[1](rider[0])

    def body(r_ref, lw_ref, k_ref, v_ref, a_ref, b_ref, ride_in, y_ref, st_ref, inv_ref, ride_out, s_ref, *sem_refs):
        _ride(rider, pl.program_id(0), steps, ride_in, ride_out, sem_refs)

        @pl.when(pl.program_id(0) == 0)
        def _():
            s_ref[...] = jnp.zeros_like(s_ref)

        s0 = s_ref[...]
        for j in range(P):
            rows = pl.ds(j * L, L)
            st_ref[j] = s0
            y, s0, inv = _scan_chunk(*[t[rows, :] for t in (r_ref, lw_ref, k_ref, v_ref, a_ref, b_ref)], s0, with_inverse=True)
            y_ref[rows, :] = y
            inv_ref[j] = inv
        s_ref[...] = s0

    blk = pl.BlockSpec((P * L, R_WIDTH), lambda c: (c, 0))
    per_chunk = pl.BlockSpec((P, H, N, N), lambda c: (c, 0, 0, 0))
    return pl.pallas_call(
        body, name="scan_fwd", grid=(steps,), in_specs=[blk] * 6 + [_HBM], out_specs=[blk, per_chunk, per_chunk, _HBM],
        out_shape=[jax.ShapeDtypeStruct((T, R_WIDTH), F32)] + [jax.ShapeDtypeStruct((nc, H, N, N), F32)] * 2 + [ride_shape],
        scratch_shapes=[pltpu.VMEM((H, N, N), F32)] + list(ride_sems),
        compiler_params=_cparams(("arbitrary",)),
    )(r, lw, k, v, a, b, rider[0])


def _scan_bwd(r, lw, k, v, a, b, states, inverses, dy, rider):
    T = r.shape[0]
    H, N, L, P = R_HEADS, R_HEAD, SCAN_CHUNK, SCAN_PER_STEP
    nc = T // L
    steps = nc // P
    ride_shape, ride_sems, _ = rider[1](rider[0])

    def body(r_ref, lw_ref, k_ref, v_ref, a_ref, b_ref, st_ref, inv_ref, dy_ref, ride_in,
             dr_ref, dlw_ref, dk_ref, dv_ref, da_ref, db_ref, ride_out, ds_ref, *sem_refs):
        _ride(rider, pl.program_id(0), steps, ride_in, ride_out, sem_refs)

        @pl.when(pl.program_id(0) == 0)
        def _():
            ds_ref[...] = jnp.zeros_like(ds_ref)

        ds = ds_ref[...]
        for j in reversed(range(P)):
            rows = pl.ds(j * L, L)
            args = [t[rows, :] for t in (r_ref, lw_ref, k_ref, v_ref, a_ref, b_ref)] + [st_ref[j]]
            inv = inv_ref[j]
            _, pull = jax.vjp(lambda *xs, inv=inv: _scan_chunk(*xs, inv=inv), *args)
            grads = pull((dy_ref[rows, :], ds))
            for o_ref, g_ in zip((dr_ref, dlw_ref, dk_ref, dv_ref, da_ref, db_ref), grads[:6], strict=True):
                o_ref[rows, :] = g_
            ds = grads[6]
        ds_ref[...] = ds

    blk = pl.BlockSpec((P * L, R_WIDTH), lambda c: (steps - 1 - c, 0))
    per_chunk = pl.BlockSpec((P, H, N, N), lambda c: (steps - 1 - c, 0, 0, 0))
    return pl.pallas_call(
        body, name="scan_bwd", grid=(steps,), in_specs=[blk] * 6 + [per_chunk, per_chunk, blk, _HBM], out_specs=[blk] * 6 + [_HBM],
        out_shape=[jax.ShapeDtypeStruct((T, R_WIDTH), F32)] * 6 + [ride_shape],
        scratch_shapes=[pltpu.VMEM((H, N, N), F32)] + list(ride_sems),
        compiler_params=_cparams(("arbitrary",)),
    )(r, lw, k, v, a, b, states, inverses, dy, rider[0])


def _place():
    x, y, c = lax.axis_index("x"), lax.axis_index("y"), lax.axis_index("c")
    return x, y, c


def _gather_def(block):
    R, C = block.shape

    def phases(x_ref, out_ref, send_sems, recv_sems, local_sem):
        x, y, c = _place()
        me, sibling = (x, y, c), (x, y, 1 - c)
        chips = [(1 - x, y), (x, 1 - y), (1 - x, 1 - y)]

        def slot(px, py, pc):
            return out_ref.at[4 * px + 2 * py + pc]

        def copy(k, blk, to, src=None):
            return pltpu.make_async_remote_copy(
                src_ref=slot(*blk) if src is None else src, dst_ref=slot(*blk),
                send_sem=send_sems.at[k], recv_sem=recv_sems.at[k], device_id=to, device_id_type=_MESH_ID)

        mine = pltpu.make_async_copy(x_ref, slot(*me), local_sem)
        first = [copy(0, me, sibling, src=x_ref)]
        first += [copy(1 + j, me, (*chip, c), src=x_ref) for j, chip in enumerate(chips)]
        passed = [copy(4 + j, (*chip, c), sibling) for j, chip in enumerate(chips)]

        def begin():
            mine.start()
            for cp in first:
                cp.start()

        def forward():
            for j, chip in enumerate(chips):
                copy(1 + j, (*chip, c), me).wait_recv()
                passed[j].start()

        def finish():
            copy(0, sibling, me).wait_recv()
            for j, chip in enumerate(chips):
                copy(4 + j, (*chip, 1 - c), me).wait_recv()
            for cp in first + passed:
                cp.wait_send()
            mine.wait()

        return [begin, forward, finish]

    sems = [pltpu.SemaphoreType.DMA((7,)), pltpu.SemaphoreType.DMA((7,)), pltpu.SemaphoreType.DMA]
    return jax.ShapeDtypeStruct((_N_DEV, R, C), block.dtype), sems, phases


def _sibling_def(blocks):
    _, R, C = blocks.shape

    def phases(x_ref, out_ref, send_sems, recv_sems):
        x, y, c = _place()
        copies = [pltpu.make_async_remote_copy(
            src_ref=x_ref.at[2 * q + (1 - c)], dst_ref=out_ref.at[q], send_sem=send_sems.at[q], recv_sem=recv_sems.at[q],
            device_id=(x, y, 1 - c), device_id_type=_MESH_ID) for q in range(4)]

        def begin():
            for cp in copies:
                cp.start()

        def finish():
            for cp in copies:
                cp.wait()

        return [begin, finish]

    return jax.ShapeDtypeStruct((4, R, C), blocks.dtype), [pltpu.SemaphoreType.DMA((4,)), pltpu.SemaphoreType.DMA((4,))], phases


def _chips_def(partials):
    def phases(x_ref, out_ref, send_sems, recv_sems, local_sem):
        x, y, c = _place()
        my_chip = 2 * x + y
        mine = pltpu.make_async_copy(x_ref.at[my_chip], out_ref.at[my_chip], local_sem)
        copies = []
        for rel in range(1, 4):
            px, py = (1 - x if (rel >> 1) & 1 else x), (1 - y if rel & 1 else y)
            copies.append(pltpu.make_async_remote_copy(
                src_ref=x_ref.at[2 * px + py], dst_ref=out_ref.at[my_chip],
                send_sem=send_sems.at[rel - 1], recv_sem=recv_sems.at[rel - 1],
                device_id=(px, py, c), device_id_type=_MESH_ID))

        def begin():
            mine.start()
            for cp in copies:
                cp.start()

        def finish():
            for cp in copies:
                cp.wait()
            mine.wait()

        return [begin, finish]

    sems = [pltpu.SemaphoreType.DMA((3,)), pltpu.SemaphoreType.DMA((3,)), pltpu.SemaphoreType.DMA]
    return jax.ShapeDtypeStruct(partials.shape, partials.dtype), sems, phases


_HBM = pl.BlockSpec(memory_space=pltpu.HBM)


def _exchange(name, array, definition):
    out_shape, sems, phases = definition(array)

    def body(x_ref, out_ref, *sem_refs):
        for phase in phases(x_ref, out_ref, *sem_refs):
            phase()

    return pl.pallas_call(body, name=name, in_specs=[_HBM], out_specs=_HBM, out_shape=out_shape, scratch_shapes=sems)(array)


def _ride(rider, step, nsteps, x_ref, out_ref, sem_refs):
    array, definition, fractions = rider
    for phase, frac in zip(definition(array)[2](x_ref, out_ref, *sem_refs), fractions, strict=True):
        pl.when(step == min(int(frac * nsteps), nsteps - 1))(phase)


def _all_gather(name, block):
    return _exchange(name, block, _gather_def)


def _chip_partials(name, blocks, from_sibling, tile, out_dtype):
    _, R, C = blocks.shape

    def body(x_ref, s_ref, o_ref):
        c = lax.axis_index("c")
        for q in range(4):
            o_ref[q] = (x_ref[2 * q + c].astype(F32) + s_ref[q].astype(F32)).astype(o_ref.dtype)

    return pl.pallas_call(
        body, name=name, grid=(R // tile,),
        in_specs=[pl.BlockSpec((_N_DEV, tile, C), lambda i: (0, i, 0)), pl.BlockSpec((4, tile, C), lambda i: (0, i, 0))],
        out_specs=pl.BlockSpec((4, tile, C), lambda i: (0, i, 0)), out_shape=jax.ShapeDtypeStruct((4, R, C), out_dtype),
        compiler_params=_cparams(("parallel",)),
    )(blocks, from_sibling)


def _sum_leading(name, x, tile=None):
    n, R, C = x.shape
    tile = tile or _pick(R, (512, 256, 128, 64, 32, 16, 8))

    def body(x_ref, o_ref):
        acc = x_ref[0].astype(F32)
        for k in range(1, n):
            acc = acc + x_ref[k].astype(F32)
        o_ref[...] = acc

    return pl.pallas_call(
        body, name=name, grid=(R // tile,), in_specs=[pl.BlockSpec((n, tile, C), lambda i: (0, i, 0))],
        out_specs=pl.BlockSpec((tile, C), lambda i: (i, 0)), out_shape=jax.ShapeDtypeStruct((R, C), F32),
        compiler_params=_cparams(("parallel",)),
    )(x)


def _adamw_update(w_, g_, m_, v_):
    m2 = ADAM_B1 * m_ + (1.0 - ADAM_B1) * g_
    v2 = ADAM_B2 * v_ + (1.0 - ADAM_B2) * jnp.square(g_)
    m_hat = m2 / (1.0 - ADAM_B1 ** ADAM_STEP)
    v_hat = v2 / (1.0 - ADAM_B2 ** ADAM_STEP)
    delta = -ADAM_LR * (m_hat / (jnp.sqrt(v_hat) + ADAM_EPS) + ADAM_WD * w_)
    return delta, m2, v2


def _adamw(name, w, g, m, v):
    R, C = w.shape
    tile = _pick(R, (256, 128, 64, 32, 16, 8))
    outs, _ = _rowwise(name, lambda tv, bv: (list(_adamw_update(*tv)), []), [w, g, m, v], [], [(C, F32)] * 3, [], tile)
    return outs


def _adamw_many(name, ws, gs, ms, vs):
    n = len(ws)

    def body(*refs):
        ins, outs = refs[:4 * n], refs[4 * n:]
        for i in range(n):
            res = _adamw_update(ins[i][...], ins[n + i][...], ins[2 * n + i][...], ins[3 * n + i][...])
            for j in range(3):
                outs[j * n + i][...] = res[j]

    out_shape = [jax.ShapeDtypeStruct(w.shape, F32) for w in ws] * 3
    res = pl.pallas_call(body, name=name, out_shape=out_shape, compiler_params=_cparams())(*ws, *gs, *ms, *vs)
    return res[:n], res[n:2 * n], res[2 * n:]


def _pack_rows(arrs, lanes=128, row_mult=8):
    flat, places, off = [], [], 0
    for a_ in arrs:
        n = a_.size
        flat.append(a_.reshape(-1).astype(F32))
        places.append((off, n, a_.shape))
        off += n
    total = -(-off // (lanes * row_mult)) * (lanes * row_mult)
    if total > off:
        flat.append(jnp.zeros((total - off,), F32))
    return jnp.concatenate(flat).reshape(total // lanes, lanes), places


def _unpack_rows(packed, places):
    flat = packed.reshape(-1)
    return [flat[o:o + n].reshape(s) for (o, n, s) in places]


_WEIGHTS = ['w_ada', 'b_ada', 'w_in', 'b_in', 'g_ln_v', 'b_ln_v', 'w_spatial', 'b_spatial', 'mu_shift', 'w0', 'w_decay_up', 'a0',
            'w_aaa_up', 'w_gate_up', 'k_k', 'k_a', 'r_k', 'gn_gain', 'gn_bias', 'w_branch_a', 'w_branch_b', 'w_out', 'b_out',
            'ln1_g', 'ln1_b', 'w_ff1', 'b_ff1', 'w_ff2', 'b_ff2', 'ln2_g', 'ln2_b']
_BIG = {'w_ff1': (0, 512), 'w_ff2': (512, 512), 'w_out': (1024, 128), 'w_branch_a': (1152, 64), 'w_branch_b': (1216, 64), 'w_in': (1280, 640)}
_LATER_ROWS = 1280
_CUT_BY_COLS = ('w_ff1', 'w_in', 'w_branch_a', 'w_branch_b')
IN_SHARD = IN_COLS // _N_DEV
_LORA = {'w_decay_up': (0, LORA_W), 'w_aaa_up': (LORA_W, LORA_A), 'w_gate_up': (LORA_W + LORA_A, LORA_G)}
_COMM_DTYPE = jnp.bfloat16


def _pad_rows(a, rows):
    return jnp.pad(a, ((0, rows - a.shape[0]),) + ((0, 0),) * (a.ndim - 1))


def _pack_big(shards):
    blocks = []
    for n, (_, rows) in _BIG.items():
        a = shards[n].T if n in _CUT_BY_COLS else shards[n]
        blocks.append(_pad_rows(a.reshape(-1, D_MODEL), rows))
    return jnp.concatenate(blocks, axis=0)


def _unpack_big(block, like):
    out = {}
    for n, (r0, _) in _BIG.items():
        rr, cc = like[n].shape
        if n in _CUT_BY_COLS:
            out[n] = block[r0:r0 + rr * cc // D_MODEL].reshape(cc, rr).T
        else:
            out[n] = block[r0:r0 + rr]
    return out


def _to_padded(a, axis):
    g_end = 2 * G_WIDTH
    r_end = g_end + RW_USED
    take = lambda lo, hi: lax.slice_in_dim(a, lo, hi, axis=axis)
    zshape = list(a.shape)
    zshape[axis] = RW_COLS - RW_USED
    return jnp.concatenate([take(r_end, IN_COLS), take(g_end, r_end), jnp.zeros(zshape, a.dtype), take(0, g_end)], axis=axis)


def _from_padded(a, axis):
    take = lambda lo, hi: lax.slice_in_dim(a, lo, hi, axis=axis)
    return jnp.concatenate([take(2 * D_MODEL + RW_COLS, P_COLS), take(2 * D_MODEL, 2 * D_MODEL + RW_USED), take(0, 2 * D_MODEL)], axis=axis)


def _step(p, m, v, x, c, target):
    T = x.shape[0]
    xi, yi, ci = _place()
    me = 4 * xi + 2 * yi + ci
    tile = _pick(T, (512, 256))

    lane = jnp.arange(R_WIDTH)
    gsum = (lane[:128, None] // R_HEAD == lane[None, :128] // R_HEAD).astype(F32)
    expand = (jnp.arange(128)[:, None] == (lane[None, :] // (G_WIDTH // 8))).astype(F32)

    (c_act,), _ = _rowwise("silu_c", lambda tv, bv: ([tv[0] * _sigmoid(tv[0])], []), [c], [], [(D_MODEL, F32)], [], 1)
    small, places = _pack_rows([c_act, p['w_decay_up'], p['w_aaa_up'], p['w_gate_up']])
    small_all = _all_gather("gather_small", small)
    per_dev = [_unpack_rows(small_all[d], places) for d in range(_N_DEV)]
    c_act_all = _pad_rows(jnp.concatenate([pd[0] for pd in per_dev], axis=0), 16)
    lora_full = {n: jnp.concatenate([pd[i + 1] for pd in per_dev], axis=1) for i, n in enumerate(_LORA)}
    lora_pad = {n: jnp.zeros((LORA_PAD, R_WIDTH), F32).at[r0:r0 + nr].set(lora_full[n]) for n, (r0, nr) in _LORA.items()}

    big_names = list(_BIG)
    my_rows = _pack_big(p).astype(_MXU_DTYPE)
    b_in_p = _to_padded(p['b_in'], 1)
    mu_p = jnp.concatenate([p['mu_shift'], jnp.zeros((1, RW_COLS - RW_USED), F32)], axis=1)

    b_ada_mine = lax.dynamic_slice(p['b_ada'], (0, me * 768), (1, 768))
    mod_cols = _mm("ada_mod", c_act_all, p['w_ada'], "nn", bias=b_ada_mine)
    mod_all = _all_gather("gather_mod", mod_cols)
    mod = lax.dynamic_index_in_dim(mod_all, me, axis=1, keepdims=False).reshape(1, 6 * D_MODEL)
    sh1, sc1, gt1, sh2, sc2, gt2 = [mod[:, i * D_MODEL:(i + 1) * D_MODEL] for i in range(6)]

    (h,), _, w_in_all = _rowwise("modulate1", lambda tv, bv: (_seg_modulate(tv, bv), []), [x], [sc1, sh1], [(D_MODEL, _MXU_DTYPE)], [], tile,
                                 rider=(my_rows[_LATER_ROWS:], _gather_def, (0.0, 0.5, 1.0)))
    w_in_t = _to_padded(w_in_all[:, :IN_SHARD].reshape(IN_COLS, D_MODEL), 0)
    proj = _mm("in_proj", h, w_in_t, "nt", bias=b_in_p)
    ws = p['w_spatial']
    b_tg = jnp.zeros((G_CHUNK, 128), F32).at[:, :8].set(p['b_spatial'].T)
    gmlp_b = [p['g_ln_v'], p['b_ln_v'], ws, b_tg, expand]
    z_gmlp = (proj, 2 * G_WIDTH, 4)
    (ya,), _ = _rowwise("gmlp", lambda tv, bv: (_seg_gmlp(tv, bv), []), [z_gmlp], gmlp_b, [(G_WIDTH, F32)], [], G_CHUNK)
    z_rw = (proj, RW_COLS, 1)
    z_rw_halo = ("halo", proj, RW_COLS, 1)
    pre_b = [mu_p, p['w0'], lora_pad['w_decay_up'], p['a0'], lora_pad['w_aaa_up'], lora_pad['w_gate_up'], p['k_k'], p['k_a'], gsum]

    def pre_fwd(tv, bv):
        z_t, halo_t = tv
        return _seg_rwkv_pre([z_t, _shift_down(z_t, halo_t, pl.program_id(0) == 0)], bv), []

    pre_out, _ = _rowwise("rwkv_pre", pre_fwd, [z_rw, z_rw_halo], pre_b, [(R_WIDTH, F32)] * 7, [], tile)
    r_, lw_, k2_, v_, a_, b_, g_ = pre_out
    scan_in = (r_, lw_, k2_, v_, a_, b_)
    y_, states, inverses, later_all = _scan_fwd(*scan_in, rider=(my_rows[:_LATER_ROWS], _gather_def, (0.0, 0.875, 1.0)))

    def whole(n, rows):
        r0 = _BIG[n][0]
        return later_all[:, r0:r0 + rows].reshape(_N_DEV * rows, D_MODEL)

    w_ff1_t, w_ff2, w_out = whole('w_ff1', 512), whole('w_ff2', 512), whole('w_out', 128)
    w_ba_t = whole('w_branch_a', 64).reshape(D_MODEL, G_WIDTH)
    w_bb_t = whole('w_branch_b', 64).reshape(D_MODEL, R_WIDTH)
    post_b = [p['r_k'].reshape(1, R_WIDTH), p['gn_gain'], p['gn_bias'], gsum]
    (yb,), _ = _rowwise("rwkv_post", lambda tv, bv: (_seg_rwkv_post(tv, bv), []), [y_, r_, k2_, v_, g_], post_b, [(R_WIDTH, F32)], [], tile)
    pa = _mm("branch_a", ya, w_ba_t, "nt", out_dtype=_MXU_DTYPE)
    pb = _mm("branch_b", yb, w_bb_t, "nt", out_dtype=_MXU_DTYPE)
    gates = [(proj, D_MODEL, 0), (proj, D_MODEL, 1)]
    (merged,), _ = _rowwise("merge", lambda tv, bv: (_seg_merge(tv, bv), []), gates + [pa, pb], [], [(D_MODEL, _MXU_DTYPE)], [], tile)
    mix = _mm("out_proj", merged, w_out, "nn", bias=p['b_out'])
    mid_b = [gt1, p['ln1_g'], p['ln1_b'], sc2, sh2]
    (h1, h2in), _ = _rowwise("mid", lambda tv, bv: (_seg_mid(tv, bv), []), [x, mix], mid_b, [(D_MODEL, F32), (D_MODEL, _MXU_DTYPE)], [], tile)
    act = _mm("ff1", h2in, w_ff1_t, "nt", bias=p['b_ff1'], epi=lambda t: _seg_relu2([t], [])[0], epi_dtype=_MXU_DTYPE, raw=False)
    ff = _mm("ff2", act, w_ff2, "nn", bias=p['b_ff2'])

    def loss_fn(tv, bv):
        h1_t, ff_t, tgt = tv
        val, grads = jax.value_and_grad(lambda a0_, a1_, b0_, b1_, b2_: _seg_loss([a0_, a1_, tgt], [b0_, b1_, b2_]), argnums=(0, 1, 2, 3, 4))(h1_t, ff_t, *bv)
        return [grads[0], grads[1]], [grads[2], grads[3], grads[4], jnp.sum(grads[1], axis=0, keepdims=True), jnp.full((1, 128), val, F32)]

    (dh1_a, dff), (d_gt2, d_ln2_g, d_ln2_b, d_b_ff2, loss_row) = _rowwise(
        "loss", loss_fn, [h1, ff, target], [gt2, p['ln2_g'], p['ln2_b']], [(D_MODEL, F32), (D_MODEL, _MXU_DTYPE)], [(1, D_MODEL)] * 4 + [(1, 128)], tile)

    g = {}
    g['ln2_g'], g['ln2_b'], g['b_ff2'] = d_ln2_g, d_ln2_b, d_b_ff2
    gw = {}
    gw['w_ff2'] = _mm("g_w_ff2", act, dff, "tn", out_dtype=_COMM_DTYPE)
    df1, g['b_ff1'] = _mm("d_act", dff, w_ff2, "nt", beside=act, epi=lambda d_, a_: d_ * (2.0 * jnp.sqrt(a_.astype(F32))),
                          epi_dtype=_MXU_DTYPE, colsum=True)
    gw['w_ff1'] = _mm("g_w_ff1", df1, h2in, "tn", out_dtype=_COMM_DTYPE)
    dh2in = _mm("d_h2in", df1, w_ff1_t, "nn")
    (dx_a, dmix), (d_gt1, g['ln1_g'], g['ln1_b'], d_sc2, d_sh2), (g['b_out'],) = _rowwise_vjp(
        "mid_bwd", _seg_mid, [x, mix], mid_b, [dh1_a, dh2in], tile, [0, 1], [0, 1, 2, 3, 4], t_dtypes=[F32, _MXU_DTYPE], colsum=[1])
    gw['w_out'] = _mm("g_w_out", merged, dmix, "tn", out_dtype=_COMM_DTYPE)
    dmerged = _mm("d_merged", dmix, w_out, "nt", out_dtype=_MXU_DTYPE)
    (dproj, dpa, dpb), _, (cs_gates,) = _rowwise_vjp(
        "merge_bwd", _seg_merge, gates + [pa, pb], [], [dmerged], tile, [0, 1, 2, 3], [], t_dtypes=[_MXU_DTYPE] * 3, colsum=[0],
        finish=lambda dts, sc: [jnp.concatenate(dts[:2], axis=1), dts[2], dts[3]], out_widths=[2 * D_MODEL, D_MODEL, D_MODEL],
        into={0: (lax.empty((T, P_COLS), _MXU_DTYPE), 0)})
    gw['w_branch_a'] = _mm("g_w_branch_a", dpa, ya, "tn", out_dtype=_COMM_DTYPE)
    gw['w_branch_b'] = _mm("g_w_branch_b", dpb, yb, "tn", out_dtype=_COMM_DTYPE)
    dya = _mm("d_ya", dpa, w_ba_t, "nn")
    dyb = _mm("d_yb", dpb, w_bb_t, "nn")
    def send_rows(names):
        parts = []
        for n in names:
            per_dev = gw[n].reshape(_N_DEV, -1, D_MODEL)
            parts.append(jnp.pad(per_dev, ((0, 0), (0, _BIG[n][1] - per_dev.shape[1]), (0, 0))))
        return jnp.concatenate(parts, axis=1) if len(parts) > 1 else parts[0]

    send_early = send_rows(big_names[:-1])
    (dy, dr1, dk1, dv1, dg_), (d_r_k, g['gn_gain'], g['gn_bias']), _, sibling_early = _rowwise_vjp(
        "rwkv_post_bwd", _seg_rwkv_post, [y_, r_, k2_, v_, g_], post_b, [dyb], tile, [0, 1, 2, 3, 4], [0, 1, 2],
        rider=(send_early, _sibling_def, (0.0, 1.0)))
    g['r_k'] = d_r_k
    partials_early = _chip_partials("chip_partials_early", send_early, sibling_early, 128, _COMM_DTYPE)
    dr2, dlw, dk2, dv2, da, db, landed_early = _scan_bwd(*scan_in, states, inverses, dy, rider=(partials_early, _chips_def, (0.0, 1.0)))
    pre_tile = 256
    last_step = T // pre_tile - 1

    def pre_prep(prim):
        z_t, halo_t = prim
        return [z_t, _shift_down(z_t, halo_t, pl.program_id(0) == last_step)]

    def pre_finish(dts, sc):
        dz_direct, dprev = dts
        (row_after,) = sc

        @pl.when(pl.program_id(0) == 0)
        def _():
            row_after[...] = jnp.zeros_like(row_after)

        dz = dz_direct + _shift_up(dprev, row_after[...])
        row_after[...] = dprev[:_HALO]
        return [dz]

    small_names = [n for n in _WEIGHTS if n not in _BIG and n not in ('w_ada', 'b_ada')]
    (dproj,), (g['g_ln_v'], g['b_ln_v'], g['w_spatial'], d_b_tg), (cs_g,) = _rowwise_vjp(
        "gmlp_bwd", _seg_gmlp, [z_gmlp], gmlp_b, [dya], G_CHUNK, [0], [0, 1, 2, 3], t_dtypes=[_MXU_DTYPE], colsum=[0],
        into={0: (dproj, 4)})
    g['b_spatial'] = d_b_tg[:, :8].T
    names_1 = [n for n in small_names if n in g]
    packed_1, places_1 = _pack_rows([g[n] for n in names_1] + [loss_row], row_mult=256)
    (dproj,), (d_mu, g['w0'], d_wd, g['a0'], d_wa, d_wg, g['k_k'], g['k_a']), (cs_rw,), small_all_1 = _rowwise_vjp(
        "rwkv_pre_bwd", _seg_rwkv_pre, [z_rw, z_rw_halo], pre_b, [[dr1, dr2], dlw, [dk1, dk2], [dv1, dv2], da, db, dg_], pre_tile,
        [0, 1], [0, 1, 2, 3, 4, 5, 6, 7], t_dtypes=[_MXU_DTYPE], colsum=[0], prep=pre_prep, finish=pre_finish,
        out_widths=[RW_COLS], reverse=True, scratch=[(_HALO, RW_COLS)], into={0: (dproj, 1)},
        rider=(packed_1, _gather_def, (0.0, 0.6, 1.0)))
    g['mu_shift'] = d_mu[:, :RW_USED]
    for n, d_ in (('w_decay_up', d_wd), ('w_aaa_up', d_wa), ('w_gate_up', d_wg)):
        r0, nr = _LORA[n]
        g[n] = d_[r0:r0 + nr]
    g['b_in'] = _from_padded(jnp.concatenate([cs_gates, cs_rw, cs_g], axis=1), 1)
    names_2 = [n for n in small_names if n not in names_1]
    packed_2, places_2 = _pack_rows([g[n] for n in names_2], row_mult=256)
    gw_in_t, small_all_2 = _mm("g_w_in", dproj, h, "tn", out_dtype=_COMM_DTYPE, rider=(packed_2, _gather_def, (0.0, 0.6, 1.0)))
    gw['w_in'] = _from_padded(gw_in_t, 0)
    send_late = send_rows(big_names[-1:])
    sibling_late = _exchange("pair_exchange_late", send_late, _sibling_def)
    partials_late = _chip_partials("chip_partials_late", send_late, sibling_late, 128, _COMM_DTYPE)
    dh, landed_late = _mm("d_h", dproj, w_in_t, "nn", rider=(partials_late, _chips_def, (0.0, 1.0)))

    def mod1_bwd(tv, bv):
        x_t, dh_t, dxa_t = tv
        (sc,) = bv
        return [dxa_t + dh_t * (1.0 + sc)], [jnp.sum(dh_t * x_t, axis=0, keepdims=True), jnp.sum(dh_t, axis=0, keepdims=True)]

    (grad_x,), (d_sc1, d_sh1) = _rowwise("modulate1_bwd", mod1_bwd, [x, dh, dx_a], [sc1], [(D_MODEL, F32)], [(1, D_MODEL)] * 2, tile)

    dmod = jnp.concatenate([d_sh1, d_sc1, d_gt1, d_sh2, d_sc2, d_gt2], axis=1).reshape(6 * D_MODEL // 128, 128)
    dmod_all = _all_gather("gather_dmod", dmod)
    g['b_ada'] = _sum_leading("sum_dmod", dmod_all).reshape(1, 6 * D_MODEL)
    dmod_mine = lax.dynamic_slice(dmod_all.reshape(_N_DEV, 6 * D_MODEL), (0, me * 768), (_N_DEV, 768))
    g_w_ada = _mm("g_w_ada", c_act_all, _pad_rows(dmod_mine, 16), "tn")

    sums_1 = _unpack_rows(_sum_leading("sum_small_1", small_all_1), places_1)
    loss = sums_1.pop()[0, 0]
    sums_2 = _unpack_rows(_sum_leading("sum_small_2", small_all_2), places_2)
    g.update(zip(names_1 + names_2, sums_1 + sums_2, strict=True))
    for n in _LORA:
        g[n] = lax.dynamic_slice(g[n], (0, me * R_HEAD), (g[n].shape[0], R_HEAD))
    g['w_ada'] = g_w_ada

    summed = jnp.concatenate([_sum_leading("sum_early", landed_early), _sum_leading("sum_late", landed_late)], axis=0)
    g.update(_unpack_big(summed, p))

    delta, new_m, new_v = {}, {}, {}
    own_call = ['w_ada'] + big_names
    for n in own_call:
        if n == 'w_in':
            r0 = _BIG[n][0]
            outs_t = _adamw("adamw_" + n, p[n].T, summed[r0:r0 + IN_SHARD], m[n].T, v[n].T)
            delta[n], new_m[n], new_v[n] = [o.T for o in outs_t]
        else:
            delta[n], new_m[n], new_v[n] = _adamw("adamw_" + n, p[n], g[n], m[n], v[n])
    rest = [n for n in _WEIGHTS if n not in own_call]
    outs = _adamw_many("adamw_rest", *[[d[n].reshape(p[n].shape) for n in rest] for d in (p, g, m, v)])
    for d, o in zip((delta, new_m, new_v), outs, strict=True):
        d.update(zip(rest, o, strict=True))
    return loss, grad_x, g, delta, new_m, new_v


def kernel(x, c, w_ada, b_ada, w_in, b_in, g_ln_v, b_ln_v, w_spatial, b_spatial, mu_shift, w0, w_decay_up, a0, w_aaa_up, w_gate_up, k_k, k_a, r_k, gn_gain, gn_bias, w_branch_a, w_branch_b, w_out, b_out, ln1_g, ln1_b, w_ff1, b_ff1, w_ff2, b_ff2, ln2_g, ln2_b, loss_target, m_w_ada, m_b_ada, m_w_in, m_b_in, m_g_ln_v, m_b_ln_v, m_w_spatial, m_b_spatial, m_mu_shift, m_w0, m_w_decay_up, m_a0, m_w_aaa_up, m_w_gate_up, m_k_k, m_k_a, m_r_k, m_gn_gain, m_gn_bias, m_w_branch_a, m_w_branch_b, m_w_out, m_b_out, m_ln1_g, m_ln1_b, m_w_ff1, m_b_ff1, m_w_ff2, m_b_ff2, m_ln2_g, m_ln2_b, v_w_ada, v_b_ada, v_w_in, v_b_in, v_g_ln_v, v_b_ln_v, v_w_spatial, v_b_spatial, v_mu_shift, v_w0, v_w_decay_up, v_a0, v_w_aaa_up, v_w_gate_up, v_k_k, v_k_a, v_r_k, v_gn_gain, v_gn_bias, v_w_branch_a, v_w_branch_b, v_w_out, v_b_out, v_ln1_g, v_ln1_b, v_w_ff1, v_b_ff1, v_w_ff2, v_b_ff2, v_ln2_g, v_ln2_b):
    given = dict(locals())
    shapes = {n: given[n].shape for n in _WEIGHTS}
    def two_d(a_):
        a_ = a_[0]
        return a_.reshape(1, -1) if a_.ndim == 1 else a_
    p = {n: two_d(given[n]) for n in _WEIGHTS}
    m = {n: two_d(given["m_" + n]) for n in _WEIGHTS}
    v = {n: two_d(given["v_" + n]) for n in _WEIGHTS}
    loss, grad_x, g, delta, new_m, new_v = _step(p, m, v, x[0], c, loss_target[0])
    outs = [loss, grad_x[None]]
    for d in (g, delta, new_m, new_v):
        outs += [d[n].reshape(shapes[n]) for n in _WEIGHTS]
    return tuple(outs)
```

```python
import functools

import jax
import jax.numpy as jnp
from jax import lax
from jax.experimental import pallas as pl
from jax.experimental.pallas import tpu as pltpu

F32 = jnp.float32
_MXU_DTYPE = jnp.bfloat16
_HI = lax.Precision.HIGHEST
_VMEM_LIMIT = 48 * 1024 * 1024
_MESH_ID = pl.DeviceIdType.MESH
_N_DEV = 8

D_MODEL = 1024
G_WIDTH = 512
G_CHUNK = 128
R_WIDTH = 512
R_HEADS = 8
R_HEAD = 64
LORA_W, LORA_A, LORA_G = 32, 32, 96
D_FF = 4096
ALPHA = 2.0 ** 0.25
LN_EPS = 1e-5
GN_EPS = 64e-5
SCAN_CHUNK = 64
SCAN_PER_STEP = 4
ADAM_LR, ADAM_B1, ADAM_B2, ADAM_EPS, ADAM_WD, ADAM_STEP = 0.001, 0.9, 0.999, 1e-08, 0.01, 10

P_COLS = 5120
RW_COLS = 2048
RW_USED = 3 * R_WIDTH + LORA_W + LORA_A + LORA_G
LORA_PAD = 256
IN_COLS = 2 * G_WIDTH + RW_USED + 2 * D_MODEL


def _cparams(sem=None, **kw):
    if sem is not None:
        kw["dimension_semantics"] = sem
    return pltpu.CompilerParams(vmem_limit_bytes=_VMEM_LIMIT, **kw)


def _dot(a, b, dims=(((1,), (0,)), ((), ())), hi=False):
    if hi:
        return lax.dot_general(a.astype(F32), b.astype(F32), dims, precision=_HI, preferred_element_type=F32)
    return lax.dot_general(a.astype(_MXU_DTYPE), b.astype(_MXU_DTYPE), dims, preferred_element_type=F32)


_NN = (((1,), (0,)), ((), ()))
_NT = (((1,), (1,)), ((), ()))
_TN = (((0,), (0,)), ((), ()))


def _pick(n, pref):
    for t in pref:
        if n % t == 0:
            return t
    return n


def _mm(name, a, b, mode, bias=None, out_dtype=F32, epi=None, epi_dtype=None, raw=True, beside=None, colsum=False, rider=None,
        tm=None, tn=None, tk=None):
    if mode == "nn":
        (M, K), (_, N) = a.shape, b.shape
    elif mode == "nt":
        (M, K), (N, _) = a.shape, b.shape
    else:
        (K, M), (_, N) = a.shape, b.shape
    tm = tm or _pick(M, (2048, 1280, 1024, 512, 256, 128, 64, 32, 16, 8))
    tn = tn or _pick(N, (1024, 512, 640, 384, 256, 128))
    tk = tk or _pick(K, (1024, 512, 256, 128))
    nk = K // tk
    dims = {"nn": _NN, "nt": _NT, "tn": _TN}[mode]
    a_spec = pl.BlockSpec((tk, tm), lambda i, j, k: (k, i)) if mode == "tn" else pl.BlockSpec((tm, tk), lambda i, j, k: (i, k))
    b_spec = pl.BlockSpec((tn, tk), lambda i, j, k: (j, k)) if mode == "nt" else pl.BlockSpec((tk, tn), lambda i, j, k: (k, j))
    o_spec = pl.BlockSpec((tm, tn), lambda i, j, k: (i, j))
    has_bias, has_beside = bias is not None, beside is not None
    two = epi is not None and not has_beside and raw
    only_epi = epi is not None and not has_beside and not raw
    grid = (M // tm, N // tn, nk)
    ride_shape, ride_sems, _ = rider[1](rider[0]) if rider else (None, [], None)

    def body(*refs):
        refs = list(refs)
        n_in = 2 + has_bias + has_beside
        if rider:
            sem_refs = [refs.pop() for _ in ride_sems][::-1]
            ride_out = refs.pop(n_in + 1 + 1 + two)
            ride_in = refs.pop(n_in)
            step = (pl.program_id(0) * grid[1] + pl.program_id(1)) * grid[2] + pl.program_id(2)
            _ride(rider, step, grid[0] * grid[1] * grid[2], ride_in, ride_out, sem_refs)
        a_ref, b_ref = refs[0], refs[1]
        bias_ref = refs[2] if has_bias else None
        beside_ref = refs[n_in - 1] if has_beside else None
        outs = refs[n_in:]
        o_ref, acc_ref = outs[0], outs[-1]
        k = pl.program_id(2)

        @pl.when(k == 0)
        def _():
            acc_ref[...] = jnp.zeros_like(acc_ref)

        acc_ref[...] += _dot(a_ref[...], b_ref[...], dims)

        @pl.when(k == nk - 1)
        def _():
            res = acc_ref[...]
            if has_bias:
                res = res + bias_ref[...]
            if has_beside:
                val = epi(res, beside_ref[...])
                o_ref[...] = val.astype(o_ref.dtype)
                if colsum:
                    outs[1][...] = jnp.broadcast_to(jnp.sum(val, axis=0, keepdims=True), outs[1].shape)
            elif only_epi:
                o_ref[...] = epi(res).astype(o_ref.dtype)
            else:
                o_ref[...] = res.astype(o_ref.dtype)
                if two:
                    outs[1][...] = epi(res).astype(outs[1].dtype)

    in_specs = [a_spec, b_spec]
    args = [a, b]
    if has_bias:
        in_specs.append(pl.BlockSpec((1, tn), lambda i, j, k: (0, j)))
        args.append(bias)
    if has_beside:
        in_specs.append(o_spec)
        args.append(beside)
    out_shape = [jax.ShapeDtypeStruct((M, N), epi_dtype if (has_beside or only_epi) else out_dtype)]
    out_specs = [o_spec]
    if two:
        out_shape.append(jax.ShapeDtypeStruct((M, N), epi_dtype))
        out_specs.append(o_spec)
    if colsum:
        assert has_beside and not rider
        out_shape.append(jax.ShapeDtypeStruct((grid[0] * 8, N), F32))
        out_specs.append(pl.BlockSpec((8, tn), lambda i, j, k: (i, j)))
    if rider:
        in_specs.append(_HBM)
        args.append(rider[0])
        out_shape.append(ride_shape)
        out_specs.append(_HBM)
    res = pl.pallas_call(
        body, name=name, grid=grid, in_specs=in_specs, out_specs=out_specs, out_shape=out_shape,
        scratch_shapes=[pltpu.VMEM((tm, tn), F32)] + list(ride_sems),
        compiler_params=_cparams(("arbitrary",) * 3 if rider else ("parallel", "parallel", "arbitrary")),
    )(*args)
    if colsum:
        return res[0], _sum_leading(name + "_colsum", res[1].reshape(grid[0], 8, N))[0:1]
    return res if (two or rider) else res[0]


_HALO = 8


def _rowwise(name, fn, tiled, bcast, tiled_out, red_out, tile, reverse=False, scratch=(), rider=None):
    tiled = [t if isinstance(t, tuple) else (t, t.shape[1], 0) for t in tiled]
    T = next(t[0] for t in tiled if not isinstance(t[0], str)).shape[0]
    n = T // tile
    nt, nb, nto, nsc, nro = len(tiled), len(bcast), len(tiled_out), len(scratch), len(red_out)
    ride_shape, ride_sems, _ = rider[1](rider[0]) if rider else (None, [], None)
    nr = 1 if rider else 0
    into = [(j, t[2], t[3]) for j, t in enumerate(tiled_out) if len(t) == 4]
    na = len(into)

    def row_block(i):
        return n - 1 - i if reverse else i

    def body(*refs):
        i2 = nt + nb
        o0 = i2 + nr + na
        o1, o2 = o0 + nto, o0 + nto + nro
        s0 = o2 + nr
        if rider:
            _ride(rider, pl.program_id(0), n, refs[i2], refs[o2], refs[s0 + nsc:])
        t_refs, b_refs = refs[:nt], refs[nt:i2]
        to_refs, ro_refs = refs[o0:o1], refs[o1:o2]
        extra = (list(refs[s0:s0 + nsc]),) if nsc else ()
        touts, routs = fn([r[...] for r in t_refs], [r[...] for r in b_refs], *extra)
        for r, v in zip(to_refs, touts, strict=True):
            r[...] = v.astype(r.dtype)
        if ro_refs:
            i = pl.program_id(0)

            @pl.when(i == 0)
            def _():
                for r, v in zip(ro_refs, routs, strict=True):
                    r[...] = v.astype(F32)

            @pl.when(i > 0)
            def _():
                for r, v in zip(ro_refs, routs, strict=True):
                    r[...] += v.astype(F32)

    def whole(shape):
        nd = len(shape)
        return pl.BlockSpec(tuple(shape), lambda i: (0,) * nd)

    per_tile = tile // _HALO
    in_specs, arrays = [], []
    for t in tiled:
        if isinstance(t[0], str):
            _, arr, w, cb = t
            in_specs.append(pl.BlockSpec((_HALO, w), functools.partial(lambda i, cb: (jnp.maximum(row_block(i) * per_tile - 1, 0), cb), cb=cb)))
        else:
            arr, w, cb = t
            in_specs.append(pl.BlockSpec((tile, w), functools.partial(lambda i, cb: (row_block(i), cb), cb=cb)))
        arrays.append(arr)
    in_specs += [whole(b.shape) for b in bcast]
    out_specs, out_shape = [], []
    for t in tiled_out:
        cb = t[3] if len(t) == 4 else 0
        out_specs.append(pl.BlockSpec((tile, t[0]), functools.partial(lambda i, cb: (row_block(i), cb), cb=cb)))
        out_shape.append(jax.ShapeDtypeStruct(t[2].shape if len(t) == 4 else (T, t[0]), t[1]))
    out_specs += [whole(s) for s in red_out]
    out_shape += [jax.ShapeDtypeStruct(tuple(s), F32) for s in red_out]
    ride_args = []
    if rider:
        in_specs.append(_HBM)
        out_specs.append(_HBM)
        out_shape.append(ride_shape)
        ride_args = [rider[0]]
    in_specs += [pl.BlockSpec(memory_space=pl.ANY)] * na
    aliases = {nt + nb + nr + k: j for k, (j, _, _) in enumerate(into)}
    res = pl.pallas_call(
        body, name=name, grid=(n,), in_specs=in_specs, out_specs=out_specs, out_shape=out_shape,
        scratch_shapes=[pltpu.VMEM(tuple(s), F32) for s in scratch] + list(ride_sems),
        input_output_aliases=aliases, compiler_params=_cparams(("arbitrary",)),
    )(*arrays, *bcast, *ride_args, *[buf for (_, buf, _) in into])
    if rider:
        return list(res[:nto]), list(res[nto:nto + nro]), res[-1]
    return list(res[:nto]), list(res[nto:])


def _rowwise_vjp(name, f, tiled, bcast, cts, tile, wrt_t, wrt_b, t_dtypes=None, colsum=(), prep=None, finish=None,
                 out_widths=None, reverse=False, scratch=(), rider=None, into=None):
    tiled = [t if isinstance(t, tuple) else (t, t.shape[1], 0) for t in tiled]
    npr = len(tiled)
    t_dtypes = t_dtypes or [F32] * len(wrt_t)
    groups = [c if isinstance(c, list) else [c] for c in cts]
    cts = [a_ for grp in groups for a_ in grp]

    def fn(tv, bv, sc=None):
        prim, flat_ct = tv[:npr], list(tv[npr:])
        if prep is not None:
            prim = prep(prim)
        ct = []
        for grp in groups:
            parts = [flat_ct.pop(0).astype(F32) for _ in grp]
            ct.append(functools.reduce(lambda p_, q_: p_ + q_, parts))

        def g(dt_vals, db_vals):
            full_t, full_b = list(prim), list(bv)
            for i, v in zip(wrt_t, dt_vals, strict=True):
                full_t[i] = v
            for j, v in zip(wrt_b, db_vals, strict=True):
                full_b[j] = v
            return f(full_t, full_b)

        outs, pull = jax.vjp(g, [prim[i].astype(F32) for i in wrt_t], [bv[j] for j in wrt_b])
        dts, dbs = pull([c.astype(o.dtype) for c, o in zip(ct, outs, strict=True)])
        if finish is not None:
            dts = finish(dts, sc)
        sums = [jnp.sum(dts[i].astype(F32), axis=0, keepdims=True) for i in colsum]
        return dts, list(dbs) + sums

    widths = out_widths or [tiled[i][1] for i in wrt_t]
    tiled_out = [(w, dt) + tuple((into or {}).get(j, ())) for j, (w, dt) in enumerate(zip(widths, t_dtypes, strict=True))]
    red_out = [bcast[j].shape for j in wrt_b] + [(1, widths[i]) for i in colsum]
    res = _rowwise(name, fn, tiled + list(cts), bcast, tiled_out, red_out, tile, reverse=reverse, scratch=scratch, rider=rider)
    dts, reds = res[0], res[1]
    nb = len(wrt_b)
    return (dts, reds[:nb], reds[nb:]) + tuple(res[2:])


def _layer_norm(x, g, b, eps):
    mu = jnp.mean(x, axis=-1, keepdims=True)
    xc = x - mu
    var = jnp.mean(xc * xc, axis=-1, keepdims=True)
    return xc * lax.rsqrt(var + eps) * g + b


def _gelu_tanh(x):
    return 0.5 * x * (1.0 + jnp.tanh(0.7978845608028654 * (x + 0.044715 * (x * x * x))))


def _sigmoid(x):
    return 1.0 / (1.0 + jnp.exp(-x))


def _seg_modulate(tv, bv):
    (x,), (sc, sh) = tv, bv
    return [x * (1.0 + sc) + sh]


def _seg_gmlp(tv, bv):
    (z,), (g_ln, b_ln, ws, b_tg, expand) = tv, bv
    bias_full = _dot(b_tg, expand, hi=True)
    zz = _gelu_tanh(z)
    u, v = zz[:, :G_WIDTH], zz[:, G_WIDTH:]
    v = _layer_norm(v, g_ln, b_ln, LN_EPS)
    row = lax.broadcasted_iota(jnp.int32, (G_CHUNK, G_CHUNK), 0)
    col = lax.broadcasted_iota(jnp.int32, (G_CHUNK, G_CHUNK), 1)
    causal = col <= row
    first_group = lax.broadcasted_iota(jnp.int32, (G_CHUNK, 128), 1) < 64
    parts = []
    for p in range(4):
        vp = v[:, 128 * p:128 * (p + 1)]
        s_even = _dot(jnp.where(causal, ws[2 * p], 0.0), vp)
        s_odd = _dot(jnp.where(causal, ws[2 * p + 1], 0.0), vp)
        parts.append(jnp.where(first_group, s_even, s_odd))
    s = jnp.concatenate(parts, axis=1) + bias_full
    return [u * s]


def _split2(x):
    hi = x.astype(_MXU_DTYPE)
    return hi, (x - hi.astype(F32)).astype(_MXU_DTYPE)


@jax.custom_vjp
def _group_sum(x, ones_blocks):
    hi, lo = _split2(x)
    w = ones_blocks.shape[0]
    return jnp.concatenate([_dot(hi[:, j:j + w], ones_blocks) + _dot(lo[:, j:j + w], ones_blocks) for j in range(0, x.shape[1], w)], axis=1)


def _group_sum_fwd(x, ones_blocks):
    return _group_sum(x, ones_blocks), ones_blocks


def _group_sum_bwd(ones_blocks, ct):
    return _group_sum(ct, ones_blocks), jnp.zeros_like(ones_blocks)


_group_sum.defvjp(_group_sum_fwd, _group_sum_bwd)


def _shift_down(z, halo, is_first):
    _, W = z.shape
    rolled = pltpu.roll(z, 1, 0)
    before = jnp.where(is_first, 0.0, pltpu.roll(halo, 1, 0))
    top_row = lax.broadcasted_iota(jnp.int32, (_HALO, W), 0) == 0
    return jnp.concatenate([jnp.where(top_row, before, rolled[:_HALO]), rolled[_HALO:]], axis=0)


def _shift_up(d, after):
    tile, W = d.shape
    rolled = pltpu.roll(d, tile - 1, 0)
    last_row = lax.broadcasted_iota(jnp.int32, (_HALO, W), 0) == _HALO - 1
    bottom = jnp.where(last_row, pltpu.roll(after, _HALO - 1, 0), rolled[tile - _HALO:])
    return jnp.concatenate([rolled[:tile - _HALO], bottom], axis=0)


def _seg_rwkv_pre(tv, bv):
    (z, prev), (mu, w0, wd, a0, wa, wg, k_k, k_a, gsum) = tv, bv
    zs = z + (prev - z) * mu
    r, k, v = zs[:, 0:512], zs[:, 512:1024], zs[:, 1024:1536]
    zl = zs[:, 3 * R_WIDTH:3 * R_WIDTH + LORA_PAD]
    x = w0 + _dot(jnp.tanh(zl), wd)
    softplus = jnp.maximum(-x, 0.0) + jnp.log(1.0 + jnp.exp(-jnp.abs(x)))
    lw = -jnp.exp(-softplus - 0.5)
    a = _sigmoid(a0 + _dot(zl, wa))
    g = _dot(_sigmoid(zl), wg)
    kk = k * k_k
    nrm = jnp.sqrt(_group_sum(kk * kk, gsum))
    kk = kk / jnp.maximum(nrm, 1e-12)
    k2 = k * (1.0 + (a - 1.0) * k_a)
    return [r, lw, k2, v, -kk, kk * a, g]


def _seg_rwkv_post(tv, bv):
    (y, r, k2, v, g), (r_k, gain, bias, gsum) = tv, bv
    mu = _group_sum(y, gsum) * (1.0 / R_HEAD)
    yc = y - mu
    var = _group_sum(yc * yc, gsum) * (1.0 / R_HEAD)
    yn = yc * lax.rsqrt(var + GN_EPS) * gain + bias
    bonus = _group_sum(r * k2 * r_k, gsum) * v
    return [(yn + bonus) * g]


def _seg_merge(tv, bv):
    (ga, gb, pa, pb), () = tv, bv
    return [_sigmoid(ga) * pa + _sigmoid(gb) * pb]


def _seg_mid(tv, bv):
    (x, mix), (gt1, g1, b1, sc2, sh2) = tv, bv
    h1 = _layer_norm(ALPHA * x + gt1 * mix, g1, b1, LN_EPS)
    return [h1, h1 * (1.0 + sc2) + sh2]


def _seg_relu2(tv, bv):
    (f1,), () = tv, bv
    return [jnp.square(jnp.maximum(f1, 0.0))]


def _seg_loss(tv, bv):
    (h1, ff, target), (gt2, g2, b2) = tv, bv
    out = _layer_norm(ALPHA * h1 + gt2 * ff, g2, b2, LN_EPS)
    err = jnp.square(out - target)
    return 0.5 * jnp.sum(jnp.mean(err, axis=-1))


_BNN = (((2,), (1,)), ((0,), (0,)))
_BNT = (((2,), (2,)), ((0,), (0,)))
_BTN = (((1,), (1,)), ((0,), (0,)))


def _tri_dot(x, dims):
    L = x.shape[0]
    tri = (lax.broadcasted_iota(jnp.int32, (L, L), 1) <= lax.broadcasted_iota(jnp.int32, (L, L), 0)).astype(F32)
    hi, lo = _split2(x)
    return _dot(tri, hi, dims) + _dot(tri, lo, dims)


@jax.custom_vjp
def _running_sum(x):
    return _tri_dot(x, _NN)


_running_sum.defvjp(lambda x: (_tri_dot(x, _NN), None), lambda _, ct: (_tri_dot(ct, _TN),))


def _cut_heads(x):
    return jnp.stack([x[:, R_HEAD * h:R_HEAD * (h + 1)] for h in range(R_HEADS)])


def _join_heads(x):
    return jnp.concatenate([x[h] for h in range(R_HEADS)], axis=1)


@jax.custom_vjp
def _split_heads(x):
    return _cut_heads(x)


@jax.custom_vjp
def _merge_heads(x):
    return _join_heads(x)


_split_heads.defvjp(lambda x: (_cut_heads(x), None), lambda _, ct: (_join_heads(ct),))
_merge_heads.defvjp(lambda x: (_join_heads(x), None), lambda _, ct: (_cut_heads(ct),))


def _inverse_pullback(inv, ct):
    return _dot(_dot(inv, ct, _BTN), inv, _BNT)


@jax.custom_vjp
def _unit_lower_inverse(n_mat):
    H, L, _ = n_mat.shape
    eye = lax.broadcasted_iota(jnp.int32, (H, L, L), 1) == lax.broadcasted_iota(jnp.int32, (H, L, L), 2)
    inv = jnp.where(eye, 1.0, 0.0) + n_mat
    pw = n_mat
    n = 2
    while n < L:
        pw = _dot(pw, pw, _BNN)
        inv = inv + _dot(inv, pw, _BNN)
        n *= 2
    return inv


def _unit_lower_inverse_fwd(n_mat):
    inv = _unit_lower_inverse(n_mat)
    return inv, inv


_unit_lower_inverse.defvjp(_unit_lower_inverse_fwd, lambda inv, ct: (_inverse_pullback(inv, ct),))


@jax.custom_vjp
def _known_inverse(n_mat, inv):
    return inv


_known_inverse.defvjp(lambda n_mat, inv: (inv, inv), lambda inv, ct: (_inverse_pullback(inv, ct), jnp.zeros_like(inv)))


def _scan_chunk(r, lw, k, v, a, b, s0, inv=None, with_inverse=False):
    L, H = r.shape[0], R_HEADS
    cs = _running_sum(lw)
    cs_end = cs[L - 1:L, :]
    p, p_inv, to_end = jnp.exp(cs), jnp.exp(-cs), jnp.exp(cs_end - cs)
    at, bt, kt, rt = [_split_heads(t) for t in (a * jnp.exp(cs - lw), b * p_inv, k * p_inv, r * p)]
    b_end, k_end, v = [_split_heads(t) for t in (b * to_end, k * to_end, v)]
    row = lax.broadcasted_iota(jnp.int32, (H, L, L), 1)
    col = lax.broadcasted_iota(jnp.int32, (H, L, L), 2)
    incl, strict = col <= row, col < row
    a_ab = jnp.where(strict, _dot(at, bt, _BNT), 0.0)
    a_ak = jnp.where(strict, _dot(at, kt, _BNT), 0.0)
    a_rb = jnp.where(incl, _dot(rt, bt, _BNT), 0.0)
    a_rk = jnp.where(incl, _dot(rt, kt, _BNT), 0.0)
    inv = _unit_lower_inverse(a_ab) if inv is None else _known_inverse(a_ab, inv)
    u = _dot(inv, _dot(at, s0, _BNT) + _dot(a_ak, v, _BNN), _BNN)
    y = _merge_heads(_dot(rt, s0, _BNT) + _dot(a_rb, u, _BNN) + _dot(a_rk, v, _BNN))
    s1 = s0 * _split_heads(jnp.exp(cs_end)) + _dot(u, b_end, _BTN) + _dot(v, k_end, _BTN)
    return (y, s1, inv) if with_inverse else (y, s1)


def _scan_fwd(r, lw, k, v, a, b, rider):
    T = r.shape[0]
    H, N, L, P = R_HEADS, R_HEAD, SCAN_CHUNK, SCAN_PER_STEP
    nc = T // L
    steps = nc // P
    ride_shape, ride_sems, _ = rider[1](rider[0])

    def body(r_ref, lw_ref, k_ref, v_ref, a_ref, b_ref, ride_in, y_ref, st_ref, inv_ref, ride_out, s_ref, *sem_refs):
        _ride(rider, pl.program_id(0), steps, ride_in, ride_out, sem_refs)

        @pl.when(pl.program_id(0) == 0)
        def _():
            s_ref[...] = jnp.zeros_like(s_ref)

        s0 = s_ref[...]
        for j in range(P):
            rows = pl.ds(j * L, L)
            st_ref[j] = s0
            y, s0, inv = _scan_chunk(*[t[rows, :] for t in (r_ref, lw_ref, k_ref, v_ref, a_ref, b_ref)], s0, with_inverse=True)
            y_ref[rows, :] = y
            inv_ref[j] = inv
        s_ref[...] = s0

    blk = pl.BlockSpec((P * L, R_WIDTH), lambda c: (c, 0))
    per_chunk = pl.BlockSpec((P, H, N, N), lambda c: (c, 0, 0, 0))
    return pl.pallas_call(
        body, name="scan_fwd", grid=(steps,), in_specs=[blk] * 6 + [_HBM], out_specs=[blk, per_chunk, per_chunk, _HBM],
        out_shape=[jax.ShapeDtypeStruct((T, R_WIDTH), F32)] + [jax.ShapeDtypeStruct((nc, H, N, N), F32)] * 2 + [ride_shape],
        scratch_shapes=[pltpu.VMEM((H, N, N), F32)] + list(ride_sems),
        compiler_params=_cparams(("arbitrary",)),
    )(r, lw, k, v, a, b, rider[0])


def _scan_bwd(r, lw, k, v, a, b, states, inverses, dy, rider):
    T = r.shape[0]
    H, N, L, P = R_HEADS, R_HEAD, SCAN_CHUNK, SCAN_PER_STEP
    nc = T // L
    steps = nc // P
    ride_shape, ride_sems, _ = rider[1](rider[0])

    def body(r_ref, lw_ref, k_ref, v_ref, a_ref, b_ref, st_ref, inv_ref, dy_ref, ride_in,
             dr_ref, dlw_ref, dk_ref, dv_ref, da_ref, db_ref, ride_out, ds_ref, *sem_refs):
        _ride(rider, pl.program_id(0), steps, ride_in, ride_out, sem_refs)

        @pl.when(pl.program_id(0) == 0)
        def _():
            ds_ref[...] = jnp.zeros_like(ds_ref)

        ds = ds_ref[...]
        for j in reversed(range(P)):
            rows = pl.ds(j * L, L)
            args = [t[rows, :] for t in (r_ref, lw_ref, k_ref, v_ref, a_ref, b_ref)] + [st_ref[j]]
            inv = inv_ref[j]
            _, pull = jax.vjp(lambda *xs, inv=inv: _scan_chunk(*xs, inv=inv), *args)
            grads = pull((dy_ref[rows, :], ds))
            for o_ref, g_ in zip((dr_ref, dlw_ref, dk_ref, dv_ref, da_ref, db_ref), grads[:6], strict=True):
                o_ref[rows, :] = g_
            ds = grads[6]
        ds_ref[...] = ds

    blk = pl.BlockSpec((P * L, R_WIDTH), lambda c: (steps - 1 - c, 0))
    per_chunk = pl.BlockSpec((P, H, N, N), lambda c: (steps - 1 - c, 0, 0, 0))
    return pl.pallas_call(
        body, name="scan_bwd", grid=(steps,), in_specs=[blk] * 6 + [per_chunk, per_chunk, blk, _HBM], out_specs=[blk] * 6 + [_HBM],
        out_shape=[jax.ShapeDtypeStruct((T, R_WIDTH), F32)] * 6 + [ride_shape],
        scratch_shapes=[pltpu.VMEM((H, N, N), F32)] + list(ride_sems),
        compiler_params=_cparams(("arbitrary",)),
    )(r, lw, k, v, a, b, states, inverses, dy, rider[0])


def _place():
    x, y, c = lax.axis_index("x"), lax.axis_index("y"), lax.axis_index("c")
    return x, y, c


def _gather_def(block):
    R, C = block.shape

    def phases(x_ref, out_ref, send_sems, recv_sems, local_sem):
        x, y, c = _place()
        me, sibling = (x, y, c), (x, y, 1 - c)
        chips = [(1 - x, y), (x, 1 - y), (1 - x, 1 - y)]

        def slot(px, py, pc):
            return out_ref.at[4 * px + 2 * py + pc]

        def copy(k, blk, to, src=None):
            return pltpu.make_async_remote_copy(
                src_ref=slot(*blk) if src is None else src, dst_ref=slot(*blk),
                send_sem=send_sems.at[k], recv_sem=recv_sems.at[k], device_id=to, device_id_type=_MESH_ID)

        mine = pltpu.make_async_copy(x_ref, slot(*me), local_sem)
        first = [copy(0, me, sibling, src=x_ref)]
        first += [copy(1 + j, me, (*chip, c), src=x_ref) for j, chip in enumerate(chips)]
        passed = [copy(4 + j, (*chip, c), sibling) for j, chip in enumerate(chips)]

        def begin():
            mine.start()
            for cp in first:
                cp.start()

        def forward():
            for j, chip in enumerate(chips):
                copy(1 + j, (*chip, c), me).wait_recv()
                passed[j].start()

        def finish():
            copy(0, sibling, me).wait_recv()
            for j, chip in enumerate(chips):
                copy(4 + j, (*chip, 1 - c), me).wait_recv()
            for cp in first + passed:
                cp.wait_send()
            mine.wait()

        return [begin, forward, finish]

    sems = [pltpu.SemaphoreType.DMA((7,)), pltpu.SemaphoreType.DMA((7,)), pltpu.SemaphoreType.DMA]
    return jax.ShapeDtypeStruct((_N_DEV, R, C), block.dtype), sems, phases


def _sibling_def(blocks):
    _, R, C = blocks.shape

    def phases(x_ref, out_ref, send_sems, recv_sems):
        x, y, c = _place()
        copies = [pltpu.make_async_remote_copy(
            src_ref=x_ref.at[2 * q + (1 - c)], dst_ref=out_ref.at[q], send_sem=send_sems.at[q], recv_sem=recv_sems.at[q],
            device_id=(x, y, 1 - c), device_id_type=_MESH_ID) for q in range(4)]

        def begin():
            for cp in copies:
                cp.start()

        def finish():
            for cp in copies:
                cp.wait()

        return [begin, finish]

    return jax.ShapeDtypeStruct((4, R, C), blocks.dtype), [pltpu.SemaphoreType.DMA((4,)), pltpu.SemaphoreType.DMA((4,))], phases


def _chips_def(partials):
    def phases(x_ref, out_ref, send_sems, recv_sems, local_sem):
        x, y, c = _place()
        my_chip = 2 * x + y
        mine = pltpu.make_async_copy(x_ref.at[my_chip], out_ref.at[my_chip], local_sem)
        copies = []
        for rel in range(1, 4):
            px, py = (1 - x if (rel >> 1) & 1 else x), (1 - y if rel & 1 else y)
            copies.append(pltpu.make_async_remote_copy(
                src_ref=x_ref.at[2 * px + py], dst_ref=out_ref.at[my_chip],
                send_sem=send_sems.at[rel - 1], recv_sem=recv_sems.at[rel - 1],
                device_id=(px, py, c), device_id_type=_MESH_ID))

        def begin():
            mine.start()
            for cp in copies:
                cp.start()

        def finish():
            for cp in copies:
                cp.wait()
            mine.wait()

        return [begin, finish]

    sems = [pltpu.SemaphoreType.DMA((3,)), pltpu.SemaphoreType.DMA((3,)), pltpu.SemaphoreType.DMA]
    return jax.ShapeDtypeStruct(partials.shape, partials.dtype), sems, phases


_HBM = pl.BlockSpec(memory_space=pltpu.HBM)


def _exchange(name, array, definition):
    out_shape, sems, phases = definition(array)

    def body(x_ref, out_ref, *sem_refs):
        for phase in phases(x_ref, out_ref, *sem_refs):
            phase()

    return pl.pallas_call(body, name=name, in_specs=[_HBM], out_specs=_HBM, out_shape=out_shape, scratch_shapes=sems)(array)


def _ride(rider, step, nsteps, x_ref, out_ref, sem_refs):
    array, definition, fractions = rider
    for phase, frac in zip(definition(array)[2](x_ref, out_ref, *sem_refs), fractions, strict=True):
        pl.when(step == min(int(frac * nsteps), nsteps - 1))(phase)


def _all_gather(name, block):
    return _exchange(name, block, _gather_def)


def _chip_partials(name, blocks, from_sibling, tile, out_dtype):
    _, R, C = blocks.shape

    def body(x_ref, s_ref, o_ref):
        c = lax.axis_index("c")
        for q in range(4):
            o_ref[q] = (x_ref[2 * q + c].astype(F32) + s_ref[q].astype(F32)).astype(o_ref.dtype)

    return pl.pallas_call(
        body, name=name, grid=(R // tile,),
        in_specs=[pl.BlockSpec((_N_DEV, tile, C), lambda i: (0, i, 0)), pl.BlockSpec((4, tile, C), lambda i: (0, i, 0))],
        out_specs=pl.BlockSpec((4, tile, C), lambda i: (0, i, 0)), out_shape=jax.ShapeDtypeStruct((4, R, C), out_dtype),
        compiler_params=_cparams(("parallel",)),
    )(blocks, from_sibling)


def _sum_leading(name, x, tile=None):
    n, R, C = x.shape
    tile = tile or _pick(R, (512, 256, 128, 64, 32, 16, 8))

    def body(x_ref, o_ref):
        acc = x_ref[0].astype(F32)
        for k in range(1, n):
            acc = acc + x_ref[k].astype(F32)
        o_ref[...] = acc

    return pl.pallas_call(
        body, name=name, grid=(R // tile,), in_specs=[pl.BlockSpec((n, tile, C), lambda i: (0, i, 0))],
        out_specs=pl.BlockSpec((tile, C), lambda i: (i, 0)), out_shape=jax.ShapeDtypeStruct((R, C), F32),
        compiler_params=_cparams(("parallel",)),
    )(x)


def _adamw_update(w_, g_, m_, v_):
    m2 = ADAM_B1 * m_ + (1.0 - ADAM_B1) * g_
    v2 = ADAM_B2 * v_ + (1.0 - ADAM_B2) * jnp.square(g_)
    m_hat = m2 / (1.0 - ADAM_B1 ** ADAM_STEP)
    v_hat = v2 / (1.0 - ADAM_B2 ** ADAM_STEP)
    delta = -ADAM_LR * (m_hat / (jnp.sqrt(v_hat) + ADAM_EPS) + ADAM_WD * w_)
    return delta, m2, v2


def _adamw(name, w, g, m, v):
    R, C = w.shape
    tile = _pick(R, (256, 128, 64, 32, 16, 8))
    outs, _ = _rowwise(name, lambda tv, bv: (list(_adamw_update(*tv)), []), [w, g, m, v], [], [(C, F32)] * 3, [], tile)
    return outs


def _adamw_many(name, ws, gs, ms, vs):
    n = len(ws)

    def body(*refs):
        ins, outs = refs[:4 * n], refs[4 * n:]
        for i in range(n):
            res = _adamw_update(ins[i][...], ins[n + i][...], ins[2 * n + i][...], ins[3 * n + i][...])
            for j in range(3):
                outs[j * n + i][...] = res[j]

    out_shape = [jax.ShapeDtypeStruct(w.shape, F32) for w in ws] * 3
    res = pl.pallas_call(body, name=name, out_shape=out_shape, compiler_params=_cparams())(*ws, *gs, *ms, *vs)
    return res[:n], res[n:2 * n], res[2 * n:]


def _pack_rows(arrs, lanes=128, row_mult=8):
    flat, places, off = [], [], 0
    for a_ in arrs:
        n = a_.size
        flat.append(a_.reshape(-1).astype(F32))
        places.append((off, n, a_.shape))
        off += n
    total = -(-off // (lanes * row_mult)) * (lanes * row_mult)
    if total > off:
        flat.append(jnp.zeros((total - off,), F32))
    return jnp.concatenate(flat).reshape(total // lanes, lanes), places


def _unpack_rows(packed, places):
    flat = packed.reshape(-1)
    return [flat[o:o + n].reshape(s) for (o, n, s) in places]


_WEIGHTS = ['w_ada', 'b_ada', 'w_in', 'b_in', 'g_ln_v', 'b_ln_v', 'w_spatial', 'b_spatial', 'mu_shift', 'w0', 'w_decay_up', 'a0',
            'w_aaa_up', 'w_gate_up', 'k_k', 'k_a', 'r_k', 'gn_gain', 'gn_bias', 'w_branch_a', 'w_branch_b', 'w_out', 'b_out',
            'ln1_g', 'ln1_b', 'w_ff1', 'b_ff1', 'w_ff2', 'b_ff2', 'ln2_g', 'ln2_b']
_BIG = {'w_ff1': (0, 512), 'w_ff2': (512, 512), 'w_out': (1024, 128), 'w_branch_a': (1152, 64), 'w_branch_b': (1216, 64), 'w_in': (1280, 640)}
_LATER_ROWS = 1280
_CUT_BY_COLS = ('w_ff1', 'w_in', 'w_branch_a', 'w_branch_b')
IN_SHARD = IN_COLS // _N_DEV
_LORA = {'w_decay_up': (0, LORA_W), 'w_aaa_up': (LORA_W, LORA_A), 'w_gate_up': (LORA_W + LORA_A, LORA_G)}
_COMM_DTYPE = jnp.bfloat16


def _pad_rows(a, rows):
    return jnp.pad(a, ((0, rows - a.shape[0]),) + ((0, 0),) * (a.ndim - 1))


def _pack_big(shards):
    blocks = []
    for n, (_, rows) in _BIG.items():
        a = shards[n].T if n in _CUT_BY_COLS else shards[n]
        blocks.append(_pad_rows(a.reshape(-1, D_MODEL), rows))
    return jnp.concatenate(blocks, axis=0)


def _unpack_big(block, like):
    out = {}
    for n, (r0, _) in _BIG.items():
        rr, cc = like[n].shape
        if n in _CUT_BY_COLS:
            out[n] = block[r0:r0 + rr * cc // D_MODEL].reshape(cc, rr).T
        else:
            out[n] = block[r0:r0 + rr]
    return out


def _to_padded(a, axis):
    g_end = 2 * G_WIDTH
    r_end = g_end + RW_USED
    take = lambda lo, hi: lax.slice_in_dim(a, lo, hi, axis=axis)
    zshape = list(a.shape)
    zshape[axis] = RW_COLS - RW_USED
    return jnp.concatenate([take(r_end, IN_COLS), take(g_end, r_end), jnp.zeros(zshape, a.dtype), take(0, g_end)], axis=axis)


def _from_padded(a, axis):
    take = lambda lo, hi: lax.slice_in_dim(a, lo, hi, axis=axis)
    return jnp.concatenate([take(2 * D_MODEL + RW_COLS, P_COLS), take(2 * D_MODEL, 2 * D_MODEL + RW_USED), take(0, 2 * D_MODEL)], axis=axis)


def _step(p, m, v, x, c, target):
    T = x.shape[0]
    xi, yi, ci = _place()
    me = 4 * xi + 2 * yi + ci
    tile = _pick(T, (512, 256))

    lane = jnp.arange(R_WIDTH)
    gsum = (lane[:128, None] // R_HEAD == lane[None, :128] // R_HEAD).astype(F32)
    expand = (jnp.arange(128)[:, None] == (lane[None, :] // (G_WIDTH // 8))).astype(F32)

    (c_act,), _ = _rowwise("silu_c", lambda tv, bv: ([tv[0] * _sigmoid(tv[0])], []), [c], [], [(D_MODEL, F32)], [], 1)
    small, places = _pack_rows([c_act, p['w_decay_up'], p['w_aaa_up'], p['w_gate_up']])
    small_all = _all_gather("gather_small", small)
    per_dev = [_unpack_rows(small_all[d], places) for d in range(_N_DEV)]
    c_act_all = _pad_rows(jnp.concatenate([pd[0] for pd in per_dev], axis=0), 16)
    lora_full = {n: jnp.concatenate([pd[i + 1] for pd in per_dev], axis=1) for i, n in enumerate(_LORA)}
    lora_pad = {n: jnp.zeros((LORA_PAD, R_WIDTH), F32).at[r0:r0 + nr].set(lora_full[n]) for n, (r0, nr) in _LORA.items()}

    big_names = list(_BIG)
    my_rows = _pack_big(p).astype(_MXU_DTYPE)
    b_in_p = _to_padded(p['b_in'], 1)
    mu_p = jnp.concatenate([p['mu_shift'], jnp.zeros((1, RW_COLS - RW_USED), F32)], axis=1)

    b_ada_mine = lax.dynamic_slice(p['b_ada'], (0, me * 768), (1, 768))
    mod_cols = _mm("ada_mod", c_act_all, p['w_ada'], "nn", bias=b_ada_mine)
    mod_all = _all_gather("gather_mod", mod_cols)
    mod = lax.dynamic_index_in_dim(mod_all, me, axis=1, keepdims=False).reshape(1, 6 * D_MODEL)
    sh1, sc1, gt1, sh2, sc2, gt2 = [mod[:, i * D_MODEL:(i + 1) * D_MODEL] for i in range(6)]

    (h,), _, w_in_all = _rowwise("modulate1", lambda tv, bv: (_seg_modulate(tv, bv), []), [x], [sc1, sh1], [(D_MODEL, _MXU_DTYPE)], [], tile,
                                 rider=(my_rows[_LATER_ROWS:], _gather_def, (0.0, 0.5, 1.0)))
    w_in_t = _to_padded(w_in_all[:, :IN_SHARD].reshape(IN_COLS, D_MODEL), 0)
    proj = _mm("in_proj", h, w_in_t, "nt", bias=b_in_p)
    ws = p['w_spatial']
    b_tg = jnp.zeros((G_CHUNK, 128), F32).at[:, :8].set(p['b_spatial'].T)
    gmlp_b = [p['g_ln_v'], p['b_ln_v'], ws, b_tg, expand]
    z_gmlp = (proj, 2 * G_WIDTH, 4)
    (ya,), _ = _rowwise("gmlp", lambda tv, bv: (_seg_gmlp(tv, bv), []), [z_gmlp], gmlp_b, [(G_WIDTH, F32)], [], G_CHUNK)
    z_rw = (proj, RW_COLS, 1)
    z_rw_halo = ("halo", proj, RW_COLS, 1)
    pre_b = [mu_p, p['w0'], lora_pad['w_decay_up'], p['a0'], lora_pad['w_aaa_up'], lora_pad['w_gate_up'], p['k_k'], p['k_a'], gsum]

    def pre_fwd(tv, bv):
        z_t, halo_t = tv
        return _seg_rwkv_pre([z_t, _shift_down(z_t, halo_t, pl.program_id(0) == 0)], bv), []

    pre_out, _ = _rowwise("rwkv_pre", pre_fwd, [z_rw, z_rw_halo], pre_b, [(R_WIDTH, F32)] * 7, [], tile)
    r_, lw_, k2_, v_, a_, b_, g_ = pre_out
    scan_in = (r_, lw_, k2_, v_, a_, b_)
    y_, states, inverses, later_all = _scan_fwd(*scan_in, rider=(my_rows[:_LATER_ROWS], _gather_def, (0.0, 0.875, 1.0)))

    def whole(n, rows):
        r0 = _BIG[n][0]
        return later_all[:, r0:r0 + rows].reshape(_N_DEV * rows, D_MODEL)

    w_ff1_t, w_ff2, w_out = whole('w_ff1', 512), whole('w_ff2', 512), whole('w_out', 128)
    w_ba_t = whole('w_branch_a', 64).reshape(D_MODEL, G_WIDTH)
    w_bb_t = whole('w_branch_b', 64).reshape(D_MODEL, R_WIDTH)
    post_b = [p['r_k'].reshape(1, R_WIDTH), p['gn_gain'], p['gn_bias'], gsum]
    (yb,), _ = _rowwise("rwkv_post", lambda tv, bv: (_seg_rwkv_post(tv, bv), []), [y_, r_, k2_, v_, g_], post_b, [(R_WIDTH, F32)], [], tile)
    pa = _mm("branch_a", ya, w_ba_t, "nt", out_dtype=_MXU_DTYPE)
    pb = _mm("branch_b", yb, w_bb_t, "nt", out_dtype=_MXU_DTYPE)
    gates = [(proj, D_MODEL, 0), (proj, D_MODEL, 1)]
    (merged,), _ = _rowwise("merge", lambda tv, bv: (_seg_merge(tv, bv), []), gates + [pa, pb], [], [(D_MODEL, _MXU_DTYPE)], [], tile)
    mix = _mm("out_proj", merged, w_out, "nn", bias=p['b_out'])
    mid_b = [gt1, p['ln1_g'], p['ln1_b'], sc2, sh2]
    (h1, h2in), _ = _rowwise("mid", lambda tv, bv: (_seg_mid(tv, bv), []), [x, mix], mid_b, [(D_MODEL, F32), (D_MODEL, _MXU_DTYPE)], [], tile)
    act = _mm("ff1", h2in, w_ff1_t, "nt", bias=p['b_ff1'], epi=lambda t: _seg_relu2([t], [])[0], epi_dtype=_MXU_DTYPE, raw=False)
    ff = _mm("ff2", act, w_ff2, "nn", bias=p['b_ff2'])

    def loss_fn(tv, bv):
        h1_t, ff_t, tgt = tv
        val, grads = jax.value_and_grad(lambda a0_, a1_, b0_, b1_, b2_: _seg_loss([a0_, a1_, tgt], [b0_, b1_, b2_]), argnums=(0, 1, 2, 3, 4))(h1_t, ff_t, *bv)
        return [grads[0], grads[1]], [grads[2], grads[3], grads[4], jnp.sum(grads[1], axis=0, keepdims=True), jnp.full((1, 128), val, F32)]

    (dh1_a, dff), (d_gt2, d_ln2_g, d_ln2_b, d_b_ff2, loss_row) = _rowwise(
        "loss", loss_fn, [h1, ff, target], [gt2, p['ln2_g'], p['ln2_b']], [(D_MODEL, F32), (D_MODEL, _MXU_DTYPE)], [(1, D_MODEL)] * 4 + [(1, 128)], tile)

    g = {}
    g['ln2_g'], g['ln2_b'], g['b_ff2'] = d_ln2_g, d_ln2_b, d_b_ff2
    gw = {}
    gw['w_ff2'] = _mm("g_w_ff2", act, dff, "tn", out_dtype=_COMM_DTYPE)
    df1, g['b_ff1'] = _mm("d_act", dff, w_ff2, "nt", beside=act, epi=lambda d_, a_: d_ * (2.0 * jnp.sqrt(a_.astype(F32))),
                          epi_dtype=_MXU_DTYPE, colsum=True)
    gw['w_ff1'] = _mm("g_w_ff1", df1, h2in, "tn", out_dtype=_COMM_DTYPE)
    dh2in = _mm("d_h2in", df1, w_ff1_t, "nn")
    (dx_a, dmix), (d_gt1, g['ln1_g'], g['ln1_b'], d_sc2, d_sh2), (g['b_out'],) = _rowwise_vjp(
        "mid_bwd", _seg_mid, [x, mix], mid_b, [dh1_a, dh2in], tile, [0, 1], [0, 1, 2, 3, 4], t_dtypes=[F32, _MXU_DTYPE], colsum=[1])
    gw['w_out'] = _mm("g_w_out", merged, dmix, "tn", out_dtype=_COMM_DTYPE)
    dmerged = _mm("d_merged", dmix, w_out, "nt", out_dtype=_MXU_DTYPE)
    (dproj, dpa, dpb), _, (cs_gates,) = _rowwise_vjp(
        "merge_bwd", _seg_merge, gates + [pa, pb], [], [dmerged], tile, [0, 1, 2, 3], [], t_dtypes=[_MXU_DTYPE] * 3, colsum=[0],
        finish=lambda dts, sc: [jnp.concatenate(dts[:2], axis=1), dts[2], dts[3]], out_widths=[2 * D_MODEL, D_MODEL, D_MODEL],
        into={0: (lax.empty((T, P_COLS), _MXU_DTYPE), 0)})
    gw['w_branch_a'] = _mm("g_w_branch_a", dpa, ya, "tn", out_dtype=_COMM_DTYPE)
    gw['w_branch_b'] = _mm("g_w_branch_b", dpb, yb, "tn", out_dtype=_COMM_DTYPE)
    dya = _mm("d_ya", dpa, w_ba_t, "nn")
    dyb = _mm("d_yb", dpb, w_bb_t, "nn")
    def send_rows(names):
        parts = []
        for n in names:
            per_dev = gw[n].reshape(_N_DEV, -1, D_MODEL)
            parts.append(jnp.pad(per_dev, ((0, 0), (0, _BIG[n][1] - per_dev.shape[1]), (0, 0))))
        return jnp.concatenate(parts, axis=1) if len(parts) > 1 else parts[0]

    send_early = send_rows(big_names[:-1])
    (dy, dr1, dk1, dv1, dg_), (d_r_k, g['gn_gain'], g['gn_bias']), _, sibling_early = _rowwise_vjp(
        "rwkv_post_bwd", _seg_rwkv_post, [y_, r_, k2_, v_, g_], post_b, [dyb], tile, [0, 1, 2, 3, 4], [0, 1, 2],
        rider=(send_early, _sibling_def, (0.0, 1.0)))
    g['r_k'] = d_r_k
    partials_early = _chip_partials("chip_partials_early", send_early, sibling_early, 128, _COMM_DTYPE)
    dr2, dlw, dk2, dv2, da, db, landed_early = _scan_bwd(*scan_in, states, inverses, dy, rider=(partials_early, _chips_def, (0.0, 1.0)))
    pre_tile = 256
    last_step = T // pre_tile - 1

    def pre_prep(prim):
        z_t, halo_t = prim
        return [z_t, _shift_down(z_t, halo_t, pl.program_id(0) == last_step)]

    def pre_finish(dts, sc):
        dz_direct, dprev = dts
        (row_after,) = sc

        @pl.when(pl.program_id(0) == 0)
        def _():
            row_after[...] = jnp.zeros_like(row_after)

        dz = dz_direct + _shift_up(dprev, row_after[...])
        row_after[...] = dprev[:_HALO]
        return [dz]

    small_names = [n for n in _WEIGHTS if n not in _BIG and n not in ('w_ada', 'b_ada')]
    (dproj,), (g['g_ln_v'], g['b_ln_v'], g['w_spatial'], d_b_tg), (cs_g,) = _rowwise_vjp(
        "gmlp_bwd", _seg_gmlp, [z_gmlp], gmlp_b, [dya], G_CHUNK, [0], [0, 1, 2, 3], t_dtypes=[_MXU_DTYPE], colsum=[0],
        into={0: (dproj, 4)})
    g['b_spatial'] = d_b_tg[:, :8].T
    names_1 = [n for n in small_names if n in g]
    packed_1, places_1 = _pack_rows([g[n] for n in names_1] + [loss_row], row_mult=256)
    (dproj,), (d_mu, g['w0'], d_wd, g['a0'], d_wa, d_wg, g['k_k'], g['k_a']), (cs_rw,), small_all_1 = _rowwise_vjp(
        "rwkv_pre_bwd", _seg_rwkv_pre, [z_rw, z_rw_halo], pre_b, [[dr1, dr2], dlw, [dk1, dk2], [dv1, dv2], da, db, dg_], pre_tile,
        [0, 1], [0, 1, 2, 3, 4, 5, 6, 7], t_dtypes=[_MXU_DTYPE], colsum=[0], prep=pre_prep, finish=pre_finish,
        out_widths=[RW_COLS], reverse=True, scratch=[(_HALO, RW_COLS)], into={0: (dproj, 1)},
        rider=(packed_1, _gather_def, (0.0, 0.6, 1.0)))
    g['mu_shift'] = d_mu[:, :RW_USED]
    for n, d_ in (('w_decay_up', d_wd), ('w_aaa_up', d_wa), ('w_gate_up', d_wg)):
        r0, nr = _LORA[n]
        g[n] = d_[r0:r0 + nr]
    g['b_in'] = _from_padded(jnp.concatenate([cs_gates, cs_rw, cs_g], axis=1), 1)
    names_2 = [n for n in small_names if n not in names_1]
    packed_2, places_2 = _pack_rows([g[n] for n in names_2], row_mult=256)
    gw_in_t, small_all_2 = _mm("g_w_in", dproj, h, "tn", out_dtype=_COMM_DTYPE, rider=(packed_2, _gather_def, (0.0, 0.6, 1.0)))
    gw['w_in'] = _from_padded(gw_in_t, 0)
    send_late = send_rows(big_names[-1:])
    sibling_late = _exchange("pair_exchange_late", send_late, _sibling_def)
    partials_late = _chip_partials("chip_partials_late", send_late, sibling_late, 128, _COMM_DTYPE)
    half = partials_late.shape[1] // 2
    dh, landed_late_a = _mm("d_h", dproj, w_in_t, "nn", rider=(partials_late[:, :half], _chips_def, (0.0, 1.0)))

    def mod1_bwd(tv, bv):
        x_t, dh_t, dxa_t = tv
        (sc,) = bv
        return [dxa_t + dh_t * (1.0 + sc)], [jnp.sum(dh_t * x_t, axis=0, keepdims=True), jnp.sum(dh_t, axis=0, keepdims=True)]

    (grad_x,), (d_sc1, d_sh1), landed_late_b = _rowwise(
        "modulate1_bwd", mod1_bwd, [x, dh, dx_a], [sc1], [(D_MODEL, F32)], [(1, D_MODEL)] * 2, tile,
        rider=(partials_late[:, half:], _chips_def, (0.0, 1.0)))
    landed_late = jnp.concatenate([landed_late_a, landed_late_b], axis=1)

    dmod = jnp.concatenate([d_sh1, d_sc1, d_gt1, d_sh2, d_sc2, d_gt2], axis=1).reshape(6 * D_MODEL // 128, 128)
    dmod_all = _all_gather("gather_dmod", dmod)
    g['b_ada'] = _sum_leading("sum_dmod", dmod_all).reshape(1, 6 * D_MODEL)
    dmod_mine = lax.dynamic_slice(dmod_all.reshape(_N_DEV, 6 * D_MODEL), (0, me * 768), (_N_DEV, 768))
    g_w_ada = _mm("g_w_ada", c_act_all, _pad_rows(dmod_mine, 16), "tn")

    sums_1 = _unpack_rows(_sum_leading("sum_small_1", small_all_1), places_1)
    loss = sums_1.pop()[0, 0]
    sums_2 = _unpack_rows(_sum_leading("sum_small_2", small_all_2), places_2)
    g.update(zip(names_1 + names_2, sums_1 + sums_2, strict=True))
    for n in _LORA:
        g[n] = lax.dynamic_slice(g[n], (0, me * R_HEAD), (g[n].shape[0], R_HEAD))
    g['w_ada'] = g_w_ada

    summed = jnp.concatenate([_sum_leading("sum_early", landed_early), _sum_leading("sum_late", landed_late)], axis=0)
    g.update(_unpack_big(summed, p))

    delta, new_m, new_v = {}, {}, {}
    own_call = ['w_ada'] + big_names
    for n in own_call:
        if n == 'w_in':
            r0 = _BIG[n][0]
            outs_t = _adamw("adamw_" + n, p[n].T, summed[r0:r0 + IN_SHARD], m[n].T, v[n].T)
            delta[n], new_m[n], new_v[n] = [o.T for o in outs_t]
        else:
            delta[n], new_m[n], new_v[n] = _adamw("adamw_" + n, p[n], g[n], m[n], v[n])
    rest = [n for n in _WEIGHTS if n not in own_call]
    outs = _adamw_many("adamw_rest", *[[d[n].reshape(p[n].shape) for n in rest] for d in (p, g, m, v)])
    for d, o in zip((delta, new_m, new_v), outs, strict=True):
        d.update(zip(rest, o, strict=True))
    return loss, grad_x, g, delta, new_m, new_v


def kernel(x, c, w_ada, b_ada, w_in, b_in, g_ln_v, b_ln_v, w_spatial, b_spatial, mu_shift, w0, w_decay_up, a0, w_aaa_up, w_gate_up, k_k, k_a, r_k, gn_gain, gn_bias, w_branch_a, w_branch_b, w_out, b_out, ln1_g, ln1_b, w_ff1, b_ff1, w_ff2, b_ff2, ln2_g, ln2_b, loss_target, m_w_ada, m_b_ada, m_w_in, m_b_in, m_g_ln_v, m_b_ln_v, m_w_spatial, m_b_spatial, m_mu_shift, m_w0, m_w_decay_up, m_a0, m_w_aaa_up, m_w_gate_up, m_k_k, m_k_a, m_r_k, m_gn_gain, m_gn_bias, m_w_branch_a, m_w_branch_b, m_w_out, m_b_out, m_ln1_g, m_ln1_b, m_w_ff1, m_b_ff1, m_w_ff2, m_b_ff2, m_ln2_g, m_ln2_b, v_w_ada, v_b_ada, v_w_in, v_b_in, v_g_ln_v, v_b_ln_v, v_w_spatial, v_b_spatial, v_mu_shift, v_w0, v_w_decay_up, v_a0, v_w_aaa_up, v_w_gate_up, v_k_k, v_k_a, v_r_k, v_gn_gain, v_gn_bias, v_w_branch_a, v_w_branch_b, v_w_out, v_b_out, v_ln1_g, v_ln1_b, v_w_ff1, v_b_ff1, v_w_ff2, v_b_ff2, v_ln2_g, v_ln2_b):
    given = dict(locals())
    shapes = {n: given[n].shape for n in _WEIGHTS}
    def two_d(a_):
        a_ = a_[0]
        return a_.reshape(1, -1) if a_.ndim == 1 else a_
    p = {n: two_d(given[n]) for n in _WEIGHTS}
    m = {n: two_d(given["m_" + n]) for n in _WEIGHTS}
    v = {n: two_d(given["v_" + n]) for n in _WEIGHTS}
    loss, grad_x, g, delta, new_m, new_v = _step(p, m, v, x[0], c, loss_target[0])
    outs = [loss, grad_x[None]]
    for d in (g, delta, new_m, new_v):
        outs += [d[n].reshape(shapes[n]) for n in _WEIGHTS]
    return tuple(outs)
```

```python
import functools

import jax
import jax.numpy as jnp
from jax import lax
from jax.experimental import pallas as pl
from jax.experimental.pallas import tpu as pltpu

F32 = jnp.float32
_MXU_DTYPE = jnp.bfloat16
_HI = lax.Precision.HIGHEST
_VMEM_LIMIT = 48 * 1024 * 1024
_MESH_ID = pl.DeviceIdType.MESH
_N_DEV = 8

D_MODEL = 1024
G_WIDTH = 512
G_CHUNK = 128
R_WIDTH = 512
R_HEADS = 8
R_HEAD = 64
LORA_W, LORA_A, LORA_G = 32, 32, 96
D_FF = 4096
ALPHA = 2.0 ** 0.25
LN_EPS = 1e-5
GN_EPS = 64e-5
SCAN_CHUNK = 64
SCAN_PER_STEP = 4
ADAM_LR, ADAM_B1, ADAM_B2, ADAM_EPS, ADAM_WD, ADAM_STEP = 0.001, 0.9, 0.999, 1e-08, 0.01, 10

P_COLS = 5120
RW_COLS = 2048
RW_USED = 3 * R_WIDTH + LORA_W + LORA_A + LORA_G
LORA_PAD = 256
IN_COLS = 2 * G_WIDTH + RW_USED + 2 * D_MODEL


def _cparams(sem=None, **kw):
    if sem is not None:
        kw["dimension_semantics"] = sem
    return pltpu.CompilerParams(vmem_limit_bytes=_VMEM_LIMIT, **kw)


def _dot(a, b, dims=(((1,), (0,)), ((), ())), hi=False):
    if hi:
        return lax.dot_general(a.astype(F32), b.astype(F32), dims, precision=_HI, preferred_element_type=F32)
    return lax.dot_general(a.astype(_MXU_DTYPE), b.astype(_MXU_DTYPE), dims, preferred_element_type=F32)


_NN = (((1,), (0,)), ((), ()))
_NT = (((1,), (1,)), ((), ()))
_TN = (((0,), (0,)), ((), ()))


def _pick(n, pref):
    for t in pref:
        if n % t == 0:
            return t
    return n


def _mm(name, a, b, mode, bias=None, out_dtype=F32, epi=None, epi_dtype=None, raw=True, beside=None, colsum=False, rider=None,
        tm=None, tn=None, tk=None):
    if mode == "nn":
        (M, K), (_, N) = a.shape, b.shape
    elif mode == "nt":
        (M, K), (N, _) = a.shape, b.shape
    else:
        (K, M), (_, N) = a.shape, b.shape
    tm = tm or _pick(M, (2048, 1280, 1024, 512, 256, 128, 64, 32, 16, 8))
    tn = tn or _pick(N, (1024, 512, 640, 384, 256, 128))
    tk = tk or _pick(K, (1024, 512, 256, 128))
    nk = K // tk
    dims = {"nn": _NN, "nt": _NT, "tn": _TN}[mode]
    a_spec = pl.BlockSpec((tk, tm), lambda i, j, k: (k, i)) if mode == "tn" else pl.BlockSpec((tm, tk), lambda i, j, k: (i, k))
    b_spec = pl.BlockSpec((tn, tk), lambda i, j, k: (j, k)) if mode == "nt" else pl.BlockSpec((tk, tn), lambda i, j, k: (k, j))
    o_spec = pl.BlockSpec((tm, tn), lambda i, j, k: (i, j))
    has_bias, has_beside = bias is not None, beside is not None
    two = epi is not None and not has_beside and raw
    only_epi = epi is not None and not has_beside and not raw
    grid = (M // tm, N // tn, nk)
    ride_shape, ride_sems, _ = rider[1](rider[0]) if rider else (None, [], None)

    def body(*refs):
        refs = list(refs)
        n_in = 2 + has_bias + has_beside
        if rider:
            sem_refs = [refs.pop() for _ in ride_sems][::-1]
            ride_out = refs.pop(n_in + 1 + 1 + two)
            ride_in = refs.pop(n_in)
            step = (pl.program_id(0) * grid[1] + pl.program_id(1)) * grid[2] + pl.program_id(2)
            _ride(rider, step, grid[0] * grid[1] * grid[2], ride_in, ride_out, sem_refs)
        a_ref, b_ref = refs[0], refs[1]
        bias_ref = refs[2] if has_bias else None
        beside_ref = refs[n_in - 1] if has_beside else None
        outs = refs[n_in:]
        o_ref, acc_ref = outs[0], outs[-1]
        k = pl.program_id(2)

        @pl.when(k == 0)
        def _():
            acc_ref[...] = jnp.zeros_like(acc_ref)

        acc_ref[...] += _dot(a_ref[...], b_ref[...], dims)

        @pl.when(k == nk - 1)
        def _():
            res = acc_ref[...]
            if has_bias:
                res = res + bias_ref[...]
            if has_beside:
                val = epi(res, beside_ref[...])
                o_ref[...] = val.astype(o_ref.dtype)
                if colsum:
                    outs[1][...] = jnp.broadcast_to(jnp.sum(val, axis=0, keepdims=True), outs[1].shape)
            elif only_epi:
                o_ref[...] = epi(res).astype(o_ref.dtype)
            else:
                o_ref[...] = res.astype(o_ref.dtype)
                if two:
                    outs[1][...] = epi(res).astype(outs[1].dtype)

    in_specs = [a_spec, b_spec]
    args = [a, b]
    if has_bias:
        in_specs.append(pl.BlockSpec((1, tn), lambda i, j, k: (0, j)))
        args.append(bias)
    if has_beside:
        in_specs.append(o_spec)
        args.append(beside)
    out_shape = [jax.ShapeDtypeStruct((M, N), epi_dtype if (has_beside or only_epi) else out_dtype)]
    out_specs = [o_spec]
    if two:
        out_shape.append(jax.ShapeDtypeStruct((M, N), epi_dtype))
        out_specs.append(o_spec)
    if colsum:
        assert has_beside and not rider
        out_shape.append(jax.ShapeDtypeStruct((grid[0] * 8, N), F32))
        out_specs.append(pl.BlockSpec((8, tn), lambda i, j, k: (i, j)))
    if rider:
        in_specs.append(_HBM)
        args.append(rider[0])
        out_shape.append(ride_shape)
        out_specs.append(_HBM)
    res = pl.pallas_call(
        body, name=name, grid=grid, in_specs=in_specs, out_specs=out_specs, out_shape=out_shape,
        scratch_shapes=[pltpu.VMEM((tm, tn), F32)] + list(ride_sems),
        compiler_params=_cparams(("arbitrary",) * 3 if rider else ("parallel", "parallel", "arbitrary")),
    )(*args)
    if colsum:
        return res[0], _sum_leading(name + "_colsum", res[1].reshape(grid[0], 8, N))[0:1]
    return res if (two or rider) else res[0]


_HALO = 8


def _rowwise(name, fn, tiled, bcast, tiled_out, red_out, tile, reverse=False, scratch=(), rider=None):
    tiled = [t if isinstance(t, tuple) else (t, t.shape[1], 0) for t in tiled]
    T = next(t[0] for t in tiled if not isinstance(t[0], str)).shape[0]
    n = T // tile
    nt, nb, nto, nsc, nro = len(tiled), len(bcast), len(tiled_out), len(scratch), len(red_out)
    ride_shape, ride_sems, _ = rider[1](rider[0]) if rider else (None, [], None)
    nr = 1 if rider else 0
    into = [(j, t[2], t[3]) for j, t in enumerate(tiled_out) if len(t) == 4]
    na = len(into)

    def row_block(i):
        return n - 1 - i if reverse else i

    def body(*refs):
        i2 = nt + nb
        o0 = i2 + nr + na
        o1, o2 = o0 + nto, o0 + nto + nro
        s0 = o2 + nr
        if rider:
            _ride(rider, pl.program_id(0), n, refs[i2], refs[o2], refs[s0 + nsc:])
        t_refs, b_refs = refs[:nt], refs[nt:i2]
        to_refs, ro_refs = refs[o0:o1], refs[o1:o2]
        extra = (list(refs[s0:s0 + nsc]),) if nsc else ()
        touts, routs = fn([r[...] for r in t_refs], [r[...] for r in b_refs], *extra)
        for r, v in zip(to_refs, touts, strict=True):
            r[...] = v.astype(r.dtype)
        if ro_refs:
            i = pl.program_id(0)

            @pl.when(i == 0)
            def _():
                for r, v in zip(ro_refs, routs, strict=True):
                    r[...] = v.astype(F32)

            @pl.when(i > 0)
            def _():
                for r, v in zip(ro_refs, routs, strict=True):
                    r[...] += v.astype(F32)

    def whole(shape):
        nd = len(shape)
        return pl.BlockSpec(tuple(shape), lambda i: (0,) * nd)

    per_tile = tile // _HALO
    in_specs, arrays = [], []
    for t in tiled:
        if isinstance(t[0], str):
            _, arr, w, cb = t
            in_specs.append(pl.BlockSpec((_HALO, w), functools.partial(lambda i, cb: (jnp.maximum(row_block(i) * per_tile - 1, 0), cb), cb=cb)))
        else:
            arr, w, cb = t
            in_specs.append(pl.BlockSpec((tile, w), functools.partial(lambda i, cb: (row_block(i), cb), cb=cb)))
        arrays.append(arr)
    in_specs += [whole(b.shape) for b in bcast]
    out_specs, out_shape = [], []
    for t in tiled_out:
        cb = t[3] if len(t) == 4 else 0
        out_specs.append(pl.BlockSpec((tile, t[0]), functools.partial(lambda i, cb: (row_block(i), cb), cb=cb)))
        out_shape.append(jax.ShapeDtypeStruct(t[2].shape if len(t) == 4 else (T, t[0]), t[1]))
    out_specs += [whole(s) for s in red_out]
    out_shape += [jax.ShapeDtypeStruct(tuple(s), F32) for s in red_out]
    ride_args = []
    if rider:
        in_specs.append(_HBM)
        out_specs.append(_HBM)
        out_shape.append(ride_shape)
        ride_args = [rider[0]]
    in_specs += [pl.BlockSpec(memory_space=pl.ANY)] * na
    aliases = {nt + nb + nr + k: j for k, (j, _, _) in enumerate(into)}
    res = pl.pallas_call(
        body, name=name, grid=(n,), in_specs=in_specs, out_specs=out_specs, out_shape=out_shape,
        scratch_shapes=[pltpu.VMEM(tuple(s), F32) for s in scratch] + list(ride_sems),
        input_output_aliases=aliases, compiler_params=_cparams(("arbitrary",)),
    )(*arrays, *bcast, *ride_args, *[buf for (_, buf, _) in into])
    if rider:
        return list(res[:nto]), list(res[nto:nto + nro]), res[-1]
    return list(res[:nto]), list(res[nto:])


def _rowwise_vjp(name, f, tiled, bcast, cts, tile, wrt_t, wrt_b, t_dtypes=None, colsum=(), prep=None, finish=None,
                 out_widths=None, reverse=False, scratch=(), rider=None, into=None):
    tiled = [t if isinstance(t, tuple) else (t, t.shape[1], 0) for t in tiled]
    npr = len(tiled)
    t_dtypes = t_dtypes or [F32] * len(wrt_t)
    groups = [c if isinstance(c, list) else [c] for c in cts]
    cts = [a_ for grp in groups for a_ in grp]

    def fn(tv, bv, sc=None):
        prim, flat_ct = tv[:npr], list(tv[npr:])
        if prep is not None:
            prim = prep(prim)
        ct = []
        for grp in groups:
            parts = [flat_ct.pop(0).astype(F32) for _ in grp]
            ct.append(functools.reduce(lambda p_, q_: p_ + q_, parts))

        def g(dt_vals, db_vals):
            full_t, full_b = list(prim), list(bv)
            for i, v in zip(wrt_t, dt_vals, strict=True):
                full_t[i] = v
            for j, v in zip(wrt_b, db_vals, strict=True):
                full_b[j] = v
            return f(full_t, full_b)

        outs, pull = jax.vjp(g, [prim[i].astype(F32) for i in wrt_t], [bv[j] for j in wrt_b])
        dts, dbs = pull([c.astype(o.dtype) for c, o in zip(ct, outs, strict=True)])
        if finish is not None:
            dts = finish(dts, sc)
        sums = [jnp.sum(dts[i].astype(F32), axis=0, keepdims=True) for i in colsum]
        return dts, list(dbs) + sums

    widths = out_widths or [tiled[i][1] for i in wrt_t]
    tiled_out = [(w, dt) + tuple((into or {}).get(j, ())) for j, (w, dt) in enumerate(zip(widths, t_dtypes, strict=True))]
    red_out = [bcast[j].shape for j in wrt_b] + [(1, widths[i]) for i in colsum]
    res = _rowwise(name, fn, tiled + list(cts), bcast, tiled_out, red_out, tile, reverse=reverse, scratch=scratch, rider=rider)
    dts, reds = res[0], res[1]
    nb = len(wrt_b)
    return (dts, reds[:nb], reds[nb:]) + tuple(res[2:])


def _layer_norm(x, g, b, eps):
    mu = jnp.mean(x, axis=-1, keepdims=True)
    xc = x - mu
    var = jnp.mean(xc * xc, axis=-1, keepdims=True)
    return xc * lax.rsqrt(var + eps) * g + b


def _gelu_tanh(x):
    return 0.5 * x * (1.0 + jnp.tanh(0.7978845608028654 * (x + 0.044715 * (x * x * x))))


def _sigmoid(x):
    return 1.0 / (1.0 + jnp.exp(-x))


def _seg_modulate(tv, bv):
    (x,), (sc, sh) = tv, bv
    return [x * (1.0 + sc) + sh]


def _seg_gmlp(tv, bv):
    (z,), (g_ln, b_ln, ws, b_tg, expand) = tv, bv
    bias_full = _dot(b_tg, expand, hi=True)
    zz = _gelu_tanh(z)
    u, v = zz[:, :G_WIDTH], zz[:, G_WIDTH:]
    v = _layer_norm(v, g_ln, b_ln, LN_EPS)
    row = lax.broadcasted_iota(jnp.int32, (G_CHUNK, G_CHUNK), 0)
    col = lax.broadcasted_iota(jnp.int32, (G_CHUNK, G_CHUNK), 1)
    causal = col <= row
    first_group = lax.broadcasted_iota(jnp.int32, (G_CHUNK, 128), 1) < 64
    parts = []
    for p in range(4):
        vp = v[:, 128 * p:128 * (p + 1)]
        s_even = _dot(jnp.where(causal, ws[2 * p], 0.0), vp)
        s_odd = _dot(jnp.where(causal, ws[2 * p + 1], 0.0), vp)
        parts.append(jnp.where(first_group, s_even, s_odd))
    s = jnp.concatenate(parts, axis=1) + bias_full
    return [u * s]


def _split2(x):
    hi = x.astype(_MXU_DTYPE)
    return hi, (x - hi.astype(F32)).astype(_MXU_DTYPE)


@jax.custom_vjp
def _group_sum(x, ones_blocks):
    hi, lo = _split2(x)
    w = ones_blocks.shape[0]
    return jnp.concatenate([_dot(hi[:, j:j + w], ones_blocks) + _dot(lo[:, j:j + w], ones_blocks) for j in range(0, x.shape[1], w)], axis=1)


def _group_sum_fwd(x, ones_blocks):
    return _group_sum(x, ones_blocks), ones_blocks


def _group_sum_bwd(ones_blocks, ct):
    return _group_sum(ct, ones_blocks), jnp.zeros_like(ones_blocks)


_group_sum.defvjp(_group_sum_fwd, _group_sum_bwd)


def _shift_down(z, halo, is_first):
    _, W = z.shape
    rolled = pltpu.roll(z, 1, 0)
    before = jnp.where(is_first, 0.0, pltpu.roll(halo, 1, 0))
    top_row = lax.broadcasted_iota(jnp.int32, (_HALO, W), 0) == 0
    return jnp.concatenate([jnp.where(top_row, before, rolled[:_HALO]), rolled[_HALO:]], axis=0)


def _shift_up(d, after):
    tile, W = d.shape
    rolled = pltpu.roll(d, tile - 1, 0)
    last_row = lax.broadcasted_iota(jnp.int32, (_HALO, W), 0) == _HALO - 1
    bottom = jnp.where(last_row, pltpu.roll(after, _HALO - 1, 0), rolled[tile - _HALO:])
    return jnp.concatenate([rolled[:tile - _HALO], bottom], axis=0)


def _seg_rwkv_pre(tv, bv):
    (z, prev), (mu, w0, wd, a0, wa, wg, k_k, k_a, gsum) = tv, bv
    zs = z + (prev - z) * mu
    r, k, v = zs[:, 0:512], zs[:, 512:1024], zs[:, 1024:1536]
    zl = zs[:, 3 * R_WIDTH:3 * R_WIDTH + LORA_PAD]
    x = w0 + _dot(jnp.tanh(zl), wd)
    softplus = jnp.maximum(-x, 0.0) + jnp.log(1.0 + jnp.exp(-jnp.abs(x)))
    lw = -jnp.exp(-softplus - 0.5)
    a = _sigmoid(a0 + _dot(zl, wa))
    g = _dot(_sigmoid(zl), wg)
    kk = k * k_k
    nrm = jnp.sqrt(_group_sum(kk * kk, gsum))
    kk = kk / jnp.maximum(nrm, 1e-12)
    k2 = k * (1.0 + (a - 1.0) * k_a)
    return [r, lw, k2, v, -kk, kk * a, g]


def _seg_rwkv_post(tv, bv):
    (y, r, k2, v, g), (r_k, gain, bias, gsum) = tv, bv
    mu = _group_sum(y, gsum) * (1.0 / R_HEAD)
    yc = y - mu
    var = _group_sum(yc * yc, gsum) * (1.0 / R_HEAD)
    yn = yc * lax.rsqrt(var + GN_EPS) * gain + bias
    bonus = _group_sum(r * k2 * r_k, gsum) * v
    return [(yn + bonus) * g]


def _seg_merge(tv, bv):
    (ga, gb, pa, pb), () = tv, bv
    return [_sigmoid(ga) * pa + _sigmoid(gb) * pb]


def _seg_mid(tv, bv):
    (x, mix), (gt1, g1, b1, sc2, sh2) = tv, bv
    h1 = _layer_norm(ALPHA * x + gt1 * mix, g1, b1, LN_EPS)
    return [h1, h1 * (1.0 + sc2) + sh2]


def _seg_relu2(tv, bv):
    (f1,), () = tv, bv
    return [jnp.square(jnp.maximum(f1, 0.0))]


def _seg_loss(tv, bv):
    (h1, ff, target), (gt2, g2, b2) = tv, bv
    out = _layer_norm(ALPHA * h1 + gt2 * ff, g2, b2, LN_EPS)
    err = jnp.square(out - target)
    return 0.5 * jnp.sum(jnp.mean(err, axis=-1))


_BNN = (((2,), (1,)), ((0,), (0,)))
_BNT = (((2,), (2,)), ((0,), (0,)))
_BTN = (((1,), (1,)), ((0,), (0,)))


def _tri_dot(x, dims):
    L = x.shape[0]
    tri = (lax.broadcasted_iota(jnp.int32, (L, L), 1) <= lax.broadcasted_iota(jnp.int32, (L, L), 0)).astype(F32)
    hi, lo = _split2(x)
    return _dot(tri, hi, dims) + _dot(tri, lo, dims)


@jax.custom_vjp
def _running_sum(x):
    return _tri_dot(x, _NN)


_running_sum.defvjp(lambda x: (_tri_dot(x, _NN), None), lambda _, ct: (_tri_dot(ct, _TN),))


def _cut_heads(x):
    return jnp.stack([x[:, R_HEAD * h:R_HEAD * (h + 1)] for h in range(R_HEADS)])


def _join_heads(x):
    return jnp.concatenate([x[h] for h in range(R_HEADS)], axis=1)


@jax.custom_vjp
def _split_heads(x):
    return _cut_heads(x)


@jax.custom_vjp
def _merge_heads(x):
    return _join_heads(x)


_split_heads.defvjp(lambda x: (_cut_heads(x), None), lambda _, ct: (_join_heads(ct),))
_merge_heads.defvjp(lambda x: (_join_heads(x), None), lambda _, ct: (_cut_heads(ct),))


def _inverse_pullback(inv, ct):
    return _dot(_dot(inv, ct, _BTN), inv, _BNT)


@jax.custom_vjp
def _unit_lower_inverse(n_mat):
    H, L, _ = n_mat.shape
    eye = lax.broadcasted_iota(jnp.int32, (H, L, L), 1) == lax.broadcasted_iota(jnp.int32, (H, L, L), 2)
    inv = jnp.where(eye, 1.0, 0.0) + n_mat
    pw = n_mat
    n = 2
    while n < L:
        pw = _dot(pw, pw, _BNN)
        inv = inv + _dot(inv, pw, _BNN)
        n *= 2
    return inv


def _unit_lower_inverse_fwd(n_mat):
    inv = _unit_lower_inverse(n_mat)
    return inv, inv


_unit_lower_inverse.defvjp(_unit_lower_inverse_fwd, lambda inv, ct: (_inverse_pullback(inv, ct),))


@jax.custom_vjp
def _known_inverse(n_mat, inv):
    return inv


_known_inverse.defvjp(lambda n_mat, inv: (inv, inv), lambda inv, ct: (_inverse_pullback(inv, ct), jnp.zeros_like(inv)))


def _scan_chunk(r, lw, k, v, a, b, s0, inv=None, with_inverse=False):
    L, H = r.shape[0], R_HEADS
    cs = _running_sum(lw)
    cs_end = cs[L - 1:L, :]
    p, p_inv, to_end = jnp.exp(cs), jnp.exp(-cs), jnp.exp(cs_end - cs)
    at, bt, kt, rt = [_split_heads(t) for t in (a * jnp.exp(cs - lw), b * p_inv, k * p_inv, r * p)]
    b_end, k_end, v = [_split_heads(t) for t in (b * to_end, k * to_end, v)]
    row = lax.broadcasted_iota(jnp.int32, (H, L, L), 1)
    col = lax.broadcasted_iota(jnp.int32, (H, L, L), 2)
    incl, strict = col <= row, col < row
    a_ab = jnp.where(strict, _dot(at, bt, _BNT), 0.0)
    a_ak = jnp.where(strict, _dot(at, kt, _BNT), 0.0)
    a_rb = jnp.where(incl, _dot(rt, bt, _BNT), 0.0)
    a_rk = jnp.where(incl, _dot(rt, kt, _BNT), 0.0)
    inv = _unit_lower_inverse(a_ab) if inv is None else _known_inverse(a_ab, inv)
    u = _dot(inv, _dot(at, s0, _BNT) + _dot(a_ak, v, _BNN), _BNN)
    y = _merge_heads(_dot(rt, s0, _BNT) + _dot(a_rb, u, _BNN) + _dot(a_rk, v, _BNN))
    s1 = s0 * _split_heads(jnp.exp(cs_end)) + _dot(u, b_end, _BTN) + _dot(v, k_end, _BTN)
    return (y, s1, inv) if with_inverse else (y, s1)


def _scan_fwd(r, lw, k, v, a, b, rider):
    T = r.shape[0]
    H, N, L, P = R_HEADS, R_HEAD, SCAN_CHUNK, SCAN_PER_STEP
    nc = T // L
    steps = nc // P
    ride_shape, ride_sems, _ = rider[1](rider[0])

    def body(r_ref, lw_ref, k_ref, v_ref, a_ref, b_ref, ride_in, y_ref, st_ref, inv_ref, ride_out, s_ref, *sem_refs):
        _ride(rider, pl.program_id(0), steps, ride_in, ride_out, sem_refs)

        @pl.when(pl.program_id(0) == 0)
        def _():
            s_ref[...] = jnp.zeros_like(s_ref)

        s0 = s_ref[...]
        for j in range(P):
            rows = pl.ds(j * L, L)
            st_ref[j] = s0
            y, s0, inv = _scan_chunk(*[t[rows, :] for t in (r_ref, lw_ref, k_ref, v_ref, a_ref, b_ref)], s0, with_inverse=True)
            y_ref[rows, :] = y
            inv_ref[j] = inv
        s_ref[...] = s0

    blk = pl.BlockSpec((P * L, R_WIDTH), lambda c: (c, 0))
    per_chunk = pl.BlockSpec((P, H, N, N), lambda c: (c, 0, 0, 0))
    return pl.pallas_call(
        body, name="scan_fwd", grid=(steps,), in_specs=[blk] * 6 + [_HBM], out_specs=[blk, per_chunk, per_chunk, _HBM],
        out_shape=[jax.ShapeDtypeStruct((T, R_WIDTH), F32)] + [jax.ShapeDtypeStruct((nc, H, N, N), F32)] * 2 + [ride_shape],
        scratch_shapes=[pltpu.VMEM((H, N, N), F32)] + list(ride_sems),
        compiler_params=_cparams(("arbitrary",)),
    )(r, lw, k, v, a, b, rider[0])


def _scan_bwd(r, lw, k, v, a, b, states, inverses, dy, rider):
    T = r.shape[0]
    H, N, L, P = R_HEADS, R_HEAD, SCAN_CHUNK, SCAN_PER_STEP
    nc = T // L
    steps = nc // P
    ride_shape, ride_sems, _ = rider[1](rider[0])

    def body(r_ref, lw_ref, k_ref, v_ref, a_ref, b_ref, st_ref, inv_ref, dy_ref, ride_in,
             dr_ref, dlw_ref, dk_ref, dv_ref, da_ref, db_ref, ride_out, ds_ref, *sem_refs):
        _ride(rider, pl.program_id(0), steps, ride_in, ride_out, sem_refs)

        @pl.when(pl.program_id(0) == 0)
        def _():
            ds_ref[...] = jnp.zeros_like(ds_ref)

        ds = ds_ref[...]
        for j in reversed(range(P)):
            rows = pl.ds(j * L, L)
            args = [t[rows, :] for t in (r_ref, lw_ref, k_ref, v_ref, a_ref, b_ref)] + [st_ref[j]]
            inv = inv_ref[j]
            _, pull = jax.vjp(lambda *xs, inv=inv: _scan_chunk(*xs, inv=inv), *args)
            grads = pull((dy_ref[rows, :], ds))
            for o_ref, g_ in zip((dr_ref, dlw_ref, dk_ref, dv_ref, da_ref, db_ref), grads[:6], strict=True):
                o_ref[rows, :] = g_
            ds = grads[6]
        ds_ref[...] = ds

    blk = pl.BlockSpec((P * L, R_WIDTH), lambda c: (steps - 1 - c, 0))
    per_chunk = pl.BlockSpec((P, H, N, N), lambda c: (steps - 1 - c, 0, 0, 0))
    return pl.pallas_call(
        body, name="scan_bwd", grid=(steps,), in_specs=[blk] * 6 + [per_chunk, per_chunk, blk, _HBM], out_specs=[blk] * 6 + [_HBM],
        out_shape=[jax.ShapeDtypeStruct((T, R_WIDTH), F32)] * 6 + [ride_shape],
        scratch_shapes=[pltpu.VMEM((H, N, N), F32)] + list(ride_sems),
        compiler_params=_cparams(("arbitrary",)),
    )(r, lw, k, v, a, b, states, inverses, dy, rider[0])


def _place():
    x, y, c = lax.axis_index("x"), lax.axis_index("y"), lax.axis_index("c")
    return x, y, c


def _gather_def(block):
    R, C = block.shape

    def phases(x_ref, out_ref, send_sems, recv_sems, local_sem):
        x, y, c = _place()
        me, sibling = (x, y, c), (x, y, 1 - c)
        chips = [(1 - x, y), (x, 1 - y), (1 - x, 1 - y)]

        def slot(px, py, pc):
            return out_ref.at[4 * px + 2 * py + pc]

        def copy(k, blk, to, src=None):
            return pltpu.make_async_remote_copy(
                src_ref=slot(*blk) if src is None else src, dst_ref=slot(*blk),
                send_sem=send_sems.at[k], recv_sem=recv_sems.at[k], device_id=to, device_id_type=_MESH_ID)

        mine = pltpu.make_async_copy(x_ref, slot(*me), local_sem)
        first = [copy(0, me, sibling, src=x_ref)]
        first += [copy(1 + j, me, (*chip, c), src=x_ref) for j, chip in enumerate(chips)]
        passed = [copy(4 + j, (*chip, c), sibling) for j, chip in enumerate(chips)]

        def begin():
            mine.start()
            for cp in first:
                cp.start()

        def forward():
            for j, chip in enumerate(chips):
                copy(1 + j, (*chip, c), me).wait_recv()
                passed[j].start()

        def finish():
            copy(0, sibling, me).wait_recv()
            for j, chip in enumerate(chips):
                copy(4 + j, (*chip, 1 - c), me).wait_recv()
            for cp in first + passed:
                cp.wait_send()
            mine.wait()

        return [begin, forward, finish]

    sems = [pltpu.SemaphoreType.DMA((7,)), pltpu.SemaphoreType.DMA((7,)), pltpu.SemaphoreType.DMA]
    return jax.ShapeDtypeStruct((_N_DEV, R, C), block.dtype), sems, phases


def _sibling_def(blocks):
    _, R, C = blocks.shape

    def phases(x_ref, out_ref, send_sems, recv_sems):
        x, y, c = _place()
        copies = [pltpu.make_async_remote_copy(
            src_ref=x_ref.at[2 * q + (1 - c)], dst_ref=out_ref.at[q], send_sem=send_sems.at[q], recv_sem=recv_sems.at[q],
            device_id=(x, y, 1 - c), device_id_type=_MESH_ID) for q in range(4)]

        def begin():
            for cp in copies:
                cp.start()

        def finish():
            for cp in copies:
                cp.wait()

        return [begin, finish]

    return jax.ShapeDtypeStruct((4, R, C), blocks.dtype), [pltpu.SemaphoreType.DMA((4,)), pltpu.SemaphoreType.DMA((4,))], phases


def _chips_def(partials):
    def phases(x_ref, out_ref, send_sems, recv_sems, local_sem):
        x, y, c = _place()
        my_chip = 2 * x + y
        mine = pltpu.make_async_copy(x_ref.at[my_chip], out_ref.at[my_chip], local_sem)
        copies = []
        for rel in range(1, 4):
            px, py = (1 - x if (rel >> 1) & 1 else x), (1 - y if rel & 1 else y)
            copies.append(pltpu.make_async_remote_copy(
                src_ref=x_ref.at[2 * px + py], dst_ref=out_ref.at[my_chip],
                send_sem=send_sems.at[rel - 1], recv_sem=recv_sems.at[rel - 1],
                device_id=(px, py, c), device_id_type=_MESH_ID))

        def begin():
            mine.start()
            for cp in copies:
                cp.start()

        def finish():
            for cp in copies:
                cp.wait()
            mine.wait()

        return [begin, finish]

    sems = [pltpu.SemaphoreType.DMA((3,)), pltpu.SemaphoreType.DMA((3,)), pltpu.SemaphoreType.DMA]
    return jax.ShapeDtypeStruct(partials.shape, partials.dtype), sems, phases


_HBM = pl.BlockSpec(memory_space=pltpu.HBM)


def _exchange(name, array, definition):
    out_shape, sems, phases = definition(array)

    def body(x_ref, out_ref, *sem_refs):
        for phase in phases(x_ref, out_ref, *sem_refs):
            phase()

    return pl.pallas_call(body, name=name, in_specs=[_HBM], out_specs=_HBM, out_shape=out_shape, scratch_shapes=sems)(array)


def _ride(rider, step, nsteps, x_ref, out_ref, sem_refs):
    array, definition, fractions = rider
    for phase, frac in zip(definition(array)[2](x_ref, out_ref, *sem_refs), fractions, strict=True):
        pl.when(step == min(int(frac * nsteps), nsteps - 1))(phase)


def _all_gather(name, block):
    return _exchange(name, block, _gather_def)


def _chip_partials(name, blocks, from_sibling, tile, out_dtype):
    _, R, C = blocks.shape

    def body(x_ref, s_ref, o_ref):
        c = lax.axis_index("c")
        for q in range(4):
            o_ref[q] = (x_ref[2 * q + c].astype(F32) + s_ref[q].astype(F32)).astype(o_ref.dtype)

    return pl.pallas_call(
        body, name=name, grid=(R // tile,),
        in_specs=[pl.BlockSpec((_N_DEV, tile, C), lambda i: (0, i, 0)), pl.BlockSpec((4, tile, C), lambda i: (0, i, 0))],
        out_specs=pl.BlockSpec((4, tile, C), lambda i: (0, i, 0)), out_shape=jax.ShapeDtypeStruct((4, R, C), out_dtype),
        compiler_params=_cparams(("parallel",)),
    )(blocks, from_sibling)


def _sum_leading(name, x, tile=None):
    n, R, C = x.shape
    tile = tile or _pick(R, (512, 256, 128, 64, 32, 16, 8))

    def body(x_ref, o_ref):
        acc = x_ref[0].astype(F32)
        for k in range(1, n):
            acc = acc + x_ref[k].astype(F32)
        o_ref[...] = acc

    return pl.pallas_call(
        body, name=name, grid=(R // tile,), in_specs=[pl.BlockSpec((n, tile, C), lambda i: (0, i, 0))],
        out_specs=pl.BlockSpec((tile, C), lambda i: (i, 0)), out_shape=jax.ShapeDtypeStruct((R, C), F32),
        compiler_params=_cparams(("parallel",)),
    )(x)


def _adamw_update(w_, g_, m_, v_):
    m2 = ADAM_B1 * m_ + (1.0 - ADAM_B1) * g_
    v2 = ADAM_B2 * v_ + (1.0 - ADAM_B2) * jnp.square(g_)
    m_hat = m2 / (1.0 - ADAM_B1 ** ADAM_STEP)
    v_hat = v2 / (1.0 - ADAM_B2 ** ADAM_STEP)
    delta = -ADAM_LR * (m_hat / (jnp.sqrt(v_hat) + ADAM_EPS) + ADAM_WD * w_)
    return delta, m2, v2


def _adamw(name, w, g, m, v):
    R, C = w.shape
    tile = _pick(R, (256, 128, 64, 32, 16, 8))
    outs, _ = _rowwise(name, lambda tv, bv: (list(_adamw_update(*tv)), []), [w, g, m, v], [], [(C, F32)] * 3, [], tile)
    return outs


def _adamw_many(name, ws, gs, ms, vs):
    n = len(ws)

    def body(*refs):
        ins, outs = refs[:4 * n], refs[4 * n:]
        for i in range(n):
            res = _adamw_update(ins[i][...], ins[n + i][...], ins[2 * n + i][...], ins[3 * n + i][...])
            for j in range(3):
                outs[j * n + i][...] = res[j]

    out_shape = [jax.ShapeDtypeStruct(w.shape, F32) for w in ws] * 3
    res = pl.pallas_call(body, name=name, out_shape=out_shape, compiler_params=_cparams())(*ws, *gs, *ms, *vs)
    return res[:n], res[n:2 * n], res[2 * n:]


def _pack_rows(arrs, lanes=128, row_mult=8):
    flat, places, off = [], [], 0
    for a_ in arrs:
        n = a_.size
        flat.append(a_.reshape(-1).astype(F32))
        places.append((off, n, a_.shape))
        off += n
    total = -(-off // (lanes * row_mult)) * (lanes * row_mult)
    if total > off:
        flat.append(jnp.zeros((total - off,), F32))
    return jnp.concatenate(flat).reshape(total // lanes, lanes), places


def _unpack_rows(packed, places):
    flat = packed.reshape(-1)
    return [flat[o:o + n].reshape(s) for (o, n, s) in places]


_WEIGHTS = ['w_ada', 'b_ada', 'w_in', 'b_in', 'g_ln_v', 'b_ln_v', 'w_spatial', 'b_spatial', 'mu_shift', 'w0', 'w_decay_up', 'a0',
            'w_aaa_up', 'w_gate_up', 'k_k', 'k_a', 'r_k', 'gn_gain', 'gn_bias', 'w_branch_a', 'w_branch_b', 'w_out', 'b_out',
            'ln1_g', 'ln1_b', 'w_ff1', 'b_ff1', 'w_ff2', 'b_ff2', 'ln2_g', 'ln2_b']
_BIG = {'w_ff1': (0, 512), 'w_ff2': (512, 512), 'w_out': (1024, 128), 'w_branch_a': (1152, 64), 'w_branch_b': (1216, 64), 'w_in': (1280, 640)}
_LATER_ROWS = 1280
_CUT_BY_COLS = ('w_ff1', 'w_in', 'w_branch_a', 'w_branch_b')
IN_SHARD = IN_COLS // _N_DEV
_LORA = {'w_decay_up': (0, LORA_W), 'w_aaa_up': (LORA_W, LORA_A), 'w_gate_up': (LORA_W + LORA_A, LORA_G)}
_COMM_DTYPE = jnp.bfloat16


def _pad_rows(a, rows):
    return jnp.pad(a, ((0, rows - a.shape[0]),) + ((0, 0),) * (a.ndim - 1))


def _pack_big(shards):
    blocks = []
    for n, (_, rows) in _BIG.items():
        a = shards[n].T if n in _CUT_BY_COLS else shards[n]
        blocks.append(_pad_rows(a.reshape(-1, D_MODEL), rows))
    return jnp.concatenate(blocks, axis=0)


def _unpack_big(block, like):
    out = {}
    for n, (r0, _) in _BIG.items():
        rr, cc = like[n].shape
        if n in _CUT_BY_COLS:
            out[n] = block[r0:r0 + rr * cc // D_MODEL].reshape(cc, rr).T
        else:
            out[n] = block[r0:r0 + rr]
    return out


def _to_padded(a, axis):
    g_end = 2 * G_WIDTH
    r_end = g_end + RW_USED
    take = lambda lo, hi: lax.slice_in_dim(a, lo, hi, axis=axis)
    zshape = list(a.shape)
    zshape[axis] = RW_COLS - RW_USED
    return jnp.concatenate([take(r_end, IN_COLS), take(g_end, r_end), jnp.zeros(zshape, a.dtype), take(0, g_end)], axis=axis)


def _from_padded(a, axis):
    take = lambda lo, hi: lax.slice_in_dim(a, lo, hi, axis=axis)
    return jnp.concatenate([take(2 * D_MODEL + RW_COLS, P_COLS), take(2 * D_MODEL, 2 * D_MODEL + RW_USED), take(0, 2 * D_MODEL)], axis=axis)


_SEGMENTS = ((2 * G_WIDTH + RW_USED, IN_COLS, 0), (2 * G_WIDTH, 2 * G_WIDTH + RW_USED, 2 * D_MODEL), (0, 2 * G_WIDTH, 2 * D_MODEL + RW_COLS))


def _w_in_rows_to_padded(per_device):
    pieces = []
    for lo, hi, _ in _SEGMENTS:
        for d in range(_N_DEV):
            a_, b_ = max(lo, d * IN_SHARD), min(hi, (d + 1) * IN_SHARD)
            if a_ < b_:
                pieces.append(per_device[d, a_ - d * IN_SHARD:b_ - d * IN_SHARD])
        if hi == 2 * G_WIDTH + RW_USED:
            pieces.append(jnp.zeros((RW_COLS - RW_USED, D_MODEL), per_device.dtype))
    return jnp.concatenate(pieces, axis=0)


def _step(p, m, v, x, c, target):
    T = x.shape[0]
    xi, yi, ci = _place()
    me = 4 * xi + 2 * yi + ci
    tile = _pick(T, (512, 256))

    lane = jnp.arange(R_WIDTH)
    gsum = (lane[:128, None] // R_HEAD == lane[None, :128] // R_HEAD).astype(F32)
    expand = (jnp.arange(128)[:, None] == (lane[None, :] // (G_WIDTH // 8))).astype(F32)

    (c_act,), _ = _rowwise("silu_c", lambda tv, bv: ([tv[0] * _sigmoid(tv[0])], []), [c], [], [(D_MODEL, F32)], [], 1)
    small, places = _pack_rows([c_act, p['w_decay_up'], p['w_aaa_up'], p['w_gate_up']])
    small_all = _all_gather("gather_small", small)
    per_dev = [_unpack_rows(small_all[d], places) for d in range(_N_DEV)]
    c_act_all = _pad_rows(jnp.concatenate([pd[0] for pd in per_dev], axis=0), 16)
    lora_full = {n: jnp.concatenate([pd[i + 1] for pd in per_dev], axis=1) for i, n in enumerate(_LORA)}
    lora_pad = {n: jnp.zeros((LORA_PAD, R_WIDTH), F32).at[r0:r0 + nr].set(lora_full[n]) for n, (r0, nr) in _LORA.items()}

    big_names = list(_BIG)
    my_rows = _pack_big(p).astype(_MXU_DTYPE)
    b_in_p = _to_padded(p['b_in'], 1)
    mu_p = jnp.concatenate([p['mu_shift'], jnp.zeros((1, RW_COLS - RW_USED), F32)], axis=1)

    b_ada_mine = lax.dynamic_slice(p['b_ada'], (0, me * 768), (1, 768))
    mod_cols = _mm("ada_mod", c_act_all, p['w_ada'], "nn", bias=b_ada_mine)
    mod_all = _all_gather("gather_mod", mod_cols)
    mod = lax.dynamic_index_in_dim(mod_all, me, axis=1, keepdims=False).reshape(1, 6 * D_MODEL)
    sh1, sc1, gt1, sh2, sc2, gt2 = [mod[:, i * D_MODEL:(i + 1) * D_MODEL] for i in range(6)]

    (h,), _, w_in_all = _rowwise("modulate1", lambda tv, bv: (_seg_modulate(tv, bv), []), [x], [sc1, sh1], [(D_MODEL, _MXU_DTYPE)], [], tile,
                                 rider=(my_rows[_LATER_ROWS:], _gather_def, (0.0, 0.5, 1.0)))
    w_in_t = _w_in_rows_to_padded(w_in_all)
    proj = _mm("in_proj", h, w_in_t, "nt", bias=b_in_p)
    ws = p['w_spatial']
    b_tg = jnp.zeros((G_CHUNK, 128), F32).at[:, :8].set(p['b_spatial'].T)
    gmlp_b = [p['g_ln_v'], p['b_ln_v'], ws, b_tg, expand]
    z_gmlp = (proj, 2 * G_WIDTH, 4)
    (ya,), _ = _rowwise("gmlp", lambda tv, bv: (_seg_gmlp(tv, bv), []), [z_gmlp], gmlp_b, [(G_WIDTH, F32)], [], G_CHUNK)
    z_rw = (proj, RW_COLS, 1)
    z_rw_halo = ("halo", proj, RW_COLS, 1)
    pre_b = [mu_p, p['w0'], lora_pad['w_decay_up'], p['a0'], lora_pad['w_aaa_up'], lora_pad['w_gate_up'], p['k_k'], p['k_a'], gsum]

    def pre_fwd(tv, bv):
        z_t, halo_t = tv
        return _seg_rwkv_pre([z_t, _shift_down(z_t, halo_t, pl.program_id(0) == 0)], bv), []

    pre_out, _ = _rowwise("rwkv_pre", pre_fwd, [z_rw, z_rw_halo], pre_b, [(R_WIDTH, F32)] * 7, [], tile)
    r_, lw_, k2_, v_, a_, b_, g_ = pre_out
    scan_in = (r_, lw_, k2_, v_, a_, b_)
    y_, states, inverses, later_all = _scan_fwd(*scan_in, rider=(my_rows[:_LATER_ROWS], _gather_def, (0.0, 0.875, 1.0)))

    def whole(n, rows):
        r0 = _BIG[n][0]
        return later_all[:, r0:r0 + rows].reshape(_N_DEV * rows, D_MODEL)

    w_ff1_t, w_ff2, w_out = whole('w_ff1', 512), whole('w_ff2', 512), whole('w_out', 128)
    w_ba_t = whole('w_branch_a', 64).reshape(D_MODEL, G_WIDTH)
    w_bb_t = whole('w_branch_b', 64).reshape(D_MODEL, R_WIDTH)
    post_b = [p['r_k'].reshape(1, R_WIDTH), p['gn_gain'], p['gn_bias'], gsum]
    (yb,), _ = _rowwise("rwkv_post", lambda tv, bv: (_seg_rwkv_post(tv, bv), []), [y_, r_, k2_, v_, g_], post_b, [(R_WIDTH, F32)], [], tile)
    pa = _mm("branch_a", ya, w_ba_t, "nt", out_dtype=_MXU_DTYPE)
    pb = _mm("branch_b", yb, w_bb_t, "nt", out_dtype=_MXU_DTYPE)
    gates = [(proj, D_MODEL, 0), (proj, D_MODEL, 1)]
    (merged,), _ = _rowwise("merge", lambda tv, bv: (_seg_merge(tv, bv), []), gates + [pa, pb], [], [(D_MODEL, _MXU_DTYPE)], [], tile)
    mix = _mm("out_proj", merged, w_out, "nn", bias=p['b_out'])
    mid_b = [gt1, p['ln1_g'], p['ln1_b'], sc2, sh2]
    (h1, h2in), _ = _rowwise("mid", lambda tv, bv: (_seg_mid(tv, bv), []), [x, mix], mid_b, [(D_MODEL, F32), (D_MODEL, _MXU_DTYPE)], [], tile)
    act = _mm("ff1", h2in, w_ff1_t, "nt", bias=p['b_ff1'], epi=lambda t: _seg_relu2([t], [])[0], epi_dtype=_MXU_DTYPE, raw=False)
    ff = _mm("ff2", act, w_ff2, "nn", bias=p['b_ff2'])

    def loss_fn(tv, bv):
        h1_t, ff_t, tgt = tv
        val, grads = jax.value_and_grad(lambda a0_, a1_, b0_, b1_, b2_: _seg_loss([a0_, a1_, tgt], [b0_, b1_, b2_]), argnums=(0, 1, 2, 3, 4))(h1_t, ff_t, *bv)
        return [grads[0], grads[1]], [grads[2], grads[3], grads[4], jnp.sum(grads[1], axis=0, keepdims=True), jnp.full((1, 128), val, F32)]

    (dh1_a, dff), (d_gt2, d_ln2_g, d_ln2_b, d_b_ff2, loss_row) = _rowwise(
        "loss", loss_fn, [h1, ff, target], [gt2, p['ln2_g'], p['ln2_b']], [(D_MODEL, F32), (D_MODEL, _MXU_DTYPE)], [(1, D_MODEL)] * 4 + [(1, 128)], tile)

    g = {}
    g['ln2_g'], g['ln2_b'], g['b_ff2'] = d_ln2_g, d_ln2_b, d_b_ff2
    gw = {}
    gw['w_ff2'] = _mm("g_w_ff2", act, dff, "tn", out_dtype=_COMM_DTYPE)
    df1, g['b_ff1'] = _mm("d_act", dff, w_ff2, "nt", beside=act, epi=lambda d_, a_: d_ * (2.0 * jnp.sqrt(a_.astype(F32))),
                          epi_dtype=_MXU_DTYPE, colsum=True)
    gw['w_ff1'] = _mm("g_w_ff1", df1, h2in, "tn", out_dtype=_COMM_DTYPE)
    dh2in = _mm("d_h2in", df1, w_ff1_t, "nn")
    (dx_a, dmix), (d_gt1, g['ln1_g'], g['ln1_b'], d_sc2, d_sh2), (g['b_out'],) = _rowwise_vjp(
        "mid_bwd", _seg_mid, [x, mix], mid_b, [dh1_a, dh2in], tile, [0, 1], [0, 1, 2, 3, 4], t_dtypes=[F32, _MXU_DTYPE], colsum=[1])
    gw['w_out'] = _mm("g_w_out", merged, dmix, "tn", out_dtype=_COMM_DTYPE)
    dmerged = _mm("d_merged", dmix, w_out, "nt", out_dtype=_MXU_DTYPE)
    (dproj, dpa, dpb), _, (cs_gates,) = _rowwise_vjp(
        "merge_bwd", _seg_merge, gates + [pa, pb], [], [dmerged], tile, [0, 1, 2, 3], [], t_dtypes=[_MXU_DTYPE] * 3, colsum=[0],
        finish=lambda dts, sc: [jnp.concatenate(dts[:2], axis=1), dts[2], dts[3]], out_widths=[2 * D_MODEL, D_MODEL, D_MODEL],
        into={0: (lax.empty((T, P_COLS), _MXU_DTYPE), 0)})
    gw['w_branch_a'] = _mm("g_w_branch_a", dpa, ya, "tn", out_dtype=_COMM_DTYPE)
    gw['w_branch_b'] = _mm("g_w_branch_b", dpb, yb, "tn", out_dtype=_COMM_DTYPE)
    dya = _mm("d_ya", dpa, w_ba_t, "nn")
    dyb = _mm("d_yb", dpb, w_bb_t, "nn")
    def send_rows(names):
        parts = []
        for n in names:
            per_dev = gw[n].reshape(_N_DEV, -1, D_MODEL)
            parts.append(jnp.pad(per_dev, ((0, 0), (0, _BIG[n][1] - per_dev.shape[1]), (0, 0))))
        return jnp.concatenate(parts, axis=1) if len(parts) > 1 else parts[0]

    send_early = send_rows(big_names[:-1])
    (dy, dr1, dk1, dv1, dg_), (d_r_k, g['gn_gain'], g['gn_bias']), _, sibling_early = _rowwise_vjp(
        "rwkv_post_bwd", _seg_rwkv_post, [y_, r_, k2_, v_, g_], post_b, [dyb], tile, [0, 1, 2, 3, 4], [0, 1, 2],
        rider=(send_early, _sibling_def, (0.0, 1.0)))
    g['r_k'] = d_r_k
    partials_early = _chip_partials("chip_partials_early", send_early, sibling_early, 128, _COMM_DTYPE)
    dr2, dlw, dk2, dv2, da, db, landed_early = _scan_bwd(*scan_in, states, inverses, dy, rider=(partials_early, _chips_def, (0.0, 1.0)))
    pre_tile = 256
    last_step = T // pre_tile - 1

    def pre_prep(prim):
        z_t, halo_t = prim
        return [z_t, _shift_down(z_t, halo_t, pl.program_id(0) == last_step)]

    def pre_finish(dts, sc):
        dz_direct, dprev = dts
        (row_after,) = sc

        @pl.when(pl.program_id(0) == 0)
        def _():
            row_after[...] = jnp.zeros_like(row_after)

        dz = dz_direct + _shift_up(dprev, row_after[...])
        row_after[...] = dprev[:_HALO]
        return [dz]

    small_names = [n for n in _WEIGHTS if n not in _BIG and n not in ('w_ada', 'b_ada')]
    (dproj,), (g['g_ln_v'], g['b_ln_v'], g['w_spatial'], d_b_tg), (cs_g,) = _rowwise_vjp(
        "gmlp_bwd", _seg_gmlp, [z_gmlp], gmlp_b, [dya], G_CHUNK, [0], [0, 1, 2, 3], t_dtypes=[_MXU_DTYPE], colsum=[0],
        into={0: (dproj, 4)})
    g['b_spatial'] = d_b_tg[:, :8].T
    names_1 = [n for n in small_names if n in g]
    packed_1, places_1 = _pack_rows([g[n] for n in names_1] + [loss_row], row_mult=256)
    (dproj,), (d_mu, g['w0'], d_wd, g['a0'], d_wa, d_wg, g['k_k'], g['k_a']), (cs_rw,), small_all_1 = _rowwise_vjp(
        "rwkv_pre_bwd", _seg_rwkv_pre, [z_rw, z_rw_halo], pre_b, [[dr1, dr2], dlw, [dk1, dk2], [dv1, dv2], da, db, dg_], pre_tile,
        [0, 1], [0, 1, 2, 3, 4, 5, 6, 7], t_dtypes=[_MXU_DTYPE], colsum=[0], prep=pre_prep, finish=pre_finish,
        out_widths=[RW_COLS], reverse=True, scratch=[(_HALO, RW_COLS)], into={0: (dproj, 1)},
        rider=(packed_1, _gather_def, (0.0, 0.6, 1.0)))
    g['mu_shift'] = d_mu[:, :RW_USED]
    for n, d_ in (('w_decay_up', d_wd), ('w_aaa_up', d_wa), ('w_gate_up', d_wg)):
        r0, nr = _LORA[n]
        g[n] = d_[r0:r0 + nr]
    g['b_in'] = _from_padded(jnp.concatenate([cs_gates, cs_rw, cs_g], axis=1), 1)
    names_2 = [n for n in small_names if n not in names_1]
    packed_2, places_2 = _pack_rows([g[n] for n in names_2], row_mult=256)
    gw_in_t, small_all_2 = _mm("g_w_in", dproj, h, "tn", out_dtype=_COMM_DTYPE, rider=(packed_2, _gather_def, (0.0, 0.6, 1.0)))
    gw['w_in'] = _from_padded(gw_in_t, 0)
    send_late = send_rows(big_names[-1:])
    sibling_late = _exchange("pair_exchange_late", send_late, _sibling_def)
    partials_late = _chip_partials("chip_partials_late", send_late, sibling_late, 128, _COMM_DTYPE)
    dh, landed_late = _mm("d_h", dproj, w_in_t, "nn", rider=(partials_late, _chips_def, (0.0, 1.0)))

    def mod1_bwd(tv, bv):
        x_t, dh_t, dxa_t = tv
        (sc,) = bv
        return [dxa_t + dh_t * (1.0 + sc)], [jnp.sum(dh_t * x_t, axis=0, keepdims=True), jnp.sum(dh_t, axis=0, keepdims=True)]

    (grad_x,), (d_sc1, d_sh1) = _rowwise("modulate1_bwd", mod1_bwd, [x, dh, dx_a], [sc1], [(D_MODEL, F32)], [(1, D_MODEL)] * 2, tile)

    dmod = jnp.concatenate([d_sh1, d_sc1, d_gt1, d_sh2, d_sc2, d_gt2], axis=1).reshape(6 * D_MODEL // 128, 128)
    dmod_all = _all_gather("gather_dmod", dmod)
    g['b_ada'] = _sum_leading("sum_dmod", dmod_all).reshape(1, 6 * D_MODEL)
    dmod_mine = lax.dynamic_slice(dmod_all.reshape(_N_DEV, 6 * D_MODEL), (0, me * 768), (_N_DEV, 768))
    g_w_ada = _mm("g_w_ada", c_act_all, _pad_rows(dmod_mine, 16), "tn")

    sums_1 = _unpack_rows(_sum_leading("sum_small_1", small_all_1), places_1)
    loss = sums_1.pop()[0, 0]
    sums_2 = _unpack_rows(_sum_leading("sum_small_2", small_all_2), places_2)
    g.update(zip(names_1 + names_2, sums_1 + sums_2, strict=True))
    for n in _LORA:
        g[n] = lax.dynamic_slice(g[n], (0, me * R_HEAD), (g[n].shape[0], R_HEAD))
    g['w_ada'] = g_w_ada

    summed = jnp.concatenate([_sum_leading("sum_early", landed_early), _sum_leading("sum_late", landed_late)], axis=0)
    g.update(_unpack_big(summed, p))

    delta, new_m, new_v = {}, {}, {}
    own_call = ['w_ada'] + big_names
    for n in own_call:
        if n == 'w_in':
            r0 = _BIG[n][0]
            outs_t = _adamw("adamw_" + n, p[n].T, summed[r0:r0 + IN_SHARD], m[n].T, v[n].T)
            delta[n], new_m[n], new_v[n] = [o.T for o in outs_t]
        else:
            delta[n], new_m[n], new_v[n] = _adamw("adamw_" + n, p[n], g[n], m[n], v[n])
    rest = [n for n in _WEIGHTS if n not in own_call]
    outs = _adamw_many("adamw_rest", *[[d[n].reshape(p[n].shape) for n in rest] for d in (p, g, m, v)])
    for d, o in zip((delta, new_m, new_v), outs, strict=True):
        d.update(zip(rest, o, strict=True))
    return loss, grad_x, g, delta, new_m, new_v


def kernel(x, c, w_ada, b_ada, w_in, b_in, g_ln_v, b_ln_v, w_spatial, b_spatial, mu_shift, w0, w_decay_up, a0, w_aaa_up, w_gate_up, k_k, k_a, r_k, gn_gain, gn_bias, w_branch_a, w_branch_b, w_out, b_out, ln1_g, ln1_b, w_ff1, b_ff1, w_ff2, b_ff2, ln2_g, ln2_b, loss_target, m_w_ada, m_b_ada, m_w_in, m_b_in, m_g_ln_v, m_b_ln_v, m_w_spatial, m_b_spatial, m_mu_shift, m_w0, m_w_decay_up, m_a0, m_w_aaa_up, m_w_gate_up, m_k_k, m_k_a, m_r_k, m_gn_gain, m_gn_bias, m_w_branch_a, m_w_branch_b, m_w_out, m_b_out, m_ln1_g, m_ln1_b, m_w_ff1, m_b_ff1, m_w_ff2, m_b_ff2, m_ln2_g, m_ln2_b, v_w_ada, v_b_ada, v_w_in, v_b_in, v_g_ln_v, v_b_ln_v, v_w_spatial, v_b_spatial, v_mu_shift, v_w0, v_w_decay_up, v_a0, v_w_aaa_up, v_w_gate_up, v_k_k, v_k_a, v_r_k, v_gn_gain, v_gn_bias, v_w_branch_a, v_w_branch_b, v_w_out, v_b_out, v_ln1_g, v_ln1_b, v_w_ff1, v_b_ff1, v_w_ff2, v_b_ff2, v_ln2_g, v_ln2_b):
    given = dict(locals())
    shapes = {n: given[n].shape for n in _WEIGHTS}
    def two_d(a_):
        a_ = a_[0]
        return a_.reshape(1, -1) if a_.ndim == 1 else a_
    p = {n: two_d(given[n]) for n in _WEIGHTS}
    m = {n: two_d(given["m_" + n]) for n in _WEIGHTS}
    v = {n: two_d(given["v_" + n]) for n in _WEIGHTS}
    loss, grad_x, g, delta, new_m, new_v = _step(p, m, v, x[0], c, loss_target[0])
    outs = [loss, grad_x[None]]
    for d in (g, delta, new_m, new_v):
        outs += [d[n].reshape(shapes[n]) for n in _WEIGHTS]
    return tuple(outs)
```

```python
import functools

import jax
import jax.numpy as jnp
from jax import lax
from jax.experimental import pallas as pl
from jax.experimental.pallas import tpu as pltpu

F32 = jnp.float32
_MXU_DTYPE = jnp.bfloat16
_HI = lax.Precision.HIGHEST
_VMEM_LIMIT = 48 * 1024 * 1024
_MESH_ID = pl.DeviceIdType.MESH
_N_DEV = 8

D_MODEL = 1024
G_WIDTH = 512
G_CHUNK = 128
R_WIDTH = 512
R_HEADS = 8
R_HEAD = 64
LORA_W, LORA_A, LORA_G = 32, 32, 96
D_FF = 4096
ALPHA = 2.0 ** 0.25
LN_EPS = 1e-5
GN_EPS = 64e-5
SCAN_CHUNK = 64
SCAN_PER_STEP = 4
ADAM_LR, ADAM_B1, ADAM_B2, ADAM_EPS, ADAM_WD, ADAM_STEP = 0.001, 0.9, 0.999, 1e-08, 0.01, 10

P_COLS = 5120
RW_COLS = 2048
RW_USED = 3 * R_WIDTH + LORA_W + LORA_A + LORA_G
LORA_PAD = 256
IN_COLS = 2 * G_WIDTH + RW_USED + 2 * D_MODEL


def _cparams(sem=None, **kw):
    if sem is not None:
        kw["dimension_semantics"] = sem
    return pltpu.CompilerParams(vmem_limit_bytes=_VMEM_LIMIT, **kw)


def _dot(a, b, dims=(((1,), (0,)), ((), ())), hi=False):
    if hi:
        return lax.dot_general(a.astype(F32), b.astype(F32), dims, precision=_HI, preferred_element_type=F32)
    return lax.dot_general(a.astype(_MXU_DTYPE), b.astype(_MXU_DTYPE), dims, preferred_element_type=F32)


_NN = (((1,), (0,)), ((), ()))
_NT = (((1,), (1,)), ((), ()))
_TN = (((0,), (0,)), ((), ()))


def _pick(n, pref):
    for t in pref:
        if n % t == 0:
            return t
    return n


def _mm(name, a, b, mode, bias=None, out_dtype=F32, epi=None, epi_dtype=None, raw=True, beside=None, colsum=False, rider=None,
        tm=None, tn=None, tk=None):
    if mode == "nn":
        (M, K), (_, N) = a.shape, b.shape
    elif mode == "nt":
        (M, K), (N, _) = a.shape, b.shape
    else:
        (K, M), (_, N) = a.shape, b.shape
    tm = tm or _pick(M, (2048, 1280, 1024, 512, 256, 128, 64, 32, 16, 8))
    tn = tn or _pick(N, (1024, 512, 640, 384, 256, 128))
    tk = tk or _pick(K, (1024, 512, 256, 128))
    nk = K // tk
    dims = {"nn": _NN, "nt": _NT, "tn": _TN}[mode]
    a_spec = pl.BlockSpec((tk, tm), lambda i, j, k: (k, i)) if mode == "tn" else pl.BlockSpec((tm, tk), lambda i, j, k: (i, k))
    b_spec = pl.BlockSpec((tn, tk), lambda i, j, k: (j, k)) if mode == "nt" else pl.BlockSpec((tk, tn), lambda i, j, k: (k, j))
    o_spec = pl.BlockSpec((tm, tn), lambda i, j, k: (i, j))
    has_bias, has_beside = bias is not None, beside is not None
    two = epi is not None and not has_beside and raw
    only_epi = epi is not None and not has_beside and not raw
    grid = (M // tm, N // tn, nk)
    ride_shape, ride_sems, _ = rider[1](rider[0]) if rider else (None, [], None)

    def body(*refs):
        refs = list(refs)
        n_in = 2 + has_bias + has_beside
        if rider:
            sem_refs = [refs.pop() for _ in ride_sems][::-1]
            ride_out = refs.pop(n_in + 1 + 1 + two)
            ride_in = refs.pop(n_in)
            step = (pl.program_id(0) * grid[1] + pl.program_id(1)) * grid[2] + pl.program_id(2)
            _ride(rider, step, grid[0] * grid[1] * grid[2], ride_in, ride_out, sem_refs)
        a_ref, b_ref = refs[0], refs[1]
        bias_ref = refs[2] if has_bias else None
        beside_ref = refs[n_in - 1] if has_beside else None
        outs = refs[n_in:]
        o_ref, acc_ref = outs[0], outs[-1]
        k = pl.program_id(2)

        @pl.when(k == 0)
        def _():
            acc_ref[...] = jnp.zeros_like(acc_ref)

        acc_ref[...] += _dot(a_ref[...], b_ref[...], dims)

        @pl.when(k == nk - 1)
        def _():
            res = acc_ref[...]
            if has_bias:
                res = res + bias_ref[...]
            if has_beside:
                val = epi(res, beside_ref[...])
                o_ref[...] = val.astype(o_ref.dtype)
                if colsum:
                    outs[1][...] = jnp.broadcast_to(jnp.sum(val, axis=0, keepdims=True), outs[1].shape)
            elif only_epi:
                o_ref[...] = epi(res).astype(o_ref.dtype)
            else:
                o_ref[...] = res.astype(o_ref.dtype)
                if two:
                    outs[1][...] = epi(res).astype(outs[1].dtype)

    in_specs = [a_spec, b_spec]
    args = [a, b]
    if has_bias:
        in_specs.append(pl.BlockSpec((1, tn), lambda i, j, k: (0, j)))
        args.append(bias)
    if has_beside:
        in_specs.append(o_spec)
        args.append(beside)
    out_shape = [jax.ShapeDtypeStruct((M, N), epi_dtype if (has_beside or only_epi) else out_dtype)]
    out_specs = [o_spec]
    if two:
        out_shape.append(jax.ShapeDtypeStruct((M, N), epi_dtype))
        out_specs.append(o_spec)
    if colsum:
        assert has_beside and not rider
        out_shape.append(jax.ShapeDtypeStruct((grid[0] * 8, N), F32))
        out_specs.append(pl.BlockSpec((8, tn), lambda i, j, k: (i, j)))
    if rider:
        in_specs.append(_HBM)
        args.append(rider[0])
        out_shape.append(ride_shape)
        out_specs.append(_HBM)
    res = pl.pallas_call(
        body, name=name, grid=grid, in_specs=in_specs, out_specs=out_specs, out_shape=out_shape,
        scratch_shapes=[pltpu.VMEM((tm, tn), F32)] + list(ride_sems),
        compiler_params=_cparams(("arbitrary",) * 3 if rider else ("parallel", "parallel", "arbitrary")),
    )(*args)
    if colsum:
        return res[0], _sum_leading(name + "_colsum", res[1].reshape(grid[0], 8, N))[0:1]
    return res if (two or rider) else res[0]


_HALO = 8


def _rowwise(name, fn, tiled, bcast, tiled_out, red_out, tile, reverse=False, scratch=(), rider=None):
    tiled = [t if isinstance(t, tuple) else (t, t.shape[1], 0) for t in tiled]
    T = next(t[0] for t in tiled if not isinstance(t[0], str)).shape[0]
    n = T // tile
    nt, nb, nto, nsc, nro = len(tiled), len(bcast), len(tiled_out), len(scratch), len(red_out)
    ride_shape, ride_sems, _ = rider[1](rider[0]) if rider else (None, [], None)
    nr = 1 if rider else 0
    into = [(j, t[2], t[3]) for j, t in enumerate(tiled_out) if len(t) == 4]
    na = len(into)

    def row_block(i):
        return n - 1 - i if reverse else i

    def body(*refs):
        i2 = nt + nb
        o0 = i2 + nr + na
        o1, o2 = o0 + nto, o0 + nto + nro
        s0 = o2 + nr
        if rider:
            _ride(rider, pl.program_id(0), n, refs[i2], refs[o2], refs[s0 + nsc:])
        t_refs, b_refs = refs[:nt], refs[nt:i2]
        to_refs, ro_refs = refs[o0:o1], refs[o1:o2]
        extra = (list(refs[s0:s0 + nsc]),) if nsc else ()
        touts, routs = fn([r[...] for r in t_refs], [r[...] for r in b_refs], *extra)
        for r, v in zip(to_refs, touts, strict=True):
            r[...] = v.astype(r.dtype)
        if ro_refs:
            i = pl.program_id(0)

            @pl.when(i == 0)
            def _():
                for r, v in zip(ro_refs, routs, strict=True):
                    r[...] = v.astype(F32)

            @pl.when(i > 0)
            def _():
                for r, v in zip(ro_refs, routs, strict=True):
                    r[...] += v.astype(F32)

    def whole(shape):
        nd = len(shape)
        return pl.BlockSpec(tuple(shape), lambda i: (0,) * nd)

    per_tile = tile // _HALO
    in_specs, arrays = [], []
    for t in tiled:
        if isinstance(t[0], str):
            _, arr, w, cb = t
            in_specs.append(pl.BlockSpec((_HALO, w), functools.partial(lambda i, cb: (jnp.maximum(row_block(i) * per_tile - 1, 0), cb), cb=cb)))
        else:
            arr, w, cb = t
            in_specs.append(pl.BlockSpec((tile, w), functools.partial(lambda i, cb: (row_block(i), cb), cb=cb)))
        arrays.append(arr)
    in_specs += [whole(b.shape) for b in bcast]
    out_specs, out_shape = [], []
    for t in tiled_out:
        cb = t[3] if len(t) == 4 else 0
        out_specs.append(pl.BlockSpec((tile, t[0]), functools.partial(lambda i, cb: (row_block(i), cb), cb=cb)))
        out_shape.append(jax.ShapeDtypeStruct(t[2].shape if len(t) == 4 else (T, t[0]), t[1]))
    out_specs += [whole(s) for s in red_out]
    out_shape += [jax.ShapeDtypeStruct(tuple(s), F32) for s in red_out]
    ride_args = []
    if rider:
        in_specs.append(_HBM)
        out_specs.append(_HBM)
        out_shape.append(ride_shape)
        ride_args = [rider[0]]
    in_specs += [pl.BlockSpec(memory_space=pl.ANY)] * na
    aliases = {nt + nb + nr + k: j for k, (j, _, _) in enumerate(into)}
    res = pl.pallas_call(
        body, name=name, grid=(n,), in_specs=in_specs, out_specs=out_specs, out_shape=out_shape,
        scratch_shapes=[pltpu.VMEM(tuple(s), F32) for s in scratch] + list(ride_sems),
        input_output_aliases=aliases, compiler_params=_cparams(("arbitrary",)),
    )(*arrays, *bcast, *ride_args, *[buf for (_, buf, _) in into])
    if rider:
        return list(res[:nto]), list(res[nto:nto + nro]), res[-1]
    return list(res[:nto]), list(res[nto:])


def _rowwise_vjp(name, f, tiled, bcast, cts, tile, wrt_t, wrt_b, t_dtypes=None, colsum=(), prep=None, finish=None,
                 out_widths=None, reverse=False, scratch=(), rider=None, into=None):
    tiled = [t if isinstance(t, tuple) else (t, t.shape[1], 0) for t in tiled]
    npr = len(tiled)
    t_dtypes = t_dtypes or [F32] * len(wrt_t)
    groups = [c if isinstance(c, list) else [c] for c in cts]
    cts = [a_ for grp in groups for a_ in grp]

    def fn(tv, bv, sc=None):
        prim, flat_ct = tv[:npr], list(tv[npr:])
        if prep is not None:
            prim = prep(prim)
        ct = []
        for grp in groups:
            parts = [flat_ct.pop(0).astype(F32) for _ in grp]
            ct.append(functools.reduce(lambda p_, q_: p_ + q_, parts))

        def g(dt_vals, db_vals):
            full_t, full_b = list(prim), list(bv)
            for i, v in zip(wrt_t, dt_vals, strict=True):
                full_t[i] = v
            for j, v in zip(wrt_b, db_vals, strict=True):
                full_b[j] = v
            return f(full_t, full_b)

        outs, pull = jax.vjp(g, [prim[i].astype(F32) for i in wrt_t], [bv[j] for j in wrt_b])
        dts, dbs = pull([c.astype(o.dtype) for c, o in zip(ct, outs, strict=True)])
        if finish is not None:
            dts = finish(dts, sc)
        sums = [jnp.sum(dts[i].astype(F32), axis=0, keepdims=True) for i in colsum]
        return dts, list(dbs) + sums

    widths = out_widths or [tiled[i][1] for i in wrt_t]
    tiled_out = [(w, dt) + tuple((into or {}).get(j, ())) for j, (w, dt) in enumerate(zip(widths, t_dtypes, strict=True))]
    red_out = [bcast[j].shape for j in wrt_b] + [(1, widths[i]) for i in colsum]
    res = _rowwise(name, fn, tiled + list(cts), bcast, tiled_out, red_out, tile, reverse=reverse, scratch=scratch, rider=rider)
    dts, reds = res[0], res[1]
    nb = len(wrt_b)
    return (dts, reds[:nb], reds[nb:]) + tuple(res[2:])


def _layer_norm(x, g, b, eps):
    mu = jnp.mean(x, axis=-1, keepdims=True)
    xc = x - mu
    var = jnp.mean(xc * xc, axis=-1, keepdims=True)
    return xc * lax.rsqrt(var + eps) * g + b


def _gelu_tanh(x):
    return 0.5 * x * (1.0 + jnp.tanh(0.7978845608028654 * (x + 0.044715 * (x * x * x))))


def _sigmoid(x):
    return 1.0 / (1.0 + jnp.exp(-x))


def _seg_modulate(tv, bv):
    (x,), (sc, sh) = tv, bv
    return [x * (1.0 + sc) + sh]


def _seg_gmlp(tv, bv):
    (z,), (g_ln, b_ln, ws, b_tg, expand) = tv, bv
    bias_full = _dot(b_tg, expand, hi=True)
    zz = _gelu_tanh(z)
    u, v = zz[:, :G_WIDTH], zz[:, G_WIDTH:]
    v = _layer_norm(v, g_ln, b_ln, LN_EPS)
    row = lax.broadcasted_iota(jnp.int32, (G_CHUNK, G_CHUNK), 0)
    col = lax.broadcasted_iota(jnp.int32, (G_CHUNK, G_CHUNK), 1)
    causal = col <= row
    first_group = lax.broadcasted_iota(jnp.int32, (G_CHUNK, 128), 1) < 64
    parts = []
    for p in range(4):
        vp = v[:, 128 * p:128 * (p + 1)]
        s_even = _dot(jnp.where(causal, ws[2 * p], 0.0), vp)
        s_odd = _dot(jnp.where(causal, ws[2 * p + 1], 0.0), vp)
        parts.append(jnp.where(first_group, s_even, s_odd))
    s = jnp.concatenate(parts, axis=1) + bias_full
    return [u * s]


def _split2(x):
    hi = x.astype(_MXU_DTYPE)
    return hi, (x - hi.astype(F32)).astype(_MXU_DTYPE)


@jax.custom_vjp
def _group_sum(x, ones_blocks):
    hi, lo = _split2(x)
    w = ones_blocks.shape[0]
    return jnp.concatenate([_dot(hi[:, j:j + w], ones_blocks) + _dot(lo[:, j:j + w], ones_blocks) for j in range(0, x.shape[1], w)], axis=1)


def _group_sum_fwd(x, ones_blocks):
    return _group_sum(x, ones_blocks), ones_blocks


def _group_sum_bwd(ones_blocks, ct):
    return _group_sum(ct, ones_blocks), jnp.zeros_like(ones_blocks)


_group_sum.defvjp(_group_sum_fwd, _group_sum_bwd)


def _shift_down(z, halo, is_first):
    _, W = z.shape
    rolled = pltpu.roll(z, 1, 0)
    before = jnp.where(is_first, 0.0, pltpu.roll(halo, 1, 0))
    top_row = lax.broadcasted_iota(jnp.int32, (_HALO, W), 0) == 0
    return jnp.concatenate([jnp.where(top_row, before, rolled[:_HALO]), rolled[_HALO:]], axis=0)


def _shift_up(d, after):
    tile, W = d.shape
    rolled = pltpu.roll(d, tile - 1, 0)
    last_row = lax.broadcasted_iota(jnp.int32, (_HALO, W), 0) == _HALO - 1
    bottom = jnp.where(last_row, pltpu.roll(after, _HALO - 1, 0), rolled[tile - _HALO:])
    return jnp.concatenate([rolled[:tile - _HALO], bottom], axis=0)


def _seg_rwkv_pre(tv, bv):
    (z, prev), (mu, w0, wd, a0, wa, wg, k_k, k_a, gsum) = tv, bv
    zs = z + (prev - z) * mu
    r, k, v = zs[:, 0:512], zs[:, 512:1024], zs[:, 1024:1536]
    zl = zs[:, 3 * R_WIDTH:3 * R_WIDTH + LORA_PAD]
    x = w0 + _dot(jnp.tanh(zl), wd)
    softplus = jnp.maximum(-x, 0.0) + jnp.log(1.0 + jnp.exp(-jnp.abs(x)))
    lw = -jnp.exp(-softplus - 0.5)
    a = _sigmoid(a0 + _dot(zl, wa))
    g = _dot(_sigmoid(zl), wg)
    kk = k * k_k
    nrm = jnp.sqrt(_group_sum(kk * kk, gsum))
    kk = kk / jnp.maximum(nrm, 1e-12)
    k2 = k * (1.0 + (a - 1.0) * k_a)
    return [r, lw, k2, v, -kk, kk * a, g]


def _seg_rwkv_post(tv, bv):
    (y, r, k2, v, g), (r_k, gain, bias, gsum) = tv, bv
    mu = _group_sum(y, gsum) * (1.0 / R_HEAD)
    yc = y - mu
    var = _group_sum(yc * yc, gsum) * (1.0 / R_HEAD)
    yn = yc * lax.rsqrt(var + GN_EPS) * gain + bias
    bonus = _group_sum(r * k2 * r_k, gsum) * v
    return [(yn + bonus) * g]


def _seg_merge(tv, bv):
    (ga, gb, pa, pb), () = tv, bv
    return [_sigmoid(ga) * pa + _sigmoid(gb) * pb]


def _seg_mid(tv, bv):
    (x, mix), (gt1, g1, b1, sc2, sh2) = tv, bv
    h1 = _layer_norm(ALPHA * x + gt1 * mix, g1, b1, LN_EPS)
    return [h1, h1 * (1.0 + sc2) + sh2]


def _seg_relu2(tv, bv):
    (f1,), () = tv, bv
    return [jnp.square(jnp.maximum(f1, 0.0))]


def _seg_loss(tv, bv):
    (h1, ff, target), (gt2, g2, b2) = tv, bv
    out = _layer_norm(ALPHA * h1 + gt2 * ff, g2, b2, LN_EPS)
    err = jnp.square(out - target)
    return 0.5 * jnp.sum(jnp.mean(err, axis=-1))


_BNN = (((2,), (1,)), ((0,), (0,)))
_BNT = (((2,), (2,)), ((0,), (0,)))
_BTN = (((1,), (1,)), ((0,), (0,)))


def _tri_dot(x, dims):
    L = x.shape[0]
    tri = (lax.broadcasted_iota(jnp.int32, (L, L), 1) <= lax.broadcasted_iota(jnp.int32, (L, L), 0)).astype(F32)
    hi, lo = _split2(x)
    return _dot(tri, hi, dims) + _dot(tri, lo, dims)


@jax.custom_vjp
def _running_sum(x):
    return _tri_dot(x, _NN)


_running_sum.defvjp(lambda x: (_tri_dot(x, _NN), None), lambda _, ct: (_tri_dot(ct, _TN),))


def _cut_heads(x):
    return jnp.stack([x[:, R_HEAD * h:R_HEAD * (h + 1)] for h in range(R_HEADS)])


def _join_heads(x):
    return jnp.concatenate([x[h] for h in range(R_HEADS)], axis=1)


@jax.custom_vjp
def _split_heads(x):
    return _cut_heads(x)


@jax.custom_vjp
def _merge_heads(x):
    return _join_heads(x)


_split_heads.defvjp(lambda x: (_cut_heads(x), None), lambda _, ct: (_join_heads(ct),))
_merge_heads.defvjp(lambda x: (_join_heads(x), None), lambda _, ct: (_cut_heads(ct),))


def _inverse_pullback(inv, ct):
    return _dot(_dot(inv, ct, _BTN), inv, _BNT)


@jax.custom_vjp
def _unit_lower_inverse(n_mat):
    H, L, _ = n_mat.shape
    eye = lax.broadcasted_iota(jnp.int32, (H, L, L), 1) == lax.broadcasted_iota(jnp.int32, (H, L, L), 2)
    inv = jnp.where(eye, 1.0, 0.0) + n_mat
    pw = n_mat
    n = 2
    while n < L:
        pw = _dot(pw, pw, _BNN)
        inv = inv + _dot(inv, pw, _BNN)
        n *= 2
    return inv


def _unit_lower_inverse_fwd(n_mat):
    inv = _unit_lower_inverse(n_mat)
    return inv, inv


_unit_lower_inverse.defvjp(_unit_lower_inverse_fwd, lambda inv, ct: (_inverse_pullback(inv, ct),))


@jax.custom_vjp
def _known_inverse(n_mat, inv):
    return inv


_known_inverse.defvjp(lambda n_mat, inv: (inv, inv), lambda inv, ct: (_inverse_pullback(inv, ct), jnp.zeros_like(inv)))


def _scan_chunk(r, lw, k, v, a, b, s0, inv=None, with_inverse=False):
    L, H = r.shape[0], R_HEADS
    cs = _running_sum(lw)
    cs_end = cs[L - 1:L, :]
    p, p_inv, to_end = jnp.exp(cs), jnp.exp(-cs), jnp.exp(cs_end - cs)
    at, bt, kt, rt = [_split_heads(t) for t in (a * jnp.exp(cs - lw), b * p_inv, k * p_inv, r * p)]
    b_end, k_end, v = [_split_heads(t) for t in (b * to_end, k * to_end, v)]
    row = lax.broadcasted_iota(jnp.int32, (H, L, L), 1)
    col = lax.broadcasted_iota(jnp.int32, (H, L, L), 2)
    incl, strict = col <= row, col < row
    a_ab = jnp.where(strict, _dot(at, bt, _BNT), 0.0)
    a_ak = jnp.where(strict, _dot(at, kt, _BNT), 0.0)
    a_rb = jnp.where(incl, _dot(rt, bt, _BNT), 0.0)
    a_rk = jnp.where(incl, _dot(rt, kt, _BNT), 0.0)
    inv = _unit_lower_inverse(a_ab) if inv is None else _known_inverse(a_ab, inv)
    u = _dot(inv, _dot(at, s0, _BNT) + _dot(a_ak, v, _BNN), _BNN)
    y = _merge_heads(_dot(rt, s0, _BNT) + _dot(a_rb, u, _BNN) + _dot(a_rk, v, _BNN))
    s1 = s0 * _split_heads(jnp.exp(cs_end)) + _dot(u, b_end, _BTN) + _dot(v, k_end, _BTN)
    return (y, s1, inv) if with_inverse else (y, s1)


def _scan_fwd(r, lw, k, v, a, b, rider):
    T = r.shape[0]
    H, N, L, P = R_HEADS, R_HEAD, SCAN_CHUNK, SCAN_PER_STEP
    nc = T // L
    steps = nc // P
    ride_shape, ride_sems, _ = rider[1](rider[0])

    def body(r_ref, lw_ref, k_ref, v_ref, a_ref, b_ref, ride_in, y_ref, st_ref, inv_ref, ride_out, s_ref, *sem_refs):
        _ride(rider, pl.program_id(0), steps, ride_in, ride_out, sem_refs)

        @pl.when(pl.program_id(0) == 0)
        def _():
            s_ref[...] = jnp.zeros_like(s_ref)

        s0 = s_ref[...]
        for j in range(P):
            rows = pl.ds(j * L, L)
            st_ref[j] = s0
            y, s0, inv = _scan_chunk(*[t[rows, :] for t in (r_ref, lw_ref, k_ref, v_ref, a_ref, b_ref)], s0, with_inverse=True)
            y_ref[rows, :] = y
            inv_ref[j] = inv
        s_ref[...] = s0

    blk = pl.BlockSpec((P * L, R_WIDTH), lambda c: (c, 0))
    per_chunk = pl.BlockSpec((P, H, N, N), lambda c: (c, 0, 0, 0))
    return pl.pallas_call(
        body, name="scan_fwd", grid=(steps,), in_specs=[blk] * 6 + [_HBM], out_specs=[blk, per_chunk, per_chunk, _HBM],
        out_shape=[jax.ShapeDtypeStruct((T, R_WIDTH), F32)] + [jax.ShapeDtypeStruct((nc, H, N, N), F32)] * 2 + [ride_shape],
        scratch_shapes=[pltpu.VMEM((H, N, N), F32)] + list(ride_sems),
        compiler_params=_cparams(("arbitrary",)),
    )(r, lw, k, v, a, b, rider[0])


def _scan_bwd(r, lw, k, v, a, b, states, inverses, dy, rider):
    T = r.shape[0]
    H, N, L, P = R_HEADS, R_HEAD, SCAN_CHUNK, SCAN_PER_STEP
    nc = T // L
    steps = nc // P
    ride_shape, ride_sems, _ = rider[1](rider[0])

    def body(r_ref, lw_ref, k_ref, v_ref, a_ref, b_ref, st_ref, inv_ref, dy_ref, ride_in,
             dr_ref, dlw_ref, dk_ref, dv_ref, da_ref, db_ref, ride_out, ds_ref, *sem_refs):
        _ride(rider, pl.program_id(0), steps, ride_in, ride_out, sem_refs)

        @pl.when(pl.program_id(0) == 0)
        def _():
            ds_ref[...] = jnp.zeros_like(ds_ref)

        ds = ds_ref[...]
        for j in reversed(range(P)):
            rows = pl.ds(j * L, L)
            args = [t[rows, :] for t in (r_ref, lw_ref, k_ref, v_ref, a_ref, b_ref)] + [st_ref[j]]
            inv = inv_ref[j]
            _, pull = jax.vjp(lambda *xs, inv=inv: _scan_chunk(*xs, inv=inv), *args)
            grads = pull((dy_ref[rows, :], ds))
            for o_ref, g_ in zip((dr_ref, dlw_ref, dk_ref, dv_ref, da_ref, db_ref), grads[:6], strict=True):
                o_ref[rows, :] = g_
            ds = grads[6]
        ds_ref[...] = ds

    blk = pl.BlockSpec((P * L, R_WIDTH), lambda c: (steps - 1 - c, 0))
    per_chunk = pl.BlockSpec((P, H, N, N), lambda c: (steps - 1 - c, 0, 0, 0))
    return pl.pallas_call(
        body, name="scan_bwd", grid=(steps,), in_specs=[blk] * 6 + [per_chunk, per_chunk, blk, _HBM], out_specs=[blk] * 6 + [_HBM],
        out_shape=[jax.ShapeDtypeStruct((T, R_WIDTH), F32)] * 6 + [ride_shape],
        scratch_shapes=[pltpu.VMEM((H, N, N), F32)] + list(ride_sems),
        compiler_params=_cparams(("arbitrary",)),
    )(r, lw, k, v, a, b, states, inverses, dy, rider[0])


def _place():
    x, y, c = lax.axis_index("x"), lax.axis_index("y"), lax.axis_index("c")
    return x, y, c


def _gather_def(block):
    R, C = block.shape

    def phases(x_ref, out_ref, send_sems, recv_sems, local_sem):
        x, y, c = _place()
        me, sibling = (x, y, c), (x, y, 1 - c)
        chips = [(1 - x, y), (x, 1 - y), (1 - x, 1 - y)]

        def slot(px, py, pc):
            return out_ref.at[4 * px + 2 * py + pc]

        def copy(k, blk, to, src=None):
            return pltpu.make_async_remote_copy(
                src_ref=slot(*blk) if src is None else src, dst_ref=slot(*blk),
                send_sem=send_sems.at[k], recv_sem=recv_sems.at[k], device_id=to, device_id_type=_MESH_ID)

        mine = pltpu.make_async_copy(x_ref, slot(*me), local_sem)
        first = [copy(0, me, sibling, src=x_ref)]
        first += [copy(1 + j, me, (*chip, c), src=x_ref) for j, chip in enumerate(chips)]
        passed = [copy(4 + j, (*chip, c), sibling) for j, chip in enumerate(chips)]

        def begin():
            mine.start()
            for cp in first:
                cp.start()

        def forward():
            for j, chip in enumerate(chips):
                copy(1 + j, (*chip, c), me).wait_recv()
                passed[j].start()

        def finish():
            copy(0, sibling, me).wait_recv()
            for j, chip in enumerate(chips):
                copy(4 + j, (*chip, 1 - c), me).wait_recv()
            for cp in first + passed:
                cp.wait_send()
            mine.wait()

        return [begin, forward, finish]

    sems = [pltpu.SemaphoreType.DMA((7,)), pltpu.SemaphoreType.DMA((7,)), pltpu.SemaphoreType.DMA]
    return jax.ShapeDtypeStruct((_N_DEV, R, C), block.dtype), sems, phases


def _sibling_def(blocks):
    _, R, C = blocks.shape

    def phases(x_ref, out_ref, send_sems, recv_sems):
        x, y, c = _place()
        copies = [pltpu.make_async_remote_copy(
            src_ref=x_ref.at[2 * q + (1 - c)], dst_ref=out_ref.at[q], send_sem=send_sems.at[q], recv_sem=recv_sems.at[q],
            device_id=(x, y, 1 - c), device_id_type=_MESH_ID) for q in range(4)]

        def begin():
            for cp in copies:
                cp.start()

        def finish():
            for cp in copies:
                cp.wait()

        return [begin, finish]

    return jax.ShapeDtypeStruct((4, R, C), blocks.dtype), [pltpu.SemaphoreType.DMA((4,)), pltpu.SemaphoreType.DMA((4,))], phases


def _chips_def(partials):
    def phases(x_ref, out_ref, send_sems, recv_sems, local_sem):
        x, y, c = _place()
        my_chip = 2 * x + y
        mine = pltpu.make_async_copy(x_ref.at[my_chip], out_ref.at[my_chip], local_sem)
        copies = []
        for rel in range(1, 4):
            px, py = (1 - x if (rel >> 1) & 1 else x), (1 - y if rel & 1 else y)
            copies.append(pltpu.make_async_remote_copy(
                src_ref=x_ref.at[2 * px + py], dst_ref=out_ref.at[my_chip],
                send_sem=send_sems.at[rel - 1], recv_sem=recv_sems.at[rel - 1],
                device_id=(px, py, c), device_id_type=_MESH_ID))

        def begin():
            mine.start()
            for cp in copies:
                cp.start()

        def finish():
            for cp in copies:
                cp.wait()
            mine.wait()

        return [begin, finish]

    sems = [pltpu.SemaphoreType.DMA((3,)), pltpu.SemaphoreType.DMA((3,)), pltpu.SemaphoreType.DMA]
    return jax.ShapeDtypeStruct(partials.shape, partials.dtype), sems, phases


_HBM = pl.BlockSpec(memory_space=pltpu.HBM)


def _exchange(name, array, definition):
    out_shape, sems, phases = definition(array)

    def body(x_ref, out_ref, *sem_refs):
        for phase in phases(x_ref, out_ref, *sem_refs):
            phase()

    return pl.pallas_call(body, name=name, in_specs=[_HBM], out_specs=_HBM, out_shape=out_shape, scratch_shapes=sems)(array)


def _ride(rider, step, nsteps, x_ref, out_ref, sem_refs):
    array, definition, fractions = rider
    for phase, frac in zip(definition(array)[2](x_ref, out_ref, *sem_refs), fractions, strict=True):
        pl.when(step == min(int(frac * nsteps), nsteps - 1))(phase)


def _small_gather_def(block):
    R, C = block.shape

    def phases(x_ref, out_ref, send_sems, recv_sems, local_sem):
        x, y, c = _place()
        me = 4 * x + 2 * y + c
        mine = pltpu.make_async_copy(x_ref, out_ref.at[me], local_sem)
        copies = []
        for rel in range(1, _N_DEV):
            px, py, pc = (1 - x if rel & 4 else x), (1 - y if rel & 2 else y), (1 - c if rel & 1 else c)
            copies.append(pltpu.make_async_remote_copy(
                src_ref=x_ref, dst_ref=out_ref.at[me], send_sem=send_sems.at[rel - 1], recv_sem=recv_sems.at[rel - 1],
                device_id=(px, py, pc), device_id_type=_MESH_ID))

        def begin():
            mine.start()
            for cp in copies:
                cp.start()

        def finish():
            for cp in copies:
                cp.wait()
            mine.wait()

        return [begin, finish]

    sems = [pltpu.SemaphoreType.DMA((7,)), pltpu.SemaphoreType.DMA((7,)), pltpu.SemaphoreType.DMA]
    return jax.ShapeDtypeStruct((_N_DEV, R, C), block.dtype), sems, phases


def _all_gather(name, block):
    return _exchange(name, block, _small_gather_def)


def _chip_partials(name, blocks, from_sibling, tile, out_dtype):
    _, R, C = blocks.shape

    def body(x_ref, s_ref, o_ref):
        c = lax.axis_index("c")
        for q in range(4):
            o_ref[q] = (x_ref[2 * q + c].astype(F32) + s_ref[q].astype(F32)).astype(o_ref.dtype)

    return pl.pallas_call(
        body, name=name, grid=(R // tile,),
        in_specs=[pl.BlockSpec((_N_DEV, tile, C), lambda i: (0, i, 0)), pl.BlockSpec((4, tile, C), lambda i: (0, i, 0))],
        out_specs=pl.BlockSpec((4, tile, C), lambda i: (0, i, 0)), out_shape=jax.ShapeDtypeStruct((4, R, C), out_dtype),
        compiler_params=_cparams(("parallel",)),
    )(blocks, from_sibling)


def _sum_leading(name, x, tile=None):
    n, R, C = x.shape
    tile = tile or _pick(R, (512, 256, 128, 64, 32, 16, 8))

    def body(x_ref, o_ref):
        acc = x_ref[0].astype(F32)
        for k in range(1, n):
            acc = acc + x_ref[k].astype(F32)
        o_ref[...] = acc

    return pl.pallas_call(
        body, name=name, grid=(R // tile,), in_specs=[pl.BlockSpec((n, tile, C), lambda i: (0, i, 0))],
        out_specs=pl.BlockSpec((tile, C), lambda i: (i, 0)), out_shape=jax.ShapeDtypeStruct((R, C), F32),
        compiler_params=_cparams(("parallel",)),
    )(x)


def _adamw_update(w_, g_, m_, v_):
    m2 = ADAM_B1 * m_ + (1.0 - ADAM_B1) * g_
    v2 = ADAM_B2 * v_ + (1.0 - ADAM_B2) * jnp.square(g_)
    m_hat = m2 / (1.0 - ADAM_B1 ** ADAM_STEP)
    v_hat = v2 / (1.0 - ADAM_B2 ** ADAM_STEP)
    delta = -ADAM_LR * (m_hat / (jnp.sqrt(v_hat) + ADAM_EPS) + ADAM_WD * w_)
    return delta, m2, v2


def _adamw(name, w, g, m, v):
    R, C = w.shape
    tile = _pick(R, (256, 128, 64, 32, 16, 8))
    outs, _ = _rowwise(name, lambda tv, bv: (list(_adamw_update(*tv)), []), [w, g, m, v], [], [(C, F32)] * 3, [], tile)
    return outs


def _adamw_many(name, ws, gs, ms, vs):
    n = len(ws)

    def body(*refs):
        ins, outs = refs[:4 * n], refs[4 * n:]
        for i in range(n):
            res = _adamw_update(ins[i][...], ins[n + i][...], ins[2 * n + i][...], ins[3 * n + i][...])
            for j in range(3):
                outs[j * n + i][...] = res[j]

    out_shape = [jax.ShapeDtypeStruct(w.shape, F32) for w in ws] * 3
    res = pl.pallas_call(body, name=name, out_shape=out_shape, compiler_params=_cparams())(*ws, *gs, *ms, *vs)
    return res[:n], res[n:2 * n], res[2 * n:]


def _pack_rows(arrs, lanes=128, row_mult=8):
    flat, places, off = [], [], 0
    for a_ in arrs:
        n = a_.size
        flat.append(a_.reshape(-1).astype(F32))
        places.append((off, n, a_.shape))
        off += n
    total = -(-off // (lanes * row_mult)) * (lanes * row_mult)
    if total > off:
        flat.append(jnp.zeros((total - off,), F32))
    return jnp.concatenate(flat).reshape(total // lanes, lanes), places


def _unpack_rows(packed, places):
    flat = packed.reshape(-1)
    return [flat[o:o + n].reshape(s) for (o, n, s) in places]


_WEIGHTS = ['w_ada', 'b_ada', 'w_in', 'b_in', 'g_ln_v', 'b_ln_v', 'w_spatial', 'b_spatial', 'mu_shift', 'w0', 'w_decay_up', 'a0',
            'w_aaa_up', 'w_gate_up', 'k_k', 'k_a', 'r_k', 'gn_gain', 'gn_bias', 'w_branch_a', 'w_branch_b', 'w_out', 'b_out',
            'ln1_g', 'ln1_b', 'w_ff1', 'b_ff1', 'w_ff2', 'b_ff2', 'ln2_g', 'ln2_b']
_BIG = {'w_ff1': (0, 512), 'w_ff2': (512, 512), 'w_out': (1024, 128), 'w_branch_a': (1152, 64), 'w_branch_b': (1216, 64), 'w_in': (1280, 640)}
_LATER_ROWS = 1280
_CUT_BY_COLS = ('w_ff1', 'w_in', 'w_branch_a', 'w_branch_b')
IN_SHARD = IN_COLS // _N_DEV
_LORA = {'w_decay_up': (0, LORA_W), 'w_aaa_up': (LORA_W, LORA_A), 'w_gate_up': (LORA_W + LORA_A, LORA_G)}
_COMM_DTYPE = jnp.bfloat16


def _pad_rows(a, rows):
    return jnp.pad(a, ((0, rows - a.shape[0]),) + ((0, 0),) * (a.ndim - 1))


def _pack_big(shards):
    blocks = []
    for n, (_, rows) in _BIG.items():
        a = shards[n].T if n in _CUT_BY_COLS else shards[n]
        blocks.append(_pad_rows(a.reshape(-1, D_MODEL), rows))
    return jnp.concatenate(blocks, axis=0)


def _unpack_big(block, like):
    out = {}
    for n, (r0, _) in _BIG.items():
        rr, cc = like[n].shape
        if n in _CUT_BY_COLS:
            out[n] = block[r0:r0 + rr * cc // D_MODEL].reshape(cc, rr).T
        else:
            out[n] = block[r0:r0 + rr]
    return out


def _to_padded(a, axis):
    g_end = 2 * G_WIDTH
    r_end = g_end + RW_USED
    take = lambda lo, hi: lax.slice_in_dim(a, lo, hi, axis=axis)
    zshape = list(a.shape)
    zshape[axis] = RW_COLS - RW_USED
    return jnp.concatenate([take(r_end, IN_COLS), take(g_end, r_end), jnp.zeros(zshape, a.dtype), take(0, g_end)], axis=axis)


def _from_padded(a, axis):
    take = lambda lo, hi: lax.slice_in_dim(a, lo, hi, axis=axis)
    return jnp.concatenate([take(2 * D_MODEL + RW_COLS, P_COLS), take(2 * D_MODEL, 2 * D_MODEL + RW_USED), take(0, 2 * D_MODEL)], axis=axis)


def _step(p, m, v, x, c, target):
    T = x.shape[0]
    xi, yi, ci = _place()
    me = 4 * xi + 2 * yi + ci
    tile = _pick(T, (512, 256))

    lane = jnp.arange(R_WIDTH)
    gsum = (lane[:128, None] // R_HEAD == lane[None, :128] // R_HEAD).astype(F32)
    expand = (jnp.arange(128)[:, None] == (lane[None, :] // (G_WIDTH // 8))).astype(F32)

    (c_act,), _ = _rowwise("silu_c", lambda tv, bv: ([tv[0] * _sigmoid(tv[0])], []), [c], [], [(D_MODEL, F32)], [], 1)
    small, places = _pack_rows([c_act, p['w_decay_up'], p['w_aaa_up'], p['w_gate_up']])
    small_all = _all_gather("gather_small", small)
    per_dev = [_unpack_rows(small_all[d], places) for d in range(_N_DEV)]
    c_act_all = _pad_rows(jnp.concatenate([pd[0] for pd in per_dev], axis=0), 16)
    lora_full = {n: jnp.concatenate([pd[i + 1] for pd in per_dev], axis=1) for i, n in enumerate(_LORA)}
    lora_pad = {n: jnp.zeros((LORA_PAD, R_WIDTH), F32).at[r0:r0 + nr].set(lora_full[n]) for n, (r0, nr) in _LORA.items()}

    big_names = list(_BIG)
    my_rows = _pack_big(p).astype(_MXU_DTYPE)
    b_in_p = _to_padded(p['b_in'], 1)
    mu_p = jnp.concatenate([p['mu_shift'], jnp.zeros((1, RW_COLS - RW_USED), F32)], axis=1)

    b_ada_mine = lax.dynamic_slice(p['b_ada'], (0, me * 768), (1, 768))
    mod_cols = _mm("ada_mod", c_act_all, p['w_ada'], "nn", bias=b_ada_mine)
    mod_all = _all_gather("gather_mod", mod_cols)
    mod = lax.dynamic_index_in_dim(mod_all, me, axis=1, keepdims=False).reshape(1, 6 * D_MODEL)
    sh1, sc1, gt1, sh2, sc2, gt2 = [mod[:, i * D_MODEL:(i + 1) * D_MODEL] for i in range(6)]

    (h,), _, w_in_all = _rowwise("modulate1", lambda tv, bv: (_seg_modulate(tv, bv), []), [x], [sc1, sh1], [(D_MODEL, _MXU_DTYPE)], [], tile,
                                 rider=(my_rows[_LATER_ROWS:], _gather_def, (0.0, 0.5, 1.0)))
    w_in_t = _to_padded(w_in_all[:, :IN_SHARD].reshape(IN_COLS, D_MODEL), 0)
    proj = _mm("in_proj", h, w_in_t, "nt", bias=b_in_p)
    ws = p['w_spatial']
    b_tg = jnp.zeros((G_CHUNK, 128), F32).at[:, :8].set(p['b_spatial'].T)
    gmlp_b = [p['g_ln_v'], p['b_ln_v'], ws, b_tg, expand]
    z_gmlp = (proj, 2 * G_WIDTH, 4)
    (ya,), _ = _rowwise("gmlp", lambda tv, bv: (_seg_gmlp(tv, bv), []), [z_gmlp], gmlp_b, [(G_WIDTH, F32)], [], G_CHUNK)
    z_rw = (proj, RW_COLS, 1)
    z_rw_halo = ("halo", proj, RW_COLS, 1)
    pre_b = [mu_p, p['w0'], lora_pad['w_decay_up'], p['a0'], lora_pad['w_aaa_up'], lora_pad['w_gate_up'], p['k_k'], p['k_a'], gsum]

    def pre_fwd(tv, bv):
        z_t, halo_t = tv
        return _seg_rwkv_pre([z_t, _shift_down(z_t, halo_t, pl.program_id(0) == 0)], bv), []

    pre_out, _ = _rowwise("rwkv_pre", pre_fwd, [z_rw, z_rw_halo], pre_b, [(R_WIDTH, F32)] * 7, [], tile)
    r_, lw_, k2_, v_, a_, b_, g_ = pre_out
    scan_in = (r_, lw_, k2_, v_, a_, b_)
    y_, states, inverses, later_all = _scan_fwd(*scan_in, rider=(my_rows[:_LATER_ROWS], _gather_def, (0.0, 0.875, 1.0)))

    def whole(n, rows):
        r0 = _BIG[n][0]
        return later_all[:, r0:r0 + rows].reshape(_N_DEV * rows, D_MODEL)

    w_ff1_t, w_ff2, w_out = whole('w_ff1', 512), whole('w_ff2', 512), whole('w_out', 128)
    w_ba_t = whole('w_branch_a', 64).reshape(D_MODEL, G_WIDTH)
    w_bb_t = whole('w_branch_b', 64).reshape(D_MODEL, R_WIDTH)
    post_b = [p['r_k'].reshape(1, R_WIDTH), p['gn_gain'], p['gn_bias'], gsum]
    (yb,), _ = _rowwise("rwkv_post", lambda tv, bv: (_seg_rwkv_post(tv, bv), []), [y_, r_, k2_, v_, g_], post_b, [(R_WIDTH, F32)], [], tile)
    pa = _mm("branch_a", ya, w_ba_t, "nt", out_dtype=_MXU_DTYPE)
    pb = _mm("branch_b", yb, w_bb_t, "nt", out_dtype=_MXU_DTYPE)
    gates = [(proj, D_MODEL, 0), (proj, D_MODEL, 1)]
    (merged,), _ = _rowwise("merge", lambda tv, bv: (_seg_merge(tv, bv), []), gates + [pa, pb], [], [(D_MODEL, _MXU_DTYPE)], [], tile)
    mix = _mm("out_proj", merged, w_out, "nn", bias=p['b_out'])
    mid_b = [gt1, p['ln1_g'], p['ln1_b'], sc2, sh2]
    (h1, h2in), _ = _rowwise("mid", lambda tv, bv: (_seg_mid(tv, bv), []), [x, mix], mid_b, [(D_MODEL, F32), (D_MODEL, _MXU_DTYPE)], [], tile)
    act = _mm("ff1", h2in, w_ff1_t, "nt", bias=p['b_ff1'], epi=lambda t: _seg_relu2([t], [])[0], epi_dtype=_MXU_DTYPE, raw=False)
    ff = _mm("ff2", act, w_ff2, "nn", bias=p['b_ff2'])

    def loss_fn(tv, bv):
        h1_t, ff_t, tgt = tv
        val, grads = jax.value_and_grad(lambda a0_, a1_, b0_, b1_, b2_: _seg_loss([a0_, a1_, tgt], [b0_, b1_, b2_]), argnums=(0, 1, 2, 3, 4))(h1_t, ff_t, *bv)
        return [grads[0], grads[1]], [grads[2], grads[3], grads[4], jnp.sum(grads[1], axis=0, keepdims=True), jnp.full((1, 128), val, F32)]

    (dh1_a, dff), (d_gt2, d_ln2_g, d_ln2_b, d_b_ff2, loss_row) = _rowwise(
        "loss", loss_fn, [h1, ff, target], [gt2, p['ln2_g'], p['ln2_b']], [(D_MODEL, F32), (D_MODEL, _MXU_DTYPE)], [(1, D_MODEL)] * 4 + [(1, 128)], tile)

    g = {}
    g['ln2_g'], g['ln2_b'], g['b_ff2'] = d_ln2_g, d_ln2_b, d_b_ff2
    gw = {}
    gw['w_ff2'] = _mm("g_w_ff2", act, dff, "tn", out_dtype=_COMM_DTYPE)
    df1, g['b_ff1'] = _mm("d_act", dff, w_ff2, "nt", beside=act, epi=lambda d_, a_: d_ * (2.0 * jnp.sqrt(a_.astype(F32))),
                          epi_dtype=_MXU_DTYPE, colsum=True)
    gw['w_ff1'] = _mm("g_w_ff1", df1, h2in, "tn", out_dtype=_COMM_DTYPE)
    dh2in = _mm("d_h2in", df1, w_ff1_t, "nn")
    (dx_a, dmix), (d_gt1, g['ln1_g'], g['ln1_b'], d_sc2, d_sh2), (g['b_out'],) = _rowwise_vjp(
        "mid_bwd", _seg_mid, [x, mix], mid_b, [dh1_a, dh2in], tile, [0, 1], [0, 1, 2, 3, 4], t_dtypes=[F32, _MXU_DTYPE], colsum=[1])
    gw['w_out'] = _mm("g_w_out", merged, dmix, "tn", out_dtype=_COMM_DTYPE)
    dmerged = _mm("d_merged", dmix, w_out, "nt", out_dtype=_MXU_DTYPE)
    (dproj, dpa, dpb), _, (cs_gates,) = _rowwise_vjp(
        "merge_bwd", _seg_merge, gates + [pa, pb], [], [dmerged], tile, [0, 1, 2, 3], [], t_dtypes=[_MXU_DTYPE] * 3, colsum=[0],
        finish=lambda dts, sc: [jnp.concatenate(dts[:2], axis=1), dts[2], dts[3]], out_widths=[2 * D_MODEL, D_MODEL, D_MODEL],
        into={0: (lax.empty((T, P_COLS), _MXU_DTYPE), 0)})
    gw['w_branch_a'] = _mm("g_w_branch_a", dpa, ya, "tn", out_dtype=_COMM_DTYPE)
    gw['w_branch_b'] = _mm("g_w_branch_b", dpb, yb, "tn", out_dtype=_COMM_DTYPE)
    dya = _mm("d_ya", dpa, w_ba_t, "nn")
    dyb = _mm("d_yb", dpb, w_bb_t, "nn")
    def send_rows(names):
        parts = []
        for n in names:
            per_dev = gw[n].reshape(_N_DEV, -1, D_MODEL)
            parts.append(jnp.pad(per_dev, ((0, 0), (0, _BIG[n][1] - per_dev.shape[1]), (0, 0))))
        return jnp.concatenate(parts, axis=1) if len(parts) > 1 else parts[0]

    send_early = send_rows(big_names[:-1])
    (dy, dr1, dk1, dv1, dg_), (d_r_k, g['gn_gain'], g['gn_bias']), _, sibling_early = _rowwise_vjp(
        "rwkv_post_bwd", _seg_rwkv_post, [y_, r_, k2_, v_, g_], post_b, [dyb], tile, [0, 1, 2, 3, 4], [0, 1, 2],
        rider=(send_early, _sibling_def, (0.0, 1.0)))
    g['r_k'] = d_r_k
    partials_early = _chip_partials("chip_partials_early", send_early, sibling_early, 128, _COMM_DTYPE)
    dr2, dlw, dk2, dv2, da, db, landed_early = _scan_bwd(*scan_in, states, inverses, dy, rider=(partials_early, _chips_def, (0.0, 1.0)))
    pre_tile = 256
    last_step = T // pre_tile - 1

    def pre_prep(prim):
        z_t, halo_t = prim
        return [z_t, _shift_down(z_t, halo_t, pl.program_id(0) == last_step)]

    def pre_finish(dts, sc):
        dz_direct, dprev = dts
        (row_after,) = sc

        @pl.when(pl.program_id(0) == 0)
        def _():
            row_after[...] = jnp.zeros_like(row_after)

        dz = dz_direct + _shift_up(dprev, row_after[...])
        row_after[...] = dprev[:_HALO]
        return [dz]

    small_names = [n for n in _WEIGHTS if n not in _BIG and n not in ('w_ada', 'b_ada')]
    (dproj,), (g['g_ln_v'], g['b_ln_v'], g['w_spatial'], d_b_tg), (cs_g,) = _rowwise_vjp(
        "gmlp_bwd", _seg_gmlp, [z_gmlp], gmlp_b, [dya], G_CHUNK, [0], [0, 1, 2, 3], t_dtypes=[_MXU_DTYPE], colsum=[0],
        into={0: (dproj, 4)})
    g['b_spatial'] = d_b_tg[:, :8].T
    names_1 = [n for n in small_names if n in g]
    packed_1, places_1 = _pack_rows([g[n] for n in names_1] + [loss_row], row_mult=256)
    (dproj,), (d_mu, g['w0'], d_wd, g['a0'], d_wa, d_wg, g['k_k'], g['k_a']), (cs_rw,), small_all_1 = _rowwise_vjp(
        "rwkv_pre_bwd", _seg_rwkv_pre, [z_rw, z_rw_halo], pre_b, [[dr1, dr2], dlw, [dk1, dk2], [dv1, dv2], da, db, dg_], pre_tile,
        [0, 1], [0, 1, 2, 3, 4, 5, 6, 7], t_dtypes=[_MXU_DTYPE], colsum=[0], prep=pre_prep, finish=pre_finish,
        out_widths=[RW_COLS], reverse=True, scratch=[(_HALO, RW_COLS)], into={0: (dproj, 1)},
        rider=(packed_1, _gather_def, (0.0, 0.6, 1.0)))
    g['mu_shift'] = d_mu[:, :RW_USED]
    for n, d_ in (('w_decay_up', d_wd), ('w_aaa_up', d_wa), ('w_gate_up', d_wg)):
        r0, nr = _LORA[n]
        g[n] = d_[r0:r0 + nr]
    g['b_in'] = _from_padded(jnp.concatenate([cs_gates, cs_rw, cs_g], axis=1), 1)
    names_2 = [n for n in small_names if n not in names_1]
    packed_2, places_2 = _pack_rows([g[n] for n in names_2], row_mult=256)
    gw_in_t, small_all_2 = _mm("g_w_in", dproj, h, "tn", out_dtype=_COMM_DTYPE, rider=(packed_2, _gather_def, (0.0, 0.6, 1.0)))
    gw['w_in'] = _from_padded(gw_in_t, 0)
    send_late = send_rows(big_names[-1:])
    sibling_late = _exchange("pair_exchange_late", send_late, _sibling_def)
    partials_late = _chip_partials("chip_partials_late", send_late, sibling_late, 128, _COMM_DTYPE)
    dh, landed_late = _mm("d_h", dproj, w_in_t, "nn", rider=(partials_late, _chips_def, (0.0, 1.0)))

    def mod1_bwd(tv, bv):
        x_t, dh_t, dxa_t = tv
        (sc,) = bv
        return [dxa_t + dh_t * (1.0 + sc)], [jnp.sum(dh_t * x_t, axis=0, keepdims=True), jnp.sum(dh_t, axis=0, keepdims=True)]

    (grad_x,), (d_sc1, d_sh1) = _rowwise("modulate1_bwd", mod1_bwd, [x, dh, dx_a], [sc1], [(D_MODEL, F32)], [(1, D_MODEL)] * 2, tile)

    dmod = jnp.concatenate([d_sh1, d_sc1, d_gt1, d_sh2, d_sc2, d_gt2], axis=1).reshape(6 * D_MODEL // 128, 128)
    dmod_all = _all_gather("gather_dmod", dmod)
    g['b_ada'] = _sum_leading("sum_dmod", dmod_all).reshape(1, 6 * D_MODEL)
    dmod_mine = lax.dynamic_slice(dmod_all.reshape(_N_DEV, 6 * D_MODEL), (0, me * 768), (_N_DEV, 768))
    g_w_ada = _mm("g_w_ada", c_act_all, _pad_rows(dmod_mine, 16), "tn")

    sums_1 = _unpack_rows(_sum_leading("sum_small_1", small_all_1), places_1)
    loss = sums_1.pop()[0, 0]
    sums_2 = _unpack_rows(_sum_leading("sum_small_2", small_all_2), places_2)
    g.update(zip(names_1 + names_2, sums_1 + sums_2, strict=True))
    for n in _LORA:
        g[n] = lax.dynamic_slice(g[n], (0, me * R_HEAD), (g[n].shape[0], R_HEAD))
    g['w_ada'] = g_w_ada

    summed = jnp.concatenate([_sum_leading("sum_early", landed_early), _sum_leading("sum_late", landed_late)], axis=0)
    g.update(_unpack_big(summed, p))

    delta, new_m, new_v = {}, {}, {}
    own_call = ['w_ada'] + big_names
    for n in own_call:
        if n == 'w_in':
            r0 = _BIG[n][0]
            outs_t = _adamw("adamw_" + n, p[n].T, summed[r0:r0 + IN_SHARD], m[n].T, v[n].T)
            delta[n], new_m[n], new_v[n] = [o.T for o in outs_t]
        else:
            delta[n], new_m[n], new_v[n] = _adamw("adamw_" + n, p[n], g[n], m[n], v[n])
    rest = [n for n in _WEIGHTS if n not in own_call]
    outs = _adamw_many("adamw_rest", *[[d[n].reshape(p[n].shape) for n in rest] for d in (p, g, m, v)])
    for d, o in zip((delta, new_m, new_v), outs, strict=True):
        d.update(zip(rest, o, strict=True))
    return loss, grad_x, g, delta, new_m, new_v


def kernel(x, c, w_ada, b_ada, w_in, b_in, g_ln_v, b_ln_v, w_spatial, b_spatial, mu_shift, w0, w_decay_up, a0, w_aaa_up, w_gate_up, k_k, k_a, r_k, gn_gain, gn_bias, w_branch_a, w_branch_b, w_out, b_out, ln1_g, ln1_b, w_ff1, b_ff1, w_ff2, b_ff2, ln2_g, ln2_b, loss_target, m_w_ada, m_b_ada, m_w_in, m_b_in, m_g_ln_v, m_b_ln_v, m_w_spatial, m_b_spatial, m_mu_shift, m_w0, m_w_decay_up, m_a0, m_w_aaa_up, m_w_gate_up, m_k_k, m_k_a, m_r_k, m_gn_gain, m_gn_bias, m_w_branch_a, m_w_branch_b, m_w_out, m_b_out, m_ln1_g, m_ln1_b, m_w_ff1, m_b_ff1, m_w_ff2, m_b_ff2, m_ln2_g, m_ln2_b, v_w_ada, v_b_ada, v_w_in, v_b_in, v_g_ln_v, v_b_ln_v, v_w_spatial, v_b_spatial, v_mu_shift, v_w0, v_w_decay_up, v_a0, v_w_aaa_up, v_w_gate_up, v_k_k, v_k_a, v_r_k, v_gn_gain, v_gn_bias, v_w_branch_a, v_w_branch_b, v_w_out, v_b_out, v_ln1_g, v_ln1_b, v_w_ff1, v_b_ff1, v_w_ff2, v_b_ff2, v_ln2_g, v_ln2_b):
    given = dict(locals())
    shapes = {n: given[n].shape for n in _WEIGHTS}
    def two_d(a_):
        a_ = a_[0]
        return a_.reshape(1, -1) if a_.ndim == 1 else a_
    p = {n: two_d(given[n]) for n in _WEIGHTS}
    m = {n: two_d(given["m_" + n]) for n in _WEIGHTS}
    v = {n: two_d(given["v_" + n]) for n in _WEIGHTS}
    loss, grad_x, g, delta, new_m, new_v = _step(p, m, v, x[0], c, loss_target[0])
    outs = [loss, grad_x[None]]
    for d in (g, delta, new_m, new_v):
        outs += [d[n].reshape(shapes[n]) for n in _WEIGHTS]
    return tuple(outs)
```

```python
import functools

import jax
import jax.numpy as jnp
from jax import lax
from jax.experimental import pallas as pl
from jax.experimental.pallas import tpu as pltpu

F32 = jnp.float32
_MXU_DTYPE = jnp.bfloat16
_HI = lax.Precision.HIGHEST
_VMEM_LIMIT = 48 * 1024 * 1024
_MESH_ID = pl.DeviceIdType.MESH
_N_DEV = 8

D_MODEL = 1024
G_WIDTH = 512
G_CHUNK = 128
R_WIDTH = 512
R_HEADS = 8
R_HEAD = 64
LORA_W, LORA_A, LORA_G = 32, 32, 96
D_FF = 4096
ALPHA = 2.0 ** 0.25
LN_EPS = 1e-5
GN_EPS = 64e-5
SCAN_CHUNK = 64
SCAN_PER_STEP = 4
ADAM_LR, ADAM_B1, ADAM_B2, ADAM_EPS, ADAM_WD, ADAM_STEP = 0.001, 0.9, 0.999, 1e-08, 0.01, 10

P_COLS = 5120
RW_COLS = 2048
RW_USED = 3 * R_WIDTH + LORA_W + LORA_A + LORA_G
LORA_PAD = 256
IN_COLS = 2 * G_WIDTH + RW_USED + 2 * D_MODEL


def _cparams(sem=None, **kw):
    if sem is not None:
        kw["dimension_semantics"] = sem
    return pltpu.CompilerParams(vmem_limit_bytes=_VMEM_LIMIT, **kw)


def _dot(a, b, dims=(((1,), (0,)), ((), ())), hi=False):
    if hi:
        return lax.dot_general(a.astype(F32), b.astype(F32), dims, precision=_HI, preferred_element_type=F32)
    return lax.dot_general(a.astype(_MXU_DTYPE), b.astype(_MXU_DTYPE), dims, preferred_element_type=F32)


_NN = (((1,), (0,)), ((), ()))
_NT = (((1,), (1,)), ((), ()))
_TN = (((0,), (0,)), ((), ()))


def _pick(n, pref):
    for t in pref:
        if n % t == 0:
            return t
    return n


def _mm(name, a, b, mode, bias=None, out_dtype=F32, epi=None, epi_dtype=None, raw=True, beside=None, colsum=False, rider=None,
        tm=None, tn=None, tk=None):
    if mode == "nn":
        (M, K), (_, N) = a.shape, b.shape
    elif mode == "nt":
        (M, K), (N, _) = a.shape, b.shape
    else:
        (K, M), (_, N) = a.shape, b.shape
    tm = tm or _pick(M, (2048, 1280, 1024, 512, 256, 128, 64, 32, 16, 8))
    tn = tn or _pick(N, (1024, 512, 640, 384, 256, 128))
    tk = tk or _pick(K, (1024, 512, 256, 128))
    nk = K // tk
    dims = {"nn": _NN, "nt": _NT, "tn": _TN}[mode]
    a_spec = pl.BlockSpec((tk, tm), lambda i, j, k: (k, i)) if mode == "tn" else pl.BlockSpec((tm, tk), lambda i, j, k: (i, k))
    b_spec = pl.BlockSpec((tn, tk), lambda i, j, k: (j, k)) if mode == "nt" else pl.BlockSpec((tk, tn), lambda i, j, k: (k, j))
    o_spec = pl.BlockSpec((tm, tn), lambda i, j, k: (i, j))
    has_bias, has_beside = bias is not None, beside is not None
    two = epi is not None and not has_beside and raw
    only_epi = epi is not None and not has_beside and not raw
    grid = (M // tm, N // tn, nk)
    ride_shape, ride_sems, _ = rider[1](rider[0]) if rider else (None, [], None)

    def body(*refs):
        refs = list(refs)
        n_in = 2 + has_bias + has_beside
        if rider:
            sem_refs = [refs.pop() for _ in ride_sems][::-1]
            ride_out = refs.pop(n_in + 1 + 1 + two)
            ride_in = refs.pop(n_in)
            step = (pl.program_id(0) * grid[1] + pl.program_id(1)) * grid[2] + pl.program_id(2)
            _ride(rider, step, grid[0] * grid[1] * grid[2], ride_in, ride_out, sem_refs)
        a_ref, b_ref = refs[0], refs[1]
        bias_ref = refs[2] if has_bias else None
        beside_ref = refs[n_in - 1] if has_beside else None
        outs = refs[n_in:]
        o_ref, acc_ref = outs[0], outs[-1]
        k = pl.program_id(2)

        @pl.when(k == 0)
        def _():
            acc_ref[...] = jnp.zeros_like(acc_ref)

        acc_ref[...] += _dot(a_ref[...], b_ref[...], dims)

        @pl.when(k == nk - 1)
        def _():
            res = acc_ref[...]
            if has_bias:
                res = res + bias_ref[...]
            if has_beside:
                val = epi(res, beside_ref[...])
                o_ref[...] = val.astype(o_ref.dtype)
                if colsum:
                    outs[1][...] = jnp.broadcast_to(jnp.sum(val, axis=0, keepdims=True), outs[1].shape)
            elif only_epi:
                o_ref[...] = epi(res).astype(o_ref.dtype)
            else:
                o_ref[...] = res.astype(o_ref.dtype)
                if two:
                    outs[1][...] = epi(res).astype(outs[1].dtype)

    in_specs = [a_spec, b_spec]
    args = [a, b]
    if has_bias:
        in_specs.append(pl.BlockSpec((1, tn), lambda i, j, k: (0, j)))
        args.append(bias)
    if has_beside:
        in_specs.append(o_spec)
        args.append(beside)
    out_shape = [jax.ShapeDtypeStruct((M, N), epi_dtype if (has_beside or only_epi) else out_dtype)]
    out_specs = [o_spec]
    if two:
        out_shape.append(jax.ShapeDtypeStruct((M, N), epi_dtype))
        out_specs.append(o_spec)
    if colsum:
        assert has_beside and not rider
        out_shape.append(jax.ShapeDtypeStruct((grid[0] * 8, N), F32))
        out_specs.append(pl.BlockSpec((8, tn), lambda i, j, k: (i, j)))
    if rider:
        in_specs.append(_HBM)
        args.append(rider[0])
        out_shape.append(ride_shape)
        out_specs.append(_HBM)
    res = pl.pallas_call(
        body, name=name, grid=grid, in_specs=in_specs, out_specs=out_specs, out_shape=out_shape,
        scratch_shapes=[pltpu.VMEM((tm, tn), F32)] + list(ride_sems),
        compiler_params=_cparams(("arbitrary",) * 3 if rider else ("parallel", "parallel", "arbitrary")),
    )(*args)
    if colsum:
        return res[0], _sum_leading(name + "_colsum", res[1].reshape(grid[0], 8, N))[0:1]
    return res if (two or rider) else res[0]


_HALO = 8


def _rowwise(name, fn, tiled, bcast, tiled_out, red_out, tile, reverse=False, scratch=(), rider=None):
    tiled = [t if isinstance(t, tuple) else (t, t.shape[1], 0) for t in tiled]
    T = next(t[0] for t in tiled if not isinstance(t[0], str)).shape[0]
    n = T // tile
    nt, nb, nto, nsc, nro = len(tiled), len(bcast), len(tiled_out), len(scratch), len(red_out)
    ride_shape, ride_sems, _ = rider[1](rider[0]) if rider else (None, [], None)
    nr = 1 if rider else 0
    into = [(j, t[2], t[3]) for j, t in enumerate(tiled_out) if len(t) == 4]
    na = len(into)

    def row_block(i):
        return n - 1 - i if reverse else i

    def body(*refs):
        i2 = nt + nb
        o0 = i2 + nr + na
        o1, o2 = o0 + nto, o0 + nto + nro
        s0 = o2 + nr
        if rider:
            _ride(rider, pl.program_id(0), n, refs[i2], refs[o2], refs[s0 + nsc:])
        t_refs, b_refs = refs[:nt], refs[nt:i2]
        to_refs, ro_refs = refs[o0:o1], refs[o1:o2]
        extra = (list(refs[s0:s0 + nsc]),) if nsc else ()
        touts, routs = fn([r[...] for r in t_refs], [r[...] for r in b_refs], *extra)
        for r, v in zip(to_refs, touts, strict=True):
            r[...] = v.astype(r.dtype)
        if ro_refs:
            i = pl.program_id(0)

            @pl.when(i == 0)
            def _():
                for r, v in zip(ro_refs, routs, strict=True):
                    r[...] = v.astype(F32)

            @pl.when(i > 0)
            def _():
                for r, v in zip(ro_refs, routs, strict=True):
                    r[...] += v.astype(F32)

    def whole(shape):
        nd = len(shape)
        return pl.BlockSpec(tuple(shape), lambda i: (0,) * nd)

    per_tile = tile // _HALO
    in_specs, arrays = [], []
    for t in tiled:
        if isinstance(t[0], str):
            _, arr, w, cb = t
            in_specs.append(pl.BlockSpec((_HALO, w), functools.partial(lambda i, cb: (jnp.maximum(row_block(i) * per_tile - 1, 0), cb), cb=cb)))
        else:
            arr, w, cb = t
            in_specs.append(pl.BlockSpec((tile, w), functools.partial(lambda i, cb: (row_block(i), cb), cb=cb)))
        arrays.append(arr)
    in_specs += [whole(b.shape) for b in bcast]
    out_specs, out_shape = [], []
    for t in tiled_out:
        cb = t[3] if len(t) == 4 else 0
        out_specs.append(pl.BlockSpec((tile, t[0]), functools.partial(lambda i, cb: (row_block(i), cb), cb=cb)))
        out_shape.append(jax.ShapeDtypeStruct(t[2].shape if len(t) == 4 else (T, t[0]), t[1]))
    out_specs += [whole(s) for s in red_out]
    out_shape += [jax.ShapeDtypeStruct(tuple(s), F32) for s in red_out]
    ride_args = []
    if rider:
        in_specs.append(_HBM)
        out_specs.append(_HBM)
        out_shape.append(ride_shape)
        ride_args = [rider[0]]
    in_specs += [pl.BlockSpec(memory_space=pl.ANY)] * na
    aliases = {nt + nb + nr + k: j for k, (j, _, _) in enumerate(into)}
    res = pl.pallas_call(
        body, name=name, grid=(n,), in_specs=in_specs, out_specs=out_specs, out_shape=out_shape,
        scratch_shapes=[pltpu.VMEM(tuple(s), F32) for s in scratch] + list(ride_sems),
        input_output_aliases=aliases, compiler_params=_cparams(("arbitrary",)),
    )(*arrays, *bcast, *ride_args, *[buf for (_, buf, _) in into])
    if rider:
        return list(res[:nto]), list(res[nto:nto + nro]), res[-1]
    return list(res[:nto]), list(res[nto:])


def _rowwise_vjp(name, f, tiled, bcast, cts, tile, wrt_t, wrt_b, t_dtypes=None, colsum=(), prep=None, finish=None,
                 out_widths=None, reverse=False, scratch=(), rider=None, into=None):
    tiled = [t if isinstance(t, tuple) else (t, t.shape[1], 0) for t in tiled]
    npr = len(tiled)
    t_dtypes = t_dtypes or [F32] * len(wrt_t)
    groups = [c if isinstance(c, list) else [c] for c in cts]
    cts = [a_ for grp in groups for a_ in grp]

    def fn(tv, bv, sc=None):
        prim, flat_ct = tv[:npr], list(tv[npr:])
        if prep is not None:
            prim = prep(prim)
        ct = []
        for grp in groups:
            parts = [flat_ct.pop(0).astype(F32) for _ in grp]
            ct.append(functools.reduce(lambda p_, q_: p_ + q_, parts))

        def g(dt_vals, db_vals):
            full_t, full_b = list(prim), list(bv)
            for i, v in zip(wrt_t, dt_vals, strict=True):
                full_t[i] = v
            for j, v in zip(wrt_b, db_vals, strict=True):
                full_b[j] = v
            return f(full_t, full_b)

        outs, pull = jax.vjp(g, [prim[i].astype(F32) for i in wrt_t], [bv[j] for j in wrt_b])
        dts, dbs = pull([c.astype(o.dtype) for c, o in zip(ct, outs, strict=True)])
        if finish is not None:
            dts = finish(dts, sc)
        sums = [jnp.sum(dts[i].astype(F32), axis=0, keepdims=True) for i in colsum]
        return dts, list(dbs) + sums

    widths = out_widths or [tiled[i][1] for i in wrt_t]
    tiled_out = [(w, dt) + tuple((into or {}).get(j, ())) for j, (w, dt) in enumerate(zip(widths, t_dtypes, strict=True))]
    red_out = [bcast[j].shape for j in wrt_b] + [(1, widths[i]) for i in colsum]
    res = _rowwise(name, fn, tiled + list(cts), bcast, tiled_out, red_out, tile, reverse=reverse, scratch=scratch, rider=rider)
    dts, reds = res[0], res[1]
    nb = len(wrt_b)
    return (dts, reds[:nb], reds[nb:]) + tuple(res[2:])


def _layer_norm(x, g, b, eps):
    mu = jnp.mean(x, axis=-1, keepdims=True)
    xc = x - mu
    var = jnp.mean(xc * xc, axis=-1, keepdims=True)
    return xc * lax.rsqrt(var + eps) * g + b


def _gelu_tanh(x):
    return 0.5 * x * (1.0 + jnp.tanh(0.7978845608028654 * (x + 0.044715 * (x * x * x))))


def _sigmoid(x):
    return 1.0 / (1.0 + jnp.exp(-x))


def _seg_modulate(tv, bv):
    (x,), (sc, sh) = tv, bv
    return [x * (1.0 + sc) + sh]


def _seg_gmlp(tv, bv):
    (z,), (g_ln, b_ln, ws, b_tg, expand) = tv, bv
    bias_full = _dot(b_tg, expand, hi=True)
    zz = _gelu_tanh(z)
    u, v = zz[:, :G_WIDTH], zz[:, G_WIDTH:]
    v = _layer_norm(v, g_ln, b_ln, LN_EPS)
    row = lax.broadcasted_iota(jnp.int32, (G_CHUNK, G_CHUNK), 0)
    col = lax.broadcasted_iota(jnp.int32, (G_CHUNK, G_CHUNK), 1)
    causal = col <= row
    first_group = lax.broadcasted_iota(jnp.int32, (G_CHUNK, 128), 1) < 64
    parts = []
    for p in range(4):
        vp = v[:, 128 * p:128 * (p + 1)]
        s_even = _dot(jnp.where(causal, ws[2 * p], 0.0), vp)
        s_odd = _dot(jnp.where(causal, ws[2 * p + 1], 0.0), vp)
        parts.append(jnp.where(first_group, s_even, s_odd))
    s = jnp.concatenate(parts, axis=1) + bias_full
    return [u * s]


def _split2(x):
    hi = x.astype(_MXU_DTYPE)
    return hi, (x - hi.astype(F32)).astype(_MXU_DTYPE)


@jax.custom_vjp
def _group_sum(x, ones_blocks):
    hi, lo = _split2(x)
    w = ones_blocks.shape[0]
    return jnp.concatenate([_dot(hi[:, j:j + w], ones_blocks) + _dot(lo[:, j:j + w], ones_blocks) for j in range(0, x.shape[1], w)], axis=1)


def _group_sum_fwd(x, ones_blocks):
    return _group_sum(x, ones_blocks), ones_blocks


def _group_sum_bwd(ones_blocks, ct):
    return _group_sum(ct, ones_blocks), jnp.zeros_like(ones_blocks)


_group_sum.defvjp(_group_sum_fwd, _group_sum_bwd)


def _shift_down(z, halo, is_first):
    _, W = z.shape
    rolled = pltpu.roll(z, 1, 0)
    before = jnp.where(is_first, 0.0, pltpu.roll(halo, 1, 0))
    top_row = lax.broadcasted_iota(jnp.int32, (_HALO, W), 0) == 0
    return jnp.concatenate([jnp.where(top_row, before, rolled[:_HALO]), rolled[_HALO:]], axis=0)


def _shift_up(d, after):
    tile, W = d.shape
    rolled = pltpu.roll(d, tile - 1, 0)
    last_row = lax.broadcasted_iota(jnp.int32, (_HALO, W), 0) == _HALO - 1
    bottom = jnp.where(last_row, pltpu.roll(after, _HALO - 1, 0), rolled[tile - _HALO:])
    return jnp.concatenate([rolled[:tile - _HALO], bottom], axis=0)


def _seg_rwkv_pre(tv, bv):
    (z, prev), (mu, w0, wd, a0, wa, wg, k_k, k_a, gsum) = tv, bv
    zs = z + (prev - z) * mu
    r, k, v = zs[:, 0:512], zs[:, 512:1024], zs[:, 1024:1536]
    zl = zs[:, 3 * R_WIDTH:3 * R_WIDTH + LORA_PAD]
    x = w0 + _dot(jnp.tanh(zl), wd)
    softplus = jnp.maximum(-x, 0.0) + jnp.log(1.0 + jnp.exp(-jnp.abs(x)))
    lw = -jnp.exp(-softplus - 0.5)
    a = _sigmoid(a0 + _dot(zl, wa))
    g = _dot(_sigmoid(zl), wg)
    kk = k * k_k
    nrm = jnp.sqrt(_group_sum(kk * kk, gsum))
    kk = kk / jnp.maximum(nrm, 1e-12)
    k2 = k * (1.0 + (a - 1.0) * k_a)
    return [r, lw, k2, v, -kk, kk * a, g]


def _seg_rwkv_post(tv, bv):
    (y, r, k2, v, g), (r_k, gain, bias, gsum) = tv, bv
    mu = _group_sum(y, gsum) * (1.0 / R_HEAD)
    yc = y - mu
    var = _group_sum(yc * yc, gsum) * (1.0 / R_HEAD)
    yn = yc * lax.rsqrt(var + GN_EPS) * gain + bias
    bonus = _group_sum(r * k2 * r_k, gsum) * v
    return [(yn + bonus) * g]


def _seg_merge(tv, bv):
    (ga, gb, pa, pb), () = tv, bv
    return [_sigmoid(ga) * pa + _sigmoid(gb) * pb]


def _seg_mid(tv, bv):
    (x, mix), (gt1, g1, b1, sc2, sh2) = tv, bv
    h1 = _layer_norm(ALPHA * x + gt1 * mix, g1, b1, LN_EPS)
    return [h1, h1 * (1.0 + sc2) + sh2]


def _seg_relu2(tv, bv):
    (f1,), () = tv, bv
    return [jnp.square(jnp.maximum(f1, 0.0))]


def _seg_loss(tv, bv):
    (h1, ff, target), (gt2, g2, b2) = tv, bv
    out = _layer_norm(ALPHA * h1 + gt2 * ff, g2, b2, LN_EPS)
    err = jnp.square(out - target)
    return 0.5 * jnp.sum(jnp.mean(err, axis=-1))


_BNN = (((2,), (1,)), ((0,), (0,)))
_BNT = (((2,), (2,)), ((0,), (0,)))
_BTN = (((1,), (1,)), ((0,), (0,)))


def _tri_dot(x, dims):
    L = x.shape[0]
    tri = (lax.broadcasted_iota(jnp.int32, (L, L), 1) <= lax.broadcasted_iota(jnp.int32, (L, L), 0)).astype(F32)
    hi, lo = _split2(x)
    return _dot(tri, hi, dims) + _dot(tri, lo, dims)


@jax.custom_vjp
def _running_sum(x):
    return _tri_dot(x, _NN)


_running_sum.defvjp(lambda x: (_tri_dot(x, _NN), None), lambda _, ct: (_tri_dot(ct, _TN),))


def _cut_heads(x):
    return jnp.stack([x[:, R_HEAD * h:R_HEAD * (h + 1)] for h in range(R_HEADS)])


def _join_heads(x):
    return jnp.concatenate([x[h] for h in range(R_HEADS)], axis=1)


@jax.custom_vjp
def _split_heads(x):
    return _cut_heads(x)


@jax.custom_vjp
def _merge_heads(x):
    return _join_heads(x)


_split_heads.defvjp(lambda x: (_cut_heads(x), None), lambda _, ct: (_join_heads(ct),))
_merge_heads.defvjp(lambda x: (_join_heads(x), None), lambda _, ct: (_cut_heads(ct),))


def _inverse_pullback(inv, ct):
    return _dot(_dot(inv, ct, _BTN), inv, _BNT)


@jax.custom_vjp
def _unit_lower_inverse(n_mat):
    H, L, _ = n_mat.shape
    eye = lax.broadcasted_iota(jnp.int32, (H, L, L), 1) == lax.broadcasted_iota(jnp.int32, (H, L, L), 2)
    inv = jnp.where(eye, 1.0, 0.0) + n_mat
    pw = n_mat
    n = 2
    while n < L:
        pw = _dot(pw, pw, _BNN)
        inv = inv + _dot(inv, pw, _BNN)
        n *= 2
    return inv


def _unit_lower_inverse_fwd(n_mat):
    inv = _unit_lower_inverse(n_mat)
    return inv, inv


_unit_lower_inverse.defvjp(_unit_lower_inverse_fwd, lambda inv, ct: (_inverse_pullback(inv, ct),))


@jax.custom_vjp
def _known_inverse(n_mat, inv):
    return inv


_known_inverse.defvjp(lambda n_mat, inv: (inv, inv), lambda inv, ct: (_inverse_pullback(inv, ct), jnp.zeros_like(inv)))


def _scan_chunk(r, lw, k, v, a, b, s0, inv=None, with_inverse=False):
    L, H = r.shape[0], R_HEADS
    cs = _running_sum(lw)
    cs_end = cs[L - 1:L, :]
    p, p_inv, to_end = jnp.exp(cs), jnp.exp(-cs), jnp.exp(cs_end - cs)
    at, bt, kt, rt = [_split_heads(t) for t in (a * jnp.exp(cs - lw), b * p_inv, k * p_inv, r * p)]
    b_end, k_end, v = [_split_heads(t) for t in (b * to_end, k * to_end, v)]
    row = lax.broadcasted_iota(jnp.int32, (H, L, L), 1)
    col = lax.broadcasted_iota(jnp.int32, (H, L, L), 2)
    incl, strict = col <= row, col < row
    a_ab = jnp.where(strict, _dot(at, bt, _BNT), 0.0)
    a_ak = jnp.where(strict, _dot(at, kt, _BNT), 0.0)
    a_rb = jnp.where(incl, _dot(rt, bt, _BNT), 0.0)
    a_rk = jnp.where(incl, _dot(rt, kt, _BNT), 0.0)
    inv = _unit_lower_inverse(a_ab) if inv is None else _known_inverse(a_ab, inv)
    u = _dot(inv, _dot(at, s0, _BNT) + _dot(a_ak, v, _BNN), _BNN)
    y = _merge_heads(_dot(rt, s0, _BNT) + _dot(a_rb, u, _BNN) + _dot(a_rk, v, _BNN))
    s1 = s0 * _split_heads(jnp.exp(cs_end)) + _dot(u, b_end, _BTN) + _dot(v, k_end, _BTN)
    return (y, s1, inv) if with_inverse else (y, s1)


def _scan_fwd(r, lw, k, v, a, b, gate, post_b, rider):
    T = r.shape[0]
    H, N, L, P = R_HEADS, R_HEAD, SCAN_CHUNK, SCAN_PER_STEP
    nc = T // L
    steps = nc // P
    ride_shape, ride_sems, _ = rider[1](rider[0])

    def body(r_ref, lw_ref, k_ref, v_ref, a_ref, b_ref, g_ref, rk_ref, gain_ref, bias_ref, gsum_ref, ride_in,
             y_ref, yb_ref, st_ref, inv_ref, ride_out, s_ref, *sem_refs):
        _ride(rider, pl.program_id(0), steps, ride_in, ride_out, sem_refs)

        @pl.when(pl.program_id(0) == 0)
        def _():
            s_ref[...] = jnp.zeros_like(s_ref)

        s0 = s_ref[...]
        post_vals = [rk_ref[...], gain_ref[...], bias_ref[...], gsum_ref[...]]
        for j in range(P):
            rows = pl.ds(j * L, L)
            st_ref[j] = s0
            y, s0, inv = _scan_chunk(*[t[rows, :] for t in (r_ref, lw_ref, k_ref, v_ref, a_ref, b_ref)], s0, with_inverse=True)
            y_ref[rows, :] = y
            inv_ref[j] = inv
            yb_ref[rows, :] = _seg_rwkv_post([y, r_ref[rows, :], k_ref[rows, :], v_ref[rows, :], g_ref[rows, :]], post_vals)[0].astype(yb_ref.dtype)
        s_ref[...] = s0

    blk = pl.BlockSpec((P * L, R_WIDTH), lambda c: (c, 0))
    per_chunk = pl.BlockSpec((P, H, N, N), lambda c: (c, 0, 0, 0))
    whole = [pl.BlockSpec(t.shape, lambda c: (0, 0)) for t in post_b]
    return pl.pallas_call(
        body, name="scan_fwd", grid=(steps,), in_specs=[blk] * 7 + whole + [_HBM], out_specs=[blk, blk, per_chunk, per_chunk, _HBM],
        out_shape=[jax.ShapeDtypeStruct((T, R_WIDTH), F32), jax.ShapeDtypeStruct((T, R_WIDTH), _MXU_DTYPE)]
        + [jax.ShapeDtypeStruct((nc, H, N, N), F32)] * 2 + [ride_shape],
        scratch_shapes=[pltpu.VMEM((H, N, N), F32)] + list(ride_sems),
        compiler_params=_cparams(("arbitrary",)),
    )(r, lw, k, v, a, b, gate, *post_b, rider[0])


def _scan_bwd(r, lw, k, v, a, b, states, inverses, dy, rider):
    T = r.shape[0]
    H, N, L, P = R_HEADS, R_HEAD, SCAN_CHUNK, SCAN_PER_STEP
    nc = T // L
    steps = nc // P
    ride_shape, ride_sems, _ = rider[1](rider[0])

    def body(r_ref, lw_ref, k_ref, v_ref, a_ref, b_ref, st_ref, inv_ref, dy_ref, ride_in,
             dr_ref, dlw_ref, dk_ref, dv_ref, da_ref, db_ref, ride_out, ds_ref, *sem_refs):
        _ride(rider, pl.program_id(0), steps, ride_in, ride_out, sem_refs)

        @pl.when(pl.program_id(0) == 0)
        def _():
            ds_ref[...] = jnp.zeros_like(ds_ref)

        ds = ds_ref[...]
        for j in reversed(range(P)):
            rows = pl.ds(j * L, L)
            args = [t[rows, :] for t in (r_ref, lw_ref, k_ref, v_ref, a_ref, b_ref)] + [st_ref[j]]
            inv = inv_ref[j]
            _, pull = jax.vjp(lambda *xs, inv=inv: _scan_chunk(*xs, inv=inv), *args)
            grads = pull((dy_ref[rows, :], ds))
            for o_ref, g_ in zip((dr_ref, dlw_ref, dk_ref, dv_ref, da_ref, db_ref), grads[:6], strict=True):
                o_ref[rows, :] = g_
            ds = grads[6]
        ds_ref[...] = ds

    blk = pl.BlockSpec((P * L, R_WIDTH), lambda c: (steps - 1 - c, 0))
    per_chunk = pl.BlockSpec((P, H, N, N), lambda c: (steps - 1 - c, 0, 0, 0))
    return pl.pallas_call(
        body, name="scan_bwd", grid=(steps,), in_specs=[blk] * 6 + [per_chunk, per_chunk, blk, _HBM], out_specs=[blk] * 6 + [_HBM],
        out_shape=[jax.ShapeDtypeStruct((T, R_WIDTH), F32)] * 6 + [ride_shape],
        scratch_shapes=[pltpu.VMEM((H, N, N), F32)] + list(ride_sems),
        compiler_params=_cparams(("arbitrary",)),
    )(r, lw, k, v, a, b, states, inverses, dy, rider[0])


def _place():
    x, y, c = lax.axis_index("x"), lax.axis_index("y"), lax.axis_index("c")
    return x, y, c


def _gather_def(block):
    R, C = block.shape

    def phases(x_ref, out_ref, send_sems, recv_sems, local_sem):
        x, y, c = _place()
        me, sibling = (x, y, c), (x, y, 1 - c)
        chips = [(1 - x, y), (x, 1 - y), (1 - x, 1 - y)]

        def slot(px, py, pc):
            return out_ref.at[4 * px + 2 * py + pc]

        def copy(k, blk, to, src=None):
            return pltpu.make_async_remote_copy(
                src_ref=slot(*blk) if src is None else src, dst_ref=slot(*blk),
                send_sem=send_sems.at[k], recv_sem=recv_sems.at[k], device_id=to, device_id_type=_MESH_ID)

        mine = pltpu.make_async_copy(x_ref, slot(*me), local_sem)
        first = [copy(0, me, sibling, src=x_ref)]
        first += [copy(1 + j, me, (*chip, c), src=x_ref) for j, chip in enumerate(chips)]
        passed = [copy(4 + j, (*chip, c), sibling) for j, chip in enumerate(chips)]

        def begin():
            mine.start()
            for cp in first:
                cp.start()

        def forward():
            for j, chip in enumerate(chips):
                copy(1 + j, (*chip, c), me).wait_recv()
                passed[j].start()

        def finish():
            copy(0, sibling, me).wait_recv()
            for j, chip in enumerate(chips):
                copy(4 + j, (*chip, 1 - c), me).wait_recv()
            for cp in first + passed:
                cp.wait_send()
            mine.wait()

        return [begin, forward, finish]

    sems = [pltpu.SemaphoreType.DMA((7,)), pltpu.SemaphoreType.DMA((7,)), pltpu.SemaphoreType.DMA]
    return jax.ShapeDtypeStruct((_N_DEV, R, C), block.dtype), sems, phases


def _sibling_def(blocks):
    _, R, C = blocks.shape

    def phases(x_ref, out_ref, send_sems, recv_sems):
        x, y, c = _place()
        copies = [pltpu.make_async_remote_copy(
            src_ref=x_ref.at[2 * q + (1 - c)], dst_ref=out_ref.at[q], send_sem=send_sems.at[q], recv_sem=recv_sems.at[q],
            device_id=(x, y, 1 - c), device_id_type=_MESH_ID) for q in range(4)]

        def begin():
            for cp in copies:
                cp.start()

        def finish():
            for cp in copies:
                cp.wait()

        return [begin, finish]

    return jax.ShapeDtypeStruct((4, R, C), blocks.dtype), [pltpu.SemaphoreType.DMA((4,)), pltpu.SemaphoreType.DMA((4,))], phases


def _chips_def(partials):
    def phases(x_ref, out_ref, send_sems, recv_sems, local_sem):
        x, y, c = _place()
        my_chip = 2 * x + y
        mine = pltpu.make_async_copy(x_ref.at[my_chip], out_ref.at[my_chip], local_sem)
        copies = []
        for rel in range(1, 4):
            px, py = (1 - x if (rel >> 1) & 1 else x), (1 - y if rel & 1 else y)
            copies.append(pltpu.make_async_remote_copy(
                src_ref=x_ref.at[2 * px + py], dst_ref=out_ref.at[my_chip],
                send_sem=send_sems.at[rel - 1], recv_sem=recv_sems.at[rel - 1],
                device_id=(px, py, c), device_id_type=_MESH_ID))

        def begin():
            mine.start()
            for cp in copies:
                cp.start()

        def finish():
            for cp in copies:
                cp.wait()
            mine.wait()

        return [begin, finish]

    sems = [pltpu.SemaphoreType.DMA((3,)), pltpu.SemaphoreType.DMA((3,)), pltpu.SemaphoreType.DMA]
    return jax.ShapeDtypeStruct(partials.shape, partials.dtype), sems, phases


_HBM = pl.BlockSpec(memory_space=pltpu.HBM)


def _exchange(name, array, definition):
    out_shape, sems, phases = definition(array)

    def body(x_ref, out_ref, *sem_refs):
        for phase in phases(x_ref, out_ref, *sem_refs):
            phase()

    return pl.pallas_call(body, name=name, in_specs=[_HBM], out_specs=_HBM, out_shape=out_shape, scratch_shapes=sems)(array)


def _ride(rider, step, nsteps, x_ref, out_ref, sem_refs):
    array, definition, fractions = rider
    for phase, frac in zip(definition(array)[2](x_ref, out_ref, *sem_refs), fractions, strict=True):
        pl.when(step == min(int(frac * nsteps), nsteps - 1))(phase)


def _all_gather(name, block):
    return _exchange(name, block, _gather_def)


def _chip_partials(name, blocks, from_sibling, tile, out_dtype):
    _, R, C = blocks.shape

    def body(x_ref, s_ref, o_ref):
        c = lax.axis_index("c")
        for q in range(4):
            o_ref[q] = (x_ref[2 * q + c].astype(F32) + s_ref[q].astype(F32)).astype(o_ref.dtype)

    return pl.pallas_call(
        body, name=name, grid=(R // tile,),
        in_specs=[pl.BlockSpec((_N_DEV, tile, C), lambda i: (0, i, 0)), pl.BlockSpec((4, tile, C), lambda i: (0, i, 0))],
        out_specs=pl.BlockSpec((4, tile, C), lambda i: (0, i, 0)), out_shape=jax.ShapeDtypeStruct((4, R, C), out_dtype),
        compiler_params=_cparams(("parallel",)),
    )(blocks, from_sibling)


def _sum_leading(name, x, tile=None):
    n, R, C = x.shape
    tile = tile or _pick(R, (512, 256, 128, 64, 32, 16, 8))

    def body(x_ref, o_ref):
        acc = x_ref[0].astype(F32)
        for k in range(1, n):
            acc = acc + x_ref[k].astype(F32)
        o_ref[...] = acc

    return pl.pallas_call(
        body, name=name, grid=(R // tile,), in_specs=[pl.BlockSpec((n, tile, C), lambda i: (0, i, 0))],
        out_specs=pl.BlockSpec((tile, C), lambda i: (i, 0)), out_shape=jax.ShapeDtypeStruct((R, C), F32),
        compiler_params=_cparams(("parallel",)),
    )(x)


def _adamw_update(w_, g_, m_, v_):
    m2 = ADAM_B1 * m_ + (1.0 - ADAM_B1) * g_
    v2 = ADAM_B2 * v_ + (1.0 - ADAM_B2) * jnp.square(g_)
    m_hat = m2 / (1.0 - ADAM_B1 ** ADAM_STEP)
    v_hat = v2 / (1.0 - ADAM_B2 ** ADAM_STEP)
    delta = -ADAM_LR * (m_hat / (jnp.sqrt(v_hat) + ADAM_EPS) + ADAM_WD * w_)
    return delta, m2, v2


def _adamw(name, w, g, m, v):
    R, C = w.shape
    tile = _pick(R, (256, 128, 64, 32, 16, 8))
    outs, _ = _rowwise(name, lambda tv, bv: (list(_adamw_update(*tv)), []), [w, g, m, v], [], [(C, F32)] * 3, [], tile)
    return outs


def _adamw_many(name, ws, gs, ms, vs):
    n = len(ws)

    def body(*refs):
        ins, outs = refs[:4 * n], refs[4 * n:]
        for i in range(n):
            res = _adamw_update(ins[i][...], ins[n + i][...], ins[2 * n + i][...], ins[3 * n + i][...])
            for j in range(3):
                outs[j * n + i][...] = res[j]

    out_shape = [jax.ShapeDtypeStruct(w.shape, F32) for w in ws] * 3
    res = pl.pallas_call(body, name=name, out_shape=out_shape, compiler_params=_cparams())(*ws, *gs, *ms, *vs)
    return res[:n], res[n:2 * n], res[2 * n:]


def _pack_rows(arrs, lanes=128, row_mult=8):
    flat, places, off = [], [], 0
    for a_ in arrs:
        n = a_.size
        flat.append(a_.reshape(-1).astype(F32))
        places.append((off, n, a_.shape))
        off += n
    total = -(-off // (lanes * row_mult)) * (lanes * row_mult)
    if total > off:
        flat.append(jnp.zeros((total - off,), F32))
    return jnp.concatenate(flat).reshape(total // lanes, lanes), places


def _unpack_rows(packed, places):
    flat = packed.reshape(-1)
    return [flat[o:o + n].reshape(s) for (o, n, s) in places]


_WEIGHTS = ['w_ada', 'b_ada', 'w_in', 'b_in', 'g_ln_v', 'b_ln_v', 'w_spatial', 'b_spatial', 'mu_shift', 'w0', 'w_decay_up', 'a0',
            'w_aaa_up', 'w_gate_up', 'k_k', 'k_a', 'r_k', 'gn_gain', 'gn_bias', 'w_branch_a', 'w_branch_b', 'w_out', 'b_out',
            'ln1_g', 'ln1_b', 'w_ff1', 'b_ff1', 'w_ff2', 'b_ff2', 'ln2_g', 'ln2_b']
_BIG = {'w_ff1': (0, 512), 'w_ff2': (512, 512), 'w_out': (1024, 128), 'w_branch_a': (1152, 64), 'w_branch_b': (1216, 64), 'w_in': (1280, 640)}
_LATER_ROWS = 1280
_CUT_BY_COLS = ('w_ff1', 'w_in', 'w_branch_a', 'w_branch_b')
IN_SHARD = IN_COLS // _N_DEV
_LORA = {'w_decay_up': (0, LORA_W), 'w_aaa_up': (LORA_W, LORA_A), 'w_gate_up': (LORA_W + LORA_A, LORA_G)}
_COMM_DTYPE = jnp.bfloat16


def _pad_rows(a, rows):
    return jnp.pad(a, ((0, rows - a.shape[0]),) + ((0, 0),) * (a.ndim - 1))


def _pack_big(shards):
    blocks = []
    for n, (_, rows) in _BIG.items():
        a = shards[n].T if n in _CUT_BY_COLS else shards[n]
        blocks.append(_pad_rows(a.reshape(-1, D_MODEL), rows))
    return jnp.concatenate(blocks, axis=0)


def _unpack_big(block, like):
    out = {}
    for n, (r0, _) in _BIG.items():
        rr, cc = like[n].shape
        if n in _CUT_BY_COLS:
            out[n] = block[r0:r0 + rr * cc // D_MODEL].reshape(cc, rr).T
        else:
            out[n] = block[r0:r0 + rr]
    return out


def _to_padded(a, axis):
    g_end = 2 * G_WIDTH
    r_end = g_end + RW_USED
    take = lambda lo, hi: lax.slice_in_dim(a, lo, hi, axis=axis)
    zshape = list(a.shape)
    zshape[axis] = RW_COLS - RW_USED
    return jnp.concatenate([take(r_end, IN_COLS), take(g_end, r_end), jnp.zeros(zshape, a.dtype), take(0, g_end)], axis=axis)


def _from_padded(a, axis):
    take = lambda lo, hi: lax.slice_in_dim(a, lo, hi, axis=axis)
    return jnp.concatenate([take(2 * D_MODEL + RW_COLS, P_COLS), take(2 * D_MODEL, 2 * D_MODEL + RW_USED), take(0, 2 * D_MODEL)], axis=axis)


def _step(p, m, v, x, c, target):
    T = x.shape[0]
    xi, yi, ci = _place()
    me = 4 * xi + 2 * yi + ci
    tile = _pick(T, (512, 256))

    lane = jnp.arange(R_WIDTH)
    gsum = (lane[:128, None] // R_HEAD == lane[None, :128] // R_HEAD).astype(F32)
    expand = (jnp.arange(128)[:, None] == (lane[None, :] // (G_WIDTH // 8))).astype(F32)

    (c_act,), _ = _rowwise("silu_c", lambda tv, bv: ([tv[0] * _sigmoid(tv[0])], []), [c], [], [(D_MODEL, F32)], [], 1)
    small, places = _pack_rows([c_act, p['w_decay_up'], p['w_aaa_up'], p['w_gate_up']])
    small_all = _all_gather("gather_small", small)
    per_dev = [_unpack_rows(small_all[d], places) for d in range(_N_DEV)]
    c_act_all = _pad_rows(jnp.concatenate([pd[0] for pd in per_dev], axis=0), 16)
    lora_full = {n: jnp.concatenate([pd[i + 1] for pd in per_dev], axis=1) for i, n in enumerate(_LORA)}
    lora_pad = {n: jnp.zeros((LORA_PAD, R_WIDTH), F32).at[r0:r0 + nr].set(lora_full[n]) for n, (r0, nr) in _LORA.items()}

    big_names = list(_BIG)
    my_rows = _pack_big(p).astype(_MXU_DTYPE)
    b_in_p = _to_padded(p['b_in'], 1)
    mu_p = jnp.concatenate([p['mu_shift'], jnp.zeros((1, RW_COLS - RW_USED), F32)], axis=1)

    b_ada_mine = lax.dynamic_slice(p['b_ada'], (0, me * 768), (1, 768))
    mod_cols = _mm("ada_mod", c_act_all, p['w_ada'], "nn", bias=b_ada_mine)
    mod_all = _all_gather("gather_mod", mod_cols)
    mod = lax.dynamic_index_in_dim(mod_all, me, axis=1, keepdims=False).reshape(1, 6 * D_MODEL)
    sh1, sc1, gt1, sh2, sc2, gt2 = [mod[:, i * D_MODEL:(i + 1) * D_MODEL] for i in range(6)]

    (h,), _, w_in_all = _rowwise("modulate1", lambda tv, bv: (_seg_modulate(tv, bv), []), [x], [sc1, sh1], [(D_MODEL, _MXU_DTYPE)], [], tile,
                                 rider=(my_rows[_LATER_ROWS:], _gather_def, (0.0, 0.5, 1.0)))
    w_in_t = _to_padded(w_in_all[:, :IN_SHARD].reshape(IN_COLS, D_MODEL), 0)
    proj = _mm("in_proj", h, w_in_t, "nt", bias=b_in_p)
    ws = p['w_spatial']
    b_tg = jnp.zeros((G_CHUNK, 128), F32).at[:, :8].set(p['b_spatial'].T)
    gmlp_b = [p['g_ln_v'], p['b_ln_v'], ws, b_tg, expand]
    z_gmlp = (proj, 2 * G_WIDTH, 4)
    (ya,), _ = _rowwise("gmlp", lambda tv, bv: (_seg_gmlp(tv, bv), []), [z_gmlp], gmlp_b, [(G_WIDTH, F32)], [], G_CHUNK)
    z_rw = (proj, RW_COLS, 1)
    z_rw_halo = ("halo", proj, RW_COLS, 1)
    pre_b = [mu_p, p['w0'], lora_pad['w_decay_up'], p['a0'], lora_pad['w_aaa_up'], lora_pad['w_gate_up'], p['k_k'], p['k_a'], gsum]

    def pre_fwd(tv, bv):
        z_t, halo_t = tv
        return _seg_rwkv_pre([z_t, _shift_down(z_t, halo_t, pl.program_id(0) == 0)], bv), []

    pre_out, _ = _rowwise("rwkv_pre", pre_fwd, [z_rw, z_rw_halo], pre_b, [(R_WIDTH, F32)] * 7, [], tile)
    r_, lw_, k2_, v_, a_, b_, g_ = pre_out
    scan_in = (r_, lw_, k2_, v_, a_, b_)
    post_b = [p['r_k'].reshape(1, R_WIDTH), p['gn_gain'], p['gn_bias'], gsum]
    y_, yb, states, inverses, later_all = _scan_fwd(*scan_in, g_, post_b, rider=(my_rows[:_LATER_ROWS], _gather_def, (0.0, 0.875, 1.0)))

    def whole(n, rows):
        r0 = _BIG[n][0]
        return later_all[:, r0:r0 + rows].reshape(_N_DEV * rows, D_MODEL)

    w_ff1_t, w_ff2, w_out = whole('w_ff1', 512), whole('w_ff2', 512), whole('w_out', 128)
    w_ba_t = whole('w_branch_a', 64).reshape(D_MODEL, G_WIDTH)
    w_bb_t = whole('w_branch_b', 64).reshape(D_MODEL, R_WIDTH)
    pa = _mm("branch_a", ya, w_ba_t, "nt", out_dtype=_MXU_DTYPE)
    pb = _mm("branch_b", yb, w_bb_t, "nt", out_dtype=_MXU_DTYPE)
    gates = [(proj, D_MODEL, 0), (proj, D_MODEL, 1)]
    (merged,), _ = _rowwise("merge", lambda tv, bv: (_seg_merge(tv, bv), []), gates + [pa, pb], [], [(D_MODEL, _MXU_DTYPE)], [], tile)
    mix = _mm("out_proj", merged, w_out, "nn", bias=p['b_out'])
    mid_b = [gt1, p['ln1_g'], p['ln1_b'], sc2, sh2]
    (h1, h2in), _ = _rowwise("mid", lambda tv, bv: (_seg_mid(tv, bv), []), [x, mix], mid_b, [(D_MODEL, F32), (D_MODEL, _MXU_DTYPE)], [], tile)
    act = _mm("ff1", h2in, w_ff1_t, "nt", bias=p['b_ff1'], epi=lambda t: _seg_relu2([t], [])[0], epi_dtype=_MXU_DTYPE, raw=False)
    ff = _mm("ff2", act, w_ff2, "nn", bias=p['b_ff2'])

    def loss_fn(tv, bv):
        h1_t, ff_t, tgt = tv
        val, grads = jax.value_and_grad(lambda a0_, a1_, b0_, b1_, b2_: _seg_loss([a0_, a1_, tgt], [b0_, b1_, b2_]), argnums=(0, 1, 2, 3, 4))(h1_t, ff_t, *bv)
        return [grads[0], grads[1]], [grads[2], grads[3], grads[4], jnp.sum(grads[1], axis=0, keepdims=True), jnp.full((1, 128), val, F32)]

    (dh1_a, dff), (d_gt2, d_ln2_g, d_ln2_b, d_b_ff2, loss_row) = _rowwise(
        "loss", loss_fn, [h1, ff, target], [gt2, p['ln2_g'], p['ln2_b']], [(D_MODEL, F32), (D_MODEL, _MXU_DTYPE)], [(1, D_MODEL)] * 4 + [(1, 128)], tile)

    g = {}
    g['ln2_g'], g['ln2_b'], g['b_ff2'] = d_ln2_g, d_ln2_b, d_b_ff2
    gw = {}
    gw['w_ff2'] = _mm("g_w_ff2", act, dff, "tn", out_dtype=_COMM_DTYPE)
    df1, g['b_ff1'] = _mm("d_act", dff, w_ff2, "nt", beside=act, epi=lambda d_, a_: d_ * (2.0 * jnp.sqrt(a_.astype(F32))),
                          epi_dtype=_MXU_DTYPE, colsum=True)
    gw['w_ff1'] = _mm("g_w_ff1", df1, h2in, "tn", out_dtype=_COMM_DTYPE)
    dh2in = _mm("d_h2in", df1, w_ff1_t, "nn")
    (dx_a, dmix), (d_gt1, g['ln1_g'], g['ln1_b'], d_sc2, d_sh2), (g['b_out'],) = _rowwise_vjp(
        "mid_bwd", _seg_mid, [x, mix], mid_b, [dh1_a, dh2in], tile, [0, 1], [0, 1, 2, 3, 4], t_dtypes=[F32, _MXU_DTYPE], colsum=[1])
    gw['w_out'] = _mm("g_w_out", merged, dmix, "tn", out_dtype=_COMM_DTYPE)
    dmerged = _mm("d_merged", dmix, w_out, "nt", out_dtype=_MXU_DTYPE)
    (dproj, dpa, dpb), _, (cs_gates,) = _rowwise_vjp(
        "merge_bwd", _seg_merge, gates + [pa, pb], [], [dmerged], tile, [0, 1, 2, 3], [], t_dtypes=[_MXU_DTYPE] * 3, colsum=[0],
        finish=lambda dts, sc: [jnp.concatenate(dts[:2], axis=1), dts[2], dts[3]], out_widths=[2 * D_MODEL, D_MODEL, D_MODEL],
        into={0: (lax.empty((T, P_COLS), _MXU_DTYPE), 0)})
    gw['w_branch_a'] = _mm("g_w_branch_a", dpa, ya, "tn", out_dtype=_COMM_DTYPE)
    gw['w_branch_b'] = _mm("g_w_branch_b", dpb, yb, "tn", out_dtype=_COMM_DTYPE)
    dya = _mm("d_ya", dpa, w_ba_t, "nn")
    dyb = _mm("d_yb", dpb, w_bb_t, "nn")
    def send_rows(names):
        parts = []
        for n in names:
            per_dev = gw[n].reshape(_N_DEV, -1, D_MODEL)
            parts.append(jnp.pad(per_dev, ((0, 0), (0, _BIG[n][1] - per_dev.shape[1]), (0, 0))))
        return jnp.concatenate(parts, axis=1) if len(parts) > 1 else parts[0]

    send_early = send_rows(big_names[:-1])
    (dy, dr1, dk1, dv1, dg_), (d_r_k, g['gn_gain'], g['gn_bias']), _, sibling_early = _rowwise_vjp(
        "rwkv_post_bwd", _seg_rwkv_post, [y_, r_, k2_, v_, g_], post_b, [dyb], tile, [0, 1, 2, 3, 4], [0, 1, 2],
        rider=(send_early, _sibling_def, (0.0, 1.0)))
    g['r_k'] = d_r_k
    partials_early = _chip_partials("chip_partials_early", send_early, sibling_early, 128, _COMM_DTYPE)
    dr2, dlw, dk2, dv2, da, db, landed_early = _scan_bwd(*scan_in, states, inverses, dy, rider=(partials_early, _chips_def, (0.0, 1.0)))
    pre_tile = 256
    last_step = T // pre_tile - 1

    def pre_prep(prim):
        z_t, halo_t = prim
        return [z_t, _shift_down(z_t, halo_t, pl.program_id(0) == last_step)]

    def pre_finish(dts, sc):
        dz_direct, dprev = dts
        (row_after,) = sc

        @pl.when(pl.program_id(0) == 0)
        def _():
            row_after[...] = jnp.zeros_like(row_after)

        dz = dz_direct + _shift_up(dprev, row_after[...])
        row_after[...] = dprev[:_HALO]
        return [dz]

    small_names = [n for n in _WEIGHTS if n not in _BIG and n not in ('w_ada', 'b_ada')]
    (dproj,), (g['g_ln_v'], g['b_ln_v'], g['w_spatial'], d_b_tg), (cs_g,) = _rowwise_vjp(
        "gmlp_bwd", _seg_gmlp, [z_gmlp], gmlp_b, [dya], G_CHUNK, [0], [0, 1, 2, 3], t_dtypes=[_MXU_DTYPE], colsum=[0],
        into={0: (dproj, 4)})
    g['b_spatial'] = d_b_tg[:, :8].T
    names_1 = [n for n in small_names if n in g]
    packed_1, places_1 = _pack_rows([g[n] for n in names_1] + [loss_row], row_mult=256)
    (dproj,), (d_mu, g['w0'], d_wd, g['a0'], d_wa, d_wg, g['k_k'], g['k_a']), (cs_rw,), small_all_1 = _rowwise_vjp(
        "rwkv_pre_bwd", _seg_rwkv_pre, [z_rw, z_rw_halo], pre_b, [[dr1, dr2], dlw, [dk1, dk2], [dv1, dv2], da, db, dg_], pre_tile,
        [0, 1], [0, 1, 2, 3, 4, 5, 6, 7], t_dtypes=[_MXU_DTYPE], colsum=[0], prep=pre_prep, finish=pre_finish,
        out_widths=[RW_COLS], reverse=True, scratch=[(_HALO, RW_COLS)], into={0: (dproj, 1)},
        rider=(packed_1, _gather_def, (0.0, 0.6, 1.0)))
    g['mu_shift'] = d_mu[:, :RW_USED]
    for n, d_ in (('w_decay_up', d_wd), ('w_aaa_up', d_wa), ('w_gate_up', d_wg)):
        r0, nr = _LORA[n]
        g[n] = d_[r0:r0 + nr]
    g['b_in'] = _from_padded(jnp.concatenate([cs_gates, cs_rw, cs_g], axis=1), 1)
    names_2 = [n for n in small_names if n not in names_1]
    packed_2, places_2 = _pack_rows([g[n] for n in names_2], row_mult=256)
    gw_in_t, small_all_2 = _mm("g_w_in", dproj, h, "tn", out_dtype=_COMM_DTYPE, rider=(packed_2, _gather_def, (0.0, 0.6, 1.0)))
    gw['w_in'] = _from_padded(gw_in_t, 0)
    send_late = send_rows(big_names[-1:])
    sibling_late = _exchange("pair_exchange_late", send_late, _sibling_def)
    partials_late = _chip_partials("chip_partials_late", send_late, sibling_late, 128, _COMM_DTYPE)
    dh, landed_late = _mm("d_h", dproj, w_in_t, "nn", rider=(partials_late, _chips_def, (0.0, 1.0)))

    def mod1_bwd(tv, bv):
        x_t, dh_t, dxa_t = tv
        (sc,) = bv
        return [dxa_t + dh_t * (1.0 + sc)], [jnp.sum(dh_t * x_t, axis=0, keepdims=True), jnp.sum(dh_t, axis=0, keepdims=True)]

    (grad_x,), (d_sc1, d_sh1) = _rowwise("modulate1_bwd", mod1_bwd, [x, dh, dx_a], [sc1], [(D_MODEL, F32)], [(1, D_MODEL)] * 2, tile)

    dmod = jnp.concatenate([d_sh1, d_sc1, d_gt1, d_sh2, d_sc2, d_gt2], axis=1).reshape(6 * D_MODEL // 128, 128)
    dmod_all = _all_gather("gather_dmod", dmod)
    g['b_ada'] = _sum_leading("sum_dmod", dmod_all).reshape(1, 6 * D_MODEL)
    dmod_mine = lax.dynamic_slice(dmod_all.reshape(_N_DEV, 6 * D_MODEL), (0, me * 768), (_N_DEV, 768))
    g_w_ada = _mm("g_w_ada", c_act_all, _pad_rows(dmod_mine, 16), "tn")

    sums_1 = _unpack_rows(_sum_leading("sum_small_1", small_all_1), places_1)
    loss = sums_1.pop()[0, 0]
    sums_2 = _unpack_rows(_sum_leading("sum_small_2", small_all_2), places_2)
    g.update(zip(names_1 + names_2, sums_1 + sums_2, strict=True))
    for n in _LORA:
        g[n] = lax.dynamic_slice(g[n], (0, me * R_HEAD), (g[n].shape[0], R_HEAD))
    g['w_ada'] = g_w_ada

    summed = jnp.concatenate([_sum_leading("sum_early", landed_early), _sum_leading("sum_late", landed_late)], axis=0)
    g.update(_unpack_big(summed, p))

    delta, new_m, new_v = {}, {}, {}
    own_call = ['w_ada'] + big_names
    for n in own_call:
        if n == 'w_in':
            r0 = _BIG[n][0]
            outs_t = _adamw("adamw_" + n, p[n].T, summed[r0:r0 + IN_SHARD], m[n].T, v[n].T)
            delta[n], new_m[n], new_v[n] = [o.T for o in outs_t]
        else:
            delta[n], new_m[n], new_v[n] = _adamw("adamw_" + n, p[n], g[n], m[n], v[n])
    rest = [n for n in _WEIGHTS if n not in own_call]
    outs = _adamw_many("adamw_rest", *[[d[n].reshape(p[n].shape) for n in rest] for d in (p, g, m, v)])
    for d, o in zip((delta, new_m, new_v), outs, strict=True):
        d.update(zip(rest, o, strict=True))
    return loss, grad_x, g, delta, new_m, new_v


def kernel(x, c, w_ada, b_ada, w_in, b_in, g_ln_v, b_ln_v, w_spatial, b_spatial, mu_shift, w0, w_decay_up, a0, w_aaa_up, w_gate_up, k_k, k_a, r_k, gn_gain, gn_bias, w_branch_a, w_branch_b, w_out, b_out, ln1_g, ln1_b, w_ff1, b_ff1, w_ff2, b_ff2, ln2_g, ln2_b, loss_target, m_w_ada, m_b_ada, m_w_in, m_b_in, m_g_ln_v, m_b_ln_v, m_w_spatial, m_b_spatial, m_mu_shift, m_w0, m_w_decay_up, m_a0, m_w_aaa_up, m_w_gate_up, m_k_k, m_k_a, m_r_k, m_gn_gain, m_gn_bias, m_w_branch_a, m_w_branch_b, m_w_out, m_b_out, m_ln1_g, m_ln1_b, m_w_ff1, m_b_ff1, m_w_ff2, m_b_ff2, m_ln2_g, m_ln2_b, v_w_ada, v_b_ada, v_w_in, v_b_in, v_g_ln_v, v_b_ln_v, v_w_spatial, v_b_spatial, v_mu_shift, v_w0, v_w_decay_up, v_a0, v_w_aaa_up, v_w_gate_up, v_k_k, v_k_a, v_r_k, v_gn_gain, v_gn_bias, v_w_branch_a, v_w_branch_b, v_w_out, v_b_out, v_ln1_g, v_ln1_b, v_w_ff1, v_b_ff1, v_w_ff2, v_b_ff2, v_ln2_g, v_ln2_b):
    given = dict(locals())
    shapes = {n: given[n].shape for n in _WEIGHTS}
    def two_d(a_):
        a_ = a_[0]
        return a_.reshape(1, -1) if a_.ndim == 1 else a_
    p = {n: two_d(given[n]) for n in _WEIGHTS}
    m = {n: two_d(given["m_" + n]) for n in _WEIGHTS}
    v = {n: two_d(given["v_" + n]) for n in _WEIGHTS}
    loss, grad_x, g, delta, new_m, new_v = _step(p, m, v, x[0], c, loss_target[0])
    outs = [loss, grad_x[None]]
    for d in (g, delta, new_m, new_v):
        outs += [d[n].reshape(shapes[n]) for n in _WEIGHTS]
    return tuple(outs)
```

```python
import functools

import jax
import jax.numpy as jnp
from jax import lax
from jax.experimental import pallas as pl
from jax.experimental.pallas import tpu as pltpu

F32 = jnp.float32
_MXU_DTYPE = jnp.bfloat16
_HI = lax.Precision.HIGHEST
_VMEM_LIMIT = 48 * 1024 * 1024
_MESH_ID = pl.DeviceIdType.MESH
_N_DEV = 8

D_MODEL = 1024
G_WIDTH = 512
G_CHUNK = 128
R_WIDTH = 512
R_HEADS = 8
R_HEAD = 64
LORA_W, LORA_A, LORA_G = 32, 32, 96
D_FF = 4096
ALPHA = 2.0 ** 0.25
LN_EPS = 1e-5
GN_EPS = 64e-5
SCAN_CHUNK = 64
SCAN_PER_STEP = 4
ADAM_LR, ADAM_B1, ADAM_B2, ADAM_EPS, ADAM_WD, ADAM_STEP = 0.001, 0.9, 0.999, 1e-08, 0.01, 10

P_COLS = 5120
RW_COLS = 2048
RW_USED = 3 * R_WIDTH + LORA_W + LORA_A + LORA_G
LORA_PAD = 256
IN_COLS = 2 * G_WIDTH + RW_USED + 2 * D_MODEL


def _cparams(sem=None, **kw):
    if sem is not None:
        kw["dimension_semantics"] = sem
    return pltpu.CompilerParams(vmem_limit_bytes=_VMEM_LIMIT, **kw)


def _dot(a, b, dims=(((1,), (0,)), ((), ())), hi=False):
    if hi:
        return lax.dot_general(a.astype(F32), b.astype(F32), dims, precision=_HI, preferred_element_type=F32)
    return lax.dot_general(a.astype(_MXU_DTYPE), b.astype(_MXU_DTYPE), dims, preferred_element_type=F32)


_NN = (((1,), (0,)), ((), ()))
_NT = (((1,), (1,)), ((), ()))
_TN = (((0,), (0,)), ((), ()))


def _pick(n, pref):
    for t in pref:
        if n % t == 0:
            return t
    return n


def _mm(name, a, b, mode, bias=None, out_dtype=F32, epi=None, epi_dtype=None, raw=True, beside=None, colsum=False, rider=None,
        tm=None, tn=None, tk=None):
    if mode == "nn":
        (M, K), (_, N) = a.shape, b.shape
    elif mode == "nt":
        (M, K), (N, _) = a.shape, b.shape
    else:
        (K, M), (_, N) = a.shape, b.shape
    tm = tm or _pick(M, (2048, 1280, 1024, 512, 256, 128, 64, 32, 16, 8))
    tn = tn or _pick(N, (1024, 512, 640, 384, 256, 128))
    tk = tk or _pick(K, (1024, 512, 256, 128))
    nk = K // tk
    dims = {"nn": _NN, "nt": _NT, "tn": _TN}[mode]
    a_spec = pl.BlockSpec((tk, tm), lambda i, j, k: (k, i)) if mode == "tn" else pl.BlockSpec((tm, tk), lambda i, j, k: (i, k))
    b_spec = pl.BlockSpec((tn, tk), lambda i, j, k: (j, k)) if mode == "nt" else pl.BlockSpec((tk, tn), lambda i, j, k: (k, j))
    o_spec = pl.BlockSpec((tm, tn), lambda i, j, k: (i, j))
    has_bias, has_beside = bias is not None, beside is not None
    two = epi is not None and not has_beside and raw
    only_epi = epi is not None and not has_beside and not raw
    grid = (M // tm, N // tn, nk)
    ride_shape, ride_sems, _ = rider[1](rider[0]) if rider else (None, [], None)

    def body(*refs):
        refs = list(refs)
        n_in = 2 + has_bias + has_beside
        if rider:
            sem_refs = [refs.pop() for _ in ride_sems][::-1]
            ride_out = refs.pop(n_in + 1 + 1 + two)
            ride_in = refs.pop(n_in)
            step = (pl.program_id(0) * grid[1] + pl.program_id(1)) * grid[2] + pl.program_id(2)
            _ride(rider, step, grid[0] * grid[1] * grid[2], ride_in, ride_out, sem_refs)
        a_ref, b_ref = refs[0], refs[1]
        bias_ref = refs[2] if has_bias else None
        beside_ref = refs[n_in - 1] if has_beside else None
        outs = refs[n_in:]
        o_ref, acc_ref = outs[0], outs[-1]
        k = pl.program_id(2)

        @pl.when(k == 0)
        def _():
            acc_ref[...] = jnp.zeros_like(acc_ref)

        acc_ref[...] += _dot(a_ref[...], b_ref[...], dims)

        @pl.when(k == nk - 1)
        def _():
            res = acc_ref[...]
            if has_bias:
                res = res + bias_ref[...]
            if has_beside:
                val = epi(res, beside_ref[...])
                o_ref[...] = val.astype(o_ref.dtype)
                if colsum:
                    outs[1][...] = jnp.broadcast_to(jnp.sum(val, axis=0, keepdims=True), outs[1].shape)
            elif only_epi:
                o_ref[...] = epi(res).astype(o_ref.dtype)
            else:
                o_ref[...] = res.astype(o_ref.dtype)
                if two:
                    outs[1][...] = epi(res).astype(outs[1].dtype)

    in_specs = [a_spec, b_spec]
    args = [a, b]
    if has_bias:
        in_specs.append(pl.BlockSpec((1, tn), lambda i, j, k: (0, j)))
        args.append(bias)
    if has_beside:
        in_specs.append(o_spec)
        args.append(beside)
    out_shape = [jax.ShapeDtypeStruct((M, N), epi_dtype if (has_beside or only_epi) else out_dtype)]
    out_specs = [o_spec]
    if two:
        out_shape.append(jax.ShapeDtypeStruct((M, N), epi_dtype))
        out_specs.append(o_spec)
    if colsum:
        assert has_beside and not rider
        out_shape.append(jax.ShapeDtypeStruct((grid[0] * 8, N), F32))
        out_specs.append(pl.BlockSpec((8, tn), lambda i, j, k: (i, j)))
    if rider:
        in_specs.append(_HBM)
        args.append(rider[0])
        out_shape.append(ride_shape)
        out_specs.append(_HBM)
    res = pl.pallas_call(
        body, name=name, grid=grid, in_specs=in_specs, out_specs=out_specs, out_shape=out_shape,
        scratch_shapes=[pltpu.VMEM((tm, tn), F32)] + list(ride_sems),
        compiler_params=_cparams(("arbitrary",) * 3 if rider else ("parallel", "parallel", "arbitrary")),
    )(*args)
    if colsum:
        return res[0], _sum_leading(name + "_colsum", res[1].reshape(grid[0], 8, N))[0:1]
    return res if (two or rider) else res[0]


_HALO = 8


def _rowwise(name, fn, tiled, bcast, tiled_out, red_out, tile, reverse=False, scratch=(), rider=None):
    tiled = [t if isinstance(t, tuple) else (t, t.shape[1], 0) for t in tiled]
    T = next(t[0] for t in tiled if not isinstance(t[0], str)).shape[0]
    n = T // tile
    nt, nb, nto, nsc, nro = len(tiled), len(bcast), len(tiled_out), len(scratch), len(red_out)
    ride_shape, ride_sems, _ = rider[1](rider[0]) if rider else (None, [], None)
    nr = 1 if rider else 0
    into = [(j, t[2], t[3]) for j, t in enumerate(tiled_out) if len(t) == 4]
    na = len(into)

    def row_block(i):
        return n - 1 - i if reverse else i

    def body(*refs):
        i2 = nt + nb
        o0 = i2 + nr + na
        o1, o2 = o0 + nto, o0 + nto + nro
        s0 = o2 + nr
        if rider:
            _ride(rider, pl.program_id(0), n, refs[i2], refs[o2], refs[s0 + nsc:])
        t_refs, b_refs = refs[:nt], refs[nt:i2]
        to_refs, ro_refs = refs[o0:o1], refs[o1:o2]
        extra = (list(refs[s0:s0 + nsc]),) if nsc else ()
        touts, routs = fn([r[...] for r in t_refs], [r[...] for r in b_refs], *extra)
        for r, v in zip(to_refs, touts, strict=True):
            r[...] = v.astype(r.dtype)
        if ro_refs:
            i = pl.program_id(0)

            @pl.when(i == 0)
            def _():
                for r, v in zip(ro_refs, routs, strict=True):
                    r[...] = v.astype(F32)

            @pl.when(i > 0)
            def _():
                for r, v in zip(ro_refs, routs, strict=True):
                    r[...] += v.astype(F32)

    def whole(shape):
        nd = len(shape)
        return pl.BlockSpec(tuple(shape), lambda i: (0,) * nd)

    per_tile = tile // _HALO
    in_specs, arrays = [], []
    for t in tiled:
        if isinstance(t[0], str):
            _, arr, w, cb = t
            in_specs.append(pl.BlockSpec((_HALO, w), functools.partial(lambda i, cb: (jnp.maximum(row_block(i) * per_tile - 1, 0), cb), cb=cb)))
        else:
            arr, w, cb = t
            in_specs.append(pl.BlockSpec((tile, w), functools.partial(lambda i, cb: (row_block(i), cb), cb=cb)))
        arrays.append(arr)
    in_specs += [whole(b.shape) for b in bcast]
    out_specs, out_shape = [], []
    for t in tiled_out:
        cb = t[3] if len(t) == 4 else 0
        out_specs.append(pl.BlockSpec((tile, t[0]), functools.partial(lambda i, cb: (row_block(i), cb), cb=cb)))
        out_shape.append(jax.ShapeDtypeStruct(t[2].shape if len(t) == 4 else (T, t[0]), t[1]))
    out_specs += [whole(s) for s in red_out]
    out_shape += [jax.ShapeDtypeStruct(tuple(s), F32) for s in red_out]
    ride_args = []
    if rider:
        in_specs.append(_HBM)
        out_specs.append(_HBM)
        out_shape.append(ride_shape)
        ride_args = [rider[0]]
    in_specs += [pl.BlockSpec(memory_space=pl.ANY)] * na
    aliases = {nt + nb + nr + k: j for k, (j, _, _) in enumerate(into)}
    res = pl.pallas_call(
        body, name=name, grid=(n,), in_specs=in_specs, out_specs=out_specs, out_shape=out_shape,
        scratch_shapes=[pltpu.VMEM(tuple(s), F32) for s in scratch] + list(ride_sems),
        input_output_aliases=aliases, compiler_params=_cparams(("arbitrary",)),
    )(*arrays, *bcast, *ride_args, *[buf for (_, buf, _) in into])
    if rider:
        return list(res[:nto]), list(res[nto:nto + nro]), res[-1]
    return list(res[:nto]), list(res[nto:])


def _rowwise_vjp(name, f, tiled, bcast, cts, tile, wrt_t, wrt_b, t_dtypes=None, colsum=(), prep=None, finish=None,
                 out_widths=None, reverse=False, scratch=(), rider=None, into=None):
    tiled = [t if isinstance(t, tuple) else (t, t.shape[1], 0) for t in tiled]
    npr = len(tiled)
    t_dtypes = t_dtypes or [F32] * len(wrt_t)
    groups = [c if isinstance(c, list) else [c] for c in cts]
    cts = [a_ for grp in groups for a_ in grp]

    def fn(tv, bv, sc=None):
        prim, flat_ct = tv[:npr], list(tv[npr:])
        if prep is not None:
            prim = prep(prim)
        ct = []
        for grp in groups:
            parts = [flat_ct.pop(0).astype(F32) for _ in grp]
            ct.append(functools.reduce(lambda p_, q_: p_ + q_, parts))

        def g(dt_vals, db_vals):
            full_t, full_b = list(prim), list(bv)
            for i, v in zip(wrt_t, dt_vals, strict=True):
                full_t[i] = v
            for j, v in zip(wrt_b, db_vals, strict=True):
                full_b[j] = v
            return f(full_t, full_b)

        outs, pull = jax.vjp(g, [prim[i].astype(F32) for i in wrt_t], [bv[j] for j in wrt_b])
        dts, dbs = pull([c.astype(o.dtype) for c, o in zip(ct, outs, strict=True)])
        if finish is not None:
            dts = finish(dts, sc)
        sums = [jnp.sum(dts[i].astype(F32), axis=0, keepdims=True) for i in colsum]
        return dts, list(dbs) + sums

    widths = out_widths or [tiled[i][1] for i in wrt_t]
    tiled_out = [(w, dt) + tuple((into or {}).get(j, ())) for j, (w, dt) in enumerate(zip(widths, t_dtypes, strict=True))]
    red_out = [bcast[j].shape for j in wrt_b] + [(1, widths[i]) for i in colsum]
    res = _rowwise(name, fn, tiled + list(cts), bcast, tiled_out, red_out, tile, reverse=reverse, scratch=scratch, rider=rider)
    dts, reds = res[0], res[1]
    nb = len(wrt_b)
    return (dts, reds[:nb], reds[nb:]) + tuple(res[2:])


def _layer_norm(x, g, b, eps):
    mu = jnp.mean(x, axis=-1, keepdims=True)
    xc = x - mu
    var = jnp.mean(xc * xc, axis=-1, keepdims=True)
    return xc * lax.rsqrt(var + eps) * g + b


def _gelu_tanh(x):
    return 0.5 * x * (1.0 + jnp.tanh(0.7978845608028654 * (x + 0.044715 * (x * x * x))))


def _sigmoid(x):
    return 1.0 / (1.0 + jnp.exp(-x))


def _seg_modulate(tv, bv):
    (x,), (sc, sh) = tv, bv
    return [x * (1.0 + sc) + sh]


def _seg_gmlp(tv, bv):
    (z,), (g_ln, b_ln, ws, b_tg, expand) = tv, bv
    bias_full = _dot(b_tg, expand, hi=True)
    zz = _gelu_tanh(z)
    u, v = zz[:, :G_WIDTH], zz[:, G_WIDTH:]
    v = _layer_norm(v, g_ln, b_ln, LN_EPS)
    row = lax.broadcasted_iota(jnp.int32, (G_CHUNK, G_CHUNK), 0)
    col = lax.broadcasted_iota(jnp.int32, (G_CHUNK, G_CHUNK), 1)
    causal = col <= row
    first_group = lax.broadcasted_iota(jnp.int32, (G_CHUNK, 128), 1) < 64
    parts = []
    for p in range(4):
        vp = v[:, 128 * p:128 * (p + 1)]
        s_even = _dot(jnp.where(causal, ws[2 * p], 0.0), vp)
        s_odd = _dot(jnp.where(causal, ws[2 * p + 1], 0.0), vp)
        parts.append(jnp.where(first_group, s_even, s_odd))
    s = jnp.concatenate(parts, axis=1) + bias_full
    return [u * s]


def _split2(x):
    hi = x.astype(_MXU_DTYPE)
    return hi, (x - hi.astype(F32)).astype(_MXU_DTYPE)


@jax.custom_vjp
def _group_sum(x, ones_blocks):
    hi, lo = _split2(x)
    w = ones_blocks.shape[0]
    return jnp.concatenate([_dot(hi[:, j:j + w], ones_blocks) + _dot(lo[:, j:j + w], ones_blocks) for j in range(0, x.shape[1], w)], axis=1)


def _group_sum_fwd(x, ones_blocks):
    return _group_sum(x, ones_blocks), ones_blocks


def _group_sum_bwd(ones_blocks, ct):
    return _group_sum(ct, ones_blocks), jnp.zeros_like(ones_blocks)


_group_sum.defvjp(_group_sum_fwd, _group_sum_bwd)


def _shift_down(z, halo, is_first):
    _, W = z.shape
    rolled = pltpu.roll(z, 1, 0)
    before = jnp.where(is_first, 0.0, pltpu.roll(halo, 1, 0))
    top_row = lax.broadcasted_iota(jnp.int32, (_HALO, W), 0) == 0
    return jnp.concatenate([jnp.where(top_row, before, rolled[:_HALO]), rolled[_HALO:]], axis=0)


def _shift_up(d, after):
    tile, W = d.shape
    rolled = pltpu.roll(d, tile - 1, 0)
    last_row = lax.broadcasted_iota(jnp.int32, (_HALO, W), 0) == _HALO - 1
    bottom = jnp.where(last_row, pltpu.roll(after, _HALO - 1, 0), rolled[tile - _HALO:])
    return jnp.concatenate([rolled[:tile - _HALO], bottom], axis=0)


def _seg_rwkv_pre(tv, bv):
    (z, prev), (mu, w0, wd, a0, wa, wg, k_k, k_a, gsum) = tv, bv
    zs = z + (prev - z) * mu
    r, k, v = zs[:, 0:512], zs[:, 512:1024], zs[:, 1024:1536]
    zl = zs[:, 3 * R_WIDTH:3 * R_WIDTH + LORA_PAD]
    x = w0 + _dot(jnp.tanh(zl), wd)
    softplus = jnp.maximum(-x, 0.0) + jnp.log(1.0 + jnp.exp(-jnp.abs(x)))
    lw = -jnp.exp(-softplus - 0.5)
    a = _sigmoid(a0 + _dot(zl, wa))
    g = _dot(_sigmoid(zl), wg)
    kk = k * k_k
    nrm = jnp.sqrt(_group_sum(kk * kk, gsum))
    kk = kk / jnp.maximum(nrm, 1e-12)
    k2 = k * (1.0 + (a - 1.0) * k_a)
    return [r, lw, k2, v, -kk, kk * a, g]


def _seg_rwkv_post(tv, bv):
    (y, r, k2, v, g), (r_k, gain, bias, gsum) = tv, bv
    mu = _group_sum(y, gsum) * (1.0 / R_HEAD)
    yc = y - mu
    var = _group_sum(yc * yc, gsum) * (1.0 / R_HEAD)
    yn = yc * lax.rsqrt(var + GN_EPS) * gain + bias
    bonus = _group_sum(r * k2 * r_k, gsum) * v
    return [(yn + bonus) * g]


def _seg_merge(tv, bv):
    (ga, gb, pa, pb), () = tv, bv
    return [_sigmoid(ga) * pa + _sigmoid(gb) * pb]


def _seg_mid(tv, bv):
    (x, mix), (gt1, g1, b1, sc2, sh2) = tv, bv
    h1 = _layer_norm(ALPHA * x + gt1 * mix, g1, b1, LN_EPS)
    return [h1, h1 * (1.0 + sc2) + sh2]


def _seg_relu2(tv, bv):
    (f1,), () = tv, bv
    return [jnp.square(jnp.maximum(f1, 0.0))]


def _seg_loss(tv, bv):
    (h1, ff, target), (gt2, g2, b2) = tv, bv
    out = _layer_norm(ALPHA * h1 + gt2 * ff, g2, b2, LN_EPS)
    err = jnp.square(out - target)
    return 0.5 * jnp.sum(jnp.mean(err, axis=-1))


_BNN = (((2,), (1,)), ((0,), (0,)))
_BNT = (((2,), (2,)), ((0,), (0,)))
_BTN = (((1,), (1,)), ((0,), (0,)))


def _tri_dot(x, dims):
    L = x.shape[0]
    tri = (lax.broadcasted_iota(jnp.int32, (L, L), 1) <= lax.broadcasted_iota(jnp.int32, (L, L), 0)).astype(F32)
    hi, lo = _split2(x)
    return _dot(tri, hi, dims) + _dot(tri, lo, dims)


@jax.custom_vjp
def _running_sum(x):
    return _tri_dot(x, _NN)


_running_sum.defvjp(lambda x: (_tri_dot(x, _NN), None), lambda _, ct: (_tri_dot(ct, _TN),))


def _cut_heads(x):
    return jnp.stack([x[:, R_HEAD * h:R_HEAD * (h + 1)] for h in range(R_HEADS)])


def _join_heads(x):
    return jnp.concatenate([x[h] for h in range(R_HEADS)], axis=1)


@jax.custom_vjp
def _split_heads(x):
    return _cut_heads(x)


@jax.custom_vjp
def _merge_heads(x):
    return _join_heads(x)


_split_heads.defvjp(lambda x: (_cut_heads(x), None), lambda _, ct: (_join_heads(ct),))
_merge_heads.defvjp(lambda x: (_join_heads(x), None), lambda _, ct: (_cut_heads(ct),))


def _inverse_pullback(inv, ct):
    return _dot(_dot(inv, ct, _BTN), inv, _BNT)


@jax.custom_vjp
def _unit_lower_inverse(n_mat):
    H, L, _ = n_mat.shape
    eye = lax.broadcasted_iota(jnp.int32, (H, L, L), 1) == lax.broadcasted_iota(jnp.int32, (H, L, L), 2)
    inv = jnp.where(eye, 1.0, 0.0) + n_mat
    pw = n_mat
    n = 2
    while n < L:
        pw = _dot(pw, pw, _BNN)
        inv = inv + _dot(inv, pw, _BNN)
        n *= 2
    return inv


def _unit_lower_inverse_fwd(n_mat):
    inv = _unit_lower_inverse(n_mat)
    return inv, inv


_unit_lower_inverse.defvjp(_unit_lower_inverse_fwd, lambda inv, ct: (_inverse_pullback(inv, ct),))


@jax.custom_vjp
def _known_inverse(n_mat, inv):
    return inv


_known_inverse.defvjp(lambda n_mat, inv: (inv, inv), lambda inv, ct: (_inverse_pullback(inv, ct), jnp.zeros_like(inv)))


def _scan_chunk(r, lw, k, v, a, b, s0, inv=None, with_inverse=False):
    L, H = r.shape[0], R_HEADS
    cs = _running_sum(lw)
    cs_end = cs[L - 1:L, :]
    p, p_inv, to_end = jnp.exp(cs), jnp.exp(-cs), jnp.exp(cs_end - cs)
    at, bt, kt, rt = [_split_heads(t) for t in (a * jnp.exp(cs - lw), b * p_inv, k * p_inv, r * p)]
    b_end, k_end, v = [_split_heads(t) for t in (b * to_end, k * to_end, v)]
    row = lax.broadcasted_iota(jnp.int32, (H, L, L), 1)
    col = lax.broadcasted_iota(jnp.int32, (H, L, L), 2)
    incl, strict = col <= row, col < row
    a_ab = jnp.where(strict, _dot(at, bt, _BNT), 0.0)
    a_ak = jnp.where(strict, _dot(at, kt, _BNT), 0.0)
    a_rb = jnp.where(incl, _dot(rt, bt, _BNT), 0.0)
    a_rk = jnp.where(incl, _dot(rt, kt, _BNT), 0.0)
    inv = _unit_lower_inverse(a_ab) if inv is None else _known_inverse(a_ab, inv)
    u = _dot(inv, _dot(at, s0, _BNT) + _dot(a_ak, v, _BNN), _BNN)
    y = _merge_heads(_dot(rt, s0, _BNT) + _dot(a_rb, u, _BNN) + _dot(a_rk, v, _BNN))
    s1 = s0 * _split_heads(jnp.exp(cs_end)) + _dot(u, b_end, _BTN) + _dot(v, k_end, _BTN)
    return (y, s1, inv) if with_inverse else (y, s1)


def _scan_fwd(r, lw, k, v, a, b, rider):
    T = r.shape[0]
    H, N, L, P = R_HEADS, R_HEAD, SCAN_CHUNK, SCAN_PER_STEP
    nc = T // L
    steps = nc // P
    ride_shape, ride_sems, _ = rider[1](rider[0])

    def body(r_ref, lw_ref, k_ref, v_ref, a_ref, b_ref, ride_in, y_ref, st_ref, inv_ref, ride_out, s_ref, *sem_refs):
        _ride(rider, pl.program_id(0), steps, ride_in, ride_out, sem_refs)

        @pl.when(pl.program_id(0) == 0)
        def _():
            s_ref[...] = jnp.zeros_like(s_ref)

        s0 = s_ref[...]
        for j in range(P):
            rows = pl.ds(j * L, L)
            st_ref[j] = s0
            y, s0, inv = _scan_chunk(*[t[rows, :] for t in (r_ref, lw_ref, k_ref, v_ref, a_ref, b_ref)], s0, with_inverse=True)
            y_ref[rows, :] = y
            inv_ref[j] = inv
        s_ref[...] = s0

    blk = pl.BlockSpec((P * L, R_WIDTH), lambda c: (c, 0))
    per_chunk = pl.BlockSpec((P, H, N, N), lambda c: (c, 0, 0, 0))
    return pl.pallas_call(
        body, name="scan_fwd", grid=(steps,), in_specs=[blk] * 6 + [_HBM], out_specs=[blk, per_chunk, per_chunk, _HBM],
        out_shape=[jax.ShapeDtypeStruct((T, R_WIDTH), F32)] + [jax.ShapeDtypeStruct((nc, H, N, N), F32)] * 2 + [ride_shape],
        scratch_shapes=[pltpu.VMEM((H, N, N), F32)] + list(ride_sems),
        compiler_params=_cparams(("arbitrary",)),
    )(r, lw, k, v, a, b, rider[0])


def _scan_bwd(r, lw, k, v, a, b, states, inverses, dy, rider):
    T = r.shape[0]
    H, N, L, P = R_HEADS, R_HEAD, SCAN_CHUNK, SCAN_PER_STEP
    nc = T // L
    steps = nc // P
    ride_shape, ride_sems, _ = rider[1](rider[0])

    def body(r_ref, lw_ref, k_ref, v_ref, a_ref, b_ref, st_ref, inv_ref, dy_ref, ride_in,
             dr_ref, dlw_ref, dk_ref, dv_ref, da_ref, db_ref, ride_out, ds_ref, *sem_refs):
        _ride(rider, pl.program_id(0), steps, ride_in, ride_out, sem_refs)

        @pl.when(pl.program_id(0) == 0)
        def _():
            ds_ref[...] = jnp.zeros_like(ds_ref)

        ds = ds_ref[...]
        for j in reversed(range(P)):
            rows = pl.ds(j * L, L)
            args = [t[rows, :] for t in (r_ref, lw_ref, k_ref, v_ref, a_ref, b_ref)] + [st_ref[j]]
            inv = inv_ref[j]
            _, pull = jax.vjp(lambda *xs, inv=inv: _scan_chunk(*xs, inv=inv), *args)
            grads = pull((dy_ref[rows, :], ds))
            for o_ref, g_ in zip((dr_ref, dlw_ref, dk_ref, dv_ref, da_ref, db_ref), grads[:6], strict=True):
                o_ref[rows, :] = g_
            ds = grads[6]
        ds_ref[...] = ds

    blk = pl.BlockSpec((P * L, R_WIDTH), lambda c: (steps - 1 - c, 0))
    per_chunk = pl.BlockSpec((P, H, N, N), lambda c: (steps - 1 - c, 0, 0, 0))
    return pl.pallas_call(
        body, name="scan_bwd", grid=(steps,), in_specs=[blk] * 6 + [per_chunk, per_chunk, blk, _HBM], out_specs=[blk] * 6 + [_HBM],
        out_shape=[jax.ShapeDtypeStruct((T, R_WIDTH), F32)] * 6 + [ride_shape],
        scratch_shapes=[pltpu.VMEM((H, N, N), F32)] + list(ride_sems),
        compiler_params=_cparams(("arbitrary",)),
    )(r, lw, k, v, a, b, states, inverses, dy, rider[0])


def _place():
    x, y, c = lax.axis_index("x"), lax.axis_index("y"), lax.axis_index("c")
    return x, y, c


def _gather_def(block):
    R, C = block.shape

    def phases(x_ref, out_ref, send_sems, recv_sems, local_sem):
        x, y, c = _place()
        me, sibling = (x, y, c), (x, y, 1 - c)
        chips = [(1 - x, y), (x, 1 - y), (1 - x, 1 - y)]

        def slot(px, py, pc):
            return out_ref.at[4 * px + 2 * py + pc]

        def copy(k, blk, to, src=None):
            return pltpu.make_async_remote_copy(
                src_ref=slot(*blk) if src is None else src, dst_ref=slot(*blk),
                send_sem=send_sems.at[k], recv_sem=recv_sems.at[k], device_id=to, device_id_type=_MESH_ID)

        mine = pltpu.make_async_copy(x_ref, slot(*me), local_sem)
        first = [copy(0, me, sibling, src=x_ref)]
        first += [copy(1 + j, me, (*chip, c), src=x_ref) for j, chip in enumerate(chips)]
        passed = [copy(4 + j, (*chip, c), sibling) for j, chip in enumerate(chips)]

        def begin():
            mine.start()
            for cp in first:
                cp.start()

        def forward():
            for j, chip in enumerate(chips):
                copy(1 + j, (*chip, c), me).wait_recv()
                passed[j].start()

        def finish():
            copy(0, sibling, me).wait_recv()
            for j, chip in enumerate(chips):
                copy(4 + j, (*chip, 1 - c), me).wait_recv()
            for cp in first + passed:
                cp.wait_send()
            mine.wait()

        return [begin, forward, finish]

    sems = [pltpu.SemaphoreType.DMA((7,)), pltpu.SemaphoreType.DMA((7,)), pltpu.SemaphoreType.DMA]
    return jax.ShapeDtypeStruct((_N_DEV, R, C), block.dtype), sems, phases


def _sibling_def(blocks):
    _, R, C = blocks.shape

    def phases(x_ref, out_ref, send_sems, recv_sems):
        x, y, c = _place()
        copies = [pltpu.make_async_remote_copy(
            src_ref=x_ref.at[2 * q + (1 - c)], dst_ref=out_ref.at[q], send_sem=send_sems.at[q], recv_sem=recv_sems.at[q],
            device_id=(x, y, 1 - c), device_id_type=_MESH_ID) for q in range(4)]

        def begin():
            for cp in copies:
                cp.start()

        def finish():
            for cp in copies:
                cp.wait()

        return [begin, finish]

    return jax.ShapeDtypeStruct((4, R, C), blocks.dtype), [pltpu.SemaphoreType.DMA((4,)), pltpu.SemaphoreType.DMA((4,))], phases


def _chips_def(partials):
    def phases(x_ref, out_ref, send_sems, recv_sems, local_sem):
        x, y, c = _place()
        my_chip = 2 * x + y
        mine = pltpu.make_async_copy(x_ref.at[my_chip], out_ref.at[my_chip], local_sem)
        copies = []
        for rel in range(1, 4):
            px, py = (1 - x if (rel >> 1) & 1 else x), (1 - y if rel & 1 else y)
            copies.append(pltpu.make_async_remote_copy(
                src_ref=x_ref.at[2 * px + py], dst_ref=out_ref.at[my_chip],
                send_sem=send_sems.at[rel - 1], recv_sem=recv_sems.at[rel - 1],
                device_id=(px, py, c), device_id_type=_MESH_ID))

        def begin():
            mine.start()
            for cp in copies:
                cp.start()

        def finish():
            for cp in copies:
                cp.wait()
            mine.wait()

        return [begin, finish]

    sems = [pltpu.SemaphoreType.DMA((3,)), pltpu.SemaphoreType.DMA((3,)), pltpu.SemaphoreType.DMA]
    return jax.ShapeDtypeStruct(partials.shape, partials.dtype), sems, phases


_HBM = pl.BlockSpec(memory_space=pltpu.HBM)


def _exchange(name, array, definition):
    out_shape, sems, phases = definition(array)

    def body(x_ref, out_ref, *sem_refs):
        for phase in phases(x_ref, out_ref, *sem_refs):
            phase()

    return pl.pallas_call(body, name=name, in_specs=[_HBM], out_specs=_HBM, out_shape=out_shape, scratch_shapes=sems)(array)


def _ride(rider, step, nsteps, x_ref, out_ref, sem_refs):
    array, definition, fractions = rider
    for phase, frac in zip(definition(array)[2](x_ref, out_ref, *sem_refs), fractions, strict=True):
        pl.when(step == min(int(frac * nsteps), nsteps - 1))(phase)


def _all_gather(name, block):
    return _exchange(name, block, _gather_def)


def _chip_partials(name, blocks, from_sibling, tile, out_dtype):
    _, R, C = blocks.shape

    def body(x_ref, s_ref, o_ref):
        c = lax.axis_index("c")
        for q in range(4):
            o_ref[q] = (x_ref[2 * q + c].astype(F32) + s_ref[q].astype(F32)).astype(o_ref.dtype)

    return pl.pallas_call(
        body, name=name, grid=(R // tile,),
        in_specs=[pl.BlockSpec((_N_DEV, tile, C), lambda i: (0, i, 0)), pl.BlockSpec((4, tile, C), lambda i: (0, i, 0))],
        out_specs=pl.BlockSpec((4, tile, C), lambda i: (0, i, 0)), out_shape=jax.ShapeDtypeStruct((4, R, C), out_dtype),
        compiler_params=_cparams(("parallel",)),
    )(blocks, from_sibling)


def _sum_leading(name, x, tile=None):
    n, R, C = x.shape
    tile = tile or _pick(R, (512, 256, 128, 64, 32, 16, 8))

    def body(x_ref, o_ref):
        acc = x_ref[0].astype(F32)
        for k in range(1, n):
            acc = acc + x_ref[k].astype(F32)
        o_ref[...] = acc

    return pl.pallas_call(
        body, name=name, grid=(R // tile,), in_specs=[pl.BlockSpec((n, tile, C), lambda i: (0, i, 0))],
        out_specs=pl.BlockSpec((tile, C), lambda i: (i, 0)), out_shape=jax.ShapeDtypeStruct((R, C), F32),
        compiler_params=_cparams(("parallel",)),
    )(x)


def _adamw_update(w_, g_, m_, v_):
    m2 = ADAM_B1 * m_ + (1.0 - ADAM_B1) * g_
    v2 = ADAM_B2 * v_ + (1.0 - ADAM_B2) * jnp.square(g_)
    m_hat = m2 / (1.0 - ADAM_B1 ** ADAM_STEP)
    v_hat = v2 / (1.0 - ADAM_B2 ** ADAM_STEP)
    delta = -ADAM_LR * (m_hat / (jnp.sqrt(v_hat) + ADAM_EPS) + ADAM_WD * w_)
    return delta, m2, v2


def _adamw(name, w, g, m, v):
    R, C = w.shape
    tile = _pick(R, (256, 128, 64, 32, 16, 8))
    outs, _ = _rowwise(name, lambda tv, bv: (list(_adamw_update(*tv)), []), [w, g, m, v], [], [(C, F32)] * 3, [], tile)
    return outs


def _adamw_many(name, ws, gs, ms, vs):
    n = len(ws)

    def body(*refs):
        ins, outs = refs[:4 * n], refs[4 * n:]
        for i in range(n):
            res = _adamw_update(ins[i][...], ins[n + i][...], ins[2 * n + i][...], ins[3 * n + i][...])
            for j in range(3):
                outs[j * n + i][...] = res[j]

    out_shape = [jax.ShapeDtypeStruct(w.shape, F32) for w in ws] * 3
    res = pl.pallas_call(body, name=name, out_shape=out_shape, compiler_params=_cparams())(*ws, *gs, *ms, *vs)
    return res[:n], res[n:2 * n], res[2 * n:]


def _pack_rows(arrs, lanes=128, row_mult=8):
    flat, places, off = [], [], 0
    for a_ in arrs:
        n = a_.size
        flat.append(a_.reshape(-1).astype(F32))
        places.append((off, n, a_.shape))
        off += n
    total = -(-off // (lanes * row_mult)) * (lanes * row_mult)
    if total > off:
        flat.append(jnp.zeros((total - off,), F32))
    return jnp.concatenate(flat).reshape(total // lanes, lanes), places


def _unpack_rows(packed, places):
    flat = packed.reshape(-1)
    return [flat[o:o + n].reshape(s) for (o, n, s) in places]


_WEIGHTS = ['w_ada', 'b_ada', 'w_in', 'b_in', 'g_ln_v', 'b_ln_v', 'w_spatial', 'b_spatial', 'mu_shift', 'w0', 'w_decay_up', 'a0',
            'w_aaa_up', 'w_gate_up', 'k_k', 'k_a', 'r_k', 'gn_gain', 'gn_bias', 'w_branch_a', 'w_branch_b', 'w_out', 'b_out',
            'ln1_g', 'ln1_b', 'w_ff1', 'b_ff1', 'w_ff2', 'b_ff2', 'ln2_g', 'ln2_b']
_BIG = {'w_ff1': (0, 512), 'w_ff2': (512, 512), 'w_out': (1024, 128), 'w_branch_a': (1152, 64), 'w_branch_b': (1216, 64), 'w_in': (1280, 640)}
_LATER_ROWS = 1280
_CUT_BY_COLS = ('w_ff1', 'w_in', 'w_branch_a', 'w_branch_b')
IN_SHARD = IN_COLS // _N_DEV
_LORA = {'w_decay_up': (0, LORA_W), 'w_aaa_up': (LORA_W, LORA_A), 'w_gate_up': (LORA_W + LORA_A, LORA_G)}
_COMM_DTYPE = jnp.bfloat16


def _pad_rows(a, rows):
    return jnp.pad(a, ((0, rows - a.shape[0]),) + ((0, 0),) * (a.ndim - 1))


def _pack_big(shards):
    blocks = []
    for n, (_, rows) in _BIG.items():
        a = shards[n].T if n in _CUT_BY_COLS else shards[n]
        blocks.append(_pad_rows(a.reshape(-1, D_MODEL), rows))
    return jnp.concatenate(blocks, axis=0)


def _unpack_big(block, like):
    out = {}
    for n, (r0, _) in _BIG.items():
        rr, cc = like[n].shape
        if n in _CUT_BY_COLS:
            out[n] = block[r0:r0 + rr * cc // D_MODEL].reshape(cc, rr).T
        else:
            out[n] = block[r0:r0 + rr]
    return out


def _to_padded(a, axis):
    g_end = 2 * G_WIDTH
    r_end = g_end + RW_USED
    take = lambda lo, hi: lax.slice_in_dim(a, lo, hi, axis=axis)
    zshape = list(a.shape)
    zshape[axis] = RW_COLS - RW_USED
    return jnp.concatenate([take(r_end, IN_COLS), take(g_end, r_end), jnp.zeros(zshape, a.dtype), take(0, g_end)], axis=axis)


def _from_padded(a, axis):
    take = lambda lo, hi: lax.slice_in_dim(a, lo, hi, axis=axis)
    return jnp.concatenate([take(2 * D_MODEL + RW_COLS, P_COLS), take(2 * D_MODEL, 2 * D_MODEL + RW_USED), take(0, 2 * D_MODEL)], axis=axis)


def _step(p, m, v, x, c, target):
    T = x.shape[0]
    xi, yi, ci = _place()
    me = 4 * xi + 2 * yi + ci
    tile = _pick(T, (512, 256))

    lane = jnp.arange(R_WIDTH)
    gsum = (lane[:128, None] // R_HEAD == lane[None, :128] // R_HEAD).astype(F32)
    expand = (jnp.arange(128)[:, None] == (lane[None, :] // (G_WIDTH // 8))).astype(F32)

    (c_act,), _ = _rowwise("silu_c", lambda tv, bv: ([tv[0] * _sigmoid(tv[0])], []), [c], [], [(D_MODEL, F32)], [], 1)
    small, places = _pack_rows([c_act, p['w_decay_up'], p['w_aaa_up'], p['w_gate_up']])
    small_all = _all_gather("gather_small", small)
    per_dev = [_unpack_rows(small_all[d], places) for d in range(_N_DEV)]
    c_act_all = _pad_rows(jnp.concatenate([pd[0] for pd in per_dev], axis=0), 16)
    lora_full = {n: jnp.concatenate([pd[i + 1] for pd in per_dev], axis=1) for i, n in enumerate(_LORA)}
    lora_pad = {n: jnp.zeros((LORA_PAD, R_WIDTH), F32).at[r0:r0 + nr].set(lora_full[n]) for n, (r0, nr) in _LORA.items()}

    big_names = list(_BIG)
    my_rows = _pack_big(p).astype(_MXU_DTYPE)
    b_in_p = _to_padded(p['b_in'], 1)
    mu_p = jnp.concatenate([p['mu_shift'], jnp.zeros((1, RW_COLS - RW_USED), F32)], axis=1)

    b_ada_mine = lax.dynamic_slice(p['b_ada'], (0, me * 768), (1, 768))
    mod_cols = _mm("ada_mod", c_act_all, p['w_ada'], "nn", bias=b_ada_mine)
    mod_all = _all_gather("gather_mod", mod_cols)
    mod = lax.dynamic_index_in_dim(mod_all, me, axis=1, keepdims=False).reshape(1, 6 * D_MODEL)
    sh1, sc1, gt1, sh2, sc2, gt2 = [mod[:, i * D_MODEL:(i + 1) * D_MODEL] for i in range(6)]

    (h,), _, w_in_all = _rowwise("modulate1", lambda tv, bv: (_seg_modulate(tv, bv), []), [x], [sc1, sh1], [(D_MODEL, _MXU_DTYPE)], [], tile,
                                 rider=(my_rows[_LATER_ROWS:], _gather_def, (0.0, 0.5, 1.0)))
    w_in_t = _to_padded(w_in_all[:, :IN_SHARD].reshape(IN_COLS, D_MODEL), 0)
    proj = _mm("in_proj", h, w_in_t, "nt", bias=b_in_p)
    ws = p['w_spatial']
    b_tg = jnp.zeros((G_CHUNK, 128), F32).at[:, :8].set(p['b_spatial'].T)
    gmlp_b = [p['g_ln_v'], p['b_ln_v'], ws, b_tg, expand]
    z_gmlp = (proj, 2 * G_WIDTH, 4)
    (ya,), _ = _rowwise("gmlp", lambda tv, bv: (_seg_gmlp(tv, bv), []), [z_gmlp], gmlp_b, [(G_WIDTH, _MXU_DTYPE)], [], G_CHUNK)
    z_rw = (proj, RW_COLS, 1)
    z_rw_halo = ("halo", proj, RW_COLS, 1)
    pre_b = [mu_p, p['w0'], lora_pad['w_decay_up'], p['a0'], lora_pad['w_aaa_up'], lora_pad['w_gate_up'], p['k_k'], p['k_a'], gsum]

    def pre_fwd(tv, bv):
        z_t, halo_t = tv
        return _seg_rwkv_pre([z_t, _shift_down(z_t, halo_t, pl.program_id(0) == 0)], bv), []

    pre_out, _ = _rowwise("rwkv_pre", pre_fwd, [z_rw, z_rw_halo], pre_b, [(R_WIDTH, F32)] * 7, [], tile)
    r_, lw_, k2_, v_, a_, b_, g_ = pre_out
    scan_in = (r_, lw_, k2_, v_, a_, b_)
    y_, states, inverses, later_all = _scan_fwd(*scan_in, rider=(my_rows[:_LATER_ROWS], _gather_def, (0.0, 0.875, 1.0)))

    def whole(n, rows):
        r0 = _BIG[n][0]
        return later_all[:, r0:r0 + rows].reshape(_N_DEV * rows, D_MODEL)

    w_ff1_t, w_ff2, w_out = whole('w_ff1', 512), whole('w_ff2', 512), whole('w_out', 128)
    w_ba_t = whole('w_branch_a', 64).reshape(D_MODEL, G_WIDTH)
    w_bb_t = whole('w_branch_b', 64).reshape(D_MODEL, R_WIDTH)
    post_b = [p['r_k'].reshape(1, R_WIDTH), p['gn_gain'], p['gn_bias'], gsum]
    (yb,), _ = _rowwise("rwkv_post", lambda tv, bv: (_seg_rwkv_post(tv, bv), []), [y_, r_, k2_, v_, g_], post_b, [(R_WIDTH, _MXU_DTYPE)], [], tile)
    pa = _mm("branch_a", ya, w_ba_t, "nt", out_dtype=_MXU_DTYPE)
    pb = _mm("branch_b", yb, w_bb_t, "nt", out_dtype=_MXU_DTYPE)
    gates = [(proj, D_MODEL, 0), (proj, D_MODEL, 1)]
    (merged,), _ = _rowwise("merge", lambda tv, bv: (_seg_merge(tv, bv), []), gates + [pa, pb], [], [(D_MODEL, _MXU_DTYPE)], [], tile)
    mix = _mm("out_proj", merged, w_out, "nn", bias=p['b_out'])
    mid_b = [gt1, p['ln1_g'], p['ln1_b'], sc2, sh2]
    (h1, h2in), _ = _rowwise("mid", lambda tv, bv: (_seg_mid(tv, bv), []), [x, mix], mid_b, [(D_MODEL, F32), (D_MODEL, _MXU_DTYPE)], [], tile)
    act = _mm("ff1", h2in, w_ff1_t, "nt", bias=p['b_ff1'], epi=lambda t: _seg_relu2([t], [])[0], epi_dtype=_MXU_DTYPE, raw=False)
    ff = _mm("ff2", act, w_ff2, "nn", bias=p['b_ff2'])

    def loss_fn(tv, bv):
        h1_t, ff_t, tgt = tv
        val, grads = jax.value_and_grad(lambda a0_, a1_, b0_, b1_, b2_: _seg_loss([a0_, a1_, tgt], [b0_, b1_, b2_]), argnums=(0, 1, 2, 3, 4))(h1_t, ff_t, *bv)
        return [grads[0], grads[1]], [grads[2], grads[3], grads[4], jnp.sum(grads[1], axis=0, keepdims=True), jnp.full((1, 128), val, F32)]

    (dh1_a, dff), (d_gt2, d_ln2_g, d_ln2_b, d_b_ff2, loss_row) = _rowwise(
        "loss", loss_fn, [h1, ff, target], [gt2, p['ln2_g'], p['ln2_b']], [(D_MODEL, F32), (D_MODEL, _MXU_DTYPE)], [(1, D_MODEL)] * 4 + [(1, 128)], tile)

    g = {}
    g['ln2_g'], g['ln2_b'], g['b_ff2'] = d_ln2_g, d_ln2_b, d_b_ff2
    gw = {}
    gw['w_ff2'] = _mm("g_w_ff2", act, dff, "tn", out_dtype=_COMM_DTYPE)
    df1, g['b_ff1'] = _mm("d_act", dff, w_ff2, "nt", beside=act, epi=lambda d_, a_: d_ * (2.0 * jnp.sqrt(a_.astype(F32))),
                          epi_dtype=_MXU_DTYPE, colsum=True)
    gw['w_ff1'] = _mm("g_w_ff1", df1, h2in, "tn", out_dtype=_COMM_DTYPE)
    dh2in = _mm("d_h2in", df1, w_ff1_t, "nn")
    (dx_a, dmix), (d_gt1, g['ln1_g'], g['ln1_b'], d_sc2, d_sh2), (g['b_out'],) = _rowwise_vjp(
        "mid_bwd", _seg_mid, [x, mix], mid_b, [dh1_a, dh2in], tile, [0, 1], [0, 1, 2, 3, 4], t_dtypes=[F32, _MXU_DTYPE], colsum=[1])
    gw['w_out'] = _mm("g_w_out", merged, dmix, "tn", out_dtype=_COMM_DTYPE)
    dmerged = _mm("d_merged", dmix, w_out, "nt", out_dtype=_MXU_DTYPE)
    (dproj, dpa, dpb), _, (cs_gates,) = _rowwise_vjp(
        "merge_bwd", _seg_merge, gates + [pa, pb], [], [dmerged], tile, [0, 1, 2, 3], [], t_dtypes=[_MXU_DTYPE] * 3, colsum=[0],
        finish=lambda dts, sc: [jnp.concatenate(dts[:2], axis=1), dts[2], dts[3]], out_widths=[2 * D_MODEL, D_MODEL, D_MODEL],
        into={0: (lax.empty((T, P_COLS), _MXU_DTYPE), 0)})
    gw['w_branch_a'] = _mm("g_w_branch_a", dpa, ya, "tn", out_dtype=_COMM_DTYPE)
    gw['w_branch_b'] = _mm("g_w_branch_b", dpb, yb, "tn", out_dtype=_COMM_DTYPE)
    dya = _mm("d_ya", dpa, w_ba_t, "nn")
    dyb = _mm("d_yb", dpb, w_bb_t, "nn")
    def send_rows(names):
        parts = []
        for n in names:
            per_dev = gw[n].reshape(_N_DEV, -1, D_MODEL)
            parts.append(jnp.pad(per_dev, ((0, 0), (0, _BIG[n][1] - per_dev.shape[1]), (0, 0))))
        return jnp.concatenate(parts, axis=1) if len(parts) > 1 else parts[0]

    send_early = send_rows(big_names[:-1])
    (dy, dr1, dk1, dv1, dg_), (d_r_k, g['gn_gain'], g['gn_bias']), _, sibling_early = _rowwise_vjp(
        "rwkv_post_bwd", _seg_rwkv_post, [y_, r_, k2_, v_, g_], post_b, [dyb], tile, [0, 1, 2, 3, 4], [0, 1, 2],
        rider=(send_early, _sibling_def, (0.0, 1.0)))
    g['r_k'] = d_r_k
    partials_early = _chip_partials("chip_partials_early", send_early, sibling_early, 128, _COMM_DTYPE)
    dr2, dlw, dk2, dv2, da, db, landed_early = _scan_bwd(*scan_in, states, inverses, dy, rider=(partials_early, _chips_def, (0.0, 1.0)))
    pre_tile = 256
    last_step = T // pre_tile - 1

    def pre_prep(prim):
        z_t, halo_t = prim
        return [z_t, _shift_down(z_t, halo_t, pl.program_id(0) == last_step)]

    def pre_finish(dts, sc):
        dz_direct, dprev = dts
        (row_after,) = sc

        @pl.when(pl.program_id(0) == 0)
        def _():
            row_after[...] = jnp.zeros_like(row_after)

        dz = dz_direct + _shift_up(dprev, row_after[...])
        row_after[...] = dprev[:_HALO]
        return [dz]

    small_names = [n for n in _WEIGHTS if n not in _BIG and n not in ('w_ada', 'b_ada')]
    (dproj,), (g['g_ln_v'], g['b_ln_v'], g['w_spatial'], d_b_tg), (cs_g,) = _rowwise_vjp(
        "gmlp_bwd", _seg_gmlp, [z_gmlp], gmlp_b, [dya], G_CHUNK, [0], [0, 1, 2, 3], t_dtypes=[_MXU_DTYPE], colsum=[0],
        into={0: (dproj, 4)})
    g['b_spatial'] = d_b_tg[:, :8].T
    names_1 = [n for n in small_names if n in g]
    packed_1, places_1 = _pack_rows([g[n] for n in names_1] + [loss_row], row_mult=256)
    (dproj,), (d_mu, g['w0'], d_wd, g['a0'], d_wa, d_wg, g['k_k'], g['k_a']), (cs_rw,), small_all_1 = _rowwise_vjp(
        "rwkv_pre_bwd", _seg_rwkv_pre, [z_rw, z_rw_halo], pre_b, [[dr1, dr2], dlw, [dk1, dk2], [dv1, dv2], da, db, dg_], pre_tile,
        [0, 1], [0, 1, 2, 3, 4, 5, 6, 7], t_dtypes=[_MXU_DTYPE], colsum=[0], prep=pre_prep, finish=pre_finish,
        out_widths=[RW_COLS], reverse=True, scratch=[(_HALO, RW_COLS)], into={0: (dproj, 1)},
        rider=(packed_1, _gather_def, (0.0, 0.6, 1.0)))
    g['mu_shift'] = d_mu[:, :RW_USED]
    for n, d_ in (('w_decay_up', d_wd), ('w_aaa_up', d_wa), ('w_gate_up', d_wg)):
        r0, nr = _LORA[n]
        g[n] = d_[r0:r0 + nr]
    g['b_in'] = _from_padded(jnp.concatenate([cs_gates, cs_rw, cs_g], axis=1), 1)
    names_2 = [n for n in small_names if n not in names_1]
    packed_2, places_2 = _pack_rows([g[n] for n in names_2], row_mult=256)
    gw_in_t, small_all_2 = _mm("g_w_in", dproj, h, "tn", out_dtype=_COMM_DTYPE, rider=(packed_2, _gather_def, (0.0, 0.6, 1.0)))
    gw['w_in'] = _from_padded(gw_in_t, 0)
    send_late = send_rows(big_names[-1:])
    sibling_late = _exchange("pair_exchange_late", send_late, _sibling_def)
    partials_late = _chip_partials("chip_partials_late", send_late, sibling_late, 128, _COMM_DTYPE)
    dh, landed_late = _mm("d_h", dproj, w_in_t, "nn", rider=(partials_late, _chips_def, (0.0, 1.0)))

    def mod1_bwd(tv, bv):
        x_t, dh_t, dxa_t = tv
        (sc,) = bv
        return [dxa_t + dh_t * (1.0 + sc)], [jnp.sum(dh_t * x_t, axis=0, keepdims=True), jnp.sum(dh_t, axis=0, keepdims=True)]

    (grad_x,), (d_sc1, d_sh1) = _rowwise("modulate1_bwd", mod1_bwd, [x, dh, dx_a], [sc1], [(D_MODEL, F32)], [(1, D_MODEL)] * 2, tile)

    dmod = jnp.concatenate([d_sh1, d_sc1, d_gt1, d_sh2, d_sc2, d_gt2], axis=1).reshape(6 * D_MODEL // 128, 128)
    dmod_all = _all_gather("gather_dmod", dmod)
    g['b_ada'] = _sum_leading("sum_dmod", dmod_all).reshape(1, 6 * D_MODEL)
    dmod_mine = lax.dynamic_slice(dmod_all.reshape(_N_DEV, 6 * D_MODEL), (0, me * 768), (_N_DEV, 768))
    g_w_ada = _mm("g_w_ada", c_act_all, _pad_rows(dmod_mine, 16), "tn")

    sums_1 = _unpack_rows(_sum_leading("sum_small_1", small_all_1), places_1)
    loss = sums_1.pop()[0, 0]
    sums_2 = _unpack_rows(_sum_leading("sum_small_2", small_all_2), places_2)
    g.update(zip(names_1 + names_2, sums_1 + sums_2, strict=True))
    for n in _LORA:
        g[n] = lax.dynamic_slice(g[n], (0, me * R_HEAD), (g[n].shape[0], R_HEAD))
    g['w_ada'] = g_w_ada

    summed = jnp.concatenate([_sum_leading("sum_early", landed_early), _sum_leading("sum_late", landed_late)], axis=0)
    g.update(_unpack_big(summed, p))

    delta, new_m, new_v = {}, {}, {}
    own_call = ['w_ada'] + big_names
    for n in own_call:
        if n == 'w_in':
            r0 = _BIG[n][0]
            outs_t = _adamw("adamw_" + n, p[n].T, summed[r0:r0 + IN_SHARD], m[n].T, v[n].T)
            delta[n], new_m[n], new_v[n] = [o.T for o in outs_t]
        else:
            delta[n], new_m[n], new_v[n] = _adamw("adamw_" + n, p[n], g[n], m[n], v[n])
    rest = [n for n in _WEIGHTS if n not in own_call]
    outs = _adamw_many("adamw_rest", *[[d[n].reshape(p[n].shape) for n in rest] for d in (p, g, m, v)])
    for d, o in zip((delta, new_m, new_v), outs, strict=True):
        d.update(zip(rest, o, strict=True))
    return loss, grad_x, g, delta, new_m, new_v


def kernel(x, c, w_ada, b_ada, w_in, b_in, g_ln_v, b_ln_v, w_spatial, b_spatial, mu_shift, w0, w_decay_up, a0, w_aaa_up, w_gate_up, k_k, k_a, r_k, gn_gain, gn_bias, w_branch_a, w_branch_b, w_out, b_out, ln1_g, ln1_b, w_ff1, b_ff1, w_ff2, b_ff2, ln2_g, ln2_b, loss_target, m_w_ada, m_b_ada, m_w_in, m_b_in, m_g_ln_v, m_b_ln_v, m_w_spatial, m_b_spatial, m_mu_shift, m_w0, m_w_decay_up, m_a0, m_w_aaa_up, m_w_gate_up, m_k_k, m_k_a, m_r_k, m_gn_gain, m_gn_bias, m_w_branch_a, m_w_branch_b, m_w_out, m_b_out, m_ln1_g, m_ln1_b, m_w_ff1, m_b_ff1, m_w_ff2, m_b_ff2, m_ln2_g, m_ln2_b, v_w_ada, v_b_ada, v_w_in, v_b_in, v_g_ln_v, v_b_ln_v, v_w_spatial, v_b_spatial, v_mu_shift, v_w0, v_w_decay_up, v_a0, v_w_aaa_up, v_w_gate_up, v_k_k, v_k_a, v_r_k, v_gn_gain, v_gn_bias, v_w_branch_a, v_w_branch_b, v_w_out, v_b_out, v_ln1_g, v_ln1_b, v_w_ff1, v_b_ff1, v_w_ff2, v_b_ff2, v_ln2_g, v_ln2_b):
    given = dict(locals())
    shapes = {n: given[n].shape for n in _WEIGHTS}
    def two_d(a_):
        a_ = a_[0]
        return a_.reshape(1, -1) if a_.ndim == 1 else a_
    p = {n: two_d(given[n]) for n in _WEIGHTS}
    m = {n: two_d(given["m_" + n]) for n in _WEIGHTS}
    v = {n: two_d(given["v_" + n]) for n in _WEIGHTS}
    loss, grad_x, g, delta, new_m, new_v = _step(p, m, v, x[0], c, loss_target[0])
    outs = [loss, grad_x[None]]
    for d in (g, delta, new_m, new_v):
        outs += [d[n].reshape(shapes[n]) for n in _WEIGHTS]
    return tuple(outs)
```
